```python
import jax, jax.numpy as jnp
from jax import lax
import numpy as np

D_MODEL = 1024
BATCH = 8
SEQ = 2048
DEPTH = 2

N_MIXERS = 2
RMS_EPS = 1e-6
LN_EPS = 1e-5
CHUNK = 128
A_WIDTH = 2 * D_MODEL
A_GROUPS = 8
A_GROUP_DIM = A_WIDTH // A_GROUPS
B_WIDTH = 3 * D_MODEL // 2
B_HEADS = 12
B_HEAD_DIM = B_WIDTH // B_HEADS
CONV_WIDTH = 4
RG_C = 8.0

N_A_LAYERS = (DEPTH + 1) // 2
N_B_LAYERS = DEPTH // 2

kernel_name = "hybrid_sgu_rglru_trunk"


def rms_norm(x, w):
    xf = x.astype(jnp.float32)
    y = xf * lax.rsqrt(jnp.mean(xf * xf, axis=-1, keepdims=True) + RMS_EPS)
    return (y * w.astype(jnp.float32)).astype(x.dtype)


def layer_norm(x, w, b):
    xf = x.astype(jnp.float32)
    mu = jnp.mean(xf, axis=-1, keepdims=True)
    var = jnp.mean(jnp.square(xf - mu), axis=-1, keepdims=True)
    y = (xf - mu) * lax.rsqrt(var + LN_EPS)
    return (y * w.astype(jnp.float32) + b.astype(jnp.float32)).astype(x.dtype)


def chunked_sgu_mixer(h, w_in, ln_w, ln_b, w_s, b_s, w_out):
    B, S, _ = h.shape
    z = h @ w_in
    u, v, g = jnp.split(z, 3, axis=-1)
    u = jax.nn.gelu(u)
    v = layer_norm(jax.nn.gelu(v), ln_w, ln_b)
    v = v.reshape(B, S // CHUNK, CHUNK, A_GROUPS, A_GROUP_DIM)
    causal = jnp.tril(jnp.ones((CHUNK, CHUNK), dtype=w_s.dtype))
    w_causal = w_s * causal[None]
    s = jnp.einsum('gts,bnsgc->bntgc', w_causal, v) + b_s.T[None, None, :, :, None]
    y = u * s.reshape(B, S, A_WIDTH) * jax.nn.silu(g)
    return y @ w_out


def _linear_combine(left, right):
    a_l, b_l = left
    a_r, b_r = right
    return a_l * a_r, a_r * b_l + b_r


def rglru_mixer(h, w_in, conv_w, conv_b, gate_a_w, gate_a_b, gate_x_w, gate_x_b, lam, w_out):
    B, S, _ = h.shape
    xb, g = jnp.split(h @ w_in, 2, axis=-1)
    xp = jnp.pad(xb, ((0, 0), (CONV_WIDTH - 1, 0), (0, 0)))
    xc = conv_b + conv_w[CONV_WIDTH - 1] * xp[:, CONV_WIDTH - 1:CONV_WIDTH - 1 + S]
    for k in range(CONV_WIDTH - 1):
        xc = xc + conv_w[k] * xp[:, k:k + S]
    xh = xc.reshape(B, S, B_HEADS, B_HEAD_DIM)
    r = jax.nn.sigmoid(jnp.einsum('bshi,hij->bshj', xh, gate_a_w).reshape(B, S, B_WIDTH) + gate_a_b)
    i = jax.nn.sigmoid(jnp.einsum('bshi,hij->bshj', xh, gate_x_w).reshape(B, S, B_WIDTH) + gate_x_b)
    log_a = -RG_C * r.astype(jnp.float32) * jax.nn.softplus(-lam.astype(jnp.float32))
    a = jnp.exp(log_a)
    mult = jnp.sqrt(-jnp.expm1(2.0 * log_a))
    bterm = mult * (i * xc).astype(jnp.float32)
    _, hseq = lax.associative_scan(_linear_combine, (a, bterm), axis=1)
    y = hseq.astype(h.dtype) * jax.nn.silu(g)
    return y @ w_out


def _fwd_setup_inputs(seed: int = 0) -> dict:
    key = jax.random.key(seed)
    ks = jax.random.split(key, 20)
    f32 = jnp.float32
    x = jax.random.normal(ks[0], (BATCH, SEQ, D_MODEL), f32)
    norm_w = 1.0 + 0.05 * jax.random.normal(ks[1], (DEPTH, D_MODEL), f32)
    a_w_in = jax.random.normal(ks[2], (N_A_LAYERS, D_MODEL, 3 * A_WIDTH), f32) * D_MODEL ** -0.5
    a_ln_w = 1.0 + 0.05 * jax.random.normal(ks[3], (N_A_LAYERS, A_WIDTH), f32)
    a_ln_b = 0.02 * jax.random.normal(ks[4], (N_A_LAYERS, A_WIDTH), f32)
    a_w_s = jax.random.normal(ks[5], (N_A_LAYERS, A_GROUPS, CHUNK, CHUNK), f32) * CHUNK ** -0.5
    a_b_s = 1.0 + 0.05 * jax.random.normal(ks[6], (N_A_LAYERS, A_GROUPS, CHUNK), f32)
    a_w_out = jax.random.normal(ks[7], (N_A_LAYERS, A_WIDTH, D_MODEL), f32) * A_WIDTH ** -0.5
    b_w_in = jax.random.normal(ks[8], (N_B_LAYERS, D_MODEL, 2 * B_WIDTH), f32) * D_MODEL ** -0.5
    b_conv_w = jax.random.normal(ks[9], (N_B_LAYERS, CONV_WIDTH, B_WIDTH), f32) * CONV_WIDTH ** -0.5
    b_conv_b = 0.02 * jax.random.normal(ks[10], (N_B_LAYERS, B_WIDTH), f32)
    b_gate_a_w = jax.random.normal(ks[11], (N_B_LAYERS, B_HEADS, B_HEAD_DIM, B_HEAD_DIM), f32) * B_HEAD_DIM ** -0.5
    b_gate_a_b = 0.02 * jax.random.normal(ks[12], (N_B_LAYERS, B_WIDTH), f32)
    b_gate_x_w = jax.random.normal(ks[13], (N_B_LAYERS, B_HEADS, B_HEAD_DIM, B_HEAD_DIM), f32) * B_HEAD_DIM ** -0.5
    b_gate_x_b = 0.02 * jax.random.normal(ks[14], (N_B_LAYERS, B_WIDTH), f32)
    a_c = jax.random.uniform(ks[15], (N_B_LAYERS, B_WIDTH), f32, minval=0.9, maxval=0.999)
    a0 = a_c ** (1.0 / RG_C)
    b_lambda = jnp.log(a0) - jnp.log1p(-a0)
    b_w_out = jax.random.normal(ks[16], (N_B_LAYERS, B_WIDTH, D_MODEL), f32) * B_WIDTH ** -0.5
    norm_f_w = 1.0 + 0.05 * jax.random.normal(ks[17], (D_MODEL,), f32)
    return {
        "x": x, "norm_w": norm_w,
        "a_w_in": a_w_in, "a_ln_w": a_ln_w, "a_ln_b": a_ln_b,
        "a_w_s": a_w_s, "a_b_s": a_b_s, "a_w_out": a_w_out,
        "b_w_in": b_w_in, "b_conv_w": b_conv_w, "b_conv_b": b_conv_b,
        "b_gate_a_w": b_gate_a_w, "b_gate_a_b": b_gate_a_b,
        "b_gate_x_w": b_gate_x_w, "b_gate_x_b": b_gate_x_b,
        "b_lambda": b_lambda, "b_w_out": b_w_out,
        "norm_f_w": norm_f_w,
    }


def _fwd_reference(x, norm_w, a_w_in, a_ln_w, a_ln_b, a_w_s, a_b_s, a_w_out,
              b_w_in, b_conv_w, b_conv_b, b_gate_a_w, b_gate_a_b,
              b_gate_x_w, b_gate_x_b, b_lambda, b_w_out, norm_f_w):
    for layer in range(DEPTH):
        h = rms_norm(x, norm_w[layer])
        j = layer // N_MIXERS
        if layer % N_MIXERS == 0:
            y = chunked_sgu_mixer(h, a_w_in[j], a_ln_w[j], a_ln_b[j], a_w_s[j], a_b_s[j], a_w_out[j])
        else:
            y = rglru_mixer(h, b_w_in[j], b_conv_w[j], b_conv_b[j], b_gate_a_w[j], b_gate_a_b[j],
                            b_gate_x_w[j], b_gate_x_b[j], b_lambda[j], b_w_out[j])
        x = x + y
    return rms_norm(x, norm_f_w)


import jax as _jax
import jax.numpy as _jnp

TWIN_FORMAT = 'train_step'
FWD_PARAMS = ['x', 'norm_w', 'a_w_in', 'a_ln_w', 'a_ln_b', 'a_w_s', 'a_b_s', 'a_w_out', 'b_w_in', 'b_conv_w', 'b_conv_b', 'b_gate_a_w', 'b_gate_a_b', 'b_gate_x_w', 'b_gate_x_b', 'b_lambda', 'b_w_out', 'norm_f_w']
TWIN_WEIGHTS = ['norm_w', 'a_w_in', 'a_ln_w', 'a_ln_b', 'a_w_s', 'a_b_s', 'a_w_out', 'b_w_in', 'b_conv_w', 'b_conv_b', 'b_gate_a_w', 'b_gate_a_b', 'b_gate_x_w', 'b_gate_x_b', 'b_lambda', 'b_w_out', 'norm_f_w']
TWIN_DIFF_INPUT = 'x'
TWIN_INPUTS = ['x', 'norm_w', 'a_w_in', 'a_ln_w', 'a_ln_b', 'a_w_s', 'a_b_s', 'a_w_out', 'b_w_in', 'b_conv_w', 'b_conv_b', 'b_gate_a_w', 'b_gate_a_b', 'b_gate_x_w', 'b_gate_x_b', 'b_lambda', 'b_w_out', 'norm_f_w', 'loss_target', 'm_norm_w', 'm_a_w_in', 'm_a_ln_w', 'm_a_ln_b', 'm_a_w_s', 'm_a_b_s', 'm_a_w_out', 'm_b_w_in', 'm_b_conv_w', 'm_b_conv_b', 'm_b_gate_a_w', 'm_b_gate_a_b', 'm_b_gate_x_w', 'm_b_gate_x_b', 'm_b_lambda', 'm_b_w_out', 'm_norm_f_w', 'v_norm_w', 'v_a_w_in', 'v_a_ln_w', 'v_a_ln_b', 'v_a_w_s', 'v_a_b_s', 'v_a_w_out', 'v_b_w_in', 'v_b_conv_w', 'v_b_conv_b', 'v_b_gate_a_w', 'v_b_gate_a_b', 'v_b_gate_x_w', 'v_b_gate_x_b', 'v_b_lambda', 'v_b_w_out', 'v_norm_f_w']
TWIN_OUTPUTS = ['loss', 'grad_x', 'grad_norm_w', 'grad_a_w_in', 'grad_a_ln_w', 'grad_a_ln_b', 'grad_a_w_s', 'grad_a_b_s', 'grad_a_w_out', 'grad_b_w_in', 'grad_b_conv_w', 'grad_b_conv_b', 'grad_b_gate_a_w', 'grad_b_gate_a_b', 'grad_b_gate_x_w', 'grad_b_gate_x_b', 'grad_b_lambda', 'grad_b_w_out', 'grad_norm_f_w', 'delta_norm_w', 'delta_a_w_in', 'delta_a_ln_w', 'delta_a_ln_b', 'delta_a_w_s', 'delta_a_b_s', 'delta_a_w_out', 'delta_b_w_in', 'delta_b_conv_w', 'delta_b_conv_b', 'delta_b_gate_a_w', 'delta_b_gate_a_b', 'delta_b_gate_x_w', 'delta_b_gate_x_b', 'delta_b_lambda', 'delta_b_w_out', 'delta_norm_f_w', 'new_m_norm_w', 'new_m_a_w_in', 'new_m_a_ln_w', 'new_m_a_ln_b', 'new_m_a_w_s', 'new_m_a_b_s', 'new_m_a_w_out', 'new_m_b_w_in', 'new_m_b_conv_w', 'new_m_b_conv_b', 'new_m_b_gate_a_w', 'new_m_b_gate_a_b', 'new_m_b_gate_x_w', 'new_m_b_gate_x_b', 'new_m_b_lambda', 'new_m_b_w_out', 'new_m_norm_f_w', 'new_v_norm_w', 'new_v_a_w_in', 'new_v_a_ln_w', 'new_v_a_ln_b', 'new_v_a_w_s', 'new_v_a_b_s', 'new_v_a_w_out', 'new_v_b_w_in', 'new_v_b_conv_w', 'new_v_b_conv_b', 'new_v_b_gate_a_w', 'new_v_b_gate_a_b', 'new_v_b_gate_x_w', 'new_v_b_gate_x_b', 'new_v_b_lambda', 'new_v_b_w_out', 'new_v_norm_f_w']
TWIN_LEAF_KINDS = {'loss': 'loss', 'grad_x': 'grad_x', 'grad_norm_w': 'grad_w', 'grad_a_w_in': 'grad_w', 'grad_a_ln_w': 'grad_w', 'grad_a_ln_b': 'grad_w', 'grad_a_w_s': 'grad_w', 'grad_a_b_s': 'grad_w', 'grad_a_w_out': 'grad_w', 'grad_b_w_in': 'grad_w', 'grad_b_conv_w': 'grad_w', 'grad_b_conv_b': 'grad_w', 'grad_b_gate_a_w': 'grad_w', 'grad_b_gate_a_b': 'grad_w', 'grad_b_gate_x_w': 'grad_w', 'grad_b_gate_x_b': 'grad_w', 'grad_b_lambda': 'grad_w', 'grad_b_w_out': 'grad_w', 'grad_norm_f_w': 'grad_w', 'delta_norm_w': 'delta_w', 'delta_a_w_in': 'delta_w', 'delta_a_ln_w': 'delta_w', 'delta_a_ln_b': 'delta_w', 'delta_a_w_s': 'delta_w', 'delta_a_b_s': 'delta_w', 'delta_a_w_out': 'delta_w', 'delta_b_w_in': 'delta_w', 'delta_b_conv_w': 'delta_w', 'delta_b_conv_b': 'delta_w', 'delta_b_gate_a_w': 'delta_w', 'delta_b_gate_a_b': 'delta_w', 'delta_b_gate_x_w': 'delta_w', 'delta_b_gate_x_b': 'delta_w', 'delta_b_lambda': 'delta_w', 'delta_b_w_out': 'delta_w', 'delta_norm_f_w': 'delta_w', 'new_m_norm_w': 'new_m', 'new_m_a_w_in': 'new_m', 'new_m_a_ln_w': 'new_m', 'new_m_a_ln_b': 'new_m', 'new_m_a_w_s': 'new_m', 'new_m_a_b_s': 'new_m', 'new_m_a_w_out': 'new_m', 'new_m_b_w_in': 'new_m', 'new_m_b_conv_w': 'new_m', 'new_m_b_conv_b': 'new_m', 'new_m_b_gate_a_w': 'new_m', 'new_m_b_gate_a_b': 'new_m', 'new_m_b_gate_x_w': 'new_m', 'new_m_b_gate_x_b': 'new_m', 'new_m_b_lambda': 'new_m', 'new_m_b_w_out': 'new_m', 'new_m_norm_f_w': 'new_m', 'new_v_norm_w': 'new_v', 'new_v_a_w_in': 'new_v', 'new_v_a_ln_w': 'new_v', 'new_v_a_ln_b': 'new_v', 'new_v_a_w_s': 'new_v', 'new_v_a_b_s': 'new_v', 'new_v_a_w_out': 'new_v', 'new_v_b_w_in': 'new_v', 'new_v_b_conv_w': 'new_v', 'new_v_b_conv_b': 'new_v', 'new_v_b_gate_a_w': 'new_v', 'new_v_b_gate_a_b': 'new_v', 'new_v_b_gate_x_w': 'new_v', 'new_v_b_gate_x_b': 'new_v', 'new_v_b_lambda': 'new_v', 'new_v_b_w_out': 'new_v', 'new_v_norm_f_w': 'new_v'}


def _forward(args):
    return _fwd_reference(*[args[k] for k in FWD_PARAMS])


def _output_shape():
    out = _jax.eval_shape(lambda: _forward(_fwd_setup_inputs(0)))
    return out.shape, out.dtype

N_MICROBATCH = 1
ADAM_LR = 0.001
ADAM_B1 = 0.9
ADAM_B2 = 0.999
ADAM_EPS = 1e-08
ADAM_WD = 0.01
ADAM_STEP = 10
PER_EXAMPLE_BATCH_AXIS = {'x': 0, 'loss_target': 0}
SHARED_INPUTS = []
_WEIGHT_DTYPES = {'norm_w': _jnp.float32, 'a_w_in': _jnp.float32, 'a_ln_w': _jnp.float32, 'a_ln_b': _jnp.float32, 'a_w_s': _jnp.float32, 'a_b_s': _jnp.float32, 'a_w_out': _jnp.float32, 'b_w_in': _jnp.float32, 'b_conv_w': _jnp.float32, 'b_conv_b': _jnp.float32, 'b_gate_a_w': _jnp.float32, 'b_gate_a_b': _jnp.float32, 'b_gate_x_w': _jnp.float32, 'b_gate_x_b': _jnp.float32, 'b_lambda': _jnp.float32, 'b_w_out': _jnp.float32, 'norm_f_w': _jnp.float32}
MOMENT_SCALE = {'norm_w': 8.007285e-02, 'a_w_in': 3.726471e-02, 'a_ln_w': 2.270847e-02, 'a_ln_b': 2.219385e-02, 'a_w_s': 3.279518e-02, 'a_b_s': 4.635404e-02, 'a_w_out': 6.300752e-02, 'b_w_in': 4.087703e-02, 'b_conv_w': 4.056703e-02, 'b_conv_b': 3.953935e-01, 'b_gate_a_w': 1.151109e-02, 'b_gate_a_b': 1.030059e-02, 'b_gate_x_w': 2.065852e-02, 'b_gate_x_b': 1.516612e-02, 'b_lambda': 1.953651e-02, 'b_w_out': 5.129807e-02, 'norm_f_w': 1.604908e+01}


def _to_microbatches(a, axis):
    t = _jnp.moveaxis(a, axis, 0)
    t = t.reshape((N_MICROBATCH, t.shape[0] // N_MICROBATCH) + t.shape[1:])
    return _jnp.moveaxis(t, 1, axis + 1)


def setup_inputs(seed: int = 0) -> dict:
    inp = _fwd_setup_inputs(seed)
    key = _jax.random.fold_in(_jax.random.key(seed), 7919)
    shape, _ = _output_shape()
    out = dict(inp)
    out["loss_target"] = _jax.random.normal(_jax.random.fold_in(key, 0), shape, _jnp.float32)
    for i, name in enumerate(TWIN_WEIGHTS):
        w = inp[name].astype(_jnp.float32)
        if MOMENT_SCALE is None:
            s = _jnp.sqrt(_jnp.mean(_jnp.square(w)) + 1e-30)
        else:
            s = MOMENT_SCALE[name]
        km, kv = _jax.random.split(_jax.random.fold_in(key, i + 1))
        out[name] = w
        out["m_" + name] = s * _jax.random.normal(km, w.shape, _jnp.float32)
        out["v_" + name] = (s * s) * _jax.random.uniform(kv, w.shape, _jnp.float32, 0.5, 1.5)
    if N_MICROBATCH > 1:
        for name, axis in PER_EXAMPLE_BATCH_AXIS.items():
            out[name] = _to_microbatches(out[name], axis)
    return {'x': out['x'], 'norm_w': out['norm_w'], 'a_w_in': out['a_w_in'], 'a_ln_w': out['a_ln_w'], 'a_ln_b': out['a_ln_b'], 'a_w_s': out['a_w_s'], 'a_b_s': out['a_b_s'], 'a_w_out': out['a_w_out'], 'b_w_in': out['b_w_in'], 'b_conv_w': out['b_conv_w'], 'b_conv_b': out['b_conv_b'], 'b_gate_a_w': out['b_gate_a_w'], 'b_gate_a_b': out['b_gate_a_b'], 'b_gate_x_w': out['b_gate_x_w'], 'b_gate_x_b': out['b_gate_x_b'], 'b_lambda': out['b_lambda'], 'b_w_out': out['b_w_out'], 'norm_f_w': out['norm_f_w'], 'loss_target': out['loss_target'], 'm_norm_w': out['m_norm_w'], 'm_a_w_in': out['m_a_w_in'], 'm_a_ln_w': out['m_a_ln_w'], 'm_a_ln_b': out['m_a_ln_b'], 'm_a_w_s': out['m_a_w_s'], 'm_a_b_s': out['m_a_b_s'], 'm_a_w_out': out['m_a_w_out'], 'm_b_w_in': out['m_b_w_in'], 'm_b_conv_w': out['m_b_conv_w'], 'm_b_conv_b': out['m_b_conv_b'], 'm_b_gate_a_w': out['m_b_gate_a_w'], 'm_b_gate_a_b': out['m_b_gate_a_b'], 'm_b_gate_x_w': out['m_b_gate_x_w'], 'm_b_gate_x_b': out['m_b_gate_x_b'], 'm_b_lambda': out['m_b_lambda'], 'm_b_w_out': out['m_b_w_out'], 'm_norm_f_w': out['m_norm_f_w'], 'v_norm_w': out['v_norm_w'], 'v_a_w_in': out['v_a_w_in'], 'v_a_ln_w': out['v_a_ln_w'], 'v_a_ln_b': out['v_a_ln_b'], 'v_a_w_s': out['v_a_w_s'], 'v_a_b_s': out['v_a_b_s'], 'v_a_w_out': out['v_a_w_out'], 'v_b_w_in': out['v_b_w_in'], 'v_b_conv_w': out['v_b_conv_w'], 'v_b_conv_b': out['v_b_conv_b'], 'v_b_gate_a_w': out['v_b_gate_a_w'], 'v_b_gate_a_b': out['v_b_gate_a_b'], 'v_b_gate_x_w': out['v_b_gate_x_w'], 'v_b_gate_x_b': out['v_b_gate_x_b'], 'v_b_lambda': out['v_b_lambda'], 'v_b_w_out': out['v_b_w_out'], 'v_norm_f_w': out['v_norm_f_w']}


def _loss(weights, diff, rest, loss_target):
    with _jax.named_scope("forward"):
        args = {**rest, TWIN_DIFF_INPUT: diff, **{k: w.astype(_WEIGHT_DTYPES[k]) for k, w in weights.items()}}
        y = _forward(args)
    with _jax.named_scope("loss_head"):
        err = _jnp.square(y.astype(_jnp.float32) - loss_target)
        return 0.5 * _jnp.sum(_jnp.mean(err, axis=-1)) if err.ndim else 0.5 * err


def _adamw(w, g, m, v):
    m = ADAM_B1 * m + (1.0 - ADAM_B1) * g
    v = ADAM_B2 * v + (1.0 - ADAM_B2) * _jnp.square(g)
    m_hat = m / (1.0 - ADAM_B1 ** ADAM_STEP)
    v_hat = v / (1.0 - ADAM_B2 ** ADAM_STEP)
    delta = -ADAM_LR * (m_hat / (_jnp.sqrt(v_hat) + ADAM_EPS) + ADAM_WD * w)
    return delta, m, v


def reference(x, norm_w, a_w_in, a_ln_w, a_ln_b, a_w_s, a_b_s, a_w_out, b_w_in, b_conv_w, b_conv_b, b_gate_a_w, b_gate_a_b, b_gate_x_w, b_gate_x_b, b_lambda, b_w_out, norm_f_w, loss_target, m_norm_w, m_a_w_in, m_a_ln_w, m_a_ln_b, m_a_w_s, m_a_b_s, m_a_w_out, m_b_w_in, m_b_conv_w, m_b_conv_b, m_b_gate_a_w, m_b_gate_a_b, m_b_gate_x_w, m_b_gate_x_b, m_b_lambda, m_b_w_out, m_norm_f_w, v_norm_w, v_a_w_in, v_a_ln_w, v_a_ln_b, v_a_w_s, v_a_b_s, v_a_w_out, v_b_w_in, v_b_conv_w, v_b_conv_b, v_b_gate_a_w, v_b_gate_a_b, v_b_gate_x_w, v_b_gate_x_b, v_b_lambda, v_b_w_out, v_norm_f_w):
    given = dict(x=x, norm_w=norm_w, a_w_in=a_w_in, a_ln_w=a_ln_w, a_ln_b=a_ln_b, a_w_s=a_w_s, a_b_s=a_b_s, a_w_out=a_w_out, b_w_in=b_w_in, b_conv_w=b_conv_w, b_conv_b=b_conv_b, b_gate_a_w=b_gate_a_w, b_gate_a_b=b_gate_a_b, b_gate_x_w=b_gate_x_w, b_gate_x_b=b_gate_x_b, b_lambda=b_lambda, b_w_out=b_w_out, norm_f_w=norm_f_w, loss_target=loss_target, m_norm_w=m_norm_w, m_a_w_in=m_a_w_in, m_a_ln_w=m_a_ln_w, m_a_ln_b=m_a_ln_b, m_a_w_s=m_a_w_s, m_a_b_s=m_a_b_s, m_a_w_out=m_a_w_out, m_b_w_in=m_b_w_in, m_b_conv_w=m_b_conv_w, m_b_conv_b=m_b_conv_b, m_b_gate_a_w=m_b_gate_a_w, m_b_gate_a_b=m_b_gate_a_b, m_b_gate_x_w=m_b_gate_x_w, m_b_gate_x_b=m_b_gate_x_b, m_b_lambda=m_b_lambda, m_b_w_out=m_b_w_out, m_norm_f_w=m_norm_f_w, v_norm_w=v_norm_w, v_a_w_in=v_a_w_in, v_a_ln_w=v_a_ln_w, v_a_ln_b=v_a_ln_b, v_a_w_s=v_a_w_s, v_a_b_s=v_a_b_s, v_a_w_out=v_a_w_out, v_b_w_in=v_b_w_in, v_b_conv_w=v_b_conv_w, v_b_conv_b=v_b_conv_b, v_b_gate_a_w=v_b_gate_a_w, v_b_gate_a_b=v_b_gate_a_b, v_b_gate_x_w=v_b_gate_x_w, v_b_gate_x_b=v_b_gate_x_b, v_b_lambda=v_b_lambda, v_b_w_out=v_b_w_out, v_norm_f_w=v_norm_f_w)
    weights = {n: given[n] for n in TWIN_WEIGHTS}
    shared = {n: given[n] for n in SHARED_INPUTS}
    per_example = {n: given[n] for n in ['x']}
    grad_fn = _jax.value_and_grad(_loss, argnums=(0, 1))

    def one_microbatch(ex, loss_target):
        ex = dict(ex)
        diff = ex.pop(TWIN_DIFF_INPUT)
        return grad_fn(weights, diff, {**shared, **ex}, loss_target)

    if N_MICROBATCH == 1:
        loss, (grad_w, grad_x) = one_microbatch(per_example, given["loss_target"])
    else:
        def body(carry, xs):
            loss_sum, grad_sum = carry
            l_k, (gw_k, gx_k) = one_microbatch(xs[0], xs[1])
            with _jax.named_scope("update"):
                return (loss_sum + l_k, _jax.tree.map(_jnp.add, grad_sum, gw_k)), gx_k

        init = (_jnp.zeros((), _jnp.float32), _jax.tree.map(_jnp.zeros_like, weights))
        (loss, grad_w), grad_x = _jax.lax.scan(body, init, (per_example, given["loss_target"]))
    with _jax.named_scope("update"):
        delta_w, new_m, new_v = {}, {}, {}
        for n in TWIN_WEIGHTS:
            delta_w[n], new_m[n], new_v[n] = _adamw(weights[n], grad_w[n], given["m_" + n], given["v_" + n])
    return (loss, grad_x, *[grad_w[n] for n in TWIN_WEIGHTS], *[delta_w[n] for n in TWIN_WEIGHTS],
            *[new_m[n] for n in TWIN_WEIGHTS], *[new_v[n] for n in TWIN_WEIGHTS])
```

```python
import functools

import jax
import jax.numpy as jnp
from jax import lax
from jax.experimental import pallas as pl
from jax.experimental.pallas import tpu as pltpu

F32 = jnp.float32
BF16 = jnp.bfloat16

RMS_EPS = 1e-6
LN_EPS = 1e-5
RG_C = 8.0
CHUNK = 128
A_GROUPS = 8
B_HEADS = 12
CONV_WIDTH = 4

ADAM_LR = 0.001
ADAM_B1 = 0.9
ADAM_B2 = 0.999
ADAM_EPS = 1e-08
ADAM_WD = 0.01
ADAM_STEP = 10

N_CHIPS = 4
N_DEV = 8
SUBLANES = 8
LANES = 128
V7X_VMEM_BYTES = 64 * 1024 * 1024
VMEM_LIMIT = V7X_VMEM_BYTES * 7 // 8
MESH = pl.DeviceIdType.MESH

GELU_C0 = 0.7978845608028654
GELU_C1 = 0.044715


def _pcall(body, **kw):
    return pl.pallas_call(body, **kw)


def _cparams(sem=None):
    return pltpu.CompilerParams(dimension_semantics=sem, vmem_limit_bytes=VMEM_LIMIT)


def _full(shape):
    zeros = (0,) * len(shape)
    return pl.BlockSpec(shape, lambda *_: zeros)


def _sigmoid(x):
    return 1.0 / (1.0 + jnp.exp(-x))


def _gelu(x):
    t = jnp.tanh(GELU_C0 * (x + GELU_C1 * (x * x * x)))
    return x * (0.5 * (1.0 + t))


def _gelu_grad(x):
    x2 = x * x
    t = jnp.tanh(GELU_C0 * (x + GELU_C1 * (x2 * x)))
    return 0.5 * (1.0 + t) + 0.5 * x * (1.0 - t * t) * (GELU_C0 * (1.0 + 3.0 * GELU_C1 * x2))


def _silu_and_grad(x):
    s = _sigmoid(x)
    return x * s, s * (1.0 + x * (1.0 - s))


def _softplus_neg(lam):
    u = jnp.exp(-jnp.abs(lam))
    w = 1.0 + u
    log1p = jnp.where(w == 1.0, u, jnp.log(w) * (u / jnp.where(w == 1.0, 1.0, w - 1.0)))
    return jnp.maximum(-lam, 0.0) + log1p


def _dot(a, b):
    return jnp.dot(a, b, preferred_element_type=F32)


def _dot_nt(a, b):
    return lax.dot_general(a, b, (((1,), (1,)), ((), ())), preferred_element_type=F32)


def _dot_tn(a, b):
    return lax.dot_general(a, b, (((0,), (0,)), ((), ())), preferred_element_type=F32)


def _shift_down(v, halo, k):
    if k == 0:
        return v
    rolled = pltpu.roll(v, k, 0)
    row = lax.broadcasted_iota(jnp.int32, (SUBLANES, v.shape[1]), 0)
    top = jnp.where(row < k, pltpu.roll(halo, k, 0), rolled[:SUBLANES])
    return jnp.concatenate([top, rolled[SUBLANES:]], axis=0)


def _shift_up(v, head, k):
    if k == 0:
        return v
    n = v.shape[0]
    rolled = pltpu.roll(v, n - k, 0)
    row = lax.broadcasted_iota(jnp.int32, (SUBLANES, v.shape[1]), 0)
    bot = jnp.where(row >= SUBLANES - k, pltpu.roll(head, SUBLANES - k, 0), rolled[n - SUBLANES :])
    return jnp.concatenate([rolled[: n - SUBLANES], bot], axis=0)


def _scan_blocks(a_ref, b_ref, out_ref, carry, n_rows, reverse):
    width = a_ref.shape[1]
    row = lax.broadcasted_iota(jnp.int32, (SUBLANES, width), 0)
    n_blocks = n_rows // SUBLANES

    def block(j, carry):
        i = (n_blocks - 1 - j) if reverse else j
        r0 = pl.multiple_of(i * SUBLANES, SUBLANES)
        a = a_ref[pl.ds(r0, SUBLANES), :]
        b = b_ref[pl.ds(r0, SUBLANES), :]
        for d in (1, 2, 4):
            shift = (SUBLANES - d) if reverse else d
            keep = (row < SUBLANES - d) if reverse else (row >= d)
            a_s = pltpu.roll(a, shift, 0)
            b_s = pltpu.roll(b, shift, 0)
            b = jnp.where(keep, a * b_s + b, b)
            a = jnp.where(keep, a * a_s, a)
        h = a * carry + b
        out_ref[pl.ds(r0, SUBLANES), :] = h
        edge = h[0:1, :] if reverse else h[SUBLANES - 1 : SUBLANES, :]
        return jnp.broadcast_to(edge, (SUBLANES, width))

    return lax.fori_loop(0, n_blocks, block, carry)


def _rms_fwd(x, w):
    r = lax.rsqrt(jnp.mean(x * x, axis=-1, keepdims=True) + RMS_EPS)
    xh = x * r
    return xh * w, xh, r


def _rms_bwd(dh, xh, r, w):
    dxh = dh * w
    dx = r * (dxh - xh * jnp.mean(dxh * xh, axis=-1, keepdims=True))
    return dx, jnp.sum(dh * xh, axis=0, keepdims=True)


def _cast_bf16(w, rows):
    n, c = w.shape

    def body(w_ref, o_ref):
        o_ref[...] = w_ref[...].astype(BF16)

    return _pcall(
        body,
        name=f"cast_{n}x{c}",
        grid=(n // rows,),
        in_specs=[pl.BlockSpec((rows, c), lambda i: (i, 0))],
        out_specs=pl.BlockSpec((rows, c), lambda i: (i, 0)),
        out_shape=jax.ShapeDtypeStruct((n, c), BF16),
        compiler_params=_cparams(("arbitrary",)),
    )(w)


def _place():
    x, y, c = lax.axis_index("x"), lax.axis_index("y"), lax.axis_index("c")
    chips = [(1 - x, y), (x, 1 - y), (1 - x, 1 - y)]
    return x, y, c, chips


def _chip_no(chip):
    return 2 * chip[0] + chip[1]


def _rcopy(src, dst, send_sem, recv_sem, to):
    return pltpu.make_async_remote_copy(
        src_ref=src, dst_ref=dst, send_sem=send_sem, recv_sem=recv_sem, device_id=to, device_id_type=MESH
    )


def _gather_weights(big, small):
    nb = len(big)

    def body(*refs):
        big_in = refs[:nb]
        small_in = refs[nb]
        big_out = refs[nb + 1 : 2 * nb + 1]
        small_out = refs[2 * nb + 1]
        send_sems, recv_sems, local_sems = refs[2 * nb + 2 :]
        x, y, c, chips = _place()
        me = (x, y, c)
        sibling = (x, y, 1 - c)
        mine = _chip_no((x, y))

        local = []
        for b in range(nb):
            cp = pltpu.make_async_copy(big_in[b], big_out[b].at[pl.ds(2 * mine, 2)], local_sems.at[b])
            cp.start()
            local.append(cp)
        cp = pltpu.make_async_copy(small_in, small_out.at[mine], local_sems.at[nb])
        cp.start()
        local.append(cp)

        def seg(b, chip, half):
            return big_out[b].at[2 * _chip_no(chip) + half]

        def sem_no(b, k):
            return b * 6 + k

        first = []
        for b in range(nb):
            for j, chip in enumerate(chips):
                k = sem_no(b, j)
                first.append(
                    _rcopy(big_in[b].at[c], seg(b, (x, y), c), send_sems.at[k], recv_sems.at[k], (*chip, c))
                )
        for j, chip in enumerate(chips):
            k = 6 * nb + j
            first.append(_rcopy(small_in, small_out.at[mine], send_sems.at[k], recv_sems.at[k], (*chip, c)))
        for cp in first:
            cp.start()
        passed = []
        for b in range(nb):
            for j, chip in enumerate(chips):
                k = sem_no(b, j)
                _rcopy(seg(b, chip, c), seg(b, chip, c), send_sems.at[k], recv_sems.at[k], me).wait_recv()
                k2 = sem_no(b, 3 + j)
                fwd = _rcopy(seg(b, chip, c), seg(b, chip, c), send_sems.at[k2], recv_sems.at[k2], sibling)
                fwd.start()
                passed.append(fwd)
        for j, chip in enumerate(chips):
            k = 6 * nb + j
            there = small_out.at[_chip_no(chip)]
            _rcopy(there, there, send_sems.at[k], recv_sems.at[k], me).wait_recv()
        for b in range(nb):
            for j, chip in enumerate(chips):
                k2 = sem_no(b, 3 + j)
                there = seg(b, chip, 1 - c)
                _rcopy(there, there, send_sems.at[k2], recv_sems.at[k2], me).wait_recv()
        for cp in first + passed:
            cp.wait_send()
        for cp in local:
            cp.wait()

    n_sems = 6 * nb + 3
    any_spec = pl.BlockSpec(memory_space=pl.ANY)
    outs = _pcall(
        body,
        name="gather_weights",
        in_specs=[any_spec] * (nb + 1),
        out_specs=[any_spec] * (nb + 1),
        out_shape=[jax.ShapeDtypeStruct((N_DEV, *b.shape[1:]), b.dtype) for b in big]
        + [jax.ShapeDtypeStruct((N_CHIPS, *small.shape), small.dtype)],
        scratch_shapes=[
            pltpu.SemaphoreType.DMA((n_sems,)),
            pltpu.SemaphoreType.DMA((n_sems,)),
            pltpu.SemaphoreType.DMA((nb + 1,)),
        ],
    )(*big, small)
    return outs[:nb], outs[nb]


def _swap_halves(bufs):
    nb = len(bufs)

    def body(*refs):
        src = refs[:nb]
        dst = refs[nb : 2 * nb]
        send_sems, recv_sems = refs[2 * nb :]
        x, y, c, _ = _place()
        sibling = (x, y, 1 - c)
        copies = []
        for b in range(nb):
            for j in range(N_CHIPS):
                k = b * N_CHIPS + j
                cp = _rcopy(src[b].at[2 * j + 1 - c], dst[b].at[j], send_sems.at[k], recv_sems.at[k], sibling)
                cp.start()
                copies.append(cp)
        for cp in copies:
            cp.wait()

    any_spec = pl.BlockSpec(memory_space=pl.ANY)
    return _pcall(
        body,
        name="swap_halves",
        in_specs=[any_spec] * nb,
        out_specs=[any_spec] * nb,
        out_shape=[jax.ShapeDtypeStruct((N_CHIPS, *b.shape[1:]), b.dtype) for b in bufs],
        scratch_shapes=[pltpu.SemaphoreType.DMA((nb * N_CHIPS,)), pltpu.SemaphoreType.DMA((nb * N_CHIPS,))],
    )(*bufs)


def _send_to_owners(parts):
    nb = len(parts)

    def body(*refs):
        src = refs[:nb]
        dst = refs[nb : 2 * nb]
        send_sems, recv_sems = refs[2 * nb :]
        x, y, c, chips = _place()
        copies = []
        for b in range(nb):
            for j, chip in enumerate(chips):
                k = b * 3 + j
                cp = _rcopy(src[b].at[_chip_no(chip)], dst[b].at[j], send_sems.at[k], recv_sems.at[k], (*chip, c))
                cp.start()
                copies.append(cp)
        for cp in copies:
            cp.wait()

    any_spec = pl.BlockSpec(memory_space=pl.ANY)
    return _pcall(
        body,
        name="send_to_owners",
        in_specs=[any_spec] * nb,
        out_specs=[any_spec] * nb,
        out_shape=[jax.ShapeDtypeStruct((3, *p.shape[1:]), p.dtype) for p in parts],
        scratch_shapes=[pltpu.SemaphoreType.DMA((nb * 3,)), pltpu.SemaphoreType.DMA((nb * 3,))],
    )(*parts)


def _share_reduced(big, small):
    nb = len(big)

    def body(*refs):
        big_in = refs[:nb]
        small_in = refs[nb]
        big_out = refs[nb + 1 : 2 * nb + 1]
        small_out = refs[2 * nb + 1]
        send_sems, recv_sems, local_sems = refs[2 * nb + 2 :]
        x, y, c, chips = _place()
        me = (x, y, c)
        sibling = (x, y, 1 - c)

        def seg(chip, half):
            return small_out.at[2 * _chip_no(chip) + half]

        local = []
        for b in range(nb):
            cp = pltpu.make_async_copy(big_in[b], big_out[b].at[c], local_sems.at[b])
            cp.start()
            local.append(cp)
        cp = pltpu.make_async_copy(small_in, seg((x, y), c), local_sems.at[nb])
        cp.start()
        local.append(cp)

        first = []
        for b in range(nb):
            first.append(_rcopy(big_in[b], big_out[b].at[c], send_sems.at[b], recv_sems.at[b], sibling))
        s0 = nb
        first.append(_rcopy(small_in, seg((x, y), c), send_sems.at[s0], recv_sems.at[s0], sibling))
        for j, chip in enumerate(chips):
            k = s0 + 1 + j
            first.append(_rcopy(small_in, seg((x, y), c), send_sems.at[k], recv_sems.at[k], (*chip, c)))
        for cp in first:
            cp.start()
        passed = []
        for j, chip in enumerate(chips):
            k = s0 + 1 + j
            _rcopy(seg(chip, c), seg(chip, c), send_sems.at[k], recv_sems.at[k], me).wait_recv()
            k2 = s0 + 4 + j
            fwd = _rcopy(seg(chip, c), seg(chip, c), send_sems.at[k2], recv_sems.at[k2], sibling)
            fwd.start()
            passed.append(fwd)
        for b in range(nb):
            there = big_out[b].at[1 - c]
            _rcopy(there, there, send_sems.at[b], recv_sems.at[b], me).wait_recv()
        there = seg((x, y), 1 - c)
        _rcopy(there, there, send_sems.at[s0], recv_sems.at[s0], me).wait_recv()
        for j, chip in enumerate(chips):
            k2 = s0 + 4 + j
            there = seg(chip, 1 - c)
            _rcopy(there, there, send_sems.at[k2], recv_sems.at[k2], me).wait_recv()
        for cp in first + passed:
            cp.wait_send()
        for cp in local:
            cp.wait()

    n_sems = nb + 7
    any_spec = pl.BlockSpec(memory_space=pl.ANY)
    outs = _pcall(
        body,
        name="share_reduced",
        in_specs=[any_spec] * (nb + 1),
        out_specs=[any_spec] * (nb + 1),
        out_shape=[jax.ShapeDtypeStruct((2, *b.shape), b.dtype) for b in big]
        + [jax.ShapeDtypeStruct((N_DEV, *small.shape), small.dtype)],
        scratch_shapes=[
            pltpu.SemaphoreType.DMA((n_sems,)),
            pltpu.SemaphoreType.DMA((n_sems,)),
            pltpu.SemaphoreType.DMA((nb + 1,)),
        ],
    )(*big, small)
    return outs[:nb], outs[nb]


def _row_tile(rows, cols, target_bytes=2 * 1024 * 1024):
    best = SUBLANES
    for t in range(SUBLANES, rows + 1, SUBLANES):
        if rows % t == 0 and t * cols * 4 <= target_bytes:
            best = t
    return best


def _add_own_half(buf, got):
    _, rows, cols = buf.shape
    tr = _row_tile(rows, cols)
    c = lax.axis_index("c").astype(jnp.int32).reshape(1)

    def body(c_ref, a_ref, b_ref, o_ref):
        o_ref[...] = a_ref[...] + b_ref[...]

    return _pcall(
        body,
        name=f"add_own_half_{rows}x{cols}",
        grid_spec=pltpu.PrefetchScalarGridSpec(
            num_scalar_prefetch=1,
            grid=(N_CHIPS, rows // tr),
            in_specs=[
                pl.BlockSpec((None, None, tr, cols), lambda j, r, c_ref: (j, c_ref[0], r, 0)),
                pl.BlockSpec((None, tr, cols), lambda j, r, c_ref: (j, r, 0)),
            ],
            out_specs=pl.BlockSpec((None, tr, cols), lambda j, r, c_ref: (j, r, 0)),
        ),
        out_shape=jax.ShapeDtypeStruct((N_CHIPS, rows, cols), F32),
        compiler_params=_cparams(("arbitrary", "arbitrary")),
    )(c, buf.reshape(N_CHIPS, 2, rows, cols), got)


def _add_received(part, got):
    _, rows, cols = part.shape
    tr = _row_tile(rows, cols)
    mine = (2 * lax.axis_index("x") + lax.axis_index("y")).astype(jnp.int32).reshape(1)

    def body(k_ref, p_ref, g_ref, o_ref):
        o_ref[...] = ((p_ref[...] + g_ref[0]) + g_ref[1]) + g_ref[2]

    return _pcall(
        body,
        name=f"add_received_{rows}x{cols}",
        grid_spec=pltpu.PrefetchScalarGridSpec(
            num_scalar_prefetch=1,
            grid=(rows // tr,),
            in_specs=[
                pl.BlockSpec((None, tr, cols), lambda r, k_ref: (k_ref[0], r, 0)),
                pl.BlockSpec((3, tr, cols), lambda r, k_ref: (0, r, 0)),
            ],
            out_specs=pl.BlockSpec((tr, cols), lambda r, k_ref: (r, 0)),
        ),
        out_shape=jax.ShapeDtypeStruct((rows, cols), F32),
        compiler_params=_cparams(("arbitrary",)),
    )(mine, part, got)


def _reduce_to_owners(big, small):
    bufs = list(big) + [small]
    got_a = _swap_halves(bufs)
    parts = [_add_own_half(b, g) for b, g in zip(bufs, got_a)]
    got_b = _send_to_owners(parts)
    reduced = [_add_received(p, g) for p, g in zip(parts, got_b)]
    return _share_reduced(reduced[:-1], reduced[-1])


def _layer_a_fwd(x, nw, win, ln_w, ln_b, wc, bs_t, wout, tm):
    t_rows, d = x.shape
    n_sh, _, s_cols = win.shape
    aw = wout.shape[0]
    gd = aw // A_GROUPS
    tn = 512
    assert s_cols % tn == 0 and aw % tn == 0 and tm % CHUNK == 0

    def body(x_ref, nw_ref, win_ref, lnw_ref, lnb_ref, wc_ref, bst_ref, wout_ref, z_ref, x1_ref, h_ref, u_s, v_s, y_s):
        x = x_ref[...]
        h, _, _ = _rms_fwd(x, nw_ref[...])
        h = h.astype(BF16)
        h_ref[...] = h
        for j in range(3 * aw // tn):
            k, off = divmod(j * tn, s_cols)
            cols = slice((j * tn) % aw, (j * tn) % aw + tn)
            zj = _dot(h, win_ref[k, :, off : off + tn])
            z_ref[:, j * tn : (j + 1) * tn] = zj
            if j * tn < aw:
                u_s[:, cols] = _gelu(zj)
            elif j * tn < 2 * aw:
                v_s[:, cols] = _gelu(zj)
            else:
                u_s[:, cols] = u_s[:, cols] * (zj * _sigmoid(zj))
        v = v_s[...]
        mu = jnp.mean(v, axis=-1, keepdims=True)
        vc = v - mu
        rstd = lax.rsqrt(jnp.mean(vc * vc, axis=-1, keepdims=True) + LN_EPS)
        v_s[...] = (vc * rstd) * lnw_ref[...] + lnb_ref[...]
        for ck in range(tm // CHUNK):
            rows = slice(ck * CHUNK, (ck + 1) * CHUNK)
            for g in range(A_GROUPS):
                cols = slice(g * gd, (g + 1) * gd)
                s = _dot(wc_ref[g], v_s[rows, cols].astype(BF16)) + bst_ref[:, g : g + 1]
                y_s[rows, cols] = (u_s[rows, cols] * s).astype(BF16)
        x1_ref[...] = x + _dot(y_s[...], wout_ref[...])

    return _pcall(
        body,
        name="layer_a_fwd",
        grid=(t_rows // tm,),
        in_specs=[
            pl.BlockSpec((tm, d), lambda i: (i, 0)),
            _full(nw.shape),
            _full(win.shape),
            _full(ln_w.shape),
            _full(ln_b.shape),
            _full(wc.shape),
            _full(bs_t.shape),
            _full(wout.shape),
        ],
        out_specs=[
            pl.BlockSpec((tm, 3 * aw), lambda i: (i, 0)),
            pl.BlockSpec((tm, d), lambda i: (i, 0)),
            pl.BlockSpec((tm, d), lambda i: (i, 0)),
        ],
        out_shape=[
            jax.ShapeDtypeStruct((t_rows, 3 * aw), F32),
            jax.ShapeDtypeStruct((t_rows, d), F32),
            jax.ShapeDtypeStruct((t_rows, d), BF16),
        ],
        scratch_shapes=[pltpu.VMEM((tm, aw), F32), pltpu.VMEM((tm, aw), F32), pltpu.VMEM((tm, aw), BF16)],
        compiler_params=_cparams(("arbitrary",)),
    )(x, nw, win, ln_w, ln_b, wc, bs_t, wout)


def _layer_a_bwd(dout, x, z, nw, win, ln_w, ln_b, wc, wct, bs_t, wout):
    t_rows, d = x.shape
    n_sh, _, s_cols = win.shape
    aw = wout.shape[0]
    gd = aw // A_GROUPS
    tm = CHUNK

    def body(
        dout_ref, x_ref, z_ref, nw_ref, win_ref, lnw_ref, lnb_ref, wc_ref, wct_ref, bst_ref, wout_ref,
        gx_ref, dz_ref, y_ref, dob_ref, gws_ref, gbs_ref, glnw_ref, glnb_ref, gnw_ref,
        u_s, vh_s, ds_s, dvn_s,
    ):
        @pl.when(pl.program_id(0) == 0)
        def _():
            gws_ref[...] = jnp.zeros_like(gws_ref)
            gbs_ref[...] = jnp.zeros_like(gbs_ref)
            glnw_ref[...] = jnp.zeros_like(glnw_ref)
            glnb_ref[...] = jnp.zeros_like(glnb_ref)
            gnw_ref[...] = jnp.zeros_like(gnw_ref)

        dout = dout_ref[...]
        dob = dout.astype(BF16)
        dob_ref[...] = dob
        dy = _dot_nt(dob, wout_ref[...])

        zv = z_ref[:, aw : 2 * aw]
        vg = _gelu(zv)
        mu = jnp.mean(vg, axis=-1, keepdims=True)
        vc = vg - mu
        rstd = lax.rsqrt(jnp.mean(vc * vc, axis=-1, keepdims=True) + LN_EPS)
        vh = vc * rstd
        vh_s[...] = vh
        vn = (vh * lnw_ref[...] + lnb_ref[...]).astype(BF16)

        zu = z_ref[:, 0:aw]
        zg = z_ref[:, 2 * aw : 3 * aw]
        u = _gelu(zu)
        sg, dsg = _silu_and_grad(zg)
        u_s[...] = u * sg
        tril = lax.broadcasted_iota(jnp.int32, (CHUNK, CHUNK), 0) >= lax.broadcasted_iota(jnp.int32, (CHUNK, CHUNK), 1)
        for g in range(A_GROUPS):
            cols = slice(g * gd, (g + 1) * gd)
            vn_g = vn[:, cols]
            s = _dot(wc_ref[g], vn_g) + bst_ref[:, g : g + 1]
            usg = u_s[:, cols]
            dy_g = dy[:, cols]
            y_ref[:, cols] = (usg * s).astype(BF16)
            ds = dy_g * usg
            ds_s[:, cols] = dy_g * s
            gbs_ref[:, g : g + 1] += jnp.sum(ds, axis=-1, keepdims=True)
            dsb = ds.astype(BF16)
            gws_ref[g] += jnp.where(tril, _dot_nt(dsb, vn_g), 0.0)
            dvn_s[:, cols] = _dot(wct_ref[g], dsb)
        dusg = ds_s[...]
        dz_ref[:, 0:aw] = (dusg * sg * _gelu_grad(zu)).astype(BF16)
        dz_ref[:, 2 * aw : 3 * aw] = (dusg * u * dsg).astype(BF16)

        dvn = dvn_s[...]
        vh = vh_s[...]
        glnw_ref[...] += jnp.sum(dvn * vh, axis=0, keepdims=True)
        glnb_ref[...] += jnp.sum(dvn, axis=0, keepdims=True)
        dvh = dvn * lnw_ref[...]
        dvg = rstd * (
            dvh - jnp.mean(dvh, axis=-1, keepdims=True) - vh * jnp.mean(dvh * vh, axis=-1, keepdims=True)
        )
        dz_ref[:, aw : 2 * aw] = (dvg * _gelu_grad(zv)).astype(BF16)

        dh = jnp.zeros((tm, d), F32)
        for k in range(n_sh):
            dh = dh + _dot_nt(dz_ref[:, k * s_cols : (k + 1) * s_cols], win_ref[k])
        x = x_ref[...]
        nw = nw_ref[...]
        _, xh, r = _rms_fwd(x, nw)
        dx, gnw = _rms_bwd(dh, xh, r, nw)
        gnw_ref[...] += gnw
        gx_ref[...] = dout + dx

    row = lambda i: (i, 0)
    return _pcall(
        body,
        name="layer_a_bwd",
        grid=(t_rows // tm,),
        in_specs=[
            pl.BlockSpec((tm, d), row),
            pl.BlockSpec((tm, d), row),
            pl.BlockSpec((tm, 3 * aw), row),
            _full(nw.shape),
            _full(win.shape),
            _full(ln_w.shape),
            _full(ln_b.shape),
            _full(wc.shape),
            _full(wct.shape),
            _full(bs_t.shape),
            _full(wout.shape),
        ],
        out_specs=[
            pl.BlockSpec((tm, d), row),
            pl.BlockSpec((tm, 3 * aw), row),
            pl.BlockSpec((tm, aw), row),
            pl.BlockSpec((tm, d), row),
            _full((A_GROUPS, CHUNK, CHUNK)),
            _full((CHUNK, A_GROUPS)),
            _full((1, aw)),
            _full((1, aw)),
            _full((1, d)),
        ],
        out_shape=[
            jax.ShapeDtypeStruct((t_rows, d), F32),
            jax.ShapeDtypeStruct((t_rows, 3 * aw), BF16),
            jax.ShapeDtypeStruct((t_rows, aw), BF16),
            jax.ShapeDtypeStruct((t_rows, d), BF16),
            jax.ShapeDtypeStruct((A_GROUPS, CHUNK, CHUNK), F32),
            jax.ShapeDtypeStruct((CHUNK, A_GROUPS), F32),
            jax.ShapeDtypeStruct((1, aw), F32),
            jax.ShapeDtypeStruct((1, aw), F32),
            jax.ShapeDtypeStruct((1, d), F32),
        ],
        scratch_shapes=[pltpu.VMEM((tm, aw), F32)] * 4,
        compiler_params=_cparams(("arbitrary",)),
    )(dout, x, z, nw, win, ln_w, ln_b, wc, wct, bs_t, wout)


def _gates(xc_h, gab_ref, gb_ref, sp_h, h, hd):
    pre = _dot(xc_h.astype(BF16), gab_ref[h])
    bw = gb_ref.shape[1] // 2
    r = _sigmoid(pre[:, :hd] + gb_ref[:, h * hd : (h + 1) * hd])
    ig = _sigmoid(pre[:, hd:] + gb_ref[:, bw + h * hd : bw + (h + 1) * hd])
    log_a = (-RG_C) * r * sp_h
    a = jnp.exp(log_a)
    mult = jnp.sqrt(jnp.tanh(-log_a) * (a * a + 1.0))
    return r, ig, a, mult


def _conv(xb, halo, cw_ref, cb_ref):
    xc = cb_ref[...] + cw_ref[CONV_WIDTH - 1 : CONV_WIDTH, :] * xb
    for k in range(CONV_WIDTH - 1):
        xc = xc + cw_ref[k : k + 1, :] * _shift_down(xb, halo, CONV_WIDTH - 1 - k)
    return xc


def _layer_b_fwd(x1, nw, bin_w, cw, cb, gab, gb, lam, bout, nf, tgt, tm):
    t_rows, d = x1.shape
    bw = bout.shape[0]
    hd = bw // B_HEADS
    nt = t_rows // tm

    def body(
        x1_ref, nw_ref, bin_ref, cw_ref, cb_ref, gab_ref, gb_ref, lam_ref, bout_ref, nf_ref, tgt_ref,
        z_ref, h_ref, h1_ref, xbt_ref, ht_ref, dx2_ref, loss_ref, gnf_ref,
        tail_s, carry_s, a_s, b_s, hs_s, acc_s,
    ):
        @pl.when(pl.program_id(0) == 0)
        def _():
            tail_s[...] = jnp.zeros_like(tail_s)
            carry_s[...] = jnp.zeros_like(carry_s)
            acc_s[...] = jnp.zeros_like(acc_s)
            gnf_ref[...] = jnp.zeros_like(gnf_ref)

        x1 = x1_ref[...]
        h1, _, _ = _rms_fwd(x1, nw_ref[...])
        h1 = h1.astype(BF16)
        h1_ref[...] = h1
        z = jnp.concatenate([_dot(h1, bin_ref[k]) for k in range(N_CHIPS)], axis=1)
        z_ref[...] = z
        xb = z[:, :bw]
        xc = _conv(xb, tail_s[...], cw_ref, cb_ref)
        tail = xb[tm - SUBLANES :, :]
        tail_s[...] = tail
        xbt_ref[...] = tail
        sp = _softplus_neg(lam_ref[...])
        for h in range(B_HEADS):
            cols = slice(h * hd, (h + 1) * hd)
            xc_h = xc[:, cols]
            _, ig, a, mult = _gates(xc_h, gab_ref, gb_ref, sp[:, cols], h, hd)
            a_s[:, cols] = a
            b_s[:, cols] = mult * (ig * xc_h)
        carry = _scan_blocks(a_s, b_s, hs_s, carry_s[...], tm, reverse=False)
        carry_s[...] = carry
        ht_ref[...] = hs_s[tm - SUBLANES :, :]
        hs = hs_s[...]
        h_ref[...] = hs
        g = z[:, bw:]
        y = (hs * (g * _sigmoid(g))).astype(BF16)
        x2 = x1 + _dot(y, bout_ref[...])

        nf = nf_ref[...]
        o, xh, r = _rms_fwd(x2, nf)
        diff = o - tgt_ref[...]
        acc_s[...] += jnp.sum(diff * diff, axis=0, keepdims=True)
        do = diff * (1.0 / d)
        dx2, gnf = _rms_bwd(do, xh, r, nf)
        gnf_ref[...] += gnf
        dx2_ref[...] = dx2

        @pl.when(pl.program_id(0) == nt - 1)
        def _():
            total = jnp.sum(acc_s[...], axis=-1, keepdims=True) * (0.5 / d)
            loss_ref[...] = jnp.broadcast_to(total, loss_ref.shape)

    row = lambda i: (i, 0)
    return _pcall(
        body,
        name="layer_b_fwd",
        grid=(nt,),
        in_specs=[
            pl.BlockSpec((tm, d), row),
            _full(nw.shape),
            _full(bin_w.shape),
            _full(cw.shape),
            _full(cb.shape),
            _full(gab.shape),
            _full(gb.shape),
            _full(lam.shape),
            _full(bout.shape),
            _full(nf.shape),
            pl.BlockSpec((tm, d), row),
        ],
        out_specs=[
            pl.BlockSpec((tm, 2 * bw), row),
            pl.BlockSpec((tm, bw), row),
            pl.BlockSpec((tm, d), row),
            pl.BlockSpec((None, SUBLANES, bw), lambda i: (i, 0, 0)),
            pl.BlockSpec((None, SUBLANES, bw), lambda i: (i, 0, 0)),
            pl.BlockSpec((tm, d), row),
            _full((1, LANES)),
            _full((1, d)),
        ],
        out_shape=[
            jax.ShapeDtypeStruct((t_rows, 2 * bw), F32),
            jax.ShapeDtypeStruct((t_rows, bw), F32),
            jax.ShapeDtypeStruct((t_rows, d), BF16),
            jax.ShapeDtypeStruct((nt, SUBLANES, bw), F32),
            jax.ShapeDtypeStruct((nt, SUBLANES, bw), F32),
            jax.ShapeDtypeStruct((t_rows, d), F32),
            jax.ShapeDtypeStruct((1, LANES), F32),
            jax.ShapeDtypeStruct((1, d), F32),
        ],
        scratch_shapes=[
            pltpu.VMEM((SUBLANES, bw), F32),
            pltpu.VMEM((SUBLANES, bw), F32),
            pltpu.VMEM((tm, bw), F32),
            pltpu.VMEM((tm, bw), F32),
            pltpu.VMEM((tm, bw), F32),
            pltpu.VMEM((1, d), F32),
        ],
        compiler_params=_cparams(("arbitrary",)),
    )(x1, nw, bin_w, cw, cb, gab, gb, lam, bout, nf, tgt)


def _layer_b_bwd(dout, x1, z, hseq, xb_tails, h_tails, nw, bin_w, cw, cb, gab, gabt, gb, lam, bout, tm):
    t_rows, d = x1.shape
    bw = bout.shape[0]
    hd = bw // B_HEADS
    nt = t_rows // tm

    def body(
        dout_ref, x1_ref, z_ref, h_ref, xbt_ref, ht_ref, nw_ref, bin_ref, cw_ref, cb_ref, gab_ref, gabt_ref,
        gb_ref, lam_ref, bout_ref,
        dx1_ref, dz_ref, y_ref, dob_ref, ggab_ref, ggb_ref, gcw_ref, gcb_ref, glam_ref, gnw_ref,
        gcarry_s, afirst_s, head_s, aup_s, dh_s, gt_s, dxc_s, xc_s,
    ):
        step = pl.program_id(0)
        tile = nt - 1 - step

        @pl.when(step == 0)
        def _():
            for ref in (ggab_ref, ggb_ref, gcw_ref, gcb_ref, glam_ref, gnw_ref, gcarry_s, afirst_s, head_s):
                ref[...] = jnp.zeros_like(ref)

        first_tile = tile == 0
        xb_halo = jnp.where(first_tile, 0.0, xbt_ref[...])
        h_halo = jnp.where(first_tile, 0.0, ht_ref[...])

        dout = dout_ref[...]
        dob = dout.astype(BF16)
        dob_ref[...] = dob
        dy = _dot_nt(dob, bout_ref[...])
        hs = h_ref[...]
        g = z_ref[:, bw:]
        sg, dsg = _silu_and_grad(g)
        y_ref[...] = (hs * sg).astype(BF16)
        dz_ref[:, bw:] = (dy * hs * dsg).astype(BF16)
        dh_s[...] = dy * sg

        xb = z_ref[:, :bw]
        xc = _conv(xb, xb_halo, cw_ref, cb_ref)
        xc_s[...] = xc
        lam = lam_ref[...]
        sp = _softplus_neg(lam)
        for h in range(B_HEADS):
            cols = slice(h * hd, (h + 1) * hd)
            _, _, a, _ = _gates(xc[:, cols], gab_ref, gb_ref, sp[:, cols], h, hd)
            aup_s[:, cols] = _shift_up(a, afirst_s[:, cols], 1)
            afirst_s[:, cols] = jnp.broadcast_to(a[0:1, :], (SUBLANES, hd))
        carry = _scan_blocks(aup_s, dh_s, gt_s, gcarry_s[...], tm, reverse=True)
        gcarry_s[...] = carry

        h_prev = _shift_down(hs, h_halo, 1)
        dsp = jnp.zeros((1, bw), F32)
        for h in range(B_HEADS):
            cols = slice(h * hd, (h + 1) * hd)
            xc_h = xc_s[:, cols]
            sp_h = sp[:, cols]
            r, ig, a, mult = _gates(xc_h, gab_ref, gb_ref, sp_h, h, hd)
            gt = gt_s[:, cols]
            da = gt * h_prev[:, cols]
            dmult = gt * (ig * xc_h)
            dig = gt * (mult * xc_h)
            dxc_direct = gt * (mult * ig)
            dla = da * a - dmult * (a * a) / mult
            glam_ref[:, cols] += jnp.sum(dla * r, axis=0, keepdims=True)
            dr = dla * ((-RG_C) * sp_h)
            dpre = jnp.concatenate([dr * r * (1.0 - r), dig * ig * (1.0 - ig)], axis=1)
            ggb_ref[:, cols] += jnp.sum(dpre[:, :hd], axis=0, keepdims=True)
            ggb_ref[:, bw + h * hd : bw + (h + 1) * hd] += jnp.sum(dpre[:, hd:], axis=0, keepdims=True)
            dpb = dpre.astype(BF16)
            ggab_ref[h] += _dot_tn(xc_h.astype(BF16), dpb)
            dxc_s[:, cols] = dxc_direct + _dot(dpb, gabt_ref[h])
        glam_ref[...] = jnp.where(step == nt - 1, glam_ref[...] * (RG_C * _sigmoid(-lam)), glam_ref[...])

        dxc = dxc_s[...]
        gcb_ref[...] += jnp.sum(dxc, axis=0, keepdims=True)
        dxb = cw_ref[CONV_WIDTH - 1 : CONV_WIDTH, :] * dxc
        gcw_ref[CONV_WIDTH - 1 : CONV_WIDTH, :] += jnp.sum(dxc * xb, axis=0, keepdims=True)
        head = head_s[...]
        for k in range(CONV_WIDTH - 1):
            lag = CONV_WIDTH - 1 - k
            dxb = dxb + cw_ref[k : k + 1, :] * _shift_up(dxc, head, lag)
            gcw_ref[k : k + 1, :] += jnp.sum(dxc * _shift_down(xb, xb_halo, lag), axis=0, keepdims=True)
        head_s[...] = dxc[:SUBLANES, :]
        dz_ref[:, :bw] = dxb.astype(BF16)

        s_cols = 2 * bw // N_CHIPS
        dh1 = jnp.zeros((tm, d), F32)
        for k in range(N_CHIPS):
            dh1 = dh1 + _dot_nt(dz_ref[:, k * s_cols : (k + 1) * s_cols], bin_ref[k])
        x1 = x1_ref[...]
        nw = nw_ref[...]
        _, xh, r1 = _rms_fwd(x1, nw)
        dx, gnw = _rms_bwd(dh1, xh, r1, nw)
        gnw_ref[...] += gnw
        dx1_ref[...] = dout + dx

    rev = lambda i: (nt - 1 - i, 0)
    prev = lambda i: (jnp.maximum(nt - 2 - i, 0), 0, 0)
    return _pcall(
        body,
        name="layer_b_bwd",
        grid=(nt,),
        in_specs=[
            pl.BlockSpec((tm, d), rev),
            pl.BlockSpec((tm, d), rev),
            pl.BlockSpec((tm, 2 * bw), rev),
            pl.BlockSpec((tm, bw), rev),
            pl.BlockSpec((None, SUBLANES, bw), prev),
            pl.BlockSpec((None, SUBLANES, bw), prev),
            _full(nw.shape),
            _full(bin_w.shape),
            _full(cw.shape),
            _full(cb.shape),
            _full(gab.shape),
            _full(gabt.shape),
            _full(gb.shape),
            _full(lam.shape),
            _full(bout.shape),
        ],
        out_specs=[
            pl.BlockSpec((tm, d), rev),
            pl.BlockSpec((tm, 2 * bw), rev),
            pl.BlockSpec((tm, bw), rev),
            pl.BlockSpec((tm, d), rev),
            _full((B_HEADS, hd, 2 * hd)),
            _full((1, 2 * bw)),
            _full((SUBLANES, bw)),
            _full((1, bw)),
            _full((1, bw)),
            _full((1, d)),
        ],
        out_shape=[
            jax.ShapeDtypeStruct((t_rows, d), F32),
            jax.ShapeDtypeStruct((t_rows, 2 * bw), BF16),
            jax.ShapeDtypeStruct((t_rows, bw), BF16),
            jax.ShapeDtypeStruct((t_rows, d), BF16),
            jax.ShapeDtypeStruct((B_HEADS, hd, 2 * hd), F32),
            jax.ShapeDtypeStruct((1, 2 * bw), F32),
            jax.ShapeDtypeStruct((SUBLANES, bw), F32),
            jax.ShapeDtypeStruct((1, bw), F32),
            jax.ShapeDtypeStruct((1, bw), F32),
            jax.ShapeDtypeStruct((1, d), F32),
        ],
        scratch_shapes=[pltpu.VMEM((SUBLANES, bw), F32)] * 3 + [pltpu.VMEM((tm, bw), F32)] * 5,
        compiler_params=_cparams(("arbitrary",)),
    )(dout, x1, z, hseq, xb_tails, h_tails, nw, bin_w, cw, cb, gab, gabt, gb, lam, bout)


def _wgrad(a, b, m_blocks, n_blocks):
    k, m = a.shape
    n = b.shape[1]
    bm, bn = m // m_blocks, n // n_blocks

    def body(a_ref, b_ref, o_ref):
        o_ref[...] = _dot_tn(a_ref[...], b_ref[...])

    return _pcall(
        body,
        name=f"wgrad_{m}x{n}",
        grid=(n_blocks, m_blocks),
        in_specs=[pl.BlockSpec((k, bm), lambda j, i: (0, i)), pl.BlockSpec((k, bn), lambda j, i: (0, j))],
        out_specs=pl.BlockSpec((None, None, bm, bn), lambda j, i: (j, i, 0, 0)),
        out_shape=jax.ShapeDtypeStruct((n_blocks, m_blocks, bm, bn), F32),
        compiler_params=_cparams(("arbitrary", "arbitrary")),
    )(a, b)


def _adamw_math(w, g, m, v):
    m = ADAM_B1 * m + (1.0 - ADAM_B1) * g
    v = ADAM_B2 * v + (1.0 - ADAM_B2) * (g * g)
    m_hat = m / (1.0 - ADAM_B1**ADAM_STEP)
    v_hat = v / (1.0 - ADAM_B2**ADAM_STEP)
    delta = -ADAM_LR * (m_hat / (jnp.sqrt(v_hat) + ADAM_EPS) + ADAM_WD * w)
    return delta, m, v


def _adamw(w, g, m, v):
    rows, cols = w.shape
    tr = _row_tile(rows, cols, 1024 * 1024)

    def body(w_ref, g_ref, m_ref, v_ref, d_ref, mo_ref, vo_ref):
        d_ref[...], mo_ref[...], vo_ref[...] = _adamw_math(w_ref[...], g_ref[...], m_ref[...], v_ref[...])

    spec = pl.BlockSpec((tr, cols), lambda i: (i, 0))
    return _pcall(
        body,
        name=f"adamw_{rows}x{cols}",
        grid=(rows // tr,),
        in_specs=[spec] * 4,
        out_specs=[spec] * 3,
        out_shape=[jax.ShapeDtypeStruct((rows, cols), F32)] * 3,
        compiler_params=_cparams(("arbitrary",)),
    )(w, g, m, v)


def _pack_rows(parts, lanes=LANES):
    flat = jnp.concatenate([p.reshape(-1) for p in parts])
    per = N_DEV * SUBLANES * lanes
    total = -(-flat.shape[0] // per) * per
    flat = jnp.pad(flat, (0, total - flat.shape[0]))
    return flat.reshape(N_DEV, total // (N_DEV * lanes), lanes)


def _unpack(flat, shapes):
    out, at = [], 0
    for s in shapes:
        n = 1
        for dim in s:
            n *= dim
        out.append(flat[at : at + n].reshape(s))
        at += n
    return out


def kernel(x, norm_w, a_w_in, a_ln_w, a_ln_b, a_w_s, a_b_s, a_w_out, b_w_in, b_conv_w, b_conv_b, b_gate_a_w, b_gate_a_b, b_gate_x_w, b_gate_x_b, b_lambda, b_w_out, norm_f_w, loss_target, m_norm_w, m_a_w_in, m_a_ln_w, m_a_ln_b, m_a_w_s, m_a_b_s, m_a_w_out, m_b_w_in, m_b_conv_w, m_b_conv_b, m_b_gate_a_w, m_b_gate_a_b, m_b_gate_x_w, m_b_gate_x_b, m_b_lambda, m_b_w_out, m_norm_f_w, v_norm_w, v_a_w_in, v_a_ln_w, v_a_ln_b, v_a_w_s, v_a_b_s, v_a_w_out, v_b_w_in, v_b_conv_w, v_b_conv_b, v_b_gate_a_w, v_b_gate_a_b, v_b_gate_x_w, v_b_gate_x_b, v_b_lambda, v_b_w_out, v_norm_f_w):
    t_rows, d = x.shape[1], x.shape[2]
    aw = a_ln_w.shape[1]
    bw = b_gate_a_w.shape[1] * b_gate_a_w.shape[2]
    hd = bw // B_HEADS
    mine = 2 * lax.axis_index("x") + lax.axis_index("y")

    win_l = _cast_bf16(a_w_in[0], 256)
    wout_l = _cast_bf16(a_w_out[0], 256)
    bin_l = _cast_bf16(b_w_in[0], 256)
    bout_l = _cast_bf16(b_w_out[0], 192)
    small_l = jnp.concatenate([b_conv_w[0], b_conv_b, b_gate_a_b, b_gate_x_b, b_lambda], axis=0)
    halves = lambda w: w.reshape(2, w.shape[0] // 2, w.shape[1])
    (win_g, wout_g, bin_g, bout_g), small_g = _gather_weights(
        [halves(win_l), halves(wout_l), halves(bin_l), halves(bout_l)], small_l
    )
    win = win_g.reshape(N_CHIPS, d, -1)
    wout = wout_g.reshape(aw, d)
    bin_w = bin_g.reshape(N_CHIPS, d, -1)
    bout = bout_g.reshape(bw, d)
    small_f = jnp.transpose(small_g, (1, 0, 2)).reshape(SUBLANES, bw)
    cw, cb = small_f[0:CONV_WIDTH], small_f[CONV_WIDTH : CONV_WIDTH + 1]
    gb = jnp.concatenate([small_f[5:6], small_f[6:7]], axis=1)
    lam = small_f[7:8]

    tril = jnp.tril(jnp.ones((CHUNK, CHUNK), F32))
    wc = (a_w_s[0] * tril[None]).astype(BF16)
    wct = jnp.swapaxes(wc, 1, 2)
    bs_t = a_b_s[0].T
    gab = jnp.concatenate([b_gate_a_w[0], b_gate_x_w[0]], axis=2).astype(BF16)
    gabt = jnp.swapaxes(gab, 1, 2)
    nw0, nw1, nf = norm_w[0:1], norm_w[1:2], norm_f_w.reshape(1, d)

    x0 = x[0]
    z_a, x1, h0 = _layer_a_fwd(x0, nw0, win, a_ln_w, a_ln_b, wc, bs_t, wout, 256)
    z_b, hseq, h1, xb_tails, h_tails, dx2, loss_l, g_nf = _layer_b_fwd(
        x1, nw1, bin_w, cw, cb, gab, gb, lam, bout, nf, loss_target[0], 256
    )
    dx1, dz_b, y_b, dob_b, g_gab, g_gb, g_cw, g_cb, g_lam, g_nw1 = _layer_b_bwd(
        dx2, x1, z_b, hseq, xb_tails, h_tails, nw1, bin_w, cw, cb, gab, gabt, gb, lam, bout, 256
    )
    grad_x, dz_a, y_a, dob_a, g_ws, g_bst, g_lnw, g_lnb, g_nw0 = _layer_a_bwd(
        dx1, x0, z_a, nw0, win, a_ln_w, a_ln_b, wc, wct, bs_t, wout
    )
    g_bout = _wgrad(y_b, dob_b, N_CHIPS, 1)
    g_bin = _wgrad(h1, dz_b, 2, N_CHIPS)
    g_wout = _wgrad(y_a, dob_a, N_DEV, 1)
    g_win = _wgrad(h0, dz_a, 2, N_CHIPS)
    seg = lambda g: g.reshape(N_DEV, -1, g.shape[3])

    small_shapes = [
        (2, d), (1, aw), (1, aw), (A_GROUPS, CHUNK, CHUNK), (A_GROUPS, CHUNK), (B_HEADS, hd, hd), (B_HEADS, hd, hd),
        (d,), (CONV_WIDTH, bw), (1, bw), (1, bw), (1, bw), (1, bw),
    ]
    small = _pack_rows(
        [
            jnp.concatenate([g_nw0, g_nw1], axis=0), g_lnw, g_lnb, g_ws, g_bst.T, g_gab[:, :, :hd], g_gab[:, :, hd:],
            g_nf, g_cw[:CONV_WIDTH], g_cb, g_gb[:, :bw], g_gb[:, bw:], g_lam,
        ]
    )
    big = [seg(g_win), seg(g_wout), seg(g_bin), seg(g_bout)]
    (gr_win, gr_wout, gr_bin, gr_bout), small_r = _reduce_to_owners(big, small)
    gr_win = gr_win.reshape(a_w_in.shape[1:])
    gr_wout = gr_wout.reshape(a_w_out.shape[1:])
    gr_bin = gr_bin.reshape(b_w_in.shape[1:])
    gr_bout = gr_bout.reshape(b_w_out.shape[1:])
    (g_norm_w, g_a_ln_w, g_a_ln_b, g_a_w_s, g_a_b_s, g_gate_a_w, g_gate_x_w, g_norm_f, gf_cw, gf_cb, gf_gab, gf_gxb,
     gf_lam) = _unpack(small_r.reshape(-1), small_shapes)
    shard = lambda g: lax.dynamic_slice_in_dim(g, mine * (bw // N_CHIPS), bw // N_CHIPS, axis=1)

    grads = {
        "norm_w": g_norm_w, "a_w_in": gr_win[None], "a_ln_w": g_a_ln_w, "a_ln_b": g_a_ln_b, "a_w_s": g_a_w_s[None],
        "a_b_s": g_a_b_s[None], "a_w_out": gr_wout[None], "b_w_in": gr_bin[None], "b_conv_w": shard(gf_cw)[None],
        "b_conv_b": shard(gf_cb), "b_gate_a_w": g_gate_a_w[None], "b_gate_a_b": shard(gf_gab),
        "b_gate_x_w": g_gate_x_w[None], "b_gate_x_b": shard(gf_gxb), "b_lambda": shard(gf_lam),
        "b_w_out": gr_bout[None], "norm_f_w": g_norm_f,
    }
    weights = dict(norm_w=norm_w, a_w_in=a_w_in, a_ln_w=a_ln_w, a_ln_b=a_ln_b, a_w_s=a_w_s, a_b_s=a_b_s, a_w_out=a_w_out, b_w_in=b_w_in, b_conv_w=b_conv_w, b_conv_b=b_conv_b, b_gate_a_w=b_gate_a_w, b_gate_a_b=b_gate_a_b, b_gate_x_w=b_gate_x_w, b_gate_x_b=b_gate_x_b, b_lambda=b_lambda, b_w_out=b_w_out, norm_f_w=norm_f_w)
    m_in = dict(norm_w=m_norm_w, a_w_in=m_a_w_in, a_ln_w=m_a_ln_w, a_ln_b=m_a_ln_b, a_w_s=m_a_w_s, a_b_s=m_a_b_s, a_w_out=m_a_w_out, b_w_in=m_b_w_in, b_conv_w=m_b_conv_w, b_conv_b=m_b_conv_b, b_gate_a_w=m_b_gate_a_w, b_gate_a_b=m_b_gate_a_b, b_gate_x_w=m_b_gate_x_w, b_gate_x_b=m_b_gate_x_b, b_lambda=m_b_lambda, b_w_out=m_b_w_out, norm_f_w=m_norm_f_w)
    v_in = dict(norm_w=v_norm_w, a_w_in=v_a_w_in, a_ln_w=v_a_ln_w, a_ln_b=v_a_ln_b, a_w_s=v_a_w_s, a_b_s=v_a_b_s, a_w_out=v_a_w_out, b_w_in=v_b_w_in, b_conv_w=v_b_conv_w, b_conv_b=v_b_conv_b, b_gate_a_w=v_b_gate_a_w, b_gate_a_b=v_b_gate_a_b, b_gate_x_w=v_b_gate_x_w, b_gate_x_b=v_b_gate_x_b, b_lambda=v_b_lambda, b_w_out=v_b_w_out, norm_f_w=v_norm_f_w)
    names = list(weights)
    big_names = ("a_w_in", "a_w_out", "b_w_in", "b_w_out")
    delta, new_m, new_v = {}, {}, {}
    for n in big_names:
        shape = weights[n].shape
        two_d = lambda a: a.reshape(shape[1:])
        dl, mo, vo = _adamw(two_d(weights[n]), two_d(grads[n]), two_d(m_in[n]), two_d(v_in[n]))
        delta[n], new_m[n], new_v[n] = dl.reshape(shape), mo.reshape(shape), vo.reshape(shape)
    small_names = [n for n in names if n not in big_names]
    packed = [
        _pack_rows([src[n] for n in small_names]).reshape(-1, LANES) for src in (weights, grads, m_in, v_in)
    ]
    outs = _adamw(*packed)
    shapes = [weights[n].shape for n in small_names]
    for dst, flat in zip((delta, new_m, new_v), outs):
        for n, val in zip(small_names, _unpack(flat.reshape(-1), shapes)):
            dst[n] = val

    loss = lax.psum(loss_l[0, 0], ("x", "y", "c"))
    return (
        loss,
        grad_x[None],
        *[grads[n] for n in names],
        *[delta[n] for n in names],
        *[new_m[n] for n in names],
        *[new_v[n] for n in names],
    )
```

```python
import functools

import jax
import jax.numpy as jnp
from jax import lax
from jax.experimental import pallas as pl
from jax.experimental.pallas import tpu as pltpu

F32 = jnp.float32
BF16 = jnp.bfloat16

RMS_EPS = 1e-6
LN_EPS = 1e-5
RG_C = 8.0
CHUNK = 128
A_GROUPS = 8
B_HEADS = 12
CONV_WIDTH = 4

ADAM_LR = 0.001
ADAM_B1 = 0.9
ADAM_B2 = 0.999
ADAM_EPS = 1e-08
ADAM_WD = 0.01
ADAM_STEP = 10

N_CHIPS = 4
N_DEV = 8
SUBLANES = 8
LANES = 128
V7X_VMEM_BYTES = 64 * 1024 * 1024
VMEM_LIMIT = V7X_VMEM_BYTES * 7 // 8
MESH = pl.DeviceIdType.MESH

GELU_C0 = 0.7978845608028654
GELU_C1 = 0.044715


def _pcall(body, **kw):
    return pl.pallas_call(body, **kw)


def _cparams(sem=None):
    return pltpu.CompilerParams(dimension_semantics=sem, vmem_limit_bytes=VMEM_LIMIT)


def _full(shape):
    zeros = (0,) * len(shape)
    return pl.BlockSpec(shape, lambda *_: zeros)


def _sigmoid(x):
    return 1.0 / (1.0 + jnp.exp(-x))


def _gelu(x):
    t = jnp.tanh(GELU_C0 * (x + GELU_C1 * (x * x * x)))
    return x * (0.5 * (1.0 + t))


def _gelu_grad(x):
    x2 = x * x
    t = jnp.tanh(GELU_C0 * (x + GELU_C1 * (x2 * x)))
    return 0.5 * (1.0 + t) + 0.5 * x * (1.0 - t * t) * (GELU_C0 * (1.0 + 3.0 * GELU_C1 * x2))


def _silu_and_grad(x):
    s = _sigmoid(x)
    return x * s, s * (1.0 + x * (1.0 - s))


def _softplus_neg(lam):
    u = jnp.exp(-jnp.abs(lam))
    w = 1.0 + u
    log1p = jnp.where(w == 1.0, u, jnp.log(w) * (u / jnp.where(w == 1.0, 1.0, w - 1.0)))
    return jnp.maximum(-lam, 0.0) + log1p


def _dot(a, b):
    return jnp.dot(a, b, preferred_element_type=F32)


def _dot_nt(a, b):
    return lax.dot_general(a, b, (((1,), (1,)), ((), ())), preferred_element_type=F32)


def _dot_tn(a, b):
    return lax.dot_general(a, b, (((0,), (0,)), ((), ())), preferred_element_type=F32)


def _shift_down(v, halo, k):
    if k == 0:
        return v
    rolled = pltpu.roll(v, k, 0)
    row = lax.broadcasted_iota(jnp.int32, (SUBLANES, v.shape[1]), 0)
    top = jnp.where(row < k, pltpu.roll(halo, k, 0), rolled[:SUBLANES])
    return jnp.concatenate([top, rolled[SUBLANES:]], axis=0)


def _shift_up(v, head, k):
    if k == 0:
        return v
    n = v.shape[0]
    rolled = pltpu.roll(v, n - k, 0)
    row = lax.broadcasted_iota(jnp.int32, (SUBLANES, v.shape[1]), 0)
    bot = jnp.where(row >= SUBLANES - k, pltpu.roll(head, SUBLANES - k, 0), rolled[n - SUBLANES :])
    return jnp.concatenate([rolled[: n - SUBLANES], bot], axis=0)


def _scan_blocks(a_ref, b_ref, out_ref, carry, n_rows, reverse):
    width = a_ref.shape[1]
    row = lax.broadcasted_iota(jnp.int32, (SUBLANES, width), 0)
    n_blocks = n_rows // SUBLANES

    def block(j, carry):
        i = (n_blocks - 1 - j) if reverse else j
        r0 = pl.multiple_of(i * SUBLANES, SUBLANES)
        a = a_ref[pl.ds(r0, SUBLANES), :]
        b = b_ref[pl.ds(r0, SUBLANES), :]
        for d in (1, 2, 4):
            shift = (SUBLANES - d) if reverse else d
            keep = (row < SUBLANES - d) if reverse else (row >= d)
            a_s = pltpu.roll(a, shift, 0)
            b_s = pltpu.roll(b, shift, 0)
            b = jnp.where(keep, a * b_s + b, b)
            a = jnp.where(keep, a * a_s, a)
        h = a * carry + b
        out_ref[pl.ds(r0, SUBLANES), :] = h
        edge = h[0:1, :] if reverse else h[SUBLANES - 1 : SUBLANES, :]
        return jnp.broadcast_to(edge, (SUBLANES, width))

    return lax.fori_loop(0, n_blocks, block, carry)


def _rms_fwd(x, w):
    r = lax.rsqrt(jnp.mean(x * x, axis=-1, keepdims=True) + RMS_EPS)
    xh = x * r
    return xh * w, xh, r


def _rms_bwd(dh, xh, r, w):
    dxh = dh * w
    dx = r * (dxh - xh * jnp.mean(dxh * xh, axis=-1, keepdims=True))
    return dx, jnp.sum(dh * xh, axis=0, keepdims=True)


def _scalars(*vals):
    return jnp.stack([jnp.asarray(v, jnp.int32) for v in vals])


def _cast_to_segments(w, mine, rows):
    n, c = w.shape
    per = n // 2 // rows

    def body(k_ref, w_ref, o_ref):
        o_ref[...] = w_ref[...].astype(BF16)

    return _pcall(
        body,
        name=f"cast_{n}x{c}",
        grid_spec=pltpu.PrefetchScalarGridSpec(
            num_scalar_prefetch=1,
            grid=(n // rows,),
            in_specs=[pl.BlockSpec((rows, c), lambda i, k_ref: (i, 0))],
            out_specs=pl.BlockSpec((None, rows, c), lambda i, k_ref: (2 * k_ref[0] + i // per, i % per, 0)),
        ),
        out_shape=jax.ShapeDtypeStruct((N_DEV, n // 2, c), BF16),
        compiler_params=_cparams(("arbitrary",)),
    )(_scalars(mine), w)


def _place():
    x, y, c = lax.axis_index("x"), lax.axis_index("y"), lax.axis_index("c")
    chips = [(1 - x, y), (x, 1 - y), (1 - x, 1 - y)]
    return x, y, c, chips


def _chip_no(chip):
    return 2 * chip[0] + chip[1]


def _rcopy(src, dst, send_sem, recv_sem, to):
    return pltpu.make_async_remote_copy(
        src_ref=src, dst_ref=dst, send_sem=send_sem, recv_sem=recv_sem, device_id=to, device_id_type=MESH
    )


def _gather_weights(big, small):
    nb = len(big)

    def body(*refs):
        small_in = refs[nb]
        big_out = refs[nb + 1 : 2 * nb + 1]
        small_out = refs[2 * nb + 1]
        send_sems, recv_sems, local_sems = refs[2 * nb + 2 :]
        x, y, c, chips = _place()
        me = (x, y, c)
        sibling = (x, y, 1 - c)
        mine = _chip_no((x, y))

        cp = pltpu.make_async_copy(small_in, small_out.at[mine], local_sems.at[0])
        cp.start()
        local = [cp]

        def seg(b, chip, half):
            return big_out[b].at[2 * _chip_no(chip) + half]

        def sem_no(b, k):
            return b * 6 + k

        first = []
        for b in range(nb):
            for j, chip in enumerate(chips):
                k = sem_no(b, j)
                first.append(
                    _rcopy(seg(b, (x, y), c), seg(b, (x, y), c), send_sems.at[k], recv_sems.at[k], (*chip, c))
                )
        for j, chip in enumerate(chips):
            k = 6 * nb + j
            first.append(_rcopy(small_in, small_out.at[mine], send_sems.at[k], recv_sems.at[k], (*chip, c)))
        for cp in first:
            cp.start()
        passed = []
        for b in range(nb):
            for j, chip in enumerate(chips):
                k = sem_no(b, j)
                _rcopy(seg(b, chip, c), seg(b, chip, c), send_sems.at[k], recv_sems.at[k], me).wait_recv()
                k2 = sem_no(b, 3 + j)
                fwd = _rcopy(seg(b, chip, c), seg(b, chip, c), send_sems.at[k2], recv_sems.at[k2], sibling)
                fwd.start()
                passed.append(fwd)
        for j, chip in enumerate(chips):
            k = 6 * nb + j
            there = small_out.at[_chip_no(chip)]
            _rcopy(there, there, send_sems.at[k], recv_sems.at[k], me).wait_recv()
        for b in range(nb):
            for j, chip in enumerate(chips):
                k2 = sem_no(b, 3 + j)
                there = seg(b, chip, 1 - c)
                _rcopy(there, there, send_sems.at[k2], recv_sems.at[k2], me).wait_recv()
        for cp in first + passed:
            cp.wait_send()
        for cp in local:
            cp.wait()

    n_sems = 6 * nb + 3
    any_spec = pl.BlockSpec(memory_space=pl.ANY)
    outs = _pcall(
        body,
        name="gather_weights",
        in_specs=[any_spec] * (nb + 1),
        out_specs=[any_spec] * (nb + 1),
        out_shape=[jax.ShapeDtypeStruct(b.shape, b.dtype) for b in big]
        + [jax.ShapeDtypeStruct((N_CHIPS, *small.shape), small.dtype)],
        scratch_shapes=[
            pltpu.SemaphoreType.DMA((n_sems,)),
            pltpu.SemaphoreType.DMA((n_sems,)),
            pltpu.SemaphoreType.DMA((1,)),
        ],
        input_output_aliases={b: b for b in range(nb)},
    )(*big, small)
    return outs[:nb], outs[nb]


def _swap_halves(bufs):
    nb = len(bufs)

    def body(*refs):
        src = refs[:nb]
        dst = refs[nb : 2 * nb]
        send_sems, recv_sems = refs[2 * nb :]
        x, y, c, _ = _place()
        sibling = (x, y, 1 - c)
        copies = []
        for b in range(nb):
            for j in range(N_CHIPS):
                k = b * N_CHIPS + j
                cp = _rcopy(src[b].at[2 * j + 1 - c], dst[b].at[j], send_sems.at[k], recv_sems.at[k], sibling)
                cp.start()
                copies.append(cp)
        for cp in copies:
            cp.wait()

    any_spec = pl.BlockSpec(memory_space=pl.ANY)
    return _pcall(
        body,
        name="swap_halves",
        in_specs=[any_spec] * nb,
        out_specs=[any_spec] * nb,
        out_shape=[jax.ShapeDtypeStruct((N_CHIPS, *b.shape[1:]), b.dtype) for b in bufs],
        scratch_shapes=[pltpu.SemaphoreType.DMA((nb * N_CHIPS,)), pltpu.SemaphoreType.DMA((nb * N_CHIPS,))],
    )(*bufs)


def _send_to_owners(parts):
    nb = len(parts)

    def body(*refs):
        src = refs[:nb]
        dst = refs[nb : 2 * nb]
        send_sems, recv_sems = refs[2 * nb :]
        x, y, c, chips = _place()
        copies = []
        for b in range(nb):
            for j, chip in enumerate(chips):
                k = b * 3 + j
                cp = _rcopy(src[b].at[_chip_no(chip)], dst[b].at[j], send_sems.at[k], recv_sems.at[k], (*chip, c))
                cp.start()
                copies.append(cp)
        for cp in copies:
            cp.wait()

    any_spec = pl.BlockSpec(memory_space=pl.ANY)
    return _pcall(
        body,
        name="send_to_owners",
        in_specs=[any_spec] * nb,
        out_specs=[any_spec] * nb,
        out_shape=[jax.ShapeDtypeStruct((3, *p.shape[1:]), p.dtype) for p in parts],
        scratch_shapes=[pltpu.SemaphoreType.DMA((nb * 3,)), pltpu.SemaphoreType.DMA((nb * 3,))],
    )(*parts)


def _share_reduced(big, small):
    nb = len(big)

    def body(*refs):
        big_out = refs[nb + 1 : 2 * nb + 1]
        small_out = refs[2 * nb + 1]
        send_sems, recv_sems = refs[2 * nb + 2 :]
        x, y, c, chips = _place()
        me = (x, y, c)
        sibling = (x, y, 1 - c)

        def seg(chip, half):
            return small_out.at[2 * _chip_no(chip) + half]

        first = []
        for b in range(nb):
            first.append(_rcopy(big_out[b].at[c], big_out[b].at[c], send_sems.at[b], recv_sems.at[b], sibling))
        s0 = nb
        own = seg((x, y), c)
        first.append(_rcopy(own, own, send_sems.at[s0], recv_sems.at[s0], sibling))
        for j, chip in enumerate(chips):
            k = s0 + 1 + j
            first.append(_rcopy(own, own, send_sems.at[k], recv_sems.at[k], (*chip, c)))
        for cp in first:
            cp.start()
        passed = []
        for j, chip in enumerate(chips):
            k = s0 + 1 + j
            _rcopy(seg(chip, c), seg(chip, c), send_sems.at[k], recv_sems.at[k], me).wait_recv()
            k2 = s0 + 4 + j
            fwd = _rcopy(seg(chip, c), seg(chip, c), send_sems.at[k2], recv_sems.at[k2], sibling)
            fwd.start()
            passed.append(fwd)
        for b in range(nb):
            there = big_out[b].at[1 - c]
            _rcopy(there, there, send_sems.at[b], recv_sems.at[b], me).wait_recv()
        there = seg((x, y), 1 - c)
        _rcopy(there, there, send_sems.at[s0], recv_sems.at[s0], me).wait_recv()
        for j, chip in enumerate(chips):
            k2 = s0 + 4 + j
            there = seg(chip, 1 - c)
            _rcopy(there, there, send_sems.at[k2], recv_sems.at[k2], me).wait_recv()
        for cp in first + passed:
            cp.wait_send()

    n_sems = nb + 7
    any_spec = pl.BlockSpec(memory_space=pl.ANY)
    outs = _pcall(
        body,
        name="share_reduced",
        in_specs=[any_spec] * (nb + 1),
        out_specs=[any_spec] * (nb + 1),
        out_shape=[jax.ShapeDtypeStruct(b.shape, b.dtype) for b in big] + [jax.ShapeDtypeStruct(small.shape, small.dtype)],
        scratch_shapes=[pltpu.SemaphoreType.DMA((n_sems,)), pltpu.SemaphoreType.DMA((n_sems,))],
        input_output_aliases={b: b for b in range(nb + 1)},
    )(*big, small)
    return outs[:nb], outs[nb]


def _row_tile(rows, cols, target_bytes=2 * 1024 * 1024):
    best = SUBLANES
    for t in range(SUBLANES, rows + 1, SUBLANES):
        if rows % t == 0 and t * cols * 4 <= target_bytes:
            best = t
    return best


def _add_own_half(buf, got, c, wire):
    _, rows, cols = buf.shape
    tr = _row_tile(rows, cols)

    def body(c_ref, a_ref, b_ref, o_ref):
        o_ref[...] = (a_ref[...] + b_ref[...]).astype(wire)

    return _pcall(
        body,
        name=f"add_own_half_{rows}x{cols}",
        grid_spec=pltpu.PrefetchScalarGridSpec(
            num_scalar_prefetch=1,
            grid=(N_CHIPS, rows // tr),
            in_specs=[
                pl.BlockSpec((None, None, tr, cols), lambda j, r, c_ref: (j, c_ref[0], r, 0)),
                pl.BlockSpec((None, tr, cols), lambda j, r, c_ref: (j, r, 0)),
            ],
            out_specs=pl.BlockSpec((None, tr, cols), lambda j, r, c_ref: (j, r, 0)),
        ),
        out_shape=jax.ShapeDtypeStruct((N_CHIPS, rows, cols), wire),
        compiler_params=_cparams(("arbitrary", "arbitrary")),
    )(_scalars(c), buf.reshape(N_CHIPS, 2, rows, cols), got)


def _add_received(buf, got_a, got_b, mine, c, slot, n_slots):
    _, rows, cols = buf.shape
    tr = _row_tile(rows, cols)

    def body(s_ref, x_ref, a_ref, g_ref, o_ref):
        own = x_ref[...] + a_ref[...]
        o_ref[...] = ((own + g_ref[0].astype(F32)) + g_ref[1].astype(F32)) + g_ref[2].astype(F32)

    return _pcall(
        body,
        name=f"add_received_{rows}x{cols}",
        grid_spec=pltpu.PrefetchScalarGridSpec(
            num_scalar_prefetch=1,
            grid=(rows // tr,),
            in_specs=[
                pl.BlockSpec((None, None, tr, cols), lambda r, s_ref: (s_ref[0], s_ref[1], r, 0)),
                pl.BlockSpec((None, tr, cols), lambda r, s_ref: (s_ref[0], r, 0)),
                pl.BlockSpec((3, tr, cols), lambda r, s_ref: (0, r, 0)),
            ],
            out_specs=pl.BlockSpec((None, tr, cols), lambda r, s_ref: (s_ref[2], r, 0)),
        ),
        out_shape=jax.ShapeDtypeStruct((n_slots, rows, cols), F32),
        compiler_params=_cparams(("arbitrary",)),
    )(_scalars(mine, c, slot), buf.reshape(N_CHIPS, 2, rows, cols), got_a, got_b)


def _reduce_to_owners(big, small, mine, c):
    bufs = list(big) + [small]
    nb = len(big)
    got_a = _swap_halves(bufs)
    parts = [_add_own_half(b, g, c, BF16 if i < nb else F32) for i, (b, g) in enumerate(zip(bufs, got_a))]
    got_b = _send_to_owners(parts)
    reduced = [_add_received(b, ga, gb, mine, c, c, 2) for b, ga, gb in zip(big, got_a[:nb], got_b[:nb])]
    reduced_small = _add_received(small, got_a[nb], got_b[nb], mine, c, 2 * mine + c, N_DEV)
    return _share_reduced(reduced, reduced_small)


def _layer_a_fwd(x, nw, win, ln_w, ln_b, wc, bs_t, wout, tm):
    t_rows, d = x.shape
    n_sh, _, s_cols = win.shape
    aw = wout.shape[0]
    gd = aw // A_GROUPS
    tn = 512
    assert s_cols % tn == 0 and aw % tn == 0 and tm % CHUNK == 0

    def body(x_ref, nw_ref, win_ref, lnw_ref, lnb_ref, wc_ref, bst_ref, wout_ref, z_ref, x1_ref, h_ref, u_s, v_s, y_s):
        x = x_ref[...]
        h, _, _ = _rms_fwd(x, nw_ref[...])
        h = h.astype(BF16)
        h_ref[...] = h
        for j in range(3 * aw // tn):
            k, off = divmod(j * tn, s_cols)
            cols = slice((j * tn) % aw, (j * tn) % aw + tn)
            zj = _dot(h, win_ref[k, :, off : off + tn])
            z_ref[:, j * tn : (j + 1) * tn] = zj
            if j * tn < aw:
                u_s[:, cols] = _gelu(zj)
            elif j * tn < 2 * aw:
                v_s[:, cols] = _gelu(zj)
            else:
                u_s[:, cols] = u_s[:, cols] * (zj * _sigmoid(zj))
        v = v_s[...]
        mu = jnp.mean(v, axis=-1, keepdims=True)
        vc = v - mu
        rstd = lax.rsqrt(jnp.mean(vc * vc, axis=-1, keepdims=True) + LN_EPS)
        v_s[...] = (vc * rstd) * lnw_ref[...] + lnb_ref[...]
        for ck in range(tm // CHUNK):
            rows = slice(ck * CHUNK, (ck + 1) * CHUNK)
            for g in range(A_GROUPS):
                cols = slice(g * gd, (g + 1) * gd)
                s = _dot(wc_ref[g], v_s[rows, cols].astype(BF16)) + bst_ref[:, g : g + 1]
                y_s[rows, cols] = (u_s[rows, cols] * s).astype(BF16)
        x1_ref[...] = x + _dot(y_s[...], wout_ref[...])

    return _pcall(
        body,
        name="layer_a_fwd",
        grid=(t_rows // tm,),
        in_specs=[
            pl.BlockSpec((tm, d), lambda i: (i, 0)),
            _full(nw.shape),
            _full(win.shape),
            _full(ln_w.shape),
            _full(ln_b.shape),
            _full(wc.shape),
            _full(bs_t.shape),
            _full(wout.shape),
        ],
        out_specs=[
            pl.BlockSpec((tm, 3 * aw), lambda i: (i, 0)),
            pl.BlockSpec((tm, d), lambda i: (i, 0)),
            pl.BlockSpec((tm, d), lambda i: (i, 0)),
        ],
        out_shape=[
            jax.ShapeDtypeStruct((t_rows, 3 * aw), F32),
            jax.ShapeDtypeStruct((t_rows, d), F32),
            jax.ShapeDtypeStruct((t_rows, d), BF16),
        ],
        scratch_shapes=[pltpu.VMEM((tm, aw), F32), pltpu.VMEM((tm, aw), F32), pltpu.VMEM((tm, aw), BF16)],
        compiler_params=_cparams(("arbitrary",)),
    )(x, nw, win, ln_w, ln_b, wc, bs_t, wout)


def _layer_a_bwd(dout, x, z, nw, win, ln_w, ln_b, wc, wct, bs_t, wout):
    t_rows, d = x.shape
    n_sh, _, s_cols = win.shape
    aw = wout.shape[0]
    gd = aw // A_GROUPS
    tm = CHUNK

    def body(
        dout_ref, x_ref, z_ref, nw_ref, win_ref, lnw_ref, lnb_ref, wc_ref, wct_ref, bst_ref, wout_ref,
        gx_ref, dz_ref, y_ref, dob_ref, gws_ref, gbs_ref, glnw_ref, glnb_ref, gnw_ref,
        u_s, vh_s, ds_s, dvn_s,
    ):
        @pl.when(pl.program_id(0) == 0)
        def _():
            gws_ref[...] = jnp.zeros_like(gws_ref)
            gbs_ref[...] = jnp.zeros_like(gbs_ref)
            glnw_ref[...] = jnp.zeros_like(glnw_ref)
            glnb_ref[...] = jnp.zeros_like(glnb_ref)
            gnw_ref[...] = jnp.zeros_like(gnw_ref)

        dout = dout_ref[...]
        dob = dout.astype(BF16)
        dob_ref[...] = dob
        dy = _dot_nt(dob, wout_ref[...])

        zv = z_ref[:, aw : 2 * aw]
        vg = _gelu(zv)
        mu = jnp.mean(vg, axis=-1, keepdims=True)
        vc = vg - mu
        rstd = lax.rsqrt(jnp.mean(vc * vc, axis=-1, keepdims=True) + LN_EPS)
        vh = vc * rstd
        vh_s[...] = vh
        vn = (vh * lnw_ref[...] + lnb_ref[...]).astype(BF16)

        zu = z_ref[:, 0:aw]
        zg = z_ref[:, 2 * aw : 3 * aw]
        u = _gelu(zu)
        sg, dsg = _silu_and_grad(zg)
        u_s[...] = u * sg
        tril = lax.broadcasted_iota(jnp.int32, (CHUNK, CHUNK), 0) >= lax.broadcasted_iota(jnp.int32, (CHUNK, CHUNK), 1)
        for g in range(A_GROUPS):
            cols = slice(g * gd, (g + 1) * gd)
            vn_g = vn[:, cols]
            s = _dot(wc_ref[g], vn_g) + bst_ref[:, g : g + 1]
            usg = u_s[:, cols]
            dy_g = dy[:, cols]
            y_ref[:, cols] = (usg * s).astype(BF16)
            ds = dy_g * usg
            ds_s[:, cols] = dy_g * s
            gbs_ref[:, g : g + 1] += jnp.sum(ds, axis=-1, keepdims=True)
            dsb = ds.astype(BF16)
            gws_ref[g] += jnp.where(tril, _dot_nt(dsb, vn_g), 0.0)
            dvn_s[:, cols] = _dot(wct_ref[g], dsb)
        dusg = ds_s[...]
        dz_ref[:, 0:aw] = (dusg * sg * _gelu_grad(zu)).astype(BF16)
        dz_ref[:, 2 * aw : 3 * aw] = (dusg * u * dsg).astype(BF16)

        dvn = dvn_s[...]
        vh = vh_s[...]
        glnw_ref[...] += jnp.sum(dvn * vh, axis=0, keepdims=True)
        glnb_ref[...] += jnp.sum(dvn, axis=0, keepdims=True)
        dvh = dvn * lnw_ref[...]
        dvg = rstd * (
            dvh - jnp.mean(dvh, axis=-1, keepdims=True) - vh * jnp.mean(dvh * vh, axis=-1, keepdims=True)
        )
        dz_ref[:, aw : 2 * aw] = (dvg * _gelu_grad(zv)).astype(BF16)

        dh = jnp.zeros((tm, d), F32)
        for k in range(n_sh):
            dh = dh + _dot_nt(dz_ref[:, k * s_cols : (k + 1) * s_cols], win_ref[k])
        x = x_ref[...]
        nw = nw_ref[...]
        _, xh, r = _rms_fwd(x, nw)
        dx, gnw = _rms_bwd(dh, xh, r, nw)
        gnw_ref[...] += gnw
        gx_ref[...] = dout + dx

    row = lambda i: (i, 0)
    return _pcall(
        body,
        name="layer_a_bwd",
        grid=(t_rows // tm,),
        in_specs=[
            pl.BlockSpec((tm, d), row),
            pl.BlockSpec((tm, d), row),
            pl.BlockSpec((tm, 3 * aw), row),
            _full(nw.shape),
            _full(win.shape),
            _full(ln_w.shape),
            _full(ln_b.shape),
            _full(wc.shape),
            _full(wct.shape),
            _full(bs_t.shape),
            _full(wout.shape),
        ],
        out_specs=[
            pl.BlockSpec((tm, d), row),
            pl.BlockSpec((tm, 3 * aw), row),
            pl.BlockSpec((tm, aw), row),
            pl.BlockSpec((tm, d), row),
            _full((A_GROUPS, CHUNK, CHUNK)),
            _full((CHUNK, A_GROUPS)),
            _full((1, aw)),
            _full((1, aw)),
            _full((1, d)),
        ],
        out_shape=[
            jax.ShapeDtypeStruct((t_rows, d), F32),
            jax.ShapeDtypeStruct((t_rows, 3 * aw), BF16),
            jax.ShapeDtypeStruct((t_rows, aw), BF16),
            jax.ShapeDtypeStruct((t_rows, d), BF16),
            jax.ShapeDtypeStruct((A_GROUPS, CHUNK, CHUNK), F32),
            jax.ShapeDtypeStruct((CHUNK, A_GROUPS), F32),
            jax.ShapeDtypeStruct((1, aw), F32),
            jax.ShapeDtypeStruct((1, aw), F32),
            jax.ShapeDtypeStruct((1, d), F32),
        ],
        scratch_shapes=[pltpu.VMEM((tm, aw), F32)] * 4,
        compiler_params=_cparams(("arbitrary",)),
    )(dout, x, z, nw, win, ln_w, ln_b, wc, wct, bs_t, wout)


def _gates(xc_h, gab_ref, gb_ref, sp_h, h, hd):
    pre = _dot(xc_h.astype(BF16), gab_ref[h])
    bw = gb_ref.shape[1] // 2
    r = _sigmoid(pre[:, :hd] + gb_ref[:, h * hd : (h + 1) * hd])
    ig = _sigmoid(pre[:, hd:] + gb_ref[:, bw + h * hd : bw + (h + 1) * hd])
    log_a = (-RG_C) * r * sp_h
    a = jnp.exp(log_a)
    mult = jnp.sqrt(jnp.tanh(-log_a) * (a * a + 1.0))
    return r, ig, a, mult


def _conv(xb, halo, cw_ref, cb_ref):
    xc = cb_ref[...] + cw_ref[CONV_WIDTH - 1 : CONV_WIDTH, :] * xb
    for k in range(CONV_WIDTH - 1):
        xc = xc + cw_ref[k : k + 1, :] * _shift_down(xb, halo, CONV_WIDTH - 1 - k)
    return xc


def _layer_b_fwd(x1, nw, bin_w, cw, cb, gab, gb, lam, bout, nf, tgt, tm):
    t_rows, d = x1.shape
    bw = bout.shape[0]
    hd = bw // B_HEADS
    nt = t_rows // tm

    def body(
        x1_ref, nw_ref, bin_ref, cw_ref, cb_ref, gab_ref, gb_ref, lam_ref, bout_ref, nf_ref, tgt_ref,
        z_ref, h_ref, h1_ref, xbt_ref, ht_ref, dx2_ref, loss_ref, gnf_ref,
        tail_s, carry_s, a_s, b_s, hs_s, acc_s,
    ):
        @pl.when(pl.program_id(0) == 0)
        def _():
            tail_s[...] = jnp.zeros_like(tail_s)
            carry_s[...] = jnp.zeros_like(carry_s)
            acc_s[...] = jnp.zeros_like(acc_s)
            gnf_ref[...] = jnp.zeros_like(gnf_ref)

        x1 = x1_ref[...]
        h1, _, _ = _rms_fwd(x1, nw_ref[...])
        h1 = h1.astype(BF16)
        h1_ref[...] = h1
        z = jnp.concatenate([_dot(h1, bin_ref[k]) for k in range(N_CHIPS)], axis=1)
        z_ref[...] = z
        xb = z[:, :bw]
        xc = _conv(xb, tail_s[...], cw_ref, cb_ref)
        tail = xb[tm - SUBLANES :, :]
        tail_s[...] = tail
        xbt_ref[...] = tail
        sp = _softplus_neg(lam_ref[...])
        for h in range(B_HEADS):
            cols = slice(h * hd, (h + 1) * hd)
            xc_h = xc[:, cols]
            _, ig, a, mult = _gates(xc_h, gab_ref, gb_ref, sp[:, cols], h, hd)
            a_s[:, cols] = a
            b_s[:, cols] = mult * (ig * xc_h)
        carry = _scan_blocks(a_s, b_s, hs_s, carry_s[...], tm, reverse=False)
        carry_s[...] = carry
        ht_ref[...] = hs_s[tm - SUBLANES :, :]
        hs = hs_s[...]
        h_ref[...] = hs
        g = z[:, bw:]
        y = (hs * (g * _sigmoid(g))).astype(BF16)
        x2 = x1 + _dot(y, bout_ref[...])

        nf = nf_ref[...]
        o, xh, r = _rms_fwd(x2, nf)
        diff = o - tgt_ref[...]
        acc_s[...] += jnp.sum(diff * diff, axis=0, keepdims=True)
        do = diff * (1.0 / d)
        dx2, gnf = _rms_bwd(do, xh, r, nf)
        gnf_ref[...] += gnf
        dx2_ref[...] = dx2

        @pl.when(pl.program_id(0) == nt - 1)
        def _():
            total = jnp.sum(acc_s[...], axis=-1, keepdims=True) * (0.5 / d)
            loss_ref[...] = jnp.broadcast_to(total, loss_ref.shape)

    row = lambda i: (i, 0)
    return _pcall(
        body,
        name="layer_b_fwd",
        grid=(nt,),
        in_specs=[
            pl.BlockSpec((tm, d), row),
            _full(nw.shape),
            _full(bin_w.shape),
            _full(cw.shape),
            _full(cb.shape),
            _full(gab.shape),
            _full(gb.shape),
            _full(lam.shape),
            _full(bout.shape),
            _full(nf.shape),
            pl.BlockSpec((tm, d), row),
        ],
        out_specs=[
            pl.BlockSpec((tm, 2 * bw), row),
            pl.BlockSpec((tm, bw), row),
            pl.BlockSpec((tm, d), row),
            pl.BlockSpec((None, SUBLANES, bw), lambda i: (i, 0, 0)),
            pl.BlockSpec((None, SUBLANES, bw), lambda i: (i, 0, 0)),
            pl.BlockSpec((tm, d), row),
            _full((1, LANES)),
            _full((1, d)),
        ],
        out_shape=[
            jax.ShapeDtypeStruct((t_rows, 2 * bw), F32),
            jax.ShapeDtypeStruct((t_rows, bw), F32),
            jax.ShapeDtypeStruct((t_rows, d), BF16),
            jax.ShapeDtypeStruct((nt, SUBLANES, bw), F32),
            jax.ShapeDtypeStruct((nt, SUBLANES, bw), F32),
            jax.ShapeDtypeStruct((t_rows, d), F32),
            jax.ShapeDtypeStruct((1, LANES), F32),
            jax.ShapeDtypeStruct((1, d), F32),
        ],
        scratch_shapes=[
            pltpu.VMEM((SUBLANES, bw), F32),
            pltpu.VMEM((SUBLANES, bw), F32),
            pltpu.VMEM((tm, bw), F32),
            pltpu.VMEM((tm, bw), F32),
            pltpu.VMEM((tm, bw), F32),
            pltpu.VMEM((1, d), F32),
        ],
        compiler_params=_cparams(("arbitrary",)),
    )(x1, nw, bin_w, cw, cb, gab, gb, lam, bout, nf, tgt)


def _layer_b_bwd(dout, x1, z, hseq, xb_tails, h_tails, nw, bin_w, cw, cb, gab, gabt, gb, lam, bout, tm):
    t_rows, d = x1.shape
    bw = bout.shape[0]
    hd = bw // B_HEADS
    nt = t_rows // tm

    def body(
        dout_ref, x1_ref, z_ref, h_ref, xbt_ref, ht_ref, nw_ref, bin_ref, cw_ref, cb_ref, gab_ref, gabt_ref,
        gb_ref, lam_ref, bout_ref,
        dx1_ref, dz_ref, y_ref, dob_ref, ggab_ref, ggb_ref, gcw_ref, gcb_ref, glam_ref, gnw_ref,
        gcarry_s, afirst_s, head_s, aup_s, dh_s, gt_s, dxc_s, xc_s,
    ):
        step = pl.program_id(0)
        tile = nt - 1 - step

        @pl.when(step == 0)
        def _():
            for ref in (ggab_ref, ggb_ref, gcw_ref, gcb_ref, glam_ref, gnw_ref, gcarry_s, afirst_s, head_s):
                ref[...] = jnp.zeros_like(ref)

        first_tile = tile == 0
        xb_halo = jnp.where(first_tile, 0.0, xbt_ref[...])
        h_halo = jnp.where(first_tile, 0.0, ht_ref[...])

        dout = dout_ref[...]
        dob = dout.astype(BF16)
        dob_ref[...] = dob
        dy = _dot_nt(dob, bout_ref[...])
        hs = h_ref[...]
        g = z_ref[:, bw:]
        sg, dsg = _silu_and_grad(g)
        y_ref[...] = (hs * sg).astype(BF16)
        dz_ref[:, bw:] = (dy * hs * dsg).astype(BF16)
        dh_s[...] = dy * sg

        xb = z_ref[:, :bw]
        xc = _conv(xb, xb_halo, cw_ref, cb_ref)
        xc_s[...] = xc
        lam = lam_ref[...]
        sp = _softplus_neg(lam)
        for h in range(B_HEADS):
            cols = slice(h * hd, (h + 1) * hd)
            _, _, a, _ = _gates(xc[:, cols], gab_ref, gb_ref, sp[:, cols], h, hd)
            aup_s[:, cols] = _shift_up(a, afirst_s[:, cols], 1)
            afirst_s[:, cols] = jnp.broadcast_to(a[0:1, :], (SUBLANES, hd))
        carry = _scan_blocks(aup_s, dh_s, gt_s, gcarry_s[...], tm, reverse=True)
        gcarry_s[...] = carry

        h_prev = _shift_down(hs, h_halo, 1)
        dsp = jnp.zeros((1, bw), F32)
        for h in range(B_HEADS):
            cols = slice(h * hd, (h + 1) * hd)
            xc_h = xc_s[:, cols]
            sp_h = sp[:, cols]
            r, ig, a, mult = _gates(xc_h, gab_ref, gb_ref, sp_h, h, hd)
            gt = gt_s[:, cols]
            da = gt * h_prev[:, cols]
            dmult = gt * (ig * xc_h)
            dig = gt * (mult * xc_h)
            dxc_direct = gt * (mult * ig)
            dla = da * a - dmult * (a * a) / mult
            glam_ref[:, cols] += jnp.sum(dla * r, axis=0, keepdims=True)
            dr = dla * ((-RG_C) * sp_h)
            dpre = jnp.concatenate([dr * r * (1.0 - r), dig * ig * (1.0 - ig)], axis=1)
            ggb_ref[:, cols] += jnp.sum(dpre[:, :hd], axis=0, keepdims=True)
            ggb_ref[:, bw + h * hd : bw + (h + 1) * hd] += jnp.sum(dpre[:, hd:], axis=0, keepdims=True)
            dpb = dpre.astype(BF16)
            ggab_ref[h] += _dot_tn(xc_h.astype(BF16), dpb)
            dxc_s[:, cols] = dxc_direct + _dot(dpb, gabt_ref[h])
        glam_ref[...] = jnp.where(step == nt - 1, glam_ref[...] * (RG_C * _sigmoid(-lam)), glam_ref[...])

        dxc = dxc_s[...]
        gcb_ref[...] += jnp.sum(dxc, axis=0, keepdims=True)
        dxb = cw_ref[CONV_WIDTH - 1 : CONV_WIDTH, :] * dxc
        gcw_ref[CONV_WIDTH - 1 : CONV_WIDTH, :] += jnp.sum(dxc * xb, axis=0, keepdims=True)
        head = head_s[...]
        for k in range(CONV_WIDTH - 1):
            lag = CONV_WIDTH - 1 - k
            dxb = dxb + cw_ref[k : k + 1, :] * _shift_up(dxc, head, lag)
            gcw_ref[k : k + 1, :] += jnp.sum(dxc * _shift_down(xb, xb_halo, lag), axis=0, keepdims=True)
        head_s[...] = dxc[:SUBLANES, :]
        dz_ref[:, :bw] = dxb.astype(BF16)

        s_cols = 2 * bw // N_CHIPS
        dh1 = jnp.zeros((tm, d), F32)
        for k in range(N_CHIPS):
            dh1 = dh1 + _dot_nt(dz_ref[:, k * s_cols : (k + 1) * s_cols], bin_ref[k])
        x1 = x1_ref[...]
        nw = nw_ref[...]
        _, xh, r1 = _rms_fwd(x1, nw)
        dx, gnw = _rms_bwd(dh1, xh, r1, nw)
        gnw_ref[...] += gnw
        dx1_ref[...] = dout + dx

    rev = lambda i: (nt - 1 - i, 0)
    prev = lambda i: (jnp.maximum(nt - 2 - i, 0), 0, 0)
    return _pcall(
        body,
        name="layer_b_bwd",
        grid=(nt,),
        in_specs=[
            pl.BlockSpec((tm, d), rev),
            pl.BlockSpec((tm, d), rev),
            pl.BlockSpec((tm, 2 * bw), rev),
            pl.BlockSpec((tm, bw), rev),
            pl.BlockSpec((None, SUBLANES, bw), prev),
            pl.BlockSpec((None, SUBLANES, bw), prev),
            _full(nw.shape),
            _full(bin_w.shape),
            _full(cw.shape),
            _full(cb.shape),
            _full(gab.shape),
            _full(gabt.shape),
            _full(gb.shape),
            _full(lam.shape),
            _full(bout.shape),
        ],
        out_specs=[
            pl.BlockSpec((tm, d), rev),
            pl.BlockSpec((tm, 2 * bw), rev),
            pl.BlockSpec((tm, bw), rev),
            pl.BlockSpec((tm, d), rev),
            _full((B_HEADS, hd, 2 * hd)),
            _full((1, 2 * bw)),
            _full((SUBLANES, bw)),
            _full((1, bw)),
            _full((1, bw)),
            _full((1, d)),
        ],
        out_shape=[
            jax.ShapeDtypeStruct((t_rows, d), F32),
            jax.ShapeDtypeStruct((t_rows, 2 * bw), BF16),
            jax.ShapeDtypeStruct((t_rows, bw), BF16),
            jax.ShapeDtypeStruct((t_rows, d), BF16),
            jax.ShapeDtypeStruct((B_HEADS, hd, 2 * hd), F32),
            jax.ShapeDtypeStruct((1, 2 * bw), F32),
            jax.ShapeDtypeStruct((SUBLANES, bw), F32),
            jax.ShapeDtypeStruct((1, bw), F32),
            jax.ShapeDtypeStruct((1, bw), F32),
            jax.ShapeDtypeStruct((1, d), F32),
        ],
        scratch_shapes=[pltpu.VMEM((SUBLANES, bw), F32)] * 3 + [pltpu.VMEM((tm, bw), F32)] * 5,
        compiler_params=_cparams(("arbitrary",)),
    )(dout, x1, z, hseq, xb_tails, h_tails, nw, bin_w, cw, cb, gab, gabt, gb, lam, bout)


def _wgrad(a, b, m_blocks, n_blocks):
    k, m = a.shape
    n = b.shape[1]
    bm, bn = m // m_blocks, n // n_blocks

    def body(a_ref, b_ref, o_ref):
        o_ref[...] = _dot_tn(a_ref[...], b_ref[...])

    return _pcall(
        body,
        name=f"wgrad_{m}x{n}",
        grid=(n_blocks, m_blocks),
        in_specs=[pl.BlockSpec((k, bm), lambda j, i: (0, i)), pl.BlockSpec((k, bn), lambda j, i: (0, j))],
        out_specs=pl.BlockSpec((None, None, bm, bn), lambda j, i: (j, i, 0, 0)),
        out_shape=jax.ShapeDtypeStruct((n_blocks, m_blocks, bm, bn), F32),
        compiler_params=_cparams(("arbitrary", "arbitrary")),
    )(a, b)


def _adamw_math(w, g, m, v):
    m = ADAM_B1 * m + (1.0 - ADAM_B1) * g
    v = ADAM_B2 * v + (1.0 - ADAM_B2) * (g * g)
    m_hat = m / (1.0 - ADAM_B1**ADAM_STEP)
    v_hat = v / (1.0 - ADAM_B2**ADAM_STEP)
    delta = -ADAM_LR * (m_hat / (jnp.sqrt(v_hat) + ADAM_EPS) + ADAM_WD * w)
    return delta, m, v


def _adamw(w, g, m, v):
    rows, cols = w.shape
    tr = _row_tile(rows, cols, 1024 * 1024)

    def body(w_ref, g_ref, m_ref, v_ref, d_ref, mo_ref, vo_ref):
        d_ref[...], mo_ref[...], vo_ref[...] = _adamw_math(w_ref[...], g_ref[...], m_ref[...], v_ref[...])

    spec = pl.BlockSpec((tr, cols), lambda i: (i, 0))
    return _pcall(
        body,
        name=f"adamw_{rows}x{cols}",
        grid=(rows // tr,),
        in_specs=[spec] * 4,
        out_specs=[spec] * 3,
        out_shape=[jax.ShapeDtypeStruct((rows, cols), F32)] * 3,
        compiler_params=_cparams(("arbitrary",)),
    )(w, g, m, v)


def _adamw_many(ws, gs, ms, vs):
    n = len(ws)

    def body(*refs):
        w_refs, g_refs, m_refs, v_refs = (refs[i * n : (i + 1) * n] for i in range(4))
        d_refs, mo_refs, vo_refs = (refs[(4 + i) * n : (5 + i) * n] for i in range(3))
        for i in range(n):
            d_refs[i][...], mo_refs[i][...], vo_refs[i][...] = _adamw_math(
                w_refs[i][...], g_refs[i][...], m_refs[i][...], v_refs[i][...]
            )

    vmem = pl.BlockSpec(memory_space=pltpu.VMEM)
    outs = _pcall(
        body,
        name="adamw_small",
        in_specs=[vmem] * (4 * n),
        out_specs=[vmem] * (3 * n),
        out_shape=[jax.ShapeDtypeStruct(w.shape, F32) for w in ws] * 3,
        compiler_params=_cparams(),
    )(*ws, *gs, *ms, *vs)
    return outs[:n], outs[n : 2 * n], outs[2 * n :]


def _pack_rows(parts, lanes=LANES):
    flat = jnp.concatenate([p.reshape(-1) for p in parts])
    per = N_DEV * SUBLANES * lanes
    total = -(-flat.shape[0] // per) * per
    flat = jnp.pad(flat, (0, total - flat.shape[0]))
    return flat.reshape(N_DEV, total // (N_DEV * lanes), lanes)


def _unpack(flat, shapes):
    out, at = [], 0
    for s in shapes:
        n = 1
        for dim in s:
            n *= dim
        out.append(flat[at : at + n].reshape(s))
        at += n
    return out


def kernel(x, norm_w, a_w_in, a_ln_w, a_ln_b, a_w_s, a_b_s, a_w_out, b_w_in, b_conv_w, b_conv_b, b_gate_a_w, b_gate_a_b, b_gate_x_w, b_gate_x_b, b_lambda, b_w_out, norm_f_w, loss_target, m_norm_w, m_a_w_in, m_a_ln_w, m_a_ln_b, m_a_w_s, m_a_b_s, m_a_w_out, m_b_w_in, m_b_conv_w, m_b_conv_b, m_b_gate_a_w, m_b_gate_a_b, m_b_gate_x_w, m_b_gate_x_b, m_b_lambda, m_b_w_out, m_norm_f_w, v_norm_w, v_a_w_in, v_a_ln_w, v_a_ln_b, v_a_w_s, v_a_b_s, v_a_w_out, v_b_w_in, v_b_conv_w, v_b_conv_b, v_b_gate_a_w, v_b_gate_a_b, v_b_gate_x_w, v_b_gate_x_b, v_b_lambda, v_b_w_out, v_norm_f_w):
    t_rows, d = x.shape[1], x.shape[2]
    aw = a_ln_w.shape[1]
    bw = b_gate_a_w.shape[1] * b_gate_a_w.shape[2]
    hd = bw // B_HEADS
    mine = 2 * lax.axis_index("x") + lax.axis_index("y")
    core = lax.axis_index("c")

    win_l = _cast_to_segments(a_w_in[0], mine, 256)
    wout_l = _cast_to_segments(a_w_out[0], mine, 256)
    bin_l = _cast_to_segments(b_w_in[0], mine, 256)
    bout_l = _cast_to_segments(b_w_out[0], mine, 192)
    small_l = jnp.concatenate([b_conv_w[0], b_conv_b, b_gate_a_b, b_gate_x_b, b_lambda], axis=0)
    (win_g, wout_g, bin_g, bout_g), small_g = _gather_weights([win_l, wout_l, bin_l, bout_l], small_l)
    win = win_g.reshape(N_CHIPS, d, -1)
    wout = wout_g.reshape(aw, d)
    bin_w = bin_g.reshape(N_CHIPS, d, -1)
    bout = bout_g.reshape(bw, d)
    small_f = jnp.transpose(small_g, (1, 0, 2)).reshape(SUBLANES, bw)
    cw, cb = small_f[0:CONV_WIDTH], small_f[CONV_WIDTH : CONV_WIDTH + 1]
    gb = jnp.concatenate([small_f[5:6], small_f[6:7]], axis=1)
    lam = small_f[7:8]

    tril = jnp.tril(jnp.ones((CHUNK, CHUNK), F32))
    wc = (a_w_s[0] * tril[None]).astype(BF16)
    wct = jnp.swapaxes(wc, 1, 2)
    bs_t = a_b_s[0].T
    gab = jnp.concatenate([b_gate_a_w[0], b_gate_x_w[0]], axis=2).astype(BF16)
    gabt = jnp.swapaxes(gab, 1, 2)
    nw0, nw1, nf = norm_w[0:1], norm_w[1:2], norm_f_w.reshape(1, d)

    x0 = x[0]
    z_a, x1, h0 = _layer_a_fwd(x0, nw0, win, a_ln_w, a_ln_b, wc, bs_t, wout, 256)
    z_b, hseq, h1, xb_tails, h_tails, dx2, loss_l, g_nf = _layer_b_fwd(
        x1, nw1, bin_w, cw, cb, gab, gb, lam, bout, nf, loss_target[0], 256
    )
    dx1, dz_b, y_b, dob_b, g_gab, g_gb, g_cw, g_cb, g_lam, g_nw1 = _layer_b_bwd(
        dx2, x1, z_b, hseq, xb_tails, h_tails, nw1, bin_w, cw, cb, gab, gabt, gb, lam, bout, 256
    )
    grad_x, dz_a, y_a, dob_a, g_ws, g_bst, g_lnw, g_lnb, g_nw0 = _layer_a_bwd(
        dx1, x0, z_a, nw0, win, a_ln_w, a_ln_b, wc, wct, bs_t, wout
    )
    g_bout = _wgrad(y_b, dob_b, N_CHIPS, 1)
    g_bin = _wgrad(h1, dz_b, 2, N_CHIPS)
    g_wout = _wgrad(y_a, dob_a, N_DEV, 1)
    g_win = _wgrad(h0, dz_a, 2, N_CHIPS)
    seg = lambda g: g.reshape(N_DEV, -1, g.shape[3])

    small_shapes = [
        (2, d), (1, aw), (1, aw), (A_GROUPS, CHUNK, CHUNK), (A_GROUPS, CHUNK), (B_HEADS, hd, hd), (B_HEADS, hd, hd),
        (d,), (CONV_WIDTH, bw), (1, bw), (1, bw), (1, bw), (1, bw),
    ]
    small = _pack_rows(
        [
            jnp.concatenate([g_nw0, g_nw1], axis=0), g_lnw, g_lnb, g_ws, g_bst.T, g_gab[:, :, :hd], g_gab[:, :, hd:],
            g_nf, g_cw[:CONV_WIDTH], g_cb, g_gb[:, :bw], g_gb[:, bw:], g_lam,
        ]
    )
    big = [seg(g_win), seg(g_wout), seg(g_bin), seg(g_bout)]
    (gr_win, gr_wout, gr_bin, gr_bout), small_r = _reduce_to_owners(big, small, mine, core)
    gr_win = gr_win.reshape(a_w_in.shape[1:])
    gr_wout = gr_wout.reshape(a_w_out.shape[1:])
    gr_bin = gr_bin.reshape(b_w_in.shape[1:])
    gr_bout = gr_bout.reshape(b_w_out.shape[1:])
    (g_norm_w, g_a_ln_w, g_a_ln_b, g_a_w_s, g_a_b_s, g_gate_a_w, g_gate_x_w, g_norm_f, gf_cw, gf_cb, gf_gab, gf_gxb,
     gf_lam) = _unpack(small_r.reshape(-1), small_shapes)
    shard = lambda g: lax.dynamic_slice_in_dim(g, mine * (bw // N_CHIPS), bw // N_CHIPS, axis=1)

    grads = {
        "norm_w": g_norm_w, "a_w_in": gr_win[None], "a_ln_w": g_a_ln_w, "a_ln_b": g_a_ln_b, "a_w_s": g_a_w_s[None],
        "a_b_s": g_a_b_s[None], "a_w_out": gr_wout[None], "b_w_in": gr_bin[None], "b_conv_w": shard(gf_cw)[None],
        "b_conv_b": shard(gf_cb), "b_gate_a_w": g_gate_a_w[None], "b_gate_a_b": shard(gf_gab),
        "b_gate_x_w": g_gate_x_w[None], "b_gate_x_b": shard(gf_gxb), "b_lambda": shard(gf_lam),
        "b_w_out": gr_bout[None], "norm_f_w": g_norm_f,
    }
    weights = dict(norm_w=norm_w, a_w_in=a_w_in, a_ln_w=a_ln_w, a_ln_b=a_ln_b, a_w_s=a_w_s, a_b_s=a_b_s, a_w_out=a_w_out, b_w_in=b_w_in, b_conv_w=b_conv_w, b_conv_b=b_conv_b, b_gate_a_w=b_gate_a_w, b_gate_a_b=b_gate_a_b, b_gate_x_w=b_gate_x_w, b_gate_x_b=b_gate_x_b, b_lambda=b_lambda, b_w_out=b_w_out, norm_f_w=norm_f_w)
    m_in = dict(norm_w=m_norm_w, a_w_in=m_a_w_in, a_ln_w=m_a_ln_w, a_ln_b=m_a_ln_b, a_w_s=m_a_w_s, a_b_s=m_a_b_s, a_w_out=m_a_w_out, b_w_in=m_b_w_in, b_conv_w=m_b_conv_w, b_conv_b=m_b_conv_b, b_gate_a_w=m_b_gate_a_w, b_gate_a_b=m_b_gate_a_b, b_gate_x_w=m_b_gate_x_w, b_gate_x_b=m_b_gate_x_b, b_lambda=m_b_lambda, b_w_out=m_b_w_out, norm_f_w=m_norm_f_w)
    v_in = dict(norm_w=v_norm_w, a_w_in=v_a_w_in, a_ln_w=v_a_ln_w, a_ln_b=v_a_ln_b, a_w_s=v_a_w_s, a_b_s=v_a_b_s, a_w_out=v_a_w_out, b_w_in=v_b_w_in, b_conv_w=v_b_conv_w, b_conv_b=v_b_conv_b, b_gate_a_w=v_b_gate_a_w, b_gate_a_b=v_b_gate_a_b, b_gate_x_w=v_b_gate_x_w, b_gate_x_b=v_b_gate_x_b, b_lambda=v_b_lambda, b_w_out=v_b_w_out, norm_f_w=v_norm_f_w)
    names = list(weights)
    big_names = ("a_w_in", "a_w_out", "b_w_in", "b_w_out")
    delta, new_m, new_v = {}, {}, {}
    for n in big_names:
        shape = weights[n].shape
        two_d = lambda a: a.reshape(shape[1:])
        dl, mo, vo = _adamw(two_d(weights[n]), two_d(grads[n]), two_d(m_in[n]), two_d(v_in[n]))
        delta[n], new_m[n], new_v[n] = dl.reshape(shape), mo.reshape(shape), vo.reshape(shape)
    small_names = [n for n in names if n not in big_names]
    at_least_2d = lambda a: a.reshape(1, -1) if a.ndim == 1 else a
    outs = _adamw_many(*[[at_least_2d(src[n]) for n in small_names] for src in (weights, grads, m_in, v_in)])
    for dst, vals in zip((delta, new_m, new_v), outs):
        for n, val in zip(small_names, vals):
            dst[n] = val.reshape(weights[n].shape)

    loss = lax.psum(loss_l[0, 0], ("x", "y", "c"))
    return (
        loss,
        grad_x[None],
        *[grads[n] for n in names],
        *[delta[n] for n in names],
        *[new_m[n] for n in names],
        *[new_v[n] for n in names],
    )
```

```python
import jax
import jax.numpy as jnp
from jax import lax
from jax.experimental import pallas as pl
from jax.experimental.pallas import tpu as pltpu

F32 = jnp.float32
BF16 = jnp.bfloat16

RMS_EPS = 1e-6
LN_EPS = 1e-5
RG_C = 8.0
CHUNK = 128
A_GROUPS = 8
B_HEADS = 12
CONV_WIDTH = 4

ADAM_LR = 0.001
ADAM_B1 = 0.9
ADAM_B2 = 0.999
ADAM_EPS = 1e-08
ADAM_WD = 0.01
ADAM_STEP = 10

N_CHIPS = 4
N_DEV = 8
SUBLANES = 8
LANES = 128
V7X_VMEM_BYTES = 64 * 1024 * 1024
VMEM_LIMIT = V7X_VMEM_BYTES * 7 // 8
MESH = pl.DeviceIdType.MESH
ANY = pl.BlockSpec(memory_space=pl.ANY)

TM_FWD = 256
TM_A_BWD = 128

GELU_C0 = 0.7978845608028654
GELU_C1 = 0.044715


class _Hook:
    def __init__(self, operands, out_shapes, aliases, n_sems, start, finish):
        self.operands, self.out_shapes, self.aliases, self.n_sems = operands, out_shapes, aliases, n_sems
        self.start, self.finish = start, finish


def _pcall(body, hook=None, **kw):
    if hook is None:
        return pl.pallas_call(body, **kw)
    n_in, n_out = len(kw["in_specs"]), len(kw["out_shape"])
    hi, ho = len(hook.operands), len(hook.out_shapes)
    grid = kw.get("grid", ())

    def wrapped(*refs):
        ins, h_in = refs[:n_in], refs[n_in : n_in + hi]
        outs = refs[n_in + hi : n_in + hi + n_out]
        h_out = refs[n_in + hi + n_out : n_in + hi + n_out + ho]
        scratch = refs[n_in + hi + n_out + ho : -2]
        send_sems, recv_sems = refs[-2:]
        if not grid:
            hook.start(h_in, h_out, send_sems, recv_sems)
            body(*ins, *outs, *scratch)
            hook.finish(h_in, h_out, send_sems, recv_sems)
            return
        first = pl.program_id(0) == 0
        last = pl.program_id(0) == grid[0] - 1
        for axis in range(1, len(grid)):
            first = jnp.logical_and(first, pl.program_id(axis) == 0)
            last = jnp.logical_and(last, pl.program_id(axis) == grid[axis] - 1)

        @pl.when(first)
        def _():
            hook.start(h_in, h_out, send_sems, recv_sems)

        body(*ins, *outs, *scratch)

        @pl.when(last)
        def _():
            hook.finish(h_in, h_out, send_sems, recv_sems)

    aliases = dict(kw.pop("input_output_aliases", {}))
    aliases.update({n_in + i: n_out + o for i, o in hook.aliases.items()})
    kw.update(
        in_specs=list(kw["in_specs"]) + [ANY] * hi,
        out_specs=list(kw["out_specs"]) + [ANY] * ho,
        out_shape=list(kw["out_shape"]) + list(hook.out_shapes),
        scratch_shapes=list(kw.get("scratch_shapes", ()))
        + [pltpu.SemaphoreType.DMA((hook.n_sems,)), pltpu.SemaphoreType.DMA((hook.n_sems,))],
        input_output_aliases=aliases,
    )
    call = pl.pallas_call(wrapped, **kw)

    def run(*operands):
        outs = call(*operands, *hook.operands)
        return outs[:n_out], outs[n_out:]

    return run


def _run_hook(hook, name):
    def body():
        pass

    return _pcall(body, hook, name=name, in_specs=[], out_specs=[], out_shape=[])()[1]


def _cparams(sem=None):
    return pltpu.CompilerParams(dimension_semantics=sem, vmem_limit_bytes=VMEM_LIMIT)


def _full(shape):
    zeros = (0,) * len(shape)
    return pl.BlockSpec(shape, lambda *_: zeros)


def _scalars(*vals):
    return jnp.stack([jnp.asarray(v, jnp.int32) for v in vals])


def _sigmoid(x):
    return 1.0 / (1.0 + jnp.exp(-x))


def _gelu(x):
    t = jnp.tanh(GELU_C0 * (x + GELU_C1 * (x * x * x)))
    return x * (0.5 * (1.0 + t))


def _gelu_grad(x):
    x2 = x * x
    t = jnp.tanh(GELU_C0 * (x + GELU_C1 * (x2 * x)))
    return 0.5 * (1.0 + t) + 0.5 * x * (1.0 - t * t) * (GELU_C0 * (1.0 + 3.0 * GELU_C1 * x2))


def _silu_and_grad(x):
    s = _sigmoid(x)
    return x * s, s * (1.0 + x * (1.0 - s))


def _softplus_neg(lam):
    u = jnp.exp(-jnp.abs(lam))
    w = 1.0 + u
    log1p = jnp.where(w == 1.0, u, jnp.log(w) * (u / jnp.where(w == 1.0, 1.0, w - 1.0)))
    return jnp.maximum(-lam, 0.0) + log1p


def _dot(a, b):
    return jnp.dot(a, b, preferred_element_type=F32)


def _dot_nt(a, b):
    return lax.dot_general(a, b, (((1,), (1,)), ((), ())), preferred_element_type=F32)


def _dot_tn(a, b):
    return lax.dot_general(a, b, (((0,), (0,)), ((), ())), preferred_element_type=F32)


def _shift_down(v, halo, k):
    if k == 0:
        return v
    rolled = pltpu.roll(v, k, 0)
    row = lax.broadcasted_iota(jnp.int32, (SUBLANES, v.shape[1]), 0)
    top = jnp.where(row < k, pltpu.roll(halo, k, 0), rolled[:SUBLANES])
    return jnp.concatenate([top, rolled[SUBLANES:]], axis=0)


def _shift_up(v, head, k):
    if k == 0:
        return v
    n = v.shape[0]
    rolled = pltpu.roll(v, n - k, 0)
    row = lax.broadcasted_iota(jnp.int32, (SUBLANES, v.shape[1]), 0)
    bot = jnp.where(row >= SUBLANES - k, pltpu.roll(head, SUBLANES - k, 0), rolled[n - SUBLANES :])
    return jnp.concatenate([rolled[: n - SUBLANES], bot], axis=0)


def _scan_blocks(a_ref, b_ref, out_ref, carry, n_rows, reverse):
    width = a_ref.shape[1]
    row = lax.broadcasted_iota(jnp.int32, (SUBLANES, width), 0)
    n_blocks = n_rows // SUBLANES

    def block(j, carry):
        i = (n_blocks - 1 - j) if reverse else j
        r0 = pl.multiple_of(i * SUBLANES, SUBLANES)
        a = a_ref[pl.ds(r0, SUBLANES), :]
        b = b_ref[pl.ds(r0, SUBLANES), :]
        for d in (1, 2, 4):
            shift = (SUBLANES - d) if reverse else d
            keep = (row < SUBLANES - d) if reverse else (row >= d)
            a_s = pltpu.roll(a, shift, 0)
            b_s = pltpu.roll(b, shift, 0)
            b = jnp.where(keep, a * b_s + b, b)
            a = jnp.where(keep, a * a_s, a)
        h = a * carry + b
        out_ref[pl.ds(r0, SUBLANES), :] = h
        edge = h[0:1, :] if reverse else h[SUBLANES - 1 : SUBLANES, :]
        return jnp.broadcast_to(edge, (SUBLANES, width))

    return lax.fori_loop(0, n_blocks, block, carry)


def _rms_fwd(x, w):
    r = lax.rsqrt(jnp.mean(x * x, axis=-1, keepdims=True) + RMS_EPS)
    xh = x * r
    return xh * w, xh, r


def _rms_bwd(dh, xh, r, w):
    dxh = dh * w
    dx = r * (dxh - xh * jnp.mean(dxh * xh, axis=-1, keepdims=True))
    return dx, jnp.sum(dh * xh, axis=0, keepdims=True)


def _cast_to_segments(w, mine, rows):
    n, c = w.shape
    per = n // 2 // rows

    def body(k_ref, w_ref, o_ref):
        o_ref[...] = w_ref[...].astype(BF16)

    return _pcall(
        body,
        name=f"cast_{n}x{c}",
        grid_spec=pltpu.PrefetchScalarGridSpec(
            num_scalar_prefetch=1,
            grid=(n // rows,),
            in_specs=[pl.BlockSpec((rows, c), lambda i, k_ref: (i, 0))],
            out_specs=pl.BlockSpec((None, rows, c), lambda i, k_ref: (2 * k_ref[0] + i // per, i % per, 0)),
        ),
        out_shape=jax.ShapeDtypeStruct((N_DEV, n // 2, c), BF16),
        compiler_params=_cparams(("arbitrary",)),
    )(_scalars(mine), w)


def _place():
    x, y, c = lax.axis_index("x"), lax.axis_index("y"), lax.axis_index("c")
    chips = [(1 - x, y), (x, 1 - y), (1 - x, 1 - y)]
    return x, y, c, chips


def _chip_no(chip):
    return 2 * chip[0] + chip[1]


def _rcopy(src, dst, send_sem, recv_sem, to):
    return pltpu.make_async_remote_copy(
        src_ref=src, dst_ref=dst, send_sem=send_sem, recv_sem=recv_sem, device_id=to, device_id_type=MESH
    )


def _gather_hook(big, small=None):
    nb = len(big)
    n_sems = 6 * nb + 4

    def ici_copies(ins, outs, send, recv):
        x, y, c, chips = _place()
        copies = []
        for b in range(nb):
            own = outs[b].at[2 * _chip_no((x, y)) + c]
            for j, chip in enumerate(chips):
                copies.append(_rcopy(own, own, send.at[6 * b + j], recv.at[6 * b + j], (*chip, c)))
        if small is not None:
            there = outs[nb].at[_chip_no((x, y))]
            for j, chip in enumerate(chips):
                k = 6 * nb + j
                copies.append(_rcopy(ins[nb], there, send.at[k], recv.at[k], (*chip, c)))
        return copies

    def local_copy(ins, outs, send):
        x, y, _, _ = _place()
        return pltpu.make_async_copy(ins[nb], outs[nb].at[_chip_no((x, y))], send.at[6 * nb + 3])

    def start(ins, outs, send, recv):
        for cp in ici_copies(ins, outs, send, recv):
            cp.start()
        if small is not None:
            local_copy(ins, outs, send).start()

    def finish(ins, outs, send, recv):
        x, y, c, chips = _place()
        me, sibling = (x, y, c), (x, y, 1 - c)
        passed = []
        for b in range(nb):
            for j, chip in enumerate(chips):
                got = outs[b].at[2 * _chip_no(chip) + c]
                _rcopy(got, got, send.at[6 * b + j], recv.at[6 * b + j], me).wait_recv()
                fwd = _rcopy(got, got, send.at[6 * b + 3 + j], recv.at[6 * b + 3 + j], sibling)
                fwd.start()
                passed.append(fwd)
        if small is not None:
            for j, chip in enumerate(chips):
                k = 6 * nb + j
                got = outs[nb].at[_chip_no(chip)]
                _rcopy(got, got, send.at[k], recv.at[k], me).wait_recv()
        for b in range(nb):
            for j, chip in enumerate(chips):
                got = outs[b].at[2 * _chip_no(chip) + 1 - c]
                _rcopy(got, got, send.at[6 * b + 3 + j], recv.at[6 * b + 3 + j], me).wait_recv()
        for cp in ici_copies(ins, outs, send, recv) + passed:
            cp.wait_send()
        if small is not None:
            local_copy(ins, outs, send).wait()

    operands = list(big) + ([small] if small is not None else [])
    out_shapes = [jax.ShapeDtypeStruct(b.shape, b.dtype) for b in big]
    if small is not None:
        out_shapes.append(jax.ShapeDtypeStruct((N_CHIPS, *small.shape), small.dtype))
    return _Hook(operands, out_shapes, {b: b for b in range(nb)}, n_sems, start, finish)


def _both_ways_hook(operands, out_shapes, copies_of, n_sems):
    def start(ins, outs, send, recv):
        for cp in copies_of(ins, outs, send, recv):
            cp.start()

    def finish(ins, outs, send, recv):
        for cp in copies_of(ins, outs, send, recv):
            cp.wait()

    return _Hook(operands, out_shapes, {}, n_sems, start, finish)


def _swap_hook(bufs):
    def copies_of(ins, outs, send, recv):
        x, y, c, _ = _place()
        copies = []
        for b in range(len(bufs)):
            for j in range(N_CHIPS):
                k = b * N_CHIPS + j
                copies.append(_rcopy(ins[b].at[2 * j + 1 - c], outs[b].at[j], send.at[k], recv.at[k], (x, y, 1 - c)))
        return copies

    out_shapes = [jax.ShapeDtypeStruct((N_CHIPS, *b.shape[1:]), b.dtype) for b in bufs]
    return _both_ways_hook(list(bufs), out_shapes, copies_of, len(bufs) * N_CHIPS)


def _send_hook(parts):
    def copies_of(ins, outs, send, recv):
        _, _, c, chips = _place()
        copies = []
        for b in range(len(parts)):
            for j, chip in enumerate(chips):
                k = b * 3 + j
                copies.append(_rcopy(ins[b].at[_chip_no(chip)], outs[b].at[j], send.at[k], recv.at[k], (*chip, c)))
        return copies

    out_shapes = [jax.ShapeDtypeStruct((3, *p.shape[1:]), p.dtype) for p in parts]
    return _both_ways_hook(list(parts), out_shapes, copies_of, len(parts) * 3)


def _share_hook(big, small=None):
    nb = len(big)
    n_sems = nb + 7

    def first_copies(outs, send, recv):
        x, y, c, chips = _place()
        sibling = (x, y, 1 - c)
        copies = [_rcopy(outs[b].at[c], outs[b].at[c], send.at[b], recv.at[b], sibling) for b in range(nb)]
        if small is not None:
            own = outs[nb].at[2 * _chip_no((x, y)) + c]
            copies.append(_rcopy(own, own, send.at[nb], recv.at[nb], sibling))
            for j, chip in enumerate(chips):
                copies.append(_rcopy(own, own, send.at[nb + 1 + j], recv.at[nb + 1 + j], (*chip, c)))
        return copies

    def start(ins, outs, send, recv):
        for cp in first_copies(outs, send, recv):
            cp.start()

    def finish(ins, outs, send, recv):
        x, y, c, chips = _place()
        me, sibling = (x, y, c), (x, y, 1 - c)
        passed = []
        if small is not None:
            for j, chip in enumerate(chips):
                got = outs[nb].at[2 * _chip_no(chip) + c]
                _rcopy(got, got, send.at[nb + 1 + j], recv.at[nb + 1 + j], me).wait_recv()
                fwd = _rcopy(got, got, send.at[nb + 4 + j], recv.at[nb + 4 + j], sibling)
                fwd.start()
                passed.append(fwd)
        for b in range(nb):
            got = outs[b].at[1 - c]
            _rcopy(got, got, send.at[b], recv.at[b], me).wait_recv()
        if small is not None:
            got = outs[nb].at[2 * _chip_no((x, y)) + 1 - c]
            _rcopy(got, got, send.at[nb], recv.at[nb], me).wait_recv()
            for j, chip in enumerate(chips):
                got = outs[nb].at[2 * _chip_no(chip) + 1 - c]
                _rcopy(got, got, send.at[nb + 4 + j], recv.at[nb + 4 + j], me).wait_recv()
        for cp in first_copies(outs, send, recv) + passed:
            cp.wait_send()

    operands = list(big) + ([small] if small is not None else [])
    out_shapes = [jax.ShapeDtypeStruct(a.shape, a.dtype) for a in operands]
    return _Hook(operands, out_shapes, {i: i for i in range(len(operands))}, n_sems, start, finish)


def _row_tile(rows, cols, target_bytes=2 * 1024 * 1024):
    best = SUBLANES
    for t in range(SUBLANES, rows + 1, SUBLANES):
        if rows % t == 0 and t * cols * 4 <= target_bytes:
            best = t
    return best


def _add_own_half(buf, got, c, wire):
    _, rows, cols = buf.shape
    tr = _row_tile(rows, cols)

    def body(c_ref, a_ref, b_ref, o_ref):
        o_ref[...] = (a_ref[...] + b_ref[...]).astype(wire)

    return _pcall(
        body,
        name=f"add_own_half_{rows}x{cols}",
        grid_spec=pltpu.PrefetchScalarGridSpec(
            num_scalar_prefetch=1,
            grid=(N_CHIPS, rows // tr),
            in_specs=[
                pl.BlockSpec((None, None, tr, cols), lambda j, r, c_ref: (j, c_ref[0], r, 0)),
                pl.BlockSpec((None, tr, cols), lambda j, r, c_ref: (j, r, 0)),
            ],
            out_specs=pl.BlockSpec((None, tr, cols), lambda j, r, c_ref: (j, r, 0)),
        ),
        out_shape=jax.ShapeDtypeStruct((N_CHIPS, rows, cols), wire),
        compiler_params=_cparams(("arbitrary", "arbitrary")),
    )(_scalars(c), buf.reshape(N_CHIPS, 2, rows, cols), got)


def _add_received(buf, got_a, got_b, mine, c, slot, n_slots):
    _, rows, cols = buf.shape
    tr = _row_tile(rows, cols)

    def body(s_ref, x_ref, a_ref, g_ref, o_ref):
        own = x_ref[...] + a_ref[...]
        o_ref[...] = ((own + g_ref[0].astype(F32)) + g_ref[1].astype(F32)) + g_ref[2].astype(F32)

    return _pcall(
        body,
        name=f"add_received_{rows}x{cols}",
        grid_spec=pltpu.PrefetchScalarGridSpec(
            num_scalar_prefetch=1,
            grid=(rows // tr,),
            in_specs=[
                pl.BlockSpec((None, None, tr, cols), lambda r, s_ref: (s_ref[0], s_ref[1], r, 0)),
                pl.BlockSpec((None, tr, cols), lambda r, s_ref: (s_ref[0], r, 0)),
                pl.BlockSpec((3, tr, cols), lambda r, s_ref: (0, r, 0)),
            ],
            out_specs=pl.BlockSpec((None, tr, cols), lambda r, s_ref: (s_ref[2], r, 0)),
        ),
        out_shape=jax.ShapeDtypeStruct((n_slots, rows, cols), F32),
        compiler_params=_cparams(("arbitrary",)),
    )(_scalars(mine, c, slot), buf.reshape(N_CHIPS, 2, rows, cols), got_a, got_b)


def _layer_a_fwd(x, nw, win, ln_w, ln_b, wc, bs_t, wout, tm, hook):
    t_rows, d = x.shape
    n_sh, _, s_cols = win.shape
    aw = wout.shape[0]
    gd = aw // A_GROUPS
    tn = 512
    assert s_cols % tn == 0 and aw % tn == 0 and tm % CHUNK == 0

    def body(x_ref, nw_ref, win_ref, lnw_ref, lnb_ref, wc_ref, bst_ref, wout_ref, z_ref, x1_ref, h_ref, u_s, v_s, y_s):
        x = x_ref[...]
        h, _, _ = _rms_fwd(x, nw_ref[...])
        h = h.astype(BF16)
        h_ref[...] = h
        for j in range(3 * aw // tn):
            k, off = divmod(j * tn, s_cols)
            cols = slice((j * tn) % aw, (j * tn) % aw + tn)
            zj = _dot(h, win_ref[k, :, off : off + tn])
            z_ref[:, j * tn : (j + 1) * tn] = zj
            if j * tn < aw:
                u_s[:, cols] = _gelu(zj)
            elif j * tn < 2 * aw:
                v_s[:, cols] = _gelu(zj)
            else:
                u_s[:, cols] = u_s[:, cols] * (zj * _sigmoid(zj))
        v = v_s[...]
        mu = jnp.mean(v, axis=-1, keepdims=True)
        vc = v - mu
        rstd = lax.rsqrt(jnp.mean(vc * vc, axis=-1, keepdims=True) + LN_EPS)
        v_s[...] = (vc * rstd) * lnw_ref[...] + lnb_ref[...]
        for ck in range(tm // CHUNK):
            rows = slice(ck * CHUNK, (ck + 1) * CHUNK)
            for g in range(A_GROUPS):
                cols = slice(g * gd, (g + 1) * gd)
                s = _dot(wc_ref[g], v_s[rows, cols].astype(BF16)) + bst_ref[:, g : g + 1]
                y_s[rows, cols] = (u_s[rows, cols] * s).astype(BF16)
        x1_ref[...] = x + _dot(y_s[...], wout_ref[...])

    row = lambda i: (i, 0)
    return _pcall(
        body,
        hook,
        name="layer_a_fwd",
        grid=(t_rows // tm,),
        in_specs=[
            pl.BlockSpec((tm, d), row),
            _full(nw.shape),
            _full(win.shape),
            _full(ln_w.shape),
            _full(ln_b.shape),
            _full(wc.shape),
            _full(bs_t.shape),
            _full(wout.shape),
        ],
        out_specs=[pl.BlockSpec((tm, 3 * aw), row), pl.BlockSpec((tm, d), row), pl.BlockSpec((tm, d), row)],
        out_shape=[
            jax.ShapeDtypeStruct((t_rows, 3 * aw), F32),
            jax.ShapeDtypeStruct((t_rows, d), F32),
            jax.ShapeDtypeStruct((t_rows, d), BF16),
        ],
        scratch_shapes=[pltpu.VMEM((tm, aw), F32), pltpu.VMEM((tm, aw), F32), pltpu.VMEM((tm, aw), BF16)],
        compiler_params=_cparams(("arbitrary",)),
    )(x, nw, win, ln_w, ln_b, wc, bs_t, wout)


def _layer_a_bwd(dout, x, z, nw, win, ln_w, ln_b, wc, wct, bs_t, wout, tiles, earlier, hook):
    t_rows, d = x.shape
    n_sh, _, s_cols = win.shape
    aw = wout.shape[0]
    gd = aw // A_GROUPS
    tm = TM_A_BWD
    lo, hi = tiles
    n_earlier = 0 if earlier is None else len(earlier)

    def body(dout_ref, x_ref, z_ref, nw_ref, win_ref, lnw_ref, lnb_ref, wc_ref, wct_ref, bst_ref, wout_ref, *rest):
        rest = rest[n_earlier:]
        gx_ref, dz_ref, y_ref, dob_ref, gws_ref, gbs_ref, glnw_ref, glnb_ref, gnw_ref, u_s, vh_s, ds_s, dvn_s = rest

        @pl.when(pl.program_id(0) == 0)
        def _():
            gws_ref[...] = jnp.zeros_like(gws_ref)
            gbs_ref[...] = jnp.zeros_like(gbs_ref)
            glnw_ref[...] = jnp.zeros_like(glnw_ref)
            glnb_ref[...] = jnp.zeros_like(glnb_ref)
            gnw_ref[...] = jnp.zeros_like(gnw_ref)

        dout = dout_ref[...]
        dob = dout.astype(BF16)
        dob_ref[...] = dob
        dy = _dot_nt(dob, wout_ref[...])

        zv = z_ref[:, aw : 2 * aw]
        vg = _gelu(zv)
        mu = jnp.mean(vg, axis=-1, keepdims=True)
        vc = vg - mu
        rstd = lax.rsqrt(jnp.mean(vc * vc, axis=-1, keepdims=True) + LN_EPS)
        vh = vc * rstd
        vh_s[...] = vh
        vn = (vh * lnw_ref[...] + lnb_ref[...]).astype(BF16)

        zu = z_ref[:, 0:aw]
        zg = z_ref[:, 2 * aw : 3 * aw]
        u = _gelu(zu)
        sg, dsg = _silu_and_grad(zg)
        u_s[...] = u * sg
        tril = lax.broadcasted_iota(jnp.int32, (CHUNK, CHUNK), 0) >= lax.broadcasted_iota(jnp.int32, (CHUNK, CHUNK), 1)
        for g in range(A_GROUPS):
            cols = slice(g * gd, (g + 1) * gd)
            vn_g = vn[:, cols]
            s = _dot(wc_ref[g], vn_g) + bst_ref[:, g : g + 1]
            usg = u_s[:, cols]
            dy_g = dy[:, cols]
            y_ref[:, cols] = (usg * s).astype(BF16)
            ds = dy_g * usg
            ds_s[:, cols] = dy_g * s
            gbs_ref[:, g : g + 1] += jnp.sum(ds, axis=-1, keepdims=True)
            dsb = ds.astype(BF16)
            gws_ref[g] += jnp.where(tril, _dot_nt(dsb, vn_g), 0.0)
            dvn_s[:, cols] = _dot(wct_ref[g], dsb)
        dusg = ds_s[...]
        dz_ref[:, 0:aw] = (dusg * sg * _gelu_grad(zu)).astype(BF16)
        dz_ref[:, 2 * aw : 3 * aw] = (dusg * u * dsg).astype(BF16)

        dvn = dvn_s[...]
        vh = vh_s[...]
        glnw_ref[...] += jnp.sum(dvn * vh, axis=0, keepdims=True)
        glnb_ref[...] += jnp.sum(dvn, axis=0, keepdims=True)
        dvh = dvn * lnw_ref[...]
        dvg = rstd * (dvh - jnp.mean(dvh, axis=-1, keepdims=True) - vh * jnp.mean(dvh * vh, axis=-1, keepdims=True))
        dz_ref[:, aw : 2 * aw] = (dvg * _gelu_grad(zv)).astype(BF16)

        dh = jnp.zeros((tm, d), F32)
        for k in range(n_sh):
            dh = dh + _dot_nt(dz_ref[:, k * s_cols : (k + 1) * s_cols], win_ref[k])
        x = x_ref[...]
        nw = nw_ref[...]
        _, xh, r = _rms_fwd(x, nw)
        dx, gnw = _rms_bwd(dh, xh, r, nw)
        gnw_ref[...] += gnw
        gx_ref[...] = dout + dx

    row = lambda i: (i + lo, 0)
    call = _pcall(
        body,
        hook,
        name=f"layer_a_bwd_{lo}",
        grid=(hi - lo,),
        in_specs=[
            pl.BlockSpec((tm, d), row),
            pl.BlockSpec((tm, d), row),
            pl.BlockSpec((tm, 3 * aw), row),
            _full(nw.shape),
            _full(win.shape),
            _full(ln_w.shape),
            _full(ln_b.shape),
            _full(wc.shape),
            _full(wct.shape),
            _full(bs_t.shape),
            _full(wout.shape),
        ]
        + [ANY] * n_earlier,
        out_specs=[
            pl.BlockSpec((tm, d), row),
            pl.BlockSpec((tm, 3 * aw), row),
            pl.BlockSpec((tm, aw), row),
            pl.BlockSpec((tm, d), row),
            _full((A_GROUPS, CHUNK, CHUNK)),
            _full((CHUNK, A_GROUPS)),
            _full((1, aw)),
            _full((1, aw)),
            _full((1, d)),
        ],
        out_shape=[
            jax.ShapeDtypeStruct((t_rows, d), F32),
            jax.ShapeDtypeStruct((t_rows, 3 * aw), BF16),
            jax.ShapeDtypeStruct((t_rows, aw), BF16),
            jax.ShapeDtypeStruct((t_rows, d), BF16),
            jax.ShapeDtypeStruct((A_GROUPS, CHUNK, CHUNK), F32),
            jax.ShapeDtypeStruct((CHUNK, A_GROUPS), F32),
            jax.ShapeDtypeStruct((1, aw), F32),
            jax.ShapeDtypeStruct((1, aw), F32),
            jax.ShapeDtypeStruct((1, d), F32),
        ],
        scratch_shapes=[pltpu.VMEM((tm, aw), F32)] * 4,
        input_output_aliases={11 + i: i for i in range(n_earlier)},
        compiler_params=_cparams(("arbitrary",)),
    )
    return call(dout, x, z, nw, win, ln_w, ln_b, wc, wct, bs_t, wout, *(earlier or ()))


def _gates(xc_h, gab_ref, gb_ref, sp_h, h, hd):
    pre = _dot(xc_h.astype(BF16), gab_ref[h])
    bw = gb_ref.shape[1] // 2
    r = _sigmoid(pre[:, :hd] + gb_ref[:, h * hd : (h + 1) * hd])
    ig = _sigmoid(pre[:, hd:] + gb_ref[:, bw + h * hd : bw + (h + 1) * hd])
    log_a = (-RG_C) * r * sp_h
    a = jnp.exp(log_a)
    mult = jnp.sqrt(jnp.tanh(-log_a) * (a * a + 1.0))
    return r, ig, a, mult


def _conv(xb, halo, cw_ref, cb_ref):
    xc = cb_ref[...] + cw_ref[CONV_WIDTH - 1 : CONV_WIDTH, :] * xb
    for k in range(CONV_WIDTH - 1):
        xc = xc + cw_ref[k : k + 1, :] * _shift_down(xb, halo, CONV_WIDTH - 1 - k)
    return xc


def _layer_b_fwd(x1, nw, bin_w, cw, cb, gab, gb, lam, bout, nf, tgt, tm):
    t_rows, d = x1.shape
    bw = bout.shape[0]
    hd = bw // B_HEADS
    nt = t_rows // tm

    def body(
        x1_ref, nw_ref, bin_ref, cw_ref, cb_ref, gab_ref, gb_ref, lam_ref, bout_ref, nf_ref, tgt_ref,
        z_ref, h_ref, h1_ref, xbt_ref, ht_ref, dx2_ref, loss_ref, gnf_ref,
        tail_s, carry_s, a_s, b_s, hs_s, acc_s,
    ):
        @pl.when(pl.program_id(0) == 0)
        def _():
            tail_s[...] = jnp.zeros_like(tail_s)
            carry_s[...] = jnp.zeros_like(carry_s)
            acc_s[...] = jnp.zeros_like(acc_s)
            gnf_ref[...] = jnp.zeros_like(gnf_ref)

        x1 = x1_ref[...]
        h1, _, _ = _rms_fwd(x1, nw_ref[...])
        h1 = h1.astype(BF16)
        h1_ref[...] = h1
        z = jnp.concatenate([_dot(h1, bin_ref[k]) for k in range(N_CHIPS)], axis=1)
        z_ref[...] = z
        xb = z[:, :bw]
        xc = _conv(xb, tail_s[...], cw_ref, cb_ref)
        tail = xb[tm - SUBLANES :, :]
        tail_s[...] = tail
        xbt_ref[...] = tail
        sp = _softplus_neg(lam_ref[...])
        for h in range(B_HEADS):
            cols = slice(h * hd, (h + 1) * hd)
            xc_h = xc[:, cols]
            _, ig, a, mult = _gates(xc_h, gab_ref, gb_ref, sp[:, cols], h, hd)
            a_s[:, cols] = a
            b_s[:, cols] = mult * (ig * xc_h)
        carry = _scan_blocks(a_s, b_s, hs_s, carry_s[...], tm, reverse=False)
        carry_s[...] = carry
        ht_ref[...] = hs_s[tm - SUBLANES :, :]
        hs = hs_s[...]
        h_ref[...] = hs
        g = z[:, bw:]
        y = (hs * (g * _sigmoid(g))).astype(BF16)
        x2 = x1 + _dot(y, bout_ref[...])

        nf = nf_ref[...]
        o, xh, r = _rms_fwd(x2, nf)
        diff = o - tgt_ref[...]
        acc_s[...] += jnp.sum(diff * diff, axis=0, keepdims=True)
        do = diff * (1.0 / d)
        dx2, gnf = _rms_bwd(do, xh, r, nf)
        gnf_ref[...] += gnf
        dx2_ref[...] = dx2

        @pl.when(pl.program_id(0) == nt - 1)
        def _():
            total = jnp.sum(acc_s[...], axis=-1, keepdims=True) * (0.5 / d)
            loss_ref[...] = jnp.broadcast_to(total, loss_ref.shape)

    row = lambda i: (i, 0)
    return _pcall(
        body,
        name="layer_b_fwd",
        grid=(nt,),
        in_specs=[
            pl.BlockSpec((tm, d), row),
            _full(nw.shape),
            _full(bin_w.shape),
            _full(cw.shape),
            _full(cb.shape),
            _full(gab.shape),
            _full(gb.shape),
            _full(lam.shape),
            _full(bout.shape),
            _full(nf.shape),
            pl.BlockSpec((tm, d), row),
        ],
        out_specs=[
            pl.BlockSpec((tm, 2 * bw), row),
            pl.BlockSpec((tm, bw), row),
            pl.BlockSpec((tm, d), row),
            pl.BlockSpec((None, SUBLANES, bw), lambda i: (i, 0, 0)),
            pl.BlockSpec((None, SUBLANES, bw), lambda i: (i, 0, 0)),
            pl.BlockSpec((tm, d), row),
            _full((1, LANES)),
            _full((1, d)),
        ],
        out_shape=[
            jax.ShapeDtypeStruct((t_rows, 2 * bw), F32),
            jax.ShapeDtypeStruct((t_rows, bw), F32),
            jax.ShapeDtypeStruct((t_rows, d), BF16),
            jax.ShapeDtypeStruct((nt, SUBLANES, bw), F32),
            jax.ShapeDtypeStruct((nt, SUBLANES, bw), F32),
            jax.ShapeDtypeStruct((t_rows, d), F32),
            jax.ShapeDtypeStruct((1, LANES), F32),
            jax.ShapeDtypeStruct((1, d), F32),
        ],
        scratch_shapes=[
            pltpu.VMEM((SUBLANES, bw), F32),
            pltpu.VMEM((SUBLANES, bw), F32),
            pltpu.VMEM((tm, bw), F32),
            pltpu.VMEM((tm, bw), F32),
            pltpu.VMEM((tm, bw), F32),
            pltpu.VMEM((1, d), F32),
        ],
        compiler_params=_cparams(("arbitrary",)),
    )(x1, nw, bin_w, cw, cb, gab, gb, lam, bout, nf, tgt)


def _layer_b_bwd(dout, x1, z, hseq, xb_tails, h_tails, nw, bin_w, cw, cb, gab, gabt, gb, lam, bout, tm):
    t_rows, d = x1.shape
    bw = bout.shape[0]
    hd = bw // B_HEADS
    nt = t_rows // tm

    def body(
        dout_ref, x1_ref, z_ref, h_ref, xbt_ref, ht_ref, nw_ref, bin_ref, cw_ref, cb_ref, gab_ref, gabt_ref,
        gb_ref, lam_ref, bout_ref,
        dx1_ref, dz_ref, y_ref, dob_ref, ggab_ref, ggb_ref, gcw_ref, gcb_ref, glam_ref, gnw_ref,
        gcarry_s, afirst_s, head_s, aup_s, dh_s, gt_s, dxc_s, xc_s,
    ):
        step = pl.program_id(0)
        tile = nt - 1 - step

        @pl.when(step == 0)
        def _():
            for ref in (ggab_ref, ggb_ref, gcw_ref, gcb_ref, glam_ref, gnw_ref, gcarry_s, afirst_s, head_s):
                ref[...] = jnp.zeros_like(ref)

        first_tile = tile == 0
        xb_halo = jnp.where(first_tile, 0.0, xbt_ref[...])
        h_halo = jnp.where(first_tile, 0.0, ht_ref[...])

        dout = dout_ref[...]
        dob = dout.astype(BF16)
        dob_ref[...] = dob
        dy = _dot_nt(dob, bout_ref[...])
        hs = h_ref[...]
        g = z_ref[:, bw:]
        sg, dsg = _silu_and_grad(g)
        y_ref[...] = (hs * sg).astype(BF16)
        dz_ref[:, bw:] = (dy * hs * dsg).astype(BF16)
        dh_s[...] = dy * sg

        xb = z_ref[:, :bw]
        xc = _conv(xb, xb_halo, cw_ref, cb_ref)
        xc_s[...] = xc
        lam = lam_ref[...]
        sp = _softplus_neg(lam)
        for h in range(B_HEADS):
            cols = slice(h * hd, (h + 1) * hd)
            _, _, a, _ = _gates(xc[:, cols], gab_ref, gb_ref, sp[:, cols], h, hd)
            aup_s[:, cols] = _shift_up(a, afirst_s[:, cols], 1)
            afirst_s[:, cols] = jnp.broadcast_to(a[0:1, :], (SUBLANES, hd))
        carry = _scan_blocks(aup_s, dh_s, gt_s, gcarry_s[...], tm, reverse=True)
        gcarry_s[...] = carry

        h_prev = _shift_down(hs, h_halo, 1)
        for h in range(B_HEADS):
            cols = slice(h * hd, (h + 1) * hd)
            xc_h = xc_s[:, cols]
            sp_h = sp[:, cols]
            r, ig, a, mult = _gates(xc_h, gab_ref, gb_ref, sp_h, h, hd)
            gt = gt_s[:, cols]
            da = gt * h_prev[:, cols]
            dmult = gt * (ig * xc_h)
            dig = gt * (mult * xc_h)
            dxc_direct = gt * (mult * ig)
            dla = da * a - dmult * (a * a) / mult
            glam_ref[:, cols] += jnp.sum(dla * r, axis=0, keepdims=True)
            dr = dla * ((-RG_C) * sp_h)
            dpre = jnp.concatenate([dr * r * (1.0 - r), dig * ig * (1.0 - ig)], axis=1)
            ggb_ref[:, cols] += jnp.sum(dpre[:, :hd], axis=0, keepdims=True)
            ggb_ref[:, bw + h * hd : bw + (h + 1) * hd] += jnp.sum(dpre[:, hd:], axis=0, keepdims=True)
            dpb = dpre.astype(BF16)
            ggab_ref[h] += _dot_tn(xc_h.astype(BF16), dpb)
            dxc_s[:, cols] = dxc_direct + _dot(dpb, gabt_ref[h])
        glam_ref[...] = jnp.where(step == nt - 1, glam_ref[...] * (RG_C * _sigmoid(-lam)), glam_ref[...])

        dxc = dxc_s[...]
        gcb_ref[...] += jnp.sum(dxc, axis=0, keepdims=True)
        dxb = cw_ref[CONV_WIDTH - 1 : CONV_WIDTH, :] * dxc
        gcw_ref[CONV_WIDTH - 1 : CONV_WIDTH, :] += jnp.sum(dxc * xb, axis=0, keepdims=True)
        head = head_s[...]
        for k in range(CONV_WIDTH - 1):
            lag = CONV_WIDTH - 1 - k
            dxb = dxb + cw_ref[k : k + 1, :] * _shift_up(dxc, head, lag)
            gcw_ref[k : k + 1, :] += jnp.sum(dxc * _shift_down(xb, xb_halo, lag), axis=0, keepdims=True)
        head_s[...] = dxc[:SUBLANES, :]
        dz_ref[:, :bw] = dxb.astype(BF16)

        s_cols = 2 * bw // N_CHIPS
        dh1 = jnp.zeros((tm, d), F32)
        for k in range(N_CHIPS):
            dh1 = dh1 + _dot_nt(dz_ref[:, k * s_cols : (k + 1) * s_cols], bin_ref[k])
        x1 = x1_ref[...]
        nw = nw_ref[...]
        _, xh, r1 = _rms_fwd(x1, nw)
        dx, gnw = _rms_bwd(dh1, xh, r1, nw)
        gnw_ref[...] += gnw
        dx1_ref[...] = dout + dx

    rev = lambda i: (nt - 1 - i, 0)
    prev = lambda i: (jnp.maximum(nt - 2 - i, 0), 0, 0)
    return _pcall(
        body,
        name="layer_b_bwd",
        grid=(nt,),
        in_specs=[
            pl.BlockSpec((tm, d), rev),
            pl.BlockSpec((tm, d), rev),
            pl.BlockSpec((tm, 2 * bw), rev),
            pl.BlockSpec((tm, bw), rev),
            pl.BlockSpec((None, SUBLANES, bw), prev),
            pl.BlockSpec((None, SUBLANES, bw), prev),
            _full(nw.shape),
            _full(bin_w.shape),
            _full(cw.shape),
            _full(cb.shape),
            _full(gab.shape),
            _full(gabt.shape),
            _full(gb.shape),
            _full(lam.shape),
            _full(bout.shape),
        ],
        out_specs=[
            pl.BlockSpec((tm, d), rev),
            pl.BlockSpec((tm, 2 * bw), rev),
            pl.BlockSpec((tm, bw), rev),
            pl.BlockSpec((tm, d), rev),
            _full((B_HEADS, hd, 2 * hd)),
            _full((1, 2 * bw)),
            _full((SUBLANES, bw)),
            _full((1, bw)),
            _full((1, bw)),
            _full((1, d)),
        ],
        out_shape=[
            jax.ShapeDtypeStruct((t_rows, d), F32),
            jax.ShapeDtypeStruct((t_rows, 2 * bw), BF16),
            jax.ShapeDtypeStruct((t_rows, bw), BF16),
            jax.ShapeDtypeStruct((t_rows, d), BF16),
            jax.ShapeDtypeStruct((B_HEADS, hd, 2 * hd), F32),
            jax.ShapeDtypeStruct((1, 2 * bw), F32),
            jax.ShapeDtypeStruct((SUBLANES, bw), F32),
            jax.ShapeDtypeStruct((1, bw), F32),
            jax.ShapeDtypeStruct((1, bw), F32),
            jax.ShapeDtypeStruct((1, d), F32),
        ],
        scratch_shapes=[pltpu.VMEM((SUBLANES, bw), F32)] * 3 + [pltpu.VMEM((tm, bw), F32)] * 5,
        compiler_params=_cparams(("arbitrary",)),
    )(dout, x1, z, hseq, xb_tails, h_tails, nw, bin_w, cw, cb, gab, gabt, gb, lam, bout)


def _wgrad(a, b, m_blocks, n_blocks, hook=None):
    k, m = a.shape
    n = b.shape[1]
    bm, bn = m // m_blocks, n // n_blocks

    def body(a_ref, b_ref, o_ref):
        o_ref[...] = _dot_tn(a_ref[...], b_ref[...])

    out = _pcall(
        body,
        hook,
        name=f"wgrad_{m}x{n}",
        grid=(n_blocks, m_blocks),
        in_specs=[pl.BlockSpec((k, bm), lambda j, i: (0, i)), pl.BlockSpec((k, bn), lambda j, i: (0, j))],
        out_specs=[pl.BlockSpec((None, None, bm, bn), lambda j, i: (j, i, 0, 0))],
        out_shape=[jax.ShapeDtypeStruct((n_blocks, m_blocks, bm, bn), F32)],
        compiler_params=_cparams(("arbitrary", "arbitrary")),
    )(a, b)
    return out[0] if hook is None else (out[0][0], out[1])


def _adamw_math(w, g, m, v):
    m = ADAM_B1 * m + (1.0 - ADAM_B1) * g
    v = ADAM_B2 * v + (1.0 - ADAM_B2) * (g * g)
    m_hat = m / (1.0 - ADAM_B1**ADAM_STEP)
    v_hat = v / (1.0 - ADAM_B2**ADAM_STEP)
    delta = -ADAM_LR * (m_hat / (jnp.sqrt(v_hat) + ADAM_EPS) + ADAM_WD * w)
    return delta, m, v


def _adamw(w, g, m, v):
    rows, cols = w.shape
    tr = _row_tile(rows, cols, 1024 * 1024)

    def body(w_ref, g_ref, m_ref, v_ref, d_ref, mo_ref, vo_ref):
        d_ref[...], mo_ref[...], vo_ref[...] = _adamw_math(w_ref[...], g_ref[...], m_ref[...], v_ref[...])

    spec = pl.BlockSpec((tr, cols), lambda i: (i, 0))
    return _pcall(
        body,
        name=f"adamw_{rows}x{cols}",
        grid=(rows // tr,),
        in_specs=[spec] * 4,
        out_specs=[spec] * 3,
        out_shape=[jax.ShapeDtypeStruct((rows, cols), F32)] * 3,
        compiler_params=_cparams(("arbitrary",)),
    )(w, g, m, v)


def _adamw_many(ws, gs, ms, vs):
    n = len(ws)

    def body(*refs):
        w_refs, g_refs, m_refs, v_refs = (refs[i * n : (i + 1) * n] for i in range(4))
        d_refs, mo_refs, vo_refs = (refs[(4 + i) * n : (5 + i) * n] for i in range(3))
        for i in range(n):
            d_refs[i][...], mo_refs[i][...], vo_refs[i][...] = _adamw_math(
                w_refs[i][...], g_refs[i][...], m_refs[i][...], v_refs[i][...]
            )

    vmem = pl.BlockSpec(memory_space=pltpu.VMEM)
    outs = _pcall(
        body,
        name="adamw_small",
        in_specs=[vmem] * (4 * n),
        out_specs=[vmem] * (3 * n),
        out_shape=[jax.ShapeDtypeStruct(w.shape, F32) for w in ws] * 3,
        compiler_params=_cparams(),
    )(*ws, *gs, *ms, *vs)
    return outs[:n], outs[n : 2 * n], outs[2 * n :]


def _pack_rows(parts, lanes=LANES):
    flat = jnp.concatenate([p.reshape(-1) for p in parts])
    per = N_DEV * SUBLANES * lanes
    total = -(-flat.shape[0] // per) * per
    flat = jnp.pad(flat, (0, total - flat.shape[0]))
    return flat.reshape(N_DEV, total // (N_DEV * lanes), lanes)


def _unpack(flat, shapes):
    out, at = [], 0
    for s in shapes:
        n = 1
        for dim in s:
            n *= dim
        out.append(flat[at : at + n].reshape(s))
        at += n
    return out


def kernel(x, norm_w, a_w_in, a_ln_w, a_ln_b, a_w_s, a_b_s, a_w_out, b_w_in, b_conv_w, b_conv_b, b_gate_a_w, b_gate_a_b, b_gate_x_w, b_gate_x_b, b_lambda, b_w_out, norm_f_w, loss_target, m_norm_w, m_a_w_in, m_a_ln_w, m_a_ln_b, m_a_w_s, m_a_b_s, m_a_w_out, m_b_w_in, m_b_conv_w, m_b_conv_b, m_b_gate_a_w, m_b_gate_a_b, m_b_gate_x_w, m_b_gate_x_b, m_b_lambda, m_b_w_out, m_norm_f_w, v_norm_w, v_a_w_in, v_a_ln_w, v_a_ln_b, v_a_w_s, v_a_b_s, v_a_w_out, v_b_w_in, v_b_conv_w, v_b_conv_b, v_b_gate_a_w, v_b_gate_a_b, v_b_gate_x_w, v_b_gate_x_b, v_b_lambda, v_b_w_out, v_norm_f_w):
    t_rows, d = x.shape[1], x.shape[2]
    aw = a_ln_w.shape[1]
    bw = b_gate_a_w.shape[1] * b_gate_a_w.shape[2]
    hd = bw // B_HEADS
    mine = 2 * lax.axis_index("x") + lax.axis_index("y")
    core = lax.axis_index("c")

    win_l = _cast_to_segments(a_w_in[0], mine, 256)
    wout_l = _cast_to_segments(a_w_out[0], mine, 256)
    bin_l = _cast_to_segments(b_w_in[0], mine, 256)
    bout_l = _cast_to_segments(b_w_out[0], mine, 192)
    small_l = jnp.concatenate([b_conv_w[0], b_conv_b, b_gate_a_b, b_gate_x_b, b_lambda], axis=0)
    win_g, wout_g, small_g = _run_hook(_gather_hook([win_l, wout_l], small_l), "gather_layer_a")
    win = win_g.reshape(N_CHIPS, d, -1)
    wout = wout_g.reshape(aw, d)

    tril = jnp.tril(jnp.ones((CHUNK, CHUNK), F32))
    wc = (a_w_s[0] * tril[None]).astype(BF16)
    wct = jnp.swapaxes(wc, 1, 2)
    bs_t = a_b_s[0].T
    gab = jnp.concatenate([b_gate_a_w[0], b_gate_x_w[0]], axis=2).astype(BF16)
    gabt = jnp.swapaxes(gab, 1, 2)
    nw0, nw1, nf = norm_w[0:1], norm_w[1:2], norm_f_w.reshape(1, d)

    x0 = x[0]
    (z_a, x1, h0), (bin_g, bout_g) = _layer_a_fwd(
        x0, nw0, win, a_ln_w, a_ln_b, wc, bs_t, wout, TM_FWD, _gather_hook([bin_l, bout_l])
    )
    bin_w = bin_g.reshape(N_CHIPS, d, -1)
    bout = bout_g.reshape(bw, d)
    small_f = jnp.transpose(small_g, (1, 0, 2)).reshape(SUBLANES, bw)
    cw, cb = small_f[0:CONV_WIDTH], small_f[CONV_WIDTH : CONV_WIDTH + 1]
    gb = jnp.concatenate([small_f[5:6], small_f[6:7]], axis=1)
    lam = small_f[7:8]
    z_b, hseq, h1, xb_tails, h_tails, dx2, loss_l, g_nf = _layer_b_fwd(
        x1, nw1, bin_w, cw, cb, gab, gb, lam, bout, nf, loss_target[0], TM_FWD
    )
    dx1, dz_b, y_b, dob_b, g_gab, g_gb, g_cw, g_cb, g_lam, g_nw1 = _layer_b_bwd(
        dx2, x1, z_b, hseq, xb_tails, h_tails, nw1, bin_w, cw, cb, gab, gabt, gb, lam, bout, TM_FWD
    )
    seg = lambda g: g.reshape(N_DEV, -1, g.shape[3])
    g_bout = seg(_wgrad(y_b, dob_b, N_CHIPS, 1))
    g_bin = seg(_wgrad(h1, dz_b, 2, N_CHIPS))

    layer_b = [g_bin, g_bout]
    a_args = (x0, z_a, nw0, win, a_ln_w, a_ln_b, wc, wct, bs_t, wout)
    half = t_rows // TM_A_BWD // 2
    first, got_a = _layer_a_bwd(dx1, *a_args, (0, half), None, _swap_hook(layer_b))
    parts = [_add_own_half(b, g, core, BF16) for b, g in zip(layer_b, got_a)]
    second, got_b = _layer_a_bwd(dx1, *a_args, (half, 2 * half), first[:4], _send_hook(parts))
    grad_x, dz_a, y_a, dob_a = second[:4]
    g_ws, g_bst, g_lnw, g_lnb, g_nw0 = (p + q for p, q in zip(first[4:], second[4:]))
    reduced = [_add_received(b, ga, gb_, mine, core, core, 2) for b, ga, gb_ in zip(layer_b, got_a, got_b)]
    g_wout, (gr_bin, gr_bout) = _wgrad(y_a, dob_a, N_DEV, 1, _share_hook(reduced))
    g_wout = seg(g_wout)
    g_win = seg(_wgrad(h0, dz_a, 2, N_CHIPS))

    small_shapes = [
        (2, d), (1, aw), (1, aw), (A_GROUPS, CHUNK, CHUNK), (A_GROUPS, CHUNK), (B_HEADS, hd, hd), (B_HEADS, hd, hd),
        (d,), (CONV_WIDTH, bw), (1, bw), (1, bw), (1, bw), (1, bw),
    ]
    small = _pack_rows(
        [
            jnp.concatenate([g_nw0, g_nw1], axis=0), g_lnw, g_lnb, g_ws, g_bst.T, g_gab[:, :, :hd], g_gab[:, :, hd:],
            g_nf, g_cw[:CONV_WIDTH], g_cb, g_gb[:, :bw], g_gb[:, bw:], g_lam,
        ]
    )

    layer_a = [g_win, g_wout, small]
    wires = [BF16, BF16, F32]
    got_a = _run_hook(_swap_hook(layer_a), "swap_halves")
    parts = [_add_own_half(b, g, core, w) for b, g, w in zip(layer_a, got_a, wires)]
    got_b = _run_hook(_send_hook(parts), "send_to_owners")
    reduced = [_add_received(b, ga, gb_, mine, core, core, 2) for b, ga, gb_ in zip(layer_a[:2], got_a, got_b)]
    reduced_small = _add_received(small, got_a[2], got_b[2], mine, core, 2 * mine + core, N_DEV)
    gr_win, gr_wout, small_r = _run_hook(_share_hook(reduced, reduced_small), "share_reduced")

    gr_win = gr_win.reshape(a_w_in.shape[1:])
    gr_wout = gr_wout.reshape(a_w_out.shape[1:])
    gr_bin = gr_bin.reshape(b_w_in.shape[1:])
    gr_bout = gr_bout.reshape(b_w_out.shape[1:])
    (g_norm_w, g_a_ln_w, g_a_ln_b, g_a_w_s, g_a_b_s, g_gate_a_w, g_gate_x_w, g_norm_f, gf_cw, gf_cb, gf_gab, gf_gxb,
     gf_lam) = _unpack(small_r.reshape(-1), small_shapes)
    shard = lambda g: lax.dynamic_slice_in_dim(g, mine * (bw // N_CHIPS), bw // N_CHIPS, axis=1)

    grads = {
        "norm_w": g_norm_w, "a_w_in": gr_win[None], "a_ln_w": g_a_ln_w, "a_ln_b": g_a_ln_b, "a_w_s": g_a_w_s[None],
        "a_b_s": g_a_b_s[None], "a_w_out": gr_wout[None], "b_w_in": gr_bin[None], "b_conv_w": shard(gf_cw)[None],
        "b_conv_b": shard(gf_cb), "b_gate_a_w": g_gate_a_w[None], "b_gate_a_b": shard(gf_gab),
        "b_gate_x_w": g_gate_x_w[None], "b_gate_x_b": shard(gf_gxb), "b_lambda": shard(gf_lam),
        "b_w_out": gr_bout[None], "norm_f_w": g_norm_f,
    }
    weights = dict(norm_w=norm_w, a_w_in=a_w_in, a_ln_w=a_ln_w, a_ln_b=a_ln_b, a_w_s=a_w_s, a_b_s=a_b_s, a_w_out=a_w_out, b_w_in=b_w_in, b_conv_w=b_conv_w, b_conv_b=b_conv_b, b_gate_a_w=b_gate_a_w, b_gate_a_b=b_gate_a_b, b_gate_x_w=b_gate_x_w, b_gate_x_b=b_gate_x_b, b_lambda=b_lambda, b_w_out=b_w_out, norm_f_w=norm_f_w)
    m_in = dict(norm_w=m_norm_w, a_w_in=m_a_w_in, a_ln_w=m_a_ln_w, a_ln_b=m_a_ln_b, a_w_s=m_a_w_s, a_b_s=m_a_b_s, a_w_out=m_a_w_out, b_w_in=m_b_w_in, b_conv_w=m_b_conv_w, b_conv_b=m_b_conv_b, b_gate_a_w=m_b_gate_a_w, b_gate_a_b=m_b_gate_a_b, b_gate_x_w=m_b_gate_x_w, b_gate_x_b=m_b_gate_x_b, b_lambda=m_b_lambda, b_w_out=m_b_w_out, norm_f_w=m_norm_f_w)
    v_in = dict(norm_w=v_norm_w, a_w_in=v_a_w_in, a_ln_w=v_a_ln_w, a_ln_b=v_a_ln_b, a_w_s=v_a_w_s, a_b_s=v_a_b_s, a_w_out=v_a_w_out, b_w_in=v_b_w_in, b_conv_w=v_b_conv_w, b_conv_b=v_b_conv_b, b_gate_a_w=v_b_gate_a_w, b_gate_a_b=v_b_gate_a_b, b_gate_x_w=v_b_gate_x_w, b_gate_x_b=v_b_gate_x_b, b_lambda=v_b_lambda, b_w_out=v_b_w_out, norm_f_w=v_norm_f_w)
    names = list(weights)
    big_names = ("a_w_in", "a_w_out", "b_w_in", "b_w_out")
    delta, new_m, new_v = {}, {}, {}
    for n in big_names:
        shape = weights[n].shape
        two_d = lambda a: a.reshape(shape[1:])
        dl, mo, vo = _adamw(two_d(weights[n]), two_d(grads[n]), two_d(m_in[n]), two_d(v_in[n]))
        delta[n], new_m[n], new_v[n] = dl.reshape(shape), mo.reshape(shape), vo.reshape(shape)
    small_names = [n for n in names if n not in big_names]
    at_least_2d = lambda a: a.reshape(1, -1) if a.ndim == 1 else a
    outs = _adamw_many(*[[at_least_2d(src[n]) for n in small_names] for src in (weights, grads, m_in, v_in)])
    for dst, vals in zip((delta, new_m, new_v), outs):
        for n, val in zip(small_names, vals):
            dst[n] = val.reshape(weights[n].shape)

    loss = lax.psum(loss_l[0, 0], ("x", "y", "c"))
    return (
        loss,
        grad_x[None],
        *[grads[n] for n in names],
        *[delta[n] for n in names],
        *[new_m[n] for n in names],
        *[new_v[n] for n in names],
    )
```

```python
import jax
import jax.numpy as jnp
from jax import lax
from jax.experimental import pallas as pl
from jax.experimental.pallas import tpu as pltpu

F32 = jnp.float32
BF16 = jnp.bfloat16

RMS_EPS = 1e-6
LN_EPS = 1e-5
RG_C = 8.0
CHUNK = 128
A_GROUPS = 8
B_HEADS = 12
CONV_WIDTH = 4

ADAM_LR = 0.001
ADAM_B1 = 0.9
ADAM_B2 = 0.999
ADAM_EPS = 1e-08
ADAM_WD = 0.01
ADAM_STEP = 10

N_CHIPS = 4
N_DEV = 8
SUBLANES = 8
LANES = 128
V7X_VMEM_BYTES = 64 * 1024 * 1024
VMEM_LIMIT = V7X_VMEM_BYTES * 7 // 8
MESH = pl.DeviceIdType.MESH
ANY = pl.BlockSpec(memory_space=pl.ANY)

TM_FWD = 256
TM_A_BWD = 256
TM_A_DX = 512

GELU_C0 = 0.7978845608028654
GELU_C1 = 0.044715


class _Hook:
    def __init__(self, operands, out_shapes, aliases, n_sems, start, finish):
        self.operands, self.out_shapes, self.aliases, self.n_sems = operands, out_shapes, aliases, n_sems
        self.start, self.finish = start, finish


def _pcall(body, hook=None, **kw):
    if hook is None:
        return pl.pallas_call(body, **kw)
    n_in, n_out = len(kw["in_specs"]), len(kw["out_shape"])
    hi, ho = len(hook.operands), len(hook.out_shapes)
    grid = kw.get("grid", ())

    def wrapped(*refs):
        ins, h_in = refs[:n_in], refs[n_in : n_in + hi]
        outs = refs[n_in + hi : n_in + hi + n_out]
        h_out = refs[n_in + hi + n_out : n_in + hi + n_out + ho]
        scratch = refs[n_in + hi + n_out + ho : -2]
        send_sems, recv_sems = refs[-2:]
        if not grid:
            hook.start(h_in, h_out, send_sems, recv_sems)
            body(*ins, *outs, *scratch)
            hook.finish(h_in, h_out, send_sems, recv_sems)
            return
        first = pl.program_id(0) == 0
        last = pl.program_id(0) == grid[0] - 1
        for axis in range(1, len(grid)):
            first = jnp.logical_and(first, pl.program_id(axis) == 0)
            last = jnp.logical_and(last, pl.program_id(axis) == grid[axis] - 1)

        @pl.when(first)
        def _():
            hook.start(h_in, h_out, send_sems, recv_sems)

        body(*ins, *outs, *scratch)

        @pl.when(last)
        def _():
            hook.finish(h_in, h_out, send_sems, recv_sems)

    aliases = dict(kw.pop("input_output_aliases", {}))
    aliases.update({n_in + i: n_out + o for i, o in hook.aliases.items()})
    kw.update(
        in_specs=list(kw["in_specs"]) + [ANY] * hi,
        out_specs=list(kw["out_specs"]) + [ANY] * ho,
        out_shape=list(kw["out_shape"]) + list(hook.out_shapes),
        scratch_shapes=list(kw.get("scratch_shapes", ()))
        + [pltpu.SemaphoreType.DMA((hook.n_sems,)), pltpu.SemaphoreType.DMA((hook.n_sems,))],
        input_output_aliases=aliases,
    )
    call = pl.pallas_call(wrapped, **kw)

    def run(*operands):
        outs = call(*operands, *hook.operands)
        return outs[:n_out], outs[n_out:]

    return run


def _run_hook(hook, name):
    def body():
        pass

    return _pcall(body, hook, name=name, in_specs=[], out_specs=[], out_shape=[])()[1]


def _cparams(sem=None):
    return pltpu.CompilerParams(dimension_semantics=sem, vmem_limit_bytes=VMEM_LIMIT)


def _full(shape):
    zeros = (0,) * len(shape)
    return pl.BlockSpec(shape, lambda *_: zeros)


def _scalars(*vals):
    return jnp.stack([jnp.asarray(v, jnp.int32) for v in vals])


def _sigmoid(x):
    return 1.0 / (1.0 + jnp.exp(-x))


def _gelu(x):
    t = jnp.tanh(GELU_C0 * (x + GELU_C1 * (x * x * x)))
    return x * (0.5 * (1.0 + t))


def _gelu_and_grad(x):
    x2 = x * x
    t = jnp.tanh(GELU_C0 * (x + GELU_C1 * (x2 * x)))
    cdf = 0.5 * (1.0 + t)
    return x * cdf, cdf + 0.5 * x * (1.0 - t * t) * (GELU_C0 * (1.0 + 3.0 * GELU_C1 * x2))


def _silu_and_grad(x):
    s = _sigmoid(x)
    return x * s, s * (1.0 + x * (1.0 - s))


def _softplus_neg(lam):
    u = jnp.exp(-jnp.abs(lam))
    w = 1.0 + u
    log1p = jnp.where(w == 1.0, u, jnp.log(w) * (u / jnp.where(w == 1.0, 1.0, w - 1.0)))
    return jnp.maximum(-lam, 0.0) + log1p


def _dot(a, b):
    return jnp.dot(a, b, preferred_element_type=F32)


def _dot_nt(a, b):
    return lax.dot_general(a, b, (((1,), (1,)), ((), ())), preferred_element_type=F32)


def _dot_tn(a, b):
    return lax.dot_general(a, b, (((0,), (0,)), ((), ())), preferred_element_type=F32)


def _shift_down(v, halo, k):
    if k == 0:
        return v
    rolled = pltpu.roll(v, k, 0)
    row = lax.broadcasted_iota(jnp.int32, (SUBLANES, v.shape[1]), 0)
    top = jnp.where(row < k, pltpu.roll(halo, k, 0), rolled[:SUBLANES])
    return jnp.concatenate([top, rolled[SUBLANES:]], axis=0)


def _shift_up(v, head, k):
    if k == 0:
        return v
    n = v.shape[0]
    rolled = pltpu.roll(v, n - k, 0)
    row = lax.broadcasted_iota(jnp.int32, (SUBLANES, v.shape[1]), 0)
    bot = jnp.where(row >= SUBLANES - k, pltpu.roll(head, SUBLANES - k, 0), rolled[n - SUBLANES :])
    return jnp.concatenate([rolled[: n - SUBLANES], bot], axis=0)


def _scan_blocks(a_ref, b_ref, out_ref, carry, n_rows, reverse):
    width = a_ref.shape[1]
    row = lax.broadcasted_iota(jnp.int32, (SUBLANES, width), 0)
    n_blocks = n_rows // SUBLANES

    def block(j, carry):
        i = (n_blocks - 1 - j) if reverse else j
        r0 = pl.multiple_of(i * SUBLANES, SUBLANES)
        a = a_ref[pl.ds(r0, SUBLANES), :]
        b = b_ref[pl.ds(r0, SUBLANES), :]
        for d in (1, 2, 4):
            shift = (SUBLANES - d) if reverse else d
            keep = (row < SUBLANES - d) if reverse else (row >= d)
            a_s = pltpu.roll(a, shift, 0)
            b_s = pltpu.roll(b, shift, 0)
            b = jnp.where(keep, a * b_s + b, b)
            a = jnp.where(keep, a * a_s, a)
        h = a * carry + b
        out_ref[pl.ds(r0, SUBLANES), :] = h
        edge = h[0:1, :] if reverse else h[SUBLANES - 1 : SUBLANES, :]
        return jnp.broadcast_to(edge, (SUBLANES, width))

    return lax.fori_loop(0, n_blocks, block, carry)


def _rms_fwd(x, w):
    r = lax.rsqrt(jnp.mean(x * x, axis=-1, keepdims=True) + RMS_EPS)
    xh = x * r
    return xh * w, xh, r


def _rms_bwd(dh, xh, r, w):
    dxh = dh * w
    dx = r * (dxh - xh * jnp.mean(dxh * xh, axis=-1, keepdims=True))
    return dx, jnp.sum(dh * xh, axis=0, keepdims=True)


def _cast_to_segments(w, mine, rows):
    n, c = w.shape
    per = n // 2 // rows

    def body(k_ref, w_ref, o_ref):
        o_ref[...] = w_ref[...].astype(BF16)

    return _pcall(
        body,
        name=f"cast_{n}x{c}",
        grid_spec=pltpu.PrefetchScalarGridSpec(
            num_scalar_prefetch=1,
            grid=(n // rows,),
            in_specs=[pl.BlockSpec((rows, c), lambda i, k_ref: (i, 0))],
            out_specs=pl.BlockSpec((None, rows, c), lambda i, k_ref: (2 * k_ref[0] + i // per, i % per, 0)),
        ),
        out_shape=jax.ShapeDtypeStruct((N_DEV, n // 2, c), BF16),
        compiler_params=_cparams(("arbitrary",)),
    )(_scalars(mine), w)


def _place():
    x, y, c = lax.axis_index("x"), lax.axis_index("y"), lax.axis_index("c")
    chips = [(1 - x, y), (x, 1 - y), (1 - x, 1 - y)]
    return x, y, c, chips


def _chip_no(chip):
    return 2 * chip[0] + chip[1]


def _rcopy(src, dst, send_sem, recv_sem, to):
    return pltpu.make_async_remote_copy(
        src_ref=src, dst_ref=dst, send_sem=send_sem, recv_sem=recv_sem, device_id=to, device_id_type=MESH
    )


def _gather_hook(big, small=None):
    nb = len(big)
    n_sems = 6 * nb + 4

    def ici_copies(ins, outs, send, recv):
        x, y, c, chips = _place()
        copies = []
        for b in range(nb):
            own = outs[b].at[2 * _chip_no((x, y)) + c]
            for j, chip in enumerate(chips):
                copies.append(_rcopy(own, own, send.at[6 * b + j], recv.at[6 * b + j], (*chip, c)))
        if small is not None:
            there = outs[nb].at[_chip_no((x, y))]
            for j, chip in enumerate(chips):
                k = 6 * nb + j
                copies.append(_rcopy(ins[nb], there, send.at[k], recv.at[k], (*chip, c)))
        return copies

    def local_copy(ins, outs, send):
        x, y, _, _ = _place()
        return pltpu.make_async_copy(ins[nb], outs[nb].at[_chip_no((x, y))], send.at[6 * nb + 3])

    def start(ins, outs, send, recv):
        for cp in ici_copies(ins, outs, send, recv):
            cp.start()
        if small is not None:
            local_copy(ins, outs, send).start()

    def finish(ins, outs, send, recv):
        x, y, c, chips = _place()
        me, sibling = (x, y, c), (x, y, 1 - c)
        passed = []
        for b in range(nb):
            for j, chip in enumerate(chips):
                got = outs[b].at[2 * _chip_no(chip) + c]
                _rcopy(got, got, send.at[6 * b + j], recv.at[6 * b + j], me).wait_recv()
                fwd = _rcopy(got, got, send.at[6 * b + 3 + j], recv.at[6 * b + 3 + j], sibling)
                fwd.start()
                passed.append(fwd)
        if small is not None:
            for j, chip in enumerate(chips):
                k = 6 * nb + j
                got = outs[nb].at[_chip_no(chip)]
                _rcopy(got, got, send.at[k], recv.at[k], me).wait_recv()
        for b in range(nb):
            for j, chip in enumerate(chips):
                got = outs[b].at[2 * _chip_no(chip) + 1 - c]
                _rcopy(got, got, send.at[6 * b + 3 + j], recv.at[6 * b + 3 + j], me).wait_recv()
        for cp in ici_copies(ins, outs, send, recv) + passed:
            cp.wait_send()
        if small is not None:
            local_copy(ins, outs, send).wait()

    operands = list(big) + ([small] if small is not None else [])
    out_shapes = [jax.ShapeDtypeStruct(b.shape, b.dtype) for b in big]
    if small is not None:
        out_shapes.append(jax.ShapeDtypeStruct((N_CHIPS, *small.shape), small.dtype))
    return _Hook(operands, out_shapes, {b: b for b in range(nb)}, n_sems, start, finish)


def _both_ways_hook(operands, out_shapes, copies_of, n_sems):
    def start(ins, outs, send, recv):
        for cp in copies_of(ins, outs, send, recv):
            cp.start()

    def finish(ins, outs, send, recv):
        for cp in copies_of(ins, outs, send, recv):
            cp.wait()

    return _Hook(operands, out_shapes, {}, n_sems, start, finish)


def _swap_hook(bufs):
    def copies_of(ins, outs, send, recv):
        x, y, c, _ = _place()
        copies = []
        for b in range(len(bufs)):
            for j in range(N_CHIPS):
                k = b * N_CHIPS + j
                copies.append(_rcopy(ins[b].at[2 * j + 1 - c], outs[b].at[j], send.at[k], recv.at[k], (x, y, 1 - c)))
        return copies

    out_shapes = [jax.ShapeDtypeStruct((N_CHIPS, *b.shape[1:]), b.dtype) for b in bufs]
    return _both_ways_hook(list(bufs), out_shapes, copies_of, len(bufs) * N_CHIPS)


def _send_hook(parts):
    def copies_of(ins, outs, send, recv):
        _, _, c, chips = _place()
        copies = []
        for b in range(len(parts)):
            for j, chip in enumerate(chips):
                k = b * 3 + j
                copies.append(_rcopy(ins[b].at[_chip_no(chip)], outs[b].at[j], send.at[k], recv.at[k], (*chip, c)))
        return copies

    out_shapes = [jax.ShapeDtypeStruct((3, *p.shape[1:]), p.dtype) for p in parts]
    return _both_ways_hook(list(parts), out_shapes, copies_of, len(parts) * 3)


def _share_hook(big, small=None, tiny=None):
    nb = len(big)
    n_sems = nb + 7 + N_DEV
    t0 = nb + 7

    def tiny_copies(ins, outs, send, recv):
        x, y, c, _ = _place()
        there = outs[-1].at[2 * _chip_no((x, y)) + c]
        copies = []
        for r in range(1, N_DEV):
            to = (x ^ (r >> 2 & 1), y ^ (r >> 1 & 1), c ^ (r & 1))
            copies.append(_rcopy(ins[-1], there, send.at[t0 + r], recv.at[t0 + r], to))
        return copies

    def tiny_local(ins, outs, send):
        x, y, c, _ = _place()
        return pltpu.make_async_copy(ins[-1], outs[-1].at[2 * _chip_no((x, y)) + c], send.at[t0])

    def first_copies(outs, send, recv):
        x, y, c, chips = _place()
        sibling = (x, y, 1 - c)
        copies = [_rcopy(outs[b].at[c], outs[b].at[c], send.at[b], recv.at[b], sibling) for b in range(nb)]
        if small is not None:
            own = outs[nb].at[2 * _chip_no((x, y)) + c]
            copies.append(_rcopy(own, own, send.at[nb], recv.at[nb], sibling))
            for j, chip in enumerate(chips):
                copies.append(_rcopy(own, own, send.at[nb + 1 + j], recv.at[nb + 1 + j], (*chip, c)))
        return copies

    def start(ins, outs, send, recv):
        for cp in first_copies(outs, send, recv):
            cp.start()
        if tiny is not None:
            for cp in tiny_copies(ins, outs, send, recv):
                cp.start()
            tiny_local(ins, outs, send).start()

    def finish(ins, outs, send, recv):
        x, y, c, chips = _place()
        me, sibling = (x, y, c), (x, y, 1 - c)
        if tiny is not None:
            for cp in tiny_copies(ins, outs, send, recv):
                cp.wait()
            tiny_local(ins, outs, send).wait()
        passed = []
        if small is not None:
            for j, chip in enumerate(chips):
                got = outs[nb].at[2 * _chip_no(chip) + c]
                _rcopy(got, got, send.at[nb + 1 + j], recv.at[nb + 1 + j], me).wait_recv()
                fwd = _rcopy(got, got, send.at[nb + 4 + j], recv.at[nb + 4 + j], sibling)
                fwd.start()
                passed.append(fwd)
        for b in range(nb):
            got = outs[b].at[1 - c]
            _rcopy(got, got, send.at[b], recv.at[b], me).wait_recv()
        if small is not None:
            got = outs[nb].at[2 * _chip_no((x, y)) + 1 - c]
            _rcopy(got, got, send.at[nb], recv.at[nb], me).wait_recv()
            for j, chip in enumerate(chips):
                got = outs[nb].at[2 * _chip_no(chip) + 1 - c]
                _rcopy(got, got, send.at[nb + 4 + j], recv.at[nb + 4 + j], me).wait_recv()
        for cp in first_copies(outs, send, recv) + passed:
            cp.wait_send()

    operands = list(big) + ([small] if small is not None else [])
    out_shapes = [jax.ShapeDtypeStruct(a.shape, a.dtype) for a in operands]
    aliases = {i: i for i in range(len(operands))}
    if tiny is not None:
        operands.append(tiny)
        out_shapes.append(jax.ShapeDtypeStruct((N_DEV, *tiny.shape), tiny.dtype))
    return _Hook(operands, out_shapes, aliases, n_sems, start, finish)


def _row_tile(rows, cols, target_bytes=2 * 1024 * 1024):
    best = SUBLANES
    for t in range(SUBLANES, rows + 1, SUBLANES):
        if rows % t == 0 and t * cols * 4 <= target_bytes:
            best = t
    return best


def _add_own_half(buf, got, c, wire):
    _, rows, cols = buf.shape
    tr = _row_tile(rows, cols)

    def body(c_ref, a_ref, b_ref, o_ref):
        o_ref[...] = (a_ref[...] + b_ref[...]).astype(wire)

    return _pcall(
        body,
        name=f"add_own_half_{rows}x{cols}",
        grid_spec=pltpu.PrefetchScalarGridSpec(
            num_scalar_prefetch=1,
            grid=(N_CHIPS, rows // tr),
            in_specs=[
                pl.BlockSpec((None, None, tr, cols), lambda j, r, c_ref: (j, c_ref[0], r, 0)),
                pl.BlockSpec((None, tr, cols), lambda j, r, c_ref: (j, r, 0)),
            ],
            out_specs=pl.BlockSpec((None, tr, cols), lambda j, r, c_ref: (j, r, 0)),
        ),
        out_shape=jax.ShapeDtypeStruct((N_CHIPS, rows, cols), wire),
        compiler_params=_cparams(("arbitrary", "arbitrary")),
    )(_scalars(c), buf.reshape(N_CHIPS, 2, rows, cols), got)


def _add_received(buf, got_a, got_b, mine, c, slot, n_slots):
    _, rows, cols = buf.shape
    tr = _row_tile(rows, cols)

    def body(s_ref, x_ref, a_ref, g_ref, o_ref):
        own = x_ref[...] + a_ref[...]
        o_ref[...] = ((own + g_ref[0].astype(F32)) + g_ref[1].astype(F32)) + g_ref[2].astype(F32)

    return _pcall(
        body,
        name=f"add_received_{rows}x{cols}",
        grid_spec=pltpu.PrefetchScalarGridSpec(
            num_scalar_prefetch=1,
            grid=(rows // tr,),
            in_specs=[
                pl.BlockSpec((None, None, tr, cols), lambda r, s_ref: (s_ref[0], s_ref[1], r, 0)),
                pl.BlockSpec((None, tr, cols), lambda r, s_ref: (s_ref[0], r, 0)),
                pl.BlockSpec((3, tr, cols), lambda r, s_ref: (0, r, 0)),
            ],
            out_specs=pl.BlockSpec((None, tr, cols), lambda r, s_ref: (s_ref[2], r, 0)),
        ),
        out_shape=jax.ShapeDtypeStruct((n_slots, rows, cols), F32),
        compiler_params=_cparams(("arbitrary",)),
    )(_scalars(mine, c, slot), buf.reshape(N_CHIPS, 2, rows, cols), got_a, got_b)


def _layer_a_fwd(x, nw, win, ln_w, ln_b, wc, bs_t, wout, tm, hook):
    t_rows, d = x.shape
    n_sh, _, s_cols = win.shape
    aw = wout.shape[0]
    gd = aw // A_GROUPS
    tn = 512
    assert s_cols % tn == 0 and aw % tn == 0 and tm % CHUNK == 0

    def body(x_ref, nw_ref, win_ref, lnw_ref, lnb_ref, wc_ref, bst_ref, wout_ref, z_ref, x1_ref, h_ref, u_s, v_s, y_s):
        x = x_ref[...]
        h, _, _ = _rms_fwd(x, nw_ref[...])
        h = h.astype(BF16)
        h_ref[...] = h
        for j in range(3 * aw // tn):
            k, off = divmod(j * tn, s_cols)
            cols = slice((j * tn) % aw, (j * tn) % aw + tn)
            zj = _dot(h, win_ref[k, :, off : off + tn])
            z_ref[:, j * tn : (j + 1) * tn] = zj
            if j * tn < aw:
                u_s[:, cols] = _gelu(zj)
            elif j * tn < 2 * aw:
                v_s[:, cols] = _gelu(zj)
            else:
                u_s[:, cols] = u_s[:, cols] * (zj * _sigmoid(zj))
        v = v_s[...]
        mu = jnp.mean(v, axis=-1, keepdims=True)
        vc = v - mu
        rstd = lax.rsqrt(jnp.mean(vc * vc, axis=-1, keepdims=True) + LN_EPS)
        v_s[...] = (vc * rstd) * lnw_ref[...] + lnb_ref[...]
        for ck in range(tm // CHUNK):
            rows = slice(ck * CHUNK, (ck + 1) * CHUNK)
            for g in range(A_GROUPS):
                cols = slice(g * gd, (g + 1) * gd)
                s = _dot(wc_ref[g], v_s[rows, cols].astype(BF16)) + bst_ref[:, g : g + 1]
                y_s[rows, cols] = (u_s[rows, cols] * s).astype(BF16)
        x1_ref[...] = x + _dot(y_s[...], wout_ref[...])

    row = lambda i: (i, 0)
    return _pcall(
        body,
        hook,
        name="layer_a_fwd",
        grid=(t_rows // tm,),
        in_specs=[
            pl.BlockSpec((tm, d), row),
            _full(nw.shape),
            _full(win.shape),
            _full(ln_w.shape),
            _full(ln_b.shape),
            _full(wc.shape),
            _full(bs_t.shape),
            _full(wout.shape),
        ],
        out_specs=[pl.BlockSpec((tm, 3 * aw), row), pl.BlockSpec((tm, d), row), pl.BlockSpec((tm, d), row)],
        out_shape=[
            jax.ShapeDtypeStruct((t_rows, 3 * aw), F32),
            jax.ShapeDtypeStruct((t_rows, d), F32),
            jax.ShapeDtypeStruct((t_rows, d), BF16),
        ],
        scratch_shapes=[pltpu.VMEM((tm, aw), F32), pltpu.VMEM((tm, aw), F32), pltpu.VMEM((tm, aw), BF16)],
        compiler_params=_cparams(("arbitrary",)),
    )(x, nw, win, ln_w, ln_b, wc, bs_t, wout)


def _layer_a_bwd(dout, z, ln_w, ln_b, wc, wct, bs_t, wout, tiles, earlier, hook):
    t_rows, d = dout.shape
    aw = wout.shape[0]
    gd = aw // A_GROUPS
    tm = TM_A_BWD
    lo, hi = tiles
    n_earlier = 0 if earlier is None else len(earlier)

    def body(dout_ref, z_ref, lnw_ref, lnb_ref, wc_ref, wct_ref, bst_ref, wout_ref, *rest):
        dz_ref, y_ref, dob_ref, gws_ref, gbs_ref, glnw_ref, glnb_ref, u_s, vh_s, ds_s, dvn_s = rest[n_earlier:]

        @pl.when(pl.program_id(0) == 0)
        def _():
            gws_ref[...] = jnp.zeros_like(gws_ref)
            gbs_ref[...] = jnp.zeros_like(gbs_ref)
            glnw_ref[...] = jnp.zeros_like(glnw_ref)
            glnb_ref[...] = jnp.zeros_like(glnb_ref)

        dob = dout_ref[...].astype(BF16)
        dob_ref[...] = dob
        dy = _dot_nt(dob, wout_ref[...])

        zv = z_ref[:, aw : 2 * aw]
        vg, dvg_dz = _gelu_and_grad(zv)
        mu = jnp.mean(vg, axis=-1, keepdims=True)
        vc = vg - mu
        rstd = lax.rsqrt(jnp.mean(vc * vc, axis=-1, keepdims=True) + LN_EPS)
        vh = vc * rstd
        vh_s[...] = vh
        vn = (vh * lnw_ref[...] + lnb_ref[...]).astype(BF16)

        zu = z_ref[:, 0:aw]
        zg = z_ref[:, 2 * aw : 3 * aw]
        u, du_dz = _gelu_and_grad(zu)
        sg, dsg = _silu_and_grad(zg)
        u_s[...] = u * sg
        tril = lax.broadcasted_iota(jnp.int32, (CHUNK, CHUNK), 0) >= lax.broadcasted_iota(jnp.int32, (CHUNK, CHUNK), 1)
        for ck in range(tm // CHUNK):
            rows = slice(ck * CHUNK, (ck + 1) * CHUNK)
            for g in range(A_GROUPS):
                cols = slice(g * gd, (g + 1) * gd)
                vn_g = vn[rows, cols]
                s = _dot(wc_ref[g], vn_g) + bst_ref[:, g : g + 1]
                usg = u_s[rows, cols]
                dy_g = dy[rows, cols]
                y_ref[rows, cols] = (usg * s).astype(BF16)
                ds = dy_g * usg
                ds_s[rows, cols] = dy_g * s
                gbs_ref[:, g : g + 1] += jnp.sum(ds, axis=-1, keepdims=True)
                dsb = ds.astype(BF16)
                gws_ref[g] += jnp.where(tril, _dot_nt(dsb, vn_g), 0.0)
                dvn_s[rows, cols] = _dot(wct_ref[g], dsb)
        dusg = ds_s[...]
        dz_ref[:, 0:aw] = (dusg * sg * du_dz).astype(BF16)
        dz_ref[:, 2 * aw : 3 * aw] = (dusg * u * dsg).astype(BF16)

        dvn = dvn_s[...]
        vh = vh_s[...]
        glnw_ref[...] += jnp.sum(dvn * vh, axis=0, keepdims=True)
        glnb_ref[...] += jnp.sum(dvn, axis=0, keepdims=True)
        dvh = dvn * lnw_ref[...]
        dvg = rstd * (dvh - jnp.mean(dvh, axis=-1, keepdims=True) - vh * jnp.mean(dvh * vh, axis=-1, keepdims=True))
        dz_ref[:, aw : 2 * aw] = (dvg * dvg_dz).astype(BF16)

    row = lambda i: (i + lo, 0)
    call = _pcall(
        body,
        hook,
        name=f"layer_a_bwd_{lo}",
        grid=(hi - lo,),
        in_specs=[
            pl.BlockSpec((tm, d), row),
            pl.BlockSpec((tm, 3 * aw), row),
            _full(ln_w.shape),
            _full(ln_b.shape),
            _full(wc.shape),
            _full(wct.shape),
            _full(bs_t.shape),
            _full(wout.shape),
        ]
        + [ANY] * n_earlier,
        out_specs=[
            pl.BlockSpec((tm, 3 * aw), row),
            pl.BlockSpec((tm, aw), row),
            pl.BlockSpec((tm, d), row),
            _full((A_GROUPS, CHUNK, CHUNK)),
            _full((CHUNK, A_GROUPS)),
            _full((1, aw)),
            _full((1, aw)),
        ],
        out_shape=[
            jax.ShapeDtypeStruct((t_rows, 3 * aw), BF16),
            jax.ShapeDtypeStruct((t_rows, aw), BF16),
            jax.ShapeDtypeStruct((t_rows, d), BF16),
            jax.ShapeDtypeStruct((A_GROUPS, CHUNK, CHUNK), F32),
            jax.ShapeDtypeStruct((CHUNK, A_GROUPS), F32),
            jax.ShapeDtypeStruct((1, aw), F32),
            jax.ShapeDtypeStruct((1, aw), F32),
        ],
        scratch_shapes=[pltpu.VMEM((tm, aw), F32)] * 4,
        input_output_aliases={8 + i: i for i in range(n_earlier)},
        compiler_params=_cparams(("arbitrary",)),
    )
    return call(dout, z, ln_w, ln_b, wc, wct, bs_t, wout, *(earlier or ()))


def _layer_a_bwd_dx(dout, x, dz, nw, win, tm, hook):
    t_rows, d = x.shape
    n_sh, _, s_cols = win.shape

    def body(dout_ref, x_ref, dz_ref, nw_ref, win_ref, gx_ref, gnw_ref):
        @pl.when(pl.program_id(0) == 0)
        def _():
            gnw_ref[...] = jnp.zeros_like(gnw_ref)

        dh = jnp.zeros((tm, d), F32)
        for k in range(n_sh):
            dh = dh + _dot_nt(dz_ref[:, k * s_cols : (k + 1) * s_cols], win_ref[k])
        nw = nw_ref[...]
        _, xh, r = _rms_fwd(x_ref[...], nw)
        dx, gnw = _rms_bwd(dh, xh, r, nw)
        gnw_ref[0:1, :] += gnw
        gx_ref[...] = dout_ref[...] + dx

    row = lambda i: (i, 0)
    return _pcall(
        body,
        hook,
        name="layer_a_bwd_dx",
        grid=(t_rows // tm,),
        in_specs=[
            pl.BlockSpec((tm, d), row),
            pl.BlockSpec((tm, d), row),
            pl.BlockSpec((tm, n_sh * s_cols), row),
            _full(nw.shape),
            _full(win.shape),
        ],
        out_specs=[pl.BlockSpec((tm, d), row), _full((SUBLANES, d))],
        out_shape=[jax.ShapeDtypeStruct((t_rows, d), F32), jax.ShapeDtypeStruct((SUBLANES, d), F32)],
        compiler_params=_cparams(("arbitrary",)),
    )(dout, x, dz, nw, win)


def _gates(xc_h, gab_ref, gb_ref, sp_h, h, hd):
    pre = _dot(xc_h.astype(BF16), gab_ref[h])
    bw = gb_ref.shape[1] // 2
    r = _sigmoid(pre[:, :hd] + gb_ref[:, h * hd : (h + 1) * hd])
    ig = _sigmoid(pre[:, hd:] + gb_ref[:, bw + h * hd : bw + (h + 1) * hd])
    log_a = (-RG_C) * r * sp_h
    a = jnp.exp(log_a)
    mult = jnp.sqrt(jnp.tanh(-log_a) * (a * a + 1.0))
    return r, ig, a, mult


def _conv(xb, halo, cw_ref, cb_ref):
    xc = cb_ref[...] + cw_ref[CONV_WIDTH - 1 : CONV_WIDTH, :] * xb
    for k in range(CONV_WIDTH - 1):
        xc = xc + cw_ref[k : k + 1, :] * _shift_down(xb, halo, CONV_WIDTH - 1 - k)
    return xc


def _layer_b_fwd(x1, nw, bin_w, cw, cb, gab, gb, lam, bout, nf, tgt, tm):
    t_rows, d = x1.shape
    bw = bout.shape[0]
    hd = bw // B_HEADS
    nt = t_rows // tm

    def body(
        x1_ref, nw_ref, bin_ref, cw_ref, cb_ref, gab_ref, gb_ref, lam_ref, bout_ref, nf_ref, tgt_ref,
        z_ref, h_ref, h1_ref, xbt_ref, ht_ref, dx2_ref, loss_ref, gnf_ref,
        tail_s, carry_s, a_s, b_s, hs_s, acc_s,
    ):
        @pl.when(pl.program_id(0) == 0)
        def _():
            tail_s[...] = jnp.zeros_like(tail_s)
            carry_s[...] = jnp.zeros_like(carry_s)
            acc_s[...] = jnp.zeros_like(acc_s)
            gnf_ref[...] = jnp.zeros_like(gnf_ref)

        x1 = x1_ref[...]
        h1, _, _ = _rms_fwd(x1, nw_ref[...])
        h1 = h1.astype(BF16)
        h1_ref[...] = h1
        z = jnp.concatenate([_dot(h1, bin_ref[k]) for k in range(N_CHIPS)], axis=1)
        z_ref[...] = z
        xb = z[:, :bw]
        xc = _conv(xb, tail_s[...], cw_ref, cb_ref)
        tail = xb[tm - SUBLANES :, :]
        tail_s[...] = tail
        xbt_ref[...] = tail
        sp = _softplus_neg(lam_ref[...])
        for h in range(B_HEADS):
            cols = slice(h * hd, (h + 1) * hd)
            xc_h = xc[:, cols]
            _, ig, a, mult = _gates(xc_h, gab_ref, gb_ref, sp[:, cols], h, hd)
            a_s[:, cols] = a
            b_s[:, cols] = mult * (ig * xc_h)
        carry = _scan_blocks(a_s, b_s, hs_s, carry_s[...], tm, reverse=False)
        carry_s[...] = carry
        ht_ref[...] = hs_s[tm - SUBLANES :, :]
        hs = hs_s[...]
        h_ref[...] = hs
        g = z[:, bw:]
        y = (hs * (g * _sigmoid(g))).astype(BF16)
        x2 = x1 + _dot(y, bout_ref[...])

        nf = nf_ref[...]
        o, xh, r = _rms_fwd(x2, nf)
        diff = o - tgt_ref[...]
        acc_s[...] += jnp.sum(diff * diff, axis=0, keepdims=True)
        do = diff * (1.0 / d)
        dx2, gnf = _rms_bwd(do, xh, r, nf)
        gnf_ref[...] += gnf
        dx2_ref[...] = dx2

        @pl.when(pl.program_id(0) == nt - 1)
        def _():
            total = jnp.sum(acc_s[...], axis=-1, keepdims=True) * (0.5 / d)
            loss_ref[...] = jnp.broadcast_to(total, loss_ref.shape)

    row = lambda i: (i, 0)
    return _pcall(
        body,
        name="layer_b_fwd",
        grid=(nt,),
        in_specs=[
            pl.BlockSpec((tm, d), row),
            _full(nw.shape),
            _full(bin_w.shape),
            _full(cw.shape),
            _full(cb.shape),
            _full(gab.shape),
            _full(gb.shape),
            _full(lam.shape),
            _full(bout.shape),
            _full(nf.shape),
            pl.BlockSpec((tm, d), row),
        ],
        out_specs=[
            pl.BlockSpec((tm, 2 * bw), row),
            pl.BlockSpec((tm, bw), row),
            pl.BlockSpec((tm, d), row),
            pl.BlockSpec((None, SUBLANES, bw), lambda i: (i, 0, 0)),
            pl.BlockSpec((None, SUBLANES, bw), lambda i: (i, 0, 0)),
            pl.BlockSpec((tm, d), row),
            _full((1, LANES)),
            _full((1, d)),
        ],
        out_shape=[
            jax.ShapeDtypeStruct((t_rows, 2 * bw), F32),
            jax.ShapeDtypeStruct((t_rows, bw), F32),
            jax.ShapeDtypeStruct((t_rows, d), BF16),
            jax.ShapeDtypeStruct((nt, SUBLANES, bw), F32),
            jax.ShapeDtypeStruct((nt, SUBLANES, bw), F32),
            jax.ShapeDtypeStruct((t_rows, d), F32),
            jax.ShapeDtypeStruct((1, LANES), F32),
            jax.ShapeDtypeStruct((1, d), F32),
        ],
        scratch_shapes=[
            pltpu.VMEM((SUBLANES, bw), F32),
            pltpu.VMEM((SUBLANES, bw), F32),
            pltpu.VMEM((tm, bw), F32),
            pltpu.VMEM((tm, bw), F32),
            pltpu.VMEM((tm, bw), F32),
            pltpu.VMEM((1, d), F32),
        ],
        compiler_params=_cparams(("arbitrary",)),
    )(x1, nw, bin_w, cw, cb, gab, gb, lam, bout, nf, tgt)


def _layer_b_bwd(dout, x1, z, hseq, xb_tails, h_tails, nw, bin_w, cw, cb, gab, gabt, gb, lam, bout, tm):
    t_rows, d = x1.shape
    bw = bout.shape[0]
    hd = bw // B_HEADS
    nt = t_rows // tm

    def body(
        dout_ref, x1_ref, z_ref, h_ref, xbt_ref, ht_ref, nw_ref, bin_ref, cw_ref, cb_ref, gab_ref, gabt_ref,
        gb_ref, lam_ref, bout_ref,
        dx1_ref, dz_ref, y_ref, dob_ref, ggab_ref, ggb_ref, gcw_ref, gcb_ref, glam_ref, gnw_ref,
        gcarry_s, afirst_s, head_s, aup_s, dh_s, gt_s, dxc_s, xc_s,
    ):
        step = pl.program_id(0)
        tile = nt - 1 - step

        @pl.when(step == 0)
        def _():
            for ref in (ggab_ref, ggb_ref, gcw_ref, gcb_ref, glam_ref, gnw_ref, gcarry_s, afirst_s, head_s):
                ref[...] = jnp.zeros_like(ref)

        first_tile = tile == 0
        xb_halo = jnp.where(first_tile, 0.0, xbt_ref[...])
        h_halo = jnp.where(first_tile, 0.0, ht_ref[...])

        dout = dout_ref[...]
        dob = dout.astype(BF16)
        dob_ref[...] = dob
        dy = _dot_nt(dob, bout_ref[...])
        hs = h_ref[...]
        g = z_ref[:, bw:]
        sg, dsg = _silu_and_grad(g)
        y_ref[...] = (hs * sg).astype(BF16)
        dz_ref[:, bw:] = (dy * hs * dsg).astype(BF16)
        dh_s[...] = dy * sg

        xb = z_ref[:, :bw]
        xc = _conv(xb, xb_halo, cw_ref, cb_ref)
        xc_s[...] = xc
        lam = lam_ref[...]
        sp = _softplus_neg(lam)
        for h in range(B_HEADS):
            cols = slice(h * hd, (h + 1) * hd)
            _, _, a, _ = _gates(xc[:, cols], gab_ref, gb_ref, sp[:, cols], h, hd)
            aup_s[:, cols] = _shift_up(a, afirst_s[:, cols], 1)
            afirst_s[:, cols] = jnp.broadcast_to(a[0:1, :], (SUBLANES, hd))
        carry = _scan_blocks(aup_s, dh_s, gt_s, gcarry_s[...], tm, reverse=True)
        gcarry_s[...] = carry

        h_prev = _shift_down(hs, h_halo, 1)
        for h in range(B_HEADS):
            cols = slice(h * hd, (h + 1) * hd)
            xc_h = xc_s[:, cols]
            sp_h = sp[:, cols]
            r, ig, a, mult = _gates(xc_h, gab_ref, gb_ref, sp_h, h, hd)
            gt = gt_s[:, cols]
            da = gt * h_prev[:, cols]
            dmult = gt * (ig * xc_h)
            dig = gt * (mult * xc_h)
            dxc_direct = gt * (mult * ig)
            dla = da * a - dmult * (a * a) / mult
            glam_ref[:, cols] += jnp.sum(dla * r, axis=0, keepdims=True)
            dr = dla * ((-RG_C) * sp_h)
            dpre = jnp.concatenate([dr * r * (1.0 - r), dig * ig * (1.0 - ig)], axis=1)
            ggb_ref[:, cols] += jnp.sum(dpre[:, :hd], axis=0, keepdims=True)
            ggb_ref[:, bw + h * hd : bw + (h + 1) * hd] += jnp.sum(dpre[:, hd:], axis=0, keepdims=True)
            dpb = dpre.astype(BF16)
            ggab_ref[h] += _dot_tn(xc_h.astype(BF16), dpb)
            dxc_s[:, cols] = dxc_direct + _dot(dpb, gabt_ref[h])
        glam_ref[...] = jnp.where(step == nt - 1, glam_ref[...] * (RG_C * _sigmoid(-lam)), glam_ref[...])

        dxc = dxc_s[...]
        gcb_ref[...] += jnp.sum(dxc, axis=0, keepdims=True)
        dxb = cw_ref[CONV_WIDTH - 1 : CONV_WIDTH, :] * dxc
        gcw_ref[CONV_WIDTH - 1 : CONV_WIDTH, :] += jnp.sum(dxc * xb, axis=0, keepdims=True)
        head = head_s[...]
        for k in range(CONV_WIDTH - 1):
            lag = CONV_WIDTH - 1 - k
            dxb = dxb + cw_ref[k : k + 1, :] * _shift_up(dxc, head, lag)
            gcw_ref[k : k + 1, :] += jnp.sum(dxc * _shift_down(xb, xb_halo, lag), axis=0, keepdims=True)
        head_s[...] = dxc[:SUBLANES, :]
        dz_ref[:, :bw] = dxb.astype(BF16)

        s_cols = 2 * bw // N_CHIPS
        dh1 = jnp.zeros((tm, d), F32)
        for k in range(N_CHIPS):
            dh1 = dh1 + _dot_nt(dz_ref[:, k * s_cols : (k + 1) * s_cols], bin_ref[k])
        x1 = x1_ref[...]
        nw = nw_ref[...]
        _, xh, r1 = _rms_fwd(x1, nw)
        dx, gnw = _rms_bwd(dh1, xh, r1, nw)
        gnw_ref[...] += gnw
        dx1_ref[...] = dout + dx

    rev = lambda i: (nt - 1 - i, 0)
    prev = lambda i: (jnp.maximum(nt - 2 - i, 0), 0, 0)
    return _pcall(
        body,
        name="layer_b_bwd",
        grid=(nt,),
        in_specs=[
            pl.BlockSpec((tm, d), rev),
            pl.BlockSpec((tm, d), rev),
            pl.BlockSpec((tm, 2 * bw), rev),
            pl.BlockSpec((tm, bw), rev),
            pl.BlockSpec((None, SUBLANES, bw), prev),
            pl.BlockSpec((None, SUBLANES, bw), prev),
            _full(nw.shape),
            _full(bin_w.shape),
            _full(cw.shape),
            _full(cb.shape),
            _full(gab.shape),
            _full(gabt.shape),
            _full(gb.shape),
            _full(lam.shape),
            _full(bout.shape),
        ],
        out_specs=[
            pl.BlockSpec((tm, d), rev),
            pl.BlockSpec((tm, 2 * bw), rev),
            pl.BlockSpec((tm, bw), rev),
            pl.BlockSpec((tm, d), rev),
            _full((B_HEADS, hd, 2 * hd)),
            _full((1, 2 * bw)),
            _full((SUBLANES, bw)),
            _full((1, bw)),
            _full((1, bw)),
            _full((1, d)),
        ],
        out_shape=[
            jax.ShapeDtypeStruct((t_rows, d), F32),
            jax.ShapeDtypeStruct((t_rows, 2 * bw), BF16),
            jax.ShapeDtypeStruct((t_rows, bw), BF16),
            jax.ShapeDtypeStruct((t_rows, d), BF16),
            jax.ShapeDtypeStruct((B_HEADS, hd, 2 * hd), F32),
            jax.ShapeDtypeStruct((1, 2 * bw), F32),
            jax.ShapeDtypeStruct((SUBLANES, bw), F32),
            jax.ShapeDtypeStruct((1, bw), F32),
            jax.ShapeDtypeStruct((1, bw), F32),
            jax.ShapeDtypeStruct((1, d), F32),
        ],
        scratch_shapes=[pltpu.VMEM((SUBLANES, bw), F32)] * 3 + [pltpu.VMEM((tm, bw), F32)] * 5,
        compiler_params=_cparams(("arbitrary",)),
    )(dout, x1, z, hseq, xb_tails, h_tails, nw, bin_w, cw, cb, gab, gabt, gb, lam, bout)


def _wgrad(a, b, m_blocks, n_blocks, hook=None):
    k, m = a.shape
    n = b.shape[1]
    bm, bn = m // m_blocks, n // n_blocks

    def body(a_ref, b_ref, o_ref):
        o_ref[...] = _dot_tn(a_ref[...], b_ref[...])

    out = _pcall(
        body,
        hook,
        name=f"wgrad_{m}x{n}",
        grid=(n_blocks, m_blocks),
        in_specs=[pl.BlockSpec((k, bm), lambda j, i: (0, i)), pl.BlockSpec((k, bn), lambda j, i: (0, j))],
        out_specs=[pl.BlockSpec((None, None, bm, bn), lambda j, i: (j, i, 0, 0))],
        out_shape=[jax.ShapeDtypeStruct((n_blocks, m_blocks, bm, bn), F32)],
        compiler_params=_cparams(("arbitrary", "arbitrary")),
    )(a, b)
    return out[0] if hook is None else (out[0][0], out[1])


def _adamw_math(w, g, m, v):
    m = ADAM_B1 * m + (1.0 - ADAM_B1) * g
    v = ADAM_B2 * v + (1.0 - ADAM_B2) * (g * g)
    m_hat = m / (1.0 - ADAM_B1**ADAM_STEP)
    v_hat = v / (1.0 - ADAM_B2**ADAM_STEP)
    delta = -ADAM_LR * (m_hat / (jnp.sqrt(v_hat) + ADAM_EPS) + ADAM_WD * w)
    return delta, m, v


def _adamw(w, g, m, v):
    rows, cols = w.shape
    tr = _row_tile(rows, cols, 1024 * 1024)

    def body(w_ref, g_ref, m_ref, v_ref, d_ref, mo_ref, vo_ref):
        d_ref[...], mo_ref[...], vo_ref[...] = _adamw_math(w_ref[...], g_ref[...], m_ref[...], v_ref[...])

    spec = pl.BlockSpec((tr, cols), lambda i: (i, 0))
    return _pcall(
        body,
        name=f"adamw_{rows}x{cols}",
        grid=(rows // tr,),
        in_specs=[spec] * 4,
        out_specs=[spec] * 3,
        out_shape=[jax.ShapeDtypeStruct((rows, cols), F32)] * 3,
        compiler_params=_cparams(("arbitrary",)),
    )(w, g, m, v)


def _sum_partials(parts):
    def body(p_ref, o_ref):
        total = p_ref[0, 0:1, :]
        for k in range(1, N_DEV):
            total = total + p_ref[k, 0:1, :]
        o_ref[...] = total

    vmem = pl.BlockSpec(memory_space=pltpu.VMEM)
    return _pcall(
        body,
        name="sum_partials",
        in_specs=[vmem],
        out_specs=vmem,
        out_shape=jax.ShapeDtypeStruct((1, parts.shape[2]), F32),
    )(parts)


def _adamw_many(ws, gs, ms, vs, name, hook=None):
    n = len(ws)

    def body(*refs):
        w_refs, g_refs, m_refs, v_refs = (refs[i * n : (i + 1) * n] for i in range(4))
        d_refs, mo_refs, vo_refs = (refs[(4 + i) * n : (5 + i) * n] for i in range(3))
        for i in range(n):
            d_refs[i][...], mo_refs[i][...], vo_refs[i][...] = _adamw_math(
                w_refs[i][...], g_refs[i][...], m_refs[i][...], v_refs[i][...]
            )

    vmem = pl.BlockSpec(memory_space=pltpu.VMEM)
    outs = _pcall(
        body,
        hook,
        name=name,
        in_specs=[vmem] * (4 * n),
        out_specs=[vmem] * (3 * n),
        out_shape=[jax.ShapeDtypeStruct(w.shape, F32) for w in ws] * 3,
        compiler_params=_cparams(),
    )(*ws, *gs, *ms, *vs)
    extra = None
    if hook is not None:
        outs, extra = outs
    return (outs[:n], outs[n : 2 * n], outs[2 * n :]), extra


def _pack_rows(parts, lanes=LANES):
    flat = jnp.concatenate([p.reshape(-1) for p in parts])
    per = N_DEV * SUBLANES * lanes
    total = -(-flat.shape[0] // per) * per
    flat = jnp.pad(flat, (0, total - flat.shape[0]))
    return flat.reshape(N_DEV, total // (N_DEV * lanes), lanes)


def _unpack(flat, shapes):
    out, at = [], 0
    for s in shapes:
        n = 1
        for dim in s:
            n *= dim
        out.append(flat[at : at + n].reshape(s))
        at += n
    return out


def kernel(x, norm_w, a_w_in, a_ln_w, a_ln_b, a_w_s, a_b_s, a_w_out, b_w_in, b_conv_w, b_conv_b, b_gate_a_w, b_gate_a_b, b_gate_x_w, b_gate_x_b, b_lambda, b_w_out, norm_f_w, loss_target, m_norm_w, m_a_w_in, m_a_ln_w, m_a_ln_b, m_a_w_s, m_a_b_s, m_a_w_out, m_b_w_in, m_b_conv_w, m_b_conv_b, m_b_gate_a_w, m_b_gate_a_b, m_b_gate_x_w, m_b_gate_x_b, m_b_lambda, m_b_w_out, m_norm_f_w, v_norm_w, v_a_w_in, v_a_ln_w, v_a_ln_b, v_a_w_s, v_a_b_s, v_a_w_out, v_b_w_in, v_b_conv_w, v_b_conv_b, v_b_gate_a_w, v_b_gate_a_b, v_b_gate_x_w, v_b_gate_x_b, v_b_lambda, v_b_w_out, v_norm_f_w):
    t_rows, d = x.shape[1], x.shape[2]
    aw = a_ln_w.shape[1]
    bw = b_gate_a_w.shape[1] * b_gate_a_w.shape[2]
    hd = bw // B_HEADS
    mine = 2 * lax.axis_index("x") + lax.axis_index("y")
    core = lax.axis_index("c")
    weights = dict(norm_w=norm_w, a_w_in=a_w_in, a_ln_w=a_ln_w, a_ln_b=a_ln_b, a_w_s=a_w_s, a_b_s=a_b_s, a_w_out=a_w_out, b_w_in=b_w_in, b_conv_w=b_conv_w, b_conv_b=b_conv_b, b_gate_a_w=b_gate_a_w, b_gate_a_b=b_gate_a_b, b_gate_x_w=b_gate_x_w, b_gate_x_b=b_gate_x_b, b_lambda=b_lambda, b_w_out=b_w_out, norm_f_w=norm_f_w)
    m_in = dict(norm_w=m_norm_w, a_w_in=m_a_w_in, a_ln_w=m_a_ln_w, a_ln_b=m_a_ln_b, a_w_s=m_a_w_s, a_b_s=m_a_b_s, a_w_out=m_a_w_out, b_w_in=m_b_w_in, b_conv_w=m_b_conv_w, b_conv_b=m_b_conv_b, b_gate_a_w=m_b_gate_a_w, b_gate_a_b=m_b_gate_a_b, b_gate_x_w=m_b_gate_x_w, b_gate_x_b=m_b_gate_x_b, b_lambda=m_b_lambda, b_w_out=m_b_w_out, norm_f_w=m_norm_f_w)
    v_in = dict(norm_w=v_norm_w, a_w_in=v_a_w_in, a_ln_w=v_a_ln_w, a_ln_b=v_a_ln_b, a_w_s=v_a_w_s, a_b_s=v_a_b_s, a_w_out=v_a_w_out, b_w_in=v_b_w_in, b_conv_w=v_b_conv_w, b_conv_b=v_b_conv_b, b_gate_a_w=v_b_gate_a_w, b_gate_a_b=v_b_gate_a_b, b_gate_x_w=v_b_gate_x_w, b_gate_x_b=v_b_gate_x_b, b_lambda=v_b_lambda, b_w_out=v_b_w_out, norm_f_w=v_norm_f_w)

    win_l = _cast_to_segments(a_w_in[0], mine, 256)
    wout_l = _cast_to_segments(a_w_out[0], mine, 256)
    bin_l = _cast_to_segments(b_w_in[0], mine, 256)
    bout_l = _cast_to_segments(b_w_out[0], mine, 192)
    small_l = jnp.concatenate([b_conv_w[0], b_conv_b, b_gate_a_b, b_gate_x_b, b_lambda], axis=0)
    win_g, wout_g, small_g = _run_hook(_gather_hook([win_l, wout_l], small_l), "gather_layer_a")
    win = win_g.reshape(N_CHIPS, d, -1)
    wout = wout_g.reshape(aw, d)

    tril = jnp.tril(jnp.ones((CHUNK, CHUNK), F32))
    wc = (a_w_s[0] * tril[None]).astype(BF16)
    wct = jnp.swapaxes(wc, 1, 2)
    bs_t = a_b_s[0].T
    gab = jnp.concatenate([b_gate_a_w[0], b_gate_x_w[0]], axis=2).astype(BF16)
    gabt = jnp.swapaxes(gab, 1, 2)
    nw0, nw1, nf = norm_w[0:1], norm_w[1:2], norm_f_w.reshape(1, d)

    x0 = x[0]
    (z_a, x1, h0), (bin_g, bout_g) = _layer_a_fwd(
        x0, nw0, win, a_ln_w, a_ln_b, wc, bs_t, wout, TM_FWD, _gather_hook([bin_l, bout_l])
    )
    bin_w = bin_g.reshape(N_CHIPS, d, -1)
    bout = bout_g.reshape(bw, d)
    small_f = jnp.transpose(small_g, (1, 0, 2)).reshape(SUBLANES, bw)
    cw, cb = small_f[0:CONV_WIDTH], small_f[CONV_WIDTH : CONV_WIDTH + 1]
    gb = jnp.concatenate([small_f[5:6], small_f[6:7]], axis=1)
    lam = small_f[7:8]
    z_b, hseq, h1, xb_tails, h_tails, dx2, loss_l, g_nf = _layer_b_fwd(
        x1, nw1, bin_w, cw, cb, gab, gb, lam, bout, nf, loss_target[0], TM_FWD
    )
    dx1, dz_b, y_b, dob_b, g_gab, g_gb, g_cw, g_cb, g_lam, g_nw1 = _layer_b_bwd(
        dx2, x1, z_b, hseq, xb_tails, h_tails, nw1, bin_w, cw, cb, gab, gabt, gb, lam, bout, TM_FWD
    )
    seg = lambda g: g.reshape(N_DEV, -1, g.shape[3])
    g_bout = seg(_wgrad(y_b, dob_b, N_CHIPS, 1))
    g_bin = seg(_wgrad(h1, dz_b, 2, N_CHIPS))

    layer_b = [g_bin, g_bout]
    a_args = (z_a, a_ln_w, a_ln_b, wc, wct, bs_t, wout)
    half = t_rows // TM_A_BWD // 2
    first, got_a = _layer_a_bwd(dx1, *a_args, (0, half), None, _swap_hook(layer_b))
    parts = [_add_own_half(b, g, core, BF16) for b, g in zip(layer_b, got_a)]
    second, got_b = _layer_a_bwd(dx1, *a_args, (half, 2 * half), first[:3], _send_hook(parts))
    dz_a, y_a, dob_a = second[:3]
    g_ws, g_bst, g_lnw, g_lnb = (p + q for p, q in zip(first[3:], second[3:]))
    reduced = [_add_received(b, ga, gb_, mine, core, core, 2) for b, ga, gb_ in zip(layer_b, got_a, got_b)]
    g_wout, (gr_bin, gr_bout) = _wgrad(y_a, dob_a, N_DEV, 1, _share_hook(reduced))
    g_wout = seg(g_wout)
    g_win = seg(_wgrad(h0, dz_a, 2, N_CHIPS))
    gr_bin = gr_bin.reshape(b_w_in.shape[1:])
    gr_bout = gr_bout.reshape(b_w_out.shape[1:])

    small_shapes = [
        (1, d), (1, aw), (1, aw), (A_GROUPS, CHUNK, CHUNK), (A_GROUPS, CHUNK), (B_HEADS, hd, hd), (B_HEADS, hd, hd),
        (d,), (CONV_WIDTH, bw), (1, bw), (1, bw), (1, bw), (1, bw),
    ]
    small = _pack_rows(
        [
            g_nw1, g_lnw, g_lnb, g_ws, g_bst.T, g_gab[:, :, :hd], g_gab[:, :, hd:],
            g_nf, g_cw[:CONV_WIDTH], g_cb, g_gb[:, :bw], g_gb[:, bw:], g_lam,
        ]
    )

    layer_a = [g_win, g_wout, small]
    wires = [BF16, BF16, F32]
    (grad_x, g_nw0_mine), got_a = _layer_a_bwd_dx(dx1, x0, dz_a, nw0, win, TM_A_DX, _swap_hook(layer_a))
    parts = [_add_own_half(b, g, core, w) for b, g, w in zip(layer_a, got_a, wires)]
    b_names = ("b_w_in", "b_w_out")
    b_grads = {"b_w_in": gr_bin, "b_w_out": gr_bout}
    two_d = lambda a: a.reshape(a.shape[-2:])
    b_out, got_b = _adamw_many(
        [two_d(weights[n]) for n in b_names], [b_grads[n] for n in b_names], [two_d(m_in[n]) for n in b_names],
        [two_d(v_in[n]) for n in b_names], "adamw_layer_b", _send_hook(parts),
    )
    reduced = [_add_received(b, ga, gb_, mine, core, core, 2) for b, ga, gb_ in zip(layer_a[:2], got_a, got_b)]
    reduced_small = _add_received(small, got_a[2], got_b[2], mine, core, 2 * mine + core, N_DEV)
    gr_win, gr_wout, small_r, g_nw0_all = _run_hook(_share_hook(reduced, reduced_small, g_nw0_mine), "share_reduced")
    g_nw0 = _sum_partials(g_nw0_all)

    gr_win = gr_win.reshape(a_w_in.shape[1:])
    gr_wout = gr_wout.reshape(a_w_out.shape[1:])
    (g_nw1_r, g_a_ln_w, g_a_ln_b, g_a_w_s, g_a_b_s, g_gate_a_w, g_gate_x_w, g_norm_f, gf_cw, gf_cb, gf_gab, gf_gxb,
     gf_lam) = _unpack(small_r.reshape(-1), small_shapes)
    g_norm_w = jnp.concatenate([g_nw0, g_nw1_r], axis=0)
    shard = lambda g: lax.dynamic_slice_in_dim(g, mine * (bw // N_CHIPS), bw // N_CHIPS, axis=1)

    grads = {
        "norm_w": g_norm_w, "a_w_in": gr_win[None], "a_ln_w": g_a_ln_w, "a_ln_b": g_a_ln_b, "a_w_s": g_a_w_s[None],
        "a_b_s": g_a_b_s[None], "a_w_out": gr_wout[None], "b_w_in": gr_bin[None], "b_conv_w": shard(gf_cw)[None],
        "b_conv_b": shard(gf_cb), "b_gate_a_w": g_gate_a_w[None], "b_gate_a_b": shard(gf_gab),
        "b_gate_x_w": g_gate_x_w[None], "b_gate_x_b": shard(gf_gxb), "b_lambda": shard(gf_lam),
        "b_w_out": gr_bout[None], "norm_f_w": g_norm_f,
    }
    names = list(weights)
    delta, new_m, new_v = {}, {}, {}
    for n in ("a_w_in", "a_w_out"):
        dl, mo, vo = _adamw(two_d(weights[n]), two_d(grads[n]), two_d(m_in[n]), two_d(v_in[n]))
        delta[n], new_m[n], new_v[n] = dl, mo, vo
    small_names = [n for n in names if n not in ("a_w_in", "a_w_out") + b_names]
    at_least_2d = lambda a: a.reshape(1, -1) if a.ndim == 1 else a
    small_out, _ = _adamw_many(
        *[[at_least_2d(src[n]) for n in small_names] for src in (weights, grads, m_in, v_in)], "adamw_small"
    )
    for dst, vals, b_vals in zip((delta, new_m, new_v), small_out, b_out):
        dst.update(zip(small_names, vals))
        dst.update(zip(b_names, b_vals))
    for dst in (delta, new_m, new_v):
        for n in names:
            dst[n] = dst[n].reshape(weights[n].shape)

    loss = lax.psum(loss_l[0, 0], ("x", "y", "c"))
    return (
        loss,
        grad_x[None],
        *[grads[n] for n in names],
        *[delta[n] for n in names],
        *[new_m[n] for n in names],
        *[new_v[n] for n in names],
    )
```

```python
import jax
import jax.numpy as jnp
from jax import lax
from jax.experimental import pallas as pl
from jax.experimental.pallas import tpu as pltpu

F32 = jnp.float32
BF16 = jnp.bfloat16

RMS_EPS = 1e-6
LN_EPS = 1e-5
RG_C = 8.0
CHUNK = 128
A_GROUPS = 8
B_HEADS = 12
CONV_WIDTH = 4

ADAM_LR = 0.001
ADAM_B1 = 0.9
ADAM_B2 = 0.999
ADAM_EPS = 1e-08
ADAM_WD = 0.01
ADAM_STEP = 10

N_CHIPS = 4
N_DEV = 8
SUBLANES = 8
LANES = 128
V7X_VMEM_BYTES = 64 * 1024 * 1024
VMEM_LIMIT = V7X_VMEM_BYTES * 7 // 8
MESH = pl.DeviceIdType.MESH
ANY = pl.BlockSpec(memory_space=pl.ANY)

TM_FWD = 256
TM_A_BWD = 256
TM_A_DX = 512

GELU_C0 = 0.7978845608028654
GELU_C1 = 0.044715


class _Hook:
    def __init__(self, operands, out_shapes, aliases, n_sems, start, finish):
        self.operands, self.out_shapes, self.aliases, self.n_sems = operands, out_shapes, aliases, n_sems
        self.start, self.finish = start, finish


class _SemView:
    def __init__(self, base, off):
        self.base, self.off = base, off

    @property
    def at(self):
        return self

    def __getitem__(self, k):
        return self.base.at[self.off + k]


def _join_hooks(*hooks):
    if len(hooks) == 1:
        return hooks[0]
    operands, out_shapes, aliases, spans = [], [], {}, []
    n_sems = 0
    for h in hooks:
        aliases.update({len(operands) + i: len(out_shapes) + o for i, o in h.aliases.items()})
        spans.append((len(operands), len(h.operands), len(out_shapes), len(h.out_shapes), n_sems))
        operands += list(h.operands)
        out_shapes += list(h.out_shapes)
        n_sems += h.n_sems

    def each(which):
        def run(ins, outs, send, recv):
            for h, (i0, ni, o0, no, s0) in zip(hooks, spans):
                getattr(h, which)(ins[i0 : i0 + ni], outs[o0 : o0 + no], _SemView(send, s0), _SemView(recv, s0))

        return run

    return _Hook(operands, out_shapes, aliases, n_sems, each("start"), each("finish"))


def _pcall(body, hook=None, **kw):
    if hook is None:
        return pl.pallas_call(body, **kw)
    n_in, n_out = len(kw["in_specs"]), len(kw["out_shape"])
    hi, ho = len(hook.operands), len(hook.out_shapes)
    grid = kw.get("grid", ())

    def wrapped(*refs):
        ins, h_in = refs[:n_in], refs[n_in : n_in + hi]
        outs = refs[n_in + hi : n_in + hi + n_out]
        h_out = refs[n_in + hi + n_out : n_in + hi + n_out + ho]
        scratch = refs[n_in + hi + n_out + ho : -2]
        send_sems, recv_sems = refs[-2:]
        if not grid:
            hook.start(h_in, h_out, send_sems, recv_sems)
            body(*ins, *outs, *scratch)
            hook.finish(h_in, h_out, send_sems, recv_sems)
            return
        first = pl.program_id(0) == 0
        last = pl.program_id(0) == grid[0] - 1
        for axis in range(1, len(grid)):
            first = jnp.logical_and(first, pl.program_id(axis) == 0)
            last = jnp.logical_and(last, pl.program_id(axis) == grid[axis] - 1)

        @pl.when(first)
        def _():
            hook.start(h_in, h_out, send_sems, recv_sems)

        body(*ins, *outs, *scratch)

        @pl.when(last)
        def _():
            hook.finish(h_in, h_out, send_sems, recv_sems)

    aliases = dict(kw.pop("input_output_aliases", {}))
    aliases.update({n_in + i: n_out + o for i, o in hook.aliases.items()})
    kw.update(
        in_specs=list(kw["in_specs"]) + [ANY] * hi,
        out_specs=list(kw["out_specs"]) + [ANY] * ho,
        out_shape=list(kw["out_shape"]) + list(hook.out_shapes),
        scratch_shapes=list(kw.get("scratch_shapes", ()))
        + [pltpu.SemaphoreType.DMA((hook.n_sems,)), pltpu.SemaphoreType.DMA((hook.n_sems,))],
        input_output_aliases=aliases,
    )
    call = pl.pallas_call(wrapped, **kw)

    def run(*operands):
        outs = call(*operands, *hook.operands)
        return outs[:n_out], outs[n_out:]

    return run


def _run_hook(hook, name):
    def body():
        pass

    return _pcall(body, hook, name=name, in_specs=[], out_specs=[], out_shape=[])()[1]


def _cparams(sem=None):
    return pltpu.CompilerParams(dimension_semantics=sem, vmem_limit_bytes=VMEM_LIMIT)


def _full(shape):
    zeros = (0,) * len(shape)
    return pl.BlockSpec(shape, lambda *_: zeros)


def _scalars(*vals):
    return jnp.stack([jnp.asarray(v, jnp.int32) for v in vals])


def _sigmoid(x):
    return 1.0 / (1.0 + jnp.exp(-x))


def _gelu(x):
    t = jnp.tanh(GELU_C0 * (x + GELU_C1 * (x * x * x)))
    return x * (0.5 * (1.0 + t))


def _gelu_and_grad(x):
    x2 = x * x
    t = jnp.tanh(GELU_C0 * (x + GELU_C1 * (x2 * x)))
    cdf = 0.5 * (1.0 + t)
    return x * cdf, cdf + 0.5 * x * (1.0 - t * t) * (GELU_C0 * (1.0 + 3.0 * GELU_C1 * x2))


def _silu_and_grad(x):
    s = _sigmoid(x)
    return x * s, s * (1.0 + x * (1.0 - s))


def _softplus_neg(lam):
    u = jnp.exp(-jnp.abs(lam))
    w = 1.0 + u
    log1p = jnp.where(w == 1.0, u, jnp.log(w) * (u / jnp.where(w == 1.0, 1.0, w - 1.0)))
    return jnp.maximum(-lam, 0.0) + log1p


def _dot(a, b):
    return jnp.dot(a, b, preferred_element_type=F32)


def _dot_nt(a, b):
    return lax.dot_general(a, b, (((1,), (1,)), ((), ())), preferred_element_type=F32)


def _dot_tn(a, b):
    return lax.dot_general(a, b, (((0,), (0,)), ((), ())), preferred_element_type=F32)


def _shift_down(v, halo, k):
    if k == 0:
        return v
    rolled = pltpu.roll(v, k, 0)
    row = lax.broadcasted_iota(jnp.int32, (SUBLANES, v.shape[1]), 0)
    top = jnp.where(row < k, pltpu.roll(halo, k, 0), rolled[:SUBLANES])
    return jnp.concatenate([top, rolled[SUBLANES:]], axis=0)


def _shift_up(v, head, k):
    if k == 0:
        return v
    n = v.shape[0]
    rolled = pltpu.roll(v, n - k, 0)
    row = lax.broadcasted_iota(jnp.int32, (SUBLANES, v.shape[1]), 0)
    bot = jnp.where(row >= SUBLANES - k, pltpu.roll(head, SUBLANES - k, 0), rolled[n - SUBLANES :])
    return jnp.concatenate([rolled[: n - SUBLANES], bot], axis=0)


def _scan_blocks(a_ref, b_ref, out_ref, carry, n_rows, reverse):
    width = a_ref.shape[1]
    row = lax.broadcasted_iota(jnp.int32, (SUBLANES, width), 0)
    n_blocks = n_rows // SUBLANES

    def block(j, carry):
        i = (n_blocks - 1 - j) if reverse else j
        r0 = pl.multiple_of(i * SUBLANES, SUBLANES)
        a = a_ref[pl.ds(r0, SUBLANES), :]
        b = b_ref[pl.ds(r0, SUBLANES), :]
        for d in (1, 2, 4):
            shift = (SUBLANES - d) if reverse else d
            keep = (row < SUBLANES - d) if reverse else (row >= d)
            a_s = pltpu.roll(a, shift, 0)
            b_s = pltpu.roll(b, shift, 0)
            b = jnp.where(keep, a * b_s + b, b)
            a = jnp.where(keep, a * a_s, a)
        h = a * carry + b
        out_ref[pl.ds(r0, SUBLANES), :] = h
        edge = h[0:1, :] if reverse else h[SUBLANES - 1 : SUBLANES, :]
        return jnp.broadcast_to(edge, (SUBLANES, width))

    return lax.fori_loop(0, n_blocks, block, carry)


def _rms_fwd(x, w):
    r = lax.rsqrt(jnp.mean(x * x, axis=-1, keepdims=True) + RMS_EPS)
    xh = x * r
    return xh * w, xh, r


def _rms_bwd(dh, xh, r, w):
    dxh = dh * w
    dx = r * (dxh - xh * jnp.mean(dxh * xh, axis=-1, keepdims=True))
    return dx, jnp.sum(dh * xh, axis=0, keepdims=True)


def _cast_to_segments(w, mine, rows):
    n, c = w.shape
    per = n // 2 // rows

    def body(k_ref, w_ref, o_ref):
        o_ref[...] = w_ref[...].astype(BF16)

    return _pcall(
        body,
        name=f"cast_{n}x{c}",
        grid_spec=pltpu.PrefetchScalarGridSpec(
            num_scalar_prefetch=1,
            grid=(n // rows,),
            in_specs=[pl.BlockSpec((rows, c), lambda i, k_ref: (i, 0))],
            out_specs=pl.BlockSpec((None, rows, c), lambda i, k_ref: (2 * k_ref[0] + i // per, i % per, 0)),
        ),
        out_shape=jax.ShapeDtypeStruct((N_DEV, n // 2, c), BF16),
        compiler_params=_cparams(("arbitrary",)),
    )(_scalars(mine), w)


def _place():
    x, y, c = lax.axis_index("x"), lax.axis_index("y"), lax.axis_index("c")
    chips = [(1 - x, y), (x, 1 - y), (1 - x, 1 - y)]
    return x, y, c, chips


def _chip_no(chip):
    return 2 * chip[0] + chip[1]


def _rcopy(src, dst, send_sem, recv_sem, to):
    return pltpu.make_async_remote_copy(
        src_ref=src, dst_ref=dst, send_sem=send_sem, recv_sem=recv_sem, device_id=to, device_id_type=MESH
    )


def _gather_hook(big, small=None):
    nb = len(big)
    n_sems = 6 * nb + 4

    def ici_copies(ins, outs, send, recv):
        x, y, c, chips = _place()
        copies = []
        for b in range(nb):
            own = outs[b].at[2 * _chip_no((x, y)) + c]
            for j, chip in enumerate(chips):
                copies.append(_rcopy(own, own, send.at[6 * b + j], recv.at[6 * b + j], (*chip, c)))
        if small is not None:
            there = outs[nb].at[_chip_no((x, y))]
            for j, chip in enumerate(chips):
                k = 6 * nb + j
                copies.append(_rcopy(ins[nb], there, send.at[k], recv.at[k], (*chip, c)))
        return copies

    def local_copy(ins, outs, send):
        x, y, _, _ = _place()
        return pltpu.make_async_copy(ins[nb], outs[nb].at[_chip_no((x, y))], send.at[6 * nb + 3])

    def start(ins, outs, send, recv):
        for cp in ici_copies(ins, outs, send, recv):
            cp.start()
        if small is not None:
            local_copy(ins, outs, send).start()

    def finish(ins, outs, send, recv):
        x, y, c, chips = _place()
        me, sibling = (x, y, c), (x, y, 1 - c)
        passed = []
        for b in range(nb):
            for j, chip in enumerate(chips):
                got = outs[b].at[2 * _chip_no(chip) + c]
                _rcopy(got, got, send.at[6 * b + j], recv.at[6 * b + j], me).wait_recv()
                fwd = _rcopy(got, got, send.at[6 * b + 3 + j], recv.at[6 * b + 3 + j], sibling)
                fwd.start()
                passed.append(fwd)
        if small is not None:
            for j, chip in enumerate(chips):
                k = 6 * nb + j
                got = outs[nb].at[_chip_no(chip)]
                _rcopy(got, got, send.at[k], recv.at[k], me).wait_recv()
        for b in range(nb):
            for j, chip in enumerate(chips):
                got = outs[b].at[2 * _chip_no(chip) + 1 - c]
                _rcopy(got, got, send.at[6 * b + 3 + j], recv.at[6 * b + 3 + j], me).wait_recv()
        for cp in ici_copies(ins, outs, send, recv) + passed:
            cp.wait_send()
        if small is not None:
            local_copy(ins, outs, send).wait()

    operands = list(big) + ([small] if small is not None else [])
    out_shapes = [jax.ShapeDtypeStruct(b.shape, b.dtype) for b in big]
    if small is not None:
        out_shapes.append(jax.ShapeDtypeStruct((N_CHIPS, *small.shape), small.dtype))
    return _Hook(operands, out_shapes, {b: b for b in range(nb)}, n_sems, start, finish)


def _both_ways_hook(operands, out_shapes, copies_of, n_sems):
    def start(ins, outs, send, recv):
        for cp in copies_of(ins, outs, send, recv):
            cp.start()

    def finish(ins, outs, send, recv):
        for cp in copies_of(ins, outs, send, recv):
            cp.wait()

    return _Hook(operands, out_shapes, {}, n_sems, start, finish)


def _swap_hook(bufs):
    def copies_of(ins, outs, send, recv):
        x, y, c, _ = _place()
        copies = []
        for b in range(len(bufs)):
            for j in range(N_CHIPS):
                k = b * N_CHIPS + j
                copies.append(_rcopy(ins[b].at[2 * j + 1 - c], outs[b].at[j], send.at[k], recv.at[k], (x, y, 1 - c)))
        return copies

    out_shapes = [jax.ShapeDtypeStruct((N_CHIPS, *b.shape[1:]), b.dtype) for b in bufs]
    return _both_ways_hook(list(bufs), out_shapes, copies_of, len(bufs) * N_CHIPS)


def _send_hook(parts):
    def copies_of(ins, outs, send, recv):
        _, _, c, chips = _place()
        copies = []
        for b in range(len(parts)):
            for j, chip in enumerate(chips):
                k = b * 3 + j
                copies.append(_rcopy(ins[b].at[_chip_no(chip)], outs[b].at[j], send.at[k], recv.at[k], (*chip, c)))
        return copies

    out_shapes = [jax.ShapeDtypeStruct((3, *p.shape[1:]), p.dtype) for p in parts]
    return _both_ways_hook(list(parts), out_shapes, copies_of, len(parts) * 3)


def _share_hook(big, small=None, tiny=None):
    nb = len(big)
    n_sems = nb + 7 + N_DEV
    t0 = nb + 7

    def tiny_copies(ins, outs, send, recv):
        x, y, c, _ = _place()
        there = outs[-1].at[2 * _chip_no((x, y)) + c]
        copies = []
        for r in range(1, N_DEV):
            to = (x ^ (r >> 2 & 1), y ^ (r >> 1 & 1), c ^ (r & 1))
            copies.append(_rcopy(ins[-1], there, send.at[t0 + r], recv.at[t0 + r], to))
        return copies

    def tiny_local(ins, outs, send):
        x, y, c, _ = _place()
        return pltpu.make_async_copy(ins[-1], outs[-1].at[2 * _chip_no((x, y)) + c], send.at[t0])

    def first_copies(outs, send, recv):
        x, y, c, chips = _place()
        sibling = (x, y, 1 - c)
        copies = [_rcopy(outs[b].at[c], outs[b].at[c], send.at[b], recv.at[b], sibling) for b in range(nb)]
        if small is not None:
            own = outs[nb].at[2 * _chip_no((x, y)) + c]
            copies.append(_rcopy(own, own, send.at[nb], recv.at[nb], sibling))
            for j, chip in enumerate(chips):
                copies.append(_rcopy(own, own, send.at[nb + 1 + j], recv.at[nb + 1 + j], (*chip, c)))
        return copies

    def start(ins, outs, send, recv):
        for cp in first_copies(outs, send, recv):
            cp.start()
        if tiny is not None:
            for cp in tiny_copies(ins, outs, send, recv):
                cp.start()
            tiny_local(ins, outs, send).start()

    def finish(ins, outs, send, recv):
        x, y, c, chips = _place()
        me, sibling = (x, y, c), (x, y, 1 - c)
        if tiny is not None:
            for cp in tiny_copies(ins, outs, send, recv):
                cp.wait()
            tiny_local(ins, outs, send).wait()
        passed = []
        if small is not None:
            for j, chip in enumerate(chips):
                got = outs[nb].at[2 * _chip_no(chip) + c]
                _rcopy(got, got, send.at[nb + 1 + j], recv.at[nb + 1 + j], me).wait_recv()
                fwd = _rcopy(got, got, send.at[nb + 4 + j], recv.at[nb + 4 + j], sibling)
                fwd.start()
                passed.append(fwd)
        for b in range(nb):
            got = outs[b].at[1 - c]
            _rcopy(got, got, send.at[b], recv.at[b], me).wait_recv()
        if small is not None:
            got = outs[nb].at[2 * _chip_no((x, y)) + 1 - c]
            _rcopy(got, got, send.at[nb], recv.at[nb], me).wait_recv()
            for j, chip in enumerate(chips):
                got = outs[nb].at[2 * _chip_no(chip) + 1 - c]
                _rcopy(got, got, send.at[nb + 4 + j], recv.at[nb + 4 + j], me).wait_recv()
        for cp in first_copies(outs, send, recv) + passed:
            cp.wait_send()

    operands = list(big) + ([small] if small is not None else [])
    out_shapes = [jax.ShapeDtypeStruct(a.shape, a.dtype) for a in operands]
    aliases = {i: i for i in range(len(operands))}
    if tiny is not None:
        operands.append(tiny)
        out_shapes.append(jax.ShapeDtypeStruct((N_DEV, *tiny.shape), tiny.dtype))
    return _Hook(operands, out_shapes, aliases, n_sems, start, finish)


def _row_tile(rows, cols, target_bytes=2 * 1024 * 1024):
    best = SUBLANES
    for t in range(SUBLANES, rows + 1, SUBLANES):
        if rows % t == 0 and t * cols * 4 <= target_bytes:
            best = t
    return best


def _add_own_half(buf, got, c, wire):
    _, rows, cols = buf.shape
    tr = _row_tile(rows, cols)

    def body(c_ref, a_ref, b_ref, o_ref):
        o_ref[...] = (a_ref[...] + b_ref[...]).astype(wire)

    return _pcall(
        body,
        name=f"add_own_half_{rows}x{cols}",
        grid_spec=pltpu.PrefetchScalarGridSpec(
            num_scalar_prefetch=1,
            grid=(N_CHIPS, rows // tr),
            in_specs=[
                pl.BlockSpec((None, None, tr, cols), lambda j, r, c_ref: (j, c_ref[0], r, 0)),
                pl.BlockSpec((None, tr, cols), lambda j, r, c_ref: (j, r, 0)),
            ],
            out_specs=pl.BlockSpec((None, tr, cols), lambda j, r, c_ref: (j, r, 0)),
        ),
        out_shape=jax.ShapeDtypeStruct((N_CHIPS, rows, cols), wire),
        compiler_params=_cparams(("arbitrary", "arbitrary")),
    )(_scalars(c), buf.reshape(N_CHIPS, 2, rows, cols), got)


def _add_received(buf, got_a, got_b, mine, c, slot, n_slots):
    _, rows, cols = buf.shape
    tr = _row_tile(rows, cols)

    def body(s_ref, x_ref, a_ref, g_ref, o_ref):
        own = x_ref[...] + a_ref[...]
        o_ref[...] = ((own + g_ref[0].astype(F32)) + g_ref[1].astype(F32)) + g_ref[2].astype(F32)

    return _pcall(
        body,
        name=f"add_received_{rows}x{cols}",
        grid_spec=pltpu.PrefetchScalarGridSpec(
            num_scalar_prefetch=1,
            grid=(rows // tr,),
            in_specs=[
                pl.BlockSpec((None, None, tr, cols), lambda r, s_ref: (s_ref[0], s_ref[1], r, 0)),
                pl.BlockSpec((None, tr, cols), lambda r, s_ref: (s_ref[0], r, 0)),
                pl.BlockSpec((3, tr, cols), lambda r, s_ref: (0, r, 0)),
            ],
            out_specs=pl.BlockSpec((None, tr, cols), lambda r, s_ref: (s_ref[2], r, 0)),
        ),
        out_shape=jax.ShapeDtypeStruct((n_slots, rows, cols), F32),
        compiler_params=_cparams(("arbitrary",)),
    )(_scalars(mine, c, slot), buf.reshape(N_CHIPS, 2, rows, cols), got_a, got_b)


def _layer_a_fwd(x, nw, win, ln_w, ln_b, wc, bs_t, wout, tm, hook):
    t_rows, d = x.shape
    n_sh, _, s_cols = win.shape
    aw = wout.shape[0]
    gd = aw // A_GROUPS
    tn = 512
    assert s_cols % tn == 0 and aw % tn == 0 and tm % CHUNK == 0

    def body(x_ref, nw_ref, win_ref, lnw_ref, lnb_ref, wc_ref, bst_ref, wout_ref, z_ref, x1_ref, h_ref, u_s, v_s, y_s):
        x = x_ref[...]
        h, _, _ = _rms_fwd(x, nw_ref[...])
        h = h.astype(BF16)
        h_ref[...] = h
        for j in range(3 * aw // tn):
            k, off = divmod(j * tn, s_cols)
            cols = slice((j * tn) % aw, (j * tn) % aw + tn)
            zj = _dot(h, win_ref[k, :, off : off + tn])
            z_ref[:, j * tn : (j + 1) * tn] = zj
            if j * tn < aw:
                u_s[:, cols] = _gelu(zj)
            elif j * tn < 2 * aw:
                v_s[:, cols] = _gelu(zj)
            else:
                u_s[:, cols] = u_s[:, cols] * (zj * _sigmoid(zj))
        v = v_s[...]
        mu = jnp.mean(v, axis=-1, keepdims=True)
        vc = v - mu
        rstd = lax.rsqrt(jnp.mean(vc * vc, axis=-1, keepdims=True) + LN_EPS)
        v_s[...] = (vc * rstd) * lnw_ref[...] + lnb_ref[...]
        for ck in range(tm // CHUNK):
            rows = slice(ck * CHUNK, (ck + 1) * CHUNK)
            for g in range(A_GROUPS):
                cols = slice(g * gd, (g + 1) * gd)
                s = _dot(wc_ref[g], v_s[rows, cols].astype(BF16)) + bst_ref[:, g : g + 1]
                y_s[rows, cols] = (u_s[rows, cols] * s).astype(BF16)
        x1_ref[...] = x + _dot(y_s[...], wout_ref[...])

    row = lambda i: (i, 0)
    return _pcall(
        body,
        hook,
        name="layer_a_fwd",
        grid=(t_rows // tm,),
        in_specs=[
            pl.BlockSpec((tm, d), row),
            _full(nw.shape),
            _full(win.shape),
            _full(ln_w.shape),
            _full(ln_b.shape),
            _full(wc.shape),
            _full(bs_t.shape),
            _full(wout.shape),
        ],
        out_specs=[pl.BlockSpec((tm, 3 * aw), row), pl.BlockSpec((tm, d), row), pl.BlockSpec((tm, d), row)],
        out_shape=[
            jax.ShapeDtypeStruct((t_rows, 3 * aw), F32),
            jax.ShapeDtypeStruct((t_rows, d), F32),
            jax.ShapeDtypeStruct((t_rows, d), BF16),
        ],
        scratch_shapes=[pltpu.VMEM((tm, aw), F32), pltpu.VMEM((tm, aw), F32), pltpu.VMEM((tm, aw), BF16)],
        compiler_params=_cparams(("arbitrary",)),
    )(x, nw, win, ln_w, ln_b, wc, bs_t, wout)


def _layer_a_bwd(dout, z, ln_w, ln_b, wc, wct, bs_t, wout, tiles, earlier, hook):
    t_rows, d = dout.shape
    aw = wout.shape[0]
    gd = aw // A_GROUPS
    tm = TM_A_BWD
    lo, hi = tiles
    n_earlier = 0 if earlier is None else len(earlier)

    def body(dout_ref, z_ref, lnw_ref, lnb_ref, wc_ref, wct_ref, bst_ref, wout_ref, *rest):
        dz_ref, y_ref, dob_ref, gws_ref, gbs_ref, glnw_ref, glnb_ref, u_s, vh_s, ds_s, dvn_s = rest[n_earlier:]

        @pl.when(pl.program_id(0) == 0)
        def _():
            gws_ref[...] = jnp.zeros_like(gws_ref)
            gbs_ref[...] = jnp.zeros_like(gbs_ref)
            glnw_ref[...] = jnp.zeros_like(glnw_ref)
            glnb_ref[...] = jnp.zeros_like(glnb_ref)

        dob = dout_ref[...].astype(BF16)
        dob_ref[...] = dob
        dy = _dot_nt(dob, wout_ref[...])

        zv = z_ref[:, aw : 2 * aw]
        vg, dvg_dz = _gelu_and_grad(zv)
        mu = jnp.mean(vg, axis=-1, keepdims=True)
        vc = vg - mu
        rstd = lax.rsqrt(jnp.mean(vc * vc, axis=-1, keepdims=True) + LN_EPS)
        vh = vc * rstd
        vh_s[...] = vh
        vn = (vh * lnw_ref[...] + lnb_ref[...]).astype(BF16)

        zu = z_ref[:, 0:aw]
        zg = z_ref[:, 2 * aw : 3 * aw]
        u, du_dz = _gelu_and_grad(zu)
        sg, dsg = _silu_and_grad(zg)
        u_s[...] = u * sg
        tril = lax.broadcasted_iota(jnp.int32, (CHUNK, CHUNK), 0) >= lax.broadcasted_iota(jnp.int32, (CHUNK, CHUNK), 1)
        for ck in range(tm // CHUNK):
            rows = slice(ck * CHUNK, (ck + 1) * CHUNK)
            for g in range(A_GROUPS):
                cols = slice(g * gd, (g + 1) * gd)
                vn_g = vn[rows, cols]
                s = _dot(wc_ref[g], vn_g) + bst_ref[:, g : g + 1]
                usg = u_s[rows, cols]
                dy_g = dy[rows, cols]
                y_ref[rows, cols] = (usg * s).astype(BF16)
                ds = dy_g * usg
                ds_s[rows, cols] = dy_g * s
                gbs_ref[:, g : g + 1] += jnp.sum(ds, axis=-1, keepdims=True)
                dsb = ds.astype(BF16)
                gws_ref[g] += jnp.where(tril, _dot_nt(dsb, vn_g), 0.0)
                dvn_s[rows, cols] = _dot(wct_ref[g], dsb)
        dusg = ds_s[...]
        dz_ref[:, 0:aw] = (dusg * sg * du_dz).astype(BF16)
        dz_ref[:, 2 * aw : 3 * aw] = (dusg * u * dsg).astype(BF16)

        dvn = dvn_s[...]
        vh = vh_s[...]
        glnw_ref[...] += jnp.sum(dvn * vh, axis=0, keepdims=True)
        glnb_ref[...] += jnp.sum(dvn, axis=0, keepdims=True)
        dvh = dvn * lnw_ref[...]
        dvg = rstd * (dvh - jnp.mean(dvh, axis=-1, keepdims=True) - vh * jnp.mean(dvh * vh, axis=-1, keepdims=True))
        dz_ref[:, aw : 2 * aw] = (dvg * dvg_dz).astype(BF16)

    row = lambda i: (i + lo, 0)
    call = _pcall(
        body,
        hook,
        name=f"layer_a_bwd_{lo}",
        grid=(hi - lo,),
        in_specs=[
            pl.BlockSpec((tm, d), row),
            pl.BlockSpec((tm, 3 * aw), row),
            _full(ln_w.shape),
            _full(ln_b.shape),
            _full(wc.shape),
            _full(wct.shape),
            _full(bs_t.shape),
            _full(wout.shape),
        ]
        + [ANY] * n_earlier,
        out_specs=[
            pl.BlockSpec((tm, 3 * aw), row),
            pl.BlockSpec((tm, aw), row),
            pl.BlockSpec((tm, d), row),
            _full((A_GROUPS, CHUNK, CHUNK)),
            _full((CHUNK, A_GROUPS)),
            _full((1, aw)),
            _full((1, aw)),
        ],
        out_shape=[
            jax.ShapeDtypeStruct((t_rows, 3 * aw), BF16),
            jax.ShapeDtypeStruct((t_rows, aw), BF16),
            jax.ShapeDtypeStruct((t_rows, d), BF16),
            jax.ShapeDtypeStruct((A_GROUPS, CHUNK, CHUNK), F32),
            jax.ShapeDtypeStruct((CHUNK, A_GROUPS), F32),
            jax.ShapeDtypeStruct((1, aw), F32),
            jax.ShapeDtypeStruct((1, aw), F32),
        ],
        scratch_shapes=[pltpu.VMEM((tm, aw), F32)] * 4,
        input_output_aliases={8 + i: i for i in range(n_earlier)},
        compiler_params=_cparams(("arbitrary",)),
    )
    return call(dout, z, ln_w, ln_b, wc, wct, bs_t, wout, *(earlier or ()))


def _layer_a_bwd_dx(dout, x, dz, nw, win, tm, hook):
    t_rows, d = x.shape
    n_sh, _, s_cols = win.shape

    def body(dout_ref, x_ref, dz_ref, nw_ref, win_ref, gx_ref, gnw_ref):
        @pl.when(pl.program_id(0) == 0)
        def _():
            gnw_ref[...] = jnp.zeros_like(gnw_ref)

        dh = jnp.zeros((tm, d), F32)
        for k in range(n_sh):
            dh = dh + _dot_nt(dz_ref[:, k * s_cols : (k + 1) * s_cols], win_ref[k])
        nw = nw_ref[...]
        _, xh, r = _rms_fwd(x_ref[...], nw)
        dx, gnw = _rms_bwd(dh, xh, r, nw)
        gnw_ref[0:1, :] += gnw
        gx_ref[...] = dout_ref[...] + dx

    row = lambda i: (i, 0)
    return _pcall(
        body,
        hook,
        name="layer_a_bwd_dx",
        grid=(t_rows // tm,),
        in_specs=[
            pl.BlockSpec((tm, d), row),
            pl.BlockSpec((tm, d), row),
            pl.BlockSpec((tm, n_sh * s_cols), row),
            _full(nw.shape),
            _full(win.shape),
        ],
        out_specs=[pl.BlockSpec((tm, d), row), _full((SUBLANES, d))],
        out_shape=[jax.ShapeDtypeStruct((t_rows, d), F32), jax.ShapeDtypeStruct((SUBLANES, d), F32)],
        compiler_params=_cparams(("arbitrary",)),
    )(dout, x, dz, nw, win)


def _decay(r, sp_h):
    log_a = (-RG_C) * r * sp_h
    a = jnp.exp(log_a)
    mult = jnp.sqrt(jnp.tanh(-log_a) * (a * a + 1.0))
    return a, mult


def _gates(xc_h, gab_ref, gb_ref, sp_h, h, hd):
    pre = _dot(xc_h.astype(BF16), gab_ref[h])
    bw = gb_ref.shape[1] // 2
    r = _sigmoid(pre[:, :hd] + gb_ref[:, h * hd : (h + 1) * hd])
    ig = _sigmoid(pre[:, hd:] + gb_ref[:, bw + h * hd : bw + (h + 1) * hd])
    a, mult = _decay(r, sp_h)
    return r, ig, a, mult


def _conv(xb, halo, cw_ref, cb_ref):
    xc = cb_ref[...] + cw_ref[CONV_WIDTH - 1 : CONV_WIDTH, :] * xb
    for k in range(CONV_WIDTH - 1):
        xc = xc + cw_ref[k : k + 1, :] * _shift_down(xb, halo, CONV_WIDTH - 1 - k)
    return xc


def _layer_b_fwd(x1, nw, bin_w, cw, cb, gab, gb, lam, bout, nf, tgt, tm):
    t_rows, d = x1.shape
    bw = bout.shape[0]
    hd = bw // B_HEADS
    nt = t_rows // tm

    def body(
        x1_ref, nw_ref, bin_ref, cw_ref, cb_ref, gab_ref, gb_ref, lam_ref, bout_ref, nf_ref, tgt_ref,
        z_ref, h_ref, h1_ref, xbt_ref, ht_ref, dx2_ref, loss_ref, gnf_ref,
        tail_s, carry_s, a_s, b_s, hs_s, acc_s,
    ):
        @pl.when(pl.program_id(0) == 0)
        def _():
            tail_s[...] = jnp.zeros_like(tail_s)
            carry_s[...] = jnp.zeros_like(carry_s)
            acc_s[...] = jnp.zeros_like(acc_s)
            gnf_ref[...] = jnp.zeros_like(gnf_ref)

        x1 = x1_ref[...]
        h1, _, _ = _rms_fwd(x1, nw_ref[...])
        h1 = h1.astype(BF16)
        h1_ref[...] = h1
        z = jnp.concatenate([_dot(h1, bin_ref[k]) for k in range(N_CHIPS)], axis=1)
        z_ref[...] = z
        xb = z[:, :bw]
        xc = _conv(xb, tail_s[...], cw_ref, cb_ref)
        tail = xb[tm - SUBLANES :, :]
        tail_s[...] = tail
        xbt_ref[...] = tail
        sp = _softplus_neg(lam_ref[...])
        for h in range(B_HEADS):
            cols = slice(h * hd, (h + 1) * hd)
            xc_h = xc[:, cols]
            _, ig, a, mult = _gates(xc_h, gab_ref, gb_ref, sp[:, cols], h, hd)
            a_s[:, cols] = a
            b_s[:, cols] = mult * (ig * xc_h)
        carry = _scan_blocks(a_s, b_s, hs_s, carry_s[...], tm, reverse=False)
        carry_s[...] = carry
        ht_ref[...] = hs_s[tm - SUBLANES :, :]
        hs = hs_s[...]
        h_ref[...] = hs
        g = z[:, bw:]
        y = (hs * (g * _sigmoid(g))).astype(BF16)
        x2 = x1 + _dot(y, bout_ref[...])

        nf = nf_ref[...]
        o, xh, r = _rms_fwd(x2, nf)
        diff = o - tgt_ref[...]
        acc_s[...] += jnp.sum(diff * diff, axis=0, keepdims=True)
        do = diff * (1.0 / d)
        dx2, gnf = _rms_bwd(do, xh, r, nf)
        gnf_ref[...] += gnf
        dx2_ref[...] = dx2

        @pl.when(pl.program_id(0) == nt - 1)
        def _():
            total = jnp.sum(acc_s[...], axis=-1, keepdims=True) * (0.5 / d)
            loss_ref[...] = jnp.broadcast_to(total, loss_ref.shape)

    row = lambda i: (i, 0)
    return _pcall(
        body,
        name="layer_b_fwd",
        grid=(nt,),
        in_specs=[
            pl.BlockSpec((tm, d), row),
            _full(nw.shape),
            _full(bin_w.shape),
            _full(cw.shape),
            _full(cb.shape),
            _full(gab.shape),
            _full(gb.shape),
            _full(lam.shape),
            _full(bout.shape),
            _full(nf.shape),
            pl.BlockSpec((tm, d), row),
        ],
        out_specs=[
            pl.BlockSpec((tm, 2 * bw), row),
            pl.BlockSpec((tm, bw), row),
            pl.BlockSpec((tm, d), row),
            pl.BlockSpec((None, SUBLANES, bw), lambda i: (i, 0, 0)),
            pl.BlockSpec((None, SUBLANES, bw), lambda i: (i, 0, 0)),
            pl.BlockSpec((tm, d), row),
            _full((1, LANES)),
            _full((1, d)),
        ],
        out_shape=[
            jax.ShapeDtypeStruct((t_rows, 2 * bw), F32),
            jax.ShapeDtypeStruct((t_rows, bw), F32),
            jax.ShapeDtypeStruct((t_rows, d), BF16),
            jax.ShapeDtypeStruct((nt, SUBLANES, bw), F32),
            jax.ShapeDtypeStruct((nt, SUBLANES, bw), F32),
            jax.ShapeDtypeStruct((t_rows, d), F32),
            jax.ShapeDtypeStruct((1, LANES), F32),
            jax.ShapeDtypeStruct((1, d), F32),
        ],
        scratch_shapes=[
            pltpu.VMEM((SUBLANES, bw), F32),
            pltpu.VMEM((SUBLANES, bw), F32),
            pltpu.VMEM((tm, bw), F32),
            pltpu.VMEM((tm, bw), F32),
            pltpu.VMEM((tm, bw), F32),
            pltpu.VMEM((1, d), F32),
        ],
        compiler_params=_cparams(("arbitrary",)),
    )(x1, nw, bin_w, cw, cb, gab, gb, lam, bout, nf, tgt)


def _layer_b_bwd(dout, x1, z, hseq, xb_tails, h_tails, nw, bin_w, cw, cb, gab, gabt, gb, lam, bout, tm):
    t_rows, d = x1.shape
    bw = bout.shape[0]
    hd = bw // B_HEADS
    nt = t_rows // tm

    def body(
        dout_ref, x1_ref, z_ref, h_ref, xbt_ref, ht_ref, nw_ref, bin_ref, cw_ref, cb_ref, gab_ref, gabt_ref,
        gb_ref, lam_ref, bout_ref,
        dx1_ref, dz_ref, y_ref, dob_ref, ggab_ref, ggb_ref, gcw_ref, gcb_ref, glam_ref, gnw_ref,
        gcarry_s, afirst_s, head_s, aup_s, dh_s, gt_s, dxc_s, xc_s, r_s, ig_s,
    ):
        step = pl.program_id(0)
        tile = nt - 1 - step

        @pl.when(step == 0)
        def _():
            for ref in (ggab_ref, ggb_ref, gcw_ref, gcb_ref, glam_ref, gnw_ref, gcarry_s, afirst_s, head_s):
                ref[...] = jnp.zeros_like(ref)

        first_tile = tile == 0
        xb_halo = jnp.where(first_tile, 0.0, xbt_ref[...])
        h_halo = jnp.where(first_tile, 0.0, ht_ref[...])

        dout = dout_ref[...]
        dob = dout.astype(BF16)
        dob_ref[...] = dob
        dy = _dot_nt(dob, bout_ref[...])
        hs = h_ref[...]
        g = z_ref[:, bw:]
        sg, dsg = _silu_and_grad(g)
        y_ref[...] = (hs * sg).astype(BF16)
        dz_ref[:, bw:] = (dy * hs * dsg).astype(BF16)
        dh_s[...] = dy * sg

        xb = z_ref[:, :bw]
        xc = _conv(xb, xb_halo, cw_ref, cb_ref)
        xc_s[...] = xc
        lam = lam_ref[...]
        sp = _softplus_neg(lam)
        for h in range(B_HEADS):
            cols = slice(h * hd, (h + 1) * hd)
            r, ig, a, _ = _gates(xc[:, cols], gab_ref, gb_ref, sp[:, cols], h, hd)
            r_s[:, cols] = r
            ig_s[:, cols] = ig
            aup_s[:, cols] = _shift_up(a, afirst_s[:, cols], 1)
            afirst_s[:, cols] = jnp.broadcast_to(a[0:1, :], (SUBLANES, hd))
        carry = _scan_blocks(aup_s, dh_s, gt_s, gcarry_s[...], tm, reverse=True)
        gcarry_s[...] = carry

        h_prev = _shift_down(hs, h_halo, 1)
        for h in range(B_HEADS):
            cols = slice(h * hd, (h + 1) * hd)
            xc_h = xc_s[:, cols]
            sp_h = sp[:, cols]
            r, ig = r_s[:, cols], ig_s[:, cols]
            a, mult = _decay(r, sp_h)
            gt = gt_s[:, cols]
            da = gt * h_prev[:, cols]
            dmult = gt * (ig * xc_h)
            dig = gt * (mult * xc_h)
            dxc_direct = gt * (mult * ig)
            dla = da * a - dmult * (a * a) / mult
            glam_ref[:, cols] += jnp.sum(dla * r, axis=0, keepdims=True)
            dr = dla * ((-RG_C) * sp_h)
            dpre = jnp.concatenate([dr * r * (1.0 - r), dig * ig * (1.0 - ig)], axis=1)
            ggb_ref[:, cols] += jnp.sum(dpre[:, :hd], axis=0, keepdims=True)
            ggb_ref[:, bw + h * hd : bw + (h + 1) * hd] += jnp.sum(dpre[:, hd:], axis=0, keepdims=True)
            dpb = dpre.astype(BF16)
            ggab_ref[h] += _dot_tn(xc_h.astype(BF16), dpb)
            dxc_s[:, cols] = dxc_direct + _dot(dpb, gabt_ref[h])
        glam_ref[...] = jnp.where(step == nt - 1, glam_ref[...] * (RG_C * _sigmoid(-lam)), glam_ref[...])

        dxc = dxc_s[...]
        gcb_ref[...] += jnp.sum(dxc, axis=0, keepdims=True)
        dxb = cw_ref[CONV_WIDTH - 1 : CONV_WIDTH, :] * dxc
        gcw_ref[CONV_WIDTH - 1 : CONV_WIDTH, :] += jnp.sum(dxc * xb, axis=0, keepdims=True)
        head = head_s[...]
        for k in range(CONV_WIDTH - 1):
            lag = CONV_WIDTH - 1 - k
            dxb = dxb + cw_ref[k : k + 1, :] * _shift_up(dxc, head, lag)
            gcw_ref[k : k + 1, :] += jnp.sum(dxc * _shift_down(xb, xb_halo, lag), axis=0, keepdims=True)
        head_s[...] = dxc[:SUBLANES, :]
        dz_ref[:, :bw] = dxb.astype(BF16)

        s_cols = 2 * bw // N_CHIPS
        dh1 = jnp.zeros((tm, d), F32)
        for k in range(N_CHIPS):
            dh1 = dh1 + _dot_nt(dz_ref[:, k * s_cols : (k + 1) * s_cols], bin_ref[k])
        x1 = x1_ref[...]
        nw = nw_ref[...]
        _, xh, r1 = _rms_fwd(x1, nw)
        dx, gnw = _rms_bwd(dh1, xh, r1, nw)
        gnw_ref[...] += gnw
        dx1_ref[...] = dout + dx

    rev = lambda i: (nt - 1 - i, 0)
    prev = lambda i: (jnp.maximum(nt - 2 - i, 0), 0, 0)
    return _pcall(
        body,
        name="layer_b_bwd",
        grid=(nt,),
        in_specs=[
            pl.BlockSpec((tm, d), rev),
            pl.BlockSpec((tm, d), rev),
            pl.BlockSpec((tm, 2 * bw), rev),
            pl.BlockSpec((tm, bw), rev),
            pl.BlockSpec((None, SUBLANES, bw), prev),
            pl.BlockSpec((None, SUBLANES, bw), prev),
            _full(nw.shape),
            _full(bin_w.shape),
            _full(cw.shape),
            _full(cb.shape),
            _full(gab.shape),
            _full(gabt.shape),
            _full(gb.shape),
            _full(lam.shape),
            _full(bout.shape),
        ],
        out_specs=[
            pl.BlockSpec((tm, d), rev),
            pl.BlockSpec((tm, 2 * bw), rev),
            pl.BlockSpec((tm, bw), rev),
            pl.BlockSpec((tm, d), rev),
            _full((B_HEADS, hd, 2 * hd)),
            _full((1, 2 * bw)),
            _full((SUBLANES, bw)),
            _full((1, bw)),
            _full((1, bw)),
            _full((1, d)),
        ],
        out_shape=[
            jax.ShapeDtypeStruct((t_rows, d), F32),
            jax.ShapeDtypeStruct((t_rows, 2 * bw), BF16),
            jax.ShapeDtypeStruct((t_rows, bw), BF16),
            jax.ShapeDtypeStruct((t_rows, d), BF16),
            jax.ShapeDtypeStruct((B_HEADS, hd, 2 * hd), F32),
            jax.ShapeDtypeStruct((1, 2 * bw), F32),
            jax.ShapeDtypeStruct((SUBLANES, bw), F32),
            jax.ShapeDtypeStruct((1, bw), F32),
            jax.ShapeDtypeStruct((1, bw), F32),
            jax.ShapeDtypeStruct((1, d), F32),
        ],
        scratch_shapes=[pltpu.VMEM((SUBLANES, bw), F32)] * 3 + [pltpu.VMEM((tm, bw), F32)] * 7,
        compiler_params=_cparams(("arbitrary",)),
    )(dout, x1, z, hseq, xb_tails, h_tails, nw, bin_w, cw, cb, gab, gabt, gb, lam, bout)


def _wgrad(a, b, m_blocks, n_blocks, hook=None):
    k, m = a.shape
    n = b.shape[1]
    bm, bn = m // m_blocks, n // n_blocks

    def body(a_ref, b_ref, o_ref):
        o_ref[...] = _dot_tn(a_ref[...], b_ref[...])

    out = _pcall(
        body,
        hook,
        name=f"wgrad_{m}x{n}",
        grid=(n_blocks, m_blocks),
        in_specs=[pl.BlockSpec((k, bm), lambda j, i: (0, i)), pl.BlockSpec((k, bn), lambda j, i: (0, j))],
        out_specs=[pl.BlockSpec((None, None, bm, bn), lambda j, i: (j, i, 0, 0))],
        out_shape=[jax.ShapeDtypeStruct((n_blocks, m_blocks, bm, bn), F32)],
        compiler_params=_cparams(("arbitrary", "arbitrary")),
    )(a, b)
    return out[0] if hook is None else (out[0][0], out[1])


def _adamw_math(w, g, m, v):
    m = ADAM_B1 * m + (1.0 - ADAM_B1) * g
    v = ADAM_B2 * v + (1.0 - ADAM_B2) * (g * g)
    m_hat = m / (1.0 - ADAM_B1**ADAM_STEP)
    v_hat = v / (1.0 - ADAM_B2**ADAM_STEP)
    delta = -ADAM_LR * (m_hat / (jnp.sqrt(v_hat) + ADAM_EPS) + ADAM_WD * w)
    return delta, m, v


def _adamw(w, g, m, v):
    rows, cols = w.shape
    tr = _row_tile(rows, cols, 1024 * 1024)

    def body(w_ref, g_ref, m_ref, v_ref, d_ref, mo_ref, vo_ref):
        d_ref[...], mo_ref[...], vo_ref[...] = _adamw_math(w_ref[...], g_ref[...], m_ref[...], v_ref[...])

    spec = pl.BlockSpec((tr, cols), lambda i: (i, 0))
    return _pcall(
        body,
        name=f"adamw_{rows}x{cols}",
        grid=(rows // tr,),
        in_specs=[spec] * 4,
        out_specs=[spec] * 3,
        out_shape=[jax.ShapeDtypeStruct((rows, cols), F32)] * 3,
        compiler_params=_cparams(("arbitrary",)),
    )(w, g, m, v)


def _sum_partials(parts):
    def body(p_ref, o_ref):
        total = p_ref[0, 0:1, :]
        for k in range(1, N_DEV):
            total = total + p_ref[k, 0:1, :]
        o_ref[...] = total

    vmem = pl.BlockSpec(memory_space=pltpu.VMEM)
    return _pcall(
        body,
        name="sum_partials",
        in_specs=[vmem],
        out_specs=vmem,
        out_shape=jax.ShapeDtypeStruct((1, parts.shape[2]), F32),
    )(parts)


def _adamw_many(ws, gs, ms, vs, name, hook=None):
    n = len(ws)

    def body(*refs):
        w_refs, g_refs, m_refs, v_refs = (refs[i * n : (i + 1) * n] for i in range(4))
        d_refs, mo_refs, vo_refs = (refs[(4 + i) * n : (5 + i) * n] for i in range(3))
        for i in range(n):
            d_refs[i][...], mo_refs[i][...], vo_refs[i][...] = _adamw_math(
                w_refs[i][...], g_refs[i][...], m_refs[i][...], v_refs[i][...]
            )

    vmem = pl.BlockSpec(memory_space=pltpu.VMEM)
    outs = _pcall(
        body,
        hook,
        name=name,
        in_specs=[vmem] * (4 * n),
        out_specs=[vmem] * (3 * n),
        out_shape=[jax.ShapeDtypeStruct(w.shape, F32) for w in ws] * 3,
        compiler_params=_cparams(),
    )(*ws, *gs, *ms, *vs)
    extra = None
    if hook is not None:
        outs, extra = outs
    return (outs[:n], outs[n : 2 * n], outs[2 * n :]), extra


def _pack_rows(parts, lanes=LANES):
    flat = jnp.concatenate([p.reshape(-1) for p in parts])
    per = N_DEV * SUBLANES * lanes
    total = -(-flat.shape[0] // per) * per
    flat = jnp.pad(flat, (0, total - flat.shape[0]))
    return flat.reshape(N_DEV, total // (N_DEV * lanes), lanes)


def _unpack(flat, shapes):
    out, at = [], 0
    for s in shapes:
        n = 1
        for dim in s:
            n *= dim
        out.append(flat[at : at + n].reshape(s))
        at += n
    return out


def kernel(x, norm_w, a_w_in, a_ln_w, a_ln_b, a_w_s, a_b_s, a_w_out, b_w_in, b_conv_w, b_conv_b, b_gate_a_w, b_gate_a_b, b_gate_x_w, b_gate_x_b, b_lambda, b_w_out, norm_f_w, loss_target, m_norm_w, m_a_w_in, m_a_ln_w, m_a_ln_b, m_a_w_s, m_a_b_s, m_a_w_out, m_b_w_in, m_b_conv_w, m_b_conv_b, m_b_gate_a_w, m_b_gate_a_b, m_b_gate_x_w, m_b_gate_x_b, m_b_lambda, m_b_w_out, m_norm_f_w, v_norm_w, v_a_w_in, v_a_ln_w, v_a_ln_b, v_a_w_s, v_a_b_s, v_a_w_out, v_b_w_in, v_b_conv_w, v_b_conv_b, v_b_gate_a_w, v_b_gate_a_b, v_b_gate_x_w, v_b_gate_x_b, v_b_lambda, v_b_w_out, v_norm_f_w):
    t_rows, d = x.shape[1], x.shape[2]
    aw = a_ln_w.shape[1]
    bw = b_gate_a_w.shape[1] * b_gate_a_w.shape[2]
    hd = bw // B_HEADS
    mine = 2 * lax.axis_index("x") + lax.axis_index("y")
    core = lax.axis_index("c")
    weights = dict(norm_w=norm_w, a_w_in=a_w_in, a_ln_w=a_ln_w, a_ln_b=a_ln_b, a_w_s=a_w_s, a_b_s=a_b_s, a_w_out=a_w_out, b_w_in=b_w_in, b_conv_w=b_conv_w, b_conv_b=b_conv_b, b_gate_a_w=b_gate_a_w, b_gate_a_b=b_gate_a_b, b_gate_x_w=b_gate_x_w, b_gate_x_b=b_gate_x_b, b_lambda=b_lambda, b_w_out=b_w_out, norm_f_w=norm_f_w)
    m_in = dict(norm_w=m_norm_w, a_w_in=m_a_w_in, a_ln_w=m_a_ln_w, a_ln_b=m_a_ln_b, a_w_s=m_a_w_s, a_b_s=m_a_b_s, a_w_out=m_a_w_out, b_w_in=m_b_w_in, b_conv_w=m_b_conv_w, b_conv_b=m_b_conv_b, b_gate_a_w=m_b_gate_a_w, b_gate_a_b=m_b_gate_a_b, b_gate_x_w=m_b_gate_x_w, b_gate_x_b=m_b_gate_x_b, b_lambda=m_b_lambda, b_w_out=m_b_w_out, norm_f_w=m_norm_f_w)
    v_in = dict(norm_w=v_norm_w, a_w_in=v_a_w_in, a_ln_w=v_a_ln_w, a_ln_b=v_a_ln_b, a_w_s=v_a_w_s, a_b_s=v_a_b_s, a_w_out=v_a_w_out, b_w_in=v_b_w_in, b_conv_w=v_b_conv_w, b_conv_b=v_b_conv_b, b_gate_a_w=v_b_gate_a_w, b_gate_a_b=v_b_gate_a_b, b_gate_x_w=v_b_gate_x_w, b_gate_x_b=v_b_gate_x_b, b_lambda=v_b_lambda, b_w_out=v_b_w_out, norm_f_w=v_norm_f_w)

    win_l = _cast_to_segments(a_w_in[0], mine, 256)
    wout_l = _cast_to_segments(a_w_out[0], mine, 256)
    bin_l = _cast_to_segments(b_w_in[0], mine, 256)
    bout_l = _cast_to_segments(b_w_out[0], mine, 192)
    small_l = jnp.concatenate([b_conv_w[0], b_conv_b, b_gate_a_b, b_gate_x_b, b_lambda], axis=0)
    win_g, wout_g, small_g = _run_hook(_gather_hook([win_l, wout_l], small_l), "gather_layer_a")
    win = win_g.reshape(N_CHIPS, d, -1)
    wout = wout_g.reshape(aw, d)

    tril = jnp.tril(jnp.ones((CHUNK, CHUNK), F32))
    wc = (a_w_s[0] * tril[None]).astype(BF16)
    wct = jnp.swapaxes(wc, 1, 2)
    bs_t = a_b_s[0].T
    gab = jnp.concatenate([b_gate_a_w[0], b_gate_x_w[0]], axis=2).astype(BF16)
    gabt = jnp.swapaxes(gab, 1, 2)
    nw0, nw1, nf = norm_w[0:1], norm_w[1:2], norm_f_w.reshape(1, d)

    x0 = x[0]
    (z_a, x1, h0), (bin_g, bout_g) = _layer_a_fwd(
        x0, nw0, win, a_ln_w, a_ln_b, wc, bs_t, wout, TM_FWD, _gather_hook([bin_l, bout_l])
    )
    bin_w = bin_g.reshape(N_CHIPS, d, -1)
    bout = bout_g.reshape(bw, d)
    small_f = jnp.transpose(small_g, (1, 0, 2)).reshape(SUBLANES, bw)
    cw, cb = small_f[0:CONV_WIDTH], small_f[CONV_WIDTH : CONV_WIDTH + 1]
    gb = jnp.concatenate([small_f[5:6], small_f[6:7]], axis=1)
    lam = small_f[7:8]
    z_b, hseq, h1, xb_tails, h_tails, dx2, loss_l, g_nf = _layer_b_fwd(
        x1, nw1, bin_w, cw, cb, gab, gb, lam, bout, nf, loss_target[0], TM_FWD
    )
    dx1, dz_b, y_b, dob_b, g_gab, g_gb, g_cw, g_cb, g_lam, g_nw1 = _layer_b_bwd(
        dx2, x1, z_b, hseq, xb_tails, h_tails, nw1, bin_w, cw, cb, gab, gabt, gb, lam, bout, TM_FWD
    )
    seg = lambda g: g.reshape(N_DEV, -1, g.shape[3])
    own_half = lambda bufs, got, wires: [_add_own_half(b, g, core, w) for b, g, w in zip(bufs, got, wires)]
    received = lambda bufs, got_a, got_b: [
        _add_received(b, ga, gb_, mine, core, core, 2) for b, ga, gb_ in zip(bufs, got_a, got_b)
    ]

    g_bout = seg(_wgrad(y_b, dob_b, N_CHIPS, 1))
    g_bin, got_a1 = _wgrad(h1, dz_b, 2, N_CHIPS, _swap_hook([g_bout]))
    g_bin = seg(g_bin)
    parts1 = own_half([g_bout], got_a1, [BF16])
    a_args = (z_a, a_ln_w, a_ln_b, wc, wct, bs_t, wout)
    half = t_rows // TM_A_BWD // 2
    first, rode = _layer_a_bwd(dx1, *a_args, (0, half), None, _join_hooks(_swap_hook([g_bin]), _send_hook(parts1)))
    got_a2, got_b1 = rode[:1], rode[1:]
    parts2 = own_half([g_bin], got_a2, [BF16])
    red1 = received([g_bout], got_a1, got_b1)
    second, rode = _layer_a_bwd(
        dx1, *a_args, (half, 2 * half), first[:3], _join_hooks(_send_hook(parts2), _share_hook(red1))
    )
    got_b2, gr_bout = rode[:1], rode[1].reshape(b_w_out.shape[1:])
    dz_a, y_a, dob_a = second[:3]
    g_ws, g_bst, g_lnw, g_lnb = (p + q for p, q in zip(first[3:], second[3:]))
    red2 = received([g_bin], got_a2, got_b2)
    g_wout, (gr_bin,) = _wgrad(y_a, dob_a, N_DEV, 1, _share_hook(red2))
    g_wout = seg(g_wout)
    gr_bin = gr_bin.reshape(b_w_in.shape[1:])

    small_shapes = [
        (1, d), (1, aw), (1, aw), (A_GROUPS, CHUNK, CHUNK), (A_GROUPS, CHUNK), (B_HEADS, hd, hd), (B_HEADS, hd, hd),
        (d,), (CONV_WIDTH, bw), (1, bw), (1, bw), (1, bw), (1, bw), (1, 1),
    ]
    small = _pack_rows(
        [
            g_nw1, g_lnw, g_lnb, g_ws, g_bst.T, g_gab[:, :, :hd], g_gab[:, :, hd:],
            g_nf, g_cw[:CONV_WIDTH], g_cb, g_gb[:, :bw], g_gb[:, bw:], g_lam, loss_l[:, :1],
        ]
    )

    g_win, got_a1 = _wgrad(h0, dz_a, 2, N_CHIPS, _swap_hook([g_wout, small]))
    g_win = seg(g_win)
    parts1 = own_half([g_wout, small], got_a1, [BF16, F32])
    (grad_x, g_nw0_mine), rode = _layer_a_bwd_dx(
        dx1, x0, dz_a, nw0, win, TM_A_DX, _join_hooks(_swap_hook([g_win]), _send_hook(parts1))
    )
    got_a2, got_b1 = rode[:1], rode[1:]
    parts2 = own_half([g_win], got_a2, [BF16])
    red_wout = received([g_wout], got_a1[:1], got_b1[:1])
    red_small = _add_received(small, got_a1[1], got_b1[1], mine, core, 2 * mine + core, N_DEV)
    b_names = ("b_w_in", "b_w_out")
    b_grads = {"b_w_in": gr_bin, "b_w_out": gr_bout}
    two_d = lambda a: a.reshape(a.shape[-2:])
    b_out, rode = _adamw_many(
        [two_d(weights[n]) for n in b_names], [b_grads[n] for n in b_names], [two_d(m_in[n]) for n in b_names],
        [two_d(v_in[n]) for n in b_names], "adamw_layer_b",
        _join_hooks(_send_hook(parts2), _share_hook(red_wout, red_small, g_nw0_mine)),
    )
    got_b2, (gr_wout, small_r, g_nw0_all) = rode[:1], rode[1:]
    (gr_win,) = _run_hook(_share_hook(received([g_win], got_a2, got_b2)), "share_reduced")
    g_nw0 = _sum_partials(g_nw0_all)

    gr_win = gr_win.reshape(a_w_in.shape[1:])
    gr_wout = gr_wout.reshape(a_w_out.shape[1:])
    (g_nw1_r, g_a_ln_w, g_a_ln_b, g_a_w_s, g_a_b_s, g_gate_a_w, g_gate_x_w, g_norm_f, gf_cw, gf_cb, gf_gab, gf_gxb,
     gf_lam, loss) = _unpack(small_r.reshape(-1), small_shapes)
    g_norm_w = jnp.concatenate([g_nw0, g_nw1_r], axis=0)
    shard = lambda g: lax.dynamic_slice_in_dim(g, mine * (bw // N_CHIPS), bw // N_CHIPS, axis=1)

    grads = {
        "norm_w": g_norm_w, "a_w_in": gr_win[None], "a_ln_w": g_a_ln_w, "a_ln_b": g_a_ln_b, "a_w_s": g_a_w_s[None],
        "a_b_s": g_a_b_s[None], "a_w_out": gr_wout[None], "b_w_in": gr_bin[None], "b_conv_w": shard(gf_cw)[None],
        "b_conv_b": shard(gf_cb), "b_gate_a_w": g_gate_a_w[None], "b_gate_a_b": shard(gf_gab),
        "b_gate_x_w": g_gate_x_w[None], "b_gate_x_b": shard(gf_gxb), "b_lambda": shard(gf_lam),
        "b_w_out": gr_bout[None], "norm_f_w": g_norm_f,
    }
    names = list(weights)
    delta, new_m, new_v = {}, {}, {}
    for n in ("a_w_in", "a_w_out"):
        dl, mo, vo = _adamw(two_d(weights[n]), two_d(grads[n]), two_d(m_in[n]), two_d(v_in[n]))
        delta[n], new_m[n], new_v[n] = dl, mo, vo
    small_names = [n for n in names if n not in ("a_w_in", "a_w_out") + b_names]
    at_least_2d = lambda a: a.reshape(1, -1) if a.ndim == 1 else a
    small_out, _ = _adamw_many(
        *[[at_least_2d(src[n]) for n in small_names] for src in (weights, grads, m_in, v_in)], "adamw_small"
    )
    for dst, vals, b_vals in zip((delta, new_m, new_v), small_out, b_out):
        dst.update(zip(small_names, vals))
        dst.update(zip(b_names, b_vals))
    for dst in (delta, new_m, new_v):
        for n in names:
            dst[n] = dst[n].reshape(weights[n].shape)

    return (
        loss.reshape(()),
        grad_x[None],
        *[grads[n] for n in names],
        *[delta[n] for n in names],
        *[new_m[n] for n in names],
        *[new_v[n] for n in names],
    )
```

```python
import jax
import jax.numpy as jnp
from jax import lax
from jax.experimental import pallas as pl
from jax.experimental.pallas import tpu as pltpu

F32 = jnp.float32
BF16 = jnp.bfloat16

RMS_EPS = 1e-6
LN_EPS = 1e-5
RG_C = 8.0
CHUNK = 128
A_GROUPS = 8
B_HEADS = 12
CONV_WIDTH = 4

ADAM_LR = 0.001
ADAM_B1 = 0.9
ADAM_B2 = 0.999
ADAM_EPS = 1e-08
ADAM_WD = 0.01
ADAM_STEP = 10

N_CHIPS = 4
N_DEV = 8
SUBLANES = 8
LANES = 128
V7X_VMEM_BYTES = 64 * 1024 * 1024
VMEM_LIMIT = V7X_VMEM_BYTES * 7 // 8
MESH = pl.DeviceIdType.MESH
ANY = pl.BlockSpec(memory_space=pl.ANY)

TM_FWD = 256
TM_A_BWD = 256
TM_A_DX = 512

GELU_C0 = 0.7978845608028654
GELU_C1 = 0.044715


class _Hook:
    def __init__(self, operands, out_shapes, aliases, n_sems, start, finish):
        self.operands, self.out_shapes, self.aliases, self.n_sems = operands, out_shapes, aliases, n_sems
        self.start, self.finish = start, finish


class _SemView:
    def __init__(self, base, off):
        self.base, self.off = base, off

    @property
    def at(self):
        return self

    def __getitem__(self, k):
        return self.base.at[self.off + k]


def _join_hooks(*hooks):
    if len(hooks) == 1:
        return hooks[0]
    operands, out_shapes, aliases, spans = [], [], {}, []
    n_sems = 0
    for h in hooks:
        aliases.update({len(operands) + i: len(out_shapes) + o for i, o in h.aliases.items()})
        spans.append((len(operands), len(h.operands), len(out_shapes), len(h.out_shapes), n_sems))
        operands += list(h.operands)
        out_shapes += list(h.out_shapes)
        n_sems += h.n_sems

    def each(which):
        def run(ins, outs, send, recv):
            for h, (i0, ni, o0, no, s0) in zip(hooks, spans):
                getattr(h, which)(ins[i0 : i0 + ni], outs[o0 : o0 + no], _SemView(send, s0), _SemView(recv, s0))

        return run

    return _Hook(operands, out_shapes, aliases, n_sems, each("start"), each("finish"))


def _pcall(body, hook=None, **kw):
    if hook is None:
        return pl.pallas_call(body, **kw)
    n_in, n_out = len(kw["in_specs"]), len(kw["out_shape"])
    hi, ho = len(hook.operands), len(hook.out_shapes)
    grid = kw.get("grid", ())

    def wrapped(*refs):
        ins, h_in = refs[:n_in], refs[n_in : n_in + hi]
        outs = refs[n_in + hi : n_in + hi + n_out]
        h_out = refs[n_in + hi + n_out : n_in + hi + n_out + ho]
        scratch = refs[n_in + hi + n_out + ho : -2]
        send_sems, recv_sems = refs[-2:]
        if not grid:
            hook.start(h_in, h_out, send_sems, recv_sems)
            body(*ins, *outs, *scratch)
            hook.finish(h_in, h_out, send_sems, recv_sems)
            return
        first = pl.program_id(0) == 0
        last = pl.program_id(0) == grid[0] - 1
        for axis in range(1, len(grid)):
            first = jnp.logical_and(first, pl.program_id(axis) == 0)
            last = jnp.logical_and(last, pl.program_id(axis) == grid[axis] - 1)

        @pl.when(first)
        def _():
            hook.start(h_in, h_out, send_sems, recv_sems)

        body(*ins, *outs, *scratch)

        @pl.when(last)
        def _():
            hook.finish(h_in, h_out, send_sems, recv_sems)

    aliases = dict(kw.pop("input_output_aliases", {}))
    aliases.update({n_in + i: n_out + o for i, o in hook.aliases.items()})
    kw.update(
        in_specs=list(kw["in_specs"]) + [ANY] * hi,
        out_specs=list(kw["out_specs"]) + [ANY] * ho,
        out_shape=list(kw["out_shape"]) + list(hook.out_shapes),
        scratch_shapes=list(kw.get("scratch_shapes", ()))
        + [pltpu.SemaphoreType.DMA((hook.n_sems,)), pltpu.SemaphoreType.DMA((hook.n_sems,))],
        input_output_aliases=aliases,
    )
    call = pl.pallas_call(wrapped, **kw)

    def run(*operands):
        outs = call(*operands, *hook.operands)
        return outs[:n_out], outs[n_out:]

    return run


def _run_hook(hook, name):
    def body():
        pass

    return _pcall(body, hook, name=name, in_specs=[], out_specs=[], out_shape=[])()[1]


def _cparams(sem=None):
    return pltpu.CompilerParams(dimension_semantics=sem, vmem_limit_bytes=VMEM_LIMIT)


def _full(shape):
    zeros = (0,) * len(shape)
    return pl.BlockSpec(shape, lambda *_: zeros)


def _scalars(*vals):
    return jnp.stack([jnp.asarray(v, jnp.int32) for v in vals])


def _sigmoid(x):
    return 1.0 / (1.0 + jnp.exp(-x))


def _gelu(x):
    t = jnp.tanh(GELU_C0 * (x + GELU_C1 * (x * x * x)))
    return x * (0.5 * (1.0 + t))


def _gelu_and_grad(x):
    x2 = x * x
    t = jnp.tanh(GELU_C0 * (x + GELU_C1 * (x2 * x)))
    cdf = 0.5 * (1.0 + t)
    return x * cdf, cdf + 0.5 * x * (1.0 - t * t) * (GELU_C0 * (1.0 + 3.0 * GELU_C1 * x2))


def _silu_and_grad(x):
    s = _sigmoid(x)
    return x * s, s * (1.0 + x * (1.0 - s))


def _softplus_neg(lam):
    u = jnp.exp(-jnp.abs(lam))
    w = 1.0 + u
    log1p = jnp.where(w == 1.0, u, jnp.log(w) * (u / jnp.where(w == 1.0, 1.0, w - 1.0)))
    return jnp.maximum(-lam, 0.0) + log1p


def _dot(a, b):
    return jnp.dot(a, b, preferred_element_type=F32)


def _dot_nt(a, b):
    return lax.dot_general(a, b, (((1,), (1,)), ((), ())), preferred_element_type=F32)


def _dot_tn(a, b):
    return lax.dot_general(a, b, (((0,), (0,)), ((), ())), preferred_element_type=F32)


def _shift_down(v, halo, k):
    if k == 0:
        return v
    rolled = pltpu.roll(v, k, 0)
    row = lax.broadcasted_iota(jnp.int32, (SUBLANES, v.shape[1]), 0)
    top = jnp.where(row < k, pltpu.roll(halo, k, 0), rolled[:SUBLANES])
    return jnp.concatenate([top, rolled[SUBLANES:]], axis=0)


def _shift_up(v, head, k):
    if k == 0:
        return v
    n = v.shape[0]
    rolled = pltpu.roll(v, n - k, 0)
    row = lax.broadcasted_iota(jnp.int32, (SUBLANES, v.shape[1]), 0)
    bot = jnp.where(row >= SUBLANES - k, pltpu.roll(head, SUBLANES - k, 0), rolled[n - SUBLANES :])
    return jnp.concatenate([rolled[: n - SUBLANES], bot], axis=0)


def _scan_blocks(a_ref, b_ref, out_ref, carry, n_rows, reverse):
    width = a_ref.shape[1]
    row = lax.broadcasted_iota(jnp.int32, (SUBLANES, width), 0)
    n_blocks = n_rows // SUBLANES

    def block(j, carry):
        i = (n_blocks - 1 - j) if reverse else j
        r0 = pl.multiple_of(i * SUBLANES, SUBLANES)
        a = a_ref[pl.ds(r0, SUBLANES), :]
        b = b_ref[pl.ds(r0, SUBLANES), :]
        for d in (1, 2, 4):
            shift = (SUBLANES - d) if reverse else d
            keep = (row < SUBLANES - d) if reverse else (row >= d)
            a_s = pltpu.roll(a, shift, 0)
            b_s = pltpu.roll(b, shift, 0)
            b = jnp.where(keep, a * b_s + b, b)
            a = jnp.where(keep, a * a_s, a)
        h = a * carry + b
        out_ref[pl.ds(r0, SUBLANES), :] = h
        edge = h[0:1, :] if reverse else h[SUBLANES - 1 : SUBLANES, :]
        return jnp.broadcast_to(edge, (SUBLANES, width))

    return lax.fori_loop(0, n_blocks, block, carry)


def _rms_fwd(x, w):
    r = lax.rsqrt(jnp.mean(x * x, axis=-1, keepdims=True) + RMS_EPS)
    xh = x * r
    return xh * w, xh, r


def _rms_bwd(dh, xh, r, w):
    dxh = dh * w
    dx = r * (dxh - xh * jnp.mean(dxh * xh, axis=-1, keepdims=True))
    return dx, jnp.sum(dh * xh, axis=0, keepdims=True)


def _cast_to_segments(w, mine, rows):
    n, c = w.shape
    per = n // 2 // rows

    def body(k_ref, w_ref, o_ref):
        o_ref[...] = w_ref[...].astype(BF16)

    return _pcall(
        body,
        name=f"cast_{n}x{c}",
        grid_spec=pltpu.PrefetchScalarGridSpec(
            num_scalar_prefetch=1,
            grid=(n // rows,),
            in_specs=[pl.BlockSpec((rows, c), lambda i, k_ref: (i, 0))],
            out_specs=pl.BlockSpec((None, rows, c), lambda i, k_ref: (2 * k_ref[0] + i // per, i % per, 0)),
        ),
        out_shape=jax.ShapeDtypeStruct((N_DEV, n // 2, c), BF16),
        compiler_params=_cparams(("arbitrary",)),
    )(_scalars(mine), w)


def _place():
    x, y, c = lax.axis_index("x"), lax.axis_index("y"), lax.axis_index("c")
    chips = [(1 - x, y), (x, 1 - y), (1 - x, 1 - y)]
    return x, y, c, chips


def _chip_no(chip):
    return 2 * chip[0] + chip[1]


def _rcopy(src, dst, send_sem, recv_sem, to):
    return pltpu.make_async_remote_copy(
        src_ref=src, dst_ref=dst, send_sem=send_sem, recv_sem=recv_sem, device_id=to, device_id_type=MESH
    )


def _gather_hook(big, small=None):
    nb = len(big)
    n_sems = 6 * nb + 4

    def ici_copies(ins, outs, send, recv):
        x, y, c, chips = _place()
        copies = []
        for b in range(nb):
            own = outs[b].at[2 * _chip_no((x, y)) + c]
            for j, chip in enumerate(chips):
                copies.append(_rcopy(own, own, send.at[6 * b + j], recv.at[6 * b + j], (*chip, c)))
        if small is not None:
            there = outs[nb].at[_chip_no((x, y))]
            for j, chip in enumerate(chips):
                k = 6 * nb + j
                copies.append(_rcopy(ins[nb], there, send.at[k], recv.at[k], (*chip, c)))
        return copies

    def local_copy(ins, outs, send):
        x, y, _, _ = _place()
        return pltpu.make_async_copy(ins[nb], outs[nb].at[_chip_no((x, y))], send.at[6 * nb + 3])

    def start(ins, outs, send, recv):
        for cp in ici_copies(ins, outs, send, recv):
            cp.start()
        if small is not None:
            local_copy(ins, outs, send).start()

    def finish(ins, outs, send, recv):
        x, y, c, chips = _place()
        me, sibling = (x, y, c), (x, y, 1 - c)
        passed = []
        for b in range(nb):
            for j, chip in enumerate(chips):
                got = outs[b].at[2 * _chip_no(chip) + c]
                _rcopy(got, got, send.at[6 * b + j], recv.at[6 * b + j], me).wait_recv()
                fwd = _rcopy(got, got, send.at[6 * b + 3 + j], recv.at[6 * b + 3 + j], sibling)
                fwd.start()
                passed.append(fwd)
        if small is not None:
            for j, chip in enumerate(chips):
                k = 6 * nb + j
                got = outs[nb].at[_chip_no(chip)]
                _rcopy(got, got, send.at[k], recv.at[k], me).wait_recv()
        for b in range(nb):
            for j, chip in enumerate(chips):
                got = outs[b].at[2 * _chip_no(chip) + 1 - c]
                _rcopy(got, got, send.at[6 * b + 3 + j], recv.at[6 * b + 3 + j], me).wait_recv()
        for cp in ici_copies(ins, outs, send, recv) + passed:
            cp.wait_send()
        if small is not None:
            local_copy(ins, outs, send).wait()

    operands = list(big) + ([small] if small is not None else [])
    out_shapes = [jax.ShapeDtypeStruct(b.shape, b.dtype) for b in big]
    if small is not None:
        out_shapes.append(jax.ShapeDtypeStruct((N_CHIPS, *small.shape), small.dtype))
    return _Hook(operands, out_shapes, {b: b for b in range(nb)}, n_sems, start, finish)


def _both_ways_hook(operands, out_shapes, copies_of, n_sems):
    def start(ins, outs, send, recv):
        for cp in copies_of(ins, outs, send, recv):
            cp.start()

    def finish(ins, outs, send, recv):
        for cp in copies_of(ins, outs, send, recv):
            cp.wait()

    return _Hook(operands, out_shapes, {}, n_sems, start, finish)


def _swap_hook(bufs):
    def copies_of(ins, outs, send, recv):
        x, y, c, _ = _place()
        copies = []
        for b in range(len(bufs)):
            for j in range(N_CHIPS):
                k = b * N_CHIPS + j
                copies.append(_rcopy(ins[b].at[2 * j + 1 - c], outs[b].at[j], send.at[k], recv.at[k], (x, y, 1 - c)))
        return copies

    out_shapes = [jax.ShapeDtypeStruct((N_CHIPS, *b.shape[1:]), b.dtype) for b in bufs]
    return _both_ways_hook(list(bufs), out_shapes, copies_of, len(bufs) * N_CHIPS)


def _send_hook(parts):
    def copies_of(ins, outs, send, recv):
        _, _, c, chips = _place()
        copies = []
        for b in range(len(parts)):
            for j, chip in enumerate(chips):
                k = b * 3 + j
                copies.append(_rcopy(ins[b].at[_chip_no(chip)], outs[b].at[j], send.at[k], recv.at[k], (*chip, c)))
        return copies

    out_shapes = [jax.ShapeDtypeStruct((3, *p.shape[1:]), p.dtype) for p in parts]
    return _both_ways_hook(list(parts), out_shapes, copies_of, len(parts) * 3)


def _share_hook(big, small=None, tiny=None):
    nb = len(big)
    n_sems = nb + 7 + N_DEV
    t0 = nb + 7

    def tiny_copies(ins, outs, send, recv):
        x, y, c, _ = _place()
        there = outs[-1].at[2 * _chip_no((x, y)) + c]
        copies = []
        for r in range(1, N_DEV):
            to = (x ^ (r >> 2 & 1), y ^ (r >> 1 & 1), c ^ (r & 1))
            copies.append(_rcopy(ins[-1], there, send.at[t0 + r], recv.at[t0 + r], to))
        return copies

    def tiny_local(ins, outs, send):
        x, y, c, _ = _place()
        return pltpu.make_async_copy(ins[-1], outs[-1].at[2 * _chip_no((x, y)) + c], send.at[t0])

    def first_copies(outs, send, recv):
        x, y, c, chips = _place()
        sibling = (x, y, 1 - c)
        copies = [_rcopy(outs[b].at[c], outs[b].at[c], send.at[b], recv.at[b], sibling) for b in range(nb)]
        if small is not None:
            own = outs[nb].at[2 * _chip_no((x, y)) + c]
            copies.append(_rcopy(own, own, send.at[nb], recv.at[nb], sibling))
            for j, chip in enumerate(chips):
                copies.append(_rcopy(own, own, send.at[nb + 1 + j], recv.at[nb + 1 + j], (*chip, c)))
        return copies

    def start(ins, outs, send, recv):
        for cp in first_copies(outs, send, recv):
            cp.start()
        if tiny is not None:
            for cp in tiny_copies(ins, outs, send, recv):
                cp.start()
            tiny_local(ins, outs, send).start()

    def finish(ins, outs, send, recv):
        x, y, c, chips = _place()
        me, sibling = (x, y, c), (x, y, 1 - c)
        if tiny is not None:
            for cp in tiny_copies(ins, outs, send, recv):
                cp.wait()
            tiny_local(ins, outs, send).wait()
        passed = []
        if small is not None:
            for j, chip in enumerate(chips):
                got = outs[nb].at[2 * _chip_no(chip) + c]
                _rcopy(got, got, send.at[nb + 1 + j], recv.at[nb + 1 + j], me).wait_recv()
                fwd = _rcopy(got, got, send.at[nb + 4 + j], recv.at[nb + 4 + j], sibling)
                fwd.start()
                passed.append(fwd)
        for b in range(nb):
            got = outs[b].at[1 - c]
            _rcopy(got, got, send.at[b], recv.at[b], me).wait_recv()
        if small is not None:
            got = outs[nb].at[2 * _chip_no((x, y)) + 1 - c]
            _rcopy(got, got, send.at[nb], recv.at[nb], me).wait_recv()
            for j, chip in enumerate(chips):
                got = outs[nb].at[2 * _chip_no(chip) + 1 - c]
                _rcopy(got, got, send.at[nb + 4 + j], recv.at[nb + 4 + j], me).wait_recv()
        for cp in first_copies(outs, send, recv) + passed:
            cp.wait_send()

    operands = list(big) + ([small] if small is not None else [])
    out_shapes = [jax.ShapeDtypeStruct(a.shape, a.dtype) for a in operands]
    aliases = {i: i for i in range(len(operands))}
    if tiny is not None:
        operands.append(tiny)
        out_shapes.append(jax.ShapeDtypeStruct((N_DEV, *tiny.shape), tiny.dtype))
    return _Hook(operands, out_shapes, aliases, n_sems, start, finish)


def _row_tile(rows, cols, target_bytes=2 * 1024 * 1024):
    best = SUBLANES
    for t in range(SUBLANES, rows + 1, SUBLANES):
        if rows % t == 0 and t * cols * 4 <= target_bytes:
            best = t
    return best


def _add_own_half(buf, got, c, wire):
    _, rows, cols = buf.shape
    tr = _row_tile(rows, cols)

    def body(c_ref, a_ref, b_ref, o_ref):
        o_ref[...] = (a_ref[...] + b_ref[...]).astype(wire)

    return _pcall(
        body,
        name=f"add_own_half_{rows}x{cols}",
        grid_spec=pltpu.PrefetchScalarGridSpec(
            num_scalar_prefetch=1,
            grid=(N_CHIPS, rows // tr),
            in_specs=[
                pl.BlockSpec((None, None, tr, cols), lambda j, r, c_ref: (j, c_ref[0], r, 0)),
                pl.BlockSpec((None, tr, cols), lambda j, r, c_ref: (j, r, 0)),
            ],
            out_specs=pl.BlockSpec((None, tr, cols), lambda j, r, c_ref: (j, r, 0)),
        ),
        out_shape=jax.ShapeDtypeStruct((N_CHIPS, rows, cols), wire),
        compiler_params=_cparams(("arbitrary", "arbitrary")),
    )(_scalars(c), buf.reshape(N_CHIPS, 2, rows, cols), got)


def _add_received(buf, got_a, got_b, mine, c, slot, n_slots):
    _, rows, cols = buf.shape
    tr = _row_tile(rows, cols)

    def body(s_ref, x_ref, a_ref, g_ref, o_ref):
        own = x_ref[...] + a_ref[...]
        o_ref[...] = ((own + g_ref[0].astype(F32)) + g_ref[1].astype(F32)) + g_ref[2].astype(F32)

    return _pcall(
        body,
        name=f"add_received_{rows}x{cols}",
        grid_spec=pltpu.PrefetchScalarGridSpec(
            num_scalar_prefetch=1,
            grid=(rows // tr,),
            in_specs=[
                pl.BlockSpec((None, None, tr, cols), lambda r, s_ref: (s_ref[0], s_ref[1], r, 0)),
                pl.BlockSpec((None, tr, cols), lambda r, s_ref: (s_ref[0], r, 0)),
                pl.BlockSpec((3, tr, cols), lambda r, s_ref: (0, r, 0)),
            ],
            out_specs=pl.BlockSpec((None, tr, cols), lambda r, s_ref: (s_ref[2], r, 0)),
        ),
        out_shape=jax.ShapeDtypeStruct((n_slots, rows, cols), F32),
        compiler_params=_cparams(("arbitrary",)),
    )(_scalars(mine, c, slot), buf.reshape(N_CHIPS, 2, rows, cols), got_a, got_b)


def _layer_a_fwd(x, nw, win, ln_w, ln_b, wc, bs_t, wout, tm, hook):
    t_rows, d = x.shape
    n_sh, _, s_cols = win.shape
    aw = wout.shape[0]
    gd = aw // A_GROUPS
    tn = 512
    assert s_cols % tn == 0 and aw % tn == 0 and tm % CHUNK == 0

    def body(x_ref, nw_ref, win_ref, lnw_ref, lnb_ref, wc_ref, bst_ref, wout_ref, z_ref, x1_ref, h_ref, u_s, v_s, y_s):
        x = x_ref[...]
        h, _, _ = _rms_fwd(x, nw_ref[...])
        h = h.astype(BF16)
        h_ref[...] = h
        for j in range(3 * aw // tn):
            k, off = divmod(j * tn, s_cols)
            cols = slice((j * tn) % aw, (j * tn) % aw + tn)
            zj = _dot(h, win_ref[k, :, off : off + tn])
            z_ref[:, j * tn : (j + 1) * tn] = zj
            if j * tn < aw:
                u_s[:, cols] = _gelu(zj)
            elif j * tn < 2 * aw:
                v_s[:, cols] = _gelu(zj)
            else:
                u_s[:, cols] = u_s[:, cols] * (zj * _sigmoid(zj))
        v = v_s[...]
        mu = jnp.mean(v, axis=-1, keepdims=True)
        vc = v - mu
        rstd = lax.rsqrt(jnp.mean(vc * vc, axis=-1, keepdims=True) + LN_EPS)
        v_s[...] = (vc * rstd) * lnw_ref[...] + lnb_ref[...]
        for ck in range(tm // CHUNK):
            rows = slice(ck * CHUNK, (ck + 1) * CHUNK)
            for g in range(A_GROUPS):
                cols = slice(g * gd, (g + 1) * gd)
                s = _dot(wc_ref[g], v_s[rows, cols].astype(BF16)) + bst_ref[:, g : g + 1]
                y_s[rows, cols] = (u_s[rows, cols] * s).astype(BF16)
        x1_ref[...] = x + _dot(y_s[...], wout_ref[...])

    row = lambda i: (i, 0)
    return _pcall(
        body,
        hook,
        name="layer_a_fwd",
        grid=(t_rows // tm,),
        in_specs=[
            pl.BlockSpec((tm, d), row),
            _full(nw.shape),
            _full(win.shape),
            _full(ln_w.shape),
            _full(ln_b.shape),
            _full(wc.shape),
            _full(bs_t.shape),
            _full(wout.shape),
        ],
        out_specs=[pl.BlockSpec((tm, 3 * aw), row), pl.BlockSpec((tm, d), row), pl.BlockSpec((tm, d), row)],
        out_shape=[
            jax.ShapeDtypeStruct((t_rows, 3 * aw), F32),
            jax.ShapeDtypeStruct((t_rows, d), F32),
            jax.ShapeDtypeStruct((t_rows, d), BF16),
        ],
        scratch_shapes=[pltpu.VMEM((tm, aw), F32), pltpu.VMEM((tm, aw), F32), pltpu.VMEM((tm, aw), BF16)],
        compiler_params=_cparams(("arbitrary",)),
    )(x, nw, win, ln_w, ln_b, wc, bs_t, wout)


def _layer_a_bwd(dout, z, ln_w, ln_b, wc, wct, bs_t, wout, tiles, earlier, hook):
    t_rows, d = dout.shape
    aw = wout.shape[0]
    gd = aw // A_GROUPS
    tm = TM_A_BWD
    lo, hi = tiles
    n_earlier = 0 if earlier is None else len(earlier)

    def body(dout_ref, z_ref, lnw_ref, lnb_ref, wc_ref, wct_ref, bst_ref, wout_ref, *rest):
        dz_ref, y_ref, dob_ref, gws_ref, gbs_ref, glnw_ref, glnb_ref, u_s, vh_s, ds_s, dvn_s = rest[n_earlier:]

        @pl.when(pl.program_id(0) == 0)
        def _():
            gws_ref[...] = jnp.zeros_like(gws_ref)
            gbs_ref[...] = jnp.zeros_like(gbs_ref)
            glnw_ref[...] = jnp.zeros_like(glnw_ref)
            glnb_ref[...] = jnp.zeros_like(glnb_ref)

        dob = dout_ref[...].astype(BF16)
        dob_ref[...] = dob
        dy = _dot_nt(dob, wout_ref[...])

        zv = z_ref[:, aw : 2 * aw]
        vg, dvg_dz = _gelu_and_grad(zv)
        mu = jnp.mean(vg, axis=-1, keepdims=True)
        vc = vg - mu
        rstd = lax.rsqrt(jnp.mean(vc * vc, axis=-1, keepdims=True) + LN_EPS)
        vh = vc * rstd
        vh_s[...] = vh
        vn = (vh * lnw_ref[...] + lnb_ref[...]).astype(BF16)

        zu = z_ref[:, 0:aw]
        zg = z_ref[:, 2 * aw : 3 * aw]
        u, du_dz = _gelu_and_grad(zu)
        sg, dsg = _silu_and_grad(zg)
        u_s[...] = u * sg
        tril = lax.broadcasted_iota(jnp.int32, (CHUNK, CHUNK), 0) >= lax.broadcasted_iota(jnp.int32, (CHUNK, CHUNK), 1)
        for ck in range(tm // CHUNK):
            rows = slice(ck * CHUNK, (ck + 1) * CHUNK)
            for g in range(A_GROUPS):
                cols = slice(g * gd, (g + 1) * gd)
                vn_g = vn[rows, cols]
                s = _dot(wc_ref[g], vn_g) + bst_ref[:, g : g + 1]
                usg = u_s[rows, cols]
                dy_g = dy[rows, cols]
                y_ref[rows, cols] = (usg * s).astype(BF16)
                ds = dy_g * usg
                ds_s[rows, cols] = dy_g * s
                gbs_ref[:, g : g + 1] += jnp.sum(ds, axis=-1, keepdims=True)
                dsb = ds.astype(BF16)
                gws_ref[g] += jnp.where(tril, _dot_nt(dsb, vn_g), 0.0)
                dvn_s[rows, cols] = _dot(wct_ref[g], dsb)
        dusg = ds_s[...]
        dz_ref[:, 0:aw] = (dusg * sg * du_dz).astype(BF16)
        dz_ref[:, 2 * aw : 3 * aw] = (dusg * u * dsg).astype(BF16)

        dvn = dvn_s[...]
        vh = vh_s[...]
        glnw_ref[...] += jnp.sum(dvn * vh, axis=0, keepdims=True)
        glnb_ref[...] += jnp.sum(dvn, axis=0, keepdims=True)
        dvh = dvn * lnw_ref[...]
        dvg = rstd * (dvh - jnp.mean(dvh, axis=-1, keepdims=True) - vh * jnp.mean(dvh * vh, axis=-1, keepdims=True))
        dz_ref[:, aw : 2 * aw] = (dvg * dvg_dz).astype(BF16)

    row = lambda i: (i + lo, 0)
    call = _pcall(
        body,
        hook,
        name=f"layer_a_bwd_{lo}",
        grid=(hi - lo,),
        in_specs=[
            pl.BlockSpec((tm, d), row),
            pl.BlockSpec((tm, 3 * aw), row),
            _full(ln_w.shape),
            _full(ln_b.shape),
            _full(wc.shape),
            _full(wct.shape),
            _full(bs_t.shape),
            _full(wout.shape),
        ]
        + [ANY] * n_earlier,
        out_specs=[
            pl.BlockSpec((tm, 3 * aw), row),
            pl.BlockSpec((tm, aw), row),
            pl.BlockSpec((tm, d), row),
            _full((A_GROUPS, CHUNK, CHUNK)),
            _full((CHUNK, A_GROUPS)),
            _full((1, aw)),
            _full((1, aw)),
        ],
        out_shape=[
            jax.ShapeDtypeStruct((t_rows, 3 * aw), BF16),
            jax.ShapeDtypeStruct((t_rows, aw), BF16),
            jax.ShapeDtypeStruct((t_rows, d), BF16),
            jax.ShapeDtypeStruct((A_GROUPS, CHUNK, CHUNK), F32),
            jax.ShapeDtypeStruct((CHUNK, A_GROUPS), F32),
            jax.ShapeDtypeStruct((1, aw), F32),
            jax.ShapeDtypeStruct((1, aw), F32),
        ],
        scratch_shapes=[pltpu.VMEM((tm, aw), F32)] * 4,
        input_output_aliases={8 + i: i for i in range(n_earlier)},
        compiler_params=_cparams(("arbitrary",)),
    )
    return call(dout, z, ln_w, ln_b, wc, wct, bs_t, wout, *(earlier or ()))


def _layer_a_bwd_dx(dout, x, dz, nw, win, tm, hook):
    t_rows, d = x.shape
    n_sh, _, s_cols = win.shape

    def body(dout_ref, x_ref, dz_ref, nw_ref, win_ref, gx_ref, gnw_ref):
        @pl.when(pl.program_id(0) == 0)
        def _():
            gnw_ref[...] = jnp.zeros_like(gnw_ref)

        dh = jnp.zeros((tm, d), F32)
        for k in range(n_sh):
            dh = dh + _dot_nt(dz_ref[:, k * s_cols : (k + 1) * s_cols], win_ref[k])
        nw = nw_ref[...]
        _, xh, r = _rms_fwd(x_ref[...], nw)
        dx, gnw = _rms_bwd(dh, xh, r, nw)
        gnw_ref[0:1, :] += gnw
        gx_ref[...] = dout_ref[...] + dx

    row = lambda i: (i, 0)
    return _pcall(
        body,
        hook,
        name="layer_a_bwd_dx",
        grid=(t_rows // tm,),
        in_specs=[
            pl.BlockSpec((tm, d), row),
            pl.BlockSpec((tm, d), row),
            pl.BlockSpec((tm, n_sh * s_cols), row),
            _full(nw.shape),
            _full(win.shape),
        ],
        out_specs=[pl.BlockSpec((tm, d), row), _full((SUBLANES, d))],
        out_shape=[jax.ShapeDtypeStruct((t_rows, d), F32), jax.ShapeDtypeStruct((SUBLANES, d), F32)],
        compiler_params=_cparams(("arbitrary",)),
    )(dout, x, dz, nw, win)


def _decay(r, sp_h):
    log_a = (-RG_C) * r * sp_h
    a = jnp.exp(log_a)
    mult = jnp.sqrt(jnp.tanh(-log_a) * (a * a + 1.0))
    return a, mult


def _gates(xc_h, gab_ref, gb_ref, sp_h, h, hd):
    pre = _dot(xc_h.astype(BF16), gab_ref[h])
    bw = gb_ref.shape[1] // 2
    r = _sigmoid(pre[:, :hd] + gb_ref[:, h * hd : (h + 1) * hd])
    ig = _sigmoid(pre[:, hd:] + gb_ref[:, bw + h * hd : bw + (h + 1) * hd])
    a, mult = _decay(r, sp_h)
    return r, ig, a, mult


def _conv(xb, halo, cw_ref, cb_ref):
    xc = cb_ref[...] + cw_ref[CONV_WIDTH - 1 : CONV_WIDTH, :] * xb
    for k in range(CONV_WIDTH - 1):
        xc = xc + cw_ref[k : k + 1, :] * _shift_down(xb, halo, CONV_WIDTH - 1 - k)
    return xc


def _layer_b_fwd(x1, nw, bin_w, cw, cb, gab, gb, lam, bout, nf, tgt, tm):
    t_rows, d = x1.shape
    bw = bout.shape[0]
    hd = bw // B_HEADS
    nt = t_rows // tm

    def body(
        x1_ref, nw_ref, bin_ref, cw_ref, cb_ref, gab_ref, gb_ref, lam_ref, bout_ref, nf_ref, tgt_ref,
        z_ref, h_ref, h1_ref, xbt_ref, ht_ref, dx2_ref, loss_ref, gnf_ref,
        tail_s, carry_s, a_s, b_s, hs_s, acc_s,
    ):
        @pl.when(pl.program_id(0) == 0)
        def _():
            tail_s[...] = jnp.zeros_like(tail_s)
            carry_s[...] = jnp.zeros_like(carry_s)
            acc_s[...] = jnp.zeros_like(acc_s)
            gnf_ref[...] = jnp.zeros_like(gnf_ref)

        x1 = x1_ref[...]
        h1, _, _ = _rms_fwd(x1, nw_ref[...])
        h1 = h1.astype(BF16)
        h1_ref[...] = h1
        z = jnp.concatenate([_dot(h1, bin_ref[k]) for k in range(N_CHIPS)], axis=1)
        z_ref[...] = z
        xb = z[:, :bw]
        xc = _conv(xb, tail_s[...], cw_ref, cb_ref)
        tail = xb[tm - SUBLANES :, :]
        tail_s[...] = tail
        xbt_ref[...] = tail
        sp = _softplus_neg(lam_ref[...])
        for h in range(B_HEADS):
            cols = slice(h * hd, (h + 1) * hd)
            xc_h = xc[:, cols]
            _, ig, a, mult = _gates(xc_h, gab_ref, gb_ref, sp[:, cols], h, hd)
            a_s[:, cols] = a
            b_s[:, cols] = mult * (ig * xc_h)
        carry = _scan_blocks(a_s, b_s, hs_s, carry_s[...], tm, reverse=False)
        carry_s[...] = carry
        ht_ref[...] = hs_s[tm - SUBLANES :, :]
        hs = hs_s[...]
        h_ref[...] = hs
        g = z[:, bw:]
        y = (hs * (g * _sigmoid(g))).astype(BF16)
        x2 = x1 + _dot(y, bout_ref[...])

        nf = nf_ref[...]
        o, xh, r = _rms_fwd(x2, nf)
        diff = o - tgt_ref[...]
        acc_s[...] += jnp.sum(diff * diff, axis=0, keepdims=True)
        do = diff * (1.0 / d)
        dx2, gnf = _rms_bwd(do, xh, r, nf)
        gnf_ref[...] += gnf
        dx2_ref[...] = dx2

        @pl.when(pl.program_id(0) == nt - 1)
        def _():
            total = jnp.sum(acc_s[...], axis=-1, keepdims=True) * (0.5 / d)
            loss_ref[...] = jnp.broadcast_to(total, loss_ref.shape)

    row = lambda i: (i, 0)
    return _pcall(
        body,
        name="layer_b_fwd",
        grid=(nt,),
        in_specs=[
            pl.BlockSpec((tm, d), row),
            _full(nw.shape),
            _full(bin_w.shape),
            _full(cw.shape),
            _full(cb.shape),
            _full(gab.shape),
            _full(gb.shape),
            _full(lam.shape),
            _full(bout.shape),
            _full(nf.shape),
            pl.BlockSpec((tm, d), row),
        ],
        out_specs=[
            pl.BlockSpec((tm, 2 * bw), row),
            pl.BlockSpec((tm, bw), row),
            pl.BlockSpec((tm, d), row),
            pl.BlockSpec((None, SUBLANES, bw), lambda i: (i, 0, 0)),
            pl.BlockSpec((None, SUBLANES, bw), lambda i: (i, 0, 0)),
            pl.BlockSpec((tm, d), row),
            _full((1, LANES)),
            _full((1, d)),
        ],
        out_shape=[
            jax.ShapeDtypeStruct((t_rows, 2 * bw), F32),
            jax.ShapeDtypeStruct((t_rows, bw), F32),
            jax.ShapeDtypeStruct((t_rows, d), BF16),
            jax.ShapeDtypeStruct((nt, SUBLANES, bw), F32),
            jax.ShapeDtypeStruct((nt, SUBLANES, bw), F32),
            jax.ShapeDtypeStruct((t_rows, d), F32),
            jax.ShapeDtypeStruct((1, LANES), F32),
            jax.ShapeDtypeStruct((1, d), F32),
        ],
        scratch_shapes=[
            pltpu.VMEM((SUBLANES, bw), F32),
            pltpu.VMEM((SUBLANES, bw), F32),
            pltpu.VMEM((tm, bw), F32),
            pltpu.VMEM((tm, bw), F32),
            pltpu.VMEM((tm, bw), F32),
            pltpu.VMEM((1, d), F32),
        ],
        compiler_params=_cparams(("arbitrary",)),
    )(x1, nw, bin_w, cw, cb, gab, gb, lam, bout, nf, tgt)


def _layer_b_bwd(dout, x1, z, hseq, xb_tails, h_tails, nw, bin_w, cw, cb, gab, gabt, gb, lam, bout, tm):
    t_rows, d = x1.shape
    bw = bout.shape[0]
    hd = bw // B_HEADS
    nt = t_rows // tm

    def body(
        dout_ref, x1_ref, z_ref, h_ref, xbt_ref, ht_ref, nw_ref, bin_ref, cw_ref, cb_ref, gab_ref, gabt_ref,
        gb_ref, lam_ref, bout_ref,
        dx1_ref, dz_ref, y_ref, dob_ref, ggab_ref, ggb_ref, gcw_ref, gcb_ref, glam_ref, gnw_ref,
        gcarry_s, afirst_s, head_s, aup_s, dh_s, gt_s, dxc_s, xc_s, r_s, ig_s,
    ):
        step = pl.program_id(0)
        tile = nt - 1 - step

        @pl.when(step == 0)
        def _():
            for ref in (ggab_ref, ggb_ref, gcw_ref, gcb_ref, glam_ref, gnw_ref, gcarry_s, afirst_s, head_s):
                ref[...] = jnp.zeros_like(ref)

        first_tile = tile == 0
        xb_halo = jnp.where(first_tile, 0.0, xbt_ref[...])
        h_halo = jnp.where(first_tile, 0.0, ht_ref[...])

        dout = dout_ref[...]
        dob = dout.astype(BF16)
        dob_ref[...] = dob
        dy = _dot_nt(dob, bout_ref[...])
        hs = h_ref[...]
        g = z_ref[:, bw:]
        sg, dsg = _silu_and_grad(g)
        y_ref[...] = (hs * sg).astype(BF16)
        dz_ref[:, bw:] = (dy * hs * dsg).astype(BF16)
        dh_s[...] = dy * sg

        xb = z_ref[:, :bw]
        xc = _conv(xb, xb_halo, cw_ref, cb_ref)
        xc_s[...] = xc
        lam = lam_ref[...]
        sp = _softplus_neg(lam)
        for h in range(B_HEADS):
            cols = slice(h * hd, (h + 1) * hd)
            r, ig, a, _ = _gates(xc[:, cols], gab_ref, gb_ref, sp[:, cols], h, hd)
            r_s[:, cols] = r
            ig_s[:, cols] = ig
            aup_s[:, cols] = _shift_up(a, afirst_s[:, cols], 1)
            afirst_s[:, cols] = jnp.broadcast_to(a[0:1, :], (SUBLANES, hd))
        carry = _scan_blocks(aup_s, dh_s, gt_s, gcarry_s[...], tm, reverse=True)
        gcarry_s[...] = carry

        h_prev = _shift_down(hs, h_halo, 1)
        for h in range(B_HEADS):
            cols = slice(h * hd, (h + 1) * hd)
            xc_h = xc_s[:, cols]
            sp_h = sp[:, cols]
            r, ig = r_s[:, cols], ig_s[:, cols]
            a, mult = _decay(r, sp_h)
            gt = gt_s[:, cols]
            da = gt * h_prev[:, cols]
            dmult = gt * (ig * xc_h)
            dig = gt * (mult * xc_h)
            dxc_direct = gt * (mult * ig)
            dla = da * a - dmult * (a * a) / mult
            glam_ref[:, cols] += jnp.sum(dla * r, axis=0, keepdims=True)
            dr = dla * ((-RG_C) * sp_h)
            dpre = jnp.concatenate([dr * r * (1.0 - r), dig * ig * (1.0 - ig)], axis=1)
            ggb_ref[:, cols] += jnp.sum(dpre[:, :hd], axis=0, keepdims=True)
            ggb_ref[:, bw + h * hd : bw + (h + 1) * hd] += jnp.sum(dpre[:, hd:], axis=0, keepdims=True)
            dpb = dpre.astype(BF16)
            ggab_ref[h] += _dot_tn(xc_h.astype(BF16), dpb)
            dxc_s[:, cols] = dxc_direct + _dot(dpb, gabt_ref[h])
        glam_ref[...] = jnp.where(step == nt - 1, glam_ref[...] * (RG_C * _sigmoid(-lam)), glam_ref[...])

        dxc = dxc_s[...]
        gcb_ref[...] += jnp.sum(dxc, axis=0, keepdims=True)
        dxb = cw_ref[CONV_WIDTH - 1 : CONV_WIDTH, :] * dxc
        gcw_ref[CONV_WIDTH - 1 : CONV_WIDTH, :] += jnp.sum(dxc * xb, axis=0, keepdims=True)
        head = head_s[...]
        for k in range(CONV_WIDTH - 1):
            lag = CONV_WIDTH - 1 - k
            dxb = dxb + cw_ref[k : k + 1, :] * _shift_up(dxc, head, lag)
            gcw_ref[k : k + 1, :] += jnp.sum(dxc * _shift_down(xb, xb_halo, lag), axis=0, keepdims=True)
        head_s[...] = dxc[:SUBLANES, :]
        dz_ref[:, :bw] = dxb.astype(BF16)

        s_cols = 2 * bw // N_CHIPS
        dh1 = jnp.zeros((tm, d), F32)
        for k in range(N_CHIPS):
            dh1 = dh1 + _dot_nt(dz_ref[:, k * s_cols : (k + 1) * s_cols], bin_ref[k])
        x1 = x1_ref[...]
        nw = nw_ref[...]
        _, xh, r1 = _rms_fwd(x1, nw)
        dx, gnw = _rms_bwd(dh1, xh, r1, nw)
        gnw_ref[...] += gnw
        dx1_ref[...] = dout + dx

    rev = lambda i: (nt - 1 - i, 0)
    prev = lambda i: (jnp.maximum(nt - 2 - i, 0), 0, 0)
    return _pcall(
        body,
        name="layer_b_bwd",
        grid=(nt,),
        in_specs=[
            pl.BlockSpec((tm, d), rev),
            pl.BlockSpec((tm, d), rev),
            pl.BlockSpec((tm, 2 * bw), rev),
            pl.BlockSpec((tm, bw), rev),
            pl.BlockSpec((None, SUBLANES, bw), prev),
            pl.BlockSpec((None, SUBLANES, bw), prev),
            _full(nw.shape),
            _full(bin_w.shape),
            _full(cw.shape),
            _full(cb.shape),
            _full(gab.shape),
            _full(gabt.shape),
            _full(gb.shape),
            _full(lam.shape),
            _full(bout.shape),
        ],
        out_specs=[
            pl.BlockSpec((tm, d), rev),
            pl.BlockSpec((tm, 2 * bw), rev),
            pl.BlockSpec((tm, bw), rev),
            pl.BlockSpec((tm, d), rev),
            _full((B_HEADS, hd, 2 * hd)),
            _full((1, 2 * bw)),
            _full((SUBLANES, bw)),
            _full((1, bw)),
            _full((1, bw)),
            _full((1, d)),
        ],
        out_shape=[
            jax.ShapeDtypeStruct((t_rows, d), F32),
            jax.ShapeDtypeStruct((t_rows, 2 * bw), BF16),
            jax.ShapeDtypeStruct((t_rows, bw), BF16),
            jax.ShapeDtypeStruct((t_rows, d), BF16),
            jax.ShapeDtypeStruct((B_HEADS, hd, 2 * hd), F32),
            jax.ShapeDtypeStruct((1, 2 * bw), F32),
            jax.ShapeDtypeStruct((SUBLANES, bw), F32),
            jax.ShapeDtypeStruct((1, bw), F32),
            jax.ShapeDtypeStruct((1, bw), F32),
            jax.ShapeDtypeStruct((1, d), F32),
        ],
        scratch_shapes=[pltpu.VMEM((SUBLANES, bw), F32)] * 3 + [pltpu.VMEM((tm, bw), F32)] * 7,
        compiler_params=_cparams(("arbitrary",)),
    )(dout, x1, z, hseq, xb_tails, h_tails, nw, bin_w, cw, cb, gab, gabt, gb, lam, bout)


def _wgrad(a, b, m_blocks, n_blocks, hook=None):
    k, m = a.shape
    n = b.shape[1]
    bm, bn = m // m_blocks, n // n_blocks

    def body(a_ref, b_ref, o_ref):
        o_ref[...] = _dot_tn(a_ref[...], b_ref[...])

    out = _pcall(
        body,
        hook,
        name=f"wgrad_{m}x{n}",
        grid=(n_blocks, m_blocks),
        in_specs=[pl.BlockSpec((k, bm), lambda j, i: (0, i)), pl.BlockSpec((k, bn), lambda j, i: (0, j))],
        out_specs=[pl.BlockSpec((None, None, bm, bn), lambda j, i: (j, i, 0, 0))],
        out_shape=[jax.ShapeDtypeStruct((n_blocks, m_blocks, bm, bn), F32)],
        compiler_params=_cparams(("arbitrary", "arbitrary")),
    )(a, b)
    return out[0] if hook is None else (out[0][0], out[1])


def _adamw_math(w, g, m, v):
    m = ADAM_B1 * m + (1.0 - ADAM_B1) * g
    v = ADAM_B2 * v + (1.0 - ADAM_B2) * (g * g)
    m_hat = m / (1.0 - ADAM_B1**ADAM_STEP)
    v_hat = v / (1.0 - ADAM_B2**ADAM_STEP)
    delta = -ADAM_LR * (m_hat / (jnp.sqrt(v_hat) + ADAM_EPS) + ADAM_WD * w)
    return delta, m, v


def _adamw(w, g, m, v, hook=None):
    rows, cols = w.shape
    tr = _row_tile(rows, cols, 1024 * 1024)

    def body(w_ref, g_ref, m_ref, v_ref, d_ref, mo_ref, vo_ref):
        d_ref[...], mo_ref[...], vo_ref[...] = _adamw_math(w_ref[...], g_ref[...], m_ref[...], v_ref[...])

    spec = pl.BlockSpec((tr, cols), lambda i: (i, 0))
    return _pcall(
        body,
        hook,
        name=f"adamw_{rows}x{cols}",
        grid=(rows // tr,),
        in_specs=[spec] * 4,
        out_specs=[spec] * 3,
        out_shape=[jax.ShapeDtypeStruct((rows, cols), F32)] * 3,
        compiler_params=_cparams(("arbitrary",)),
    )(w, g, m, v)


def _sum_partials(parts):
    def body(p_ref, o_ref):
        total = p_ref[0, 0:1, :]
        for k in range(1, N_DEV):
            total = total + p_ref[k, 0:1, :]
        o_ref[...] = total

    vmem = pl.BlockSpec(memory_space=pltpu.VMEM)
    return _pcall(
        body,
        name="sum_partials",
        in_specs=[vmem],
        out_specs=vmem,
        out_shape=jax.ShapeDtypeStruct((1, parts.shape[2]), F32),
    )(parts)


def _adamw_many(ws, gs, ms, vs, name, hook=None):
    n = len(ws)

    def body(*refs):
        w_refs, g_refs, m_refs, v_refs = (refs[i * n : (i + 1) * n] for i in range(4))
        d_refs, mo_refs, vo_refs = (refs[(4 + i) * n : (5 + i) * n] for i in range(3))
        for i in range(n):
            d_refs[i][...], mo_refs[i][...], vo_refs[i][...] = _adamw_math(
                w_refs[i][...], g_refs[i][...], m_refs[i][...], v_refs[i][...]
            )

    vmem = pl.BlockSpec(memory_space=pltpu.VMEM)
    outs = _pcall(
        body,
        hook,
        name=name,
        in_specs=[vmem] * (4 * n),
        out_specs=[vmem] * (3 * n),
        out_shape=[jax.ShapeDtypeStruct(w.shape, F32) for w in ws] * 3,
        compiler_params=_cparams(),
    )(*ws, *gs, *ms, *vs)
    extra = None
    if hook is not None:
        outs, extra = outs
    return (outs[:n], outs[n : 2 * n], outs[2 * n :]), extra


def _pack_rows(parts, lanes=LANES):
    flat = jnp.concatenate([p.reshape(-1) for p in parts])
    per = N_DEV * SUBLANES * lanes
    total = -(-flat.shape[0] // per) * per
    flat = jnp.pad(flat, (0, total - flat.shape[0]))
    return flat.reshape(N_DEV, total // (N_DEV * lanes), lanes)


def _unpack(flat, shapes):
    out, at = [], 0
    for s in shapes:
        n = 1
        for dim in s:
            n *= dim
        out.append(flat[at : at + n].reshape(s))
        at += n
    return out


def kernel(x, norm_w, a_w_in, a_ln_w, a_ln_b, a_w_s, a_b_s, a_w_out, b_w_in, b_conv_w, b_conv_b, b_gate_a_w, b_gate_a_b, b_gate_x_w, b_gate_x_b, b_lambda, b_w_out, norm_f_w, loss_target, m_norm_w, m_a_w_in, m_a_ln_w, m_a_ln_b, m_a_w_s, m_a_b_s, m_a_w_out, m_b_w_in, m_b_conv_w, m_b_conv_b, m_b_gate_a_w, m_b_gate_a_b, m_b_gate_x_w, m_b_gate_x_b, m_b_lambda, m_b_w_out, m_norm_f_w, v_norm_w, v_a_w_in, v_a_ln_w, v_a_ln_b, v_a_w_s, v_a_b_s, v_a_w_out, v_b_w_in, v_b_conv_w, v_b_conv_b, v_b_gate_a_w, v_b_gate_a_b, v_b_gate_x_w, v_b_gate_x_b, v_b_lambda, v_b_w_out, v_norm_f_w):
    t_rows, d = x.shape[1], x.shape[2]
    aw = a_ln_w.shape[1]
    bw = b_gate_a_w.shape[1] * b_gate_a_w.shape[2]
    hd = bw // B_HEADS
    mine = 2 * lax.axis_index("x") + lax.axis_index("y")
    core = lax.axis_index("c")
    weights = dict(norm_w=norm_w, a_w_in=a_w_in, a_ln_w=a_ln_w, a_ln_b=a_ln_b, a_w_s=a_w_s, a_b_s=a_b_s, a_w_out=a_w_out, b_w_in=b_w_in, b_conv_w=b_conv_w, b_conv_b=b_conv_b, b_gate_a_w=b_gate_a_w, b_gate_a_b=b_gate_a_b, b_gate_x_w=b_gate_x_w, b_gate_x_b=b_gate_x_b, b_lambda=b_lambda, b_w_out=b_w_out, norm_f_w=norm_f_w)
    m_in = dict(norm_w=m_norm_w, a_w_in=m_a_w_in, a_ln_w=m_a_ln_w, a_ln_b=m_a_ln_b, a_w_s=m_a_w_s, a_b_s=m_a_b_s, a_w_out=m_a_w_out, b_w_in=m_b_w_in, b_conv_w=m_b_conv_w, b_conv_b=m_b_conv_b, b_gate_a_w=m_b_gate_a_w, b_gate_a_b=m_b_gate_a_b, b_gate_x_w=m_b_gate_x_w, b_gate_x_b=m_b_gate_x_b, b_lambda=m_b_lambda, b_w_out=m_b_w_out, norm_f_w=m_norm_f_w)
    v_in = dict(norm_w=v_norm_w, a_w_in=v_a_w_in, a_ln_w=v_a_ln_w, a_ln_b=v_a_ln_b, a_w_s=v_a_w_s, a_b_s=v_a_b_s, a_w_out=v_a_w_out, b_w_in=v_b_w_in, b_conv_w=v_b_conv_w, b_conv_b=v_b_conv_b, b_gate_a_w=v_b_gate_a_w, b_gate_a_b=v_b_gate_a_b, b_gate_x_w=v_b_gate_x_w, b_gate_x_b=v_b_gate_x_b, b_lambda=v_b_lambda, b_w_out=v_b_w_out, norm_f_w=v_norm_f_w)

    win_l = _cast_to_segments(a_w_in[0], mine, 256)
    wout_l = _cast_to_segments(a_w_out[0], mine, 256)
    bin_l = _cast_to_segments(b_w_in[0], mine, 256)
    bout_l = _cast_to_segments(b_w_out[0], mine, 192)
    small_l = jnp.concatenate([b_conv_w[0], b_conv_b, b_gate_a_b, b_gate_x_b, b_lambda], axis=0)
    win_g, wout_g, small_g = _run_hook(_gather_hook([win_l, wout_l], small_l), "gather_layer_a")
    win = win_g.reshape(N_CHIPS, d, -1)
    wout = wout_g.reshape(aw, d)

    tril = jnp.tril(jnp.ones((CHUNK, CHUNK), F32))
    wc = (a_w_s[0] * tril[None]).astype(BF16)
    wct = jnp.swapaxes(wc, 1, 2)
    bs_t = a_b_s[0].T
    gab = jnp.concatenate([b_gate_a_w[0], b_gate_x_w[0]], axis=2).astype(BF16)
    gabt = jnp.swapaxes(gab, 1, 2)
    nw0, nw1, nf = norm_w[0:1], norm_w[1:2], norm_f_w.reshape(1, d)

    x0 = x[0]
    (z_a, x1, h0), (bin_g, bout_g) = _layer_a_fwd(
        x0, nw0, win, a_ln_w, a_ln_b, wc, bs_t, wout, TM_FWD, _gather_hook([bin_l, bout_l])
    )
    bin_w = bin_g.reshape(N_CHIPS, d, -1)
    bout = bout_g.reshape(bw, d)
    small_f = jnp.transpose(small_g, (1, 0, 2)).reshape(SUBLANES, bw)
    cw, cb = small_f[0:CONV_WIDTH], small_f[CONV_WIDTH : CONV_WIDTH + 1]
    gb = jnp.concatenate([small_f[5:6], small_f[6:7]], axis=1)
    lam = small_f[7:8]
    z_b, hseq, h1, xb_tails, h_tails, dx2, loss_l, g_nf = _layer_b_fwd(
        x1, nw1, bin_w, cw, cb, gab, gb, lam, bout, nf, loss_target[0], TM_FWD
    )
    dx1, dz_b, y_b, dob_b, g_gab, g_gb, g_cw, g_cb, g_lam, g_nw1 = _layer_b_bwd(
        dx2, x1, z_b, hseq, xb_tails, h_tails, nw1, bin_w, cw, cb, gab, gabt, gb, lam, bout, TM_FWD
    )
    seg = lambda g: g.reshape(N_DEV, -1, g.shape[3])
    own_half = lambda bufs, got, wires: [_add_own_half(b, g, core, w) for b, g, w in zip(bufs, got, wires)]
    received = lambda bufs, got_a, got_b: [
        _add_received(b, ga, gb_, mine, core, core, 2) for b, ga, gb_ in zip(bufs, got_a, got_b)
    ]

    g_bout = seg(_wgrad(y_b, dob_b, N_CHIPS, 1))
    g_bin, got_a1 = _wgrad(h1, dz_b, 2, N_CHIPS, _swap_hook([g_bout]))
    g_bin = seg(g_bin)
    parts1 = own_half([g_bout], got_a1, [BF16])
    a_args = (z_a, a_ln_w, a_ln_b, wc, wct, bs_t, wout)
    half = t_rows // TM_A_BWD // 2
    first, rode = _layer_a_bwd(dx1, *a_args, (0, half), None, _join_hooks(_swap_hook([g_bin]), _send_hook(parts1)))
    got_a2, got_b1 = rode[:1], rode[1:]
    parts2 = own_half([g_bin], got_a2, [BF16])
    red1 = received([g_bout], got_a1, got_b1)
    second, rode = _layer_a_bwd(
        dx1, *a_args, (half, 2 * half), first[:3], _join_hooks(_send_hook(parts2), _share_hook(red1))
    )
    got_b2, gr_bout = rode[:1], rode[1].reshape(b_w_out.shape[1:])
    dz_a, y_a, dob_a = second[:3]
    g_ws, g_bst, g_lnw, g_lnb = (p + q for p, q in zip(first[3:], second[3:]))
    red2 = received([g_bin], got_a2, got_b2)
    g_win, (gr_bin,) = _wgrad(h0, dz_a, 2, N_CHIPS, _share_hook(red2))
    g_win = seg(g_win)
    gr_bin = gr_bin.reshape(b_w_in.shape[1:])
    g_wout, got_a1 = _wgrad(y_a, dob_a, N_DEV, 1, _swap_hook([g_win]))
    g_wout = seg(g_wout)

    small_shapes = [
        (1, d), (1, aw), (1, aw), (A_GROUPS, CHUNK, CHUNK), (A_GROUPS, CHUNK), (B_HEADS, hd, hd), (B_HEADS, hd, hd),
        (d,), (CONV_WIDTH, bw), (1, bw), (1, bw), (1, bw), (1, bw), (1, 1),
    ]
    small = _pack_rows(
        [
            g_nw1, g_lnw, g_lnb, g_ws, g_bst.T, g_gab[:, :, :hd], g_gab[:, :, hd:],
            g_nf, g_cw[:CONV_WIDTH], g_cb, g_gb[:, :bw], g_gb[:, bw:], g_lam, loss_l[:, :1],
        ]
    )

    parts1 = own_half([g_win], got_a1, [BF16])
    (grad_x, g_nw0_mine), rode = _layer_a_bwd_dx(
        dx1, x0, dz_a, nw0, win, TM_A_DX, _join_hooks(_send_hook(parts1), _swap_hook([g_wout, small]))
    )
    got_b1, got_a2 = rode[:1], rode[1:]
    parts2 = own_half([g_wout, small], got_a2, [BF16, F32])
    red_win = received([g_win], got_a1, got_b1)
    b_names = ("b_w_in", "b_w_out")
    b_grads = {"b_w_in": gr_bin, "b_w_out": gr_bout}
    two_d = lambda a: a.reshape(a.shape[-2:])
    b_out, rode = _adamw_many(
        [two_d(weights[n]) for n in b_names], [b_grads[n] for n in b_names], [two_d(m_in[n]) for n in b_names],
        [two_d(v_in[n]) for n in b_names], "adamw_layer_b",
        _join_hooks(_send_hook(parts2), _share_hook(red_win, None, g_nw0_mine)),
    )
    got_b2, (gr_win, g_nw0_all) = rode[:2], rode[2:]
    gr_win = gr_win.reshape(a_w_in.shape[1:])
    red_wout = received([g_wout], got_a2[:1], got_b2[:1])
    red_small = _add_received(small, got_a2[1], got_b2[1], mine, core, 2 * mine + core, N_DEV)
    win_out, (gr_wout, small_r) = _adamw(
        two_d(a_w_in), gr_win, two_d(m_a_w_in), two_d(v_a_w_in), _share_hook(red_wout, red_small)
    )
    g_nw0 = _sum_partials(g_nw0_all)
    gr_wout = gr_wout.reshape(a_w_out.shape[1:])
    (g_nw1_r, g_a_ln_w, g_a_ln_b, g_a_w_s, g_a_b_s, g_gate_a_w, g_gate_x_w, g_norm_f, gf_cw, gf_cb, gf_gab, gf_gxb,
     gf_lam, loss) = _unpack(small_r.reshape(-1), small_shapes)
    g_norm_w = jnp.concatenate([g_nw0, g_nw1_r], axis=0)
    shard = lambda g: lax.dynamic_slice_in_dim(g, mine * (bw // N_CHIPS), bw // N_CHIPS, axis=1)

    grads = {
        "norm_w": g_norm_w, "a_w_in": gr_win[None], "a_ln_w": g_a_ln_w, "a_ln_b": g_a_ln_b, "a_w_s": g_a_w_s[None],
        "a_b_s": g_a_b_s[None], "a_w_out": gr_wout[None], "b_w_in": gr_bin[None], "b_conv_w": shard(gf_cw)[None],
        "b_conv_b": shard(gf_cb), "b_gate_a_w": g_gate_a_w[None], "b_gate_a_b": shard(gf_gab),
        "b_gate_x_w": g_gate_x_w[None], "b_gate_x_b": shard(gf_gxb), "b_lambda": shard(gf_lam),
        "b_w_out": gr_bout[None], "norm_f_w": g_norm_f,
    }
    names = list(weights)
    delta, new_m, new_v = {}, {}, {}
    delta["a_w_in"], new_m["a_w_in"], new_v["a_w_in"] = win_out
    delta["a_w_out"], new_m["a_w_out"], new_v["a_w_out"] = _adamw(
        two_d(a_w_out), gr_wout, two_d(m_a_w_out), two_d(v_a_w_out)
    )
    small_names = [n for n in names if n not in ("a_w_in", "a_w_out") + b_names]
    at_least_2d = lambda a: a.reshape(1, -1) if a.ndim == 1 else a
    small_out, _ = _adamw_many(
        *[[at_least_2d(src[n]) for n in small_names] for src in (weights, grads, m_in, v_in)], "adamw_small"
    )
    for dst, vals, b_vals in zip((delta, new_m, new_v), small_out, b_out):
        dst.update(zip(small_names, vals))
        dst.update(zip(b_names, b_vals))
    for dst in (delta, new_m, new_v):
        for n in names:
            dst[n] = dst[n].reshape(weights[n].shape)

    return (
        loss.reshape(()),
        grad_x[None],
        *[grads[n] for n in names],
        *[delta[n] for n in names],
        *[new_m[n] for n in names],
        *[new_v[n] for n in names],
    )
```

```python
import jax
import jax.numpy as jnp
from jax import lax
from jax.experimental import pallas as pl
from jax.experimental.pallas import tpu as pltpu

F32 = jnp.float32
BF16 = jnp.bfloat16

RMS_EPS = 1e-6
LN_EPS = 1e-5
RG_C = 8.0
CHUNK = 128
A_GROUPS = 8
B_HEADS = 12
CONV_WIDTH = 4

ADAM_LR = 0.001
ADAM_B1 = 0.9
ADAM_B2 = 0.999
ADAM_EPS = 1e-08
ADAM_WD = 0.01
ADAM_STEP = 10

N_CHIPS = 4
N_DEV = 8
SUBLANES = 8
LANES = 128
V7X_VMEM_BYTES = 64 * 1024 * 1024
VMEM_LIMIT = V7X_VMEM_BYTES * 7 // 8
MESH = pl.DeviceIdType.MESH
ANY = pl.BlockSpec(memory_space=pl.ANY)

TM_FWD = 256
TM_A_BWD = 256
TM_A_DX = 512

GELU_C0 = 0.7978845608028654
GELU_C1 = 0.044715


class _Hook:
    def __init__(self, operands, out_shapes, aliases, n_sems, start, finish, middle=None):
        self.operands, self.out_shapes, self.aliases, self.n_sems = operands, out_shapes, aliases, n_sems
        self.start, self.finish, self.middle = start, finish, middle


class _SemView:
    def __init__(self, base, off):
        self.base, self.off = base, off

    @property
    def at(self):
        return self

    def __getitem__(self, k):
        return self.base.at[self.off + k]


def _join_hooks(*hooks):
    if len(hooks) == 1:
        return hooks[0]
    operands, out_shapes, aliases, spans = [], [], {}, []
    n_sems = 0
    for h in hooks:
        aliases.update({len(operands) + i: len(out_shapes) + o for i, o in h.aliases.items()})
        spans.append((len(operands), len(h.operands), len(out_shapes), len(h.out_shapes), n_sems))
        operands += list(h.operands)
        out_shapes += list(h.out_shapes)
        n_sems += h.n_sems

    def each(which):
        def run(ins, outs, send, recv):
            for h, (i0, ni, o0, no, s0) in zip(hooks, spans):
                step = getattr(h, which)
                if step is not None:
                    step(ins[i0 : i0 + ni], outs[o0 : o0 + no], _SemView(send, s0), _SemView(recv, s0))

        return run

    middle = each("middle") if any(h.middle is not None for h in hooks) else None
    return _Hook(operands, out_shapes, aliases, n_sems, each("start"), each("finish"), middle)


def _pcall(body, hook=None, **kw):
    if hook is None:
        return pl.pallas_call(body, **kw)
    n_in, n_out = len(kw["in_specs"]), len(kw["out_shape"])
    hi, ho = len(hook.operands), len(hook.out_shapes)
    grid = kw.get("grid", ())

    def wrapped(*refs):
        ins, h_in = refs[:n_in], refs[n_in : n_in + hi]
        outs = refs[n_in + hi : n_in + hi + n_out]
        h_out = refs[n_in + hi + n_out : n_in + hi + n_out + ho]
        scratch = refs[n_in + hi + n_out + ho : -2]
        send_sems, recv_sems = refs[-2:]
        if not grid:
            hook.start(h_in, h_out, send_sems, recv_sems)
            if hook.middle is not None:
                hook.middle(h_in, h_out, send_sems, recv_sems)
            body(*ins, *outs, *scratch)
            hook.finish(h_in, h_out, send_sems, recv_sems)
            return
        first = pl.program_id(0) == 0
        last = pl.program_id(0) == grid[0] - 1
        for axis in range(1, len(grid)):
            first = jnp.logical_and(first, pl.program_id(axis) == 0)
            last = jnp.logical_and(last, pl.program_id(axis) == grid[axis] - 1)

        @pl.when(first)
        def _():
            hook.start(h_in, h_out, send_sems, recv_sems)

        if hook.middle is not None:
            assert len(grid) == 1 and grid[0] >= 4
            half_way = pl.program_id(0) == grid[0] * 3 // 8

            @pl.when(half_way)
            def _():
                hook.middle(h_in, h_out, send_sems, recv_sems)

        body(*ins, *outs, *scratch)

        @pl.when(last)
        def _():
            hook.finish(h_in, h_out, send_sems, recv_sems)

    aliases = dict(kw.pop("input_output_aliases", {}))
    aliases.update({n_in + i: n_out + o for i, o in hook.aliases.items()})
    kw.update(
        in_specs=list(kw["in_specs"]) + [ANY] * hi,
        out_specs=list(kw["out_specs"]) + [ANY] * ho,
        out_shape=list(kw["out_shape"]) + list(hook.out_shapes),
        scratch_shapes=list(kw.get("scratch_shapes", ()))
        + [pltpu.SemaphoreType.DMA((hook.n_sems,)), pltpu.SemaphoreType.DMA((hook.n_sems,))],
        input_output_aliases=aliases,
    )
    call = pl.pallas_call(wrapped, **kw)

    def run(*operands):
        outs = call(*operands, *hook.operands)
        return outs[:n_out], outs[n_out:]

    return run


def _run_hook(hook, name):
    def body():
        pass

    return _pcall(body, hook, name=name, in_specs=[], out_specs=[], out_shape=[])()[1]


def _cparams(sem=None):
    return pltpu.CompilerParams(dimension_semantics=sem, vmem_limit_bytes=VMEM_LIMIT)


def _full(shape):
    zeros = (0,) * len(shape)
    return pl.BlockSpec(shape, lambda *_: zeros)


def _scalars(*vals):
    return jnp.stack([jnp.asarray(v, jnp.int32) for v in vals])


def _sigmoid(x):
    return 1.0 / (1.0 + jnp.exp(-x))


def _gelu(x):
    t = jnp.tanh(GELU_C0 * (x + GELU_C1 * (x * x * x)))
    return x * (0.5 * (1.0 + t))


def _gelu_and_grad(x):
    x2 = x * x
    t = jnp.tanh(GELU_C0 * (x + GELU_C1 * (x2 * x)))
    cdf = 0.5 * (1.0 + t)
    return x * cdf, cdf + 0.5 * x * (1.0 - t * t) * (GELU_C0 * (1.0 + 3.0 * GELU_C1 * x2))


def _silu_and_grad(x):
    s = _sigmoid(x)
    return x * s, s * (1.0 + x * (1.0 - s))


def _softplus_neg(lam):
    u = jnp.exp(-jnp.abs(lam))
    w = 1.0 + u
    log1p = jnp.where(w == 1.0, u, jnp.log(w) * (u / jnp.where(w == 1.0, 1.0, w - 1.0)))
    return jnp.maximum(-lam, 0.0) + log1p


def _dot(a, b):
    return jnp.dot(a, b, preferred_element_type=F32)


def _dot_nt(a, b):
    return lax.dot_general(a, b, (((1,), (1,)), ((), ())), preferred_element_type=F32)


def _dot_tn(a, b):
    return lax.dot_general(a, b, (((0,), (0,)), ((), ())), preferred_element_type=F32)


def _shift_down(v, halo, k):
    if k == 0:
        return v
    rolled = pltpu.roll(v, k, 0)
    row = lax.broadcasted_iota(jnp.int32, (SUBLANES, v.shape[1]), 0)
    top = jnp.where(row < k, pltpu.roll(halo, k, 0), rolled[:SUBLANES])
    return jnp.concatenate([top, rolled[SUBLANES:]], axis=0)


def _shift_up(v, head, k):
    if k == 0:
        return v
    n = v.shape[0]
    rolled = pltpu.roll(v, n - k, 0)
    row = lax.broadcasted_iota(jnp.int32, (SUBLANES, v.shape[1]), 0)
    bot = jnp.where(row >= SUBLANES - k, pltpu.roll(head, SUBLANES - k, 0), rolled[n - SUBLANES :])
    return jnp.concatenate([rolled[: n - SUBLANES], bot], axis=0)


def _scan_blocks(a_ref, b_ref, out_ref, carry, n_rows, reverse):
    width = a_ref.shape[1]
    row = lax.broadcasted_iota(jnp.int32, (SUBLANES, width), 0)
    n_blocks = n_rows // SUBLANES

    def block(j, carry):
        i = (n_blocks - 1 - j) if reverse else j
        r0 = pl.multiple_of(i * SUBLANES, SUBLANES)
        a = a_ref[pl.ds(r0, SUBLANES), :]
        b = b_ref[pl.ds(r0, SUBLANES), :]
        for d in (1, 2, 4):
            shift = (SUBLANES - d) if reverse else d
            keep = (row < SUBLANES - d) if reverse else (row >= d)
            a_s = pltpu.roll(a, shift, 0)
            b_s = pltpu.roll(b, shift, 0)
            b = jnp.where(keep, a * b_s + b, b)
            a = jnp.where(keep, a * a_s, a)
        h = a * carry + b
        out_ref[pl.ds(r0, SUBLANES), :] = h
        edge = h[0:1, :] if reverse else h[SUBLANES - 1 : SUBLANES, :]
        return jnp.broadcast_to(edge, (SUBLANES, width))

    return lax.fori_loop(0, n_blocks, block, carry)


def _rms_fwd(x, w):
    r = lax.rsqrt(jnp.mean(x * x, axis=-1, keepdims=True) + RMS_EPS)
    xh = x * r
    return xh * w, xh, r


def _rms_bwd(dh, xh, r, w):
    dxh = dh * w
    dx = r * (dxh - xh * jnp.mean(dxh * xh, axis=-1, keepdims=True))
    return dx, jnp.sum(dh * xh, axis=0, keepdims=True)


def _cast_to_segments(w, mine, rows):
    n, c = w.shape
    per = n // 2 // rows

    def body(k_ref, w_ref, o_ref):
        o_ref[...] = w_ref[...].astype(BF16)

    return _pcall(
        body,
        name=f"cast_{n}x{c}",
        grid_spec=pltpu.PrefetchScalarGridSpec(
            num_scalar_prefetch=1,
            grid=(n // rows,),
            in_specs=[pl.BlockSpec((rows, c), lambda i, k_ref: (i, 0))],
            out_specs=pl.BlockSpec((None, rows, c), lambda i, k_ref: (2 * k_ref[0] + i // per, i % per, 0)),
        ),
        out_shape=jax.ShapeDtypeStruct((N_DEV, n // 2, c), BF16),
        compiler_params=_cparams(("arbitrary",)),
    )(_scalars(mine), w)


def _place():
    x, y, c = lax.axis_index("x"), lax.axis_index("y"), lax.axis_index("c")
    chips = [(1 - x, y), (x, 1 - y), (1 - x, 1 - y)]
    return x, y, c, chips


def _chip_no(chip):
    return 2 * chip[0] + chip[1]


def _rcopy(src, dst, send_sem, recv_sem, to):
    return pltpu.make_async_remote_copy(
        src_ref=src, dst_ref=dst, send_sem=send_sem, recv_sem=recv_sem, device_id=to, device_id_type=MESH
    )


def _gather_hook(big, small=None):
    nb = len(big)
    n_sems = 6 * nb + 4

    def places():
        x, y, c, chips = _place()
        first = (x ^ (1 - c), y ^ c)
        second = (x ^ c, y ^ (1 - c))
        return x, y, c, chips, first, second, (1 - x, 1 - y)

    def seg(outs, b, chip, half):
        return outs[b].at[2 * _chip_no(chip) + half]

    def step1(outs, send, recv):
        x, y, c, _, first, _, _ = places()
        return [
            _rcopy(seg(outs, b, (x, y), c), seg(outs, b, (x, y), c), send.at[6 * b], recv.at[6 * b], (*first, c))
            for b in range(nb)
        ]

    def step2(outs, send, recv):
        x, y, c, _, first, second, _ = places()
        copies = []
        for b in range(nb):
            for k, chip in ((1, (x, y)), (2, first)):
                src = seg(outs, b, chip, c)
                copies.append(_rcopy(src, src, send.at[6 * b + k], recv.at[6 * b + k], (*second, c)))
        return copies

    def hand_over(outs, send, recv, k, chip):
        x, y, c, *_ = places()
        return [
            _rcopy(seg(outs, b, chip, c), seg(outs, b, chip, c), send.at[6 * b + k], recv.at[6 * b + k], (x, y, 1 - c))
            for b in range(nb)
        ]

    def wait_landed(outs, send, recv, k, chip, half):
        x, y, c, *_ = places()
        for b in range(nb):
            got = seg(outs, b, chip, half)
            _rcopy(got, got, send.at[6 * b + k], recv.at[6 * b + k], (x, y, c)).wait_recv()

    def small_copies(ins, outs, send, recv):
        x, y, c, chips, *_ = places()
        there = outs[nb].at[_chip_no((x, y))]
        return [
            _rcopy(ins[nb], there, send.at[6 * nb + j], recv.at[6 * nb + j], (*chip, c)) for j, chip in enumerate(chips)
        ]

    def local_copy(ins, outs, send):
        x, y, _, _ = _place()
        return pltpu.make_async_copy(ins[nb], outs[nb].at[_chip_no((x, y))], send.at[6 * nb + 3])

    def start(ins, outs, send, recv):
        for cp in step1(outs, send, recv):
            cp.start()
        if small is not None:
            for cp in small_copies(ins, outs, send, recv):
                cp.start()
            local_copy(ins, outs, send).start()

    def middle(ins, outs, send, recv):
        *_, first, _, _ = places()
        wait_landed(outs, send, recv, 0, first, places()[2])
        for cp in step2(outs, send, recv) + hand_over(outs, send, recv, 3, first):
            cp.start()

    def finish(ins, outs, send, recv):
        x, y, c, chips, first, second, diagonal = places()
        wait_landed(outs, send, recv, 1, second, c)
        wait_landed(outs, send, recv, 2, diagonal, c)
        late = hand_over(outs, send, recv, 4, second) + hand_over(outs, send, recv, 5, diagonal)
        for cp in late:
            cp.start()
        wait_landed(outs, send, recv, 3, second, 1 - c)
        wait_landed(outs, send, recv, 4, first, 1 - c)
        wait_landed(outs, send, recv, 5, diagonal, 1 - c)
        sent = step1(outs, send, recv) + step2(outs, send, recv) + hand_over(outs, send, recv, 3, first) + late
        for cp in sent:
            cp.wait_send()
        if small is not None:
            for j, chip in enumerate(chips):
                got = outs[nb].at[_chip_no(chip)]
                _rcopy(got, got, send.at[6 * nb + j], recv.at[6 * nb + j], (x, y, c)).wait_recv()
            for cp in small_copies(ins, outs, send, recv):
                cp.wait_send()
            local_copy(ins, outs, send).wait()

    operands = list(big) + ([small] if small is not None else [])
    out_shapes = [jax.ShapeDtypeStruct(b.shape, b.dtype) for b in big]
    if small is not None:
        out_shapes.append(jax.ShapeDtypeStruct((N_CHIPS, *small.shape), small.dtype))
    return _Hook(operands, out_shapes, {b: b for b in range(nb)}, n_sems, start, finish, middle)


def _both_ways_hook(operands, out_shapes, copies_of, n_sems):
    def start(ins, outs, send, recv):
        for cp in copies_of(ins, outs, send, recv):
            cp.start()

    def finish(ins, outs, send, recv):
        for cp in copies_of(ins, outs, send, recv):
            cp.wait()

    return _Hook(operands, out_shapes, {}, n_sems, start, finish)


def _swap_hook(bufs):
    def copies_of(ins, outs, send, recv):
        x, y, c, _ = _place()
        copies = []
        for b in range(len(bufs)):
            for j in range(N_CHIPS):
                k = b * N_CHIPS + j
                copies.append(_rcopy(ins[b].at[2 * j + 1 - c], outs[b].at[j], send.at[k], recv.at[k], (x, y, 1 - c)))
        return copies

    out_shapes = [jax.ShapeDtypeStruct((N_CHIPS, *b.shape[1:]), b.dtype) for b in bufs]
    return _both_ways_hook(list(bufs), out_shapes, copies_of, len(bufs) * N_CHIPS)


def _send_hook(parts):
    def copies_of(ins, outs, send, recv):
        _, _, c, chips = _place()
        copies = []
        for b in range(len(parts)):
            for j, chip in enumerate(chips):
                k = b * 3 + j
                copies.append(_rcopy(ins[b].at[_chip_no(chip)], outs[b].at[j], send.at[k], recv.at[k], (*chip, c)))
        return copies

    out_shapes = [jax.ShapeDtypeStruct((3, *p.shape[1:]), p.dtype) for p in parts]
    return _both_ways_hook(list(parts), out_shapes, copies_of, len(parts) * 3)


def _share_hook(big, small=None, tiny=None):
    nb = len(big)
    n_sems = nb + 7 + N_DEV
    t0 = nb + 7

    def tiny_copies(ins, outs, send, recv):
        x, y, c, _ = _place()
        there = outs[-1].at[2 * _chip_no((x, y)) + c]
        copies = []
        for r in range(1, N_DEV):
            to = (x ^ (r >> 2 & 1), y ^ (r >> 1 & 1), c ^ (r & 1))
            copies.append(_rcopy(ins[-1], there, send.at[t0 + r], recv.at[t0 + r], to))
        return copies

    def tiny_local(ins, outs, send):
        x, y, c, _ = _place()
        return pltpu.make_async_copy(ins[-1], outs[-1].at[2 * _chip_no((x, y)) + c], send.at[t0])

    def first_copies(outs, send, recv):
        x, y, c, chips = _place()
        sibling = (x, y, 1 - c)
        copies = [_rcopy(outs[b].at[c], outs[b].at[c], send.at[b], recv.at[b], sibling) for b in range(nb)]
        if small is not None:
            own = outs[nb].at[2 * _chip_no((x, y)) + c]
            copies.append(_rcopy(own, own, send.at[nb], recv.at[nb], sibling))
            for j, chip in enumerate(chips):
                copies.append(_rcopy(own, own, send.at[nb + 1 + j], recv.at[nb + 1 + j], (*chip, c)))
        return copies

    def start(ins, outs, send, recv):
        for cp in first_copies(outs, send, recv):
            cp.start()
        if tiny is not None:
            for cp in tiny_copies(ins, outs, send, recv):
                cp.start()
            tiny_local(ins, outs, send).start()

    def finish(ins, outs, send, recv):
        x, y, c, chips = _place()
        me, sibling = (x, y, c), (x, y, 1 - c)
        if tiny is not None:
            for cp in tiny_copies(ins, outs, send, recv):
                cp.wait()
            tiny_local(ins, outs, send).wait()
        passed = []
        if small is not None:
            for j, chip in enumerate(chips):
                got = outs[nb].at[2 * _chip_no(chip) + c]
                _rcopy(got, got, send.at[nb + 1 + j], recv.at[nb + 1 + j], me).wait_recv()
                fwd = _rcopy(got, got, send.at[nb + 4 + j], recv.at[nb + 4 + j], sibling)
                fwd.start()
                passed.append(fwd)
        for b in range(nb):
            got = outs[b].at[1 - c]
            _rcopy(got, got, send.at[b], recv.at[b], me).wait_recv()
        if small is not None:
            got = outs[nb].at[2 * _chip_no((x, y)) + 1 - c]
            _rcopy(got, got, send.at[nb], recv.at[nb], me).wait_recv()
            for j, chip in enumerate(chips):
                got = outs[nb].at[2 * _chip_no(chip) + 1 - c]
                _rcopy(got, got, send.at[nb + 4 + j], recv.at[nb + 4 + j], me).wait_recv()
        for cp in first_copies(outs, send, recv) + passed:
            cp.wait_send()

    operands = list(big) + ([small] if small is not None else [])
    out_shapes = [jax.ShapeDtypeStruct(a.shape, a.dtype) for a in operands]
    aliases = {i: i for i in range(len(operands))}
    if tiny is not None:
        operands.append(tiny)
        out_shapes.append(jax.ShapeDtypeStruct((N_DEV, *tiny.shape), tiny.dtype))
    return _Hook(operands, out_shapes, aliases, n_sems, start, finish)


def _row_tile(rows, cols, target_bytes=2 * 1024 * 1024):
    best = SUBLANES
    for t in range(SUBLANES, rows + 1, SUBLANES):
        if rows % t == 0 and t * cols * 4 <= target_bytes:
            best = t
    return best


def _add_own_half(buf, got, c, wire):
    _, rows, cols = buf.shape
    tr = _row_tile(rows, cols)

    def body(c_ref, a_ref, b_ref, o_ref):
        o_ref[...] = (a_ref[...] + b_ref[...]).astype(wire)

    return _pcall(
        body,
        name=f"add_own_half_{rows}x{cols}",
        grid_spec=pltpu.PrefetchScalarGridSpec(
            num_scalar_prefetch=1,
            grid=(N_CHIPS, rows // tr),
            in_specs=[
                pl.BlockSpec((None, None, tr, cols), lambda j, r, c_ref: (j, c_ref[0], r, 0)),
                pl.BlockSpec((None, tr, cols), lambda j, r, c_ref: (j, r, 0)),
            ],
            out_specs=pl.BlockSpec((None, tr, cols), lambda j, r, c_ref: (j, r, 0)),
        ),
        out_shape=jax.ShapeDtypeStruct((N_CHIPS, rows, cols), wire),
        compiler_params=_cparams(("arbitrary", "arbitrary")),
    )(_scalars(c), buf.reshape(N_CHIPS, 2, rows, cols), got)


def _add_received(buf, got_a, got_b, mine, c, slot, n_slots):
    _, rows, cols = buf.shape
    tr = _row_tile(rows, cols)

    def body(s_ref, x_ref, a_ref, g_ref, o_ref):
        own = x_ref[...] + a_ref[...]
        o_ref[...] = ((own + g_ref[0].astype(F32)) + g_ref[1].astype(F32)) + g_ref[2].astype(F32)

    return _pcall(
        body,
        name=f"add_received_{rows}x{cols}",
        grid_spec=pltpu.PrefetchScalarGridSpec(
            num_scalar_prefetch=1,
            grid=(rows // tr,),
            in_specs=[
                pl.BlockSpec((None, None, tr, cols), lambda r, s_ref: (s_ref[0], s_ref[1], r, 0)),
                pl.BlockSpec((None, tr, cols), lambda r, s_ref: (s_ref[0], r, 0)),
                pl.BlockSpec((3, tr, cols), lambda r, s_ref: (0, r, 0)),
            ],
            out_specs=pl.BlockSpec((None, tr, cols), lambda r, s_ref: (s_ref[2], r, 0)),
        ),
        out_shape=jax.ShapeDtypeStruct((n_slots, rows, cols), F32),
        compiler_params=_cparams(("arbitrary",)),
    )(_scalars(mine, c, slot), buf.reshape(N_CHIPS, 2, rows, cols), got_a, got_b)


def _layer_a_fwd(x, nw, win, ln_w, ln_b, wc, bs_t, wout, tm, hook):
    t_rows, d = x.shape
    n_sh, _, s_cols = win.shape
    aw = wout.shape[0]
    gd = aw // A_GROUPS
    tn = 512
    assert s_cols % tn == 0 and aw % tn == 0 and tm % CHUNK == 0

    def body(x_ref, nw_ref, win_ref, lnw_ref, lnb_ref, wc_ref, bst_ref, wout_ref, z_ref, x1_ref, h_ref, u_s, v_s, y_s):
        x = x_ref[...]
        h, _, _ = _rms_fwd(x, nw_ref[...])
        h = h.astype(BF16)
        h_ref[...] = h
        for j in range(3 * aw // tn):
            k, off = divmod(j * tn, s_cols)
            cols = slice((j * tn) % aw, (j * tn) % aw + tn)
            zj = _dot(h, win_ref[k, :, off : off + tn])
            z_ref[:, j * tn : (j + 1) * tn] = zj
            if j * tn < aw:
                u_s[:, cols] = _gelu(zj)
            elif j * tn < 2 * aw:
                v_s[:, cols] = _gelu(zj)
            else:
                u_s[:, cols] = u_s[:, cols] * (zj * _sigmoid(zj))
        v = v_s[...]
        mu = jnp.mean(v, axis=-1, keepdims=True)
        vc = v - mu
        rstd = lax.rsqrt(jnp.mean(vc * vc, axis=-1, keepdims=True) + LN_EPS)
        v_s[...] = (vc * rstd) * lnw_ref[...] + lnb_ref[...]
        for ck in range(tm // CHUNK):
            rows = slice(ck * CHUNK, (ck + 1) * CHUNK)
            for g in range(A_GROUPS):
                cols = slice(g * gd, (g + 1) * gd)
                s = _dot(wc_ref[g], v_s[rows, cols].astype(BF16)) + bst_ref[:, g : g + 1]
                y_s[rows, cols] = (u_s[rows, cols] * s).astype(BF16)
        x1_ref[...] = x + _dot(y_s[...], wout_ref[...])

    row = lambda i: (i, 0)
    return _pcall(
        body,
        hook,
        name="layer_a_fwd",
        grid=(t_rows // tm,),
        in_specs=[
            pl.BlockSpec((tm, d), row),
            _full(nw.shape),
            _full(win.shape),
            _full(ln_w.shape),
            _full(ln_b.shape),
            _full(wc.shape),
            _full(bs_t.shape),
            _full(wout.shape),
        ],
        out_specs=[pl.BlockSpec((tm, 3 * aw), row), pl.BlockSpec((tm, d), row), pl.BlockSpec((tm, d), row)],
        out_shape=[
            jax.ShapeDtypeStruct((t_rows, 3 * aw), F32),
            jax.ShapeDtypeStruct((t_rows, d), F32),
            jax.ShapeDtypeStruct((t_rows, d), BF16),
        ],
        scratch_shapes=[pltpu.VMEM((tm, aw), F32), pltpu.VMEM((tm, aw), F32), pltpu.VMEM((tm, aw), BF16)],
        compiler_params=_cparams(("arbitrary",)),
    )(x, nw, win, ln_w, ln_b, wc, bs_t, wout)


def _layer_a_bwd(dout, z, ln_w, ln_b, wc, wct, bs_t, wout, tiles, earlier, hook):
    t_rows, d = dout.shape
    aw = wout.shape[0]
    gd = aw // A_GROUPS
    tm = TM_A_BWD
    lo, hi = tiles
    n_earlier = 0 if earlier is None else len(earlier)

    def body(dout_ref, z_ref, lnw_ref, lnb_ref, wc_ref, wct_ref, bst_ref, wout_ref, *rest):
        dz_ref, y_ref, dob_ref, gws_ref, gbs_ref, glnw_ref, glnb_ref, u_s, vh_s, ds_s, dvn_s = rest[n_earlier:]

        @pl.when(pl.program_id(0) == 0)
        def _():
            gws_ref[...] = jnp.zeros_like(gws_ref)
            gbs_ref[...] = jnp.zeros_like(gbs_ref)
            glnw_ref[...] = jnp.zeros_like(glnw_ref)
            glnb_ref[...] = jnp.zeros_like(glnb_ref)

        dob = dout_ref[...].astype(BF16)
        dob_ref[...] = dob
        dy = _dot_nt(dob, wout_ref[...])

        zv = z_ref[:, aw : 2 * aw]
        vg, dvg_dz = _gelu_and_grad(zv)
        mu = jnp.mean(vg, axis=-1, keepdims=True)
        vc = vg - mu
        rstd = lax.rsqrt(jnp.mean(vc * vc, axis=-1, keepdims=True) + LN_EPS)
        vh = vc * rstd
        vh_s[...] = vh
        vn = (vh * lnw_ref[...] + lnb_ref[...]).astype(BF16)

        zu = z_ref[:, 0:aw]
        zg = z_ref[:, 2 * aw : 3 * aw]
        u, du_dz = _gelu_and_grad(zu)
        sg, dsg = _silu_and_grad(zg)
        u_s[...] = u * sg
        tril = lax.broadcasted_iota(jnp.int32, (CHUNK, CHUNK), 0) >= lax.broadcasted_iota(jnp.int32, (CHUNK, CHUNK), 1)
        for ck in range(tm // CHUNK):
            rows = slice(ck * CHUNK, (ck + 1) * CHUNK)
            for g in range(A_GROUPS):
                cols = slice(g * gd, (g + 1) * gd)
                vn_g = vn[rows, cols]
                s = _dot(wc_ref[g], vn_g) + bst_ref[:, g : g + 1]
                usg = u_s[rows, cols]
                dy_g = dy[rows, cols]
                y_ref[rows, cols] = (usg * s).astype(BF16)
                ds = dy_g * usg
                ds_s[rows, cols] = dy_g * s
                gbs_ref[:, g : g + 1] += jnp.sum(ds, axis=-1, keepdims=True)
                dsb = ds.astype(BF16)
                gws_ref[g] += jnp.where(tril, _dot_nt(dsb, vn_g), 0.0)
                dvn_s[rows, cols] = _dot(wct_ref[g], dsb)
        dusg = ds_s[...]
        dz_ref[:, 0:aw] = (dusg * sg * du_dz).astype(BF16)
        dz_ref[:, 2 * aw : 3 * aw] = (dusg * u * dsg).astype(BF16)

        dvn = dvn_s[...]
        vh = vh_s[...]
        glnw_ref[...] += jnp.sum(dvn * vh, axis=0, keepdims=True)
        glnb_ref[...] += jnp.sum(dvn, axis=0, keepdims=True)
        dvh = dvn * lnw_ref[...]
        dvg = rstd * (dvh - jnp.mean(dvh, axis=-1, keepdims=True) - vh * jnp.mean(dvh * vh, axis=-1, keepdims=True))
        dz_ref[:, aw : 2 * aw] = (dvg * dvg_dz).astype(BF16)

    row = lambda i: (i + lo, 0)
    call = _pcall(
        body,
        hook,
        name=f"layer_a_bwd_{lo}",
        grid=(hi - lo,),
        in_specs=[
            pl.BlockSpec((tm, d), row),
            pl.BlockSpec((tm, 3 * aw), row),
            _full(ln_w.shape),
            _full(ln_b.shape),
            _full(wc.shape),
            _full(wct.shape),
            _full(bs_t.shape),
            _full(wout.shape),
        ]
        + [ANY] * n_earlier,
        out_specs=[
            pl.BlockSpec((tm, 3 * aw), row),
            pl.BlockSpec((tm, aw), row),
            pl.BlockSpec((tm, d), row),
            _full((A_GROUPS, CHUNK, CHUNK)),
            _full((CHUNK, A_GROUPS)),
            _full((1, aw)),
            _full((1, aw)),
        ],
        out_shape=[
            jax.ShapeDtypeStruct((t_rows, 3 * aw), BF16),
            jax.ShapeDtypeStruct((t_rows, aw), BF16),
            jax.ShapeDtypeStruct((t_rows, d), BF16),
            jax.ShapeDtypeStruct((A_GROUPS, CHUNK, CHUNK), F32),
            jax.ShapeDtypeStruct((CHUNK, A_GROUPS), F32),
            jax.ShapeDtypeStruct((1, aw), F32),
            jax.ShapeDtypeStruct((1, aw), F32),
        ],
        scratch_shapes=[pltpu.VMEM((tm, aw), F32)] * 4,
        input_output_aliases={8 + i: i for i in range(n_earlier)},
        compiler_params=_cparams(("arbitrary",)),
    )
    return call(dout, z, ln_w, ln_b, wc, wct, bs_t, wout, *(earlier or ()))


def _layer_a_bwd_dx(dout, x, dz, nw, win, tm, hook):
    t_rows, d = x.shape
    n_sh, _, s_cols = win.shape

    def body(dout_ref, x_ref, dz_ref, nw_ref, win_ref, gx_ref, gnw_ref):
        @pl.when(pl.program_id(0) == 0)
        def _():
            gnw_ref[...] = jnp.zeros_like(gnw_ref)

        dh = jnp.zeros((tm, d), F32)
        for k in range(n_sh):
            dh = dh + _dot_nt(dz_ref[:, k * s_cols : (k + 1) * s_cols], win_ref[k])
        nw = nw_ref[...]
        _, xh, r = _rms_fwd(x_ref[...], nw)
        dx, gnw = _rms_bwd(dh, xh, r, nw)
        gnw_ref[0:1, :] += gnw
        gx_ref[...] = dout_ref[...] + dx

    row = lambda i: (i, 0)
    return _pcall(
        body,
        hook,
        name="layer_a_bwd_dx",
        grid=(t_rows // tm,),
        in_specs=[
            pl.BlockSpec((tm, d), row),
            pl.BlockSpec((tm, d), row),
            pl.BlockSpec((tm, n_sh * s_cols), row),
            _full(nw.shape),
            _full(win.shape),
        ],
        out_specs=[pl.BlockSpec((tm, d), row), _full((SUBLANES, d))],
        out_shape=[jax.ShapeDtypeStruct((t_rows, d), F32), jax.ShapeDtypeStruct((SUBLANES, d), F32)],
        compiler_params=_cparams(("arbitrary",)),
    )(dout, x, dz, nw, win)


def _decay(r, sp_h):
    log_a = (-RG_C) * r * sp_h
    a = jnp.exp(log_a)
    mult = jnp.sqrt(jnp.tanh(-log_a) * (a * a + 1.0))
    return a, mult


def _gates(xc_h, gab_ref, gb_ref, sp_h, h, hd):
    pre = _dot(xc_h.astype(BF16), gab_ref[h])
    bw = gb_ref.shape[1] // 2
    r = _sigmoid(pre[:, :hd] + gb_ref[:, h * hd : (h + 1) * hd])
    ig = _sigmoid(pre[:, hd:] + gb_ref[:, bw + h * hd : bw + (h + 1) * hd])
    a, mult = _decay(r, sp_h)
    return r, ig, a, mult


def _conv(xb, halo, cw_ref, cb_ref):
    xc = cb_ref[...] + cw_ref[CONV_WIDTH - 1 : CONV_WIDTH, :] * xb
    for k in range(CONV_WIDTH - 1):
        xc = xc + cw_ref[k : k + 1, :] * _shift_down(xb, halo, CONV_WIDTH - 1 - k)
    return xc


def _layer_b_fwd(x1, nw, bin_w, cw, cb, gab, gb, lam, bout, nf, tgt, tm):
    t_rows, d = x1.shape
    bw = bout.shape[0]
    hd = bw // B_HEADS
    nt = t_rows // tm

    def body(
        x1_ref, nw_ref, bin_ref, cw_ref, cb_ref, gab_ref, gb_ref, lam_ref, bout_ref, nf_ref, tgt_ref,
        z_ref, h_ref, h1_ref, xbt_ref, ht_ref, dx2_ref, loss_ref, gnf_ref,
        tail_s, carry_s, a_s, b_s, hs_s, acc_s,
    ):
        @pl.when(pl.program_id(0) == 0)
        def _():
            tail_s[...] = jnp.zeros_like(tail_s)
            carry_s[...] = jnp.zeros_like(carry_s)
            acc_s[...] = jnp.zeros_like(acc_s)
            gnf_ref[...] = jnp.zeros_like(gnf_ref)

        x1 = x1_ref[...]
        h1, _, _ = _rms_fwd(x1, nw_ref[...])
        h1 = h1.astype(BF16)
        h1_ref[...] = h1
        z = jnp.concatenate([_dot(h1, bin_ref[k]) for k in range(N_CHIPS)], axis=1)
        z_ref[...] = z
        xb = z[:, :bw]
        xc = _conv(xb, tail_s[...], cw_ref, cb_ref)
        tail = xb[tm - SUBLANES :, :]
        tail_s[...] = tail
        xbt_ref[...] = tail
        sp = _softplus_neg(lam_ref[...])
        for h in range(B_HEADS):
            cols = slice(h * hd, (h + 1) * hd)
            xc_h = xc[:, cols]
            _, ig, a, mult = _gates(xc_h, gab_ref, gb_ref, sp[:, cols], h, hd)
            a_s[:, cols] = a
            b_s[:, cols] = mult * (ig * xc_h)
        carry = _scan_blocks(a_s, b_s, hs_s, carry_s[...], tm, reverse=False)
        carry_s[...] = carry
        ht_ref[...] = hs_s[tm - SUBLANES :, :]
        hs = hs_s[...]
        h_ref[...] = hs
        g = z[:, bw:]
        y = (hs * (g * _sigmoid(g))).astype(BF16)
        x2 = x1 + _dot(y, bout_ref[...])

        nf = nf_ref[...]
        o, xh, r = _rms_fwd(x2, nf)
        diff = o - tgt_ref[...]
        acc_s[...] += jnp.sum(diff * diff, axis=0, keepdims=True)
        do = diff * (1.0 / d)
        dx2, gnf = _rms_bwd(do, xh, r, nf)
        gnf_ref[...] += gnf
        dx2_ref[...] = dx2

        @pl.when(pl.program_id(0) == nt - 1)
        def _():
            total = jnp.sum(acc_s[...], axis=-1, keepdims=True) * (0.5 / d)
            loss_ref[...] = jnp.broadcast_to(total, loss_ref.shape)

    row = lambda i: (i, 0)
    return _pcall(
        body,
        name="layer_b_fwd",
        grid=(nt,),
        in_specs=[
            pl.BlockSpec((tm, d), row),
            _full(nw.shape),
            _full(bin_w.shape),
            _full(cw.shape),
            _full(cb.shape),
            _full(gab.shape),
            _full(gb.shape),
            _full(lam.shape),
            _full(bout.shape),
            _full(nf.shape),
            pl.BlockSpec((tm, d), row),
        ],
        out_specs=[
            pl.BlockSpec((tm, 2 * bw), row),
            pl.BlockSpec((tm, bw), row),
            pl.BlockSpec((tm, d), row),
            pl.BlockSpec((None, SUBLANES, bw), lambda i: (i, 0, 0)),
            pl.BlockSpec((None, SUBLANES, bw), lambda i: (i, 0, 0)),
            pl.BlockSpec((tm, d), row),
            _full((1, LANES)),
            _full((1, d)),
        ],
        out_shape=[
            jax.ShapeDtypeStruct((t_rows, 2 * bw), F32),
            jax.ShapeDtypeStruct((t_rows, bw), F32),
            jax.ShapeDtypeStruct((t_rows, d), BF16),
            jax.ShapeDtypeStruct((nt, SUBLANES, bw), F32),
            jax.ShapeDtypeStruct((nt, SUBLANES, bw), F32),
            jax.ShapeDtypeStruct((t_rows, d), F32),
            jax.ShapeDtypeStruct((1, LANES), F32),
            jax.ShapeDtypeStruct((1, d), F32),
        ],
        scratch_shapes=[
            pltpu.VMEM((SUBLANES, bw), F32),
            pltpu.VMEM((SUBLANES, bw), F32),
            pltpu.VMEM((tm, bw), F32),
            pltpu.VMEM((tm, bw), F32),
            pltpu.VMEM((tm, bw), F32),
            pltpu.VMEM((1, d), F32),
        ],
        compiler_params=_cparams(("arbitrary",)),
    )(x1, nw, bin_w, cw, cb, gab, gb, lam, bout, nf, tgt)


def _layer_b_bwd(dout, x1, z, hseq, xb_tails, h_tails, nw, bin_w, cw, cb, gab, gabt, gb, lam, bout, tm):
    t_rows, d = x1.shape
    bw = bout.shape[0]
    hd = bw // B_HEADS
    nt = t_rows // tm

    def body(
        dout_ref, x1_ref, z_ref, h_ref, xbt_ref, ht_ref, nw_ref, bin_ref, cw_ref, cb_ref, gab_ref, gabt_ref,
        gb_ref, lam_ref, bout_ref,
        dx1_ref, dz_ref, y_ref, dob_ref, ggab_ref, ggb_ref, gcw_ref, gcb_ref, glam_ref, gnw_ref,
        gcarry_s, afirst_s, head_s, aup_s, dh_s, gt_s, dxc_s, xc_s, r_s, ig_s,
    ):
        step = pl.program_id(0)
        tile = nt - 1 - step

        @pl.when(step == 0)
        def _():
            for ref in (ggab_ref, ggb_ref, gcw_ref, gcb_ref, glam_ref, gnw_ref, gcarry_s, afirst_s, head_s):
                ref[...] = jnp.zeros_like(ref)

        first_tile = tile == 0
        xb_halo = jnp.where(first_tile, 0.0, xbt_ref[...])
        h_halo = jnp.where(first_tile, 0.0, ht_ref[...])

        dout = dout_ref[...]
        dob = dout.astype(BF16)
        dob_ref[...] = dob
        dy = _dot_nt(dob, bout_ref[...])
        hs = h_ref[...]
        g = z_ref[:, bw:]
        sg, dsg = _silu_and_grad(g)
        y_ref[...] = (hs * sg).astype(BF16)
        dz_ref[:, bw:] = (dy * hs * dsg).astype(BF16)
        dh_s[...] = dy * sg

        xb = z_ref[:, :bw]
        xc = _conv(xb, xb_halo, cw_ref, cb_ref)
        xc_s[...] = xc
        lam = lam_ref[...]
        sp = _softplus_neg(lam)
        for h in range(B_HEADS):
            cols = slice(h * hd, (h + 1) * hd)
            r, ig, a, _ = _gates(xc[:, cols], gab_ref, gb_ref, sp[:, cols], h, hd)
            r_s[:, cols] = r
            ig_s[:, cols] = ig
            aup_s[:, cols] = _shift_up(a, afirst_s[:, cols], 1)
            afirst_s[:, cols] = jnp.broadcast_to(a[0:1, :], (SUBLANES, hd))
        carry = _scan_blocks(aup_s, dh_s, gt_s, gcarry_s[...], tm, reverse=True)
        gcarry_s[...] = carry

        h_prev = _shift_down(hs, h_halo, 1)
        for h in range(B_HEADS):
            cols = slice(h * hd, (h + 1) * hd)
            xc_h = xc_s[:, cols]
            sp_h = sp[:, cols]
            r, ig = r_s[:, cols], ig_s[:, cols]
            a, mult = _decay(r, sp_h)
            gt = gt_s[:, cols]
            da = gt * h_prev[:, cols]
            dmult = gt * (ig * xc_h)
            dig = gt * (mult * xc_h)
            dxc_direct = gt * (mult * ig)
            dla = da * a - dmult * (a * a) / mult
            glam_ref[:, cols] += jnp.sum(dla * r, axis=0, keepdims=True)
            dr = dla * ((-RG_C) * sp_h)
            dpre = jnp.concatenate([dr * r * (1.0 - r), dig * ig * (1.0 - ig)], axis=1)
            ggb_ref[:, cols] += jnp.sum(dpre[:, :hd], axis=0, keepdims=True)
            ggb_ref[:, bw + h * hd : bw + (h + 1) * hd] += jnp.sum(dpre[:, hd:], axis=0, keepdims=True)
            dpb = dpre.astype(BF16)
            ggab_ref[h] += _dot_tn(xc_h.astype(BF16), dpb)
            dxc_s[:, cols] = dxc_direct + _dot(dpb, gabt_ref[h])
        glam_ref[...] = jnp.where(step == nt - 1, glam_ref[...] * (RG_C * _sigmoid(-lam)), glam_ref[...])

        dxc = dxc_s[...]
        gcb_ref[...] += jnp.sum(dxc, axis=0, keepdims=True)
        dxb = cw_ref[CONV_WIDTH - 1 : CONV_WIDTH, :] * dxc
        gcw_ref[CONV_WIDTH - 1 : CONV_WIDTH, :] += jnp.sum(dxc * xb, axis=0, keepdims=True)
        head = head_s[...]
        for k in range(CONV_WIDTH - 1):
            lag = CONV_WIDTH - 1 - k
            dxb = dxb + cw_ref[k : k + 1, :] * _shift_up(dxc, head, lag)
            gcw_ref[k : k + 1, :] += jnp.sum(dxc * _shift_down(xb, xb_halo, lag), axis=0, keepdims=True)
        head_s[...] = dxc[:SUBLANES, :]
        dz_ref[:, :bw] = dxb.astype(BF16)

        s_cols = 2 * bw // N_CHIPS
        dh1 = jnp.zeros((tm, d), F32)
        for k in range(N_CHIPS):
            dh1 = dh1 + _dot_nt(dz_ref[:, k * s_cols : (k + 1) * s_cols], bin_ref[k])
        x1 = x1_ref[...]
        nw = nw_ref[...]
        _, xh, r1 = _rms_fwd(x1, nw)
        dx, gnw = _rms_bwd(dh1, xh, r1, nw)
        gnw_ref[...] += gnw
        dx1_ref[...] = dout + dx

    rev = lambda i: (nt - 1 - i, 0)
    prev = lambda i: (jnp.maximum(nt - 2 - i, 0), 0, 0)
    return _pcall(
        body,
        name="layer_b_bwd",
        grid=(nt,),
        in_specs=[
            pl.BlockSpec((tm, d), rev),
            pl.BlockSpec((tm, d), rev),
            pl.BlockSpec((tm, 2 * bw), rev),
            pl.BlockSpec((tm, bw), rev),
            pl.BlockSpec((None, SUBLANES, bw), prev),
            pl.BlockSpec((None, SUBLANES, bw), prev),
            _full(nw.shape),
            _full(bin_w.shape),
            _full(cw.shape),
            _full(cb.shape),
            _full(gab.shape),
            _full(gabt.shape),
            _full(gb.shape),
            _full(lam.shape),
            _full(bout.shape),
        ],
        out_specs=[
            pl.BlockSpec((tm, d), rev),
            pl.BlockSpec((tm, 2 * bw), rev),
            pl.BlockSpec((tm, bw), rev),
            pl.BlockSpec((tm, d), rev),
            _full((B_HEADS, hd, 2 * hd)),
            _full((1, 2 * bw)),
            _full((SUBLANES, bw)),
            _full((1, bw)),
            _full((1, bw)),
            _full((1, d)),
        ],
        out_shape=[
            jax.ShapeDtypeStruct((t_rows, d), F32),
            jax.ShapeDtypeStruct((t_rows, 2 * bw), BF16),
            jax.ShapeDtypeStruct((t_rows, bw), BF16),
            jax.ShapeDtypeStruct((t_rows, d), BF16),
            jax.ShapeDtypeStruct((B_HEADS, hd, 2 * hd), F32),
            jax.ShapeDtypeStruct((1, 2 * bw), F32),
            jax.ShapeDtypeStruct((SUBLANES, bw), F32),
            jax.ShapeDtypeStruct((1, bw), F32),
            jax.ShapeDtypeStruct((1, bw), F32),
            jax.ShapeDtypeStruct((1, d), F32),
        ],
        scratch_shapes=[pltpu.VMEM((SUBLANES, bw), F32)] * 3 + [pltpu.VMEM((tm, bw), F32)] * 7,
        compiler_params=_cparams(("arbitrary",)),
    )(dout, x1, z, hseq, xb_tails, h_tails, nw, bin_w, cw, cb, gab, gabt, gb, lam, bout)


def _wgrad(a, b, m_blocks, n_blocks, hook=None):
    k, m = a.shape
    n = b.shape[1]
    bm, bn = m // m_blocks, n // n_blocks

    def body(a_ref, b_ref, o_ref):
        o_ref[...] = _dot_tn(a_ref[...], b_ref[...])

    out = _pcall(
        body,
        hook,
        name=f"wgrad_{m}x{n}",
        grid=(n_blocks, m_blocks),
        in_specs=[pl.BlockSpec((k, bm), lambda j, i: (0, i)), pl.BlockSpec((k, bn), lambda j, i: (0, j))],
        out_specs=[pl.BlockSpec((None, None, bm, bn), lambda j, i: (j, i, 0, 0))],
        out_shape=[jax.ShapeDtypeStruct((n_blocks, m_blocks, bm, bn), F32)],
        compiler_params=_cparams(("arbitrary", "arbitrary")),
    )(a, b)
    return out[0] if hook is None else (out[0][0], out[1])


def _adamw_math(w, g, m, v):
    m = ADAM_B1 * m + (1.0 - ADAM_B1) * g
    v = ADAM_B2 * v + (1.0 - ADAM_B2) * (g * g)
    m_hat = m / (1.0 - ADAM_B1**ADAM_STEP)
    v_hat = v / (1.0 - ADAM_B2**ADAM_STEP)
    delta = -ADAM_LR * (m_hat / (jnp.sqrt(v_hat) + ADAM_EPS) + ADAM_WD * w)
    return delta, m, v


def _adamw(w, g, m, v, hook=None):
    rows, cols = w.shape
    tr = _row_tile(rows, cols, 1024 * 1024)

    def body(w_ref, g_ref, m_ref, v_ref, d_ref, mo_ref, vo_ref):
        d_ref[...], mo_ref[...], vo_ref[...] = _adamw_math(w_ref[...], g_ref[...], m_ref[...], v_ref[...])

    spec = pl.BlockSpec((tr, cols), lambda i: (i, 0))
    return _pcall(
        body,
        hook,
        name=f"adamw_{rows}x{cols}",
        grid=(rows // tr,),
        in_specs=[spec] * 4,
        out_specs=[spec] * 3,
        out_shape=[jax.ShapeDtypeStruct((rows, cols), F32)] * 3,
        compiler_params=_cparams(("arbitrary",)),
    )(w, g, m, v)


def _sum_partials(parts):
    def body(p_ref, o_ref):
        total = p_ref[0, 0:1, :]
        for k in range(1, N_DEV):
            total = total + p_ref[k, 0:1, :]
        o_ref[...] = total

    vmem = pl.BlockSpec(memory_space=pltpu.VMEM)
    return _pcall(
        body,
        name="sum_partials",
        in_specs=[vmem],
        out_specs=vmem,
        out_shape=jax.ShapeDtypeStruct((1, parts.shape[2]), F32),
    )(parts)


def _adamw_many(ws, gs, ms, vs, name, hook=None):
    n = len(ws)

    def body(*refs):
        w_refs, g_refs, m_refs, v_refs = (refs[i * n : (i + 1) * n] for i in range(4))
        d_refs, mo_refs, vo_refs = (refs[(4 + i) * n : (5 + i) * n] for i in range(3))
        for i in range(n):
            d_refs[i][...], mo_refs[i][...], vo_refs[i][...] = _adamw_math(
                w_refs[i][...], g_refs[i][...], m_refs[i][...], v_refs[i][...]
            )

    vmem = pl.BlockSpec(memory_space=pltpu.VMEM)
    outs = _pcall(
        body,
        hook,
        name=name,
        in_specs=[vmem] * (4 * n),
        out_specs=[vmem] * (3 * n),
        out_shape=[jax.ShapeDtypeStruct(w.shape, F32) for w in ws] * 3,
        compiler_params=_cparams(),
    )(*ws, *gs, *ms, *vs)
    extra = None
    if hook is not None:
        outs, extra = outs
    return (outs[:n], outs[n : 2 * n], outs[2 * n :]), extra


def _pack_rows(parts, lanes=LANES):
    flat = jnp.concatenate([p.reshape(-1) for p in parts])
    per = N_DEV * SUBLANES * lanes
    total = -(-flat.shape[0] // per) * per
    flat = jnp.pad(flat, (0, total - flat.shape[0]))
    return flat.reshape(N_DEV, total // (N_DEV * lanes), lanes)


def _unpack(flat, shapes):
    out, at = [], 0
    for s in shapes:
        n = 1
        for dim in s:
            n *= dim
        out.append(flat[at : at + n].reshape(s))
        at += n
    return out


def kernel(x, norm_w, a_w_in, a_ln_w, a_ln_b, a_w_s, a_b_s, a_w_out, b_w_in, b_conv_w, b_conv_b, b_gate_a_w, b_gate_a_b, b_gate_x_w, b_gate_x_b, b_lambda, b_w_out, norm_f_w, loss_target, m_norm_w, m_a_w_in, m_a_ln_w, m_a_ln_b, m_a_w_s, m_a_b_s, m_a_w_out, m_b_w_in, m_b_conv_w, m_b_conv_b, m_b_gate_a_w, m_b_gate_a_b, m_b_gate_x_w, m_b_gate_x_b, m_b_lambda, m_b_w_out, m_norm_f_w, v_norm_w, v_a_w_in, v_a_ln_w, v_a_ln_b, v_a_w_s, v_a_b_s, v_a_w_out, v_b_w_in, v_b_conv_w, v_b_conv_b, v_b_gate_a_w, v_b_gate_a_b, v_b_gate_x_w, v_b_gate_x_b, v_b_lambda, v_b_w_out, v_norm_f_w):
    t_rows, d = x.shape[1], x.shape[2]
    aw = a_ln_w.shape[1]
    bw = b_gate_a_w.shape[1] * b_gate_a_w.shape[2]
    hd = bw // B_HEADS
    mine = 2 * lax.axis_index("x") + lax.axis_index("y")
    core = lax.axis_index("c")
    weights = dict(norm_w=norm_w, a_w_in=a_w_in, a_ln_w=a_ln_w, a_ln_b=a_ln_b, a_w_s=a_w_s, a_b_s=a_b_s, a_w_out=a_w_out, b_w_in=b_w_in, b_conv_w=b_conv_w, b_conv_b=b_conv_b, b_gate_a_w=b_gate_a_w, b_gate_a_b=b_gate_a_b, b_gate_x_w=b_gate_x_w, b_gate_x_b=b_gate_x_b, b_lambda=b_lambda, b_w_out=b_w_out, norm_f_w=norm_f_w)
    m_in = dict(norm_w=m_norm_w, a_w_in=m_a_w_in, a_ln_w=m_a_ln_w, a_ln_b=m_a_ln_b, a_w_s=m_a_w_s, a_b_s=m_a_b_s, a_w_out=m_a_w_out, b_w_in=m_b_w_in, b_conv_w=m_b_conv_w, b_conv_b=m_b_conv_b, b_gate_a_w=m_b_gate_a_w, b_gate_a_b=m_b_gate_a_b, b_gate_x_w=m_b_gate_x_w, b_gate_x_b=m_b_gate_x_b, b_lambda=m_b_lambda, b_w_out=m_b_w_out, norm_f_w=m_norm_f_w)
    v_in = dict(norm_w=v_norm_w, a_w_in=v_a_w_in, a_ln_w=v_a_ln_w, a_ln_b=v_a_ln_b, a_w_s=v_a_w_s, a_b_s=v_a_b_s, a_w_out=v_a_w_out, b_w_in=v_b_w_in, b_conv_w=v_b_conv_w, b_conv_b=v_b_conv_b, b_gate_a_w=v_b_gate_a_w, b_gate_a_b=v_b_gate_a_b, b_gate_x_w=v_b_gate_x_w, b_gate_x_b=v_b_gate_x_b, b_lambda=v_b_lambda, b_w_out=v_b_w_out, norm_f_w=v_norm_f_w)

    win_l = _cast_to_segments(a_w_in[0], mine, 256)
    wout_l = _cast_to_segments(a_w_out[0], mine, 256)
    bin_l = _cast_to_segments(b_w_in[0], mine, 256)
    bout_l = _cast_to_segments(b_w_out[0], mine, 192)
    small_l = jnp.concatenate([b_conv_w[0], b_conv_b, b_gate_a_b, b_gate_x_b, b_lambda], axis=0)
    win_g, wout_g, small_g = _run_hook(_gather_hook([win_l, wout_l], small_l), "gather_layer_a")
    win = win_g.reshape(N_CHIPS, d, -1)
    wout = wout_g.reshape(aw, d)

    tril = jnp.tril(jnp.ones((CHUNK, CHUNK), F32))
    wc = (a_w_s[0] * tril[None]).astype(BF16)
    wct = jnp.swapaxes(wc, 1, 2)
    bs_t = a_b_s[0].T
    gab = jnp.concatenate([b_gate_a_w[0], b_gate_x_w[0]], axis=2).astype(BF16)
    gabt = jnp.swapaxes(gab, 1, 2)
    nw0, nw1, nf = norm_w[0:1], norm_w[1:2], norm_f_w.reshape(1, d)

    x0 = x[0]
    (z_a, x1, h0), (bin_g, bout_g) = _layer_a_fwd(
        x0, nw0, win, a_ln_w, a_ln_b, wc, bs_t, wout, TM_FWD, _gather_hook([bin_l, bout_l])
    )
    bin_w = bin_g.reshape(N_CHIPS, d, -1)
    bout = bout_g.reshape(bw, d)
    small_f = jnp.transpose(small_g, (1, 0, 2)).reshape(SUBLANES, bw)
    cw, cb = small_f[0:CONV_WIDTH], small_f[CONV_WIDTH : CONV_WIDTH + 1]
    gb = jnp.concatenate([small_f[5:6], small_f[6:7]], axis=1)
    lam = small_f[7:8]
    z_b, hseq, h1, xb_tails, h_tails, dx2, loss_l, g_nf = _layer_b_fwd(
        x1, nw1, bin_w, cw, cb, gab, gb, lam, bout, nf, loss_target[0], TM_FWD
    )
    dx1, dz_b, y_b, dob_b, g_gab, g_gb, g_cw, g_cb, g_lam, g_nw1 = _layer_b_bwd(
        dx2, x1, z_b, hseq, xb_tails, h_tails, nw1, bin_w, cw, cb, gab, gabt, gb, lam, bout, TM_FWD
    )
    seg = lambda g: g.reshape(N_DEV, -1, g.shape[3])
    own_half = lambda bufs, got, wires: [_add_own_half(b, g, core, w) for b, g, w in zip(bufs, got, wires)]
    received = lambda bufs, got_a, got_b: [
        _add_received(b, ga, gb_, mine, core, core, 2) for b, ga, gb_ in zip(bufs, got_a, got_b)
    ]

    g_bout = seg(_wgrad(y_b, dob_b, N_CHIPS, 1))
    g_bin, got_a1 = _wgrad(h1, dz_b, 2, N_CHIPS, _swap_hook([g_bout]))
    g_bin = seg(g_bin)
    parts1 = own_half([g_bout], got_a1, [BF16])
    a_args = (z_a, a_ln_w, a_ln_b, wc, wct, bs_t, wout)
    half = t_rows // TM_A_BWD // 2
    first, rode = _layer_a_bwd(dx1, *a_args, (0, half), None, _join_hooks(_swap_hook([g_bin]), _send_hook(parts1)))
    got_a2, got_b1 = rode[:1], rode[1:]
    parts2 = own_half([g_bin], got_a2, [BF16])
    red1 = received([g_bout], got_a1, got_b1)
    second, rode = _layer_a_bwd(
        dx1, *a_args, (half, 2 * half), first[:3], _join_hooks(_send_hook(parts2), _share_hook(red1))
    )
    got_b2, gr_bout = rode[:1], rode[1].reshape(b_w_out.shape[1:])
    dz_a, y_a, dob_a = second[:3]
    g_ws, g_bst, g_lnw, g_lnb = (p + q for p, q in zip(first[3:], second[3:]))
    red2 = received([g_bin], got_a2, got_b2)
    g_win, (gr_bin,) = _wgrad(h0, dz_a, 2, N_CHIPS, _share_hook(red2))
    g_win = seg(g_win)
    gr_bin = gr_bin.reshape(b_w_in.shape[1:])
    g_wout, got_a1 = _wgrad(y_a, dob_a, N_DEV, 1, _swap_hook([g_win]))
    g_wout = seg(g_wout)

    small_shapes = [
        (1, d), (1, aw), (1, aw), (A_GROUPS, CHUNK, CHUNK), (A_GROUPS, CHUNK), (B_HEADS, hd, hd), (B_HEADS, hd, hd),
        (d,), (CONV_WIDTH, bw), (1, bw), (1, bw), (1, bw), (1, bw), (1, 1),
    ]
    small = _pack_rows(
        [
            g_nw1, g_lnw, g_lnb, g_ws, g_bst.T, g_gab[:, :, :hd], g_gab[:, :, hd:],
            g_nf, g_cw[:CONV_WIDTH], g_cb, g_gb[:, :bw], g_gb[:, bw:], g_lam, loss_l[:, :1],
        ]
    )

    parts1 = own_half([g_win], got_a1, [BF16])
    (grad_x, g_nw0_mine), rode = _layer_a_bwd_dx(
        dx1, x0, dz_a, nw0, win, TM_A_DX, _join_hooks(_send_hook(parts1), _swap_hook([g_wout, small]))
    )
    got_b1, got_a2 = rode[:1], rode[1:]
    parts2 = own_half([g_wout, small], got_a2, [BF16, F32])
    red_win = received([g_win], got_a1, got_b1)
    b_names = ("b_w_in", "b_w_out")
    b_grads = {"b_w_in": gr_bin, "b_w_out": gr_bout}
    two_d = lambda a: a.reshape(a.shape[-2:])
    b_out, rode = _adamw_many(
        [two_d(weights[n]) for n in b_names], [b_grads[n] for n in b_names], [two_d(m_in[n]) for n in b_names],
        [two_d(v_in[n]) for n in b_names], "adamw_layer_b",
        _join_hooks(_send_hook(parts2), _share_hook(red_win, None, g_nw0_mine)),
    )
    got_b2, (gr_win, g_nw0_all) = rode[:2], rode[2:]
    gr_win = gr_win.reshape(a_w_in.shape[1:])
    red_wout = received([g_wout], got_a2[:1], got_b2[:1])
    red_small = _add_received(small, got_a2[1], got_b2[1], mine, core, 2 * mine + core, N_DEV)
    gr_wout, small_r = _run_hook(_share_hook(red_wout, red_small), "share_reduced")
    win_out = _adamw(two_d(a_w_in), gr_win, two_d(m_a_w_in), two_d(v_a_w_in))
    g_nw0 = _sum_partials(g_nw0_all)
    gr_wout = gr_wout.reshape(a_w_out.shape[1:])
    (g_nw1_r, g_a_ln_w, g_a_ln_b, g_a_w_s, g_a_b_s, g_gate_a_w, g_gate_x_w, g_norm_f, gf_cw, gf_cb, gf_gab, gf_gxb,
     gf_lam, loss) = _unpack(small_r.reshape(-1), small_shapes)
    g_norm_w = jnp.concatenate([g_nw0, g_nw1_r], axis=0)
    shard = lambda g: lax.dynamic_slice_in_dim(g, mine * (bw // N_CHIPS), bw // N_CHIPS, axis=1)

    grads = {
        "norm_w": g_norm_w, "a_w_in": gr_win[None], "a_ln_w": g_a_ln_w, "a_ln_b": g_a_ln_b, "a_w_s": g_a_w_s[None],
        "a_b_s": g_a_b_s[None], "a_w_out": gr_wout[None], "b_w_in": gr_bin[None], "b_conv_w": shard(gf_cw)[None],
        "b_conv_b": shard(gf_cb), "b_gate_a_w": g_gate_a_w[None], "b_gate_a_b": shard(gf_gab),
        "b_gate_x_w": g_gate_x_w[None], "b_gate_x_b": shard(gf_gxb), "b_lambda": shard(gf_lam),
        "b_w_out": gr_bout[None], "norm_f_w": g_norm_f,
    }
    names = list(weights)
    delta, new_m, new_v = {}, {}, {}
    delta["a_w_in"], new_m["a_w_in"], new_v["a_w_in"] = win_out
    delta["a_w_out"], new_m["a_w_out"], new_v["a_w_out"] = _adamw(
        two_d(a_w_out), gr_wout, two_d(m_a_w_out), two_d(v_a_w_out)
    )
    small_names = [n for n in names if n not in ("a_w_in", "a_w_out") + b_names]
    at_least_2d = lambda a: a.reshape(1, -1) if a.ndim == 1 else a
    small_out, _ = _adamw_many(
        *[[at_least_2d(src[n]) for n in small_names] for src in (weights, grads, m_in, v_in)], "adamw_small"
    )
    for dst, vals, b_vals in zip((delta, new_m, new_v), small_out, b_out):
        dst.update(zip(small_names, vals))
        dst.update(zip(b_names, b_vals))
    for dst in (delta, new_m, new_v):
        for n in names:
            dst[n] = dst[n].reshape(weights[n].shape)

    return (
        loss.reshape(()),
        grad_x[None],
        *[grads[n] for n in names],
        *[delta[n] for n in names],
        *[new_m[n] for n in names],
        *[new_v[n] for n in names],
    )
```

```python
import jax
import jax.numpy as jnp
from jax import lax
from jax.experimental import pallas as pl
from jax.experimental.pallas import tpu as pltpu

F32 = jnp.float32
BF16 = jnp.bfloat16

RMS_EPS = 1e-6
LN_EPS = 1e-5
RG_C = 8.0
CHUNK = 128
A_GROUPS = 8
B_HEADS = 12
CONV_WIDTH = 4

ADAM_LR = 0.001
ADAM_B1 = 0.9
ADAM_B2 = 0.999
ADAM_EPS = 1e-08
ADAM_WD = 0.01
ADAM_STEP = 10

N_CHIPS = 4
N_DEV = 8
SUBLANES = 8
LANES = 128
V7X_VMEM_BYTES = 64 * 1024 * 1024
VMEM_LIMIT = V7X_VMEM_BYTES * 7 // 8
MESH = pl.DeviceIdType.MESH
ANY = pl.BlockSpec(memory_space=pl.ANY)

TM_FWD = 256
TM_A_BWD = 256
TM_A_DX = 512

GELU_C0 = 0.7978845608028654
GELU_C1 = 0.044715


class _Hook:
    def __init__(self, operands, out_shapes, aliases, n_sems, start, finish, middle=None):
        self.operands, self.out_shapes, self.aliases, self.n_sems = operands, out_shapes, aliases, n_sems
        self.start, self.finish, self.middle = start, finish, middle


class _SemView:
    def __init__(self, base, off):
        self.base, self.off = base, off

    @property
    def at(self):
        return self

    def __getitem__(self, k):
        return self.base.at[self.off + k]


def _join_hooks(*hooks):
    if len(hooks) == 1:
        return hooks[0]
    operands, out_shapes, aliases, spans = [], [], {}, []
    n_sems = 0
    for h in hooks:
        aliases.update({len(operands) + i: len(out_shapes) + o for i, o in h.aliases.items()})
        spans.append((len(operands), len(h.operands), len(out_shapes), len(h.out_shapes), n_sems))
        operands += list(h.operands)
        out_shapes += list(h.out_shapes)
        n_sems += h.n_sems

    def each(which):
        def run(ins, outs, send, recv):
            for h, (i0, ni, o0, no, s0) in zip(hooks, spans):
                step = getattr(h, which)
                if step is not None:
                    step(ins[i0 : i0 + ni], outs[o0 : o0 + no], _SemView(send, s0), _SemView(recv, s0))

        return run

    middle = each("middle") if any(h.middle is not None for h in hooks) else None
    return _Hook(operands, out_shapes, aliases, n_sems, each("start"), each("finish"), middle)


def _pcall(body, hook=None, **kw):
    if hook is None:
        return pl.pallas_call(body, **kw)
    n_in, n_out = len(kw["in_specs"]), len(kw["out_shape"])
    hi, ho = len(hook.operands), len(hook.out_shapes)
    grid = kw.get("grid", ())

    def wrapped(*refs):
        ins, h_in = refs[:n_in], refs[n_in : n_in + hi]
        outs = refs[n_in + hi : n_in + hi + n_out]
        h_out = refs[n_in + hi + n_out : n_in + hi + n_out + ho]
        scratch = refs[n_in + hi + n_out + ho : -2]
        send_sems, recv_sems = refs[-2:]
        if not grid:
            hook.start(h_in, h_out, send_sems, recv_sems)
            if hook.middle is not None:
                hook.middle(h_in, h_out, send_sems, recv_sems)
            body(*ins, *outs, *scratch)
            hook.finish(h_in, h_out, send_sems, recv_sems)
            return
        first = pl.program_id(0) == 0
        last = pl.program_id(0) == grid[0] - 1
        for axis in range(1, len(grid)):
            first = jnp.logical_and(first, pl.program_id(axis) == 0)
            last = jnp.logical_and(last, pl.program_id(axis) == grid[axis] - 1)

        @pl.when(first)
        def _():
            hook.start(h_in, h_out, send_sems, recv_sems)

        if hook.middle is not None:
            assert len(grid) == 1 and grid[0] >= 4
            half_way = pl.program_id(0) == grid[0] * 3 // 8

            @pl.when(half_way)
            def _():
                hook.middle(h_in, h_out, send_sems, recv_sems)

        body(*ins, *outs, *scratch)

        @pl.when(last)
        def _():
            hook.finish(h_in, h_out, send_sems, recv_sems)

    aliases = dict(kw.pop("input_output_aliases", {}))
    aliases.update({n_in + i: n_out + o for i, o in hook.aliases.items()})
    kw.update(
        in_specs=list(kw["in_specs"]) + [ANY] * hi,
        out_specs=list(kw["out_specs"]) + [ANY] * ho,
        out_shape=list(kw["out_shape"]) + list(hook.out_shapes),
        scratch_shapes=list(kw.get("scratch_shapes", ()))
        + [pltpu.SemaphoreType.DMA((hook.n_sems,)), pltpu.SemaphoreType.DMA((hook.n_sems,))],
        input_output_aliases=aliases,
    )
    call = pl.pallas_call(wrapped, **kw)

    def run(*operands):
        outs = call(*operands, *hook.operands)
        return outs[:n_out], outs[n_out:]

    return run


def _run_hook(hook, name):
    def body():
        pass

    return _pcall(body, hook, name=name, in_specs=[], out_specs=[], out_shape=[])()[1]


def _cparams(sem=None):
    return pltpu.CompilerParams(dimension_semantics=sem, vmem_limit_bytes=VMEM_LIMIT)


def _full(shape):
    zeros = (0,) * len(shape)
    return pl.BlockSpec(shape, lambda *_: zeros)


def _scalars(*vals):
    return jnp.stack([jnp.asarray(v, jnp.int32) for v in vals])


def _sigmoid(x):
    return 1.0 / (1.0 + jnp.exp(-x))


def _gelu(x):
    t = jnp.tanh(GELU_C0 * (x + GELU_C1 * (x * x * x)))
    return x * (0.5 * (1.0 + t))


def _gelu_and_grad(x):
    x2 = x * x
    t = jnp.tanh(GELU_C0 * (x + GELU_C1 * (x2 * x)))
    cdf = 0.5 * (1.0 + t)
    return x * cdf, cdf + 0.5 * x * (1.0 - t * t) * (GELU_C0 * (1.0 + 3.0 * GELU_C1 * x2))


def _silu_and_grad(x):
    s = _sigmoid(x)
    return x * s, s * (1.0 + x * (1.0 - s))


def _softplus_neg(lam):
    u = jnp.exp(-jnp.abs(lam))
    w = 1.0 + u
    log1p = jnp.where(w == 1.0, u, jnp.log(w) * (u / jnp.where(w == 1.0, 1.0, w - 1.0)))
    return jnp.maximum(-lam, 0.0) + log1p


def _dot(a, b):
    return jnp.dot(a, b, preferred_element_type=F32)


def _dot_nt(a, b):
    return lax.dot_general(a, b, (((1,), (1,)), ((), ())), preferred_element_type=F32)


def _dot_tn(a, b):
    return lax.dot_general(a, b, (((0,), (0,)), ((), ())), preferred_element_type=F32)


def _shift_down(v, halo, k):
    if k == 0:
        return v
    rolled = pltpu.roll(v, k, 0)
    row = lax.broadcasted_iota(jnp.int32, (SUBLANES, v.shape[1]), 0)
    top = jnp.where(row < k, pltpu.roll(halo, k, 0), rolled[:SUBLANES])
    return jnp.concatenate([top, rolled[SUBLANES:]], axis=0)


def _shift_up(v, head, k):
    if k == 0:
        return v
    n = v.shape[0]
    rolled = pltpu.roll(v, n - k, 0)
    row = lax.broadcasted_iota(jnp.int32, (SUBLANES, v.shape[1]), 0)
    bot = jnp.where(row >= SUBLANES - k, pltpu.roll(head, SUBLANES - k, 0), rolled[n - SUBLANES :])
    return jnp.concatenate([rolled[: n - SUBLANES], bot], axis=0)


def _scan_blocks(a_ref, b_ref, out_ref, carry, n_rows, reverse):
    width = a_ref.shape[1]
    row = lax.broadcasted_iota(jnp.int32, (SUBLANES, width), 0)
    n_blocks = n_rows // SUBLANES

    def block(j, carry):
        i = (n_blocks - 1 - j) if reverse else j
        r0 = pl.multiple_of(i * SUBLANES, SUBLANES)
        a = a_ref[pl.ds(r0, SUBLANES), :]
        b = b_ref[pl.ds(r0, SUBLANES), :]
        for d in (1, 2, 4):
            shift = (SUBLANES - d) if reverse else d
            keep = (row < SUBLANES - d) if reverse else (row >= d)
            a_s = pltpu.roll(a, shift, 0)
            b_s = pltpu.roll(b, shift, 0)
            b = jnp.where(keep, a * b_s + b, b)
            a = jnp.where(keep, a * a_s, a)
        h = a * carry + b
        out_ref[pl.ds(r0, SUBLANES), :] = h
        edge = h[0:1, :] if reverse else h[SUBLANES - 1 : SUBLANES, :]
        return jnp.broadcast_to(edge, (SUBLANES, width))

    return lax.fori_loop(0, n_blocks, block, carry)


def _rms_fwd(x, w):
    r = lax.rsqrt(jnp.mean(x * x, axis=-1, keepdims=True) + RMS_EPS)
    xh = x * r
    return xh * w, xh, r


def _rms_bwd(dh, xh, r, w):
    dxh = dh * w
    dx = r * (dxh - xh * jnp.mean(dxh * xh, axis=-1, keepdims=True))
    return dx, jnp.sum(dh * xh, axis=0, keepdims=True)


def _cast_to_segments(w, mine, rows):
    n, c = w.shape
    per = n // 2 // rows

    def body(k_ref, w_ref, o_ref):
        o_ref[...] = w_ref[...].astype(BF16)

    return _pcall(
        body,
        name=f"cast_{n}x{c}",
        grid_spec=pltpu.PrefetchScalarGridSpec(
            num_scalar_prefetch=1,
            grid=(n // rows,),
            in_specs=[pl.BlockSpec((rows, c), lambda i, k_ref: (i, 0))],
            out_specs=pl.BlockSpec((None, rows, c), lambda i, k_ref: (2 * k_ref[0] + i // per, i % per, 0)),
        ),
        out_shape=jax.ShapeDtypeStruct((N_DEV, n // 2, c), BF16),
        compiler_params=_cparams(("arbitrary",)),
    )(_scalars(mine), w)


def _place():
    x, y, c = lax.axis_index("x"), lax.axis_index("y"), lax.axis_index("c")
    chips = [(1 - x, y), (x, 1 - y), (1 - x, 1 - y)]
    return x, y, c, chips


def _chip_no(chip):
    return 2 * chip[0] + chip[1]


def _rcopy(src, dst, send_sem, recv_sem, to):
    return pltpu.make_async_remote_copy(
        src_ref=src, dst_ref=dst, send_sem=send_sem, recv_sem=recv_sem, device_id=to, device_id_type=MESH
    )


def _gather_hook(big, small=None):
    nb = len(big)
    n_sems = 6 * nb + 4

    def places():
        x, y, c, chips = _place()
        first = (x ^ (1 - c), y ^ c)
        second = (x ^ c, y ^ (1 - c))
        return x, y, c, chips, first, second, (1 - x, 1 - y)

    def seg(outs, b, chip, half):
        return outs[b].at[2 * _chip_no(chip) + half]

    def step1(outs, send, recv):
        x, y, c, _, first, _, _ = places()
        return [
            _rcopy(seg(outs, b, (x, y), c), seg(outs, b, (x, y), c), send.at[6 * b], recv.at[6 * b], (*first, c))
            for b in range(nb)
        ]

    def step2(outs, send, recv):
        x, y, c, _, first, second, _ = places()
        copies = []
        for b in range(nb):
            for k, chip in ((1, (x, y)), (2, first)):
                src = seg(outs, b, chip, c)
                copies.append(_rcopy(src, src, send.at[6 * b + k], recv.at[6 * b + k], (*second, c)))
        return copies

    def hand_over(outs, send, recv, k, chip):
        x, y, c, *_ = places()
        return [
            _rcopy(seg(outs, b, chip, c), seg(outs, b, chip, c), send.at[6 * b + k], recv.at[6 * b + k], (x, y, 1 - c))
            for b in range(nb)
        ]

    def wait_landed(outs, send, recv, k, chip, half):
        x, y, c, *_ = places()
        for b in range(nb):
            got = seg(outs, b, chip, half)
            _rcopy(got, got, send.at[6 * b + k], recv.at[6 * b + k], (x, y, c)).wait_recv()

    def small_copies(ins, outs, send, recv):
        x, y, c, chips, *_ = places()
        there = outs[nb].at[_chip_no((x, y))]
        return [
            _rcopy(ins[nb], there, send.at[6 * nb + j], recv.at[6 * nb + j], (*chip, c)) for j, chip in enumerate(chips)
        ]

    def local_copy(ins, outs, send):
        x, y, _, _ = _place()
        return pltpu.make_async_copy(ins[nb], outs[nb].at[_chip_no((x, y))], send.at[6 * nb + 3])

    def start(ins, outs, send, recv):
        for cp in step1(outs, send, recv):
            cp.start()
        if small is not None:
            for cp in small_copies(ins, outs, send, recv):
                cp.start()
            local_copy(ins, outs, send).start()

    def middle(ins, outs, send, recv):
        *_, first, _, _ = places()
        wait_landed(outs, send, recv, 0, first, places()[2])
        for cp in step2(outs, send, recv) + hand_over(outs, send, recv, 3, first):
            cp.start()

    def finish(ins, outs, send, recv):
        x, y, c, chips, first, second, diagonal = places()
        wait_landed(outs, send, recv, 1, second, c)
        wait_landed(outs, send, recv, 2, diagonal, c)
        late = hand_over(outs, send, recv, 4, second) + hand_over(outs, send, recv, 5, diagonal)
        for cp in late:
            cp.start()
        wait_landed(outs, send, recv, 3, second, 1 - c)
        wait_landed(outs, send, recv, 4, first, 1 - c)
        wait_landed(outs, send, recv, 5, diagonal, 1 - c)
        sent = step1(outs, send, recv) + step2(outs, send, recv) + hand_over(outs, send, recv, 3, first) + late
        for cp in sent:
            cp.wait_send()
        if small is not None:
            for j, chip in enumerate(chips):
                got = outs[nb].at[_chip_no(chip)]
                _rcopy(got, got, send.at[6 * nb + j], recv.at[6 * nb + j], (x, y, c)).wait_recv()
            for cp in small_copies(ins, outs, send, recv):
                cp.wait_send()
            local_copy(ins, outs, send).wait()

    operands = list(big) + ([small] if small is not None else [])
    out_shapes = [jax.ShapeDtypeStruct(b.shape, b.dtype) for b in big]
    if small is not None:
        out_shapes.append(jax.ShapeDtypeStruct((N_CHIPS, *small.shape), small.dtype))
    return _Hook(operands, out_shapes, {b: b for b in range(nb)}, n_sems, start, finish, middle)


def _both_ways_hook(operands, out_shapes, copies_of, n_sems):
    def start(ins, outs, send, recv):
        for cp in copies_of(ins, outs, send, recv):
            cp.start()

    def finish(ins, outs, send, recv):
        for cp in copies_of(ins, outs, send, recv):
            cp.wait()

    return _Hook(operands, out_shapes, {}, n_sems, start, finish)


def _swap_hook(bufs):
    def copies_of(ins, outs, send, recv):
        x, y, c, _ = _place()
        copies = []
        for b in range(len(bufs)):
            for j in range(N_CHIPS):
                k = b * N_CHIPS + j
                copies.append(_rcopy(ins[b].at[2 * j + 1 - c], outs[b].at[j], send.at[k], recv.at[k], (x, y, 1 - c)))
        return copies

    out_shapes = [jax.ShapeDtypeStruct((N_CHIPS, *b.shape[1:]), b.dtype) for b in bufs]
    return _both_ways_hook(list(bufs), out_shapes, copies_of, len(bufs) * N_CHIPS)


def _axis_order():
    x, y, c, _ = _place()
    return (x, y), c, (x ^ (1 - c), y ^ c), (x ^ c, y ^ (1 - c)), (1 - x, 1 - y)


def _send_first_hook(parts):
    def copies_of(ins, outs, send, recv):
        _, c, first, _, _ = _axis_order()
        copies = []
        for b in range(len(parts)):
            for k in range(2):
                sem = 2 * b + k
                copies.append(_rcopy(ins[b].at[k], outs[b].at[k], send.at[sem], recv.at[sem], (*first, c)))
        return copies

    out_shapes = [jax.ShapeDtypeStruct((2, *p.shape[1:]), p.dtype) for p in parts]
    return _both_ways_hook(list(parts), out_shapes, copies_of, len(parts) * 2)


def _send_second_hook(mids):
    def copies_of(ins, outs, send, recv):
        _, c, _, second, _ = _axis_order()
        return [_rcopy(ins[b], outs[b], send.at[b], recv.at[b], (*second, c)) for b in range(len(mids))]

    out_shapes = [jax.ShapeDtypeStruct(m.shape, m.dtype) for m in mids]
    return _both_ways_hook(list(mids), out_shapes, copies_of, len(mids))


def _share_hook(big, small=None, tiny=None):
    nb = len(big)
    n_sems = nb + 7 + N_DEV
    t0 = nb + 7

    def tiny_copies(ins, outs, send, recv):
        x, y, c, _ = _place()
        there = outs[-1].at[2 * _chip_no((x, y)) + c]
        copies = []
        for r in range(1, N_DEV):
            to = (x ^ (r >> 2 & 1), y ^ (r >> 1 & 1), c ^ (r & 1))
            copies.append(_rcopy(ins[-1], there, send.at[t0 + r], recv.at[t0 + r], to))
        return copies

    def tiny_local(ins, outs, send):
        x, y, c, _ = _place()
        return pltpu.make_async_copy(ins[-1], outs[-1].at[2 * _chip_no((x, y)) + c], send.at[t0])

    def first_copies(outs, send, recv):
        x, y, c, chips = _place()
        sibling = (x, y, 1 - c)
        copies = [_rcopy(outs[b].at[c], outs[b].at[c], send.at[b], recv.at[b], sibling) for b in range(nb)]
        if small is not None:
            own = outs[nb].at[2 * _chip_no((x, y)) + c]
            copies.append(_rcopy(own, own, send.at[nb], recv.at[nb], sibling))
            for j, chip in enumerate(chips):
                copies.append(_rcopy(own, own, send.at[nb + 1 + j], recv.at[nb + 1 + j], (*chip, c)))
        return copies

    def start(ins, outs, send, recv):
        for cp in first_copies(outs, send, recv):
            cp.start()
        if tiny is not None:
            for cp in tiny_copies(ins, outs, send, recv):
                cp.start()
            tiny_local(ins, outs, send).start()

    def finish(ins, outs, send, recv):
        x, y, c, chips = _place()
        me, sibling = (x, y, c), (x, y, 1 - c)
        if tiny is not None:
            for cp in tiny_copies(ins, outs, send, recv):
                cp.wait()
            tiny_local(ins, outs, send).wait()
        passed = []
        if small is not None:
            for j, chip in enumerate(chips):
                got = outs[nb].at[2 * _chip_no(chip) + c]
                _rcopy(got, got, send.at[nb + 1 + j], recv.at[nb + 1 + j], me).wait_recv()
                fwd = _rcopy(got, got, send.at[nb + 4 + j], recv.at[nb + 4 + j], sibling)
                fwd.start()
                passed.append(fwd)
        for b in range(nb):
            got = outs[b].at[1 - c]
            _rcopy(got, got, send.at[b], recv.at[b], me).wait_recv()
        if small is not None:
            got = outs[nb].at[2 * _chip_no((x, y)) + 1 - c]
            _rcopy(got, got, send.at[nb], recv.at[nb], me).wait_recv()
            for j, chip in enumerate(chips):
                got = outs[nb].at[2 * _chip_no(chip) + 1 - c]
                _rcopy(got, got, send.at[nb + 4 + j], recv.at[nb + 4 + j], me).wait_recv()
        for cp in first_copies(outs, send, recv) + passed:
            cp.wait_send()

    operands = list(big) + ([small] if small is not None else [])
    out_shapes = [jax.ShapeDtypeStruct(a.shape, a.dtype) for a in operands]
    aliases = {i: i for i in range(len(operands))}
    if tiny is not None:
        operands.append(tiny)
        out_shapes.append(jax.ShapeDtypeStruct((N_DEV, *tiny.shape), tiny.dtype))
    return _Hook(operands, out_shapes, aliases, n_sems, start, finish)


def _row_tile(rows, cols, target_bytes=2 * 1024 * 1024):
    best = SUBLANES
    for t in range(SUBLANES, rows + 1, SUBLANES):
        if rows % t == 0 and t * cols * 4 <= target_bytes:
            best = t
    return best


def _add_own_half(buf, got, owners, c, wire):
    _, rows, cols = buf.shape
    tr = _row_tile(rows, cols)

    def body(s_ref, a_ref, b_ref, o_ref):
        o_ref[...] = (a_ref[...] + b_ref[...]).astype(wire)

    return _pcall(
        body,
        name=f"add_own_half_{rows}x{cols}",
        grid_spec=pltpu.PrefetchScalarGridSpec(
            num_scalar_prefetch=1,
            grid=(2, rows // tr),
            in_specs=[
                pl.BlockSpec((None, None, tr, cols), lambda j, r, s_ref: (s_ref[j], s_ref[2], r, 0)),
                pl.BlockSpec((None, tr, cols), lambda j, r, s_ref: (s_ref[j], r, 0)),
            ],
            out_specs=pl.BlockSpec((None, tr, cols), lambda j, r, s_ref: (j, r, 0)),
        ),
        out_shape=jax.ShapeDtypeStruct((2, rows, cols), wire),
        compiler_params=_cparams(("arbitrary", "arbitrary")),
    )(_scalars(owners[0], owners[1], c), buf.reshape(N_CHIPS, 2, rows, cols), got)


def _add_for_neighbour(buf, got_a, got1, second, c, wire):
    _, rows, cols = buf.shape
    tr = _row_tile(rows, cols)

    def body(s_ref, x_ref, a_ref, g_ref, o_ref):
        o_ref[...] = ((x_ref[...] + a_ref[...]) + g_ref[...].astype(F32)).astype(wire)

    return _pcall(
        body,
        name=f"add_for_neighbour_{rows}x{cols}",
        grid_spec=pltpu.PrefetchScalarGridSpec(
            num_scalar_prefetch=1,
            grid=(rows // tr,),
            in_specs=[
                pl.BlockSpec((None, None, tr, cols), lambda r, s_ref: (s_ref[0], s_ref[1], r, 0)),
                pl.BlockSpec((None, tr, cols), lambda r, s_ref: (s_ref[0], r, 0)),
                pl.BlockSpec((None, tr, cols), lambda r, s_ref: (1, r, 0)),
            ],
            out_specs=pl.BlockSpec((tr, cols), lambda r, s_ref: (r, 0)),
        ),
        out_shape=jax.ShapeDtypeStruct((rows, cols), wire),
        compiler_params=_cparams(("arbitrary",)),
    )(_scalars(second, c), buf.reshape(N_CHIPS, 2, rows, cols), got_a, got1)


def _add_received(buf, got_a, got1, got2, mine, c, slot, n_slots):
    _, rows, cols = buf.shape
    tr = _row_tile(rows, cols)

    def body(s_ref, x_ref, a_ref, g1_ref, g2_ref, o_ref):
        own = x_ref[...] + a_ref[...]
        o_ref[...] = (own + g1_ref[...].astype(F32)) + g2_ref[...].astype(F32)

    return _pcall(
        body,
        name=f"add_received_{rows}x{cols}",
        grid_spec=pltpu.PrefetchScalarGridSpec(
            num_scalar_prefetch=1,
            grid=(rows // tr,),
            in_specs=[
                pl.BlockSpec((None, None, tr, cols), lambda r, s_ref: (s_ref[0], s_ref[1], r, 0)),
                pl.BlockSpec((None, tr, cols), lambda r, s_ref: (s_ref[0], r, 0)),
                pl.BlockSpec((None, tr, cols), lambda r, s_ref: (0, r, 0)),
                pl.BlockSpec((tr, cols), lambda r, s_ref: (r, 0)),
            ],
            out_specs=pl.BlockSpec((None, tr, cols), lambda r, s_ref: (s_ref[2], r, 0)),
        ),
        out_shape=jax.ShapeDtypeStruct((n_slots, rows, cols), F32),
        compiler_params=_cparams(("arbitrary",)),
    )(_scalars(mine, c, slot), buf.reshape(N_CHIPS, 2, rows, cols), got_a, got1, got2)


def _layer_a_fwd(x, nw, win, ln_w, ln_b, wc, bs_t, wout, tm, hook):
    t_rows, d = x.shape
    n_sh, _, s_cols = win.shape
    aw = wout.shape[0]
    gd = aw // A_GROUPS
    tn = 512
    assert s_cols % tn == 0 and aw % tn == 0 and tm % CHUNK == 0

    def body(x_ref, nw_ref, win_ref, lnw_ref, lnb_ref, wc_ref, bst_ref, wout_ref, z_ref, x1_ref, h_ref, u_s, v_s, y_s):
        x = x_ref[...]
        h, _, _ = _rms_fwd(x, nw_ref[...])
        h = h.astype(BF16)
        h_ref[...] = h
        for j in range(3 * aw // tn):
            k, off = divmod(j * tn, s_cols)
            cols = slice((j * tn) % aw, (j * tn) % aw + tn)
            zj = _dot(h, win_ref[k, :, off : off + tn])
            z_ref[:, j * tn : (j + 1) * tn] = zj
            if j * tn < aw:
                u_s[:, cols] = _gelu(zj)
            elif j * tn < 2 * aw:
                v_s[:, cols] = _gelu(zj)
            else:
                u_s[:, cols] = u_s[:, cols] * (zj * _sigmoid(zj))
        v = v_s[...]
        mu = jnp.mean(v, axis=-1, keepdims=True)
        vc = v - mu
        rstd = lax.rsqrt(jnp.mean(vc * vc, axis=-1, keepdims=True) + LN_EPS)
        v_s[...] = (vc * rstd) * lnw_ref[...] + lnb_ref[...]
        for ck in range(tm // CHUNK):
            rows = slice(ck * CHUNK, (ck + 1) * CHUNK)
            for g in range(A_GROUPS):
                cols = slice(g * gd, (g + 1) * gd)
                s = _dot(wc_ref[g], v_s[rows, cols].astype(BF16)) + bst_ref[:, g : g + 1]
                y_s[rows, cols] = (u_s[rows, cols] * s).astype(BF16)
        x1_ref[...] = x + _dot(y_s[...], wout_ref[...])

    row = lambda i: (i, 0)
    return _pcall(
        body,
        hook,
        name="layer_a_fwd",
        grid=(t_rows // tm,),
        in_specs=[
            pl.BlockSpec((tm, d), row),
            _full(nw.shape),
            _full(win.shape),
            _full(ln_w.shape),
            _full(ln_b.shape),
            _full(wc.shape),
            _full(bs_t.shape),
            _full(wout.shape),
        ],
        out_specs=[pl.BlockSpec((tm, 3 * aw), row), pl.BlockSpec((tm, d), row), pl.BlockSpec((tm, d), row)],
        out_shape=[
            jax.ShapeDtypeStruct((t_rows, 3 * aw), F32),
            jax.ShapeDtypeStruct((t_rows, d), F32),
            jax.ShapeDtypeStruct((t_rows, d), BF16),
        ],
        scratch_shapes=[pltpu.VMEM((tm, aw), F32), pltpu.VMEM((tm, aw), F32), pltpu.VMEM((tm, aw), BF16)],
        compiler_params=_cparams(("arbitrary",)),
    )(x, nw, win, ln_w, ln_b, wc, bs_t, wout)


def _layer_a_bwd(dout, z, ln_w, ln_b, wc, wct, bs_t, wout, tiles, earlier, hook):
    t_rows, d = dout.shape
    aw = wout.shape[0]
    gd = aw // A_GROUPS
    tm = TM_A_BWD
    lo, hi = tiles
    n_earlier = 0 if earlier is None else len(earlier)

    def body(dout_ref, z_ref, lnw_ref, lnb_ref, wc_ref, wct_ref, bst_ref, wout_ref, *rest):
        dz_ref, y_ref, dob_ref, gws_ref, gbs_ref, glnw_ref, glnb_ref, u_s, vh_s, ds_s, dvn_s = rest[n_earlier:]

        @pl.when(pl.program_id(0) == 0)
        def _():
            gws_ref[...] = jnp.zeros_like(gws_ref)
            gbs_ref[...] = jnp.zeros_like(gbs_ref)
            glnw_ref[...] = jnp.zeros_like(glnw_ref)
            glnb_ref[...] = jnp.zeros_like(glnb_ref)

        dob = dout_ref[...].astype(BF16)
        dob_ref[...] = dob
        dy = _dot_nt(dob, wout_ref[...])

        zv = z_ref[:, aw : 2 * aw]
        vg, dvg_dz = _gelu_and_grad(zv)
        mu = jnp.mean(vg, axis=-1, keepdims=True)
        vc = vg - mu
        rstd = lax.rsqrt(jnp.mean(vc * vc, axis=-1, keepdims=True) + LN_EPS)
        vh = vc * rstd
        vh_s[...] = vh
        vn = (vh * lnw_ref[...] + lnb_ref[...]).astype(BF16)

        zu = z_ref[:, 0:aw]
        zg = z_ref[:, 2 * aw : 3 * aw]
        u, du_dz = _gelu_and_grad(zu)
        sg, dsg = _silu_and_grad(zg)
        u_s[...] = u * sg
        tril = lax.broadcasted_iota(jnp.int32, (CHUNK, CHUNK), 0) >= lax.broadcasted_iota(jnp.int32, (CHUNK, CHUNK), 1)
        for ck in range(tm // CHUNK):
            rows = slice(ck * CHUNK, (ck + 1) * CHUNK)
            for g in range(A_GROUPS):
                cols = slice(g * gd, (g + 1) * gd)
                vn_g = vn[rows, cols]
                s = _dot(wc_ref[g], vn_g) + bst_ref[:, g : g + 1]
                usg = u_s[rows, cols]
                dy_g = dy[rows, cols]
                y_ref[rows, cols] = (usg * s).astype(BF16)
                ds = dy_g * usg
                ds_s[rows, cols] = dy_g * s
                gbs_ref[:, g : g + 1] += jnp.sum(ds, axis=-1, keepdims=True)
                dsb = ds.astype(BF16)
                gws_ref[g] += jnp.where(tril, _dot_nt(dsb, vn_g), 0.0)
                dvn_s[rows, cols] = _dot(wct_ref[g], dsb)
        dusg = ds_s[...]
        dz_ref[:, 0:aw] = (dusg * sg * du_dz).astype(BF16)
        dz_ref[:, 2 * aw : 3 * aw] = (dusg * u * dsg).astype(BF16)

        dvn = dvn_s[...]
        vh = vh_s[...]
        glnw_ref[...] += jnp.sum(dvn * vh, axis=0, keepdims=True)
        glnb_ref[...] += jnp.sum(dvn, axis=0, keepdims=True)
        dvh = dvn * lnw_ref[...]
        dvg = rstd * (dvh - jnp.mean(dvh, axis=-1, keepdims=True) - vh * jnp.mean(dvh * vh, axis=-1, keepdims=True))
        dz_ref[:, aw : 2 * aw] = (dvg * dvg_dz).astype(BF16)

    row = lambda i: (i + lo, 0)
    call = _pcall(
        body,
        hook,
        name=f"layer_a_bwd_{lo}",
        grid=(hi - lo,),
        in_specs=[
            pl.BlockSpec((tm, d), row),
            pl.BlockSpec((tm, 3 * aw), row),
            _full(ln_w.shape),
            _full(ln_b.shape),
            _full(wc.shape),
            _full(wct.shape),
            _full(bs_t.shape),
            _full(wout.shape),
        ]
        + [ANY] * n_earlier,
        out_specs=[
            pl.BlockSpec((tm, 3 * aw), row),
            pl.BlockSpec((tm, aw), row),
            pl.BlockSpec((tm, d), row),
            _full((A_GROUPS, CHUNK, CHUNK)),
            _full((CHUNK, A_GROUPS)),
            _full((1, aw)),
            _full((1, aw)),
        ],
        out_shape=[
            jax.ShapeDtypeStruct((t_rows, 3 * aw), BF16),
            jax.ShapeDtypeStruct((t_rows, aw), BF16),
            jax.ShapeDtypeStruct((t_rows, d), BF16),
            jax.ShapeDtypeStruct((A_GROUPS, CHUNK, CHUNK), F32),
            jax.ShapeDtypeStruct((CHUNK, A_GROUPS), F32),
            jax.ShapeDtypeStruct((1, aw), F32),
            jax.ShapeDtypeStruct((1, aw), F32),
        ],
        scratch_shapes=[pltpu.VMEM((tm, aw), F32)] * 4,
        input_output_aliases={8 + i: i for i in range(n_earlier)},
        compiler_params=_cparams(("arbitrary",)),
    )
    return call(dout, z, ln_w, ln_b, wc, wct, bs_t, wout, *(earlier or ()))


def _layer_a_bwd_dx(dout, x, dz, nw, win, tm, hook):
    t_rows, d = x.shape
    n_sh, _, s_cols = win.shape

    def body(dout_ref, x_ref, dz_ref, nw_ref, win_ref, gx_ref, gnw_ref):
        @pl.when(pl.program_id(0) == 0)
        def _():
            gnw_ref[...] = jnp.zeros_like(gnw_ref)

        dh = jnp.zeros((tm, d), F32)
        for k in range(n_sh):
            dh = dh + _dot_nt(dz_ref[:, k * s_cols : (k + 1) * s_cols], win_ref[k])
        nw = nw_ref[...]
        _, xh, r = _rms_fwd(x_ref[...], nw)
        dx, gnw = _rms_bwd(dh, xh, r, nw)
        gnw_ref[0:1, :] += gnw
        gx_ref[...] = dout_ref[...] + dx

    row = lambda i: (i, 0)
    return _pcall(
        body,
        hook,
        name="layer_a_bwd_dx",
        grid=(t_rows // tm,),
        in_specs=[
            pl.BlockSpec((tm, d), row),
            pl.BlockSpec((tm, d), row),
            pl.BlockSpec((tm, n_sh * s_cols), row),
            _full(nw.shape),
            _full(win.shape),
        ],
        out_specs=[pl.BlockSpec((tm, d), row), _full((SUBLANES, d))],
        out_shape=[jax.ShapeDtypeStruct((t_rows, d), F32), jax.ShapeDtypeStruct((SUBLANES, d), F32)],
        compiler_params=_cparams(("arbitrary",)),
    )(dout, x, dz, nw, win)


def _decay(r, sp_h):
    log_a = (-RG_C) * r * sp_h
    a = jnp.exp(log_a)
    mult = jnp.sqrt(jnp.tanh(-log_a) * (a * a + 1.0))
    return a, mult


def _gates(xc_h, gab_ref, gb_ref, sp_h, h, hd):
    pre = _dot(xc_h.astype(BF16), gab_ref[h])
    bw = gb_ref.shape[1] // 2
    r = _sigmoid(pre[:, :hd] + gb_ref[:, h * hd : (h + 1) * hd])
    ig = _sigmoid(pre[:, hd:] + gb_ref[:, bw + h * hd : bw + (h + 1) * hd])
    a, mult = _decay(r, sp_h)
    return r, ig, a, mult


def _conv(xb, halo, cw_ref, cb_ref):
    xc = cb_ref[...] + cw_ref[CONV_WIDTH - 1 : CONV_WIDTH, :] * xb
    for k in range(CONV_WIDTH - 1):
        xc = xc + cw_ref[k : k + 1, :] * _shift_down(xb, halo, CONV_WIDTH - 1 - k)
    return xc


def _layer_b_fwd(x1, nw, bin_w, cw, cb, gab, gb, lam, bout, nf, tgt, tm):
    t_rows, d = x1.shape
    bw = bout.shape[0]
    hd = bw // B_HEADS
    nt = t_rows // tm

    def body(
        x1_ref, nw_ref, bin_ref, cw_ref, cb_ref, gab_ref, gb_ref, lam_ref, bout_ref, nf_ref, tgt_ref,
        z_ref, h_ref, h1_ref, xbt_ref, ht_ref, dx2_ref, loss_ref, gnf_ref,
        tail_s, carry_s, a_s, b_s, hs_s, acc_s,
    ):
        @pl.when(pl.program_id(0) == 0)
        def _():
            tail_s[...] = jnp.zeros_like(tail_s)
            carry_s[...] = jnp.zeros_like(carry_s)
            acc_s[...] = jnp.zeros_like(acc_s)
            gnf_ref[...] = jnp.zeros_like(gnf_ref)

        x1 = x1_ref[...]
        h1, _, _ = _rms_fwd(x1, nw_ref[...])
        h1 = h1.astype(BF16)
        h1_ref[...] = h1
        z = jnp.concatenate([_dot(h1, bin_ref[k]) for k in range(N_CHIPS)], axis=1)
        z_ref[...] = z
        xb = z[:, :bw]
        xc = _conv(xb, tail_s[...], cw_ref, cb_ref)
        tail = xb[tm - SUBLANES :, :]
        tail_s[...] = tail
        xbt_ref[...] = tail
        sp = _softplus_neg(lam_ref[...])
        for h in range(B_HEADS):
            cols = slice(h * hd, (h + 1) * hd)
            xc_h = xc[:, cols]
            _, ig, a, mult = _gates(xc_h, gab_ref, gb_ref, sp[:, cols], h, hd)
            a_s[:, cols] = a
            b_s[:, cols] = mult * (ig * xc_h)
        carry = _scan_blocks(a_s, b_s, hs_s, carry_s[...], tm, reverse=False)
        carry_s[...] = carry
        ht_ref[...] = hs_s[tm - SUBLANES :, :]
        hs = hs_s[...]
        h_ref[...] = hs
        g = z[:, bw:]
        y = (hs * (g * _sigmoid(g))).astype(BF16)
        x2 = x1 + _dot(y, bout_ref[...])

        nf = nf_ref[...]
        o, xh, r = _rms_fwd(x2, nf)
        diff = o - tgt_ref[...]
        acc_s[...] += jnp.sum(diff * diff, axis=0, keepdims=True)
        do = diff * (1.0 / d)
        dx2, gnf = _rms_bwd(do, xh, r, nf)
        gnf_ref[...] += gnf
        dx2_ref[...] = dx2

        @pl.when(pl.program_id(0) == nt - 1)
        def _():
            total = jnp.sum(acc_s[...], axis=-1, keepdims=True) * (0.5 / d)
            loss_ref[...] = jnp.broadcast_to(total, loss_ref.shape)

    row = lambda i: (i, 0)
    return _pcall(
        body,
        name="layer_b_fwd",
        grid=(nt,),
        in_specs=[
            pl.BlockSpec((tm, d), row),
            _full(nw.shape),
            _full(bin_w.shape),
            _full(cw.shape),
            _full(cb.shape),
            _full(gab.shape),
            _full(gb.shape),
            _full(lam.shape),
            _full(bout.shape),
            _full(nf.shape),
            pl.BlockSpec((tm, d), row),
        ],
        out_specs=[
            pl.BlockSpec((tm, 2 * bw), row),
            pl.BlockSpec((tm, bw), row),
            pl.BlockSpec((tm, d), row),
            pl.BlockSpec((None, SUBLANES, bw), lambda i: (i, 0, 0)),
            pl.BlockSpec((None, SUBLANES, bw), lambda i: (i, 0, 0)),
            pl.BlockSpec((tm, d), row),
            _full((1, LANES)),
            _full((1, d)),
        ],
        out_shape=[
            jax.ShapeDtypeStruct((t_rows, 2 * bw), F32),
            jax.ShapeDtypeStruct((t_rows, bw), F32),
            jax.ShapeDtypeStruct((t_rows, d), BF16),
            jax.ShapeDtypeStruct((nt, SUBLANES, bw), F32),
            jax.ShapeDtypeStruct((nt, SUBLANES, bw), F32),
            jax.ShapeDtypeStruct((t_rows, d), F32),
            jax.ShapeDtypeStruct((1, LANES), F32),
            jax.ShapeDtypeStruct((1, d), F32),
        ],
        scratch_shapes=[
            pltpu.VMEM((SUBLANES, bw), F32),
            pltpu.VMEM((SUBLANES, bw), F32),
            pltpu.VMEM((tm, bw), F32),
            pltpu.VMEM((tm, bw), F32),
            pltpu.VMEM((tm, bw), F32),
            pltpu.VMEM((1, d), F32),
        ],
        compiler_params=_cparams(("arbitrary",)),
    )(x1, nw, bin_w, cw, cb, gab, gb, lam, bout, nf, tgt)


def _layer_b_bwd(dout, x1, z, hseq, xb_tails, h_tails, nw, bin_w, cw, cb, gab, gabt, gb, lam, bout, tm):
    t_rows, d = x1.shape
    bw = bout.shape[0]
    hd = bw // B_HEADS
    nt = t_rows // tm

    def body(
        dout_ref, x1_ref, z_ref, h_ref, xbt_ref, ht_ref, nw_ref, bin_ref, cw_ref, cb_ref, gab_ref, gabt_ref,
        gb_ref, lam_ref, bout_ref,
        dx1_ref, dz_ref, y_ref, dob_ref, ggab_ref, ggb_ref, gcw_ref, gcb_ref, glam_ref, gnw_ref,
        gcarry_s, afirst_s, head_s, aup_s, dh_s, gt_s, dxc_s, xc_s, r_s, ig_s,
    ):
        step = pl.program_id(0)
        tile = nt - 1 - step

        @pl.when(step == 0)
        def _():
            for ref in (ggab_ref, ggb_ref, gcw_ref, gcb_ref, glam_ref, gnw_ref, gcarry_s, afirst_s, head_s):
                ref[...] = jnp.zeros_like(ref)

        first_tile = tile == 0
        xb_halo = jnp.where(first_tile, 0.0, xbt_ref[...])
        h_halo = jnp.where(first_tile, 0.0, ht_ref[...])

        dout = dout_ref[...]
        dob = dout.astype(BF16)
        dob_ref[...] = dob
        dy = _dot_nt(dob, bout_ref[...])
        hs = h_ref[...]
        g = z_ref[:, bw:]
        sg, dsg = _silu_and_grad(g)
        y_ref[...] = (hs * sg).astype(BF16)
        dz_ref[:, bw:] = (dy * hs * dsg).astype(BF16)
        dh_s[...] = dy * sg

        xb = z_ref[:, :bw]
        xc = _conv(xb, xb_halo, cw_ref, cb_ref)
        xc_s[...] = xc
        lam = lam_ref[...]
        sp = _softplus_neg(lam)
        for h in range(B_HEADS):
            cols = slice(h * hd, (h + 1) * hd)
            r, ig, a, _ = _gates(xc[:, cols], gab_ref, gb_ref, sp[:, cols], h, hd)
            r_s[:, cols] = r
            ig_s[:, cols] = ig
            aup_s[:, cols] = _shift_up(a, afirst_s[:, cols], 1)
            afirst_s[:, cols] = jnp.broadcast_to(a[0:1, :], (SUBLANES, hd))
        carry = _scan_blocks(aup_s, dh_s, gt_s, gcarry_s[...], tm, reverse=True)
        gcarry_s[...] = carry

        h_prev = _shift_down(hs, h_halo, 1)
        for h in range(B_HEADS):
            cols = slice(h * hd, (h + 1) * hd)
            xc_h = xc_s[:, cols]
            sp_h = sp[:, cols]
            r, ig = r_s[:, cols], ig_s[:, cols]
            a, mult = _decay(r, sp_h)
            gt = gt_s[:, cols]
            da = gt * h_prev[:, cols]
            dmult = gt * (ig * xc_h)
            dig = gt * (mult * xc_h)
            dxc_direct = gt * (mult * ig)
            dla = da * a - dmult * (a * a) / mult
            glam_ref[:, cols] += jnp.sum(dla * r, axis=0, keepdims=True)
            dr = dla * ((-RG_C) * sp_h)
            dpre = jnp.concatenate([dr * r * (1.0 - r), dig * ig * (1.0 - ig)], axis=1)
            ggb_ref[:, cols] += jnp.sum(dpre[:, :hd], axis=0, keepdims=True)
            ggb_ref[:, bw + h * hd : bw + (h + 1) * hd] += jnp.sum(dpre[:, hd:], axis=0, keepdims=True)
            dpb = dpre.astype(BF16)
            ggab_ref[h] += _dot_tn(xc_h.astype(BF16), dpb)
            dxc_s[:, cols] = dxc_direct + _dot(dpb, gabt_ref[h])
        glam_ref[...] = jnp.where(step == nt - 1, glam_ref[...] * (RG_C * _sigmoid(-lam)), glam_ref[...])

        dxc = dxc_s[...]
        gcb_ref[...] += jnp.sum(dxc, axis=0, keepdims=True)
        dxb = cw_ref[CONV_WIDTH - 1 : CONV_WIDTH, :] * dxc
        gcw_ref[CONV_WIDTH - 1 : CONV_WIDTH, :] += jnp.sum(dxc * xb, axis=0, keepdims=True)
        head = head_s[...]
        for k in range(CONV_WIDTH - 1):
            lag = CONV_WIDTH - 1 - k
            dxb = dxb + cw_ref[k : k + 1, :] * _shift_up(dxc, head, lag)
            gcw_ref[k : k + 1, :] += jnp.sum(dxc * _shift_down(xb, xb_halo, lag), axis=0, keepdims=True)
        head_s[...] = dxc[:SUBLANES, :]
        dz_ref[:, :bw] = dxb.astype(BF16)

        s_cols = 2 * bw // N_CHIPS
        dh1 = jnp.zeros((tm, d), F32)
        for k in range(N_CHIPS):
            dh1 = dh1 + _dot_nt(dz_ref[:, k * s_cols : (k + 1) * s_cols], bin_ref[k])
        x1 = x1_ref[...]
        nw = nw_ref[...]
        _, xh, r1 = _rms_fwd(x1, nw)
        dx, gnw = _rms_bwd(dh1, xh, r1, nw)
        gnw_ref[...] += gnw
        dx1_ref[...] = dout + dx

    rev = lambda i: (nt - 1 - i, 0)
    prev = lambda i: (jnp.maximum(nt - 2 - i, 0), 0, 0)
    return _pcall(
        body,
        name="layer_b_bwd",
        grid=(nt,),
        in_specs=[
            pl.BlockSpec((tm, d), rev),
            pl.BlockSpec((tm, d), rev),
            pl.BlockSpec((tm, 2 * bw), rev),
            pl.BlockSpec((tm, bw), rev),
            pl.BlockSpec((None, SUBLANES, bw), prev),
            pl.BlockSpec((None, SUBLANES, bw), prev),
            _full(nw.shape),
            _full(bin_w.shape),
            _full(cw.shape),
            _full(cb.shape),
            _full(gab.shape),
            _full(gabt.shape),
            _full(gb.shape),
            _full(lam.shape),
            _full(bout.shape),
        ],
        out_specs=[
            pl.BlockSpec((tm, d), rev),
            pl.BlockSpec((tm, 2 * bw), rev),
            pl.BlockSpec((tm, bw), rev),
            pl.BlockSpec((tm, d), rev),
            _full((B_HEADS, hd, 2 * hd)),
            _full((1, 2 * bw)),
            _full((SUBLANES, bw)),
            _full((1, bw)),
            _full((1, bw)),
            _full((1, d)),
        ],
        out_shape=[
            jax.ShapeDtypeStruct((t_rows, d), F32),
            jax.ShapeDtypeStruct((t_rows, 2 * bw), BF16),
            jax.ShapeDtypeStruct((t_rows, bw), BF16),
            jax.ShapeDtypeStruct((t_rows, d), BF16),
            jax.ShapeDtypeStruct((B_HEADS, hd, 2 * hd), F32),
            jax.ShapeDtypeStruct((1, 2 * bw), F32),
            jax.ShapeDtypeStruct((SUBLANES, bw), F32),
            jax.ShapeDtypeStruct((1, bw), F32),
            jax.ShapeDtypeStruct((1, bw), F32),
            jax.ShapeDtypeStruct((1, d), F32),
        ],
        scratch_shapes=[pltpu.VMEM((SUBLANES, bw), F32)] * 3 + [pltpu.VMEM((tm, bw), F32)] * 7,
        compiler_params=_cparams(("arbitrary",)),
    )(dout, x1, z, hseq, xb_tails, h_tails, nw, bin_w, cw, cb, gab, gabt, gb, lam, bout)


def _wgrad(a, b, m_blocks, n_blocks, hook=None):
    k, m = a.shape
    n = b.shape[1]
    bm, bn = m // m_blocks, n // n_blocks

    def body(a_ref, b_ref, o_ref):
        o_ref[...] = _dot_tn(a_ref[...], b_ref[...])

    out = _pcall(
        body,
        hook,
        name=f"wgrad_{m}x{n}",
        grid=(n_blocks, m_blocks),
        in_specs=[pl.BlockSpec((k, bm), lambda j, i: (0, i)), pl.BlockSpec((k, bn), lambda j, i: (0, j))],
        out_specs=[pl.BlockSpec((None, None, bm, bn), lambda j, i: (j, i, 0, 0))],
        out_shape=[jax.ShapeDtypeStruct((n_blocks, m_blocks, bm, bn), F32)],
        compiler_params=_cparams(("arbitrary", "arbitrary")),
    )(a, b)
    return out[0] if hook is None else (out[0][0], out[1])


def _adamw_math(w, g, m, v):
    m = ADAM_B1 * m + (1.0 - ADAM_B1) * g
    v = ADAM_B2 * v + (1.0 - ADAM_B2) * (g * g)
    m_hat = m / (1.0 - ADAM_B1**ADAM_STEP)
    v_hat = v / (1.0 - ADAM_B2**ADAM_STEP)
    delta = -ADAM_LR * (m_hat / (jnp.sqrt(v_hat) + ADAM_EPS) + ADAM_WD * w)
    return delta, m, v


def _adamw(w, g, m, v, hook=None):
    rows, cols = w.shape
    tr = _row_tile(rows, cols, 1024 * 1024)

    def body(w_ref, g_ref, m_ref, v_ref, d_ref, mo_ref, vo_ref):
        d_ref[...], mo_ref[...], vo_ref[...] = _adamw_math(w_ref[...], g_ref[...], m_ref[...], v_ref[...])

    spec = pl.BlockSpec((tr, cols), lambda i: (i, 0))
    return _pcall(
        body,
        hook,
        name=f"adamw_{rows}x{cols}",
        grid=(rows // tr,),
        in_specs=[spec] * 4,
        out_specs=[spec] * 3,
        out_shape=[jax.ShapeDtypeStruct((rows, cols), F32)] * 3,
        compiler_params=_cparams(("arbitrary",)),
    )(w, g, m, v)


def _sum_partials(parts):
    def body(p_ref, o_ref):
        total = p_ref[0, 0:1, :]
        for k in range(1, N_DEV):
            total = total + p_ref[k, 0:1, :]
        o_ref[...] = total

    vmem = pl.BlockSpec(memory_space=pltpu.VMEM)
    return _pcall(
        body,
        name="sum_partials",
        in_specs=[vmem],
        out_specs=vmem,
        out_shape=jax.ShapeDtypeStruct((1, parts.shape[2]), F32),
    )(parts)


def _adamw_many(ws, gs, ms, vs, name, hook=None):
    n = len(ws)

    def body(*refs):
        w_refs, g_refs, m_refs, v_refs = (refs[i * n : (i + 1) * n] for i in range(4))
        d_refs, mo_refs, vo_refs = (refs[(4 + i) * n : (5 + i) * n] for i in range(3))
        for i in range(n):
            d_refs[i][...], mo_refs[i][...], vo_refs[i][...] = _adamw_math(
                w_refs[i][...], g_refs[i][...], m_refs[i][...], v_refs[i][...]
            )

    vmem = pl.BlockSpec(memory_space=pltpu.VMEM)
    outs = _pcall(
        body,
        hook,
        name=name,
        in_specs=[vmem] * (4 * n),
        out_specs=[vmem] * (3 * n),
        out_shape=[jax.ShapeDtypeStruct(w.shape, F32) for w in ws] * 3,
        compiler_params=_cparams(),
    )(*ws, *gs, *ms, *vs)
    extra = None
    if hook is not None:
        outs, extra = outs
    return (outs[:n], outs[n : 2 * n], outs[2 * n :]), extra


def _pack_rows(parts, lanes=LANES):
    flat = jnp.concatenate([p.reshape(-1) for p in parts])
    per = N_DEV * SUBLANES * lanes
    total = -(-flat.shape[0] // per) * per
    flat = jnp.pad(flat, (0, total - flat.shape[0]))
    return flat.reshape(N_DEV, total // (N_DEV * lanes), lanes)


def _unpack(flat, shapes):
    out, at = [], 0
    for s in shapes:
        n = 1
        for dim in s:
            n *= dim
        out.append(flat[at : at + n].reshape(s))
        at += n
    return out


def kernel(x, norm_w, a_w_in, a_ln_w, a_ln_b, a_w_s, a_b_s, a_w_out, b_w_in, b_conv_w, b_conv_b, b_gate_a_w, b_gate_a_b, b_gate_x_w, b_gate_x_b, b_lambda, b_w_out, norm_f_w, loss_target, m_norm_w, m_a_w_in, m_a_ln_w, m_a_ln_b, m_a_w_s, m_a_b_s, m_a_w_out, m_b_w_in, m_b_conv_w, m_b_conv_b, m_b_gate_a_w, m_b_gate_a_b, m_b_gate_x_w, m_b_gate_x_b, m_b_lambda, m_b_w_out, m_norm_f_w, v_norm_w, v_a_w_in, v_a_ln_w, v_a_ln_b, v_a_w_s, v_a_b_s, v_a_w_out, v_b_w_in, v_b_conv_w, v_b_conv_b, v_b_gate_a_w, v_b_gate_a_b, v_b_gate_x_w, v_b_gate_x_b, v_b_lambda, v_b_w_out, v_norm_f_w):
    t_rows, d = x.shape[1], x.shape[2]
    aw = a_ln_w.shape[1]
    bw = b_gate_a_w.shape[1] * b_gate_a_w.shape[2]
    hd = bw // B_HEADS
    mine = 2 * lax.axis_index("x") + lax.axis_index("y")
    core = lax.axis_index("c")
    weights = dict(norm_w=norm_w, a_w_in=a_w_in, a_ln_w=a_ln_w, a_ln_b=a_ln_b, a_w_s=a_w_s, a_b_s=a_b_s, a_w_out=a_w_out, b_w_in=b_w_in, b_conv_w=b_conv_w, b_conv_b=b_conv_b, b_gate_a_w=b_gate_a_w, b_gate_a_b=b_gate_a_b, b_gate_x_w=b_gate_x_w, b_gate_x_b=b_gate_x_b, b_lambda=b_lambda, b_w_out=b_w_out, norm_f_w=norm_f_w)
    m_in = dict(norm_w=m_norm_w, a_w_in=m_a_w_in, a_ln_w=m_a_ln_w, a_ln_b=m_a_ln_b, a_w_s=m_a_w_s, a_b_s=m_a_b_s, a_w_out=m_a_w_out, b_w_in=m_b_w_in, b_conv_w=m_b_conv_w, b_conv_b=m_b_conv_b, b_gate_a_w=m_b_gate_a_w, b_gate_a_b=m_b_gate_a_b, b_gate_x_w=m_b_gate_x_w, b_gate_x_b=m_b_gate_x_b, b_lambda=m_b_lambda, b_w_out=m_b_w_out, norm_f_w=m_norm_f_w)
    v_in = dict(norm_w=v_norm_w, a_w_in=v_a_w_in, a_ln_w=v_a_ln_w, a_ln_b=v_a_ln_b, a_w_s=v_a_w_s, a_b_s=v_a_b_s, a_w_out=v_a_w_out, b_w_in=v_b_w_in, b_conv_w=v_b_conv_w, b_conv_b=v_b_conv_b, b_gate_a_w=v_b_gate_a_w, b_gate_a_b=v_b_gate_a_b, b_gate_x_w=v_b_gate_x_w, b_gate_x_b=v_b_gate_x_b, b_lambda=v_b_lambda, b_w_out=v_b_w_out, norm_f_w=v_norm_f_w)

    win_l = _cast_to_segments(a_w_in[0], mine, 256)
    wout_l = _cast_to_segments(a_w_out[0], mine, 256)
    bin_l = _cast_to_segments(b_w_in[0], mine, 256)
    bout_l = _cast_to_segments(b_w_out[0], mine, 192)
    small_l = jnp.concatenate([b_conv_w[0], b_conv_b, b_gate_a_b, b_gate_x_b, b_lambda], axis=0)
    win_g, wout_g, small_g = _run_hook(_gather_hook([win_l, wout_l], small_l), "gather_layer_a")
    win = win_g.reshape(N_CHIPS, d, -1)
    wout = wout_g.reshape(aw, d)

    tril = jnp.tril(jnp.ones((CHUNK, CHUNK), F32))
    wc = (a_w_s[0] * tril[None]).astype(BF16)
    wct = jnp.swapaxes(wc, 1, 2)
    bs_t = a_b_s[0].T
    gab = jnp.concatenate([b_gate_a_w[0], b_gate_x_w[0]], axis=2).astype(BF16)
    gabt = jnp.swapaxes(gab, 1, 2)
    nw0, nw1, nf = norm_w[0:1], norm_w[1:2], norm_f_w.reshape(1, d)

    x0 = x[0]
    (z_a, x1, h0), (bin_g, bout_g) = _layer_a_fwd(
        x0, nw0, win, a_ln_w, a_ln_b, wc, bs_t, wout, TM_FWD, _gather_hook([bin_l, bout_l])
    )
    bin_w = bin_g.reshape(N_CHIPS, d, -1)
    bout = bout_g.reshape(bw, d)
    small_f = jnp.transpose(small_g, (1, 0, 2)).reshape(SUBLANES, bw)
    cw, cb = small_f[0:CONV_WIDTH], small_f[CONV_WIDTH : CONV_WIDTH + 1]
    gb = jnp.concatenate([small_f[5:6], small_f[6:7]], axis=1)
    lam = small_f[7:8]
    z_b, hseq, h1, xb_tails, h_tails, dx2, loss_l, g_nf = _layer_b_fwd(
        x1, nw1, bin_w, cw, cb, gab, gb, lam, bout, nf, loss_target[0], TM_FWD
    )
    dx1, dz_b, y_b, dob_b, g_gab, g_gb, g_cw, g_cb, g_lam, g_nw1 = _layer_b_bwd(
        dx2, x1, z_b, hseq, xb_tails, h_tails, nw1, bin_w, cw, cb, gab, gabt, gb, lam, bout, TM_FWD
    )
    seg = lambda g: g.reshape(N_DEV, -1, g.shape[3])
    x_at, y_at = lax.axis_index("x"), lax.axis_index("y")
    first_no = 2 * (x_at ^ (1 - core)) + (y_at ^ core)
    second_no = 2 * (x_at ^ core) + (y_at ^ (1 - core))
    bf16s = lambda bufs: [BF16] * len(bufs)
    own_half = lambda bufs, got, wires: [
        _add_own_half(b, g, (first_no, N_CHIPS - 1 - mine), core, w) for b, g, w in zip(bufs, got, wires)
    ]
    for_neighbour = lambda bufs, got_a, got1, wires: [
        _add_for_neighbour(b, ga, g1, second_no, core, w) for b, ga, g1, w in zip(bufs, got_a, got1, wires)
    ]
    received = lambda bufs, got_a, got1, got2: [
        _add_received(b, ga, g1, g2, mine, core, core, 2) for b, ga, g1, g2 in zip(bufs, got_a, got1, got2)
    ]

    g_bout = [seg(_wgrad(y_b, dob_b, N_CHIPS, 1))]
    g_bin, swap_o = _wgrad(h1, dz_b, 2, N_CHIPS, _swap_hook(g_bout))
    g_bin = [seg(g_bin)]
    part_o = own_half(g_bout, swap_o, bf16s(g_bout))
    a_args = (z_a, a_ln_w, a_ln_b, wc, wct, bs_t, wout)
    half = t_rows // TM_A_BWD // 2
    first, rode = _layer_a_bwd(
        dx1, *a_args, (0, half), None, _join_hooks(_swap_hook(g_bin), _send_first_hook(part_o))
    )
    swap_i, got1_o = rode[:1], rode[1:]
    part_i = own_half(g_bin, swap_i, bf16s(g_bin))
    mid_o = for_neighbour(g_bout, swap_o, got1_o, bf16s(g_bout))
    second, rode = _layer_a_bwd(
        dx1, *a_args, (half, 2 * half), first[:3], _join_hooks(_send_first_hook(part_i), _send_second_hook(mid_o))
    )
    got1_i, got2_o = rode[:1], rode[1:]
    dz_a, y_a, dob_a = second[:3]
    g_ws, g_bst, g_lnw, g_lnb = (p + q for p, q in zip(first[3:], second[3:]))
    mid_i = for_neighbour(g_bin, swap_i, got1_i, bf16s(g_bin))
    red_o = received(g_bout, swap_o, got1_o, got2_o)
    g_win, rode = _wgrad(h0, dz_a, 2, N_CHIPS, _join_hooks(_send_second_hook(mid_i), _share_hook(red_o)))
    g_win = [seg(g_win)]
    got2_i, gr_bout = rode[:1], rode[1].reshape(b_w_out.shape[1:])
    red_i = received(g_bin, swap_i, got1_i, got2_i)
    g_wout, rode = _wgrad(y_a, dob_a, N_DEV, 1, _join_hooks(_swap_hook(g_win), _share_hook(red_i)))
    g_wout = seg(g_wout)
    swap_w, gr_bin = rode[:1], rode[1].reshape(b_w_in.shape[1:])

    small_shapes = [
        (1, d), (1, aw), (1, aw), (A_GROUPS, CHUNK, CHUNK), (A_GROUPS, CHUNK), (B_HEADS, hd, hd), (B_HEADS, hd, hd),
        (d,), (CONV_WIDTH, bw), (1, bw), (1, bw), (1, bw), (1, bw), (1, 1),
    ]
    small = _pack_rows(
        [
            g_nw1, g_lnw, g_lnb, g_ws, g_bst.T, g_gab[:, :, :hd], g_gab[:, :, hd:],
            g_nf, g_cw[:CONV_WIDTH], g_cb, g_gb[:, :bw], g_gb[:, bw:], g_lam, loss_l[:, :1],
        ]
    )

    g_u, wire_u = [g_wout, small], [BF16, F32]
    part_w = own_half(g_win, swap_w, bf16s(g_win))
    (grad_x, g_nw0_mine), rode = _layer_a_bwd_dx(
        dx1, x0, dz_a, nw0, win, TM_A_DX, _join_hooks(_send_first_hook(part_w), _swap_hook(g_u))
    )
    got1_w, swap_u = rode[:1], rode[1:]
    part_u = own_half(g_u, swap_u, wire_u)
    mid_w = for_neighbour(g_win, swap_w, got1_w, bf16s(g_win))
    b_names = ("b_w_in", "b_w_out")
    b_grads = {"b_w_in": gr_bin, "b_w_out": gr_bout}
    two_d = lambda a: a.reshape(a.shape[-2:])
    b_out, rode = _adamw_many(
        [two_d(weights[n]) for n in b_names], [b_grads[n] for n in b_names], [two_d(m_in[n]) for n in b_names],
        [two_d(v_in[n]) for n in b_names], "adamw_layer_b",
        _join_hooks(_send_second_hook(mid_w), _send_first_hook(part_u)),
    )
    got2_w, got1_u = rode[:1], rode[1:]
    mid_u = for_neighbour(g_u, swap_u, got1_u, wire_u)
    red_w = received(g_win, swap_w, got1_w, got2_w)
    rode = _run_hook(_join_hooks(_send_second_hook(mid_u), _share_hook(red_w)), "second_axis_and_share")
    got2_u, gr_win = rode[:2], rode[2].reshape(a_w_in.shape[1:])
    red_wout = received(g_u[:1], swap_u[:1], got1_u[:1], got2_u[:1])
    red_small = _add_received(small, swap_u[1], got1_u[1], got2_u[1], mine, core, 2 * mine + core, N_DEV)
    gr_wout, small_r, g_nw0_all = _run_hook(_share_hook(red_wout, red_small, g_nw0_mine), "share_reduced")
    win_out = _adamw(two_d(a_w_in), gr_win, two_d(m_a_w_in), two_d(v_a_w_in))
    g_nw0 = _sum_partials(g_nw0_all)
    gr_wout = gr_wout.reshape(a_w_out.shape[1:])
    (g_nw1_r, g_a_ln_w, g_a_ln_b, g_a_w_s, g_a_b_s, g_gate_a_w, g_gate_x_w, g_norm_f, gf_cw, gf_cb, gf_gab, gf_gxb,
     gf_lam, loss) = _unpack(small_r.reshape(-1), small_shapes)
    g_norm_w = jnp.concatenate([g_nw0, g_nw1_r], axis=0)
    shard = lambda g: lax.dynamic_slice_in_dim(g, mine * (bw // N_CHIPS), bw // N_CHIPS, axis=1)

    grads = {
        "norm_w": g_norm_w, "a_w_in": gr_win[None], "a_ln_w": g_a_ln_w, "a_ln_b": g_a_ln_b, "a_w_s": g_a_w_s[None],
        "a_b_s": g_a_b_s[None], "a_w_out": gr_wout[None], "b_w_in": gr_bin[None], "b_conv_w": shard(gf_cw)[None],
        "b_conv_b": shard(gf_cb), "b_gate_a_w": g_gate_a_w[None], "b_gate_a_b": shard(gf_gab),
        "b_gate_x_w": g_gate_x_w[None], "b_gate_x_b": shard(gf_gxb), "b_lambda": shard(gf_lam),
        "b_w_out": gr_bout[None], "norm_f_w": g_norm_f,
    }
    names = list(weights)
    delta, new_m, new_v = {}, {}, {}
    delta["a_w_in"], new_m["a_w_in"], new_v["a_w_in"] = win_out
    delta["a_w_out"], new_m["a_w_out"], new_v["a_w_out"] = _adamw(
        two_d(a_w_out), gr_wout, two_d(m_a_w_out), two_d(v_a_w_out)
    )
    small_names = [n for n in names if n not in ("a_w_in", "a_w_out") + b_names]
    at_least_2d = lambda a: a.reshape(1, -1) if a.ndim == 1 else a
    small_out, _ = _adamw_many(
        *[[at_least_2d(src[n]) for n in small_names] for src in (weights, grads, m_in, v_in)], "adamw_small"
    )
    for dst, vals, b_vals in zip((delta, new_m, new_v), small_out, b_out):
        dst.update(zip(small_names, vals))
        dst.update(zip(b_names, b_vals))
    for dst in (delta, new_m, new_v):
        for n in names:
            dst[n] = dst[n].reshape(weights[n].shape)

    return (
        loss.reshape(()),
        grad_x[None],
        *[grads[n] for n in names],
        *[delta[n] for n in names],
        *[new_m[n] for n in names],
        *[new_v[n] for n in names],
    )
```

```python
import jax
import jax.numpy as jnp
from jax import lax
from jax.experimental import pallas as pl
from jax.experimental.pallas import tpu as pltpu

F32 = jnp.float32
BF16 = jnp.bfloat16

RMS_EPS = 1e-6
LN_EPS = 1e-5
RG_C = 8.0
CHUNK = 128
A_GROUPS = 8
B_HEADS = 12
CONV_WIDTH = 4

ADAM_LR = 0.001
ADAM_B1 = 0.9
ADAM_B2 = 0.999
ADAM_EPS = 1e-08
ADAM_WD = 0.01
ADAM_STEP = 10

N_CHIPS = 4
N_DEV = 8
SUBLANES = 8
LANES = 128
V7X_VMEM_BYTES = 64 * 1024 * 1024
VMEM_LIMIT = V7X_VMEM_BYTES * 7 // 8
MESH = pl.DeviceIdType.MESH
ANY = pl.BlockSpec(memory_space=pl.ANY)

TM_FWD = 256
TM_A_BWD = 256
TM_A_DX = 512

GELU_C0 = 0.7978845608028654
GELU_C1 = 0.044715


class _Hook:
    def __init__(self, operands, out_shapes, aliases, n_sems, start, finish, middle=None):
        self.operands, self.out_shapes, self.aliases, self.n_sems = operands, out_shapes, aliases, n_sems
        self.start, self.finish, self.middle = start, finish, middle


class _SemView:
    def __init__(self, base, off):
        self.base, self.off = base, off

    @property
    def at(self):
        return self

    def __getitem__(self, k):
        return self.base.at[self.off + k]


def _join_hooks(*hooks):
    if len(hooks) == 1:
        return hooks[0]
    operands, out_shapes, aliases, spans = [], [], {}, []
    n_sems = 0
    for h in hooks:
        aliases.update({len(operands) + i: len(out_shapes) + o for i, o in h.aliases.items()})
        spans.append((len(operands), len(h.operands), len(out_shapes), len(h.out_shapes), n_sems))
        operands += list(h.operands)
        out_shapes += list(h.out_shapes)
        n_sems += h.n_sems

    def each(which):
        def run(ins, outs, send, recv):
            for h, (i0, ni, o0, no, s0) in zip(hooks, spans):
                step = getattr(h, which)
                if step is not None:
                    step(ins[i0 : i0 + ni], outs[o0 : o0 + no], _SemView(send, s0), _SemView(recv, s0))

        return run

    middle = each("middle") if any(h.middle is not None for h in hooks) else None
    return _Hook(operands, out_shapes, aliases, n_sems, each("start"), each("finish"), middle)


def _pcall(body, hook=None, **kw):
    if hook is None:
        return pl.pallas_call(body, **kw)
    n_in, n_out = len(kw["in_specs"]), len(kw["out_shape"])
    hi, ho = len(hook.operands), len(hook.out_shapes)
    grid = kw.get("grid", ())

    def wrapped(*refs):
        ins, h_in = refs[:n_in], refs[n_in : n_in + hi]
        outs = refs[n_in + hi : n_in + hi + n_out]
        h_out = refs[n_in + hi + n_out : n_in + hi + n_out + ho]
        scratch = refs[n_in + hi + n_out + ho : -2]
        send_sems, recv_sems = refs[-2:]
        if not grid:
            hook.start(h_in, h_out, send_sems, recv_sems)
            if hook.middle is not None:
                hook.middle(h_in, h_out, send_sems, recv_sems)
            body(*ins, *outs, *scratch)
            hook.finish(h_in, h_out, send_sems, recv_sems)
            return
        first = pl.program_id(0) == 0
        last = pl.program_id(0) == grid[0] - 1
        for axis in range(1, len(grid)):
            first = jnp.logical_and(first, pl.program_id(axis) == 0)
            last = jnp.logical_and(last, pl.program_id(axis) == grid[axis] - 1)

        @pl.when(first)
        def _():
            hook.start(h_in, h_out, send_sems, recv_sems)

        if hook.middle is not None:
            assert len(grid) == 1 and grid[0] >= 4
            half_way = pl.program_id(0) == grid[0] * 3 // 8

            @pl.when(half_way)
            def _():
                hook.middle(h_in, h_out, send_sems, recv_sems)

        body(*ins, *outs, *scratch)

        @pl.when(last)
        def _():
            hook.finish(h_in, h_out, send_sems, recv_sems)

    aliases = dict(kw.pop("input_output_aliases", {}))
    aliases.update({n_in + i: n_out + o for i, o in hook.aliases.items()})
    kw.update(
        in_specs=list(kw["in_specs"]) + [ANY] * hi,
        out_specs=list(kw["out_specs"]) + [ANY] * ho,
        out_shape=list(kw["out_shape"]) + list(hook.out_shapes),
        scratch_shapes=list(kw.get("scratch_shapes", ()))
        + [pltpu.SemaphoreType.DMA((hook.n_sems,)), pltpu.SemaphoreType.DMA((hook.n_sems,))],
        input_output_aliases=aliases,
    )
    call = pl.pallas_call(wrapped, **kw)

    def run(*operands):
        outs = call(*operands, *hook.operands)
        return outs[:n_out], outs[n_out:]

    return run


def _run_hook(hook, name):
    def body():
        pass

    return _pcall(body, hook, name=name, in_specs=[], out_specs=[], out_shape=[])()[1]


def _cparams(sem=None):
    return pltpu.CompilerParams(dimension_semantics=sem, vmem_limit_bytes=VMEM_LIMIT)


def _full(shape):
    zeros = (0,) * len(shape)
    return pl.BlockSpec(shape, lambda *_: zeros)


def _scalars(*vals):
    return jnp.stack([jnp.asarray(v, jnp.int32) for v in vals])


def _sigmoid(x):
    return 1.0 / (1.0 + jnp.exp(-x))


def _gelu(x):
    t = jnp.tanh(GELU_C0 * (x + GELU_C1 * (x * x * x)))
    return x * (0.5 * (1.0 + t))


def _gelu_and_grad(x):
    x2 = x * x
    t = jnp.tanh(GELU_C0 * (x + GELU_C1 * (x2 * x)))
    cdf = 0.5 * (1.0 + t)
    return x * cdf, cdf + 0.5 * x * (1.0 - t * t) * (GELU_C0 * (1.0 + 3.0 * GELU_C1 * x2))


def _silu_and_grad(x):
    s = _sigmoid(x)
    return x * s, s * (1.0 + x * (1.0 - s))


def _softplus_neg(lam):
    u = jnp.exp(-jnp.abs(lam))
    w = 1.0 + u
    log1p = jnp.where(w == 1.0, u, jnp.log(w) * (u / jnp.where(w == 1.0, 1.0, w - 1.0)))
    return jnp.maximum(-lam, 0.0) + log1p


def _dot(a, b):
    return jnp.dot(a, b, preferred_element_type=F32)


def _dot_nt(a, b):
    return lax.dot_general(a, b, (((1,), (1,)), ((), ())), preferred_element_type=F32)


def _dot_tn(a, b):
    return lax.dot_general(a, b, (((0,), (0,)), ((), ())), preferred_element_type=F32)


def _shift_down(v, halo, k):
    if k == 0:
        return v
    rolled = pltpu.roll(v, k, 0)
    row = lax.broadcasted_iota(jnp.int32, (SUBLANES, v.shape[1]), 0)
    top = jnp.where(row < k, pltpu.roll(halo, k, 0), rolled[:SUBLANES])
    return jnp.concatenate([top, rolled[SUBLANES:]], axis=0)


def _shift_up(v, head, k):
    if k == 0:
        return v
    n = v.shape[0]
    rolled = pltpu.roll(v, n - k, 0)
    row = lax.broadcasted_iota(jnp.int32, (SUBLANES, v.shape[1]), 0)
    bot = jnp.where(row >= SUBLANES - k, pltpu.roll(head, SUBLANES - k, 0), rolled[n - SUBLANES :])
    return jnp.concatenate([rolled[: n - SUBLANES], bot], axis=0)


def _scan_blocks(a_ref, b_ref, out_ref, carry, n_rows, reverse):
    width = a_ref.shape[1]
    row = lax.broadcasted_iota(jnp.int32, (SUBLANES, width), 0)
    n_blocks = n_rows // SUBLANES

    def block(j, carry):
        i = (n_blocks - 1 - j) if reverse else j
        r0 = pl.multiple_of(i * SUBLANES, SUBLANES)
        a = a_ref[pl.ds(r0, SUBLANES), :]
        b = b_ref[pl.ds(r0, SUBLANES), :]
        for d in (1, 2, 4):
            shift = (SUBLANES - d) if reverse else d
            keep = (row < SUBLANES - d) if reverse else (row >= d)
            a_s = pltpu.roll(a, shift, 0)
            b_s = pltpu.roll(b, shift, 0)
            b = jnp.where(keep, a * b_s + b, b)
            a = jnp.where(keep, a * a_s, a)
        h = a * carry + b
        out_ref[pl.ds(r0, SUBLANES), :] = h
        edge = h[0:1, :] if reverse else h[SUBLANES - 1 : SUBLANES, :]
        return jnp.broadcast_to(edge, (SUBLANES, width))

    return lax.fori_loop(0, n_blocks, block, carry)


def _rms_fwd(x, w):
    r = lax.rsqrt(jnp.mean(x * x, axis=-1, keepdims=True) + RMS_EPS)
    xh = x * r
    return xh * w, xh, r


def _rms_bwd(dh, xh, r, w):
    dxh = dh * w
    dx = r * (dxh - xh * jnp.mean(dxh * xh, axis=-1, keepdims=True))
    return dx, jnp.sum(dh * xh, axis=0, keepdims=True)


def _cast_to_segments(w, mine, rows):
    n, c = w.shape
    per = n // 2 // rows

    def body(k_ref, w_ref, o_ref):
        o_ref[...] = w_ref[...].astype(BF16)

    return _pcall(
        body,
        name=f"cast_{n}x{c}",
        grid_spec=pltpu.PrefetchScalarGridSpec(
            num_scalar_prefetch=1,
            grid=(n // rows,),
            in_specs=[pl.BlockSpec((rows, c), lambda i, k_ref: (i, 0))],
            out_specs=pl.BlockSpec((None, rows, c), lambda i, k_ref: (2 * k_ref[0] + i // per, i % per, 0)),
        ),
        out_shape=jax.ShapeDtypeStruct((N_DEV, n // 2, c), BF16),
        compiler_params=_cparams(("arbitrary",)),
    )(_scalars(mine), w)


def _place():
    x, y, c = lax.axis_index("x"), lax.axis_index("y"), lax.axis_index("c")
    chips = [(1 - x, y), (x, 1 - y), (1 - x, 1 - y)]
    return x, y, c, chips


def _chip_no(chip):
    return 2 * chip[0] + chip[1]


def _rcopy(src, dst, send_sem, recv_sem, to):
    return pltpu.make_async_remote_copy(
        src_ref=src, dst_ref=dst, send_sem=send_sem, recv_sem=recv_sem, device_id=to, device_id_type=MESH
    )


def _gather_hook(big, small=None):
    nb = len(big)
    n_sems = 6 * nb + 4

    def places():
        x, y, c, chips = _place()
        first = (x ^ (1 - c), y ^ c)
        second = (x ^ c, y ^ (1 - c))
        return x, y, c, chips, first, second, (1 - x, 1 - y)

    def seg(outs, b, chip, half):
        return outs[b].at[2 * _chip_no(chip) + half]

    def step1(outs, send, recv):
        x, y, c, _, first, _, _ = places()
        return [
            _rcopy(seg(outs, b, (x, y), c), seg(outs, b, (x, y), c), send.at[6 * b], recv.at[6 * b], (*first, c))
            for b in range(nb)
        ]

    def step2(outs, send, recv):
        x, y, c, _, first, second, _ = places()
        copies = []
        for b in range(nb):
            for k, chip in ((1, (x, y)), (2, first)):
                src = seg(outs, b, chip, c)
                copies.append(_rcopy(src, src, send.at[6 * b + k], recv.at[6 * b + k], (*second, c)))
        return copies

    def hand_over(outs, send, recv, k, chip):
        x, y, c, *_ = places()
        return [
            _rcopy(seg(outs, b, chip, c), seg(outs, b, chip, c), send.at[6 * b + k], recv.at[6 * b + k], (x, y, 1 - c))
            for b in range(nb)
        ]

    def wait_landed(outs, send, recv, k, chip, half):
        x, y, c, *_ = places()
        for b in range(nb):
            got = seg(outs, b, chip, half)
            _rcopy(got, got, send.at[6 * b + k], recv.at[6 * b + k], (x, y, c)).wait_recv()

    def small_copies(ins, outs, send, recv):
        x, y, c, chips, *_ = places()
        there = outs[nb].at[_chip_no((x, y))]
        return [
            _rcopy(ins[nb], there, send.at[6 * nb + j], recv.at[6 * nb + j], (*chip, c)) for j, chip in enumerate(chips)
        ]

    def local_copy(ins, outs, send):
        x, y, _, _ = _place()
        return pltpu.make_async_copy(ins[nb], outs[nb].at[_chip_no((x, y))], send.at[6 * nb + 3])

    def start(ins, outs, send, recv):
        for cp in step1(outs, send, recv):
            cp.start()
        if small is not None:
            for cp in small_copies(ins, outs, send, recv):
                cp.start()
            local_copy(ins, outs, send).start()

    def middle(ins, outs, send, recv):
        *_, first, _, _ = places()
        wait_landed(outs, send, recv, 0, first, places()[2])
        for cp in step2(outs, send, recv) + hand_over(outs, send, recv, 3, first):
            cp.start()

    def finish(ins, outs, send, recv):
        x, y, c, chips, first, second, diagonal = places()
        wait_landed(outs, send, recv, 1, second, c)
        wait_landed(outs, send, recv, 2, diagonal, c)
        late = hand_over(outs, send, recv, 4, second) + hand_over(outs, send, recv, 5, diagonal)
        for cp in late:
            cp.start()
        wait_landed(outs, send, recv, 3, second, 1 - c)
        wait_landed(outs, send, recv, 4, first, 1 - c)
        wait_landed(outs, send, recv, 5, diagonal, 1 - c)
        sent = step1(outs, send, recv) + step2(outs, send, recv) + hand_over(outs, send, recv, 3, first) + late
        for cp in sent:
            cp.wait_send()
        if small is not None:
            for j, chip in enumerate(chips):
                got = outs[nb].at[_chip_no(chip)]
                _rcopy(got, got, send.at[6 * nb + j], recv.at[6 * nb + j], (x, y, c)).wait_recv()
            for cp in small_copies(ins, outs, send, recv):
                cp.wait_send()
            local_copy(ins, outs, send).wait()

    operands = list(big) + ([small] if small is not None else [])
    out_shapes = [jax.ShapeDtypeStruct(b.shape, b.dtype) for b in big]
    if small is not None:
        out_shapes.append(jax.ShapeDtypeStruct((N_CHIPS, *small.shape), small.dtype))
    return _Hook(operands, out_shapes, {b: b for b in range(nb)}, n_sems, start, finish, middle)


def _both_ways_hook(operands, out_shapes, copies_of, n_sems):
    def start(ins, outs, send, recv):
        for cp in copies_of(ins, outs, send, recv):
            cp.start()

    def finish(ins, outs, send, recv):
        for cp in copies_of(ins, outs, send, recv):
            cp.wait()

    return _Hook(operands, out_shapes, {}, n_sems, start, finish)


def _swap_hook(bufs):
    def copies_of(ins, outs, send, recv):
        x, y, c, _ = _place()
        copies = []
        for b in range(len(bufs)):
            for j in range(N_CHIPS):
                k = b * N_CHIPS + j
                copies.append(_rcopy(ins[b].at[2 * j + 1 - c], outs[b].at[j], send.at[k], recv.at[k], (x, y, 1 - c)))
        return copies

    out_shapes = [jax.ShapeDtypeStruct((N_CHIPS, *b.shape[1:]), b.dtype) for b in bufs]
    return _both_ways_hook(list(bufs), out_shapes, copies_of, len(bufs) * N_CHIPS)


def _axis_order():
    x, y, c, _ = _place()
    return (x, y), c, (x ^ (1 - c), y ^ c), (x ^ c, y ^ (1 - c)), (1 - x, 1 - y)


def _send_first_hook(parts):
    def copies_of(ins, outs, send, recv):
        _, c, first, _, _ = _axis_order()
        copies = []
        for b in range(len(parts)):
            for k in range(2):
                sem = 2 * b + k
                copies.append(_rcopy(ins[b].at[k], outs[b].at[k], send.at[sem], recv.at[sem], (*first, c)))
        return copies

    out_shapes = [jax.ShapeDtypeStruct((2, *p.shape[1:]), p.dtype) for p in parts]
    return _both_ways_hook(list(parts), out_shapes, copies_of, len(parts) * 2)


def _send_second_hook(mids):
    def copies_of(ins, outs, send, recv):
        _, c, _, second, _ = _axis_order()
        return [_rcopy(ins[b], outs[b], send.at[b], recv.at[b], (*second, c)) for b in range(len(mids))]

    out_shapes = [jax.ShapeDtypeStruct(m.shape, m.dtype) for m in mids]
    return _both_ways_hook(list(mids), out_shapes, copies_of, len(mids))


def _share_hook(big, small=None, tiny=None):
    nb = len(big)
    n_sems = nb + 7 + N_DEV
    t0 = nb + 7

    def tiny_copies(ins, outs, send, recv):
        x, y, c, _ = _place()
        there = outs[-1].at[2 * _chip_no((x, y)) + c]
        copies = []
        for r in range(1, N_DEV):
            to = (x ^ (r >> 2 & 1), y ^ (r >> 1 & 1), c ^ (r & 1))
            copies.append(_rcopy(ins[-1], there, send.at[t0 + r], recv.at[t0 + r], to))
        return copies

    def tiny_local(ins, outs, send):
        x, y, c, _ = _place()
        return pltpu.make_async_copy(ins[-1], outs[-1].at[2 * _chip_no((x, y)) + c], send.at[t0])

    def first_copies(outs, send, recv):
        x, y, c, chips = _place()
        sibling = (x, y, 1 - c)
        copies = [_rcopy(outs[b].at[c], outs[b].at[c], send.at[b], recv.at[b], sibling) for b in range(nb)]
        if small is not None:
            own = outs[nb].at[2 * _chip_no((x, y)) + c]
            copies.append(_rcopy(own, own, send.at[nb], recv.at[nb], sibling))
            for j, chip in enumerate(chips):
                copies.append(_rcopy(own, own, send.at[nb + 1 + j], recv.at[nb + 1 + j], (*chip, c)))
        return copies

    def start(ins, outs, send, recv):
        for cp in first_copies(outs, send, recv):
            cp.start()
        if tiny is not None:
            for cp in tiny_copies(ins, outs, send, recv):
                cp.start()
            tiny_local(ins, outs, send).start()

    def finish(ins, outs, send, recv):
        x, y, c, chips = _place()
        me, sibling = (x, y, c), (x, y, 1 - c)
        if tiny is not None:
            for cp in tiny_copies(ins, outs, send, recv):
                cp.wait()
            tiny_local(ins, outs, send).wait()
        passed = []
        if small is not None:
            for j, chip in enumerate(chips):
                got = outs[nb].at[2 * _chip_no(chip) + c]
                _rcopy(got, got, send.at[nb + 1 + j], recv.at[nb + 1 + j], me).wait_recv()
                fwd = _rcopy(got, got, send.at[nb + 4 + j], recv.at[nb + 4 + j], sibling)
                fwd.start()
                passed.append(fwd)
        for b in range(nb):
            got = outs[b].at[1 - c]
            _rcopy(got, got, send.at[b], recv.at[b], me).wait_recv()
        if small is not None:
            got = outs[nb].at[2 * _chip_no((x, y)) + 1 - c]
            _rcopy(got, got, send.at[nb], recv.at[nb], me).wait_recv()
            for j, chip in enumerate(chips):
                got = outs[nb].at[2 * _chip_no(chip) + 1 - c]
                _rcopy(got, got, send.at[nb + 4 + j], recv.at[nb + 4 + j], me).wait_recv()
        for cp in first_copies(outs, send, recv) + passed:
            cp.wait_send()

    operands = list(big) + ([small] if small is not None else [])
    out_shapes = [jax.ShapeDtypeStruct(a.shape, a.dtype) for a in operands]
    aliases = {i: i for i in range(len(operands))}
    if tiny is not None:
        operands.append(tiny)
        out_shapes.append(jax.ShapeDtypeStruct((N_DEV, *tiny.shape), tiny.dtype))
    return _Hook(operands, out_shapes, aliases, n_sems, start, finish)


def _row_tile(rows, cols, target_bytes=2 * 1024 * 1024):
    best = SUBLANES
    for t in range(SUBLANES, rows + 1, SUBLANES):
        if rows % t == 0 and t * cols * 4 <= target_bytes:
            best = t
    return best


def _add_own_half(buf, got, owners, c, wire):
    _, rows, cols = buf.shape
    tr = _row_tile(rows, cols)

    def body(s_ref, a_ref, b_ref, o_ref):
        o_ref[...] = (a_ref[...] + b_ref[...]).astype(wire)

    return _pcall(
        body,
        name=f"add_own_half_{rows}x{cols}",
        grid_spec=pltpu.PrefetchScalarGridSpec(
            num_scalar_prefetch=1,
            grid=(2, rows // tr),
            in_specs=[
                pl.BlockSpec((None, None, tr, cols), lambda j, r, s_ref: (s_ref[j], s_ref[2], r, 0)),
                pl.BlockSpec((None, tr, cols), lambda j, r, s_ref: (s_ref[j], r, 0)),
            ],
            out_specs=pl.BlockSpec((None, tr, cols), lambda j, r, s_ref: (j, r, 0)),
        ),
        out_shape=jax.ShapeDtypeStruct((2, rows, cols), wire),
        compiler_params=_cparams(("arbitrary", "arbitrary")),
    )(_scalars(owners[0], owners[1], c), buf.reshape(N_CHIPS, 2, rows, cols), got)


def _add_for_neighbour(buf, got_a, got1, second, c, wire):
    _, rows, cols = buf.shape
    tr = _row_tile(rows, cols)

    def body(s_ref, x_ref, a_ref, g_ref, o_ref):
        o_ref[...] = ((x_ref[...] + a_ref[...]) + g_ref[...].astype(F32)).astype(wire)

    return _pcall(
        body,
        name=f"add_for_neighbour_{rows}x{cols}",
        grid_spec=pltpu.PrefetchScalarGridSpec(
            num_scalar_prefetch=1,
            grid=(rows // tr,),
            in_specs=[
                pl.BlockSpec((None, None, tr, cols), lambda r, s_ref: (s_ref[0], s_ref[1], r, 0)),
                pl.BlockSpec((None, tr, cols), lambda r, s_ref: (s_ref[0], r, 0)),
                pl.BlockSpec((None, tr, cols), lambda r, s_ref: (1, r, 0)),
            ],
            out_specs=pl.BlockSpec((tr, cols), lambda r, s_ref: (r, 0)),
        ),
        out_shape=jax.ShapeDtypeStruct((rows, cols), wire),
        compiler_params=_cparams(("arbitrary",)),
    )(_scalars(second, c), buf.reshape(N_CHIPS, 2, rows, cols), got_a, got1)


def _add_received(buf, got_a, got1, got2, mine, c, slot, n_slots):
    _, rows, cols = buf.shape
    tr = _row_tile(rows, cols)

    def body(s_ref, x_ref, a_ref, g1_ref, g2_ref, o_ref):
        own = x_ref[...] + a_ref[...]
        o_ref[...] = (own + g1_ref[...].astype(F32)) + g2_ref[...].astype(F32)

    return _pcall(
        body,
        name=f"add_received_{rows}x{cols}",
        grid_spec=pltpu.PrefetchScalarGridSpec(
            num_scalar_prefetch=1,
            grid=(rows // tr,),
            in_specs=[
                pl.BlockSpec((None, None, tr, cols), lambda r, s_ref: (s_ref[0], s_ref[1], r, 0)),
                pl.BlockSpec((None, tr, cols), lambda r, s_ref: (s_ref[0], r, 0)),
                pl.BlockSpec((None, tr, cols), lambda r, s_ref: (0, r, 0)),
                pl.BlockSpec((tr, cols), lambda r, s_ref: (r, 0)),
            ],
            out_specs=pl.BlockSpec((None, tr, cols), lambda r, s_ref: (s_ref[2], r, 0)),
        ),
        out_shape=jax.ShapeDtypeStruct((n_slots, rows, cols), F32),
        compiler_params=_cparams(("arbitrary",)),
    )(_scalars(mine, c, slot), buf.reshape(N_CHIPS, 2, rows, cols), got_a, got1, got2)


def _layer_a_fwd(x, nw, win, ln_w, ln_b, wc, bs_t, wout, tm, hook):
    t_rows, d = x.shape
    n_sh, _, s_cols = win.shape
    aw = wout.shape[0]
    gd = aw // A_GROUPS
    tn = 512
    assert s_cols % tn == 0 and aw % tn == 0 and tm % CHUNK == 0

    def body(x_ref, nw_ref, win_ref, lnw_ref, lnb_ref, wc_ref, bst_ref, wout_ref, z_ref, x1_ref, h_ref, u_s, v_s, y_s):
        x = x_ref[...]
        h, _, _ = _rms_fwd(x, nw_ref[...])
        h = h.astype(BF16)
        h_ref[...] = h
        for j in range(3 * aw // tn):
            k, off = divmod(j * tn, s_cols)
            cols = slice((j * tn) % aw, (j * tn) % aw + tn)
            zj = _dot(h, win_ref[k, :, off : off + tn])
            z_ref[:, j * tn : (j + 1) * tn] = zj
            if j * tn < aw:
                u_s[:, cols] = _gelu(zj)
            elif j * tn < 2 * aw:
                v_s[:, cols] = _gelu(zj)
            else:
                u_s[:, cols] = u_s[:, cols] * (zj * _sigmoid(zj))
        v = v_s[...]
        mu = jnp.mean(v, axis=-1, keepdims=True)
        vc = v - mu
        rstd = lax.rsqrt(jnp.mean(vc * vc, axis=-1, keepdims=True) + LN_EPS)
        v_s[...] = (vc * rstd) * lnw_ref[...] + lnb_ref[...]
        for ck in range(tm // CHUNK):
            rows = slice(ck * CHUNK, (ck + 1) * CHUNK)
            for g in range(A_GROUPS):
                cols = slice(g * gd, (g + 1) * gd)
                s = _dot(wc_ref[g], v_s[rows, cols].astype(BF16)) + bst_ref[:, g : g + 1]
                y_s[rows, cols] = (u_s[rows, cols] * s).astype(BF16)
        x1_ref[...] = x + _dot(y_s[...], wout_ref[...])

    row = lambda i: (i, 0)
    return _pcall(
        body,
        hook,
        name="layer_a_fwd",
        grid=(t_rows // tm,),
        in_specs=[
            pl.BlockSpec((tm, d), row),
            _full(nw.shape),
            _full(win.shape),
            _full(ln_w.shape),
            _full(ln_b.shape),
            _full(wc.shape),
            _full(bs_t.shape),
            _full(wout.shape),
        ],
        out_specs=[pl.BlockSpec((tm, 3 * aw), row), pl.BlockSpec((tm, d), row), pl.BlockSpec((tm, d), row)],
        out_shape=[
            jax.ShapeDtypeStruct((t_rows, 3 * aw), F32),
            jax.ShapeDtypeStruct((t_rows, d), F32),
            jax.ShapeDtypeStruct((t_rows, d), BF16),
        ],
        scratch_shapes=[pltpu.VMEM((tm, aw), F32), pltpu.VMEM((tm, aw), F32), pltpu.VMEM((tm, aw), BF16)],
        compiler_params=_cparams(("arbitrary",)),
    )(x, nw, win, ln_w, ln_b, wc, bs_t, wout)


def _layer_a_bwd(dout, z, ln_w, ln_b, wc, wct, bs_t, wout, tiles, earlier, hook):
    t_rows, d = dout.shape
    aw = wout.shape[0]
    gd = aw // A_GROUPS
    tm = TM_A_BWD
    lo, hi = tiles
    n_earlier = 0 if earlier is None else len(earlier)

    def body(dout_ref, z_ref, lnw_ref, lnb_ref, wc_ref, wct_ref, bst_ref, wout_ref, *rest):
        dz_ref, y_ref, dob_ref, gws_ref, gbs_ref, glnw_ref, glnb_ref, u_s, vh_s, ds_s, dvn_s = rest[n_earlier:]

        @pl.when(pl.program_id(0) == 0)
        def _():
            gws_ref[...] = jnp.zeros_like(gws_ref)
            gbs_ref[...] = jnp.zeros_like(gbs_ref)
            glnw_ref[...] = jnp.zeros_like(glnw_ref)
            glnb_ref[...] = jnp.zeros_like(glnb_ref)

        dob = dout_ref[...].astype(BF16)
        dob_ref[...] = dob
        dy = _dot_nt(dob, wout_ref[...])

        zv = z_ref[:, aw : 2 * aw]
        vg, dvg_dz = _gelu_and_grad(zv)
        mu = jnp.mean(vg, axis=-1, keepdims=True)
        vc = vg - mu
        rstd = lax.rsqrt(jnp.mean(vc * vc, axis=-1, keepdims=True) + LN_EPS)
        vh = vc * rstd
        vh_s[...] = vh
        vn = (vh * lnw_ref[...] + lnb_ref[...]).astype(BF16)

        zu = z_ref[:, 0:aw]
        zg = z_ref[:, 2 * aw : 3 * aw]
        u, du_dz = _gelu_and_grad(zu)
        sg, dsg = _silu_and_grad(zg)
        u_s[...] = u * sg
        tril = lax.broadcasted_iota(jnp.int32, (CHUNK, CHUNK), 0) >= lax.broadcasted_iota(jnp.int32, (CHUNK, CHUNK), 1)
        for ck in range(tm // CHUNK):
            rows = slice(ck * CHUNK, (ck + 1) * CHUNK)
            for g in range(A_GROUPS):
                cols = slice(g * gd, (g + 1) * gd)
                vn_g = vn[rows, cols]
                s = _dot(wc_ref[g], vn_g) + bst_ref[:, g : g + 1]
                usg = u_s[rows, cols]
                dy_g = dy[rows, cols]
                y_ref[rows, cols] = (usg * s).astype(BF16)
                ds = dy_g * usg
                ds_s[rows, cols] = dy_g * s
                gbs_ref[:, g : g + 1] += jnp.sum(ds, axis=-1, keepdims=True)
                dsb = ds.astype(BF16)
                gws_ref[g] += jnp.where(tril, _dot_nt(dsb, vn_g), 0.0)
                dvn_s[rows, cols] = _dot(wct_ref[g], dsb)
        dusg = ds_s[...]
        dz_ref[:, 0:aw] = (dusg * sg * du_dz).astype(BF16)
        dz_ref[:, 2 * aw : 3 * aw] = (dusg * u * dsg).astype(BF16)

        dvn = dvn_s[...]
        vh = vh_s[...]
        glnw_ref[...] += jnp.sum(dvn * vh, axis=0, keepdims=True)
        glnb_ref[...] += jnp.sum(dvn, axis=0, keepdims=True)
        dvh = dvn * lnw_ref[...]
        dvg = rstd * (dvh - jnp.mean(dvh, axis=-1, keepdims=True) - vh * jnp.mean(dvh * vh, axis=-1, keepdims=True))
        dz_ref[:, aw : 2 * aw] = (dvg * dvg_dz).astype(BF16)

    row = lambda i: (i + lo, 0)
    call = _pcall(
        body,
        hook,
        name=f"layer_a_bwd_{lo}",
        grid=(hi - lo,),
        in_specs=[
            pl.BlockSpec((tm, d), row),
            pl.BlockSpec((tm, 3 * aw), row),
            _full(ln_w.shape),
            _full(ln_b.shape),
            _full(wc.shape),
            _full(wct.shape),
            _full(bs_t.shape),
            _full(wout.shape),
        ]
        + [ANY] * n_earlier,
        out_specs=[
            pl.BlockSpec((tm, 3 * aw), row),
            pl.BlockSpec((tm, aw), row),
            pl.BlockSpec((tm, d), row),
            _full((A_GROUPS, CHUNK, CHUNK)),
            _full((CHUNK, A_GROUPS)),
            _full((1, aw)),
            _full((1, aw)),
        ],
        out_shape=[
            jax.ShapeDtypeStruct((t_rows, 3 * aw), BF16),
            jax.ShapeDtypeStruct((t_rows, aw), BF16),
            jax.ShapeDtypeStruct((t_rows, d), BF16),
            jax.ShapeDtypeStruct((A_GROUPS, CHUNK, CHUNK), F32),
            jax.ShapeDtypeStruct((CHUNK, A_GROUPS), F32),
            jax.ShapeDtypeStruct((1, aw), F32),
            jax.ShapeDtypeStruct((1, aw), F32),
        ],
        scratch_shapes=[pltpu.VMEM((tm, aw), F32)] * 4,
        input_output_aliases={8 + i: i for i in range(n_earlier)},
        compiler_params=_cparams(("arbitrary",)),
    )
    return call(dout, z, ln_w, ln_b, wc, wct, bs_t, wout, *(earlier or ()))


def _layer_a_bwd_dx(dout, x, dz, nw, win, tm, hook):
    t_rows, d = x.shape
    n_sh, _, s_cols = win.shape

    def body(dout_ref, x_ref, dz_ref, nw_ref, win_ref, gx_ref, gnw_ref):
        @pl.when(pl.program_id(0) == 0)
        def _():
            gnw_ref[...] = jnp.zeros_like(gnw_ref)

        dh = jnp.zeros((tm, d), F32)
        for k in range(n_sh):
            dh = dh + _dot_nt(dz_ref[:, k * s_cols : (k + 1) * s_cols], win_ref[k])
        nw = nw_ref[...]
        _, xh, r = _rms_fwd(x_ref[...], nw)
        dx, gnw = _rms_bwd(dh, xh, r, nw)
        gnw_ref[0:1, :] += gnw
        gx_ref[...] = dout_ref[...] + dx

    row = lambda i: (i, 0)
    return _pcall(
        body,
        hook,
        name="layer_a_bwd_dx",
        grid=(t_rows // tm,),
        in_specs=[
            pl.BlockSpec((tm, d), row),
            pl.BlockSpec((tm, d), row),
            pl.BlockSpec((tm, n_sh * s_cols), row),
            _full(nw.shape),
            _full(win.shape),
        ],
        out_specs=[pl.BlockSpec((tm, d), row), _full((SUBLANES, d))],
        out_shape=[jax.ShapeDtypeStruct((t_rows, d), F32), jax.ShapeDtypeStruct((SUBLANES, d), F32)],
        compiler_params=_cparams(("arbitrary",)),
    )(dout, x, dz, nw, win)


def _decay(r, sp_h):
    log_a = (-RG_C) * r * sp_h
    a = jnp.exp(log_a)
    mult = jnp.sqrt(jnp.tanh(-log_a) * (a * a + 1.0))
    return a, mult


def _gates(xc_h, gab_ref, gb_ref, sp_h, h, hd):
    pre = _dot(xc_h.astype(BF16), gab_ref[h])
    bw = gb_ref.shape[1] // 2
    r = _sigmoid(pre[:, :hd] + gb_ref[:, h * hd : (h + 1) * hd])
    ig = _sigmoid(pre[:, hd:] + gb_ref[:, bw + h * hd : bw + (h + 1) * hd])
    a, mult = _decay(r, sp_h)
    return r, ig, a, mult


def _conv(xb, halo, cw_ref, cb_ref):
    xc = cb_ref[...] + cw_ref[CONV_WIDTH - 1 : CONV_WIDTH, :] * xb
    for k in range(CONV_WIDTH - 1):
        xc = xc + cw_ref[k : k + 1, :] * _shift_down(xb, halo, CONV_WIDTH - 1 - k)
    return xc


def _layer_b_fwd(x1, nw, bin_w, cw, cb, gab, gb, lam, bout, nf, tgt, tm):
    t_rows, d = x1.shape
    bw = bout.shape[0]
    hd = bw // B_HEADS
    nt = t_rows // tm

    def body(
        x1_ref, nw_ref, bin_ref, cw_ref, cb_ref, gab_ref, gb_ref, lam_ref, bout_ref, nf_ref, tgt_ref,
        z_ref, h_ref, h1_ref, xbt_ref, ht_ref, dx2_ref, loss_ref, gnf_ref,
        tail_s, carry_s, a_s, b_s, hs_s, acc_s,
    ):
        @pl.when(pl.program_id(0) == 0)
        def _():
            tail_s[...] = jnp.zeros_like(tail_s)
            carry_s[...] = jnp.zeros_like(carry_s)
            acc_s[...] = jnp.zeros_like(acc_s)
            gnf_ref[...] = jnp.zeros_like(gnf_ref)

        x1 = x1_ref[...]
        h1, _, _ = _rms_fwd(x1, nw_ref[...])
        h1 = h1.astype(BF16)
        h1_ref[...] = h1
        z = jnp.concatenate([_dot(h1, bin_ref[k]) for k in range(N_CHIPS)], axis=1)
        z_ref[...] = z
        xb = z[:, :bw]
        xc = _conv(xb, tail_s[...], cw_ref, cb_ref)
        tail = xb[tm - SUBLANES :, :]
        tail_s[...] = tail
        xbt_ref[...] = tail
        sp = _softplus_neg(lam_ref[...])
        for h in range(B_HEADS):
            cols = slice(h * hd, (h + 1) * hd)
            xc_h = xc[:, cols]
            _, ig, a, mult = _gates(xc_h, gab_ref, gb_ref, sp[:, cols], h, hd)
            a_s[:, cols] = a
            b_s[:, cols] = mult * (ig * xc_h)
        carry = _scan_blocks(a_s, b_s, hs_s, carry_s[...], tm, reverse=False)
        carry_s[...] = carry
        ht_ref[...] = hs_s[tm - SUBLANES :, :]
        hs = hs_s[...]
        h_ref[...] = hs
        g = z[:, bw:]
        y = (hs * (g * _sigmoid(g))).astype(BF16)
        x2 = x1 + _dot(y, bout_ref[...])

        nf = nf_ref[...]
        o, xh, r = _rms_fwd(x2, nf)
        diff = o - tgt_ref[...]
        acc_s[...] += jnp.sum(diff * diff, axis=0, keepdims=True)
        do = diff * (1.0 / d)
        dx2, gnf = _rms_bwd(do, xh, r, nf)
        gnf_ref[...] += gnf
        dx2_ref[...] = dx2

        @pl.when(pl.program_id(0) == nt - 1)
        def _():
            total = jnp.sum(acc_s[...], axis=-1, keepdims=True) * (0.5 / d)
            loss_ref[...] = jnp.broadcast_to(total, loss_ref.shape)

    row = lambda i: (i, 0)
    return _pcall(
        body,
        name="layer_b_fwd",
        grid=(nt,),
        in_specs=[
            pl.BlockSpec((tm, d), row),
            _full(nw.shape),
            _full(bin_w.shape),
            _full(cw.shape),
            _full(cb.shape),
            _full(gab.shape),
            _full(gb.shape),
            _full(lam.shape),
            _full(bout.shape),
            _full(nf.shape),
            pl.BlockSpec((tm, d), row),
        ],
        out_specs=[
            pl.BlockSpec((tm, 2 * bw), row),
            pl.BlockSpec((tm, bw), row),
            pl.BlockSpec((tm, d), row),
            pl.BlockSpec((None, SUBLANES, bw), lambda i: (i, 0, 0)),
            pl.BlockSpec((None, SUBLANES, bw), lambda i: (i, 0, 0)),
            pl.BlockSpec((tm, d), row),
            _full((1, LANES)),
            _full((1, d)),
        ],
        out_shape=[
            jax.ShapeDtypeStruct((t_rows, 2 * bw), F32),
            jax.ShapeDtypeStruct((t_rows, bw), F32),
            jax.ShapeDtypeStruct((t_rows, d), BF16),
            jax.ShapeDtypeStruct((nt, SUBLANES, bw), F32),
            jax.ShapeDtypeStruct((nt, SUBLANES, bw), F32),
            jax.ShapeDtypeStruct((t_rows, d), F32),
            jax.ShapeDtypeStruct((1, LANES), F32),
            jax.ShapeDtypeStruct((1, d), F32),
        ],
        scratch_shapes=[
            pltpu.VMEM((SUBLANES, bw), F32),
            pltpu.VMEM((SUBLANES, bw), F32),
            pltpu.VMEM((tm, bw), F32),
            pltpu.VMEM((tm, bw), F32),
            pltpu.VMEM((tm, bw), F32),
            pltpu.VMEM((1, d), F32),
        ],
        compiler_params=_cparams(("arbitrary",)),
    )(x1, nw, bin_w, cw, cb, gab, gb, lam, bout, nf, tgt)


def _layer_b_bwd(dout, x1, z, hseq, xb_tails, h_tails, nw, bin_w, cw, cb, gab, gabt, gb, lam, bout, tm):
    t_rows, d = x1.shape
    bw = bout.shape[0]
    hd = bw // B_HEADS
    nt = t_rows // tm

    def body(
        dout_ref, x1_ref, z_ref, h_ref, xbt_ref, ht_ref, nw_ref, bin_ref, cw_ref, cb_ref, gab_ref, gabt_ref,
        gb_ref, lam_ref, bout_ref,
        dx1_ref, dz_ref, y_ref, dob_ref, ggab_ref, ggb_ref, gcw_ref, gcb_ref, glam_ref, gnw_ref,
        gcarry_s, afirst_s, head_s, aup_s, dh_s, gt_s, dxc_s, xc_s, r_s, ig_s,
    ):
        step = pl.program_id(0)
        tile = nt - 1 - step

        @pl.when(step == 0)
        def _():
            for ref in (ggab_ref, ggb_ref, gcw_ref, gcb_ref, glam_ref, gnw_ref, gcarry_s, afirst_s, head_s):
                ref[...] = jnp.zeros_like(ref)

        first_tile = tile == 0
        xb_halo = jnp.where(first_tile, 0.0, xbt_ref[...])
        h_halo = jnp.where(first_tile, 0.0, ht_ref[...])

        dout = dout_ref[...]
        dob = dout.astype(BF16)
        dob_ref[...] = dob
        dy = _dot_nt(dob, bout_ref[...])
        hs = h_ref[...]
        g = z_ref[:, bw:]
        sg, dsg = _silu_and_grad(g)
        y_ref[...] = (hs * sg).astype(BF16)
        dz_ref[:, bw:] = (dy * hs * dsg).astype(BF16)
        dh_s[...] = dy * sg

        xb = z_ref[:, :bw]
        xc = _conv(xb, xb_halo, cw_ref, cb_ref)
        xc_s[...] = xc
        lam = lam_ref[...]
        sp = _softplus_neg(lam)
        for h in range(B_HEADS):
            cols = slice(h * hd, (h + 1) * hd)
            r, ig, a, _ = _gates(xc[:, cols], gab_ref, gb_ref, sp[:, cols], h, hd)
            r_s[:, cols] = r
            ig_s[:, cols] = ig
            aup_s[:, cols] = _shift_up(a, afirst_s[:, cols], 1)
            afirst_s[:, cols] = jnp.broadcast_to(a[0:1, :], (SUBLANES, hd))
        carry = _scan_blocks(aup_s, dh_s, gt_s, gcarry_s[...], tm, reverse=True)
        gcarry_s[...] = carry

        h_prev = _shift_down(hs, h_halo, 1)
        for h in range(B_HEADS):
            cols = slice(h * hd, (h + 1) * hd)
            xc_h = xc_s[:, cols]
            sp_h = sp[:, cols]
            r, ig = r_s[:, cols], ig_s[:, cols]
            a, mult = _decay(r, sp_h)
            gt = gt_s[:, cols]
            da = gt * h_prev[:, cols]
            dmult = gt * (ig * xc_h)
            dig = gt * (mult * xc_h)
            dxc_direct = gt * (mult * ig)
            dla = da * a - dmult * (a * a) / mult
            glam_ref[:, cols] += jnp.sum(dla * r, axis=0, keepdims=True)
            dr = dla * ((-RG_C) * sp_h)
            dpre = jnp.concatenate([dr * r * (1.0 - r), dig * ig * (1.0 - ig)], axis=1)
            ggb_ref[:, cols] += jnp.sum(dpre[:, :hd], axis=0, keepdims=True)
            ggb_ref[:, bw + h * hd : bw + (h + 1) * hd] += jnp.sum(dpre[:, hd:], axis=0, keepdims=True)
            dpb = dpre.astype(BF16)
            ggab_ref[h] += _dot_tn(xc_h.astype(BF16), dpb)
            dxc_s[:, cols] = dxc_direct + _dot(dpb, gabt_ref[h])
        glam_ref[...] = jnp.where(step == nt - 1, glam_ref[...] * (RG_C * _sigmoid(-lam)), glam_ref[...])

        dxc = dxc_s[...]
        gcb_ref[...] += jnp.sum(dxc, axis=0, keepdims=True)
        dxb = cw_ref[CONV_WIDTH - 1 : CONV_WIDTH, :] * dxc
        gcw_ref[CONV_WIDTH - 1 : CONV_WIDTH, :] += jnp.sum(dxc * xb, axis=0, keepdims=True)
        head = head_s[...]
        for k in range(CONV_WIDTH - 1):
            lag = CONV_WIDTH - 1 - k
            dxb = dxb + cw_ref[k : k + 1, :] * _shift_up(dxc, head, lag)
            gcw_ref[k : k + 1, :] += jnp.sum(dxc * _shift_down(xb, xb_halo, lag), axis=0, keepdims=True)
        head_s[...] = dxc[:SUBLANES, :]
        dz_ref[:, :bw] = dxb.astype(BF16)

        s_cols = 2 * bw // N_CHIPS
        dh1 = jnp.zeros((tm, d), F32)
        for k in range(N_CHIPS):
            dh1 = dh1 + _dot_nt(dz_ref[:, k * s_cols : (k + 1) * s_cols], bin_ref[k])
        x1 = x1_ref[...]
        nw = nw_ref[...]
        _, xh, r1 = _rms_fwd(x1, nw)
        dx, gnw = _rms_bwd(dh1, xh, r1, nw)
        gnw_ref[...] += gnw
        dx1_ref[...] = dout + dx

    rev = lambda i: (nt - 1 - i, 0)
    prev = lambda i: (jnp.maximum(nt - 2 - i, 0), 0, 0)
    return _pcall(
        body,
        name="layer_b_bwd",
        grid=(nt,),
        in_specs=[
            pl.BlockSpec((tm, d), rev),
            pl.BlockSpec((tm, d), rev),
            pl.BlockSpec((tm, 2 * bw), rev),
            pl.BlockSpec((tm, bw), rev),
            pl.BlockSpec((None, SUBLANES, bw), prev),
            pl.BlockSpec((None, SUBLANES, bw), prev),
            _full(nw.shape),
            _full(bin_w.shape),
            _full(cw.shape),
            _full(cb.shape),
            _full(gab.shape),
            _full(gabt.shape),
            _full(gb.shape),
            _full(lam.shape),
            _full(bout.shape),
        ],
        out_specs=[
            pl.BlockSpec((tm, d), rev),
            pl.BlockSpec((tm, 2 * bw), rev),
            pl.BlockSpec((tm, bw), rev),
            pl.BlockSpec((tm, d), rev),
            _full((B_HEADS, hd, 2 * hd)),
            _full((1, 2 * bw)),
            _full((SUBLANES, bw)),
            _full((1, bw)),
            _full((1, bw)),
            _full((1, d)),
        ],
        out_shape=[
            jax.ShapeDtypeStruct((t_rows, d), F32),
            jax.ShapeDtypeStruct((t_rows, 2 * bw), BF16),
            jax.ShapeDtypeStruct((t_rows, bw), BF16),
            jax.ShapeDtypeStruct((t_rows, d), BF16),
            jax.ShapeDtypeStruct((B_HEADS, hd, 2 * hd), F32),
            jax.ShapeDtypeStruct((1, 2 * bw), F32),
            jax.ShapeDtypeStruct((SUBLANES, bw), F32),
            jax.ShapeDtypeStruct((1, bw), F32),
            jax.ShapeDtypeStruct((1, bw), F32),
            jax.ShapeDtypeStruct((1, d), F32),
        ],
        scratch_shapes=[pltpu.VMEM((SUBLANES, bw), F32)] * 3 + [pltpu.VMEM((tm, bw), F32)] * 7,
        compiler_params=_cparams(("arbitrary",)),
    )(dout, x1, z, hseq, xb_tails, h_tails, nw, bin_w, cw, cb, gab, gabt, gb, lam, bout)


def _wgrad(a, b, m_blocks, n_blocks, hook=None):
    k, m = a.shape
    n = b.shape[1]
    bm, bn = m // m_blocks, n // n_blocks

    def body(a_ref, b_ref, o_ref):
        o_ref[...] = _dot_tn(a_ref[...], b_ref[...])

    out = _pcall(
        body,
        hook,
        name=f"wgrad_{m}x{n}",
        grid=(n_blocks, m_blocks),
        in_specs=[pl.BlockSpec((k, bm), lambda j, i: (0, i)), pl.BlockSpec((k, bn), lambda j, i: (0, j))],
        out_specs=[pl.BlockSpec((None, None, bm, bn), lambda j, i: (j, i, 0, 0))],
        out_shape=[jax.ShapeDtypeStruct((n_blocks, m_blocks, bm, bn), F32)],
        compiler_params=_cparams(("arbitrary", "arbitrary")),
    )(a, b)
    return out[0] if hook is None else (out[0][0], out[1])


def _adamw_math(w, g, m, v):
    m = ADAM_B1 * m + (1.0 - ADAM_B1) * g
    v = ADAM_B2 * v + (1.0 - ADAM_B2) * (g * g)
    m_hat = m / (1.0 - ADAM_B1**ADAM_STEP)
    v_hat = v / (1.0 - ADAM_B2**ADAM_STEP)
    delta = -ADAM_LR * (m_hat / (jnp.sqrt(v_hat) + ADAM_EPS) + ADAM_WD * w)
    return delta, m, v


def _adamw(w, g, m, v, hook=None):
    rows, cols = w.shape
    tr = _row_tile(rows, cols, 1024 * 1024)

    def body(w_ref, g_ref, m_ref, v_ref, d_ref, mo_ref, vo_ref):
        d_ref[...], mo_ref[...], vo_ref[...] = _adamw_math(w_ref[...], g_ref[...], m_ref[...], v_ref[...])

    spec = pl.BlockSpec((tr, cols), lambda i: (i, 0))
    return _pcall(
        body,
        hook,
        name=f"adamw_{rows}x{cols}",
        grid=(rows // tr,),
        in_specs=[spec] * 4,
        out_specs=[spec] * 3,
        out_shape=[jax.ShapeDtypeStruct((rows, cols), F32)] * 3,
        compiler_params=_cparams(("arbitrary",)),
    )(w, g, m, v)


def _sum_partials(parts):
    def body(p_ref, o_ref):
        total = p_ref[0, 0:1, :]
        for k in range(1, N_DEV):
            total = total + p_ref[k, 0:1, :]
        o_ref[...] = total

    vmem = pl.BlockSpec(memory_space=pltpu.VMEM)
    return _pcall(
        body,
        name="sum_partials",
        in_specs=[vmem],
        out_specs=vmem,
        out_shape=jax.ShapeDtypeStruct((1, parts.shape[2]), F32),
    )(parts)


def _adamw_many(ws, gs, ms, vs, name, hook=None):
    n = len(ws)

    def body(*refs):
        w_refs, g_refs, m_refs, v_refs = (refs[i * n : (i + 1) * n] for i in range(4))
        d_refs, mo_refs, vo_refs = (refs[(4 + i) * n : (5 + i) * n] for i in range(3))
        for i in range(n):
            d_refs[i][...], mo_refs[i][...], vo_refs[i][...] = _adamw_math(
                w_refs[i][...], g_refs[i][...], m_refs[i][...], v_refs[i][...]
            )

    vmem = pl.BlockSpec(memory_space=pltpu.VMEM)
    outs = _pcall(
        body,
        hook,
        name=name,
        in_specs=[vmem] * (4 * n),
        out_specs=[vmem] * (3 * n),
        out_shape=[jax.ShapeDtypeStruct(w.shape, F32) for w in ws] * 3,
        compiler_params=_cparams(),
    )(*ws, *gs, *ms, *vs)
    extra = None
    if hook is not None:
        outs, extra = outs
    return (outs[:n], outs[n : 2 * n], outs[2 * n :]), extra


def _pack_rows(parts, lanes=LANES):
    flat = jnp.concatenate([p.reshape(-1) for p in parts])
    per = N_DEV * SUBLANES * lanes
    total = -(-flat.shape[0] // per) * per
    flat = jnp.pad(flat, (0, total - flat.shape[0]))
    return flat.reshape(N_DEV, total // (N_DEV * lanes), lanes)


def _unpack(flat, shapes):
    out, at = [], 0
    for s in shapes:
        n = 1
        for dim in s:
            n *= dim
        out.append(flat[at : at + n].reshape(s))
        at += n
    return out


def kernel(x, norm_w, a_w_in, a_ln_w, a_ln_b, a_w_s, a_b_s, a_w_out, b_w_in, b_conv_w, b_conv_b, b_gate_a_w, b_gate_a_b, b_gate_x_w, b_gate_x_b, b_lambda, b_w_out, norm_f_w, loss_target, m_norm_w, m_a_w_in, m_a_ln_w, m_a_ln_b, m_a_w_s, m_a_b_s, m_a_w_out, m_b_w_in, m_b_conv_w, m_b_conv_b, m_b_gate_a_w, m_b_gate_a_b, m_b_gate_x_w, m_b_gate_x_b, m_b_lambda, m_b_w_out, m_norm_f_w, v_norm_w, v_a_w_in, v_a_ln_w, v_a_ln_b, v_a_w_s, v_a_b_s, v_a_w_out, v_b_w_in, v_b_conv_w, v_b_conv_b, v_b_gate_a_w, v_b_gate_a_b, v_b_gate_x_w, v_b_gate_x_b, v_b_lambda, v_b_w_out, v_norm_f_w):
    t_rows, d = x.shape[1], x.shape[2]
    aw = a_ln_w.shape[1]
    bw = b_gate_a_w.shape[1] * b_gate_a_w.shape[2]
    hd = bw // B_HEADS
    mine = 2 * lax.axis_index("x") + lax.axis_index("y")
    core = lax.axis_index("c")
    weights = dict(norm_w=norm_w, a_w_in=a_w_in, a_ln_w=a_ln_w, a_ln_b=a_ln_b, a_w_s=a_w_s, a_b_s=a_b_s, a_w_out=a_w_out, b_w_in=b_w_in, b_conv_w=b_conv_w, b_conv_b=b_conv_b, b_gate_a_w=b_gate_a_w, b_gate_a_b=b_gate_a_b, b_gate_x_w=b_gate_x_w, b_gate_x_b=b_gate_x_b, b_lambda=b_lambda, b_w_out=b_w_out, norm_f_w=norm_f_w)
    m_in = dict(norm_w=m_norm_w, a_w_in=m_a_w_in, a_ln_w=m_a_ln_w, a_ln_b=m_a_ln_b, a_w_s=m_a_w_s, a_b_s=m_a_b_s, a_w_out=m_a_w_out, b_w_in=m_b_w_in, b_conv_w=m_b_conv_w, b_conv_b=m_b_conv_b, b_gate_a_w=m_b_gate_a_w, b_gate_a_b=m_b_gate_a_b, b_gate_x_w=m_b_gate_x_w, b_gate_x_b=m_b_gate_x_b, b_lambda=m_b_lambda, b_w_out=m_b_w_out, norm_f_w=m_norm_f_w)
    v_in = dict(norm_w=v_norm_w, a_w_in=v_a_w_in, a_ln_w=v_a_ln_w, a_ln_b=v_a_ln_b, a_w_s=v_a_w_s, a_b_s=v_a_b_s, a_w_out=v_a_w_out, b_w_in=v_b_w_in, b_conv_w=v_b_conv_w, b_conv_b=v_b_conv_b, b_gate_a_w=v_b_gate_a_w, b_gate_a_b=v_b_gate_a_b, b_gate_x_w=v_b_gate_x_w, b_gate_x_b=v_b_gate_x_b, b_lambda=v_b_lambda, b_w_out=v_b_w_out, norm_f_w=v_norm_f_w)

    win_l = _cast_to_segments(a_w_in[0], mine, 256)
    wout_l = _cast_to_segments(a_w_out[0], mine, 256)
    bin_l = _cast_to_segments(b_w_in[0], mine, 256)
    bout_l = _cast_to_segments(b_w_out[0], mine, 192)
    small_l = jnp.concatenate([b_conv_w[0], b_conv_b, b_gate_a_b, b_gate_x_b, b_lambda], axis=0)
    win_g, wout_g, small_g = _run_hook(_gather_hook([win_l, wout_l], small_l), "gather_layer_a")
    win = win_g.reshape(N_CHIPS, d, -1)
    wout = wout_g.reshape(aw, d)

    tril = jnp.tril(jnp.ones((CHUNK, CHUNK), F32))
    wc = (a_w_s[0] * tril[None]).astype(BF16)
    wct = jnp.swapaxes(wc, 1, 2)
    bs_t = a_b_s[0].T
    gab = jnp.concatenate([b_gate_a_w[0], b_gate_x_w[0]], axis=2).astype(BF16)
    gabt = jnp.swapaxes(gab, 1, 2)
    nw0, nw1, nf = norm_w[0:1], norm_w[1:2], norm_f_w.reshape(1, d)

    x0 = x[0]
    (z_a, x1, h0), (bin_g, bout_g) = _layer_a_fwd(
        x0, nw0, win, a_ln_w, a_ln_b, wc, bs_t, wout, TM_FWD, _gather_hook([bin_l, bout_l])
    )
    bin_w = bin_g.reshape(N_CHIPS, d, -1)
    bout = bout_g.reshape(bw, d)
    small_f = jnp.transpose(small_g, (1, 0, 2)).reshape(SUBLANES, bw)
    cw, cb = small_f[0:CONV_WIDTH], small_f[CONV_WIDTH : CONV_WIDTH + 1]
    gb = jnp.concatenate([small_f[5:6], small_f[6:7]], axis=1)
    lam = small_f[7:8]
    z_b, hseq, h1, xb_tails, h_tails, dx2, loss_l, g_nf = _layer_b_fwd(
        x1, nw1, bin_w, cw, cb, gab, gb, lam, bout, nf, loss_target[0], TM_FWD
    )
    dx1, dz_b, y_b, dob_b, g_gab, g_gb, g_cw, g_cb, g_lam, g_nw1 = _layer_b_bwd(
        dx2, x1, z_b, hseq, xb_tails, h_tails, nw1, bin_w, cw, cb, gab, gabt, gb, lam, bout, TM_FWD
    )
    seg = lambda g: g.reshape(N_DEV, -1, g.shape[3])
    x_at, y_at = lax.axis_index("x"), lax.axis_index("y")
    first_no = 2 * (x_at ^ (1 - core)) + (y_at ^ core)
    second_no = 2 * (x_at ^ core) + (y_at ^ (1 - core))
    bf16s = lambda bufs: [BF16] * len(bufs)
    own_half = lambda bufs, got, wires: [
        _add_own_half(b, g, (first_no, N_CHIPS - 1 - mine), core, w) for b, g, w in zip(bufs, got, wires)
    ]
    for_neighbour = lambda bufs, got_a, got1, wires: [
        _add_for_neighbour(b, ga, g1, second_no, core, w) for b, ga, g1, w in zip(bufs, got_a, got1, wires)
    ]
    received = lambda bufs, got_a, got1, got2: [
        _add_received(b, ga, g1, g2, mine, core, core, 2) for b, ga, g1, g2 in zip(bufs, got_a, got1, got2)
    ]

    g_bout = [seg(_wgrad(y_b, dob_b, N_CHIPS, 1))]
    g_bin, swap_o = _wgrad(h1, dz_b, 2, N_CHIPS, _swap_hook(g_bout))
    g_bin = [seg(g_bin)]
    part_o = own_half(g_bout, swap_o, bf16s(g_bout))
    a_args = (z_a, a_ln_w, a_ln_b, wc, wct, bs_t, wout)
    half = t_rows // TM_A_BWD // 2
    first, rode = _layer_a_bwd(
        dx1, *a_args, (0, half), None, _join_hooks(_swap_hook(g_bin), _send_first_hook(part_o))
    )
    swap_i, got1_o = rode[:1], rode[1:]
    part_i = own_half(g_bin, swap_i, bf16s(g_bin))
    mid_o = for_neighbour(g_bout, swap_o, got1_o, bf16s(g_bout))
    second, rode = _layer_a_bwd(
        dx1, *a_args, (half, 2 * half), first[:3], _join_hooks(_send_first_hook(part_i), _send_second_hook(mid_o))
    )
    got1_i, got2_o = rode[:1], rode[1:]
    dz_a, y_a, dob_a = second[:3]
    g_ws, g_bst, g_lnw, g_lnb = (p + q for p, q in zip(first[3:], second[3:]))
    mid_i = for_neighbour(g_bin, swap_i, got1_i, bf16s(g_bin))
    red_o = received(g_bout, swap_o, got1_o, got2_o)
    g_win, rode = _wgrad(h0, dz_a, 2, N_CHIPS, _join_hooks(_send_second_hook(mid_i), _share_hook(red_o)))
    g_win = [seg(g_win)]
    got2_i, gr_bout = rode[:1], rode[1].reshape(b_w_out.shape[1:])
    red_i = received(g_bin, swap_i, got1_i, got2_i)
    g_wout, rode = _wgrad(y_a, dob_a, N_DEV, 1, _join_hooks(_swap_hook(g_win), _share_hook(red_i)))
    g_wout = seg(g_wout)
    swap_w, gr_bin = rode[:1], rode[1].reshape(b_w_in.shape[1:])

    small_shapes = [
        (1, d), (1, aw), (1, aw), (A_GROUPS, CHUNK, CHUNK), (A_GROUPS, CHUNK), (B_HEADS, hd, hd), (B_HEADS, hd, hd),
        (d,), (CONV_WIDTH, bw), (1, bw), (1, bw), (1, bw), (1, bw), (1, 1),
    ]
    small = _pack_rows(
        [
            g_nw1, g_lnw, g_lnb, g_ws, g_bst.T, g_gab[:, :, :hd], g_gab[:, :, hd:],
            g_nf, g_cw[:CONV_WIDTH], g_cb, g_gb[:, :bw], g_gb[:, bw:], g_lam, loss_l[:, :1],
        ]
    )

    g_u, wire_u = [g_wout, small], [BF16, F32]
    part_w = own_half(g_win, swap_w, bf16s(g_win))
    (grad_x, g_nw0_mine), rode = _layer_a_bwd_dx(
        dx1, x0, dz_a, nw0, win, TM_A_DX, _join_hooks(_send_first_hook(part_w), _swap_hook(g_u))
    )
    got1_w, swap_u = rode[:1], rode[1:]
    part_u = own_half(g_u, swap_u, wire_u)
    mid_w = for_neighbour(g_win, swap_w, got1_w, bf16s(g_win))
    b_names = ("b_w_in", "b_w_out")
    b_grads = {"b_w_in": gr_bin, "b_w_out": gr_bout}
    two_d = lambda a: a.reshape(a.shape[-2:])
    bin_out, rode = _adamw(
        two_d(b_w_in), gr_bin, two_d(m_b_w_in), two_d(v_b_w_in),
        _join_hooks(_send_second_hook(mid_w), _send_first_hook(part_u)),
    )
    bout_out = _adamw(two_d(b_w_out), gr_bout, two_d(m_b_w_out), two_d(v_b_w_out))
    b_out = list(zip(bin_out, bout_out))
    got2_w, got1_u = rode[:1], rode[1:]
    mid_u = for_neighbour(g_u, swap_u, got1_u, wire_u)
    red_w = received(g_win, swap_w, got1_w, got2_w)
    rode = _run_hook(_join_hooks(_send_second_hook(mid_u), _share_hook(red_w)), "second_axis_and_share")
    got2_u, gr_win = rode[:2], rode[2].reshape(a_w_in.shape[1:])
    red_wout = received(g_u[:1], swap_u[:1], got1_u[:1], got2_u[:1])
    red_small = _add_received(small, swap_u[1], got1_u[1], got2_u[1], mine, core, 2 * mine + core, N_DEV)
    gr_wout, small_r, g_nw0_all = _run_hook(_share_hook(red_wout, red_small, g_nw0_mine), "share_reduced")
    win_out = _adamw(two_d(a_w_in), gr_win, two_d(m_a_w_in), two_d(v_a_w_in))
    g_nw0 = _sum_partials(g_nw0_all)
    gr_wout = gr_wout.reshape(a_w_out.shape[1:])
    (g_nw1_r, g_a_ln_w, g_a_ln_b, g_a_w_s, g_a_b_s, g_gate_a_w, g_gate_x_w, g_norm_f, gf_cw, gf_cb, gf_gab, gf_gxb,
     gf_lam, loss) = _unpack(small_r.reshape(-1), small_shapes)
    g_norm_w = jnp.concatenate([g_nw0, g_nw1_r], axis=0)
    shard = lambda g: lax.dynamic_slice_in_dim(g, mine * (bw // N_CHIPS), bw // N_CHIPS, axis=1)

    grads = {
        "norm_w": g_norm_w, "a_w_in": gr_win[None], "a_ln_w": g_a_ln_w, "a_ln_b": g_a_ln_b, "a_w_s": g_a_w_s[None],
        "a_b_s": g_a_b_s[None], "a_w_out": gr_wout[None], "b_w_in": gr_bin[None], "b_conv_w": shard(gf_cw)[None],
        "b_conv_b": shard(gf_cb), "b_gate_a_w": g_gate_a_w[None], "b_gate_a_b": shard(gf_gab),
        "b_gate_x_w": g_gate_x_w[None], "b_gate_x_b": shard(gf_gxb), "b_lambda": shard(gf_lam),
        "b_w_out": gr_bout[None], "norm_f_w": g_norm_f,
    }
    names = list(weights)
    delta, new_m, new_v = {}, {}, {}
    delta["a_w_in"], new_m["a_w_in"], new_v["a_w_in"] = win_out
    delta["a_w_out"], new_m["a_w_out"], new_v["a_w_out"] = _adamw(
        two_d(a_w_out), gr_wout, two_d(m_a_w_out), two_d(v_a_w_out)
    )
    small_names = [n for n in names if n not in ("a_w_in", "a_w_out") + b_names]
    at_least_2d = lambda a: a.reshape(1, -1) if a.ndim == 1 else a
    small_out, _ = _adamw_many(
        *[[at_least_2d(src[n]) for n in small_names] for src in (weights, grads, m_in, v_in)], "adamw_small"
    )
    for dst, vals, b_vals in zip((delta, new_m, new_v), small_out, b_out):
        dst.update(zip(small_names, vals))
        dst.update(zip(b_names, b_vals))
    for dst in (delta, new_m, new_v):
        for n in names:
            dst[n] = dst[n].reshape(weights[n].shape)

    return (
        loss.reshape(()),
        grad_x[None],
        *[grads[n] for n in names],
        *[delta[n] for n in names],
        *[new_m[n] for n in names],
        *[new_v[n] for n in names],
    )
```

```python
import jax
import jax.numpy as jnp
from jax import lax
from jax.experimental import pallas as pl
from jax.experimental.pallas import tpu as pltpu

F32 = jnp.float32
BF16 = jnp.bfloat16

RMS_EPS = 1e-6
LN_EPS = 1e-5
RG_C = 8.0
CHUNK = 128
A_GROUPS = 8
B_HEADS = 12
CONV_WIDTH = 4

ADAM_LR = 0.001
ADAM_B1 = 0.9
ADAM_B2 = 0.999
ADAM_EPS = 1e-08
ADAM_WD = 0.01
ADAM_STEP = 10

N_CHIPS = 4
N_DEV = 8
SUBLANES = 8
LANES = 128
V7X_VMEM_BYTES = 64 * 1024 * 1024
VMEM_LIMIT = V7X_VMEM_BYTES * 7 // 8
MESH = pl.DeviceIdType.MESH
ANY = pl.BlockSpec(memory_space=pl.ANY)

TM_FWD = 256
TM_A_BWD = 256
TM_A_DX = 512
SCAN_UNROLL = 4

GELU_C0 = 0.7978845608028654
GELU_C1 = 0.044715


class _Hook:
    def __init__(self, operands, out_shapes, aliases, n_sems, start, finish, middle=None):
        self.operands, self.out_shapes, self.aliases, self.n_sems = operands, out_shapes, aliases, n_sems
        self.start, self.finish, self.middle = start, finish, middle


class _SemView:
    def __init__(self, base, off):
        self.base, self.off = base, off

    @property
    def at(self):
        return self

    def __getitem__(self, k):
        return self.base.at[self.off + k]


def _join_hooks(*hooks):
    if len(hooks) == 1:
        return hooks[0]
    operands, out_shapes, aliases, spans = [], [], {}, []
    n_sems = 0
    for h in hooks:
        aliases.update({len(operands) + i: len(out_shapes) + o for i, o in h.aliases.items()})
        spans.append((len(operands), len(h.operands), len(out_shapes), len(h.out_shapes), n_sems))
        operands += list(h.operands)
        out_shapes += list(h.out_shapes)
        n_sems += h.n_sems

    def each(which):
        def run(ins, outs, send, recv):
            for h, (i0, ni, o0, no, s0) in zip(hooks, spans):
                step = getattr(h, which)
                if step is not None:
                    step(ins[i0 : i0 + ni], outs[o0 : o0 + no], _SemView(send, s0), _SemView(recv, s0))

        return run

    middle = each("middle") if any(h.middle is not None for h in hooks) else None
    return _Hook(operands, out_shapes, aliases, n_sems, each("start"), each("finish"), middle)


def _pcall(body, hook=None, **kw):
    if hook is None:
        return pl.pallas_call(body, **kw)
    n_pre = 0
    if "grid_spec" in kw:
        spec = kw.pop("grid_spec")
        n_pre = spec.num_scalar_prefetch
        kw.update(
            grid=tuple(spec.grid), in_specs=list(spec.in_specs), out_specs=list(spec.out_specs),
            scratch_shapes=list(spec.scratch_shapes),
        )
    n_in, n_out = len(kw["in_specs"]), len(kw["out_shape"])
    hi, ho = len(hook.operands), len(hook.out_shapes)
    grid = kw.get("grid", ())

    def wrapped(*refs):
        pre, refs = refs[:n_pre], refs[n_pre:]
        ins, h_in = refs[:n_in], refs[n_in : n_in + hi]
        outs = refs[n_in + hi : n_in + hi + n_out]
        h_out = refs[n_in + hi + n_out : n_in + hi + n_out + ho]
        scratch = refs[n_in + hi + n_out + ho : -2]
        send_sems, recv_sems = refs[-2:]
        if not grid:
            hook.start(h_in, h_out, send_sems, recv_sems)
            if hook.middle is not None:
                hook.middle(h_in, h_out, send_sems, recv_sems)
            body(*pre, *ins, *outs, *scratch)
            hook.finish(h_in, h_out, send_sems, recv_sems)
            return
        first = pl.program_id(0) == 0
        last = pl.program_id(0) == grid[0] - 1
        for axis in range(1, len(grid)):
            first = jnp.logical_and(first, pl.program_id(axis) == 0)
            last = jnp.logical_and(last, pl.program_id(axis) == grid[axis] - 1)

        @pl.when(first)
        def _():
            hook.start(h_in, h_out, send_sems, recv_sems)

        if hook.middle is not None:
            assert len(grid) == 1 and grid[0] >= 4
            half_way = pl.program_id(0) == grid[0] * 3 // 8

            @pl.when(half_way)
            def _():
                hook.middle(h_in, h_out, send_sems, recv_sems)

        body(*pre, *ins, *outs, *scratch)

        @pl.when(last)
        def _():
            hook.finish(h_in, h_out, send_sems, recv_sems)

    aliases = dict(kw.pop("input_output_aliases", {}))
    aliases.update({n_pre + n_in + i: n_out + o for i, o in hook.aliases.items()})
    kw.update(
        in_specs=list(kw["in_specs"]) + [ANY] * hi,
        out_specs=list(kw["out_specs"]) + [ANY] * ho,
        out_shape=list(kw["out_shape"]) + list(hook.out_shapes),
        scratch_shapes=list(kw.get("scratch_shapes", ()))
        + [pltpu.SemaphoreType.DMA((hook.n_sems,)), pltpu.SemaphoreType.DMA((hook.n_sems,))],
        input_output_aliases=aliases,
    )
    if n_pre:
        kw["grid_spec"] = pltpu.PrefetchScalarGridSpec(
            num_scalar_prefetch=n_pre, grid=kw.pop("grid"), in_specs=kw.pop("in_specs"),
            out_specs=kw.pop("out_specs"), scratch_shapes=kw.pop("scratch_shapes"),
        )
    call = pl.pallas_call(wrapped, **kw)

    def run(*operands):
        outs = call(*operands, *hook.operands)
        return outs[:n_out], outs[n_out:]

    return run


def _run_hook(hook, name):
    def body():
        pass

    return _pcall(body, hook, name=name, in_specs=[], out_specs=[], out_shape=[])()[1]


def _cparams(sem=None):
    return pltpu.CompilerParams(dimension_semantics=sem, vmem_limit_bytes=VMEM_LIMIT)


def _full(shape):
    zeros = (0,) * len(shape)
    return pl.BlockSpec(shape, lambda *_: zeros)


def _scalars(*vals):
    return jnp.stack([jnp.asarray(v, jnp.int32) for v in vals])


def _sigmoid(x):
    return 1.0 / (1.0 + jnp.exp(-x))


def _gelu(x):
    t = jnp.tanh(GELU_C0 * (x + GELU_C1 * (x * x * x)))
    return x * (0.5 * (1.0 + t))


def _gelu_and_grad(x):
    x2 = x * x
    t = jnp.tanh(GELU_C0 * (x + GELU_C1 * (x2 * x)))
    cdf = 0.5 * (1.0 + t)
    return x * cdf, cdf + 0.5 * x * (1.0 - t * t) * (GELU_C0 * (1.0 + 3.0 * GELU_C1 * x2))


def _silu_and_grad(x):
    s = _sigmoid(x)
    return x * s, s * (1.0 + x * (1.0 - s))


def _softplus_neg(lam):
    u = jnp.exp(-jnp.abs(lam))
    w = 1.0 + u
    log1p = jnp.where(w == 1.0, u, jnp.log(w) * (u / jnp.where(w == 1.0, 1.0, w - 1.0)))
    return jnp.maximum(-lam, 0.0) + log1p


def _dot(a, b):
    return jnp.dot(a, b, preferred_element_type=F32)


def _dot_nt(a, b):
    return lax.dot_general(a, b, (((1,), (1,)), ((), ())), preferred_element_type=F32)


def _dot_tn(a, b):
    return lax.dot_general(a, b, (((0,), (0,)), ((), ())), preferred_element_type=F32)


def _shift_down(v, halo, k):
    if k == 0:
        return v
    rolled = pltpu.roll(v, k, 0)
    row = lax.broadcasted_iota(jnp.int32, (SUBLANES, v.shape[1]), 0)
    top = jnp.where(row < k, pltpu.roll(halo, k, 0), rolled[:SUBLANES])
    return jnp.concatenate([top, rolled[SUBLANES:]], axis=0)


def _shift_up(v, head, k):
    if k == 0:
        return v
    n = v.shape[0]
    rolled = pltpu.roll(v, n - k, 0)
    row = lax.broadcasted_iota(jnp.int32, (SUBLANES, v.shape[1]), 0)
    bot = jnp.where(row >= SUBLANES - k, pltpu.roll(head, SUBLANES - k, 0), rolled[n - SUBLANES :])
    return jnp.concatenate([rolled[: n - SUBLANES], bot], axis=0)


def _scan_blocks(a_ref, b_ref, out_ref, carry, n_rows, reverse):
    width = a_ref.shape[1]
    row = lax.broadcasted_iota(jnp.int32, (SUBLANES, width), 0)
    n_blocks = n_rows // SUBLANES

    def block(j, carry):
        i = (n_blocks - 1 - j) if reverse else j
        r0 = pl.multiple_of(i * SUBLANES, SUBLANES)
        a = a_ref[pl.ds(r0, SUBLANES), :]
        b = b_ref[pl.ds(r0, SUBLANES), :]
        for d in (1, 2, 4):
            shift = (SUBLANES - d) if reverse else d
            keep = (row < SUBLANES - d) if reverse else (row >= d)
            a_s = pltpu.roll(a, shift, 0)
            b_s = pltpu.roll(b, shift, 0)
            b = jnp.where(keep, a * b_s + b, b)
            a = jnp.where(keep, a * a_s, a)
        h = a * carry + b
        out_ref[pl.ds(r0, SUBLANES), :] = h
        edge = h[0:1, :] if reverse else h[SUBLANES - 1 : SUBLANES, :]
        return jnp.broadcast_to(edge, (SUBLANES, width))

    return lax.fori_loop(0, n_blocks, block, carry, unroll=SCAN_UNROLL)


def _rms_fwd(x, w):
    r = lax.rsqrt(jnp.mean(x * x, axis=-1, keepdims=True) + RMS_EPS)
    xh = x * r
    return xh * w, xh, r


def _rms_bwd(dh, xh, r, w):
    dxh = dh * w
    dx = r * (dxh - xh * jnp.mean(dxh * xh, axis=-1, keepdims=True))
    return dx, jnp.sum(dh * xh, axis=0, keepdims=True)


def _cast_to_segments(w, mine, rows, hook=None):
    n, c = w.shape
    per = n // 2 // rows

    def body(k_ref, w_ref, o_ref):
        o_ref[...] = w_ref[...].astype(BF16)

    out = _pcall(
        body,
        hook,
        name=f"cast_{n}x{c}",
        grid_spec=pltpu.PrefetchScalarGridSpec(
            num_scalar_prefetch=1,
            grid=(n // rows,),
            in_specs=[pl.BlockSpec((rows, c), lambda i, k_ref: (i, 0))],
            out_specs=[pl.BlockSpec((None, rows, c), lambda i, k_ref: (2 * k_ref[0] + i // per, i % per, 0))],
        ),
        out_shape=[jax.ShapeDtypeStruct((N_DEV, n // 2, c), BF16)],
        compiler_params=_cparams(("arbitrary",)),
    )(_scalars(mine), w)
    return out[0] if hook is None else (out[0][0], out[1])


def _place():
    x, y, c = lax.axis_index("x"), lax.axis_index("y"), lax.axis_index("c")
    chips = [(1 - x, y), (x, 1 - y), (1 - x, 1 - y)]
    return x, y, c, chips


def _chip_no(chip):
    return 2 * chip[0] + chip[1]


def _rcopy(src, dst, send_sem, recv_sem, to):
    return pltpu.make_async_remote_copy(
        src_ref=src, dst_ref=dst, send_sem=send_sem, recv_sem=recv_sem, device_id=to, device_id_type=MESH
    )


def _gather_hook(big, small=None):
    nb = len(big)
    n_sems = 6 * nb + 4

    def places():
        x, y, c, chips = _place()
        first = (x ^ (1 - c), y ^ c)
        second = (x ^ c, y ^ (1 - c))
        return x, y, c, chips, first, second, (1 - x, 1 - y)

    def seg(outs, b, chip, half):
        return outs[b].at[2 * _chip_no(chip) + half]

    def step1(outs, send, recv):
        x, y, c, _, first, _, _ = places()
        return [
            _rcopy(seg(outs, b, (x, y), c), seg(outs, b, (x, y), c), send.at[6 * b], recv.at[6 * b], (*first, c))
            for b in range(nb)
        ]

    def step2(outs, send, recv):
        x, y, c, _, first, second, _ = places()
        copies = []
        for b in range(nb):
            for k, chip in ((1, (x, y)), (2, first)):
                src = seg(outs, b, chip, c)
                copies.append(_rcopy(src, src, send.at[6 * b + k], recv.at[6 * b + k], (*second, c)))
        return copies

    def hand_over(outs, send, recv, k, chip):
        x, y, c, *_ = places()
        return [
            _rcopy(seg(outs, b, chip, c), seg(outs, b, chip, c), send.at[6 * b + k], recv.at[6 * b + k], (x, y, 1 - c))
            for b in range(nb)
        ]

    def wait_landed(outs, send, recv, k, chip, half):
        x, y, c, *_ = places()
        for b in range(nb):
            got = seg(outs, b, chip, half)
            _rcopy(got, got, send.at[6 * b + k], recv.at[6 * b + k], (x, y, c)).wait_recv()

    def small_copies(ins, outs, send, recv):
        x, y, c, chips, *_ = places()
        there = outs[nb].at[_chip_no((x, y))]
        return [
            _rcopy(ins[nb], there, send.at[6 * nb + j], recv.at[6 * nb + j], (*chip, c)) for j, chip in enumerate(chips)
        ]

    def local_copy(ins, outs, send):
        x, y, _, _ = _place()
        return pltpu.make_async_copy(ins[nb], outs[nb].at[_chip_no((x, y))], send.at[6 * nb + 3])

    def start(ins, outs, send, recv):
        for cp in step1(outs, send, recv):
            cp.start()
        if small is not None:
            for cp in small_copies(ins, outs, send, recv):
                cp.start()
            local_copy(ins, outs, send).start()

    def middle(ins, outs, send, recv):
        *_, first, _, _ = places()
        wait_landed(outs, send, recv, 0, first, places()[2])
        for cp in step2(outs, send, recv) + hand_over(outs, send, recv, 3, first):
            cp.start()

    def finish(ins, outs, send, recv):
        x, y, c, chips, first, second, diagonal = places()
        wait_landed(outs, send, recv, 1, second, c)
        wait_landed(outs, send, recv, 2, diagonal, c)
        late = hand_over(outs, send, recv, 4, second) + hand_over(outs, send, recv, 5, diagonal)
        for cp in late:
            cp.start()
        wait_landed(outs, send, recv, 3, second, 1 - c)
        wait_landed(outs, send, recv, 4, first, 1 - c)
        wait_landed(outs, send, recv, 5, diagonal, 1 - c)
        sent = step1(outs, send, recv) + step2(outs, send, recv) + hand_over(outs, send, recv, 3, first) + late
        for cp in sent:
            cp.wait_send()
        if small is not None:
            for j, chip in enumerate(chips):
                got = outs[nb].at[_chip_no(chip)]
                _rcopy(got, got, send.at[6 * nb + j], recv.at[6 * nb + j], (x, y, c)).wait_recv()
            for cp in small_copies(ins, outs, send, recv):
                cp.wait_send()
            local_copy(ins, outs, send).wait()

    operands = list(big) + ([small] if small is not None else [])
    out_shapes = [jax.ShapeDtypeStruct(b.shape, b.dtype) for b in big]
    if small is not None:
        out_shapes.append(jax.ShapeDtypeStruct((N_CHIPS, *small.shape), small.dtype))
    return _Hook(operands, out_shapes, {b: b for b in range(nb)}, n_sems, start, finish, middle)


def _both_ways_hook(operands, out_shapes, copies_of, n_sems):
    def start(ins, outs, send, recv):
        for cp in copies_of(ins, outs, send, recv):
            cp.start()

    def finish(ins, outs, send, recv):
        for cp in copies_of(ins, outs, send, recv):
            cp.wait()

    return _Hook(operands, out_shapes, {}, n_sems, start, finish)


def _swap_hook(bufs):
    def copies_of(ins, outs, send, recv):
        x, y, c, _ = _place()
        copies = []
        for b in range(len(bufs)):
            for j in range(N_CHIPS):
                k = b * N_CHIPS + j
                copies.append(_rcopy(ins[b].at[2 * j + 1 - c], outs[b].at[j], send.at[k], recv.at[k], (x, y, 1 - c)))
        return copies

    out_shapes = [jax.ShapeDtypeStruct((N_CHIPS, *b.shape[1:]), b.dtype) for b in bufs]
    return _both_ways_hook(list(bufs), out_shapes, copies_of, len(bufs) * N_CHIPS)


def _axis_order():
    x, y, c, _ = _place()
    return (x, y), c, (x ^ (1 - c), y ^ c), (x ^ c, y ^ (1 - c)), (1 - x, 1 - y)


def _send_first_hook(parts):
    def copies_of(ins, outs, send, recv):
        _, c, first, _, _ = _axis_order()
        copies = []
        for b in range(len(parts)):
            for k in range(2):
                sem = 2 * b + k
                copies.append(_rcopy(ins[b].at[k], outs[b].at[k], send.at[sem], recv.at[sem], (*first, c)))
        return copies

    out_shapes = [jax.ShapeDtypeStruct((2, *p.shape[1:]), p.dtype) for p in parts]
    return _both_ways_hook(list(parts), out_shapes, copies_of, len(parts) * 2)


def _send_second_hook(mids):
    def copies_of(ins, outs, send, recv):
        _, c, _, second, _ = _axis_order()
        return [_rcopy(ins[b], outs[b], send.at[b], recv.at[b], (*second, c)) for b in range(len(mids))]

    out_shapes = [jax.ShapeDtypeStruct(m.shape, m.dtype) for m in mids]
    return _both_ways_hook(list(mids), out_shapes, copies_of, len(mids))


def _share_hook(big, small=None, tiny=None):
    nb = len(big)
    n_sems = nb + 7 + N_DEV
    t0 = nb + 7

    def tiny_copies(ins, outs, send, recv):
        x, y, c, _ = _place()
        there = outs[-1].at[2 * _chip_no((x, y)) + c]
        copies = []
        for r in range(1, N_DEV):
            to = (x ^ (r >> 2 & 1), y ^ (r >> 1 & 1), c ^ (r & 1))
            copies.append(_rcopy(ins[-1], there, send.at[t0 + r], recv.at[t0 + r], to))
        return copies

    def tiny_local(ins, outs, send):
        x, y, c, _ = _place()
        return pltpu.make_async_copy(ins[-1], outs[-1].at[2 * _chip_no((x, y)) + c], send.at[t0])

    def first_copies(outs, send, recv):
        x, y, c, chips = _place()
        sibling = (x, y, 1 - c)
        copies = [_rcopy(outs[b].at[c], outs[b].at[c], send.at[b], recv.at[b], sibling) for b in range(nb)]
        if small is not None:
            own = outs[nb].at[2 * _chip_no((x, y)) + c]
            copies.append(_rcopy(own, own, send.at[nb], recv.at[nb], sibling))
            for j, chip in enumerate(chips):
                copies.append(_rcopy(own, own, send.at[nb + 1 + j], recv.at[nb + 1 + j], (*chip, c)))
        return copies

    def start(ins, outs, send, recv):
        for cp in first_copies(outs, send, recv):
            cp.start()
        if tiny is not None:
            for cp in tiny_copies(ins, outs, send, recv):
                cp.start()
            tiny_local(ins, outs, send).start()

    def finish(ins, outs, send, recv):
        x, y, c, chips = _place()
        me, sibling = (x, y, c), (x, y, 1 - c)
        if tiny is not None:
            for cp in tiny_copies(ins, outs, send, recv):
                cp.wait()
            tiny_local(ins, outs, send).wait()
        passed = []
        if small is not None:
            for j, chip in enumerate(chips):
                got = outs[nb].at[2 * _chip_no(chip) + c]
                _rcopy(got, got, send.at[nb + 1 + j], recv.at[nb + 1 + j], me).wait_recv()
                fwd = _rcopy(got, got, send.at[nb + 4 + j], recv.at[nb + 4 + j], sibling)
                fwd.start()
                passed.append(fwd)
        for b in range(nb):
            got = outs[b].at[1 - c]
            _rcopy(got, got, send.at[b], recv.at[b], me).wait_recv()
        if small is not None:
            got = outs[nb].at[2 * _chip_no((x, y)) + 1 - c]
            _rcopy(got, got, send.at[nb], recv.at[nb], me).wait_recv()
            for j, chip in enumerate(chips):
                got = outs[nb].at[2 * _chip_no(chip) + 1 - c]
                _rcopy(got, got, send.at[nb + 4 + j], recv.at[nb + 4 + j], me).wait_recv()
        for cp in first_copies(outs, send, recv) + passed:
            cp.wait_send()

    operands = list(big) + ([small] if small is not None else [])
    out_shapes = [jax.ShapeDtypeStruct(a.shape, a.dtype) for a in operands]
    aliases = {i: i for i in range(len(operands))}
    if tiny is not None:
        operands.append(tiny)
        out_shapes.append(jax.ShapeDtypeStruct((N_DEV, *tiny.shape), tiny.dtype))
    return _Hook(operands, out_shapes, aliases, n_sems, start, finish)


def _row_tile(rows, cols, target_bytes=2 * 1024 * 1024):
    best = SUBLANES
    for t in range(SUBLANES, rows + 1, SUBLANES):
        if rows % t == 0 and t * cols * 4 <= target_bytes:
            best = t
    return best


def _add_own_half(buf, got, owners, c, wire):
    _, rows, cols = buf.shape
    tr = _row_tile(rows, cols)

    def body(s_ref, a_ref, b_ref, o_ref):
        o_ref[...] = (a_ref[...] + b_ref[...]).astype(wire)

    return _pcall(
        body,
        name=f"add_own_half_{rows}x{cols}",
        grid_spec=pltpu.PrefetchScalarGridSpec(
            num_scalar_prefetch=1,
            grid=(2, rows // tr),
            in_specs=[
                pl.BlockSpec((None, None, tr, cols), lambda j, r, s_ref: (s_ref[j], s_ref[2], r, 0)),
                pl.BlockSpec((None, tr, cols), lambda j, r, s_ref: (s_ref[j], r, 0)),
            ],
            out_specs=pl.BlockSpec((None, tr, cols), lambda j, r, s_ref: (j, r, 0)),
        ),
        out_shape=jax.ShapeDtypeStruct((2, rows, cols), wire),
        compiler_params=_cparams(("arbitrary", "arbitrary")),
    )(_scalars(owners[0], owners[1], c), buf.reshape(N_CHIPS, 2, rows, cols), got)


def _add_for_neighbour(buf, got_a, got1, second, c, wire):
    _, rows, cols = buf.shape
    tr = _row_tile(rows, cols)

    def body(s_ref, x_ref, a_ref, g_ref, o_ref):
        o_ref[...] = ((x_ref[...] + a_ref[...]) + g_ref[...].astype(F32)).astype(wire)

    return _pcall(
        body,
        name=f"add_for_neighbour_{rows}x{cols}",
        grid_spec=pltpu.PrefetchScalarGridSpec(
            num_scalar_prefetch=1,
            grid=(rows // tr,),
            in_specs=[
                pl.BlockSpec((None, None, tr, cols), lambda r, s_ref: (s_ref[0], s_ref[1], r, 0)),
                pl.BlockSpec((None, tr, cols), lambda r, s_ref: (s_ref[0], r, 0)),
                pl.BlockSpec((None, tr, cols), lambda r, s_ref: (1, r, 0)),
            ],
            out_specs=pl.BlockSpec((tr, cols), lambda r, s_ref: (r, 0)),
        ),
        out_shape=jax.ShapeDtypeStruct((rows, cols), wire),
        compiler_params=_cparams(("arbitrary",)),
    )(_scalars(second, c), buf.reshape(N_CHIPS, 2, rows, cols), got_a, got1)


def _add_received(buf, got_a, got1, got2, mine, c, slot, n_slots):
    _, rows, cols = buf.shape
    tr = _row_tile(rows, cols)

    def body(s_ref, x_ref, a_ref, g1_ref, g2_ref, o_ref):
        own = x_ref[...] + a_ref[...]
        o_ref[...] = (own + g1_ref[...].astype(F32)) + g2_ref[...].astype(F32)

    return _pcall(
        body,
        name=f"add_received_{rows}x{cols}",
        grid_spec=pltpu.PrefetchScalarGridSpec(
            num_scalar_prefetch=1,
            grid=(rows // tr,),
            in_specs=[
                pl.BlockSpec((None, None, tr, cols), lambda r, s_ref: (s_ref[0], s_ref[1], r, 0)),
                pl.BlockSpec((None, tr, cols), lambda r, s_ref: (s_ref[0], r, 0)),
                pl.BlockSpec((None, tr, cols), lambda r, s_ref: (0, r, 0)),
                pl.BlockSpec((tr, cols), lambda r, s_ref: (r, 0)),
            ],
            out_specs=pl.BlockSpec((None, tr, cols), lambda r, s_ref: (s_ref[2], r, 0)),
        ),
        out_shape=jax.ShapeDtypeStruct((n_slots, rows, cols), F32),
        compiler_params=_cparams(("arbitrary",)),
    )(_scalars(mine, c, slot), buf.reshape(N_CHIPS, 2, rows, cols), got_a, got1, got2)


def _layer_a_fwd(x, nw, win, ln_w, ln_b, wc, bs_t, wout, tm, hook):
    t_rows, d = x.shape
    n_sh, _, s_cols = win.shape
    aw = wout.shape[0]
    gd = aw // A_GROUPS
    tn = 512
    assert s_cols % tn == 0 and aw % tn == 0 and tm % CHUNK == 0

    def body(x_ref, nw_ref, win_ref, lnw_ref, lnb_ref, wc_ref, bst_ref, wout_ref, z_ref, x1_ref, h_ref, u_s, v_s, y_s):
        x = x_ref[...]
        h, _, _ = _rms_fwd(x, nw_ref[...])
        h = h.astype(BF16)
        h_ref[...] = h
        for j in range(3 * aw // tn):
            k, off = divmod(j * tn, s_cols)
            cols = slice((j * tn) % aw, (j * tn) % aw + tn)
            zj = _dot(h, win_ref[k, :, off : off + tn])
            z_ref[:, j * tn : (j + 1) * tn] = zj
            if j * tn < aw:
                u_s[:, cols] = _gelu(zj)
            elif j * tn < 2 * aw:
                v_s[:, cols] = _gelu(zj)
            else:
                u_s[:, cols] = u_s[:, cols] * (zj * _sigmoid(zj))
        v = v_s[...]
        mu = jnp.mean(v, axis=-1, keepdims=True)
        vc = v - mu
        rstd = lax.rsqrt(jnp.mean(vc * vc, axis=-1, keepdims=True) + LN_EPS)
        v_s[...] = (vc * rstd) * lnw_ref[...] + lnb_ref[...]
        for ck in range(tm // CHUNK):
            rows = slice(ck * CHUNK, (ck + 1) * CHUNK)
            for g in range(A_GROUPS):
                cols = slice(g * gd, (g + 1) * gd)
                s = _dot(wc_ref[g], v_s[rows, cols].astype(BF16)) + bst_ref[:, g : g + 1]
                y_s[rows, cols] = (u_s[rows, cols] * s).astype(BF16)
        x1_ref[...] = x + _dot(y_s[...], wout_ref[...])

    row = lambda i: (i, 0)
    return _pcall(
        body,
        hook,
        name="layer_a_fwd",
        grid=(t_rows // tm,),
        in_specs=[
            pl.BlockSpec((tm, d), row),
            _full(nw.shape),
            _full(win.shape),
            _full(ln_w.shape),
            _full(ln_b.shape),
            _full(wc.shape),
            _full(bs_t.shape),
            _full(wout.shape),
        ],
        out_specs=[pl.BlockSpec((tm, 3 * aw), row), pl.BlockSpec((tm, d), row), pl.BlockSpec((tm, d), row)],
        out_shape=[
            jax.ShapeDtypeStruct((t_rows, 3 * aw), F32),
            jax.ShapeDtypeStruct((t_rows, d), F32),
            jax.ShapeDtypeStruct((t_rows, d), BF16),
        ],
        scratch_shapes=[pltpu.VMEM((tm, aw), F32), pltpu.VMEM((tm, aw), F32), pltpu.VMEM((tm, aw), BF16)],
        compiler_params=_cparams(("arbitrary",)),
    )(x, nw, win, ln_w, ln_b, wc, bs_t, wout)


def _layer_a_bwd(dout, z, ln_w, ln_b, wc, wct, bs_t, wout, tiles, earlier, hook):
    t_rows, d = dout.shape
    aw = wout.shape[0]
    gd = aw // A_GROUPS
    tm = TM_A_BWD
    lo, hi = tiles
    n_earlier = 0 if earlier is None else len(earlier)

    def body(dout_ref, z_ref, lnw_ref, lnb_ref, wc_ref, wct_ref, bst_ref, wout_ref, *rest):
        dz_ref, y_ref, dob_ref, gws_ref, gbs_ref, glnw_ref, glnb_ref, u_s, vh_s, ds_s, dvn_s = rest[n_earlier:]

        @pl.when(pl.program_id(0) == 0)
        def _():
            gws_ref[...] = jnp.zeros_like(gws_ref)
            gbs_ref[...] = jnp.zeros_like(gbs_ref)
            glnw_ref[...] = jnp.zeros_like(glnw_ref)
            glnb_ref[...] = jnp.zeros_like(glnb_ref)

        dob = dout_ref[...].astype(BF16)
        dob_ref[...] = dob
        dy = _dot_nt(dob, wout_ref[...])

        zv = z_ref[:, aw : 2 * aw]
        vg, dvg_dz = _gelu_and_grad(zv)
        mu = jnp.mean(vg, axis=-1, keepdims=True)
        vc = vg - mu
        rstd = lax.rsqrt(jnp.mean(vc * vc, axis=-1, keepdims=True) + LN_EPS)
        vh = vc * rstd
        vh_s[...] = vh
        vn = (vh * lnw_ref[...] + lnb_ref[...]).astype(BF16)

        zu = z_ref[:, 0:aw]
        zg = z_ref[:, 2 * aw : 3 * aw]
        u, du_dz = _gelu_and_grad(zu)
        sg, dsg = _silu_and_grad(zg)
        u_s[...] = u * sg
        tril = lax.broadcasted_iota(jnp.int32, (CHUNK, CHUNK), 0) >= lax.broadcasted_iota(jnp.int32, (CHUNK, CHUNK), 1)
        for ck in range(tm // CHUNK):
            rows = slice(ck * CHUNK, (ck + 1) * CHUNK)
            for g in range(A_GROUPS):
                cols = slice(g * gd, (g + 1) * gd)
                vn_g = vn[rows, cols]
                s = _dot(wc_ref[g], vn_g) + bst_ref[:, g : g + 1]
                usg = u_s[rows, cols]
                dy_g = dy[rows, cols]
                y_ref[rows, cols] = (usg * s).astype(BF16)
                ds = dy_g * usg
                ds_s[rows, cols] = dy_g * s
                gbs_ref[:, g : g + 1] += jnp.sum(ds, axis=-1, keepdims=True)
                dsb = ds.astype(BF16)
                gws_ref[g] += jnp.where(tril, _dot_nt(dsb, vn_g), 0.0)
                dvn_s[rows, cols] = _dot(wct_ref[g], dsb)
        dusg = ds_s[...]
        dz_ref[:, 0:aw] = (dusg * sg * du_dz).astype(BF16)
        dz_ref[:, 2 * aw : 3 * aw] = (dusg * u * dsg).astype(BF16)

        dvn = dvn_s[...]
        vh = vh_s[...]
        glnw_ref[...] += jnp.sum(dvn * vh, axis=0, keepdims=True)
        glnb_ref[...] += jnp.sum(dvn, axis=0, keepdims=True)
        dvh = dvn * lnw_ref[...]
        dvg = rstd * (dvh - jnp.mean(dvh, axis=-1, keepdims=True) - vh * jnp.mean(dvh * vh, axis=-1, keepdims=True))
        dz_ref[:, aw : 2 * aw] = (dvg * dvg_dz).astype(BF16)

    row = lambda i: (i + lo, 0)
    call = _pcall(
        body,
        hook,
        name=f"layer_a_bwd_{lo}",
        grid=(hi - lo,),
        in_specs=[
            pl.BlockSpec((tm, d), row),
            pl.BlockSpec((tm, 3 * aw), row),
            _full(ln_w.shape),
            _full(ln_b.shape),
            _full(wc.shape),
            _full(wct.shape),
            _full(bs_t.shape),
            _full(wout.shape),
        ]
        + [ANY] * n_earlier,
        out_specs=[
            pl.BlockSpec((tm, 3 * aw), row),
            pl.BlockSpec((tm, aw), row),
            pl.BlockSpec((tm, d), row),
            _full((A_GROUPS, CHUNK, CHUNK)),
            _full((CHUNK, A_GROUPS)),
            _full((1, aw)),
            _full((1, aw)),
        ],
        out_shape=[
            jax.ShapeDtypeStruct((t_rows, 3 * aw), BF16),
            jax.ShapeDtypeStruct((t_rows, aw), BF16),
            jax.ShapeDtypeStruct((t_rows, d), BF16),
            jax.ShapeDtypeStruct((A_GROUPS, CHUNK, CHUNK), F32),
            jax.ShapeDtypeStruct((CHUNK, A_GROUPS), F32),
            jax.ShapeDtypeStruct((1, aw), F32),
            jax.ShapeDtypeStruct((1, aw), F32),
        ],
        scratch_shapes=[pltpu.VMEM((tm, aw), F32)] * 4,
        input_output_aliases={8 + i: i for i in range(n_earlier)},
        compiler_params=_cparams(("arbitrary",)),
    )
    return call(dout, z, ln_w, ln_b, wc, wct, bs_t, wout, *(earlier or ()))


def _layer_a_bwd_dx(dout, x, dz, nw, win, tm, hook):
    t_rows, d = x.shape
    n_sh, _, s_cols = win.shape

    def body(dout_ref, x_ref, dz_ref, nw_ref, win_ref, gx_ref, gnw_ref):
        @pl.when(pl.program_id(0) == 0)
        def _():
            gnw_ref[...] = jnp.zeros_like(gnw_ref)

        dh = jnp.zeros((tm, d), F32)
        for k in range(n_sh):
            dh = dh + _dot_nt(dz_ref[:, k * s_cols : (k + 1) * s_cols], win_ref[k])
        nw = nw_ref[...]
        _, xh, r = _rms_fwd(x_ref[...], nw)
        dx, gnw = _rms_bwd(dh, xh, r, nw)
        gnw_ref[0:1, :] += gnw
        gx_ref[...] = dout_ref[...] + dx

    row = lambda i: (i, 0)
    return _pcall(
        body,
        hook,
        name="layer_a_bwd_dx",
        grid=(t_rows // tm,),
        in_specs=[
            pl.BlockSpec((tm, d), row),
            pl.BlockSpec((tm, d), row),
            pl.BlockSpec((tm, n_sh * s_cols), row),
            _full(nw.shape),
            _full(win.shape),
        ],
        out_specs=[pl.BlockSpec((tm, d), row), _full((SUBLANES, d))],
        out_shape=[jax.ShapeDtypeStruct((t_rows, d), F32), jax.ShapeDtypeStruct((SUBLANES, d), F32)],
        compiler_params=_cparams(("arbitrary",)),
    )(dout, x, dz, nw, win)


def _decay(r, sp_h):
    log_a = (-RG_C) * r * sp_h
    a = jnp.exp(log_a)
    mult = jnp.sqrt(jnp.tanh(-log_a) * (a * a + 1.0))
    return a, mult


def _gates(xc_h, gab_ref, gb_ref, sp_h, h, hd):
    pre = _dot(xc_h.astype(BF16), gab_ref[h])
    bw = gb_ref.shape[1] // 2
    r = _sigmoid(pre[:, :hd] + gb_ref[:, h * hd : (h + 1) * hd])
    ig = _sigmoid(pre[:, hd:] + gb_ref[:, bw + h * hd : bw + (h + 1) * hd])
    a, mult = _decay(r, sp_h)
    return r, ig, a, mult


def _conv(xb, halo, cw_ref, cb_ref):
    xc = cb_ref[...] + cw_ref[CONV_WIDTH - 1 : CONV_WIDTH, :] * xb
    for k in range(CONV_WIDTH - 1):
        xc = xc + cw_ref[k : k + 1, :] * _shift_down(xb, halo, CONV_WIDTH - 1 - k)
    return xc


def _layer_b_fwd(x1, nw, bin_w, cw, cb, gab, gb, lam, bout, nf, tgt, tm):
    t_rows, d = x1.shape
    bw = bout.shape[0]
    hd = bw // B_HEADS
    nt = t_rows // tm

    def body(
        x1_ref, nw_ref, bin_ref, cw_ref, cb_ref, gab_ref, gb_ref, lam_ref, bout_ref, nf_ref, tgt_ref,
        z_ref, h_ref, h1_ref, xbt_ref, ht_ref, dx2_ref, loss_ref, gnf_ref,
        tail_s, carry_s, a_s, b_s, hs_s, acc_s,
    ):
        @pl.when(pl.program_id(0) == 0)
        def _():
            tail_s[...] = jnp.zeros_like(tail_s)
            carry_s[...] = jnp.zeros_like(carry_s)
            acc_s[...] = jnp.zeros_like(acc_s)
            gnf_ref[...] = jnp.zeros_like(gnf_ref)

        x1 = x1_ref[...]
        h1, _, _ = _rms_fwd(x1, nw_ref[...])
        h1 = h1.astype(BF16)
        h1_ref[...] = h1
        z = jnp.concatenate([_dot(h1, bin_ref[k]) for k in range(N_CHIPS)], axis=1)
        z_ref[...] = z
        xb = z[:, :bw]
        xc = _conv(xb, tail_s[...], cw_ref, cb_ref)
        tail = xb[tm - SUBLANES :, :]
        tail_s[...] = tail
        xbt_ref[...] = tail
        sp = _softplus_neg(lam_ref[...])
        for h in range(B_HEADS):
            cols = slice(h * hd, (h + 1) * hd)
            xc_h = xc[:, cols]
            _, ig, a, mult = _gates(xc_h, gab_ref, gb_ref, sp[:, cols], h, hd)
            a_s[:, cols] = a
            b_s[:, cols] = mult * (ig * xc_h)
        carry = _scan_blocks(a_s, b_s, hs_s, carry_s[...], tm, reverse=False)
        carry_s[...] = carry
        ht_ref[...] = hs_s[tm - SUBLANES :, :]
        hs = hs_s[...]
        h_ref[...] = hs
        g = z[:, bw:]
        y = (hs * (g * _sigmoid(g))).astype(BF16)
        x2 = x1 + _dot(y, bout_ref[...])

        nf = nf_ref[...]
        o, xh, r = _rms_fwd(x2, nf)
        diff = o - tgt_ref[...]
        acc_s[...] += jnp.sum(diff * diff, axis=0, keepdims=True)
        do = diff * (1.0 / d)
        dx2, gnf = _rms_bwd(do, xh, r, nf)
        gnf_ref[...] += gnf
        dx2_ref[...] = dx2

        @pl.when(pl.program_id(0) == nt - 1)
        def _():
            total = jnp.sum(acc_s[...], axis=-1, keepdims=True) * (0.5 / d)
            loss_ref[...] = jnp.broadcast_to(total, loss_ref.shape)

    row = lambda i: (i, 0)
    return _pcall(
        body,
        name="layer_b_fwd",
        grid=(nt,),
        in_specs=[
            pl.BlockSpec((tm, d), row),
            _full(nw.shape),
            _full(bin_w.shape),
            _full(cw.shape),
            _full(cb.shape),
            _full(gab.shape),
            _full(gb.shape),
            _full(lam.shape),
            _full(bout.shape),
            _full(nf.shape),
            pl.BlockSpec((tm, d), row),
        ],
        out_specs=[
            pl.BlockSpec((tm, 2 * bw), row),
            pl.BlockSpec((tm, bw), row),
            pl.BlockSpec((tm, d), row),
            pl.BlockSpec((None, SUBLANES, bw), lambda i: (i, 0, 0)),
            pl.BlockSpec((None, SUBLANES, bw), lambda i: (i, 0, 0)),
            pl.BlockSpec((tm, d), row),
            _full((1, LANES)),
            _full((1, d)),
        ],
        out_shape=[
            jax.ShapeDtypeStruct((t_rows, 2 * bw), F32),
            jax.ShapeDtypeStruct((t_rows, bw), F32),
            jax.ShapeDtypeStruct((t_rows, d), BF16),
            jax.ShapeDtypeStruct((nt, SUBLANES, bw), F32),
            jax.ShapeDtypeStruct((nt, SUBLANES, bw), F32),
            jax.ShapeDtypeStruct((t_rows, d), F32),
            jax.ShapeDtypeStruct((1, LANES), F32),
            jax.ShapeDtypeStruct((1, d), F32),
        ],
        scratch_shapes=[
            pltpu.VMEM((SUBLANES, bw), F32),
            pltpu.VMEM((SUBLANES, bw), F32),
            pltpu.VMEM((tm, bw), F32),
            pltpu.VMEM((tm, bw), F32),
            pltpu.VMEM((tm, bw), F32),
            pltpu.VMEM((1, d), F32),
        ],
        compiler_params=_cparams(("arbitrary",)),
    )(x1, nw, bin_w, cw, cb, gab, gb, lam, bout, nf, tgt)


def _layer_b_bwd(dout, x1, z, hseq, xb_tails, h_tails, nw, bin_w, cw, cb, gab, gabt, gb, lam, bout, tm):
    t_rows, d = x1.shape
    bw = bout.shape[0]
    hd = bw // B_HEADS
    nt = t_rows // tm

    def body(
        dout_ref, x1_ref, z_ref, h_ref, xbt_ref, ht_ref, nw_ref, bin_ref, cw_ref, cb_ref, gab_ref, gabt_ref,
        gb_ref, lam_ref, bout_ref,
        dx1_ref, dz_ref, y_ref, dob_ref, ggab_ref, ggb_ref, gcw_ref, gcb_ref, glam_ref, gnw_ref,
        gcarry_s, afirst_s, head_s, aup_s, dh_s, gt_s, dxc_s, xc_s, r_s, ig_s,
    ):
        step = pl.program_id(0)
        tile = nt - 1 - step

        @pl.when(step == 0)
        def _():
            for ref in (ggab_ref, ggb_ref, gcw_ref, gcb_ref, glam_ref, gnw_ref, gcarry_s, afirst_s, head_s):
                ref[...] = jnp.zeros_like(ref)

        first_tile = tile == 0
        xb_halo = jnp.where(first_tile, 0.0, xbt_ref[...])
        h_halo = jnp.where(first_tile, 0.0, ht_ref[...])

        dout = dout_ref[...]
        dob = dout.astype(BF16)
        dob_ref[...] = dob
        dy = _dot_nt(dob, bout_ref[...])
        hs = h_ref[...]
        g = z_ref[:, bw:]
        sg, dsg = _silu_and_grad(g)
        y_ref[...] = (hs * sg).astype(BF16)
        dz_ref[:, bw:] = (dy * hs * dsg).astype(BF16)
        dh_s[...] = dy * sg

        xb = z_ref[:, :bw]
        xc = _conv(xb, xb_halo, cw_ref, cb_ref)
        xc_s[...] = xc
        lam = lam_ref[...]
        sp = _softplus_neg(lam)
        for h in range(B_HEADS):
            cols = slice(h * hd, (h + 1) * hd)
            r, ig, a, _ = _gates(xc[:, cols], gab_ref, gb_ref, sp[:, cols], h, hd)
            r_s[:, cols] = r
            ig_s[:, cols] = ig
            aup_s[:, cols] = _shift_up(a, afirst_s[:, cols], 1)
            afirst_s[:, cols] = jnp.broadcast_to(a[0:1, :], (SUBLANES, hd))
        carry = _scan_blocks(aup_s, dh_s, gt_s, gcarry_s[...], tm, reverse=True)
        gcarry_s[...] = carry

        h_prev = _shift_down(hs, h_halo, 1)
        for h in range(B_HEADS):
            cols = slice(h * hd, (h + 1) * hd)
            xc_h = xc_s[:, cols]
            sp_h = sp[:, cols]
            r, ig = r_s[:, cols], ig_s[:, cols]
            a, mult = _decay(r, sp_h)
            gt = gt_s[:, cols]
            da = gt * h_prev[:, cols]
            dmult = gt * (ig * xc_h)
            dig = gt * (mult * xc_h)
            dxc_direct = gt * (mult * ig)
            dla = da * a - dmult * (a * a) / mult
            glam_ref[:, cols] += jnp.sum(dla * r, axis=0, keepdims=True)
            dr = dla * ((-RG_C) * sp_h)
            dpre = jnp.concatenate([dr * r * (1.0 - r), dig * ig * (1.0 - ig)], axis=1)
            ggb_ref[:, cols] += jnp.sum(dpre[:, :hd], axis=0, keepdims=True)
            ggb_ref[:, bw + h * hd : bw + (h + 1) * hd] += jnp.sum(dpre[:, hd:], axis=0, keepdims=True)
            dpb = dpre.astype(BF16)
            ggab_ref[h] += _dot_tn(xc_h.astype(BF16), dpb)
            dxc_s[:, cols] = dxc_direct + _dot(dpb, gabt_ref[h])
        glam_ref[...] = jnp.where(step == nt - 1, glam_ref[...] * (RG_C * _sigmoid(-lam)), glam_ref[...])

        dxc = dxc_s[...]
        gcb_ref[...] += jnp.sum(dxc, axis=0, keepdims=True)
        dxb = cw_ref[CONV_WIDTH - 1 : CONV_WIDTH, :] * dxc
        gcw_ref[CONV_WIDTH - 1 : CONV_WIDTH, :] += jnp.sum(dxc * xb, axis=0, keepdims=True)
        head = head_s[...]
        for k in range(CONV_WIDTH - 1):
            lag = CONV_WIDTH - 1 - k
            dxb = dxb + cw_ref[k : k + 1, :] * _shift_up(dxc, head, lag)
            gcw_ref[k : k + 1, :] += jnp.sum(dxc * _shift_down(xb, xb_halo, lag), axis=0, keepdims=True)
        head_s[...] = dxc[:SUBLANES, :]
        dz_ref[:, :bw] = dxb.astype(BF16)

        s_cols = 2 * bw // N_CHIPS
        dh1 = jnp.zeros((tm, d), F32)
        for k in range(N_CHIPS):
            dh1 = dh1 + _dot_nt(dz_ref[:, k * s_cols : (k + 1) * s_cols], bin_ref[k])
        x1 = x1_ref[...]
        nw = nw_ref[...]
        _, xh, r1 = _rms_fwd(x1, nw)
        dx, gnw = _rms_bwd(dh1, xh, r1, nw)
        gnw_ref[...] += gnw
        dx1_ref[...] = dout + dx

    rev = lambda i: (nt - 1 - i, 0)
    prev = lambda i: (jnp.maximum(nt - 2 - i, 0), 0, 0)
    return _pcall(
        body,
        name="layer_b_bwd",
        grid=(nt,),
        in_specs=[
            pl.BlockSpec((tm, d), rev),
            pl.BlockSpec((tm, d), rev),
            pl.BlockSpec((tm, 2 * bw), rev),
            pl.BlockSpec((tm, bw), rev),
            pl.BlockSpec((None, SUBLANES, bw), prev),
            pl.BlockSpec((None, SUBLANES, bw), prev),
            _full(nw.shape),
            _full(bin_w.shape),
            _full(cw.shape),
            _full(cb.shape),
            _full(gab.shape),
            _full(gabt.shape),
            _full(gb.shape),
            _full(lam.shape),
            _full(bout.shape),
        ],
        out_specs=[
            pl.BlockSpec((tm, d), rev),
            pl.BlockSpec((tm, 2 * bw), rev),
            pl.BlockSpec((tm, bw), rev),
            pl.BlockSpec((tm, d), rev),
            _full((B_HEADS, hd, 2 * hd)),
            _full((1, 2 * bw)),
            _full((SUBLANES, bw)),
            _full((1, bw)),
            _full((1, bw)),
            _full((1, d)),
        ],
        out_shape=[
            jax.ShapeDtypeStruct((t_rows, d), F32),
            jax.ShapeDtypeStruct((t_rows, 2 * bw), BF16),
            jax.ShapeDtypeStruct((t_rows, bw), BF16),
            jax.ShapeDtypeStruct((t_rows, d), BF16),
            jax.ShapeDtypeStruct((B_HEADS, hd, 2 * hd), F32),
            jax.ShapeDtypeStruct((1, 2 * bw), F32),
            jax.ShapeDtypeStruct((SUBLANES, bw), F32),
            jax.ShapeDtypeStruct((1, bw), F32),
            jax.ShapeDtypeStruct((1, bw), F32),
            jax.ShapeDtypeStruct((1, d), F32),
        ],
        scratch_shapes=[pltpu.VMEM((SUBLANES, bw), F32)] * 3 + [pltpu.VMEM((tm, bw), F32)] * 7,
        compiler_params=_cparams(("arbitrary",)),
    )(dout, x1, z, hseq, xb_tails, h_tails, nw, bin_w, cw, cb, gab, gabt, gb, lam, bout)


def _wgrad(a, b, m_blocks, n_blocks, hook=None):
    k, m = a.shape
    n = b.shape[1]
    bm, bn = m // m_blocks, n // n_blocks

    def body(a_ref, b_ref, o_ref):
        o_ref[...] = _dot_tn(a_ref[...], b_ref[...])

    out = _pcall(
        body,
        hook,
        name=f"wgrad_{m}x{n}",
        grid=(n_blocks, m_blocks),
        in_specs=[pl.BlockSpec((k, bm), lambda j, i: (0, i)), pl.BlockSpec((k, bn), lambda j, i: (0, j))],
        out_specs=[pl.BlockSpec((None, None, bm, bn), lambda j, i: (j, i, 0, 0))],
        out_shape=[jax.ShapeDtypeStruct((n_blocks, m_blocks, bm, bn), F32)],
        compiler_params=_cparams(("arbitrary", "arbitrary")),
    )(a, b)
    return out[0] if hook is None else (out[0][0], out[1])


def _adamw_math(w, g, m, v):
    m = ADAM_B1 * m + (1.0 - ADAM_B1) * g
    v = ADAM_B2 * v + (1.0 - ADAM_B2) * (g * g)
    m_hat = m / (1.0 - ADAM_B1**ADAM_STEP)
    v_hat = v / (1.0 - ADAM_B2**ADAM_STEP)
    delta = -ADAM_LR * (m_hat / (jnp.sqrt(v_hat) + ADAM_EPS) + ADAM_WD * w)
    return delta, m, v


def _adamw(w, g, m, v, hook=None):
    rows, cols = w.shape
    tr = _row_tile(rows, cols, 1024 * 1024)

    def body(w_ref, g_ref, m_ref, v_ref, d_ref, mo_ref, vo_ref):
        d_ref[...], mo_ref[...], vo_ref[...] = _adamw_math(w_ref[...], g_ref[...], m_ref[...], v_ref[...])

    spec = pl.BlockSpec((tr, cols), lambda i: (i, 0))
    return _pcall(
        body,
        hook,
        name=f"adamw_{rows}x{cols}",
        grid=(rows // tr,),
        in_specs=[spec] * 4,
        out_specs=[spec] * 3,
        out_shape=[jax.ShapeDtypeStruct((rows, cols), F32)] * 3,
        compiler_params=_cparams(("arbitrary",)),
    )(w, g, m, v)


def _sum_partials(parts):
    def body(p_ref, o_ref):
        total = p_ref[0, 0:1, :]
        for k in range(1, N_DEV):
            total = total + p_ref[k, 0:1, :]
        o_ref[...] = total

    vmem = pl.BlockSpec(memory_space=pltpu.VMEM)
    return _pcall(
        body,
        name="sum_partials",
        in_specs=[vmem],
        out_specs=vmem,
        out_shape=jax.ShapeDtypeStruct((1, parts.shape[2]), F32),
    )(parts)


def _adamw_many(ws, gs, ms, vs, name, hook=None):
    n = len(ws)

    def body(*refs):
        w_refs, g_refs, m_refs, v_refs = (refs[i * n : (i + 1) * n] for i in range(4))
        d_refs, mo_refs, vo_refs = (refs[(4 + i) * n : (5 + i) * n] for i in range(3))
        for i in range(n):
            d_refs[i][...], mo_refs[i][...], vo_refs[i][...] = _adamw_math(
                w_refs[i][...], g_refs[i][...], m_refs[i][...], v_refs[i][...]
            )

    vmem = pl.BlockSpec(memory_space=pltpu.VMEM)
    outs = _pcall(
        body,
        hook,
        name=name,
        in_specs=[vmem] * (4 * n),
        out_specs=[vmem] * (3 * n),
        out_shape=[jax.ShapeDtypeStruct(w.shape, F32) for w in ws] * 3,
        compiler_params=_cparams(),
    )(*ws, *gs, *ms, *vs)
    extra = None
    if hook is not None:
        outs, extra = outs
    return (outs[:n], outs[n : 2 * n], outs[2 * n :]), extra


def _pack_rows(parts, lanes=LANES):
    flat = jnp.concatenate([p.reshape(-1) for p in parts])
    per = N_DEV * SUBLANES * lanes
    total = -(-flat.shape[0] // per) * per
    flat = jnp.pad(flat, (0, total - flat.shape[0]))
    return flat.reshape(N_DEV, total // (N_DEV * lanes), lanes)


def _unpack(flat, shapes):
    out, at = [], 0
    for s in shapes:
        n = 1
        for dim in s:
            n *= dim
        out.append(flat[at : at + n].reshape(s))
        at += n
    return out


def kernel(x, norm_w, a_w_in, a_ln_w, a_ln_b, a_w_s, a_b_s, a_w_out, b_w_in, b_conv_w, b_conv_b, b_gate_a_w, b_gate_a_b, b_gate_x_w, b_gate_x_b, b_lambda, b_w_out, norm_f_w, loss_target, m_norm_w, m_a_w_in, m_a_ln_w, m_a_ln_b, m_a_w_s, m_a_b_s, m_a_w_out, m_b_w_in, m_b_conv_w, m_b_conv_b, m_b_gate_a_w, m_b_gate_a_b, m_b_gate_x_w, m_b_gate_x_b, m_b_lambda, m_b_w_out, m_norm_f_w, v_norm_w, v_a_w_in, v_a_ln_w, v_a_ln_b, v_a_w_s, v_a_b_s, v_a_w_out, v_b_w_in, v_b_conv_w, v_b_conv_b, v_b_gate_a_w, v_b_gate_a_b, v_b_gate_x_w, v_b_gate_x_b, v_b_lambda, v_b_w_out, v_norm_f_w):
    t_rows, d = x.shape[1], x.shape[2]
    aw = a_ln_w.shape[1]
    bw = b_gate_a_w.shape[1] * b_gate_a_w.shape[2]
    hd = bw // B_HEADS
    mine = 2 * lax.axis_index("x") + lax.axis_index("y")
    core = lax.axis_index("c")
    weights = dict(norm_w=norm_w, a_w_in=a_w_in, a_ln_w=a_ln_w, a_ln_b=a_ln_b, a_w_s=a_w_s, a_b_s=a_b_s, a_w_out=a_w_out, b_w_in=b_w_in, b_conv_w=b_conv_w, b_conv_b=b_conv_b, b_gate_a_w=b_gate_a_w, b_gate_a_b=b_gate_a_b, b_gate_x_w=b_gate_x_w, b_gate_x_b=b_gate_x_b, b_lambda=b_lambda, b_w_out=b_w_out, norm_f_w=norm_f_w)
    m_in = dict(norm_w=m_norm_w, a_w_in=m_a_w_in, a_ln_w=m_a_ln_w, a_ln_b=m_a_ln_b, a_w_s=m_a_w_s, a_b_s=m_a_b_s, a_w_out=m_a_w_out, b_w_in=m_b_w_in, b_conv_w=m_b_conv_w, b_conv_b=m_b_conv_b, b_gate_a_w=m_b_gate_a_w, b_gate_a_b=m_b_gate_a_b, b_gate_x_w=m_b_gate_x_w, b_gate_x_b=m_b_gate_x_b, b_lambda=m_b_lambda, b_w_out=m_b_w_out, norm_f_w=m_norm_f_w)
    v_in = dict(norm_w=v_norm_w, a_w_in=v_a_w_in, a_ln_w=v_a_ln_w, a_ln_b=v_a_ln_b, a_w_s=v_a_w_s, a_b_s=v_a_b_s, a_w_out=v_a_w_out, b_w_in=v_b_w_in, b_conv_w=v_b_conv_w, b_conv_b=v_b_conv_b, b_gate_a_w=v_b_gate_a_w, b_gate_a_b=v_b_gate_a_b, b_gate_x_w=v_b_gate_x_w, b_gate_x_b=v_b_gate_x_b, b_lambda=v_b_lambda, b_w_out=v_b_w_out, norm_f_w=v_norm_f_w)

    win_l = _cast_to_segments(a_w_in[0], mine, 256)
    wout_l = _cast_to_segments(a_w_out[0], mine, 256)
    small_l = jnp.concatenate([b_conv_w[0], b_conv_b, b_gate_a_b, b_gate_x_b, b_lambda], axis=0)
    bin_l, (win_g, wout_g, small_g) = _cast_to_segments(
        b_w_in[0], mine, 128, _gather_hook([win_l, wout_l], small_l)
    )
    bout_l = _cast_to_segments(b_w_out[0], mine, 192)
    win = win_g.reshape(N_CHIPS, d, -1)
    wout = wout_g.reshape(aw, d)

    tril = jnp.tril(jnp.ones((CHUNK, CHUNK), F32))
    wc = (a_w_s[0] * tril[None]).astype(BF16)
    wct = jnp.swapaxes(wc, 1, 2)
    bs_t = a_b_s[0].T
    gab = jnp.concatenate([b_gate_a_w[0], b_gate_x_w[0]], axis=2).astype(BF16)
    gabt = jnp.swapaxes(gab, 1, 2)
    nw0, nw1, nf = norm_w[0:1], norm_w[1:2], norm_f_w.reshape(1, d)

    x0 = x[0]
    (z_a, x1, h0), (bin_g, bout_g) = _layer_a_fwd(
        x0, nw0, win, a_ln_w, a_ln_b, wc, bs_t, wout, TM_FWD, _gather_hook([bin_l, bout_l])
    )
    bin_w = bin_g.reshape(N_CHIPS, d, -1)
    bout = bout_g.reshape(bw, d)
    small_f = jnp.transpose(small_g, (1, 0, 2)).reshape(SUBLANES, bw)
    cw, cb = small_f[0:CONV_WIDTH], small_f[CONV_WIDTH : CONV_WIDTH + 1]
    gb = jnp.concatenate([small_f[5:6], small_f[6:7]], axis=1)
    lam = small_f[7:8]
    z_b, hseq, h1, xb_tails, h_tails, dx2, loss_l, g_nf = _layer_b_fwd(
        x1, nw1, bin_w, cw, cb, gab, gb, lam, bout, nf, loss_target[0], TM_FWD
    )
    dx1, dz_b, y_b, dob_b, g_gab, g_gb, g_cw, g_cb, g_lam, g_nw1 = _layer_b_bwd(
        dx2, x1, z_b, hseq, xb_tails, h_tails, nw1, bin_w, cw, cb, gab, gabt, gb, lam, bout, TM_FWD
    )
    seg = lambda g: g.reshape(N_DEV, -1, g.shape[3])
    x_at, y_at = lax.axis_index("x"), lax.axis_index("y")
    first_no = 2 * (x_at ^ (1 - core)) + (y_at ^ core)
    second_no = 2 * (x_at ^ core) + (y_at ^ (1 - core))
    bf16s = lambda bufs: [BF16] * len(bufs)
    own_half = lambda bufs, got, wires: [
        _add_own_half(b, g, (first_no, N_CHIPS - 1 - mine), core, w) for b, g, w in zip(bufs, got, wires)
    ]
    for_neighbour = lambda bufs, got_a, got1, wires: [
        _add_for_neighbour(b, ga, g1, second_no, core, w) for b, ga, g1, w in zip(bufs, got_a, got1, wires)
    ]
    received = lambda bufs, got_a, got1, got2: [
        _add_received(b, ga, g1, g2, mine, core, core, 2) for b, ga, g1, g2 in zip(bufs, got_a, got1, got2)
    ]

    g_bout = [seg(_wgrad(y_b, dob_b, N_CHIPS, 1))]
    g_bin, swap_o = _wgrad(h1, dz_b, 2, N_CHIPS, _swap_hook(g_bout))
    g_bin = [seg(g_bin)]
    part_o = own_half(g_bout, swap_o, bf16s(g_bout))
    a_args = (z_a, a_ln_w, a_ln_b, wc, wct, bs_t, wout)
    half = t_rows // TM_A_BWD // 2
    first, rode = _layer_a_bwd(
        dx1, *a_args, (0, half), None, _join_hooks(_swap_hook(g_bin), _send_first_hook(part_o))
    )
    swap_i, got1_o = rode[:1], rode[1:]
    part_i = own_half(g_bin, swap_i, bf16s(g_bin))
    mid_o = for_neighbour(g_bout, swap_o, got1_o, bf16s(g_bout))
    second, rode = _layer_a_bwd(
        dx1, *a_args, (half, 2 * half), first[:3], _join_hooks(_send_first_hook(part_i), _send_second_hook(mid_o))
    )
    got1_i, got2_o = rode[:1], rode[1:]
    dz_a, y_a, dob_a = second[:3]
    g_ws, g_bst, g_lnw, g_lnb = (p + q for p, q in zip(first[3:], second[3:]))
    mid_i = for_neighbour(g_bin, swap_i, got1_i, bf16s(g_bin))
    red_o = received(g_bout, swap_o, got1_o, got2_o)
    g_win, rode = _wgrad(h0, dz_a, 2, N_CHIPS, _join_hooks(_send_second_hook(mid_i), _share_hook(red_o)))
    g_win = [seg(g_win)]
    got2_i, gr_bout = rode[:1], rode[1].reshape(b_w_out.shape[1:])
    red_i = received(g_bin, swap_i, got1_i, got2_i)

    small_shapes = [
        (1, d), (1, aw), (1, aw), (A_GROUPS, CHUNK, CHUNK), (A_GROUPS, CHUNK), (B_HEADS, hd, hd), (B_HEADS, hd, hd),
        (d,), (CONV_WIDTH, bw), (1, bw), (1, bw), (1, bw), (1, bw), (1, 1),
    ]
    small = _pack_rows(
        [
            g_nw1, g_lnw, g_lnb, g_ws, g_bst.T, g_gab[:, :, :hd], g_gab[:, :, hd:],
            g_nf, g_cw[:CONV_WIDTH], g_cb, g_gb[:, :bw], g_gb[:, bw:], g_lam, loss_l[:, :1],
        ]
    )
    g_w, wire_w = g_win + [small], [BF16, F32]
    g_wout, rode = _wgrad(y_a, dob_a, N_DEV, 1, _join_hooks(_swap_hook(g_w), _share_hook(red_i)))
    g_wout = seg(g_wout)
    swap_w, gr_bin = rode[:2], rode[2].reshape(b_w_in.shape[1:])

    g_u, wire_u = [g_wout], [BF16]
    part_w = own_half(g_w, swap_w, wire_w)
    (grad_x, g_nw0_mine), rode = _layer_a_bwd_dx(
        dx1, x0, dz_a, nw0, win, TM_A_DX, _join_hooks(_send_first_hook(part_w), _swap_hook(g_u))
    )
    got1_w, swap_u = rode[:2], rode[2:]
    part_u = own_half(g_u, swap_u, wire_u)
    mid_w = for_neighbour(g_w, swap_w, got1_w, wire_w)
    b_names = ("b_w_in", "b_w_out")
    b_grads = {"b_w_in": gr_bin, "b_w_out": gr_bout}
    two_d = lambda a: a.reshape(a.shape[-2:])
    bin_out, rode = _adamw(
        two_d(b_w_in), gr_bin, two_d(m_b_w_in), two_d(v_b_w_in),
        _join_hooks(_send_second_hook(mid_w), _send_first_hook(part_u)),
    )
    bout_out = _adamw(two_d(b_w_out), gr_bout, two_d(m_b_w_out), two_d(v_b_w_out))
    b_out = list(zip(bin_out, bout_out))
    got2_w, got1_u = rode[:2], rode[2:]
    mid_u = for_neighbour(g_u, swap_u, got1_u, wire_u)
    red_w = received(g_win, swap_w[:1], got1_w[:1], got2_w[:1])
    red_small = _add_received(small, swap_w[1], got1_w[1], got2_w[1], mine, core, 2 * mine + core, N_DEV)
    rode = _run_hook(
        _join_hooks(_send_second_hook(mid_u), _share_hook(red_w, red_small)), "second_axis_and_share"
    )
    got2_u, gr_win, small_r = rode[:1], rode[1].reshape(a_w_in.shape[1:]), rode[2]
    red_wout = received(g_u, swap_u, got1_u, got2_u)
    gr_wout, g_nw0_all = _run_hook(_share_hook(red_wout, None, g_nw0_mine), "share_reduced")
    win_out = _adamw(two_d(a_w_in), gr_win, two_d(m_a_w_in), two_d(v_a_w_in))
    g_nw0 = _sum_partials(g_nw0_all)
    gr_wout = gr_wout.reshape(a_w_out.shape[1:])
    (g_nw1_r, g_a_ln_w, g_a_ln_b, g_a_w_s, g_a_b_s, g_gate_a_w, g_gate_x_w, g_norm_f, gf_cw, gf_cb, gf_gab, gf_gxb,
     gf_lam, loss) = _unpack(small_r.reshape(-1), small_shapes)
    g_norm_w = jnp.concatenate([g_nw0, g_nw1_r], axis=0)
    shard = lambda g: lax.dynamic_slice_in_dim(g, mine * (bw // N_CHIPS), bw // N_CHIPS, axis=1)

    grads = {
        "norm_w": g_norm_w, "a_w_in": gr_win[None], "a_ln_w": g_a_ln_w, "a_ln_b": g_a_ln_b, "a_w_s": g_a_w_s[None],
        "a_b_s": g_a_b_s[None], "a_w_out": gr_wout[None], "b_w_in": gr_bin[None], "b_conv_w": shard(gf_cw)[None],
        "b_conv_b": shard(gf_cb), "b_gate_a_w": g_gate_a_w[None], "b_gate_a_b": shard(gf_gab),
        "b_gate_x_w": g_gate_x_w[None], "b_gate_x_b": shard(gf_gxb), "b_lambda": shard(gf_lam),
        "b_w_out": gr_bout[None], "norm_f_w": g_norm_f,
    }
    names = list(weights)
    delta, new_m, new_v = {}, {}, {}
    delta["a_w_in"], new_m["a_w_in"], new_v["a_w_in"] = win_out
    delta["a_w_out"], new_m["a_w_out"], new_v["a_w_out"] = _adamw(
        two_d(a_w_out), gr_wout, two_d(m_a_w_out), two_d(v_a_w_out)
    )
    small_names = [n for n in names if n not in ("a_w_in", "a_w_out") + b_names]
    at_least_2d = lambda a: a.reshape(1, -1) if a.ndim == 1 else a
    small_out, _ = _adamw_many(
        *[[at_least_2d(src[n]) for n in small_names] for src in (weights, grads, m_in, v_in)], "adamw_small"
    )
    for dst, vals, b_vals in zip((delta, new_m, new_v), small_out, b_out):
        dst.update(zip(small_names, vals))
        dst.update(zip(b_names, b_vals))
    for dst in (delta, new_m, new_v):
        for n in names:
            dst[n] = dst[n].reshape(weights[n].shape)

    return (
        loss.reshape(()),
        grad_x[None],
        *[grads[n] for n in names],
        *[delta[n] for n in names],
        *[new_m[n] for n in names],
        *[new_v[n] for n in names],
    )
```

```python
import jax
import jax.numpy as jnp
from jax import lax
from jax.experimental import pallas as pl
from jax.experimental.pallas import tpu as pltpu

F32 = jnp.float32
BF16 = jnp.bfloat16

RMS_EPS = 1e-6
LN_EPS = 1e-5
RG_C = 8.0
CHUNK = 128
A_GROUPS = 8
B_HEADS = 12
CONV_WIDTH = 4

ADAM_LR = 0.001
ADAM_B1 = 0.9
ADAM_B2 = 0.999
ADAM_EPS = 1e-08
ADAM_WD = 0.01
ADAM_STEP = 10

N_CHIPS = 4
N_DEV = 8
SUBLANES = 8
LANES = 128
V7X_VMEM_BYTES = 64 * 1024 * 1024
VMEM_LIMIT = V7X_VMEM_BYTES * 7 // 8
MESH = pl.DeviceIdType.MESH
ANY = pl.BlockSpec(memory_space=pl.ANY)

TM_FWD = 256
TM_A_BWD = 256
TM_A_DX = 512
SCAN_UNROLL = 4

GELU_C0 = 0.7978845608028654
GELU_C1 = 0.044715


class _Hook:
    def __init__(self, operands, out_shapes, aliases, n_sems, start, finish, middle=None, late=None):
        self.operands, self.out_shapes, self.aliases, self.n_sems = operands, out_shapes, aliases, n_sems
        self.start, self.finish, self.middle, self.late = start, finish, middle, late


class _SemView:
    def __init__(self, base, off):
        self.base, self.off = base, off

    @property
    def at(self):
        return self

    def __getitem__(self, k):
        return self.base.at[self.off + k]


def _join_hooks(*hooks):
    if len(hooks) == 1:
        return hooks[0]
    operands, out_shapes, aliases, spans = [], [], {}, []
    n_sems = 0
    for h in hooks:
        aliases.update({len(operands) + i: len(out_shapes) + o for i, o in h.aliases.items()})
        spans.append((len(operands), len(h.operands), len(out_shapes), len(h.out_shapes), n_sems))
        operands += list(h.operands)
        out_shapes += list(h.out_shapes)
        n_sems += h.n_sems

    def each(which):
        def run(ins, outs, send, recv):
            for h, (i0, ni, o0, no, s0) in zip(hooks, spans):
                step = getattr(h, which)
                if step is not None:
                    step(ins[i0 : i0 + ni], outs[o0 : o0 + no], _SemView(send, s0), _SemView(recv, s0))

        return run

    middle = each("middle") if any(h.middle is not None for h in hooks) else None
    late = each("late") if any(h.late is not None for h in hooks) else None
    return _Hook(operands, out_shapes, aliases, n_sems, each("start"), each("finish"), middle, late)


def _pcall(body, hook=None, **kw):
    if hook is None:
        return pl.pallas_call(body, **kw)
    n_pre = 0
    if "grid_spec" in kw:
        spec = kw.pop("grid_spec")
        n_pre = spec.num_scalar_prefetch
        kw.update(
            grid=tuple(spec.grid), in_specs=list(spec.in_specs), out_specs=list(spec.out_specs),
            scratch_shapes=list(spec.scratch_shapes),
        )
    n_in, n_out = len(kw["in_specs"]), len(kw["out_shape"])
    hi, ho = len(hook.operands), len(hook.out_shapes)
    grid = kw.get("grid", ())

    def wrapped(*refs):
        pre, refs = refs[:n_pre], refs[n_pre:]
        ins, h_in = refs[:n_in], refs[n_in : n_in + hi]
        outs = refs[n_in + hi : n_in + hi + n_out]
        h_out = refs[n_in + hi + n_out : n_in + hi + n_out + ho]
        scratch = refs[n_in + hi + n_out + ho : -2]
        send_sems, recv_sems = refs[-2:]
        if not grid:
            hook.start(h_in, h_out, send_sems, recv_sems)
            if hook.middle is not None:
                hook.middle(h_in, h_out, send_sems, recv_sems)
            body(*pre, *ins, *outs, *scratch)
            if hook.late is not None:
                hook.late(h_in, h_out, send_sems, recv_sems)
            hook.finish(h_in, h_out, send_sems, recv_sems)
            return
        first = pl.program_id(0) == 0
        last = pl.program_id(0) == grid[0] - 1
        for axis in range(1, len(grid)):
            first = jnp.logical_and(first, pl.program_id(axis) == 0)
            last = jnp.logical_and(last, pl.program_id(axis) == grid[axis] - 1)

        @pl.when(first)
        def _():
            hook.start(h_in, h_out, send_sems, recv_sems)

        for when, step in ((hook.middle, grid[0] // 4), (hook.late, grid[0] * 3 // 4)):
            if when is not None:
                assert len(grid) == 1 and grid[0] >= 4

                @pl.when(pl.program_id(0) == step)
                def _(when=when):
                    when(h_in, h_out, send_sems, recv_sems)

        body(*pre, *ins, *outs, *scratch)

        @pl.when(last)
        def _():
            hook.finish(h_in, h_out, send_sems, recv_sems)

    aliases = dict(kw.pop("input_output_aliases", {}))
    aliases.update({n_pre + n_in + i: n_out + o for i, o in hook.aliases.items()})
    kw.update(
        in_specs=list(kw["in_specs"]) + [ANY] * hi,
        out_specs=list(kw["out_specs"]) + [ANY] * ho,
        out_shape=list(kw["out_shape"]) + list(hook.out_shapes),
        scratch_shapes=list(kw.get("scratch_shapes", ()))
        + [pltpu.SemaphoreType.DMA((hook.n_sems,)), pltpu.SemaphoreType.DMA((hook.n_sems,))],
        input_output_aliases=aliases,
    )
    if n_pre:
        kw["grid_spec"] = pltpu.PrefetchScalarGridSpec(
            num_scalar_prefetch=n_pre, grid=kw.pop("grid"), in_specs=kw.pop("in_specs"),
            out_specs=kw.pop("out_specs"), scratch_shapes=kw.pop("scratch_shapes"),
        )
    call = pl.pallas_call(wrapped, **kw)

    def run(*operands):
        outs = call(*operands, *hook.operands)
        return outs[:n_out], outs[n_out:]

    return run


def _run_hook(hook, name):
    def body():
        pass

    return _pcall(body, hook, name=name, in_specs=[], out_specs=[], out_shape=[])()[1]


def _cparams(sem=None):
    return pltpu.CompilerParams(dimension_semantics=sem, vmem_limit_bytes=VMEM_LIMIT)


def _full(shape):
    zeros = (0,) * len(shape)
    return pl.BlockSpec(shape, lambda *_: zeros)


def _scalars(*vals):
    return jnp.stack([jnp.asarray(v, jnp.int32) for v in vals])


def _sigmoid(x):
    return 1.0 / (1.0 + jnp.exp(-x))


def _gelu(x):
    t = jnp.tanh(GELU_C0 * (x + GELU_C1 * (x * x * x)))
    return x * (0.5 * (1.0 + t))


def _gelu_and_grad(x):
    x2 = x * x
    t = jnp.tanh(GELU_C0 * (x + GELU_C1 * (x2 * x)))
    cdf = 0.5 * (1.0 + t)
    return x * cdf, cdf + 0.5 * x * (1.0 - t * t) * (GELU_C0 * (1.0 + 3.0 * GELU_C1 * x2))


def _silu_and_grad(x):
    s = _sigmoid(x)
    return x * s, s * (1.0 + x * (1.0 - s))


def _softplus_neg(lam):
    u = jnp.exp(-jnp.abs(lam))
    w = 1.0 + u
    log1p = jnp.where(w == 1.0, u, jnp.log(w) * (u / jnp.where(w == 1.0, 1.0, w - 1.0)))
    return jnp.maximum(-lam, 0.0) + log1p


def _dot(a, b):
    return jnp.dot(a, b, preferred_element_type=F32)


def _dot_nt(a, b):
    return lax.dot_general(a, b, (((1,), (1,)), ((), ())), preferred_element_type=F32)


def _dot_tn(a, b):
    return lax.dot_general(a, b, (((0,), (0,)), ((), ())), preferred_element_type=F32)


def _shift_down(v, halo, k):
    if k == 0:
        return v
    rolled = pltpu.roll(v, k, 0)
    row = lax.broadcasted_iota(jnp.int32, (SUBLANES, v.shape[1]), 0)
    top = jnp.where(row < k, pltpu.roll(halo, k, 0), rolled[:SUBLANES])
    return jnp.concatenate([top, rolled[SUBLANES:]], axis=0)


def _shift_up(v, head, k):
    if k == 0:
        return v
    n = v.shape[0]
    rolled = pltpu.roll(v, n - k, 0)
    row = lax.broadcasted_iota(jnp.int32, (SUBLANES, v.shape[1]), 0)
    bot = jnp.where(row >= SUBLANES - k, pltpu.roll(head, SUBLANES - k, 0), rolled[n - SUBLANES :])
    return jnp.concatenate([rolled[: n - SUBLANES], bot], axis=0)


def _scan_blocks(a_ref, b_ref, out_ref, carry, n_rows, reverse):
    width = a_ref.shape[1]
    row = lax.broadcasted_iota(jnp.int32, (SUBLANES, width), 0)
    n_blocks = n_rows // SUBLANES

    def block(j, carry):
        i = (n_blocks - 1 - j) if reverse else j
        r0 = pl.multiple_of(i * SUBLANES, SUBLANES)
        a = a_ref[pl.ds(r0, SUBLANES), :]
        b = b_ref[pl.ds(r0, SUBLANES), :]
        for d in (1, 2, 4):
            shift = (SUBLANES - d) if reverse else d
            keep = (row < SUBLANES - d) if reverse else (row >= d)
            a_s = pltpu.roll(a, shift, 0)
            b_s = pltpu.roll(b, shift, 0)
            b = jnp.where(keep, a * b_s + b, b)
            a = jnp.where(keep, a * a_s, a)
        h = a * carry + b
        out_ref[pl.ds(r0, SUBLANES), :] = h
        edge = h[0:1, :] if reverse else h[SUBLANES - 1 : SUBLANES, :]
        return jnp.broadcast_to(edge, (SUBLANES, width))

    return lax.fori_loop(0, n_blocks, block, carry, unroll=SCAN_UNROLL)


def _rms_fwd(x, w):
    r = lax.rsqrt(jnp.mean(x * x, axis=-1, keepdims=True) + RMS_EPS)
    xh = x * r
    return xh * w, xh, r


def _rms_bwd(dh, xh, r, w):
    dxh = dh * w
    dx = r * (dxh - xh * jnp.mean(dxh * xh, axis=-1, keepdims=True))
    return dx, jnp.sum(dh * xh, axis=0, keepdims=True)


def _cast_to_segments(w, mine, rows, hook=None):
    n, c = w.shape
    per = n // 2 // rows

    def body(k_ref, w_ref, o_ref):
        o_ref[...] = w_ref[...].astype(BF16)

    out = _pcall(
        body,
        hook,
        name=f"cast_{n}x{c}",
        grid_spec=pltpu.PrefetchScalarGridSpec(
            num_scalar_prefetch=1,
            grid=(n // rows,),
            in_specs=[pl.BlockSpec((rows, c), lambda i, k_ref: (i, 0))],
            out_specs=[pl.BlockSpec((None, rows, c), lambda i, k_ref: (2 * k_ref[0] + i // per, i % per, 0))],
        ),
        out_shape=[jax.ShapeDtypeStruct((N_DEV, n // 2, c), BF16)],
        compiler_params=_cparams(("arbitrary",)),
    )(_scalars(mine), w)
    return out[0] if hook is None else (out[0][0], out[1])


def _place():
    x, y, c = lax.axis_index("x"), lax.axis_index("y"), lax.axis_index("c")
    chips = [(1 - x, y), (x, 1 - y), (1 - x, 1 - y)]
    return x, y, c, chips


def _chip_no(chip):
    return 2 * chip[0] + chip[1]


def _rcopy(src, dst, send_sem, recv_sem, to):
    return pltpu.make_async_remote_copy(
        src_ref=src, dst_ref=dst, send_sem=send_sem, recv_sem=recv_sem, device_id=to, device_id_type=MESH
    )


def _gather_hook(big, small=None):
    nb = len(big)
    n_sems = 6 * nb + 4

    def places():
        x, y, c, chips = _place()
        first = (x ^ (1 - c), y ^ c)
        second = (x ^ c, y ^ (1 - c))
        return x, y, c, chips, first, second, (1 - x, 1 - y)

    def seg(outs, b, chip, half):
        return outs[b].at[2 * _chip_no(chip) + half]

    def step1(outs, send, recv):
        x, y, c, _, first, _, _ = places()
        return [
            _rcopy(seg(outs, b, (x, y), c), seg(outs, b, (x, y), c), send.at[6 * b], recv.at[6 * b], (*first, c))
            for b in range(nb)
        ]

    def step2(outs, send, recv):
        x, y, c, _, first, second, _ = places()
        copies = []
        for b in range(nb):
            for k, chip in ((1, (x, y)), (2, first)):
                src = seg(outs, b, chip, c)
                copies.append(_rcopy(src, src, send.at[6 * b + k], recv.at[6 * b + k], (*second, c)))
        return copies

    def hand_over(outs, send, recv, k, chip):
        x, y, c, *_ = places()
        return [
            _rcopy(seg(outs, b, chip, c), seg(outs, b, chip, c), send.at[6 * b + k], recv.at[6 * b + k], (x, y, 1 - c))
            for b in range(nb)
        ]

    def wait_landed(outs, send, recv, k, chip, half):
        x, y, c, *_ = places()
        for b in range(nb):
            got = seg(outs, b, chip, half)
            _rcopy(got, got, send.at[6 * b + k], recv.at[6 * b + k], (x, y, c)).wait_recv()

    def small_copies(ins, outs, send, recv):
        x, y, c, chips, *_ = places()
        there = outs[nb].at[_chip_no((x, y))]
        return [
            _rcopy(ins[nb], there, send.at[6 * nb + j], recv.at[6 * nb + j], (*chip, c)) for j, chip in enumerate(chips)
        ]

    def local_copy(ins, outs, send):
        x, y, _, _ = _place()
        return pltpu.make_async_copy(ins[nb], outs[nb].at[_chip_no((x, y))], send.at[6 * nb + 3])

    def start(ins, outs, send, recv):
        for cp in step1(outs, send, recv):
            cp.start()
        if small is not None:
            for cp in small_copies(ins, outs, send, recv):
                cp.start()
            local_copy(ins, outs, send).start()

    def middle(ins, outs, send, recv):
        *_, first, _, _ = places()
        wait_landed(outs, send, recv, 0, first, places()[2])
        for cp in step2(outs, send, recv) + hand_over(outs, send, recv, 3, first):
            cp.start()

    def late(ins, outs, send, recv):
        x, y, c, chips, first, second, diagonal = places()
        for k, chip in ((1, second), (2, diagonal)):
            wait_landed(outs, send, recv, k, chip, c)
            for cp in hand_over(outs, send, recv, 3 + k, chip):
                cp.start()

    def finish(ins, outs, send, recv):
        x, y, c, chips, first, second, diagonal = places()
        wait_landed(outs, send, recv, 3, second, 1 - c)
        wait_landed(outs, send, recv, 4, first, 1 - c)
        wait_landed(outs, send, recv, 5, diagonal, 1 - c)
        sent = step1(outs, send, recv) + step2(outs, send, recv)
        for k, chip in ((3, first), (4, second), (5, diagonal)):
            sent += hand_over(outs, send, recv, k, chip)
        for cp in sent:
            cp.wait_send()
        if small is not None:
            for j, chip in enumerate(chips):
                got = outs[nb].at[_chip_no(chip)]
                _rcopy(got, got, send.at[6 * nb + j], recv.at[6 * nb + j], (x, y, c)).wait_recv()
            for cp in small_copies(ins, outs, send, recv):
                cp.wait_send()
            local_copy(ins, outs, send).wait()

    operands = list(big) + ([small] if small is not None else [])
    out_shapes = [jax.ShapeDtypeStruct(b.shape, b.dtype) for b in big]
    if small is not None:
        out_shapes.append(jax.ShapeDtypeStruct((N_CHIPS, *small.shape), small.dtype))
    return _Hook(operands, out_shapes, {b: b for b in range(nb)}, n_sems, start, finish, middle, late)


def _both_ways_hook(operands, out_shapes, copies_of, n_sems):
    def start(ins, outs, send, recv):
        for cp in copies_of(ins, outs, send, recv):
            cp.start()

    def finish(ins, outs, send, recv):
        for cp in copies_of(ins, outs, send, recv):
            cp.wait()

    return _Hook(operands, out_shapes, {}, n_sems, start, finish)


def _swap_hook(bufs):
    def copies_of(ins, outs, send, recv):
        x, y, c, _ = _place()
        copies = []
        for b in range(len(bufs)):
            for j in range(N_CHIPS):
                k = b * N_CHIPS + j
                copies.append(_rcopy(ins[b].at[2 * j + 1 - c], outs[b].at[j], send.at[k], recv.at[k], (x, y, 1 - c)))
        return copies

    out_shapes = [jax.ShapeDtypeStruct((N_CHIPS, *b.shape[1:]), b.dtype) for b in bufs]
    return _both_ways_hook(list(bufs), out_shapes, copies_of, len(bufs) * N_CHIPS)


def _axis_order():
    x, y, c, _ = _place()
    return (x, y), c, (x ^ (1 - c), y ^ c), (x ^ c, y ^ (1 - c)), (1 - x, 1 - y)


def _send_first_hook(parts):
    def copies_of(ins, outs, send, recv):
        _, c, first, _, _ = _axis_order()
        copies = []
        for b in range(len(parts)):
            for k in range(2):
                sem = 2 * b + k
                copies.append(_rcopy(ins[b].at[k], outs[b].at[k], send.at[sem], recv.at[sem], (*first, c)))
        return copies

    out_shapes = [jax.ShapeDtypeStruct((2, *p.shape[1:]), p.dtype) for p in parts]
    return _both_ways_hook(list(parts), out_shapes, copies_of, len(parts) * 2)


def _send_second_hook(mids):
    def copies_of(ins, outs, send, recv):
        _, c, _, second, _ = _axis_order()
        return [_rcopy(ins[b], outs[b], send.at[b], recv.at[b], (*second, c)) for b in range(len(mids))]

    out_shapes = [jax.ShapeDtypeStruct(m.shape, m.dtype) for m in mids]
    return _both_ways_hook(list(mids), out_shapes, copies_of, len(mids))


def _share_hook(big, small=None, tiny=None):
    nb = len(big)
    n_sems = nb + 7 + N_DEV
    t0 = nb + 7

    def tiny_copies(ins, outs, send, recv):
        x, y, c, _ = _place()
        there = outs[-1].at[2 * _chip_no((x, y)) + c]
        copies = []
        for r in range(1, N_DEV):
            to = (x ^ (r >> 2 & 1), y ^ (r >> 1 & 1), c ^ (r & 1))
            copies.append(_rcopy(ins[-1], there, send.at[t0 + r], recv.at[t0 + r], to))
        return copies

    def tiny_local(ins, outs, send):
        x, y, c, _ = _place()
        return pltpu.make_async_copy(ins[-1], outs[-1].at[2 * _chip_no((x, y)) + c], send.at[t0])

    def first_copies(outs, send, recv):
        x, y, c, chips = _place()
        sibling = (x, y, 1 - c)
        copies = [_rcopy(outs[b].at[c], outs[b].at[c], send.at[b], recv.at[b], sibling) for b in range(nb)]
        if small is not None:
            own = outs[nb].at[2 * _chip_no((x, y)) + c]
            copies.append(_rcopy(own, own, send.at[nb], recv.at[nb], sibling))
            for j, chip in enumerate(chips):
                copies.append(_rcopy(own, own, send.at[nb + 1 + j], recv.at[nb + 1 + j], (*chip, c)))
        return copies

    def start(ins, outs, send, recv):
        for cp in first_copies(outs, send, recv):
            cp.start()
        if tiny is not None:
            for cp in tiny_copies(ins, outs, send, recv):
                cp.start()
            tiny_local(ins, outs, send).start()

    def finish(ins, outs, send, recv):
        x, y, c, chips = _place()
        me, sibling = (x, y, c), (x, y, 1 - c)
        if tiny is not None:
            for cp in tiny_copies(ins, outs, send, recv):
                cp.wait()
            tiny_local(ins, outs, send).wait()
        passed = []
        if small is not None:
            for j, chip in enumerate(chips):
                got = outs[nb].at[2 * _chip_no(chip) + c]
                _rcopy(got, got, send.at[nb + 1 + j], recv.at[nb + 1 + j], me).wait_recv()
                fwd = _rcopy(got, got, send.at[nb + 4 + j], recv.at[nb + 4 + j], sibling)
                fwd.start()
                passed.append(fwd)
        for b in range(nb):
            got = outs[b].at[1 - c]
            _rcopy(got, got, send.at[b], recv.at[b], me).wait_recv()
        if small is not None:
            got = outs[nb].at[2 * _chip_no((x, y)) + 1 - c]
            _rcopy(got, got, send.at[nb], recv.at[nb], me).wait_recv()
            for j, chip in enumerate(chips):
                got = outs[nb].at[2 * _chip_no(chip) + 1 - c]
                _rcopy(got, got, send.at[nb + 4 + j], recv.at[nb + 4 + j], me).wait_recv()
        for cp in first_copies(outs, send, recv) + passed:
            cp.wait_send()

    operands = list(big) + ([small] if small is not None else [])
    out_shapes = [jax.ShapeDtypeStruct(a.shape, a.dtype) for a in operands]
    aliases = {i: i for i in range(len(operands))}
    if tiny is not None:
        operands.append(tiny)
        out_shapes.append(jax.ShapeDtypeStruct((N_DEV, *tiny.shape), tiny.dtype))
    return _Hook(operands, out_shapes, aliases, n_sems, start, finish)


def _row_tile(rows, cols, target_bytes=2 * 1024 * 1024):
    best = SUBLANES
    for t in range(SUBLANES, rows + 1, SUBLANES):
        if rows % t == 0 and t * cols * 4 <= target_bytes:
            best = t
    return best


def _add_own_half(buf, got, owners, c, wire):
    _, rows, cols = buf.shape
    tr = _row_tile(rows, cols)

    def body(s_ref, a_ref, b_ref, o_ref):
        o_ref[...] = (a_ref[...] + b_ref[...]).astype(wire)

    return _pcall(
        body,
        name=f"add_own_half_{rows}x{cols}",
        grid_spec=pltpu.PrefetchScalarGridSpec(
            num_scalar_prefetch=1,
            grid=(2, rows // tr),
            in_specs=[
                pl.BlockSpec((None, None, tr, cols), lambda j, r, s_ref: (s_ref[j], s_ref[2], r, 0)),
                pl.BlockSpec((None, tr, cols), lambda j, r, s_ref: (s_ref[j], r, 0)),
            ],
            out_specs=pl.BlockSpec((None, tr, cols), lambda j, r, s_ref: (j, r, 0)),
        ),
        out_shape=jax.ShapeDtypeStruct((2, rows, cols), wire),
        compiler_params=_cparams(("arbitrary", "arbitrary")),
    )(_scalars(owners[0], owners[1], c), buf.reshape(N_CHIPS, 2, rows, cols), got)


def _add_for_neighbour(buf, got_a, got1, second, c, wire):
    _, rows, cols = buf.shape
    tr = _row_tile(rows, cols)

    def body(s_ref, x_ref, a_ref, g_ref, o_ref):
        o_ref[...] = ((x_ref[...] + a_ref[...]) + g_ref[...].astype(F32)).astype(wire)

    return _pcall(
        body,
        name=f"add_for_neighbour_{rows}x{cols}",
        grid_spec=pltpu.PrefetchScalarGridSpec(
            num_scalar_prefetch=1,
            grid=(rows // tr,),
            in_specs=[
                pl.BlockSpec((None, None, tr, cols), lambda r, s_ref: (s_ref[0], s_ref[1], r, 0)),
                pl.BlockSpec((None, tr, cols), lambda r, s_ref: (s_ref[0], r, 0)),
                pl.BlockSpec((None, tr, cols), lambda r, s_ref: (1, r, 0)),
            ],
            out_specs=pl.BlockSpec((tr, cols), lambda r, s_ref: (r, 0)),
        ),
        out_shape=jax.ShapeDtypeStruct((rows, cols), wire),
        compiler_params=_cparams(("arbitrary",)),
    )(_scalars(second, c), buf.reshape(N_CHIPS, 2, rows, cols), got_a, got1)


def _add_received(buf, got_a, got1, got2, mine, c, slot, n_slots):
    _, rows, cols = buf.shape
    tr = _row_tile(rows, cols)

    def body(s_ref, x_ref, a_ref, g1_ref, g2_ref, o_ref):
        own = x_ref[...] + a_ref[...]
        o_ref[...] = (own + g1_ref[...].astype(F32)) + g2_ref[...].astype(F32)

    return _pcall(
        body,
        name=f"add_received_{rows}x{cols}",
        grid_spec=pltpu.PrefetchScalarGridSpec(
            num_scalar_prefetch=1,
            grid=(rows // tr,),
            in_specs=[
                pl.BlockSpec((None, None, tr, cols), lambda r, s_ref: (s_ref[0], s_ref[1], r, 0)),
                pl.BlockSpec((None, tr, cols), lambda r, s_ref: (s_ref[0], r, 0)),
                pl.BlockSpec((None, tr, cols), lambda r, s_ref: (0, r, 0)),
                pl.BlockSpec((tr, cols), lambda r, s_ref: (r, 0)),
            ],
            out_specs=pl.BlockSpec((None, tr, cols), lambda r, s_ref: (s_ref[2], r, 0)),
        ),
        out_shape=jax.ShapeDtypeStruct((n_slots, rows, cols), F32),
        compiler_params=_cparams(("arbitrary",)),
    )(_scalars(mine, c, slot), buf.reshape(N_CHIPS, 2, rows, cols), got_a, got1, got2)


def _layer_a_fwd(x, nw, win, ln_w, ln_b, wc, bs_t, wout, tm, hook):
    t_rows, d = x.shape
    n_sh, _, s_cols = win.shape
    aw = wout.shape[0]
    gd = aw // A_GROUPS
    tn = 512
    assert s_cols % tn == 0 and aw % tn == 0 and tm % CHUNK == 0

    def body(x_ref, nw_ref, win_ref, lnw_ref, lnb_ref, wc_ref, bst_ref, wout_ref, z_ref, x1_ref, h_ref, u_s, v_s, y_s):
        x = x_ref[...]
        h, _, _ = _rms_fwd(x, nw_ref[...])
        h = h.astype(BF16)
        h_ref[...] = h
        for j in range(3 * aw // tn):
            k, off = divmod(j * tn, s_cols)
            cols = slice((j * tn) % aw, (j * tn) % aw + tn)
            zj = _dot(h, win_ref[k, :, off : off + tn])
            z_ref[:, j * tn : (j + 1) * tn] = zj
            if j * tn < aw:
                u_s[:, cols] = _gelu(zj)
            elif j * tn < 2 * aw:
                v_s[:, cols] = _gelu(zj)
            else:
                u_s[:, cols] = u_s[:, cols] * (zj * _sigmoid(zj))
        v = v_s[...]
        mu = jnp.mean(v, axis=-1, keepdims=True)
        vc = v - mu
        rstd = lax.rsqrt(jnp.mean(vc * vc, axis=-1, keepdims=True) + LN_EPS)
        v_s[...] = (vc * rstd) * lnw_ref[...] + lnb_ref[...]
        for ck in range(tm // CHUNK):
            rows = slice(ck * CHUNK, (ck + 1) * CHUNK)
            for g in range(A_GROUPS):
                cols = slice(g * gd, (g + 1) * gd)
                s = _dot(wc_ref[g], v_s[rows, cols].astype(BF16)) + bst_ref[:, g : g + 1]
                y_s[rows, cols] = (u_s[rows, cols] * s).astype(BF16)
        x1_ref[...] = x + _dot(y_s[...], wout_ref[...])

    row = lambda i: (i, 0)
    return _pcall(
        body,
        hook,
        name="layer_a_fwd",
        grid=(t_rows // tm,),
        in_specs=[
            pl.BlockSpec((tm, d), row),
            _full(nw.shape),
            _full(win.shape),
            _full(ln_w.shape),
            _full(ln_b.shape),
            _full(wc.shape),
            _full(bs_t.shape),
            _full(wout.shape),
        ],
        out_specs=[pl.BlockSpec((tm, 3 * aw), row), pl.BlockSpec((tm, d), row), pl.BlockSpec((tm, d), row)],
        out_shape=[
            jax.ShapeDtypeStruct((t_rows, 3 * aw), F32),
            jax.ShapeDtypeStruct((t_rows, d), F32),
            jax.ShapeDtypeStruct((t_rows, d), BF16),
        ],
        scratch_shapes=[pltpu.VMEM((tm, aw), F32), pltpu.VMEM((tm, aw), F32), pltpu.VMEM((tm, aw), BF16)],
        compiler_params=_cparams(("arbitrary",)),
    )(x, nw, win, ln_w, ln_b, wc, bs_t, wout)


def _layer_a_bwd(dout, z, ln_w, ln_b, wc, wct, bs_t, wout, tiles, earlier, hook):
    t_rows, d = dout.shape
    aw = wout.shape[0]
    gd = aw // A_GROUPS
    tm = TM_A_BWD
    lo, hi = tiles
    n_earlier = 0 if earlier is None else len(earlier)

    def body(dout_ref, z_ref, lnw_ref, lnb_ref, wc_ref, wct_ref, bst_ref, wout_ref, *rest):
        dz_ref, y_ref, dob_ref, gws_ref, gbs_ref, glnw_ref, glnb_ref, u_s, vh_s, ds_s, dvn_s = rest[n_earlier:]

        @pl.when(pl.program_id(0) == 0)
        def _():
            gws_ref[...] = jnp.zeros_like(gws_ref)
            gbs_ref[...] = jnp.zeros_like(gbs_ref)
            glnw_ref[...] = jnp.zeros_like(glnw_ref)
            glnb_ref[...] = jnp.zeros_like(glnb_ref)

        dob = dout_ref[...].astype(BF16)
        dob_ref[...] = dob
        dy = _dot_nt(dob, wout_ref[...])

        zv = z_ref[:, aw : 2 * aw]
        vg, dvg_dz = _gelu_and_grad(zv)
        mu = jnp.mean(vg, axis=-1, keepdims=True)
        vc = vg - mu
        rstd = lax.rsqrt(jnp.mean(vc * vc, axis=-1, keepdims=True) + LN_EPS)
        vh = vc * rstd
        vh_s[...] = vh
        vn = (vh * lnw_ref[...] + lnb_ref[...]).astype(BF16)

        zu = z_ref[:, 0:aw]
        zg = z_ref[:, 2 * aw : 3 * aw]
        u, du_dz = _gelu_and_grad(zu)
        sg, dsg = _silu_and_grad(zg)
        u_s[...] = u * sg
        tril = lax.broadcasted_iota(jnp.int32, (CHUNK, CHUNK), 0) >= lax.broadcasted_iota(jnp.int32, (CHUNK, CHUNK), 1)
        for ck in range(tm // CHUNK):
            rows = slice(ck * CHUNK, (ck + 1) * CHUNK)
            for g in range(A_GROUPS):
                cols = slice(g * gd, (g + 1) * gd)
                vn_g = vn[rows, cols]
                s = _dot(wc_ref[g], vn_g) + bst_ref[:, g : g + 1]
                usg = u_s[rows, cols]
                dy_g = dy[rows, cols]
                y_ref[rows, cols] = (usg * s).astype(BF16)
                ds = dy_g * usg
                ds_s[rows, cols] = dy_g * s
                gbs_ref[:, g : g + 1] += jnp.sum(ds, axis=-1, keepdims=True)
                dsb = ds.astype(BF16)
                gws_ref[g] += jnp.where(tril, _dot_nt(dsb, vn_g), 0.0)
                dvn_s[rows, cols] = _dot(wct_ref[g], dsb)
        dusg = ds_s[...]
        dz_ref[:, 0:aw] = (dusg * sg * du_dz).astype(BF16)
        dz_ref[:, 2 * aw : 3 * aw] = (dusg * u * dsg).astype(BF16)

        dvn = dvn_s[...]
        vh = vh_s[...]
        glnw_ref[...] += jnp.sum(dvn * vh, axis=0, keepdims=True)
        glnb_ref[...] += jnp.sum(dvn, axis=0, keepdims=True)
        dvh = dvn * lnw_ref[...]
        dvg = rstd * (dvh - jnp.mean(dvh, axis=-1, keepdims=True) - vh * jnp.mean(dvh * vh, axis=-1, keepdims=True))
        dz_ref[:, aw : 2 * aw] = (dvg * dvg_dz).astype(BF16)

    row = lambda i: (i + lo, 0)
    call = _pcall(
        body,
        hook,
        name=f"layer_a_bwd_{lo}",
        grid=(hi - lo,),
        in_specs=[
            pl.BlockSpec((tm, d), row),
            pl.BlockSpec((tm, 3 * aw), row),
            _full(ln_w.shape),
            _full(ln_b.shape),
            _full(wc.shape),
            _full(wct.shape),
            _full(bs_t.shape),
            _full(wout.shape),
        ]
        + [ANY] * n_earlier,
        out_specs=[
            pl.BlockSpec((tm, 3 * aw), row),
            pl.BlockSpec((tm, aw), row),
            pl.BlockSpec((tm, d), row),
            _full((A_GROUPS, CHUNK, CHUNK)),
            _full((CHUNK, A_GROUPS)),
            _full((1, aw)),
            _full((1, aw)),
        ],
        out_shape=[
            jax.ShapeDtypeStruct((t_rows, 3 * aw), BF16),
            jax.ShapeDtypeStruct((t_rows, aw), BF16),
            jax.ShapeDtypeStruct((t_rows, d), BF16),
            jax.ShapeDtypeStruct((A_GROUPS, CHUNK, CHUNK), F32),
            jax.ShapeDtypeStruct((CHUNK, A_GROUPS), F32),
            jax.ShapeDtypeStruct((1, aw), F32),
            jax.ShapeDtypeStruct((1, aw), F32),
        ],
        scratch_shapes=[pltpu.VMEM((tm, aw), F32)] * 4,
        input_output_aliases={8 + i: i for i in range(n_earlier)},
        compiler_params=_cparams(("arbitrary",)),
    )
    return call(dout, z, ln_w, ln_b, wc, wct, bs_t, wout, *(earlier or ()))


def _layer_a_bwd_dx(dout, x, dz, nw, win, tm, tiles, earlier, hook):
    t_rows, d = x.shape
    n_sh, _, s_cols = win.shape
    lo, hi = tiles
    n_earlier = 0 if earlier is None else 1

    def body(dout_ref, x_ref, dz_ref, nw_ref, win_ref, *rest):
        gx_ref, gnw_ref = rest[n_earlier:]

        @pl.when(pl.program_id(0) == 0)
        def _():
            gnw_ref[...] = jnp.zeros_like(gnw_ref)

        dh = jnp.zeros((tm, d), F32)
        for k in range(n_sh):
            dh = dh + _dot_nt(dz_ref[:, k * s_cols : (k + 1) * s_cols], win_ref[k])
        nw = nw_ref[...]
        _, xh, r = _rms_fwd(x_ref[...], nw)
        dx, gnw = _rms_bwd(dh, xh, r, nw)
        gnw_ref[0:1, :] += gnw
        gx_ref[...] = dout_ref[...] + dx

    row = lambda i: (i + lo, 0)
    return _pcall(
        body,
        hook,
        name=f"layer_a_bwd_dx_{lo}",
        grid=(hi - lo,),
        in_specs=[
            pl.BlockSpec((tm, d), row),
            pl.BlockSpec((tm, d), row),
            pl.BlockSpec((tm, n_sh * s_cols), row),
            _full(nw.shape),
            _full(win.shape),
        ]
        + [ANY] * n_earlier,
        out_specs=[pl.BlockSpec((tm, d), row), _full((SUBLANES, d))],
        out_shape=[jax.ShapeDtypeStruct((t_rows, d), F32), jax.ShapeDtypeStruct((SUBLANES, d), F32)],
        input_output_aliases={5: 0} if n_earlier else {},
        compiler_params=_cparams(("arbitrary",)),
    )(dout, x, dz, nw, win, *([earlier] if n_earlier else []))


def _decay(r, sp_h):
    log_a = (-RG_C) * r * sp_h
    a = jnp.exp(log_a)
    mult = jnp.sqrt(jnp.tanh(-log_a) * (a * a + 1.0))
    return a, mult


def _gates(xc_h, gab_ref, gb_ref, sp_h, h, hd):
    pre = _dot(xc_h.astype(BF16), gab_ref[h])
    bw = gb_ref.shape[1] // 2
    r = _sigmoid(pre[:, :hd] + gb_ref[:, h * hd : (h + 1) * hd])
    ig = _sigmoid(pre[:, hd:] + gb_ref[:, bw + h * hd : bw + (h + 1) * hd])
    a, mult = _decay(r, sp_h)
    return r, ig, a, mult


def _conv(xb, halo, cw_ref, cb_ref):
    xc = cb_ref[...] + cw_ref[CONV_WIDTH - 1 : CONV_WIDTH, :] * xb
    for k in range(CONV_WIDTH - 1):
        xc = xc + cw_ref[k : k + 1, :] * _shift_down(xb, halo, CONV_WIDTH - 1 - k)
    return xc


def _layer_b_fwd(x1, nw, bin_w, cw, cb, gab, gb, lam, bout, nf, tgt, tm):
    t_rows, d = x1.shape
    bw = bout.shape[0]
    hd = bw // B_HEADS
    nt = t_rows // tm

    def body(
        x1_ref, nw_ref, bin_ref, cw_ref, cb_ref, gab_ref, gb_ref, lam_ref, bout_ref, nf_ref, tgt_ref,
        z_ref, h_ref, h1_ref, xbt_ref, ht_ref, dx2_ref, loss_ref, gnf_ref,
        tail_s, carry_s, a_s, b_s, hs_s, acc_s,
    ):
        @pl.when(pl.program_id(0) == 0)
        def _():
            tail_s[...] = jnp.zeros_like(tail_s)
            carry_s[...] = jnp.zeros_like(carry_s)
            acc_s[...] = jnp.zeros_like(acc_s)
            gnf_ref[...] = jnp.zeros_like(gnf_ref)

        x1 = x1_ref[...]
        h1, _, _ = _rms_fwd(x1, nw_ref[...])
        h1 = h1.astype(BF16)
        h1_ref[...] = h1
        z = jnp.concatenate([_dot(h1, bin_ref[k]) for k in range(N_CHIPS)], axis=1)
        z_ref[...] = z
        xb = z[:, :bw]
        xc = _conv(xb, tail_s[...], cw_ref, cb_ref)
        tail = xb[tm - SUBLANES :, :]
        tail_s[...] = tail
        xbt_ref[...] = tail
        sp = _softplus_neg(lam_ref[...])
        for h in range(B_HEADS):
            cols = slice(h * hd, (h + 1) * hd)
            xc_h = xc[:, cols]
            _, ig, a, mult = _gates(xc_h, gab_ref, gb_ref, sp[:, cols], h, hd)
            a_s[:, cols] = a
            b_s[:, cols] = mult * (ig * xc_h)
        carry = _scan_blocks(a_s, b_s, hs_s, carry_s[...], tm, reverse=False)
        carry_s[...] = carry
        ht_ref[...] = hs_s[tm - SUBLANES :, :]
        hs = hs_s[...]
        h_ref[...] = hs
        g = z[:, bw:]
        y = (hs * (g * _sigmoid(g))).astype(BF16)
        x2 = x1 + _dot(y, bout_ref[...])

        nf = nf_ref[...]
        o, xh, r = _rms_fwd(x2, nf)
        diff = o - tgt_ref[...]
        acc_s[...] += jnp.sum(diff * diff, axis=0, keepdims=True)
        do = diff * (1.0 / d)
        dx2, gnf = _rms_bwd(do, xh, r, nf)
        gnf_ref[...] += gnf
        dx2_ref[...] = dx2

        @pl.when(pl.program_id(0) == nt - 1)
        def _():
            total = jnp.sum(acc_s[...], axis=-1, keepdims=True) * (0.5 / d)
            loss_ref[...] = jnp.broadcast_to(total, loss_ref.shape)

    row = lambda i: (i, 0)
    return _pcall(
        body,
        name="layer_b_fwd",
        grid=(nt,),
        in_specs=[
            pl.BlockSpec((tm, d), row),
            _full(nw.shape),
            _full(bin_w.shape),
            _full(cw.shape),
            _full(cb.shape),
            _full(gab.shape),
            _full(gb.shape),
            _full(lam.shape),
            _full(bout.shape),
            _full(nf.shape),
            pl.BlockSpec((tm, d), row),
        ],
        out_specs=[
            pl.BlockSpec((tm, 2 * bw), row),
            pl.BlockSpec((tm, bw), row),
            pl.BlockSpec((tm, d), row),
            pl.BlockSpec((None, SUBLANES, bw), lambda i: (i, 0, 0)),
            pl.BlockSpec((None, SUBLANES, bw), lambda i: (i, 0, 0)),
            pl.BlockSpec((tm, d), row),
            _full((1, LANES)),
            _full((1, d)),
        ],
        out_shape=[
            jax.ShapeDtypeStruct((t_rows, 2 * bw), F32),
            jax.ShapeDtypeStruct((t_rows, bw), F32),
            jax.ShapeDtypeStruct((t_rows, d), BF16),
            jax.ShapeDtypeStruct((nt, SUBLANES, bw), F32),
            jax.ShapeDtypeStruct((nt, SUBLANES, bw), F32),
            jax.ShapeDtypeStruct((t_rows, d), F32),
            jax.ShapeDtypeStruct((1, LANES), F32),
            jax.ShapeDtypeStruct((1, d), F32),
        ],
        scratch_shapes=[
            pltpu.VMEM((SUBLANES, bw), F32),
            pltpu.VMEM((SUBLANES, bw), F32),
            pltpu.VMEM((tm, bw), F32),
            pltpu.VMEM((tm, bw), F32),
            pltpu.VMEM((tm, bw), F32),
            pltpu.VMEM((1, d), F32),
        ],
        compiler_params=_cparams(("arbitrary",)),
    )(x1, nw, bin_w, cw, cb, gab, gb, lam, bout, nf, tgt)


def _layer_b_bwd(dout, x1, z, hseq, xb_tails, h_tails, nw, bin_w, cw, cb, gab, gabt, gb, lam, bout, tm):
    t_rows, d = x1.shape
    bw = bout.shape[0]
    hd = bw // B_HEADS
    nt = t_rows // tm

    def body(
        dout_ref, x1_ref, z_ref, h_ref, xbt_ref, ht_ref, nw_ref, bin_ref, cw_ref, cb_ref, gab_ref, gabt_ref,
        gb_ref, lam_ref, bout_ref,
        dx1_ref, dz_ref, y_ref, dob_ref, ggab_ref, ggb_ref, gcw_ref, gcb_ref, glam_ref, gnw_ref,
        gcarry_s, afirst_s, head_s, aup_s, dh_s, gt_s, dxc_s, xc_s, r_s, ig_s,
    ):
        step = pl.program_id(0)
        tile = nt - 1 - step

        @pl.when(step == 0)
        def _():
            for ref in (ggab_ref, ggb_ref, gcw_ref, gcb_ref, glam_ref, gnw_ref, gcarry_s, afirst_s, head_s):
                ref[...] = jnp.zeros_like(ref)

        first_tile = tile == 0
        xb_halo = jnp.where(first_tile, 0.0, xbt_ref[...])
        h_halo = jnp.where(first_tile, 0.0, ht_ref[...])

        dout = dout_ref[...]
        dob = dout.astype(BF16)
        dob_ref[...] = dob
        dy = _dot_nt(dob, bout_ref[...])
        hs = h_ref[...]
        g = z_ref[:, bw:]
        sg, dsg = _silu_and_grad(g)
        y_ref[...] = (hs * sg).astype(BF16)
        dz_ref[:, bw:] = (dy * hs * dsg).astype(BF16)
        dh_s[...] = dy * sg

        xb = z_ref[:, :bw]
        xc = _conv(xb, xb_halo, cw_ref, cb_ref)
        xc_s[...] = xc
        lam = lam_ref[...]
        sp = _softplus_neg(lam)
        for h in range(B_HEADS):
            cols = slice(h * hd, (h + 1) * hd)
            r, ig, a, _ = _gates(xc[:, cols], gab_ref, gb_ref, sp[:, cols], h, hd)
            r_s[:, cols] = r
            ig_s[:, cols] = ig
            aup_s[:, cols] = _shift_up(a, afirst_s[:, cols], 1)
            afirst_s[:, cols] = jnp.broadcast_to(a[0:1, :], (SUBLANES, hd))
        carry = _scan_blocks(aup_s, dh_s, gt_s, gcarry_s[...], tm, reverse=True)
        gcarry_s[...] = carry

        h_prev = _shift_down(hs, h_halo, 1)
        for h in range(B_HEADS):
            cols = slice(h * hd, (h + 1) * hd)
            xc_h = xc_s[:, cols]
            sp_h = sp[:, cols]
            r, ig = r_s[:, cols], ig_s[:, cols]
            a, mult = _decay(r, sp_h)
            gt = gt_s[:, cols]
            da = gt * h_prev[:, cols]
            dmult = gt * (ig * xc_h)
            dig = gt * (mult * xc_h)
            dxc_direct = gt * (mult * ig)
            dla = da * a - dmult * (a * a) / mult
            glam_ref[:, cols] += jnp.sum(dla * r, axis=0, keepdims=True)
            dr = dla * ((-RG_C) * sp_h)
            dpre = jnp.concatenate([dr * r * (1.0 - r), dig * ig * (1.0 - ig)], axis=1)
            ggb_ref[:, cols] += jnp.sum(dpre[:, :hd], axis=0, keepdims=True)
            ggb_ref[:, bw + h * hd : bw + (h + 1) * hd] += jnp.sum(dpre[:, hd:], axis=0, keepdims=True)
            dpb = dpre.astype(BF16)
            ggab_ref[h] += _dot_tn(xc_h.astype(BF16), dpb)
            dxc_s[:, cols] = dxc_direct + _dot(dpb, gabt_ref[h])
        glam_ref[...] = jnp.where(step == nt - 1, glam_ref[...] * (RG_C * _sigmoid(-lam)), glam_ref[...])

        dxc = dxc_s[...]
        gcb_ref[...] += jnp.sum(dxc, axis=0, keepdims=True)
        dxb = cw_ref[CONV_WIDTH - 1 : CONV_WIDTH, :] * dxc
        gcw_ref[CONV_WIDTH - 1 : CONV_WIDTH, :] += jnp.sum(dxc * xb, axis=0, keepdims=True)
        head = head_s[...]
        for k in range(CONV_WIDTH - 1):
            lag = CONV_WIDTH - 1 - k
            dxb = dxb + cw_ref[k : k + 1, :] * _shift_up(dxc, head, lag)
            gcw_ref[k : k + 1, :] += jnp.sum(dxc * _shift_down(xb, xb_halo, lag), axis=0, keepdims=True)
        head_s[...] = dxc[:SUBLANES, :]
        dz_ref[:, :bw] = dxb.astype(BF16)

        s_cols = 2 * bw // N_CHIPS
        dh1 = jnp.zeros((tm, d), F32)
        for k in range(N_CHIPS):
            dh1 = dh1 + _dot_nt(dz_ref[:, k * s_cols : (k + 1) * s_cols], bin_ref[k])
        x1 = x1_ref[...]
        nw = nw_ref[...]
        _, xh, r1 = _rms_fwd(x1, nw)
        dx, gnw = _rms_bwd(dh1, xh, r1, nw)
        gnw_ref[...] += gnw
        dx1_ref[...] = dout + dx

    rev = lambda i: (nt - 1 - i, 0)
    prev = lambda i: (jnp.maximum(nt - 2 - i, 0), 0, 0)
    return _pcall(
        body,
        name="layer_b_bwd",
        grid=(nt,),
        in_specs=[
            pl.BlockSpec((tm, d), rev),
            pl.BlockSpec((tm, d), rev),
            pl.BlockSpec((tm, 2 * bw), rev),
            pl.BlockSpec((tm, bw), rev),
            pl.BlockSpec((None, SUBLANES, bw), prev),
            pl.BlockSpec((None, SUBLANES, bw), prev),
            _full(nw.shape),
            _full(bin_w.shape),
            _full(cw.shape),
            _full(cb.shape),
            _full(gab.shape),
            _full(gabt.shape),
            _full(gb.shape),
            _full(lam.shape),
            _full(bout.shape),
        ],
        out_specs=[
            pl.BlockSpec((tm, d), rev),
            pl.BlockSpec((tm, 2 * bw), rev),
            pl.BlockSpec((tm, bw), rev),
            pl.BlockSpec((tm, d), rev),
            _full((B_HEADS, hd, 2 * hd)),
            _full((1, 2 * bw)),
            _full((SUBLANES, bw)),
            _full((1, bw)),
            _full((1, bw)),
            _full((1, d)),
        ],
        out_shape=[
            jax.ShapeDtypeStruct((t_rows, d), F32),
            jax.ShapeDtypeStruct((t_rows, 2 * bw), BF16),
            jax.ShapeDtypeStruct((t_rows, bw), BF16),
            jax.ShapeDtypeStruct((t_rows, d), BF16),
            jax.ShapeDtypeStruct((B_HEADS, hd, 2 * hd), F32),
            jax.ShapeDtypeStruct((1, 2 * bw), F32),
            jax.ShapeDtypeStruct((SUBLANES, bw), F32),
            jax.ShapeDtypeStruct((1, bw), F32),
            jax.ShapeDtypeStruct((1, bw), F32),
            jax.ShapeDtypeStruct((1, d), F32),
        ],
        scratch_shapes=[pltpu.VMEM((SUBLANES, bw), F32)] * 3 + [pltpu.VMEM((tm, bw), F32)] * 7,
        compiler_params=_cparams(("arbitrary",)),
    )(dout, x1, z, hseq, xb_tails, h_tails, nw, bin_w, cw, cb, gab, gabt, gb, lam, bout)


def _wgrad(a, b, m_blocks, n_blocks, hook=None):
    k, m = a.shape
    n = b.shape[1]
    bm, bn = m // m_blocks, n // n_blocks

    def body(a_ref, b_ref, o_ref):
        o_ref[...] = _dot_tn(a_ref[...], b_ref[...])

    out = _pcall(
        body,
        hook,
        name=f"wgrad_{m}x{n}",
        grid=(n_blocks, m_blocks),
        in_specs=[pl.BlockSpec((k, bm), lambda j, i: (0, i)), pl.BlockSpec((k, bn), lambda j, i: (0, j))],
        out_specs=[pl.BlockSpec((None, None, bm, bn), lambda j, i: (j, i, 0, 0))],
        out_shape=[jax.ShapeDtypeStruct((n_blocks, m_blocks, bm, bn), F32)],
        compiler_params=_cparams(("arbitrary", "arbitrary")),
    )(a, b)
    return out[0] if hook is None else (out[0][0], out[1])


def _adamw_math(w, g, m, v):
    m = ADAM_B1 * m + (1.0 - ADAM_B1) * g
    v = ADAM_B2 * v + (1.0 - ADAM_B2) * (g * g)
    m_hat = m / (1.0 - ADAM_B1**ADAM_STEP)
    v_hat = v / (1.0 - ADAM_B2**ADAM_STEP)
    delta = -ADAM_LR * (m_hat / (jnp.sqrt(v_hat) + ADAM_EPS) + ADAM_WD * w)
    return delta, m, v


def _adamw(w, g, m, v, hook=None):
    rows, cols = w.shape
    tr = _row_tile(rows, cols, 1024 * 1024)

    def body(w_ref, g_ref, m_ref, v_ref, d_ref, mo_ref, vo_ref):
        d_ref[...], mo_ref[...], vo_ref[...] = _adamw_math(w_ref[...], g_ref[...], m_ref[...], v_ref[...])

    spec = pl.BlockSpec((tr, cols), lambda i: (i, 0))
    return _pcall(
        body,
        hook,
        name=f"adamw_{rows}x{cols}",
        grid=(rows // tr,),
        in_specs=[spec] * 4,
        out_specs=[spec] * 3,
        out_shape=[jax.ShapeDtypeStruct((rows, cols), F32)] * 3,
        compiler_params=_cparams(("arbitrary",)),
    )(w, g, m, v)


def _sum_partials(parts):
    def body(p_ref, o_ref):
        total = p_ref[0, 0:1, :]
        for k in range(1, N_DEV):
            total = total + p_ref[k, 0:1, :]
        o_ref[...] = total

    vmem = pl.BlockSpec(memory_space=pltpu.VMEM)
    return _pcall(
        body,
        name="sum_partials",
        in_specs=[vmem],
        out_specs=vmem,
        out_shape=jax.ShapeDtypeStruct((1, parts.shape[2]), F32),
    )(parts)


def _adamw_many(ws, gs, ms, vs, name, hook=None):
    n = len(ws)

    def body(*refs):
        w_refs, g_refs, m_refs, v_refs = (refs[i * n : (i + 1) * n] for i in range(4))
        d_refs, mo_refs, vo_refs = (refs[(4 + i) * n : (5 + i) * n] for i in range(3))
        for i in range(n):
            d_refs[i][...], mo_refs[i][...], vo_refs[i][...] = _adamw_math(
                w_refs[i][...], g_refs[i][...], m_refs[i][...], v_refs[i][...]
            )

    vmem = pl.BlockSpec(memory_space=pltpu.VMEM)
    outs = _pcall(
        body,
        hook,
        name=name,
        in_specs=[vmem] * (4 * n),
        out_specs=[vmem] * (3 * n),
        out_shape=[jax.ShapeDtypeStruct(w.shape, F32) for w in ws] * 3,
        compiler_params=_cparams(),
    )(*ws, *gs, *ms, *vs)
    extra = None
    if hook is not None:
        outs, extra = outs
    return (outs[:n], outs[n : 2 * n], outs[2 * n :]), extra


def _pack_rows(parts, lanes=LANES):
    flat = jnp.concatenate([p.reshape(-1) for p in parts])
    per = N_DEV * SUBLANES * lanes
    total = -(-flat.shape[0] // per) * per
    flat = jnp.pad(flat, (0, total - flat.shape[0]))
    return flat.reshape(N_DEV, total // (N_DEV * lanes), lanes)


def _unpack(flat, shapes):
    out, at = [], 0
    for s in shapes:
        n = 1
        for dim in s:
            n *= dim
        out.append(flat[at : at + n].reshape(s))
        at += n
    return out


def kernel(x, norm_w, a_w_in, a_ln_w, a_ln_b, a_w_s, a_b_s, a_w_out, b_w_in, b_conv_w, b_conv_b, b_gate_a_w, b_gate_a_b, b_gate_x_w, b_gate_x_b, b_lambda, b_w_out, norm_f_w, loss_target, m_norm_w, m_a_w_in, m_a_ln_w, m_a_ln_b, m_a_w_s, m_a_b_s, m_a_w_out, m_b_w_in, m_b_conv_w, m_b_conv_b, m_b_gate_a_w, m_b_gate_a_b, m_b_gate_x_w, m_b_gate_x_b, m_b_lambda, m_b_w_out, m_norm_f_w, v_norm_w, v_a_w_in, v_a_ln_w, v_a_ln_b, v_a_w_s, v_a_b_s, v_a_w_out, v_b_w_in, v_b_conv_w, v_b_conv_b, v_b_gate_a_w, v_b_gate_a_b, v_b_gate_x_w, v_b_gate_x_b, v_b_lambda, v_b_w_out, v_norm_f_w):
    t_rows, d = x.shape[1], x.shape[2]
    aw = a_ln_w.shape[1]
    bw = b_gate_a_w.shape[1] * b_gate_a_w.shape[2]
    hd = bw // B_HEADS
    mine = 2 * lax.axis_index("x") + lax.axis_index("y")
    core = lax.axis_index("c")
    weights = dict(norm_w=norm_w, a_w_in=a_w_in, a_ln_w=a_ln_w, a_ln_b=a_ln_b, a_w_s=a_w_s, a_b_s=a_b_s, a_w_out=a_w_out, b_w_in=b_w_in, b_conv_w=b_conv_w, b_conv_b=b_conv_b, b_gate_a_w=b_gate_a_w, b_gate_a_b=b_gate_a_b, b_gate_x_w=b_gate_x_w, b_gate_x_b=b_gate_x_b, b_lambda=b_lambda, b_w_out=b_w_out, norm_f_w=norm_f_w)
    m_in = dict(norm_w=m_norm_w, a_w_in=m_a_w_in, a_ln_w=m_a_ln_w, a_ln_b=m_a_ln_b, a_w_s=m_a_w_s, a_b_s=m_a_b_s, a_w_out=m_a_w_out, b_w_in=m_b_w_in, b_conv_w=m_b_conv_w, b_conv_b=m_b_conv_b, b_gate_a_w=m_b_gate_a_w, b_gate_a_b=m_b_gate_a_b, b_gate_x_w=m_b_gate_x_w, b_gate_x_b=m_b_gate_x_b, b_lambda=m_b_lambda, b_w_out=m_b_w_out, norm_f_w=m_norm_f_w)
    v_in = dict(norm_w=v_norm_w, a_w_in=v_a_w_in, a_ln_w=v_a_ln_w, a_ln_b=v_a_ln_b, a_w_s=v_a_w_s, a_b_s=v_a_b_s, a_w_out=v_a_w_out, b_w_in=v_b_w_in, b_conv_w=v_b_conv_w, b_conv_b=v_b_conv_b, b_gate_a_w=v_b_gate_a_w, b_gate_a_b=v_b_gate_a_b, b_gate_x_w=v_b_gate_x_w, b_gate_x_b=v_b_gate_x_b, b_lambda=v_b_lambda, b_w_out=v_b_w_out, norm_f_w=v_norm_f_w)

    win_l = _cast_to_segments(a_w_in[0], mine, 256)
    wout_l = _cast_to_segments(a_w_out[0], mine, 256)
    small_l = jnp.concatenate([b_conv_w[0], b_conv_b, b_gate_a_b, b_gate_x_b, b_lambda], axis=0)
    bin_l, (win_g, wout_g, small_g) = _cast_to_segments(
        b_w_in[0], mine, 128, _gather_hook([win_l, wout_l], small_l)
    )
    bout_l = _cast_to_segments(b_w_out[0], mine, 192)
    win = win_g.reshape(N_CHIPS, d, -1)
    wout = wout_g.reshape(aw, d)

    tril = jnp.tril(jnp.ones((CHUNK, CHUNK), F32))
    wc = (a_w_s[0] * tril[None]).astype(BF16)
    wct = jnp.swapaxes(wc, 1, 2)
    bs_t = a_b_s[0].T
    gab = jnp.concatenate([b_gate_a_w[0], b_gate_x_w[0]], axis=2).astype(BF16)
    gabt = jnp.swapaxes(gab, 1, 2)
    nw0, nw1, nf = norm_w[0:1], norm_w[1:2], norm_f_w.reshape(1, d)

    x0 = x[0]
    (z_a, x1, h0), (bin_g, bout_g) = _layer_a_fwd(
        x0, nw0, win, a_ln_w, a_ln_b, wc, bs_t, wout, TM_FWD, _gather_hook([bin_l, bout_l])
    )
    bin_w = bin_g.reshape(N_CHIPS, d, -1)
    bout = bout_g.reshape(bw, d)
    small_f = jnp.transpose(small_g, (1, 0, 2)).reshape(SUBLANES, bw)
    cw, cb = small_f[0:CONV_WIDTH], small_f[CONV_WIDTH : CONV_WIDTH + 1]
    gb = jnp.concatenate([small_f[5:6], small_f[6:7]], axis=1)
    lam = small_f[7:8]
    z_b, hseq, h1, xb_tails, h_tails, dx2, loss_l, g_nf = _layer_b_fwd(
        x1, nw1, bin_w, cw, cb, gab, gb, lam, bout, nf, loss_target[0], TM_FWD
    )
    dx1, dz_b, y_b, dob_b, g_gab, g_gb, g_cw, g_cb, g_lam, g_nw1 = _layer_b_bwd(
        dx2, x1, z_b, hseq, xb_tails, h_tails, nw1, bin_w, cw, cb, gab, gabt, gb, lam, bout, TM_FWD
    )
    seg = lambda g: g.reshape(N_DEV, -1, g.shape[3])
    x_at, y_at = lax.axis_index("x"), lax.axis_index("y")
    first_no = 2 * (x_at ^ (1 - core)) + (y_at ^ core)
    second_no = 2 * (x_at ^ core) + (y_at ^ (1 - core))
    bf16s = lambda bufs: [BF16] * len(bufs)
    own_half = lambda bufs, got, wires: [
        _add_own_half(b, g, (first_no, N_CHIPS - 1 - mine), core, w) for b, g, w in zip(bufs, got, wires)
    ]
    for_neighbour = lambda bufs, got_a, got1, wires: [
        _add_for_neighbour(b, ga, g1, second_no, core, w) for b, ga, g1, w in zip(bufs, got_a, got1, wires)
    ]
    received = lambda bufs, got_a, got1, got2: [
        _add_received(b, ga, g1, g2, mine, core, core, 2) for b, ga, g1, g2 in zip(bufs, got_a, got1, got2)
    ]

    g_bout = [seg(_wgrad(y_b, dob_b, N_CHIPS, 1))]
    g_bin, swap_o = _wgrad(h1, dz_b, 2, N_CHIPS, _swap_hook(g_bout))
    g_bin = [seg(g_bin)]
    part_o = own_half(g_bout, swap_o, bf16s(g_bout))
    a_args = (z_a, a_ln_w, a_ln_b, wc, wct, bs_t, wout)
    half = t_rows // TM_A_BWD // 2
    first, rode = _layer_a_bwd(
        dx1, *a_args, (0, half), None, _join_hooks(_swap_hook(g_bin), _send_first_hook(part_o))
    )
    swap_i, got1_o = rode[:1], rode[1:]
    part_i = own_half(g_bin, swap_i, bf16s(g_bin))
    mid_o = for_neighbour(g_bout, swap_o, got1_o, bf16s(g_bout))
    second, rode = _layer_a_bwd(
        dx1, *a_args, (half, 2 * half), first[:3], _join_hooks(_send_first_hook(part_i), _send_second_hook(mid_o))
    )
    got1_i, got2_o = rode[:1], rode[1:]
    dz_a, y_a, dob_a = second[:3]
    g_ws, g_bst, g_lnw, g_lnb = (p + q for p, q in zip(first[3:], second[3:]))
    mid_i = for_neighbour(g_bin, swap_i, got1_i, bf16s(g_bin))
    red_o = received(g_bout, swap_o, got1_o, got2_o)
    g_win, rode = _wgrad(h0, dz_a, 2, N_CHIPS, _join_hooks(_send_second_hook(mid_i), _share_hook(red_o)))
    g_win = [seg(g_win)]
    got2_i, gr_bout = rode[:1], rode[1].reshape(b_w_out.shape[1:])
    red_i = received(g_bin, swap_i, got1_i, got2_i)

    small_shapes = [
        (1, d), (1, aw), (1, aw), (A_GROUPS, CHUNK, CHUNK), (A_GROUPS, CHUNK), (B_HEADS, hd, hd), (B_HEADS, hd, hd),
        (d,), (CONV_WIDTH, bw), (1, bw), (1, bw), (1, bw), (1, bw), (1, 1),
    ]
    small = _pack_rows(
        [
            g_nw1, g_lnw, g_lnb, g_ws, g_bst.T, g_gab[:, :, :hd], g_gab[:, :, hd:],
            g_nf, g_cw[:CONV_WIDTH], g_cb, g_gb[:, :bw], g_gb[:, bw:], g_lam, loss_l[:, :1],
        ]
    )
    g_w, wire_w = g_win + [small], [BF16, F32]
    g_wout, rode = _wgrad(y_a, dob_a, N_DEV, 1, _join_hooks(_swap_hook(g_w), _share_hook(red_i)))
    g_wout = seg(g_wout)
    swap_w, gr_bin = rode[:2], rode[2].reshape(b_w_in.shape[1:])

    g_u, wire_u = [g_wout], [BF16]
    part_w = own_half(g_w, swap_w, wire_w)
    dx_args = (dx1, x0, dz_a, nw0, win, TM_A_DX)
    dx_half = t_rows // TM_A_DX // 2
    (gx_first, g_nw0_a), got1_w = _layer_a_bwd_dx(*dx_args, (0, dx_half), None, _send_first_hook(part_w))
    mid_w = for_neighbour(g_w, swap_w, got1_w, wire_w)
    (grad_x, g_nw0_b), rode = _layer_a_bwd_dx(
        *dx_args, (dx_half, 2 * dx_half), gx_first, _join_hooks(_send_second_hook(mid_w), _swap_hook(g_u))
    )
    got2_w, swap_u = rode[:2], rode[2:]
    g_nw0_mine = g_nw0_a + g_nw0_b
    part_u = own_half(g_u, swap_u, wire_u)
    red_w = received(g_win, swap_w[:1], got1_w[:1], got2_w[:1])
    red_small = _add_received(small, swap_w[1], got1_w[1], got2_w[1], mine, core, 2 * mine + core, N_DEV)
    b_names = ("b_w_in", "b_w_out")
    two_d = lambda a: a.reshape(a.shape[-2:])
    bin_out, rode = _adamw(
        two_d(b_w_in), gr_bin, two_d(m_b_w_in), two_d(v_b_w_in),
        _join_hooks(_send_first_hook(part_u), _share_hook(red_w, red_small)),
    )
    got1_u, gr_win, small_r = rode[:1], rode[1].reshape(a_w_in.shape[1:]), rode[2]
    mid_u = for_neighbour(g_u, swap_u, got1_u, wire_u)
    bout_out, got2_u = _adamw(
        two_d(b_w_out), gr_bout, two_d(m_b_w_out), two_d(v_b_w_out), _send_second_hook(mid_u)
    )
    b_out = list(zip(bin_out, bout_out))
    red_wout = received(g_u, swap_u, got1_u, got2_u)
    gr_wout, g_nw0_all = _run_hook(_share_hook(red_wout, None, g_nw0_mine), "share_reduced")
    win_out = _adamw(two_d(a_w_in), gr_win, two_d(m_a_w_in), two_d(v_a_w_in))
    g_nw0 = _sum_partials(g_nw0_all)
    gr_wout = gr_wout.reshape(a_w_out.shape[1:])
    (g_nw1_r, g_a_ln_w, g_a_ln_b, g_a_w_s, g_a_b_s, g_gate_a_w, g_gate_x_w, g_norm_f, gf_cw, gf_cb, gf_gab, gf_gxb,
     gf_lam, loss) = _unpack(small_r.reshape(-1), small_shapes)
    g_norm_w = jnp.concatenate([g_nw0, g_nw1_r], axis=0)
    shard = lambda g: lax.dynamic_slice_in_dim(g, mine * (bw // N_CHIPS), bw // N_CHIPS, axis=1)

    grads = {
        "norm_w": g_norm_w, "a_w_in": gr_win[None], "a_ln_w": g_a_ln_w, "a_ln_b": g_a_ln_b, "a_w_s": g_a_w_s[None],
        "a_b_s": g_a_b_s[None], "a_w_out": gr_wout[None], "b_w_in": gr_bin[None], "b_conv_w": shard(gf_cw)[None],
        "b_conv_b": shard(gf_cb), "b_gate_a_w": g_gate_a_w[None], "b_gate_a_b": shard(gf_gab),
        "b_gate_x_w": g_gate_x_w[None], "b_gate_x_b": shard(gf_gxb), "b_lambda": shard(gf_lam),
        "b_w_out": gr_bout[None], "norm_f_w": g_norm_f,
    }
    names = list(weights)
    delta, new_m, new_v = {}, {}, {}
    delta["a_w_in"], new_m["a_w_in"], new_v["a_w_in"] = win_out
    delta["a_w_out"], new_m["a_w_out"], new_v["a_w_out"] = _adamw(
        two_d(a_w_out), gr_wout, two_d(m_a_w_out), two_d(v_a_w_out)
    )
    small_names = [n for n in names if n not in ("a_w_in", "a_w_out") + b_names]
    at_least_2d = lambda a: a.reshape(1, -1) if a.ndim == 1 else a
    small_out, _ = _adamw_many(
        *[[at_least_2d(src[n]) for n in small_names] for src in (weights, grads, m_in, v_in)], "adamw_small"
    )
    for dst, vals, b_vals in zip((delta, new_m, new_v), small_out, b_out):
        dst.update(zip(small_names, vals))
        dst.update(zip(b_names, b_vals))
    for dst in (delta, new_m, new_v):
        for n in names:
            dst[n] = dst[n].reshape(weights[n].shape)

    return (
        loss.reshape(()),
        grad_x[None],
        *[grads[n] for n in names],
        *[delta[n] for n in names],
        *[new_m[n] for n in names],
        *[new_v[n] for n in names],
    )
```

```python
import jax
import jax.numpy as jnp
from jax import lax
from jax.experimental import pallas as pl
from jax.experimental.pallas import tpu as pltpu

F32 = jnp.float32
BF16 = jnp.bfloat16

RMS_EPS = 1e-6
LN_EPS = 1e-5
RG_C = 8.0
CHUNK = 128
A_GROUPS = 8
B_HEADS = 12
CONV_WIDTH = 4

ADAM_LR = 0.001
ADAM_B1 = 0.9
ADAM_B2 = 0.999
ADAM_EPS = 1e-08
ADAM_WD = 0.01
ADAM_STEP = 10

N_CHIPS = 4
N_DEV = 8
SUBLANES = 8
LANES = 128
V7X_VMEM_BYTES = 64 * 1024 * 1024
VMEM_LIMIT = V7X_VMEM_BYTES * 7 // 8
MESH = pl.DeviceIdType.MESH
ANY = pl.BlockSpec(memory_space=pl.ANY)

TM_FWD = 256
TM_A_BWD = 256
TM_A_DX = 512
SCAN_UNROLL = 4

GELU_C0 = 0.7978845608028654
GELU_C1 = 0.044715


class _Hook:
    def __init__(self, operands, out_shapes, aliases, n_sems, start, finish, middle=None, late=None):
        self.operands, self.out_shapes, self.aliases, self.n_sems = operands, out_shapes, aliases, n_sems
        self.start, self.finish, self.middle, self.late = start, finish, middle, late


class _SemView:
    def __init__(self, base, off):
        self.base, self.off = base, off

    @property
    def at(self):
        return self

    def __getitem__(self, k):
        return self.base.at[self.off + k]


def _join_hooks(*hooks):
    if len(hooks) == 1:
        return hooks[0]
    operands, out_shapes, aliases, spans = [], [], {}, []
    n_sems = 0
    for h in hooks:
        aliases.update({len(operands) + i: len(out_shapes) + o for i, o in h.aliases.items()})
        spans.append((len(operands), len(h.operands), len(out_shapes), len(h.out_shapes), n_sems))
        operands += list(h.operands)
        out_shapes += list(h.out_shapes)
        n_sems += h.n_sems

    def each(which):
        def run(ins, outs, send, recv):
            for h, (i0, ni, o0, no, s0) in zip(hooks, spans):
                step = getattr(h, which)
                if step is not None:
                    step(ins[i0 : i0 + ni], outs[o0 : o0 + no], _SemView(send, s0), _SemView(recv, s0))

        return run

    middle = each("middle") if any(h.middle is not None for h in hooks) else None
    late = each("late") if any(h.late is not None for h in hooks) else None
    return _Hook(operands, out_shapes, aliases, n_sems, each("start"), each("finish"), middle, late)


def _pcall(body, hook=None, **kw):
    if hook is None:
        return pl.pallas_call(body, **kw)
    n_pre = 0
    if "grid_spec" in kw:
        spec = kw.pop("grid_spec")
        n_pre = spec.num_scalar_prefetch
        kw.update(
            grid=tuple(spec.grid), in_specs=list(spec.in_specs), out_specs=list(spec.out_specs),
            scratch_shapes=list(spec.scratch_shapes),
        )
    n_in, n_out = len(kw["in_specs"]), len(kw["out_shape"])
    hi, ho = len(hook.operands), len(hook.out_shapes)
    grid = kw.get("grid", ())

    def wrapped(*refs):
        pre, refs = refs[:n_pre], refs[n_pre:]
        ins, h_in = refs[:n_in], refs[n_in : n_in + hi]
        outs = refs[n_in + hi : n_in + hi + n_out]
        h_out = refs[n_in + hi + n_out : n_in + hi + n_out + ho]
        scratch = refs[n_in + hi + n_out + ho : -2]
        send_sems, recv_sems = refs[-2:]
        if not grid:
            hook.start(h_in, h_out, send_sems, recv_sems)
            if hook.middle is not None:
                hook.middle(h_in, h_out, send_sems, recv_sems)
            body(*pre, *ins, *outs, *scratch)
            if hook.late is not None:
                hook.late(h_in, h_out, send_sems, recv_sems)
            hook.finish(h_in, h_out, send_sems, recv_sems)
            return
        first = pl.program_id(0) == 0
        last = pl.program_id(0) == grid[0] - 1
        for axis in range(1, len(grid)):
            first = jnp.logical_and(first, pl.program_id(axis) == 0)
            last = jnp.logical_and(last, pl.program_id(axis) == grid[axis] - 1)

        @pl.when(first)
        def _():
            hook.start(h_in, h_out, send_sems, recv_sems)

        for when, step in ((hook.middle, grid[0] // 4), (hook.late, grid[0] * 3 // 4)):
            if when is not None:
                assert len(grid) == 1 and grid[0] >= 4

                @pl.when(pl.program_id(0) == step)
                def _(when=when):
                    when(h_in, h_out, send_sems, recv_sems)

        body(*pre, *ins, *outs, *scratch)

        @pl.when(last)
        def _():
            hook.finish(h_in, h_out, send_sems, recv_sems)

    aliases = dict(kw.pop("input_output_aliases", {}))
    aliases.update({n_pre + n_in + i: n_out + o for i, o in hook.aliases.items()})
    kw.update(
        in_specs=list(kw["in_specs"]) + [ANY] * hi,
        out_specs=list(kw["out_specs"]) + [ANY] * ho,
        out_shape=list(kw["out_shape"]) + list(hook.out_shapes),
        scratch_shapes=list(kw.get("scratch_shapes", ()))
        + [pltpu.SemaphoreType.DMA((hook.n_sems,)), pltpu.SemaphoreType.DMA((hook.n_sems,))],
        input_output_aliases=aliases,
    )
    if n_pre:
        kw["grid_spec"] = pltpu.PrefetchScalarGridSpec(
            num_scalar_prefetch=n_pre, grid=kw.pop("grid"), in_specs=kw.pop("in_specs"),
            out_specs=kw.pop("out_specs"), scratch_shapes=kw.pop("scratch_shapes"),
        )
    call = pl.pallas_call(wrapped, **kw)

    def run(*operands):
        outs = call(*operands, *hook.operands)
        return outs[:n_out], outs[n_out:]

    return run


def _run_hook(hook, name):
    def body():
        pass

    return _pcall(body, hook, name=name, in_specs=[], out_specs=[], out_shape=[])()[1]


def _cparams(sem=None):
    return pltpu.CompilerParams(dimension_semantics=sem, vmem_limit_bytes=VMEM_LIMIT)


def _full(shape):
    zeros = (0,) * len(shape)
    return pl.BlockSpec(shape, lambda *_: zeros)


def _scalars(*vals):
    return jnp.stack([jnp.asarray(v, jnp.int32) for v in vals])


def _sigmoid(x):
    return 1.0 / (1.0 + jnp.exp(-x))


def _gelu(x):
    t = jnp.tanh(GELU_C0 * (x + GELU_C1 * (x * x * x)))
    return x * (0.5 * (1.0 + t))


def _gelu_and_grad(x):
    x2 = x * x
    t = jnp.tanh(GELU_C0 * (x + GELU_C1 * (x2 * x)))
    cdf = 0.5 * (1.0 + t)
    return x * cdf, cdf + 0.5 * x * (1.0 - t * t) * (GELU_C0 * (1.0 + 3.0 * GELU_C1 * x2))


def _silu_and_grad(x):
    s = _sigmoid(x)
    return x * s, s * (1.0 + x * (1.0 - s))


def _softplus_neg(lam):
    u = jnp.exp(-jnp.abs(lam))
    w = 1.0 + u
    log1p = jnp.where(w == 1.0, u, jnp.log(w) * (u / jnp.where(w == 1.0, 1.0, w - 1.0)))
    return jnp.maximum(-lam, 0.0) + log1p


def _dot(a, b):
    return jnp.dot(a, b, preferred_element_type=F32)


def _dot_nt(a, b):
    return lax.dot_general(a, b, (((1,), (1,)), ((), ())), preferred_element_type=F32)


def _dot_tn(a, b):
    return lax.dot_general(a, b, (((0,), (0,)), ((), ())), preferred_element_type=F32)


def _shift_down(v, halo, k):
    if k == 0:
        return v
    rolled = pltpu.roll(v, k, 0)
    row = lax.broadcasted_iota(jnp.int32, (SUBLANES, v.shape[1]), 0)
    top = jnp.where(row < k, pltpu.roll(halo, k, 0), rolled[:SUBLANES])
    return jnp.concatenate([top, rolled[SUBLANES:]], axis=0)


def _shift_up(v, head, k):
    if k == 0:
        return v
    n = v.shape[0]
    rolled = pltpu.roll(v, n - k, 0)
    row = lax.broadcasted_iota(jnp.int32, (SUBLANES, v.shape[1]), 0)
    bot = jnp.where(row >= SUBLANES - k, pltpu.roll(head, SUBLANES - k, 0), rolled[n - SUBLANES :])
    return jnp.concatenate([rolled[: n - SUBLANES], bot], axis=0)


def _scan_blocks(a_ref, b_ref, out_ref, carry, n_rows, reverse):
    width = a_ref.shape[1]
    row = lax.broadcasted_iota(jnp.int32, (SUBLANES, width), 0)
    n_blocks = n_rows // SUBLANES

    def block(j, carry):
        i = (n_blocks - 1 - j) if reverse else j
        r0 = pl.multiple_of(i * SUBLANES, SUBLANES)
        a = a_ref[pl.ds(r0, SUBLANES), :]
        b = b_ref[pl.ds(r0, SUBLANES), :]
        for d in (1, 2, 4):
            shift = (SUBLANES - d) if reverse else d
            keep = (row < SUBLANES - d) if reverse else (row >= d)
            a_s = pltpu.roll(a, shift, 0)
            b_s = pltpu.roll(b, shift, 0)
            b = jnp.where(keep, a * b_s + b, b)
            a = jnp.where(keep, a * a_s, a)
        h = a * carry + b
        out_ref[pl.ds(r0, SUBLANES), :] = h
        edge = h[0:1, :] if reverse else h[SUBLANES - 1 : SUBLANES, :]
        return jnp.broadcast_to(edge, (SUBLANES, width))

    return lax.fori_loop(0, n_blocks, block, carry, unroll=SCAN_UNROLL)


def _rms_fwd(x, w):
    r = lax.rsqrt(jnp.mean(x * x, axis=-1, keepdims=True) + RMS_EPS)
    xh = x * r
    return xh * w, xh, r


def _rms_bwd(dh, xh, r, w):
    dxh = dh * w
    dx = r * (dxh - xh * jnp.mean(dxh * xh, axis=-1, keepdims=True))
    return dx, jnp.sum(dh * xh, axis=0, keepdims=True)


def _cast_to_segments(w, mine, rows, hook=None):
    n, c = w.shape
    per = n // 2 // rows

    def body(k_ref, w_ref, o_ref):
        o_ref[...] = w_ref[...].astype(BF16)

    out = _pcall(
        body,
        hook,
        name=f"cast_{n}x{c}",
        grid_spec=pltpu.PrefetchScalarGridSpec(
            num_scalar_prefetch=1,
            grid=(n // rows,),
            in_specs=[pl.BlockSpec((rows, c), lambda i, k_ref: (i, 0))],
            out_specs=[pl.BlockSpec((None, rows, c), lambda i, k_ref: (2 * k_ref[0] + i // per, i % per, 0))],
        ),
        out_shape=[jax.ShapeDtypeStruct((N_DEV, n // 2, c), BF16)],
        compiler_params=_cparams(("arbitrary",)),
    )(_scalars(mine), w)
    return out[0] if hook is None else (out[0][0], out[1])


def _place():
    x, y, c = lax.axis_index("x"), lax.axis_index("y"), lax.axis_index("c")
    chips = [(1 - x, y), (x, 1 - y), (1 - x, 1 - y)]
    return x, y, c, chips


def _chip_no(chip):
    return 2 * chip[0] + chip[1]


def _rcopy(src, dst, send_sem, recv_sem, to):
    return pltpu.make_async_remote_copy(
        src_ref=src, dst_ref=dst, send_sem=send_sem, recv_sem=recv_sem, device_id=to, device_id_type=MESH
    )


def _gather_hook(big, small=None):
    nb = len(big)
    n_sems = 6 * nb + 4

    def places():
        x, y, c, chips = _place()
        first = (x ^ (1 - c), y ^ c)
        second = (x ^ c, y ^ (1 - c))
        return x, y, c, chips, first, second, (1 - x, 1 - y)

    def seg(outs, b, chip, half):
        return outs[b].at[2 * _chip_no(chip) + half]

    def step1(outs, send, recv):
        x, y, c, _, first, _, _ = places()
        return [
            _rcopy(seg(outs, b, (x, y), c), seg(outs, b, (x, y), c), send.at[6 * b], recv.at[6 * b], (*first, c))
            for b in range(nb)
        ]

    def step2(outs, send, recv):
        x, y, c, _, first, second, _ = places()
        copies = []
        for b in range(nb):
            for k, chip in ((1, (x, y)), (2, first)):
                src = seg(outs, b, chip, c)
                copies.append(_rcopy(src, src, send.at[6 * b + k], recv.at[6 * b + k], (*second, c)))
        return copies

    def hand_over(outs, send, recv, k, chip):
        x, y, c, *_ = places()
        return [
            _rcopy(seg(outs, b, chip, c), seg(outs, b, chip, c), send.at[6 * b + k], recv.at[6 * b + k], (x, y, 1 - c))
            for b in range(nb)
        ]

    def wait_landed(outs, send, recv, k, chip, half):
        x, y, c, *_ = places()
        for b in range(nb):
            got = seg(outs, b, chip, half)
            _rcopy(got, got, send.at[6 * b + k], recv.at[6 * b + k], (x, y, c)).wait_recv()

    def small_copies(ins, outs, send, recv):
        x, y, c, chips, *_ = places()
        there = outs[nb].at[_chip_no((x, y))]
        return [
            _rcopy(ins[nb], there, send.at[6 * nb + j], recv.at[6 * nb + j], (*chip, c)) for j, chip in enumerate(chips)
        ]

    def local_copy(ins, outs, send):
        x, y, _, _ = _place()
        return pltpu.make_async_copy(ins[nb], outs[nb].at[_chip_no((x, y))], send.at[6 * nb + 3])

    def start(ins, outs, send, recv):
        for cp in step1(outs, send, recv):
            cp.start()
        if small is not None:
            for cp in small_copies(ins, outs, send, recv):
                cp.start()
            local_copy(ins, outs, send).start()

    def middle(ins, outs, send, recv):
        *_, first, _, _ = places()
        wait_landed(outs, send, recv, 0, first, places()[2])
        for cp in step2(outs, send, recv) + hand_over(outs, send, recv, 3, first):
            cp.start()

    def late(ins, outs, send, recv):
        x, y, c, chips, first, second, diagonal = places()
        for k, chip in ((1, second), (2, diagonal)):
            wait_landed(outs, send, recv, k, chip, c)
            for cp in hand_over(outs, send, recv, 3 + k, chip):
                cp.start()

    def finish(ins, outs, send, recv):
        x, y, c, chips, first, second, diagonal = places()
        wait_landed(outs, send, recv, 3, second, 1 - c)
        wait_landed(outs, send, recv, 4, first, 1 - c)
        wait_landed(outs, send, recv, 5, diagonal, 1 - c)
        sent = step1(outs, send, recv) + step2(outs, send, recv)
        for k, chip in ((3, first), (4, second), (5, diagonal)):
            sent += hand_over(outs, send, recv, k, chip)
        for cp in sent:
            cp.wait_send()
        if small is not None:
            for j, chip in enumerate(chips):
                got = outs[nb].at[_chip_no(chip)]
                _rcopy(got, got, send.at[6 * nb + j], recv.at[6 * nb + j], (x, y, c)).wait_recv()
            for cp in small_copies(ins, outs, send, recv):
                cp.wait_send()
            local_copy(ins, outs, send).wait()

    operands = list(big) + ([small] if small is not None else [])
    out_shapes = [jax.ShapeDtypeStruct(b.shape, b.dtype) for b in big]
    if small is not None:
        out_shapes.append(jax.ShapeDtypeStruct((N_CHIPS, *small.shape), small.dtype))
    return _Hook(operands, out_shapes, {b: b for b in range(nb)}, n_sems, start, finish, middle, late)


def _both_ways_hook(operands, out_shapes, copies_of, n_sems):
    def start(ins, outs, send, recv):
        for cp in copies_of(ins, outs, send, recv):
            cp.start()

    def finish(ins, outs, send, recv):
        for cp in copies_of(ins, outs, send, recv):
            cp.wait()

    return _Hook(operands, out_shapes, {}, n_sems, start, finish)


def _swap_hook(bufs):
    def copies_of(ins, outs, send, recv):
        x, y, c, _ = _place()
        copies = []
        for b in range(len(bufs)):
            for j in range(N_CHIPS):
                k = b * N_CHIPS + j
                copies.append(_rcopy(ins[b].at[2 * j + 1 - c], outs[b].at[j], send.at[k], recv.at[k], (x, y, 1 - c)))
        return copies

    out_shapes = [jax.ShapeDtypeStruct((N_CHIPS, *b.shape[1:]), b.dtype) for b in bufs]
    return _both_ways_hook(list(bufs), out_shapes, copies_of, len(bufs) * N_CHIPS)


def _axis_order():
    x, y, c, _ = _place()
    return (x, y), c, (x ^ (1 - c), y ^ c), (x ^ c, y ^ (1 - c)), (1 - x, 1 - y)


def _send_first_hook(parts):
    def copies_of(ins, outs, send, recv):
        _, c, first, _, _ = _axis_order()
        copies = []
        for b in range(len(parts)):
            for k in range(2):
                sem = 2 * b + k
                copies.append(_rcopy(ins[b].at[k], outs[b].at[k], send.at[sem], recv.at[sem], (*first, c)))
        return copies

    out_shapes = [jax.ShapeDtypeStruct((2, *p.shape[1:]), p.dtype) for p in parts]
    return _both_ways_hook(list(parts), out_shapes, copies_of, len(parts) * 2)


def _send_second_hook(mids):
    def copies_of(ins, outs, send, recv):
        _, c, _, second, _ = _axis_order()
        return [_rcopy(ins[b], outs[b], send.at[b], recv.at[b], (*second, c)) for b in range(len(mids))]

    out_shapes = [jax.ShapeDtypeStruct(m.shape, m.dtype) for m in mids]
    return _both_ways_hook(list(mids), out_shapes, copies_of, len(mids))


def _share_hook(big, small=None, tiny=None):
    nb = len(big)
    n_sems = nb + 7 + N_DEV
    t0 = nb + 7

    def tiny_copies(ins, outs, send, recv):
        x, y, c, _ = _place()
        there = outs[-1].at[2 * _chip_no((x, y)) + c]
        copies = []
        for r in range(1, N_DEV):
            to = (x ^ (r >> 2 & 1), y ^ (r >> 1 & 1), c ^ (r & 1))
            copies.append(_rcopy(ins[-1], there, send.at[t0 + r], recv.at[t0 + r], to))
        return copies

    def tiny_local(ins, outs, send):
        x, y, c, _ = _place()
        return pltpu.make_async_copy(ins[-1], outs[-1].at[2 * _chip_no((x, y)) + c], send.at[t0])

    def first_copies(outs, send, recv):
        x, y, c, chips = _place()
        sibling = (x, y, 1 - c)
        copies = [_rcopy(outs[b].at[c], outs[b].at[c], send.at[b], recv.at[b], sibling) for b in range(nb)]
        if small is not None:
            own = outs[nb].at[2 * _chip_no((x, y)) + c]
            copies.append(_rcopy(own, own, send.at[nb], recv.at[nb], sibling))
            for j, chip in enumerate(chips):
                copies.append(_rcopy(own, own, send.at[nb + 1 + j], recv.at[nb + 1 + j], (*chip, c)))
        return copies

    def start(ins, outs, send, recv):
        for cp in first_copies(outs, send, recv):
            cp.start()
        if tiny is not None:
            for cp in tiny_copies(ins, outs, send, recv):
                cp.start()
            tiny_local(ins, outs, send).start()

    def finish(ins, outs, send, recv):
        x, y, c, chips = _place()
        me, sibling = (x, y, c), (x, y, 1 - c)
        if tiny is not None:
            for cp in tiny_copies(ins, outs, send, recv):
                cp.wait()
            tiny_local(ins, outs, send).wait()
        passed = []
        if small is not None:
            for j, chip in enumerate(chips):
                got = outs[nb].at[2 * _chip_no(chip) + c]
                _rcopy(got, got, send.at[nb + 1 + j], recv.at[nb + 1 + j], me).wait_recv()
                fwd = _rcopy(got, got, send.at[nb + 4 + j], recv.at[nb + 4 + j], sibling)
                fwd.start()
                passed.append(fwd)
        for b in range(nb):
            got = outs[b].at[1 - c]
            _rcopy(got, got, send.at[b], recv.at[b], me).wait_recv()
        if small is not None:
            got = outs[nb].at[2 * _chip_no((x, y)) + 1 - c]
            _rcopy(got, got, send.at[nb], recv.at[nb], me).wait_recv()
            for j, chip in enumerate(chips):
                got = outs[nb].at[2 * _chip_no(chip) + 1 - c]
                _rcopy(got, got, send.at[nb + 4 + j], recv.at[nb + 4 + j], me).wait_recv()
        for cp in first_copies(outs, send, recv) + passed:
            cp.wait_send()

    operands = list(big) + ([small] if small is not None else [])
    out_shapes = [jax.ShapeDtypeStruct(a.shape, a.dtype) for a in operands]
    aliases = {i: i for i in range(len(operands))}
    if tiny is not None:
        operands.append(tiny)
        out_shapes.append(jax.ShapeDtypeStruct((N_DEV, *tiny.shape), tiny.dtype))
    return _Hook(operands, out_shapes, aliases, n_sems, start, finish)


def _row_tile(rows, cols, target_bytes=2 * 1024 * 1024):
    best = SUBLANES
    for t in range(SUBLANES, rows + 1, SUBLANES):
        if rows % t == 0 and t * cols * 4 <= target_bytes:
            best = t
    return best


def _add_own_half(buf, got, owners, c, wire):
    _, rows, cols = buf.shape
    tr = _row_tile(rows, cols)

    def body(s_ref, a_ref, b_ref, o_ref):
        o_ref[...] = (a_ref[...] + b_ref[...]).astype(wire)

    return _pcall(
        body,
        name=f"add_own_half_{rows}x{cols}",
        grid_spec=pltpu.PrefetchScalarGridSpec(
            num_scalar_prefetch=1,
            grid=(2, rows // tr),
            in_specs=[
                pl.BlockSpec((None, None, tr, cols), lambda j, r, s_ref: (s_ref[j], s_ref[2], r, 0)),
                pl.BlockSpec((None, tr, cols), lambda j, r, s_ref: (s_ref[j], r, 0)),
            ],
            out_specs=pl.BlockSpec((None, tr, cols), lambda j, r, s_ref: (j, r, 0)),
        ),
        out_shape=jax.ShapeDtypeStruct((2, rows, cols), wire),
        compiler_params=_cparams(("arbitrary", "arbitrary")),
    )(_scalars(owners[0], owners[1], c), buf.reshape(N_CHIPS, 2, rows, cols), got)


def _add_for_neighbour(buf, got_a, got1, second, c, wire):
    _, rows, cols = buf.shape
    tr = _row_tile(rows, cols)

    def body(s_ref, x_ref, a_ref, g_ref, o_ref):
        o_ref[...] = ((x_ref[...] + a_ref[...]) + g_ref[...].astype(F32)).astype(wire)

    return _pcall(
        body,
        name=f"add_for_neighbour_{rows}x{cols}",
        grid_spec=pltpu.PrefetchScalarGridSpec(
            num_scalar_prefetch=1,
            grid=(rows // tr,),
            in_specs=[
                pl.BlockSpec((None, None, tr, cols), lambda r, s_ref: (s_ref[0], s_ref[1], r, 0)),
                pl.BlockSpec((None, tr, cols), lambda r, s_ref: (s_ref[0], r, 0)),
                pl.BlockSpec((None, tr, cols), lambda r, s_ref: (1, r, 0)),
            ],
            out_specs=pl.BlockSpec((tr, cols), lambda r, s_ref: (r, 0)),
        ),
        out_shape=jax.ShapeDtypeStruct((rows, cols), wire),
        compiler_params=_cparams(("arbitrary",)),
    )(_scalars(second, c), buf.reshape(N_CHIPS, 2, rows, cols), got_a, got1)


def _add_received(buf, got_a, got1, got2, mine, c, slot, n_slots):
    _, rows, cols = buf.shape
    tr = _row_tile(rows, cols)

    def body(s_ref, x_ref, a_ref, g1_ref, g2_ref, o_ref):
        own = x_ref[...] + a_ref[...]
        o_ref[...] = (own + g1_ref[...].astype(F32)) + g2_ref[...].astype(F32)

    return _pcall(
        body,
        name=f"add_received_{rows}x{cols}",
        grid_spec=pltpu.PrefetchScalarGridSpec(
            num_scalar_prefetch=1,
            grid=(rows // tr,),
            in_specs=[
                pl.BlockSpec((None, None, tr, cols), lambda r, s_ref: (s_ref[0], s_ref[1], r, 0)),
                pl.BlockSpec((None, tr, cols), lambda r, s_ref: (s_ref[0], r, 0)),
                pl.BlockSpec((None, tr, cols), lambda r, s_ref: (0, r, 0)),
                pl.BlockSpec((tr, cols), lambda r, s_ref: (r, 0)),
            ],
            out_specs=pl.BlockSpec((None, tr, cols), lambda r, s_ref: (s_ref[2], r, 0)),
        ),
        out_shape=jax.ShapeDtypeStruct((n_slots, rows, cols), F32),
        compiler_params=_cparams(("arbitrary",)),
    )(_scalars(mine, c, slot), buf.reshape(N_CHIPS, 2, rows, cols), got_a, got1, got2)


def _layer_a_fwd(x, nw, win, ln_w, ln_b, wc, bs_t, wout, tm, hook):
    t_rows, d = x.shape
    n_sh, _, s_cols = win.shape
    aw = wout.shape[0]
    gd = aw // A_GROUPS
    tn = 512
    assert s_cols % tn == 0 and aw % tn == 0 and tm % CHUNK == 0

    def body(x_ref, nw_ref, win_ref, lnw_ref, lnb_ref, wc_ref, bst_ref, wout_ref, z_ref, x1_ref, h_ref, u_s, v_s, y_s):
        x = x_ref[...]
        h, _, _ = _rms_fwd(x, nw_ref[...])
        h = h.astype(BF16)
        h_ref[...] = h
        for j in range(3 * aw // tn):
            k, off = divmod(j * tn, s_cols)
            cols = slice((j * tn) % aw, (j * tn) % aw + tn)
            zj = _dot(h, win_ref[k, :, off : off + tn])
            z_ref[:, j * tn : (j + 1) * tn] = zj
            if j * tn < aw:
                u_s[:, cols] = _gelu(zj)
            elif j * tn < 2 * aw:
                v_s[:, cols] = _gelu(zj)
            else:
                u_s[:, cols] = u_s[:, cols] * (zj * _sigmoid(zj))
        v = v_s[...]
        mu = jnp.mean(v, axis=-1, keepdims=True)
        vc = v - mu
        rstd = lax.rsqrt(jnp.mean(vc * vc, axis=-1, keepdims=True) + LN_EPS)
        v_s[...] = (vc * rstd) * lnw_ref[...] + lnb_ref[...]
        for ck in range(tm // CHUNK):
            rows = slice(ck * CHUNK, (ck + 1) * CHUNK)
            for g in range(A_GROUPS):
                cols = slice(g * gd, (g + 1) * gd)
                s = _dot(wc_ref[g], v_s[rows, cols].astype(BF16)) + bst_ref[:, g : g + 1]
                y_s[rows, cols] = (u_s[rows, cols] * s).astype(BF16)
        x1_ref[...] = x + _dot(y_s[...], wout_ref[...])

    row = lambda i: (i, 0)
    return _pcall(
        body,
        hook,
        name="layer_a_fwd",
        grid=(t_rows // tm,),
        in_specs=[
            pl.BlockSpec((tm, d), row),
            _full(nw.shape),
            _full(win.shape),
            _full(ln_w.shape),
            _full(ln_b.shape),
            _full(wc.shape),
            _full(bs_t.shape),
            _full(wout.shape),
        ],
        out_specs=[pl.BlockSpec((tm, 3 * aw), row), pl.BlockSpec((tm, d), row), pl.BlockSpec((tm, d), row)],
        out_shape=[
            jax.ShapeDtypeStruct((t_rows, 3 * aw), F32),
            jax.ShapeDtypeStruct((t_rows, d), F32),
            jax.ShapeDtypeStruct((t_rows, d), BF16),
        ],
        scratch_shapes=[pltpu.VMEM((tm, aw), F32), pltpu.VMEM((tm, aw), F32), pltpu.VMEM((tm, aw), BF16)],
        compiler_params=_cparams(("arbitrary",)),
    )(x, nw, win, ln_w, ln_b, wc, bs_t, wout)


def _layer_a_bwd(dout, z, ln_w, ln_b, wc, wct, bs_t, wout, tiles, earlier, hook):
    t_rows, d = dout.shape
    aw = wout.shape[0]
    gd = aw // A_GROUPS
    tm = TM_A_BWD
    lo, hi = tiles
    n_earlier = 0 if earlier is None else len(earlier)

    def body(dout_ref, z_ref, lnw_ref, lnb_ref, wc_ref, wct_ref, bst_ref, wout_ref, *rest):
        dz_ref, y_ref, dob_ref, gws_ref, gbs_ref, glnw_ref, glnb_ref, u_s, vh_s, ds_s, dvn_s = rest[n_earlier:]

        @pl.when(pl.program_id(0) == 0)
        def _():
            gws_ref[...] = jnp.zeros_like(gws_ref)
            gbs_ref[...] = jnp.zeros_like(gbs_ref)
            glnw_ref[...] = jnp.zeros_like(glnw_ref)
            glnb_ref[...] = jnp.zeros_like(glnb_ref)

        dob = dout_ref[...].astype(BF16)
        dob_ref[...] = dob
        dy = _dot_nt(dob, wout_ref[...])

        zv = z_ref[:, aw : 2 * aw]
        vg, dvg_dz = _gelu_and_grad(zv)
        mu = jnp.mean(vg, axis=-1, keepdims=True)
        vc = vg - mu
        rstd = lax.rsqrt(jnp.mean(vc * vc, axis=-1, keepdims=True) + LN_EPS)
        vh = vc * rstd
        vh_s[...] = vh
        vn = (vh * lnw_ref[...] + lnb_ref[...]).astype(BF16)

        zu = z_ref[:, 0:aw]
        zg = z_ref[:, 2 * aw : 3 * aw]
        u, du_dz = _gelu_and_grad(zu)
        sg, dsg = _silu_and_grad(zg)
        u_s[...] = u * sg
        tril = lax.broadcasted_iota(jnp.int32, (CHUNK, CHUNK), 0) >= lax.broadcasted_iota(jnp.int32, (CHUNK, CHUNK), 1)
        for ck in range(tm // CHUNK):
            rows = slice(ck * CHUNK, (ck + 1) * CHUNK)
            for g in range(A_GROUPS):
                cols = slice(g * gd, (g + 1) * gd)
                vn_g = vn[rows, cols]
                s = _dot(wc_ref[g], vn_g) + bst_ref[:, g : g + 1]
                usg = u_s[rows, cols]
                dy_g = dy[rows, cols]
                y_ref[rows, cols] = (usg * s).astype(BF16)
                ds = dy_g * usg
                ds_s[rows, cols] = dy_g * s
                gbs_ref[:, g : g + 1] += jnp.sum(ds, axis=-1, keepdims=True)
                dsb = ds.astype(BF16)
                gws_ref[g] += jnp.where(tril, _dot_nt(dsb, vn_g), 0.0)
                dvn_s[rows, cols] = _dot(wct_ref[g], dsb)
        dusg = ds_s[...]
        dz_ref[:, 0:aw] = (dusg * sg * du_dz).astype(BF16)
        dz_ref[:, 2 * aw : 3 * aw] = (dusg * u * dsg).astype(BF16)

        dvn = dvn_s[...]
        vh = vh_s[...]
        glnw_ref[...] += jnp.sum(dvn * vh, axis=0, keepdims=True)
        glnb_ref[...] += jnp.sum(dvn, axis=0, keepdims=True)
        dvh = dvn * lnw_ref[...]
        dvg = rstd * (dvh - jnp.mean(dvh, axis=-1, keepdims=True) - vh * jnp.mean(dvh * vh, axis=-1, keepdims=True))
        dz_ref[:, aw : 2 * aw] = (dvg * dvg_dz).astype(BF16)

    row = lambda i: (i + lo, 0)
    call = _pcall(
        body,
        hook,
        name=f"layer_a_bwd_{lo}",
        grid=(hi - lo,),
        in_specs=[
            pl.BlockSpec((tm, d), row),
            pl.BlockSpec((tm, 3 * aw), row),
            _full(ln_w.shape),
            _full(ln_b.shape),
            _full(wc.shape),
            _full(wct.shape),
            _full(bs_t.shape),
            _full(wout.shape),
        ]
        + [ANY] * n_earlier,
        out_specs=[
            pl.BlockSpec((tm, 3 * aw), row),
            pl.BlockSpec((tm, aw), row),
            pl.BlockSpec((tm, d), row),
            _full((A_GROUPS, CHUNK, CHUNK)),
            _full((CHUNK, A_GROUPS)),
            _full((1, aw)),
            _full((1, aw)),
        ],
        out_shape=[
            jax.ShapeDtypeStruct((t_rows, 3 * aw), BF16),
            jax.ShapeDtypeStruct((t_rows, aw), BF16),
            jax.ShapeDtypeStruct((t_rows, d), BF16),
            jax.ShapeDtypeStruct((A_GROUPS, CHUNK, CHUNK), F32),
            jax.ShapeDtypeStruct((CHUNK, A_GROUPS), F32),
            jax.ShapeDtypeStruct((1, aw), F32),
            jax.ShapeDtypeStruct((1, aw), F32),
        ],
        scratch_shapes=[pltpu.VMEM((tm, aw), F32)] * 4,
        input_output_aliases={8 + i: i for i in range(n_earlier)},
        compiler_params=_cparams(("arbitrary",)),
    )
    return call(dout, z, ln_w, ln_b, wc, wct, bs_t, wout, *(earlier or ()))


def _layer_a_bwd_dx(dout, x, dz, nw, win, tm, tiles, earlier, hook):
    t_rows, d = x.shape
    n_sh, _, s_cols = win.shape
    lo, hi = tiles
    n_earlier = 0 if earlier is None else 1

    def body(dout_ref, x_ref, dz_ref, nw_ref, win_ref, *rest):
        gx_ref, gnw_ref = rest[n_earlier:]

        @pl.when(pl.program_id(0) == 0)
        def _():
            gnw_ref[...] = jnp.zeros_like(gnw_ref)

        dh = jnp.zeros((tm, d), F32)
        for k in range(n_sh):
            dh = dh + _dot_nt(dz_ref[:, k * s_cols : (k + 1) * s_cols], win_ref[k])
        nw = nw_ref[...]
        _, xh, r = _rms_fwd(x_ref[...], nw)
        dx, gnw = _rms_bwd(dh, xh, r, nw)
        gnw_ref[0:1, :] += gnw
        gx_ref[...] = dout_ref[...] + dx

    row = lambda i: (i + lo, 0)
    return _pcall(
        body,
        hook,
        name=f"layer_a_bwd_dx_{lo}",
        grid=(hi - lo,),
        in_specs=[
            pl.BlockSpec((tm, d), row),
            pl.BlockSpec((tm, d), row),
            pl.BlockSpec((tm, n_sh * s_cols), row),
            _full(nw.shape),
            _full(win.shape),
        ]
        + [ANY] * n_earlier,
        out_specs=[pl.BlockSpec((tm, d), row), _full((SUBLANES, d))],
        out_shape=[jax.ShapeDtypeStruct((t_rows, d), F32), jax.ShapeDtypeStruct((SUBLANES, d), F32)],
        input_output_aliases={5: 0} if n_earlier else {},
        compiler_params=_cparams(("arbitrary",)),
    )(dout, x, dz, nw, win, *([earlier] if n_earlier else []))


def _decay(r, sp_h):
    log_a = (-RG_C) * r * sp_h
    a = jnp.exp(log_a)
    mult = jnp.sqrt(jnp.tanh(-log_a) * (a * a + 1.0))
    return a, mult


def _gates(xc_h, gab_ref, gb_ref, sp_h, h, hd):
    pre = _dot(xc_h.astype(BF16), gab_ref[h])
    bw = gb_ref.shape[1] // 2
    r = _sigmoid(pre[:, :hd] + gb_ref[:, h * hd : (h + 1) * hd])
    ig = _sigmoid(pre[:, hd:] + gb_ref[:, bw + h * hd : bw + (h + 1) * hd])
    a, mult = _decay(r, sp_h)
    return r, ig, a, mult


def _conv(xb, halo, cw_ref, cb_ref):
    xc = cb_ref[...] + cw_ref[CONV_WIDTH - 1 : CONV_WIDTH, :] * xb
    for k in range(CONV_WIDTH - 1):
        xc = xc + cw_ref[k : k + 1, :] * _shift_down(xb, halo, CONV_WIDTH - 1 - k)
    return xc


def _layer_b_fwd(x1, nw, bin_w, cw, cb, gab, gb, lam, bout, nf, tgt, tm):
    t_rows, d = x1.shape
    bw = bout.shape[0]
    hd = bw // B_HEADS
    nt = t_rows // tm

    def body(
        x1_ref, nw_ref, bin_ref, cw_ref, cb_ref, gab_ref, gb_ref, lam_ref, bout_ref, nf_ref, tgt_ref,
        z_ref, h_ref, h1_ref, xbt_ref, ht_ref, dx2_ref, loss_ref, gnf_ref,
        tail_s, carry_s, a_s, b_s, hs_s, acc_s,
    ):
        @pl.when(pl.program_id(0) == 0)
        def _():
            tail_s[...] = jnp.zeros_like(tail_s)
            carry_s[...] = jnp.zeros_like(carry_s)
            acc_s[...] = jnp.zeros_like(acc_s)
            gnf_ref[...] = jnp.zeros_like(gnf_ref)

        x1 = x1_ref[...]
        h1, _, _ = _rms_fwd(x1, nw_ref[...])
        h1 = h1.astype(BF16)
        h1_ref[...] = h1
        z = jnp.concatenate([_dot(h1, bin_ref[k]) for k in range(N_CHIPS)], axis=1)
        z_ref[...] = z
        xb = z[:, :bw]
        xc = _conv(xb, tail_s[...], cw_ref, cb_ref)
        tail = xb[tm - SUBLANES :, :]
        tail_s[...] = tail
        xbt_ref[...] = tail
        sp = _softplus_neg(lam_ref[...])
        for h in range(B_HEADS):
            cols = slice(h * hd, (h + 1) * hd)
            xc_h = xc[:, cols]
            _, ig, a, mult = _gates(xc_h, gab_ref, gb_ref, sp[:, cols], h, hd)
            a_s[:, cols] = a
            b_s[:, cols] = mult * (ig * xc_h)
        carry = _scan_blocks(a_s, b_s, hs_s, carry_s[...], tm, reverse=False)
        carry_s[...] = carry
        ht_ref[...] = hs_s[tm - SUBLANES :, :]
        hs = hs_s[...]
        h_ref[...] = hs
        g = z[:, bw:]
        y = (hs * (g * _sigmoid(g))).astype(BF16)
        x2 = x1 + _dot(y, bout_ref[...])

        nf = nf_ref[...]
        o, xh, r = _rms_fwd(x2, nf)
        diff = o - tgt_ref[...]
        acc_s[...] += jnp.sum(diff * diff, axis=0, keepdims=True)
        do = diff * (1.0 / d)
        dx2, gnf = _rms_bwd(do, xh, r, nf)
        gnf_ref[...] += gnf
        dx2_ref[...] = dx2

        @pl.when(pl.program_id(0) == nt - 1)
        def _():
            total = jnp.sum(acc_s[...], axis=-1, keepdims=True) * (0.5 / d)
            loss_ref[...] = jnp.broadcast_to(total, loss_ref.shape)

    row = lambda i: (i, 0)
    return _pcall(
        body,
        name="layer_b_fwd",
        grid=(nt,),
        in_specs=[
            pl.BlockSpec((tm, d), row),
            _full(nw.shape),
            _full(bin_w.shape),
            _full(cw.shape),
            _full(cb.shape),
            _full(gab.shape),
            _full(gb.shape),
            _full(lam.shape),
            _full(bout.shape),
            _full(nf.shape),
            pl.BlockSpec((tm, d), row),
        ],
        out_specs=[
            pl.BlockSpec((tm, 2 * bw), row),
            pl.BlockSpec((tm, bw), row),
            pl.BlockSpec((tm, d), row),
            pl.BlockSpec((None, SUBLANES, bw), lambda i: (i, 0, 0)),
            pl.BlockSpec((None, SUBLANES, bw), lambda i: (i, 0, 0)),
            pl.BlockSpec((tm, d), row),
            _full((1, LANES)),
            _full((1, d)),
        ],
        out_shape=[
            jax.ShapeDtypeStruct((t_rows, 2 * bw), F32),
            jax.ShapeDtypeStruct((t_rows, bw), F32),
            jax.ShapeDtypeStruct((t_rows, d), BF16),
            jax.ShapeDtypeStruct((nt, SUBLANES, bw), F32),
            jax.ShapeDtypeStruct((nt, SUBLANES, bw), F32),
            jax.ShapeDtypeStruct((t_rows, d), F32),
            jax.ShapeDtypeStruct((1, LANES), F32),
            jax.ShapeDtypeStruct((1, d), F32),
        ],
        scratch_shapes=[
            pltpu.VMEM((SUBLANES, bw), F32),
            pltpu.VMEM((SUBLANES, bw), F32),
            pltpu.VMEM((tm, bw), F32),
            pltpu.VMEM((tm, bw), F32),
            pltpu.VMEM((tm, bw), F32),
            pltpu.VMEM((1, d), F32),
        ],
        compiler_params=_cparams(("arbitrary",)),
    )(x1, nw, bin_w, cw, cb, gab, gb, lam, bout, nf, tgt)


def _layer_b_bwd(dout, x1, z, hseq, xb_tails, h_tails, nw, bin_w, cw, cb, gab, gabt, gb, lam, bout, tm):
    t_rows, d = x1.shape
    bw = bout.shape[0]
    hd = bw // B_HEADS
    nt = t_rows // tm

    def body(
        dout_ref, x1_ref, z_ref, h_ref, xbt_ref, ht_ref, nw_ref, bin_ref, cw_ref, cb_ref, gab_ref, gabt_ref,
        gb_ref, lam_ref, bout_ref,
        dx1_ref, dz_ref, y_ref, dob_ref, ggab_ref, ggb_ref, gcw_ref, gcb_ref, glam_ref, gnw_ref,
        gcarry_s, afirst_s, head_s, aup_s, dh_s, gt_s, dxc_s, xc_s, r_s, ig_s,
    ):
        step = pl.program_id(0)
        tile = nt - 1 - step

        @pl.when(step == 0)
        def _():
            for ref in (ggab_ref, ggb_ref, gcw_ref, gcb_ref, glam_ref, gnw_ref, gcarry_s, afirst_s, head_s):
                ref[...] = jnp.zeros_like(ref)

        first_tile = tile == 0
        xb_halo = jnp.where(first_tile, 0.0, xbt_ref[...])
        h_halo = jnp.where(first_tile, 0.0, ht_ref[...])

        dout = dout_ref[...]
        dob = dout.astype(BF16)
        dob_ref[...] = dob
        dy = _dot_nt(dob, bout_ref[...])
        hs = h_ref[...]
        g = z_ref[:, bw:]
        sg, dsg = _silu_and_grad(g)
        y_ref[...] = (hs * sg).astype(BF16)
        dz_ref[:, bw:] = (dy * hs * dsg).astype(BF16)
        dh_s[...] = dy * sg

        xb = z_ref[:, :bw]
        xc = _conv(xb, xb_halo, cw_ref, cb_ref)
        xc_s[...] = xc
        lam = lam_ref[...]
        sp = _softplus_neg(lam)
        for h in range(B_HEADS):
            cols = slice(h * hd, (h + 1) * hd)
            r, ig, a, _ = _gates(xc[:, cols], gab_ref, gb_ref, sp[:, cols], h, hd)
            r_s[:, cols] = r
            ig_s[:, cols] = ig
            aup_s[:, cols] = _shift_up(a, afirst_s[:, cols], 1)
            afirst_s[:, cols] = jnp.broadcast_to(a[0:1, :], (SUBLANES, hd))
        carry = _scan_blocks(aup_s, dh_s, gt_s, gcarry_s[...], tm, reverse=True)
        gcarry_s[...] = carry

        h_prev = _shift_down(hs, h_halo, 1)
        for h in range(B_HEADS):
            cols = slice(h * hd, (h + 1) * hd)
            xc_h = xc_s[:, cols]
            sp_h = sp[:, cols]
            r, ig = r_s[:, cols], ig_s[:, cols]
            a, mult = _decay(r, sp_h)
            gt = gt_s[:, cols]
            da = gt * h_prev[:, cols]
            dmult = gt * (ig * xc_h)
            dig = gt * (mult * xc_h)
            dxc_direct = gt * (mult * ig)
            dla = da * a - dmult * (a * a) / mult
            glam_ref[:, cols] += jnp.sum(dla * r, axis=0, keepdims=True)
            dr = dla * ((-RG_C) * sp_h)
            dpre = jnp.concatenate([dr * r * (1.0 - r), dig * ig * (1.0 - ig)], axis=1)
            ggb_ref[:, cols] += jnp.sum(dpre[:, :hd], axis=0, keepdims=True)
            ggb_ref[:, bw + h * hd : bw + (h + 1) * hd] += jnp.sum(dpre[:, hd:], axis=0, keepdims=True)
            dpb = dpre.astype(BF16)
            ggab_ref[h] += _dot_tn(xc_h.astype(BF16), dpb)
            dxc_s[:, cols] = dxc_direct + _dot(dpb, gabt_ref[h])
        glam_ref[...] = jnp.where(step == nt - 1, glam_ref[...] * (RG_C * _sigmoid(-lam)), glam_ref[...])

        dxc = dxc_s[...]
        gcb_ref[...] += jnp.sum(dxc, axis=0, keepdims=True)
        dxb = cw_ref[CONV_WIDTH - 1 : CONV_WIDTH, :] * dxc
        gcw_ref[CONV_WIDTH - 1 : CONV_WIDTH, :] += jnp.sum(dxc * xb, axis=0, keepdims=True)
        head = head_s[...]
        for k in range(CONV_WIDTH - 1):
            lag = CONV_WIDTH - 1 - k
            dxb = dxb + cw_ref[k : k + 1, :] * _shift_up(dxc, head, lag)
            gcw_ref[k : k + 1, :] += jnp.sum(dxc * _shift_down(xb, xb_halo, lag), axis=0, keepdims=True)
        head_s[...] = dxc[:SUBLANES, :]
        dz_ref[:, :bw] = dxb.astype(BF16)

        s_cols = 2 * bw // N_CHIPS
        dh1 = jnp.zeros((tm, d), F32)
        for k in range(N_CHIPS):
            dh1 = dh1 + _dot_nt(dz_ref[:, k * s_cols : (k + 1) * s_cols], bin_ref[k])
        x1 = x1_ref[...]
        nw = nw_ref[...]
        _, xh, r1 = _rms_fwd(x1, nw)
        dx, gnw = _rms_bwd(dh1, xh, r1, nw)
        gnw_ref[...] += gnw
        dx1_ref[...] = dout + dx

    rev = lambda i: (nt - 1 - i, 0)
    prev = lambda i: (jnp.maximum(nt - 2 - i, 0), 0, 0)
    return _pcall(
        body,
        name="layer_b_bwd",
        grid=(nt,),
        in_specs=[
            pl.BlockSpec((tm, d), rev),
            pl.BlockSpec((tm, d), rev),
            pl.BlockSpec((tm, 2 * bw), rev),
            pl.BlockSpec((tm, bw), rev),
            pl.BlockSpec((None, SUBLANES, bw), prev),
            pl.BlockSpec((None, SUBLANES, bw), prev),
            _full(nw.shape),
            _full(bin_w.shape),
            _full(cw.shape),
            _full(cb.shape),
            _full(gab.shape),
            _full(gabt.shape),
            _full(gb.shape),
            _full(lam.shape),
            _full(bout.shape),
        ],
        out_specs=[
            pl.BlockSpec((tm, d), rev),
            pl.BlockSpec((tm, 2 * bw), rev),
            pl.BlockSpec((tm, bw), rev),
            pl.BlockSpec((tm, d), rev),
            _full((B_HEADS, hd, 2 * hd)),
            _full((1, 2 * bw)),
            _full((SUBLANES, bw)),
            _full((1, bw)),
            _full((1, bw)),
            _full((1, d)),
        ],
        out_shape=[
            jax.ShapeDtypeStruct((t_rows, d), F32),
            jax.ShapeDtypeStruct((t_rows, 2 * bw), BF16),
            jax.ShapeDtypeStruct((t_rows, bw), BF16),
            jax.ShapeDtypeStruct((t_rows, d), BF16),
            jax.ShapeDtypeStruct((B_HEADS, hd, 2 * hd), F32),
            jax.ShapeDtypeStruct((1, 2 * bw), F32),
            jax.ShapeDtypeStruct((SUBLANES, bw), F32),
            jax.ShapeDtypeStruct((1, bw), F32),
            jax.ShapeDtypeStruct((1, bw), F32),
            jax.ShapeDtypeStruct((1, d), F32),
        ],
        scratch_shapes=[pltpu.VMEM((SUBLANES, bw), F32)] * 3 + [pltpu.VMEM((tm, bw), F32)] * 7,
        compiler_params=_cparams(("arbitrary",)),
    )(dout, x1, z, hseq, xb_tails, h_tails, nw, bin_w, cw, cb, gab, gabt, gb, lam, bout)


def _wgrad(a, b, m_blocks, n_blocks, hook=None):
    k, m = a.shape
    n = b.shape[1]
    bm, bn = m // m_blocks, n // n_blocks

    def body(a_ref, b_ref, o_ref):
        o_ref[...] = _dot_tn(a_ref[...], b_ref[...])

    out = _pcall(
        body,
        hook,
        name=f"wgrad_{m}x{n}",
        grid=(n_blocks, m_blocks),
        in_specs=[pl.BlockSpec((k, bm), lambda j, i: (0, i)), pl.BlockSpec((k, bn), lambda j, i: (0, j))],
        out_specs=[pl.BlockSpec((None, None, bm, bn), lambda j, i: (j, i, 0, 0))],
        out_shape=[jax.ShapeDtypeStruct((n_blocks, m_blocks, bm, bn), F32)],
        compiler_params=_cparams(("arbitrary", "arbitrary")),
    )(a, b)
    return out[0] if hook is None else (out[0][0], out[1])


def _adamw_math(w, g, m, v):
    m = ADAM_B1 * m + (1.0 - ADAM_B1) * g
    v = ADAM_B2 * v + (1.0 - ADAM_B2) * (g * g)
    m_hat = m / (1.0 - ADAM_B1**ADAM_STEP)
    v_hat = v / (1.0 - ADAM_B2**ADAM_STEP)
    delta = -ADAM_LR * (m_hat / (jnp.sqrt(v_hat) + ADAM_EPS) + ADAM_WD * w)
    return delta, m, v


def _adamw(w, g, m, v, hook=None):
    rows, cols = w.shape
    tr = _row_tile(rows, cols, 1024 * 1024)

    def body(w_ref, g_ref, m_ref, v_ref, d_ref, mo_ref, vo_ref):
        d_ref[...], mo_ref[...], vo_ref[...] = _adamw_math(w_ref[...], g_ref[...], m_ref[...], v_ref[...])

    spec = pl.BlockSpec((tr, cols), lambda i: (i, 0))
    return _pcall(
        body,
        hook,
        name=f"adamw_{rows}x{cols}",
        grid=(rows // tr,),
        in_specs=[spec] * 4,
        out_specs=[spec] * 3,
        out_shape=[jax.ShapeDtypeStruct((rows, cols), F32)] * 3,
        compiler_params=_cparams(("arbitrary",)),
    )(w, g, m, v)


def _sum_partials(parts):
    def body(p_ref, o_ref):
        total = p_ref[0, 0:1, :]
        for k in range(1, N_DEV):
            total = total + p_ref[k, 0:1, :]
        o_ref[...] = total

    vmem = pl.BlockSpec(memory_space=pltpu.VMEM)
    return _pcall(
        body,
        name="sum_partials",
        in_specs=[vmem],
        out_specs=vmem,
        out_shape=jax.ShapeDtypeStruct((1, parts.shape[2]), F32),
    )(parts)


def _adamw_many(ws, gs, ms, vs, name, hook=None):
    n = len(ws)

    def body(*refs):
        w_refs, g_refs, m_refs, v_refs = (refs[i * n : (i + 1) * n] for i in range(4))
        d_refs, mo_refs, vo_refs = (refs[(4 + i) * n : (5 + i) * n] for i in range(3))
        for i in range(n):
            d_refs[i][...], mo_refs[i][...], vo_refs[i][...] = _adamw_math(
                w_refs[i][...], g_refs[i][...], m_refs[i][...], v_refs[i][...]
            )

    vmem = pl.BlockSpec(memory_space=pltpu.VMEM)
    outs = _pcall(
        body,
        hook,
        name=name,
        in_specs=[vmem] * (4 * n),
        out_specs=[vmem] * (3 * n),
        out_shape=[jax.ShapeDtypeStruct(w.shape, F32) for w in ws] * 3,
        compiler_params=_cparams(),
    )(*ws, *gs, *ms, *vs)
    extra = None
    if hook is not None:
        outs, extra = outs
    return (outs[:n], outs[n : 2 * n], outs[2 * n :]), extra


def _pack_rows(parts, lanes=LANES):
    flat = jnp.concatenate([p.reshape(-1) for p in parts])
    per = N_DEV * SUBLANES * lanes
    total = -(-flat.shape[0] // per) * per
    flat = jnp.pad(flat, (0, total - flat.shape[0]))
    return flat.reshape(N_DEV, total // (N_DEV * lanes), lanes)


def _unpack(flat, shapes):
    out, at = [], 0
    for s in shapes:
        n = 1
        for dim in s:
            n *= dim
        out.append(flat[at : at + n].reshape(s))
        at += n
    return out


def kernel(x, norm_w, a_w_in, a_ln_w, a_ln_b, a_w_s, a_b_s, a_w_out, b_w_in, b_conv_w, b_conv_b, b_gate_a_w, b_gate_a_b, b_gate_x_w, b_gate_x_b, b_lambda, b_w_out, norm_f_w, loss_target, m_norm_w, m_a_w_in, m_a_ln_w, m_a_ln_b, m_a_w_s, m_a_b_s, m_a_w_out, m_b_w_in, m_b_conv_w, m_b_conv_b, m_b_gate_a_w, m_b_gate_a_b, m_b_gate_x_w, m_b_gate_x_b, m_b_lambda, m_b_w_out, m_norm_f_w, v_norm_w, v_a_w_in, v_a_ln_w, v_a_ln_b, v_a_w_s, v_a_b_s, v_a_w_out, v_b_w_in, v_b_conv_w, v_b_conv_b, v_b_gate_a_w, v_b_gate_a_b, v_b_gate_x_w, v_b_gate_x_b, v_b_lambda, v_b_w_out, v_norm_f_w):
    t_rows, d = x.shape[1], x.shape[2]
    aw = a_ln_w.shape[1]
    bw = b_gate_a_w.shape[1] * b_gate_a_w.shape[2]
    hd = bw // B_HEADS
    mine = 2 * lax.axis_index("x") + lax.axis_index("y")
    core = lax.axis_index("c")
    weights = dict(norm_w=norm_w, a_w_in=a_w_in, a_ln_w=a_ln_w, a_ln_b=a_ln_b, a_w_s=a_w_s, a_b_s=a_b_s, a_w_out=a_w_out, b_w_in=b_w_in, b_conv_w=b_conv_w, b_conv_b=b_conv_b, b_gate_a_w=b_gate_a_w, b_gate_a_b=b_gate_a_b, b_gate_x_w=b_gate_x_w, b_gate_x_b=b_gate_x_b, b_lambda=b_lambda, b_w_out=b_w_out, norm_f_w=norm_f_w)
    m_in = dict(norm_w=m_norm_w, a_w_in=m_a_w_in, a_ln_w=m_a_ln_w, a_ln_b=m_a_ln_b, a_w_s=m_a_w_s, a_b_s=m_a_b_s, a_w_out=m_a_w_out, b_w_in=m_b_w_in, b_conv_w=m_b_conv_w, b_conv_b=m_b_conv_b, b_gate_a_w=m_b_gate_a_w, b_gate_a_b=m_b_gate_a_b, b_gate_x_w=m_b_gate_x_w, b_gate_x_b=m_b_gate_x_b, b_lambda=m_b_lambda, b_w_out=m_b_w_out, norm_f_w=m_norm_f_w)
    v_in = dict(norm_w=v_norm_w, a_w_in=v_a_w_in, a_ln_w=v_a_ln_w, a_ln_b=v_a_ln_b, a_w_s=v_a_w_s, a_b_s=v_a_b_s, a_w_out=v_a_w_out, b_w_in=v_b_w_in, b_conv_w=v_b_conv_w, b_conv_b=v_b_conv_b, b_gate_a_w=v_b_gate_a_w, b_gate_a_b=v_b_gate_a_b, b_gate_x_w=v_b_gate_x_w, b_gate_x_b=v_b_gate_x_b, b_lambda=v_b_lambda, b_w_out=v_b_w_out, norm_f_w=v_norm_f_w)

    win_l = _cast_to_segments(a_w_in[0], mine, 256)
    wout_l = _cast_to_segments(a_w_out[0], mine, 256)
    small_l = jnp.concatenate([b_conv_w[0], b_conv_b, b_gate_a_b, b_gate_x_b, b_lambda], axis=0)
    bin_l, (win_g, wout_g, small_g) = _cast_to_segments(
        b_w_in[0], mine, 128, _gather_hook([win_l, wout_l], small_l)
    )
    bout_l = _cast_to_segments(b_w_out[0], mine, 192)
    win = win_g.reshape(N_CHIPS, d, -1)
    wout = wout_g.reshape(aw, d)

    tril = jnp.tril(jnp.ones((CHUNK, CHUNK), F32))
    wc = (a_w_s[0] * tril[None]).astype(BF16)
    wct = jnp.swapaxes(wc, 1, 2)
    bs_t = a_b_s[0].T
    gab = jnp.concatenate([b_gate_a_w[0], b_gate_x_w[0]], axis=2).astype(BF16)
    gabt = jnp.swapaxes(gab, 1, 2)
    nw0, nw1, nf = norm_w[0:1], norm_w[1:2], norm_f_w.reshape(1, d)

    x0 = x[0]
    (z_a, x1, h0), (bin_g, bout_g) = _layer_a_fwd(
        x0, nw0, win, a_ln_w, a_ln_b, wc, bs_t, wout, TM_FWD, _gather_hook([bin_l, bout_l])
    )
    bin_w = bin_g.reshape(N_CHIPS, d, -1)
    bout = bout_g.reshape(bw, d)
    small_f = jnp.transpose(small_g, (1, 0, 2)).reshape(SUBLANES, bw)
    cw, cb = small_f[0:CONV_WIDTH], small_f[CONV_WIDTH : CONV_WIDTH + 1]
    gb = jnp.concatenate([small_f[5:6], small_f[6:7]], axis=1)
    lam = small_f[7:8]
    z_b, hseq, h1, xb_tails, h_tails, dx2, loss_l, g_nf = _layer_b_fwd(
        x1, nw1, bin_w, cw, cb, gab, gb, lam, bout, nf, loss_target[0], TM_FWD
    )
    dx1, dz_b, y_b, dob_b, g_gab, g_gb, g_cw, g_cb, g_lam, g_nw1 = _layer_b_bwd(
        dx2, x1, z_b, hseq, xb_tails, h_tails, nw1, bin_w, cw, cb, gab, gabt, gb, lam, bout, TM_FWD
    )
    seg = lambda g: g.reshape(N_DEV, -1, g.shape[3])
    x_at, y_at = lax.axis_index("x"), lax.axis_index("y")
    first_no = 2 * (x_at ^ (1 - core)) + (y_at ^ core)
    second_no = 2 * (x_at ^ core) + (y_at ^ (1 - core))
    bf16s = lambda bufs: [BF16] * len(bufs)
    own_half = lambda bufs, got, wires: [
        _add_own_half(b, g, (first_no, N_CHIPS - 1 - mine), core, w) for b, g, w in zip(bufs, got, wires)
    ]
    for_neighbour = lambda bufs, got_a, got1, wires: [
        _add_for_neighbour(b, ga, g1, second_no, core, w) for b, ga, g1, w in zip(bufs, got_a, got1, wires)
    ]
    received = lambda bufs, got_a, got1, got2: [
        _add_received(b, ga, g1, g2, mine, core, core, 2) for b, ga, g1, g2 in zip(bufs, got_a, got1, got2)
    ]

    g_bout = [seg(_wgrad(y_b, dob_b, N_CHIPS, 1))]
    g_bin, swap_o = _wgrad(h1, dz_b, 2, N_CHIPS, _swap_hook(g_bout))
    g_bin = [seg(g_bin)]
    part_o = own_half(g_bout, swap_o, bf16s(g_bout))
    a_args = (z_a, a_ln_w, a_ln_b, wc, wct, bs_t, wout)
    half = t_rows // TM_A_BWD // 2
    first, rode = _layer_a_bwd(
        dx1, *a_args, (0, half), None, _join_hooks(_swap_hook(g_bin), _send_first_hook(part_o))
    )
    swap_i, got1_o = rode[:1], rode[1:]
    part_i = own_half(g_bin, swap_i, bf16s(g_bin))
    mid_o = for_neighbour(g_bout, swap_o, got1_o, bf16s(g_bout))
    second, rode = _layer_a_bwd(
        dx1, *a_args, (half, 2 * half), first[:3], _join_hooks(_send_first_hook(part_i), _send_second_hook(mid_o))
    )
    got1_i, got2_o = rode[:1], rode[1:]
    dz_a, y_a, dob_a = second[:3]
    g_ws, g_bst, g_lnw, g_lnb = (p + q for p, q in zip(first[3:], second[3:]))
    mid_i = for_neighbour(g_bin, swap_i, got1_i, bf16s(g_bin))
    red_o = received(g_bout, swap_o, got1_o, got2_o)
    g_win, rode = _wgrad(h0, dz_a, 2, N_CHIPS, _join_hooks(_send_second_hook(mid_i), _share_hook(red_o)))
    g_win = [seg(g_win)]
    got2_i, gr_bout = rode[:1], rode[1].reshape(b_w_out.shape[1:])
    red_i = received(g_bin, swap_i, got1_i, got2_i)

    small_shapes = [
        (1, d), (1, aw), (1, aw), (A_GROUPS, CHUNK, CHUNK), (A_GROUPS, CHUNK), (B_HEADS, hd, hd), (B_HEADS, hd, hd),
        (d,), (CONV_WIDTH, bw), (1, bw), (1, bw), (1, bw), (1, bw), (1, 1),
    ]
    small = _pack_rows(
        [
            g_nw1, g_lnw, g_lnb, g_ws, g_bst.T, g_gab[:, :, :hd], g_gab[:, :, hd:],
            g_nf, g_cw[:CONV_WIDTH], g_cb, g_gb[:, :bw], g_gb[:, bw:], g_lam, loss_l[:, :1],
        ]
    )
    g_w, wire_w = g_win + [small], [BF16, F32]
    g_wout, rode = _wgrad(y_a, dob_a, N_DEV, 1, _join_hooks(_swap_hook(g_w), _share_hook(red_i)))
    g_wout = seg(g_wout)
    swap_w, gr_bin = rode[:2], rode[2].reshape(b_w_in.shape[1:])

    g_u, wire_u = [g_wout], [BF16]
    part_w = own_half(g_w, swap_w, wire_w)
    (grad_x, g_nw0_mine), rode = _layer_a_bwd_dx(
        dx1, x0, dz_a, nw0, win, TM_A_DX, (0, t_rows // TM_A_DX), None,
        _join_hooks(_send_first_hook(part_w), _swap_hook(g_u)),
    )
    got1_w, swap_u = rode[:2], rode[2:]
    part_u = own_half(g_u, swap_u, wire_u)
    mid_w = for_neighbour(g_w, swap_w, got1_w, wire_w)
    b_names = ("b_w_in", "b_w_out")
    two_d = lambda a: a.reshape(a.shape[-2:])
    bin_out, rode = _adamw(
        two_d(b_w_in), gr_bin, two_d(m_b_w_in), two_d(v_b_w_in),
        _join_hooks(_send_second_hook(mid_w), _send_first_hook(part_u)),
    )
    got2_w, got1_u = rode[:2], rode[2:]
    mid_u = for_neighbour(g_u, swap_u, got1_u, wire_u)
    red_w = received(g_win, swap_w[:1], got1_w[:1], got2_w[:1])
    red_small = _add_received(small, swap_w[1], got1_w[1], got2_w[1], mine, core, 2 * mine + core, N_DEV)
    bout_out, rode = _adamw(
        two_d(b_w_out), gr_bout, two_d(m_b_w_out), two_d(v_b_w_out),
        _join_hooks(_send_second_hook(mid_u), _share_hook(red_w, red_small)),
    )
    b_out = list(zip(bin_out, bout_out))
    got2_u, gr_win, small_r = rode[:1], rode[1].reshape(a_w_in.shape[1:]), rode[2]
    red_wout = received(g_u, swap_u, got1_u, got2_u)
    gr_wout, g_nw0_all = _run_hook(_share_hook(red_wout, None, g_nw0_mine), "share_reduced")
    win_out = _adamw(two_d(a_w_in), gr_win, two_d(m_a_w_in), two_d(v_a_w_in))
    g_nw0 = _sum_partials(g_nw0_all)
    gr_wout = gr_wout.reshape(a_w_out.shape[1:])
    (g_nw1_r, g_a_ln_w, g_a_ln_b, g_a_w_s, g_a_b_s, g_gate_a_w, g_gate_x_w, g_norm_f, gf_cw, gf_cb, gf_gab, gf_gxb,
     gf_lam, loss) = _unpack(small_r.reshape(-1), small_shapes)
    g_norm_w = jnp.concatenate([g_nw0, g_nw1_r], axis=0)
    shard = lambda g: lax.dynamic_slice_in_dim(g, mine * (bw // N_CHIPS), bw // N_CHIPS, axis=1)

    grads = {
        "norm_w": g_norm_w, "a_w_in": gr_win[None], "a_ln_w": g_a_ln_w, "a_ln_b": g_a_ln_b, "a_w_s": g_a_w_s[None],
        "a_b_s": g_a_b_s[None], "a_w_out": gr_wout[None], "b_w_in": gr_bin[None], "b_conv_w": shard(gf_cw)[None],
        "b_conv_b": shard(gf_cb), "b_gate_a_w": g_gate_a_w[None], "b_gate_a_b": shard(gf_gab),
        "b_gate_x_w": g_gate_x_w[None], "b_gate_x_b": shard(gf_gxb), "b_lambda": shard(gf_lam),
        "b_w_out": gr_bout[None], "norm_f_w": g_norm_f,
    }
    names = list(weights)
    delta, new_m, new_v = {}, {}, {}
    delta["a_w_in"], new_m["a_w_in"], new_v["a_w_in"] = win_out
    delta["a_w_out"], new_m["a_w_out"], new_v["a_w_out"] = _adamw(
        two_d(a_w_out), gr_wout, two_d(m_a_w_out), two_d(v_a_w_out)
    )
    small_names = [n for n in names if n not in ("a_w_in", "a_w_out") + b_names]
    at_least_2d = lambda a: a.reshape(1, -1) if a.ndim == 1 else a
    small_out, _ = _adamw_many(
        *[[at_least_2d(src[n]) for n in small_names] for src in (weights, grads, m_in, v_in)], "adamw_small"
    )
    for dst, vals, b_vals in zip((delta, new_m, new_v), small_out, b_out):
        dst.update(zip(small_names, vals))
        dst.update(zip(b_names, b_vals))
    for dst in (delta, new_m, new_v):
        for n in names:
            dst[n] = dst[n].reshape(weights[n].shape)

    return (
        loss.reshape(()),
        grad_x[None],
        *[grads[n] for n in names],
        *[delta[n] for n in names],
        *[new_m[n] for n in names],
        *[new_v[n] for n in names],
    )
```

```python
import jax
import jax.numpy as jnp
from jax import lax
from jax.experimental import pallas as pl
from jax.experimental.pallas import tpu as pltpu

F32 = jnp.float32
BF16 = jnp.bfloat16

RMS_EPS = 1e-6
LN_EPS = 1e-5
RG_C = 8.0
CHUNK = 128
A_GROUPS = 8
B_HEADS = 12
CONV_WIDTH = 4

ADAM_LR = 0.001
ADAM_B1 = 0.9
ADAM_B2 = 0.999
ADAM_EPS = 1e-08
ADAM_WD = 0.01
ADAM_STEP = 10

N_CHIPS = 4
N_DEV = 8
SUBLANES = 8
LANES = 128
V7X_VMEM_BYTES = 64 * 1024 * 1024
VMEM_LIMIT = V7X_VMEM_BYTES * 7 // 8
MESH = pl.DeviceIdType.MESH
ANY = pl.BlockSpec(memory_space=pl.ANY)

TM_FWD = 256
TM_A_BWD = 256
TM_A_DX = 512
SCAN_UNROLL = 4

GELU_C0 = 0.7978845608028654
GELU_C1 = 0.044715


class _Hook:
    def __init__(self, operands, out_shapes, aliases, n_sems, start, finish, middle=None, late=None):
        self.operands, self.out_shapes, self.aliases, self.n_sems = operands, out_shapes, aliases, n_sems
        self.start, self.finish, self.middle, self.late = start, finish, middle, late


class _SemView:
    def __init__(self, base, off):
        self.base, self.off = base, off

    @property
    def at(self):
        return self

    def __getitem__(self, k):
        return self.base.at[self.off + k]


def _join_hooks(*hooks):
    if len(hooks) == 1:
        return hooks[0]
    operands, out_shapes, aliases, spans = [], [], {}, []
    n_sems = 0
    for h in hooks:
        aliases.update({len(operands) + i: len(out_shapes) + o for i, o in h.aliases.items()})
        spans.append((len(operands), len(h.operands), len(out_shapes), len(h.out_shapes), n_sems))
        operands += list(h.operands)
        out_shapes += list(h.out_shapes)
        n_sems += h.n_sems

    def each(which):
        def run(ins, outs, send, recv):
            for h, (i0, ni, o0, no, s0) in zip(hooks, spans):
                step = getattr(h, which)
                if step is not None:
                    step(ins[i0 : i0 + ni], outs[o0 : o0 + no], _SemView(send, s0), _SemView(recv, s0))

        return run

    middle = each("middle") if any(h.middle is not None for h in hooks) else None
    late = each("late") if any(h.late is not None for h in hooks) else None
    return _Hook(operands, out_shapes, aliases, n_sems, each("start"), each("finish"), middle, late)


def _pcall(body, hook=None, **kw):
    if hook is None:
        return pl.pallas_call(body, **kw)
    n_pre = 0
    if "grid_spec" in kw:
        spec = kw.pop("grid_spec")
        n_pre = spec.num_scalar_prefetch
        kw.update(
            grid=tuple(spec.grid), in_specs=list(spec.in_specs), out_specs=list(spec.out_specs),
            scratch_shapes=list(spec.scratch_shapes),
        )
    n_in, n_out = len(kw["in_specs"]), len(kw["out_shape"])
    hi, ho = len(hook.operands), len(hook.out_shapes)
    grid = kw.get("grid", ())

    def wrapped(*refs):
        pre, refs = refs[:n_pre], refs[n_pre:]
        ins, h_in = refs[:n_in], refs[n_in : n_in + hi]
        outs = refs[n_in + hi : n_in + hi + n_out]
        h_out = refs[n_in + hi + n_out : n_in + hi + n_out + ho]
        scratch = refs[n_in + hi + n_out + ho : -2]
        send_sems, recv_sems = refs[-2:]
        if not grid:
            hook.start(h_in, h_out, send_sems, recv_sems)
            if hook.middle is not None:
                hook.middle(h_in, h_out, send_sems, recv_sems)
            body(*pre, *ins, *outs, *scratch)
            if hook.late is not None:
                hook.late(h_in, h_out, send_sems, recv_sems)
            hook.finish(h_in, h_out, send_sems, recv_sems)
            return
        first = pl.program_id(0) == 0
        last = pl.program_id(0) == grid[0] - 1
        for axis in range(1, len(grid)):
            first = jnp.logical_and(first, pl.program_id(axis) == 0)
            last = jnp.logical_and(last, pl.program_id(axis) == grid[axis] - 1)

        @pl.when(first)
        def _():
            hook.start(h_in, h_out, send_sems, recv_sems)

        for when, step in ((hook.middle, grid[0] // 4), (hook.late, grid[0] - 1)):
            if when is not None:
                assert len(grid) == 1 and grid[0] >= 4

                @pl.when(pl.program_id(0) == step)
                def _(when=when):
                    when(h_in, h_out, send_sems, recv_sems)

        body(*pre, *ins, *outs, *scratch)

        @pl.when(last)
        def _():
            hook.finish(h_in, h_out, send_sems, recv_sems)

    aliases = dict(kw.pop("input_output_aliases", {}))
    aliases.update({n_pre + n_in + i: n_out + o for i, o in hook.aliases.items()})
    kw.update(
        in_specs=list(kw["in_specs"]) + [ANY] * hi,
        out_specs=list(kw["out_specs"]) + [ANY] * ho,
        out_shape=list(kw["out_shape"]) + list(hook.out_shapes),
        scratch_shapes=list(kw.get("scratch_shapes", ()))
        + [pltpu.SemaphoreType.DMA((hook.n_sems,)), pltpu.SemaphoreType.DMA((hook.n_sems,))],
        input_output_aliases=aliases,
    )
    if n_pre:
        kw["grid_spec"] = pltpu.PrefetchScalarGridSpec(
            num_scalar_prefetch=n_pre, grid=kw.pop("grid"), in_specs=kw.pop("in_specs"),
            out_specs=kw.pop("out_specs"), scratch_shapes=kw.pop("scratch_shapes"),
        )
    call = pl.pallas_call(wrapped, **kw)

    def run(*operands):
        outs = call(*operands, *hook.operands)
        return outs[:n_out], outs[n_out:]

    return run


def _run_hook(hook, name):
    def body():
        pass

    return _pcall(body, hook, name=name, in_specs=[], out_specs=[], out_shape=[])()[1]


def _cparams(sem=None):
    return pltpu.CompilerParams(dimension_semantics=sem, vmem_limit_bytes=VMEM_LIMIT)


def _full(shape):
    zeros = (0,) * len(shape)
    return pl.BlockSpec(shape, lambda *_: zeros)


def _scalars(*vals):
    return jnp.stack([jnp.asarray(v, jnp.int32) for v in vals])


def _sigmoid(x):
    return 1.0 / (1.0 + jnp.exp(-x))


def _gelu(x):
    t = jnp.tanh(GELU_C0 * (x + GELU_C1 * (x * x * x)))
    return x * (0.5 * (1.0 + t))


def _gelu_and_grad(x):
    x2 = x * x
    t = jnp.tanh(GELU_C0 * (x + GELU_C1 * (x2 * x)))
    cdf = 0.5 * (1.0 + t)
    return x * cdf, cdf + 0.5 * x * (1.0 - t * t) * (GELU_C0 * (1.0 + 3.0 * GELU_C1 * x2))


def _silu_and_grad(x):
    s = _sigmoid(x)
    return x * s, s * (1.0 + x * (1.0 - s))


def _softplus_neg(lam):
    u = jnp.exp(-jnp.abs(lam))
    w = 1.0 + u
    log1p = jnp.where(w == 1.0, u, jnp.log(w) * (u / jnp.where(w == 1.0, 1.0, w - 1.0)))
    return jnp.maximum(-lam, 0.0) + log1p


def _dot(a, b):
    return jnp.dot(a, b, preferred_element_type=F32)


def _dot_nt(a, b):
    return lax.dot_general(a, b, (((1,), (1,)), ((), ())), preferred_element_type=F32)


def _dot_tn(a, b):
    return lax.dot_general(a, b, (((0,), (0,)), ((), ())), preferred_element_type=F32)


def _shift_down(v, halo, k):
    if k == 0:
        return v
    rolled = pltpu.roll(v, k, 0)
    row = lax.broadcasted_iota(jnp.int32, (SUBLANES, v.shape[1]), 0)
    top = jnp.where(row < k, pltpu.roll(halo, k, 0), rolled[:SUBLANES])
    return jnp.concatenate([top, rolled[SUBLANES:]], axis=0)


def _shift_up(v, head, k):
    if k == 0:
        return v
    n = v.shape[0]
    rolled = pltpu.roll(v, n - k, 0)
    row = lax.broadcasted_iota(jnp.int32, (SUBLANES, v.shape[1]), 0)
    bot = jnp.where(row >= SUBLANES - k, pltpu.roll(head, SUBLANES - k, 0), rolled[n - SUBLANES :])
    return jnp.concatenate([rolled[: n - SUBLANES], bot], axis=0)


def _scan_blocks(a_ref, b_ref, out_ref, carry, n_rows, reverse):
    width = a_ref.shape[1]
    row = lax.broadcasted_iota(jnp.int32, (SUBLANES, width), 0)
    n_blocks = n_rows // SUBLANES

    def block(j, carry):
        i = (n_blocks - 1 - j) if reverse else j
        r0 = pl.multiple_of(i * SUBLANES, SUBLANES)
        a = a_ref[pl.ds(r0, SUBLANES), :]
        b = b_ref[pl.ds(r0, SUBLANES), :]
        for d in (1, 2, 4):
            shift = (SUBLANES - d) if reverse else d
            keep = (row < SUBLANES - d) if reverse else (row >= d)
            a_s = pltpu.roll(a, shift, 0)
            b_s = pltpu.roll(b, shift, 0)
            b = jnp.where(keep, a * b_s + b, b)
            a = jnp.where(keep, a * a_s, a)
        h = a * carry + b
        out_ref[pl.ds(r0, SUBLANES), :] = h
        edge = h[0:1, :] if reverse else h[SUBLANES - 1 : SUBLANES, :]
        return jnp.broadcast_to(edge, (SUBLANES, width))

    return lax.fori_loop(0, n_blocks, block, carry, unroll=SCAN_UNROLL)


def _rms_fwd(x, w):
    r = lax.rsqrt(jnp.mean(x * x, axis=-1, keepdims=True) + RMS_EPS)
    xh = x * r
    return xh * w, xh, r


def _rms_bwd(dh, xh, r, w):
    dxh = dh * w
    dx = r * (dxh - xh * jnp.mean(dxh * xh, axis=-1, keepdims=True))
    return dx, jnp.sum(dh * xh, axis=0, keepdims=True)


def _cast_to_segments(w, mine, rows, hook=None):
    n, c = w.shape
    per = n // 2 // rows

    def body(k_ref, w_ref, o_ref):
        o_ref[...] = w_ref[...].astype(BF16)

    out = _pcall(
        body,
        hook,
        name=f"cast_{n}x{c}",
        grid_spec=pltpu.PrefetchScalarGridSpec(
            num_scalar_prefetch=1,
            grid=(n // rows,),
            in_specs=[pl.BlockSpec((rows, c), lambda i, k_ref: (i, 0))],
            out_specs=[pl.BlockSpec((None, rows, c), lambda i, k_ref: (2 * k_ref[0] + i // per, i % per, 0))],
        ),
        out_shape=[jax.ShapeDtypeStruct((N_DEV, n // 2, c), BF16)],
        compiler_params=_cparams(("arbitrary",)),
    )(_scalars(mine), w)
    return out[0] if hook is None else (out[0][0], out[1])


def _place():
    x, y, c = lax.axis_index("x"), lax.axis_index("y"), lax.axis_index("c")
    chips = [(1 - x, y), (x, 1 - y), (1 - x, 1 - y)]
    return x, y, c, chips


def _chip_no(chip):
    return 2 * chip[0] + chip[1]


def _rcopy(src, dst, send_sem, recv_sem, to):
    return pltpu.make_async_remote_copy(
        src_ref=src, dst_ref=dst, send_sem=send_sem, recv_sem=recv_sem, device_id=to, device_id_type=MESH
    )


def _gather_hook(big, small=None):
    nb = len(big)
    n_sems = 6 * nb + 4

    def places():
        x, y, c, chips = _place()
        first = (x ^ (1 - c), y ^ c)
        second = (x ^ c, y ^ (1 - c))
        return x, y, c, chips, first, second, (1 - x, 1 - y)

    def seg(outs, b, chip, half):
        return outs[b].at[2 * _chip_no(chip) + half]

    def step1(outs, send, recv):
        x, y, c, _, first, _, _ = places()
        return [
            _rcopy(seg(outs, b, (x, y), c), seg(outs, b, (x, y), c), send.at[6 * b], recv.at[6 * b], (*first, c))
            for b in range(nb)
        ]

    def step2(outs, send, recv):
        x, y, c, _, first, second, _ = places()
        copies = []
        for b in range(nb):
            for k, chip in ((1, (x, y)), (2, first)):
                src = seg(outs, b, chip, c)
                copies.append(_rcopy(src, src, send.at[6 * b + k], recv.at[6 * b + k], (*second, c)))
        return copies

    def hand_over(outs, send, recv, k, chip):
        x, y, c, *_ = places()
        return [
            _rcopy(seg(outs, b, chip, c), seg(outs, b, chip, c), send.at[6 * b + k], recv.at[6 * b + k], (x, y, 1 - c))
            for b in range(nb)
        ]

    def wait_landed(outs, send, recv, k, chip, half):
        x, y, c, *_ = places()
        for b in range(nb):
            got = seg(outs, b, chip, half)
            _rcopy(got, got, send.at[6 * b + k], recv.at[6 * b + k], (x, y, c)).wait_recv()

    def small_copies(ins, outs, send, recv):
        x, y, c, chips, *_ = places()
        there = outs[nb].at[_chip_no((x, y))]
        return [
            _rcopy(ins[nb], there, send.at[6 * nb + j], recv.at[6 * nb + j], (*chip, c)) for j, chip in enumerate(chips)
        ]

    def local_copy(ins, outs, send):
        x, y, _, _ = _place()
        return pltpu.make_async_copy(ins[nb], outs[nb].at[_chip_no((x, y))], send.at[6 * nb + 3])

    def start(ins, outs, send, recv):
        for cp in step1(outs, send, recv):
            cp.start()
        if small is not None:
            for cp in small_copies(ins, outs, send, recv):
                cp.start()
            local_copy(ins, outs, send).start()

    def middle(ins, outs, send, recv):
        *_, first, _, _ = places()
        wait_landed(outs, send, recv, 0, first, places()[2])
        for cp in step2(outs, send, recv) + hand_over(outs, send, recv, 3, first):
            cp.start()

    def late(ins, outs, send, recv):
        x, y, c, chips, first, second, diagonal = places()
        for k, chip in ((1, second), (2, diagonal)):
            wait_landed(outs, send, recv, k, chip, c)
            for cp in hand_over(outs, send, recv, 3 + k, chip):
                cp.start()

    def finish(ins, outs, send, recv):
        x, y, c, chips, first, second, diagonal = places()
        wait_landed(outs, send, recv, 3, second, 1 - c)
        wait_landed(outs, send, recv, 4, first, 1 - c)
        wait_landed(outs, send, recv, 5, diagonal, 1 - c)
        sent = step1(outs, send, recv) + step2(outs, send, recv)
        for k, chip in ((3, first), (4, second), (5, diagonal)):
            sent += hand_over(outs, send, recv, k, chip)
        for cp in sent:
            cp.wait_send()
        if small is not None:
            for j, chip in enumerate(chips):
                got = outs[nb].at[_chip_no(chip)]
                _rcopy(got, got, send.at[6 * nb + j], recv.at[6 * nb + j], (x, y, c)).wait_recv()
            for cp in small_copies(ins, outs, send, recv):
                cp.wait_send()
            local_copy(ins, outs, send).wait()

    operands = list(big) + ([small] if small is not None else [])
    out_shapes = [jax.ShapeDtypeStruct(b.shape, b.dtype) for b in big]
    if small is not None:
        out_shapes.append(jax.ShapeDtypeStruct((N_CHIPS, *small.shape), small.dtype))
    return _Hook(operands, out_shapes, {b: b for b in range(nb)}, n_sems, start, finish, middle, late)


def _both_ways_hook(operands, out_shapes, copies_of, n_sems):
    def start(ins, outs, send, recv):
        for cp in copies_of(ins, outs, send, recv):
            cp.start()

    def finish(ins, outs, send, recv):
        for cp in copies_of(ins, outs, send, recv):
            cp.wait()

    return _Hook(operands, out_shapes, {}, n_sems, start, finish)


def _swap_hook(bufs):
    def copies_of(ins, outs, send, recv):
        x, y, c, _ = _place()
        copies = []
        for b in range(len(bufs)):
            for j in range(N_CHIPS):
                k = b * N_CHIPS + j
                copies.append(_rcopy(ins[b].at[2 * j + 1 - c], outs[b].at[j], send.at[k], recv.at[k], (x, y, 1 - c)))
        return copies

    out_shapes = [jax.ShapeDtypeStruct((N_CHIPS, *b.shape[1:]), b.dtype) for b in bufs]
    return _both_ways_hook(list(bufs), out_shapes, copies_of, len(bufs) * N_CHIPS)


def _axis_order():
    x, y, c, _ = _place()
    return (x, y), c, (x ^ (1 - c), y ^ c), (x ^ c, y ^ (1 - c)), (1 - x, 1 - y)


def _send_first_hook(parts):
    def copies_of(ins, outs, send, recv):
        _, c, first, _, _ = _axis_order()
        copies = []
        for b in range(len(parts)):
            for k in range(2):
                sem = 2 * b + k
                copies.append(_rcopy(ins[b].at[k], outs[b].at[k], send.at[sem], recv.at[sem], (*first, c)))
        return copies

    out_shapes = [jax.ShapeDtypeStruct((2, *p.shape[1:]), p.dtype) for p in parts]
    return _both_ways_hook(list(parts), out_shapes, copies_of, len(parts) * 2)


def _send_second_hook(mids):
    def copies_of(ins, outs, send, recv):
        _, c, _, second, _ = _axis_order()
        return [_rcopy(ins[b], outs[b], send.at[b], recv.at[b], (*second, c)) for b in range(len(mids))]

    out_shapes = [jax.ShapeDtypeStruct(m.shape, m.dtype) for m in mids]
    return _both_ways_hook(list(mids), out_shapes, copies_of, len(mids))


def _share_hook(big, small=None, tiny=None):
    nb = len(big)
    n_sems = nb + 7 + N_DEV
    t0 = nb + 7

    def tiny_copies(ins, outs, send, recv):
        x, y, c, _ = _place()
        there = outs[-1].at[2 * _chip_no((x, y)) + c]
        copies = []
        for r in range(1, N_DEV):
            to = (x ^ (r >> 2 & 1), y ^ (r >> 1 & 1), c ^ (r & 1))
            copies.append(_rcopy(ins[-1], there, send.at[t0 + r], recv.at[t0 + r], to))
        return copies

    def tiny_local(ins, outs, send):
        x, y, c, _ = _place()
        return pltpu.make_async_copy(ins[-1], outs[-1].at[2 * _chip_no((x, y)) + c], send.at[t0])

    def first_copies(outs, send, recv):
        x, y, c, chips = _place()
        sibling = (x, y, 1 - c)
        copies = [_rcopy(outs[b].at[c], outs[b].at[c], send.at[b], recv.at[b], sibling) for b in range(nb)]
        if small is not None:
            own = outs[nb].at[2 * _chip_no((x, y)) + c]
            copies.append(_rcopy(own, own, send.at[nb], recv.at[nb], sibling))
            for j, chip in enumerate(chips):
                copies.append(_rcopy(own, own, send.at[nb + 1 + j], recv.at[nb + 1 + j], (*chip, c)))
        return copies

    def start(ins, outs, send, recv):
        for cp in first_copies(outs, send, recv):
            cp.start()
        if tiny is not None:
            for cp in tiny_copies(ins, outs, send, recv):
                cp.start()
            tiny_local(ins, outs, send).start()

    def finish(ins, outs, send, recv):
        x, y, c, chips = _place()
        me, sibling = (x, y, c), (x, y, 1 - c)
        if tiny is not None:
            for cp in tiny_copies(ins, outs, send, recv):
                cp.wait()
            tiny_local(ins, outs, send).wait()
        passed = []
        if small is not None:
            for j, chip in enumerate(chips):
                got = outs[nb].at[2 * _chip_no(chip) + c]
                _rcopy(got, got, send.at[nb + 1 + j], recv.at[nb + 1 + j], me).wait_recv()
                fwd = _rcopy(got, got, send.at[nb + 4 + j], recv.at[nb + 4 + j], sibling)
                fwd.start()
                passed.append(fwd)
        for b in range(nb):
            got = outs[b].at[1 - c]
            _rcopy(got, got, send.at[b], recv.at[b], me).wait_recv()
        if small is not None:
            got = outs[nb].at[2 * _chip_no((x, y)) + 1 - c]
            _rcopy(got, got, send.at[nb], recv.at[nb], me).wait_recv()
            for j, chip in enumerate(chips):
                got = outs[nb].at[2 * _chip_no(chip) + 1 - c]
                _rcopy(got, got, send.at[nb + 4 + j], recv.at[nb + 4 + j], me).wait_recv()
        for cp in first_copies(outs, send, recv) + passed:
            cp.wait_send()

    operands = list(big) + ([small] if small is not None else [])
    out_shapes = [jax.ShapeDtypeStruct(a.shape, a.dtype) for a in operands]
    aliases = {i: i for i in range(len(operands))}
    if tiny is not None:
        operands.append(tiny)
        out_shapes.append(jax.ShapeDtypeStruct((N_DEV, *tiny.shape), tiny.dtype))
    return _Hook(operands, out_shapes, aliases, n_sems, start, finish)


def _row_tile(rows, cols, target_bytes=2 * 1024 * 1024):
    best = SUBLANES
    for t in range(SUBLANES, rows + 1, SUBLANES):
        if rows % t == 0 and t * cols * 4 <= target_bytes:
            best = t
    return best


def _add_own_half(buf, got, owners, c, wire):
    _, rows, cols = buf.shape
    tr = _row_tile(rows, cols)

    def body(s_ref, a_ref, b_ref, o_ref):
        o_ref[...] = (a_ref[...] + b_ref[...]).astype(wire)

    return _pcall(
        body,
        name=f"add_own_half_{rows}x{cols}",
        grid_spec=pltpu.PrefetchScalarGridSpec(
            num_scalar_prefetch=1,
            grid=(2, rows // tr),
            in_specs=[
                pl.BlockSpec((None, None, tr, cols), lambda j, r, s_ref: (s_ref[j], s_ref[2], r, 0)),
                pl.BlockSpec((None, tr, cols), lambda j, r, s_ref: (s_ref[j], r, 0)),
            ],
            out_specs=pl.BlockSpec((None, tr, cols), lambda j, r, s_ref: (j, r, 0)),
        ),
        out_shape=jax.ShapeDtypeStruct((2, rows, cols), wire),
        compiler_params=_cparams(("arbitrary", "arbitrary")),
    )(_scalars(owners[0], owners[1], c), buf.reshape(N_CHIPS, 2, rows, cols), got)


def _add_for_neighbour(buf, got_a, got1, second, c, wire):
    _, rows, cols = buf.shape
    tr = _row_tile(rows, cols)

    def body(s_ref, x_ref, a_ref, g_ref, o_ref):
        o_ref[...] = ((x_ref[...] + a_ref[...]) + g_ref[...].astype(F32)).astype(wire)

    return _pcall(
        body,
        name=f"add_for_neighbour_{rows}x{cols}",
        grid_spec=pltpu.PrefetchScalarGridSpec(
            num_scalar_prefetch=1,
            grid=(rows // tr,),
            in_specs=[
                pl.BlockSpec((None, None, tr, cols), lambda r, s_ref: (s_ref[0], s_ref[1], r, 0)),
                pl.BlockSpec((None, tr, cols), lambda r, s_ref: (s_ref[0], r, 0)),
                pl.BlockSpec((None, tr, cols), lambda r, s_ref: (1, r, 0)),
            ],
            out_specs=pl.BlockSpec((tr, cols), lambda r, s_ref: (r, 0)),
        ),
        out_shape=jax.ShapeDtypeStruct((rows, cols), wire),
        compiler_params=_cparams(("arbitrary",)),
    )(_scalars(second, c), buf.reshape(N_CHIPS, 2, rows, cols), got_a, got1)


def _add_received(buf, got_a, got1, got2, mine, c, slot, n_slots):
    _, rows, cols = buf.shape
    tr = _row_tile(rows, cols)

    def body(s_ref, x_ref, a_ref, g1_ref, g2_ref, o_ref):
        own = x_ref[...] + a_ref[...]
        o_ref[...] = (own + g1_ref[...].astype(F32)) + g2_ref[...].astype(F32)

    return _pcall(
        body,
        name=f"add_received_{rows}x{cols}",
        grid_spec=pltpu.PrefetchScalarGridSpec(
            num_scalar_prefetch=1,
            grid=(rows // tr,),
            in_specs=[
                pl.BlockSpec((None, None, tr, cols), lambda r, s_ref: (s_ref[0], s_ref[1], r, 0)),
                pl.BlockSpec((None, tr, cols), lambda r, s_ref: (s_ref[0], r, 0)),
                pl.BlockSpec((None, tr, cols), lambda r, s_ref: (0, r, 0)),
                pl.BlockSpec((tr, cols), lambda r, s_ref: (r, 0)),
            ],
            out_specs=pl.BlockSpec((None, tr, cols), lambda r, s_ref: (s_ref[2], r, 0)),
        ),
        out_shape=jax.ShapeDtypeStruct((n_slots, rows, cols), F32),
        compiler_params=_cparams(("arbitrary",)),
    )(_scalars(mine, c, slot), buf.reshape(N_CHIPS, 2, rows, cols), got_a, got1, got2)


def _layer_a_fwd(x, nw, win, ln_w, ln_b, wc, bs_t, wout, tm, hook):
    t_rows, d = x.shape
    n_sh, _, s_cols = win.shape
    aw = wout.shape[0]
    gd = aw // A_GROUPS
    tn = 512
    assert s_cols % tn == 0 and aw % tn == 0 and tm % CHUNK == 0

    def body(x_ref, nw_ref, win_ref, lnw_ref, lnb_ref, wc_ref, bst_ref, wout_ref, z_ref, x1_ref, h_ref, u_s, v_s, y_s):
        x = x_ref[...]
        h, _, _ = _rms_fwd(x, nw_ref[...])
        h = h.astype(BF16)
        h_ref[...] = h
        for j in range(3 * aw // tn):
            k, off = divmod(j * tn, s_cols)
            cols = slice((j * tn) % aw, (j * tn) % aw + tn)
            zj = _dot(h, win_ref[k, :, off : off + tn])
            z_ref[:, j * tn : (j + 1) * tn] = zj
            if j * tn < aw:
                u_s[:, cols] = _gelu(zj)
            elif j * tn < 2 * aw:
                v_s[:, cols] = _gelu(zj)
            else:
                u_s[:, cols] = u_s[:, cols] * (zj * _sigmoid(zj))
        v = v_s[...]
        mu = jnp.mean(v, axis=-1, keepdims=True)
        vc = v - mu
        rstd = lax.rsqrt(jnp.mean(vc * vc, axis=-1, keepdims=True) + LN_EPS)
        v_s[...] = (vc * rstd) * lnw_ref[...] + lnb_ref[...]
        for ck in range(tm // CHUNK):
            rows = slice(ck * CHUNK, (ck + 1) * CHUNK)
            for g in range(A_GROUPS):
                cols = slice(g * gd, (g + 1) * gd)
                s = _dot(wc_ref[g], v_s[rows, cols].astype(BF16)) + bst_ref[:, g : g + 1]
                y_s[rows, cols] = (u_s[rows, cols] * s).astype(BF16)
        x1_ref[...] = x + _dot(y_s[...], wout_ref[...])

    row = lambda i: (i, 0)
    return _pcall(
        body,
        hook,
        name="layer_a_fwd",
        grid=(t_rows // tm,),
        in_specs=[
            pl.BlockSpec((tm, d), row),
            _full(nw.shape),
            _full(win.shape),
            _full(ln_w.shape),
            _full(ln_b.shape),
            _full(wc.shape),
            _full(bs_t.shape),
            _full(wout.shape),
        ],
        out_specs=[pl.BlockSpec((tm, 3 * aw), row), pl.BlockSpec((tm, d), row), pl.BlockSpec((tm, d), row)],
        out_shape=[
            jax.ShapeDtypeStruct((t_rows, 3 * aw), F32),
            jax.ShapeDtypeStruct((t_rows, d), F32),
            jax.ShapeDtypeStruct((t_rows, d), BF16),
        ],
        scratch_shapes=[pltpu.VMEM((tm, aw), F32), pltpu.VMEM((tm, aw), F32), pltpu.VMEM((tm, aw), BF16)],
        compiler_params=_cparams(("arbitrary",)),
    )(x, nw, win, ln_w, ln_b, wc, bs_t, wout)


def _layer_a_bwd(dout, z, ln_w, ln_b, wc, wct, bs_t, wout, tiles, earlier, hook):
    t_rows, d = dout.shape
    aw = wout.shape[0]
    gd = aw // A_GROUPS
    tm = TM_A_BWD
    lo, hi = tiles
    n_earlier = 0 if earlier is None else len(earlier)

    def body(dout_ref, z_ref, lnw_ref, lnb_ref, wc_ref, wct_ref, bst_ref, wout_ref, *rest):
        dz_ref, y_ref, dob_ref, gws_ref, gbs_ref, glnw_ref, glnb_ref, u_s, vh_s, ds_s, dvn_s = rest[n_earlier:]

        @pl.when(pl.program_id(0) == 0)
        def _():
            gws_ref[...] = jnp.zeros_like(gws_ref)
            gbs_ref[...] = jnp.zeros_like(gbs_ref)
            glnw_ref[...] = jnp.zeros_like(glnw_ref)
            glnb_ref[...] = jnp.zeros_like(glnb_ref)

        dob = dout_ref[...].astype(BF16)
        dob_ref[...] = dob
        dy = _dot_nt(dob, wout_ref[...])

        zv = z_ref[:, aw : 2 * aw]
        vg, dvg_dz = _gelu_and_grad(zv)
        mu = jnp.mean(vg, axis=-1, keepdims=True)
        vc = vg - mu
        rstd = lax.rsqrt(jnp.mean(vc * vc, axis=-1, keepdims=True) + LN_EPS)
        vh = vc * rstd
        vh_s[...] = vh
        vn = (vh * lnw_ref[...] + lnb_ref[...]).astype(BF16)

        zu = z_ref[:, 0:aw]
        zg = z_ref[:, 2 * aw : 3 * aw]
        u, du_dz = _gelu_and_grad(zu)
        sg, dsg = _silu_and_grad(zg)
        u_s[...] = u * sg
        tril = lax.broadcasted_iota(jnp.int32, (CHUNK, CHUNK), 0) >= lax.broadcasted_iota(jnp.int32, (CHUNK, CHUNK), 1)
        for ck in range(tm // CHUNK):
            rows = slice(ck * CHUNK, (ck + 1) * CHUNK)
            for g in range(A_GROUPS):
                cols = slice(g * gd, (g + 1) * gd)
                vn_g = vn[rows, cols]
                s = _dot(wc_ref[g], vn_g) + bst_ref[:, g : g + 1]
                usg = u_s[rows, cols]
                dy_g = dy[rows, cols]
                y_ref[rows, cols] = (usg * s).astype(BF16)
                ds = dy_g * usg
                ds_s[rows, cols] = dy_g * s
                gbs_ref[:, g : g + 1] += jnp.sum(ds, axis=-1, keepdims=True)
                dsb = ds.astype(BF16)
                gws_ref[g] += jnp.where(tril, _dot_nt(dsb, vn_g), 0.0)
                dvn_s[rows, cols] = _dot(wct_ref[g], dsb)
        dusg = ds_s[...]
        dz_ref[:, 0:aw] = (dusg * sg * du_dz).astype(BF16)
        dz_ref[:, 2 * aw : 3 * aw] = (dusg * u * dsg).astype(BF16)

        dvn = dvn_s[...]
        vh = vh_s[...]
        glnw_ref[...] += jnp.sum(dvn * vh, axis=0, keepdims=True)
        glnb_ref[...] += jnp.sum(dvn, axis=0, keepdims=True)
        dvh = dvn * lnw_ref[...]
        dvg = rstd * (dvh - jnp.mean(dvh, axis=-1, keepdims=True) - vh * jnp.mean(dvh * vh, axis=-1, keepdims=True))
        dz_ref[:, aw : 2 * aw] = (dvg * dvg_dz).astype(BF16)

    row = lambda i: (i + lo, 0)
    call = _pcall(
        body,
        hook,
        name=f"layer_a_bwd_{lo}",
        grid=(hi - lo,),
        in_specs=[
            pl.BlockSpec((tm, d), row),
            pl.BlockSpec((tm, 3 * aw), row),
            _full(ln_w.shape),
            _full(ln_b.shape),
            _full(wc.shape),
            _full(wct.shape),
            _full(bs_t.shape),
            _full(wout.shape),
        ]
        + [ANY] * n_earlier,
        out_specs=[
            pl.BlockSpec((tm, 3 * aw), row),
            pl.BlockSpec((tm, aw), row),
            pl.BlockSpec((tm, d), row),
            _full((A_GROUPS, CHUNK, CHUNK)),
            _full((CHUNK, A_GROUPS)),
            _full((1, aw)),
            _full((1, aw)),
        ],
        out_shape=[
            jax.ShapeDtypeStruct((t_rows, 3 * aw), BF16),
            jax.ShapeDtypeStruct((t_rows, aw), BF16),
            jax.ShapeDtypeStruct((t_rows, d), BF16),
            jax.ShapeDtypeStruct((A_GROUPS, CHUNK, CHUNK), F32),
            jax.ShapeDtypeStruct((CHUNK, A_GROUPS), F32),
            jax.ShapeDtypeStruct((1, aw), F32),
            jax.ShapeDtypeStruct((1, aw), F32),
        ],
        scratch_shapes=[pltpu.VMEM((tm, aw), F32)] * 4,
        input_output_aliases={8 + i: i for i in range(n_earlier)},
        compiler_params=_cparams(("arbitrary",)),
    )
    return call(dout, z, ln_w, ln_b, wc, wct, bs_t, wout, *(earlier or ()))


def _layer_a_bwd_dx(dout, x, dz, nw, win, tm, tiles, earlier, hook):
    t_rows, d = x.shape
    n_sh, _, s_cols = win.shape
    lo, hi = tiles
    n_earlier = 0 if earlier is None else 1

    def body(dout_ref, x_ref, dz_ref, nw_ref, win_ref, *rest):
        gx_ref, gnw_ref = rest[n_earlier:]

        @pl.when(pl.program_id(0) == 0)
        def _():
            gnw_ref[...] = jnp.zeros_like(gnw_ref)

        dh = jnp.zeros((tm, d), F32)
        for k in range(n_sh):
            dh = dh + _dot_nt(dz_ref[:, k * s_cols : (k + 1) * s_cols], win_ref[k])
        nw = nw_ref[...]
        _, xh, r = _rms_fwd(x_ref[...], nw)
        dx, gnw = _rms_bwd(dh, xh, r, nw)
        gnw_ref[0:1, :] += gnw
        gx_ref[...] = dout_ref[...] + dx

    row = lambda i: (i + lo, 0)
    return _pcall(
        body,
        hook,
        name=f"layer_a_bwd_dx_{lo}",
        grid=(hi - lo,),
        in_specs=[
            pl.BlockSpec((tm, d), row),
            pl.BlockSpec((tm, d), row),
            pl.BlockSpec((tm, n_sh * s_cols), row),
            _full(nw.shape),
            _full(win.shape),
        ]
        + [ANY] * n_earlier,
        out_specs=[pl.BlockSpec((tm, d), row), _full((SUBLANES, d))],
        out_shape=[jax.ShapeDtypeStruct((t_rows, d), F32), jax.ShapeDtypeStruct((SUBLANES, d), F32)],
        input_output_aliases={5: 0} if n_earlier else {},
        compiler_params=_cparams(("arbitrary",)),
    )(dout, x, dz, nw, win, *([earlier] if n_earlier else []))


def _decay(r, sp_h):
    log_a = (-RG_C) * r * sp_h
    a = jnp.exp(log_a)
    mult = jnp.sqrt(jnp.tanh(-log_a) * (a * a + 1.0))
    return a, mult


def _gates(xc_h, gab_ref, gb_ref, sp_h, h, hd):
    pre = _dot(xc_h.astype(BF16), gab_ref[h])
    bw = gb_ref.shape[1] // 2
    r = _sigmoid(pre[:, :hd] + gb_ref[:, h * hd : (h + 1) * hd])
    ig = _sigmoid(pre[:, hd:] + gb_ref[:, bw + h * hd : bw + (h + 1) * hd])
    a, mult = _decay(r, sp_h)
    return r, ig, a, mult


def _conv(xb, halo, cw_ref, cb_ref):
    xc = cb_ref[...] + cw_ref[CONV_WIDTH - 1 : CONV_WIDTH, :] * xb
    for k in range(CONV_WIDTH - 1):
        xc = xc + cw_ref[k : k + 1, :] * _shift_down(xb, halo, CONV_WIDTH - 1 - k)
    return xc


def _layer_b_fwd(x1, nw, bin_w, cw, cb, gab, gb, lam, bout, nf, tgt, tm):
    t_rows, d = x1.shape
    bw = bout.shape[0]
    hd = bw // B_HEADS
    nt = t_rows // tm

    def body(
        x1_ref, nw_ref, bin_ref, cw_ref, cb_ref, gab_ref, gb_ref, lam_ref, bout_ref, nf_ref, tgt_ref,
        z_ref, h_ref, h1_ref, xbt_ref, ht_ref, dx2_ref, loss_ref, gnf_ref,
        tail_s, carry_s, a_s, b_s, hs_s, acc_s,
    ):
        @pl.when(pl.program_id(0) == 0)
        def _():
            tail_s[...] = jnp.zeros_like(tail_s)
            carry_s[...] = jnp.zeros_like(carry_s)
            acc_s[...] = jnp.zeros_like(acc_s)
            gnf_ref[...] = jnp.zeros_like(gnf_ref)

        x1 = x1_ref[...]
        h1, _, _ = _rms_fwd(x1, nw_ref[...])
        h1 = h1.astype(BF16)
        h1_ref[...] = h1
        z = jnp.concatenate([_dot(h1, bin_ref[k]) for k in range(N_CHIPS)], axis=1)
        z_ref[...] = z
        xb = z[:, :bw]
        xc = _conv(xb, tail_s[...], cw_ref, cb_ref)
        tail = xb[tm - SUBLANES :, :]
        tail_s[...] = tail
        xbt_ref[...] = tail
        sp = _softplus_neg(lam_ref[...])
        for h in range(B_HEADS):
            cols = slice(h * hd, (h + 1) * hd)
            xc_h = xc[:, cols]
            _, ig, a, mult = _gates(xc_h, gab_ref, gb_ref, sp[:, cols], h, hd)
            a_s[:, cols] = a
            b_s[:, cols] = mult * (ig * xc_h)
        carry = _scan_blocks(a_s, b_s, hs_s, carry_s[...], tm, reverse=False)
        carry_s[...] = carry
        ht_ref[...] = hs_s[tm - SUBLANES :, :]
        hs = hs_s[...]
        h_ref[...] = hs
        g = z[:, bw:]
        y = (hs * (g * _sigmoid(g))).astype(BF16)
        x2 = x1 + _dot(y, bout_ref[...])

        nf = nf_ref[...]
        o, xh, r = _rms_fwd(x2, nf)
        diff = o - tgt_ref[...]
        acc_s[...] += jnp.sum(diff * diff, axis=0, keepdims=True)
        do = diff * (1.0 / d)
        dx2, gnf = _rms_bwd(do, xh, r, nf)
        gnf_ref[...] += gnf
        dx2_ref[...] = dx2

        @pl.when(pl.program_id(0) == nt - 1)
        def _():
            total = jnp.sum(acc_s[...], axis=-1, keepdims=True) * (0.5 / d)
            loss_ref[...] = jnp.broadcast_to(total, loss_ref.shape)

    row = lambda i: (i, 0)
    return _pcall(
        body,
        name="layer_b_fwd",
        grid=(nt,),
        in_specs=[
            pl.BlockSpec((tm, d), row),
            _full(nw.shape),
            _full(bin_w.shape),
            _full(cw.shape),
            _full(cb.shape),
            _full(gab.shape),
            _full(gb.shape),
            _full(lam.shape),
            _full(bout.shape),
            _full(nf.shape),
            pl.BlockSpec((tm, d), row),
        ],
        out_specs=[
            pl.BlockSpec((tm, 2 * bw), row),
            pl.BlockSpec((tm, bw), row),
            pl.BlockSpec((tm, d), row),
            pl.BlockSpec((None, SUBLANES, bw), lambda i: (i, 0, 0)),
            pl.BlockSpec((None, SUBLANES, bw), lambda i: (i, 0, 0)),
            pl.BlockSpec((tm, d), row),
            _full((1, LANES)),
            _full((1, d)),
        ],
        out_shape=[
            jax.ShapeDtypeStruct((t_rows, 2 * bw), F32),
            jax.ShapeDtypeStruct((t_rows, bw), F32),
            jax.ShapeDtypeStruct((t_rows, d), BF16),
            jax.ShapeDtypeStruct((nt, SUBLANES, bw), F32),
            jax.ShapeDtypeStruct((nt, SUBLANES, bw), F32),
            jax.ShapeDtypeStruct((t_rows, d), F32),
            jax.ShapeDtypeStruct((1, LANES), F32),
            jax.ShapeDtypeStruct((1, d), F32),
        ],
        scratch_shapes=[
            pltpu.VMEM((SUBLANES, bw), F32),
            pltpu.VMEM((SUBLANES, bw), F32),
            pltpu.VMEM((tm, bw), F32),
            pltpu.VMEM((tm, bw), F32),
            pltpu.VMEM((tm, bw), F32),
            pltpu.VMEM((1, d), F32),
        ],
        compiler_params=_cparams(("arbitrary",)),
    )(x1, nw, bin_w, cw, cb, gab, gb, lam, bout, nf, tgt)


def _layer_b_bwd(dout, x1, z, hseq, xb_tails, h_tails, nw, bin_w, cw, cb, gab, gabt, gb, lam, bout, tm):
    t_rows, d = x1.shape
    bw = bout.shape[0]
    hd = bw // B_HEADS
    nt = t_rows // tm

    def body(
        dout_ref, x1_ref, z_ref, h_ref, xbt_ref, ht_ref, nw_ref, bin_ref, cw_ref, cb_ref, gab_ref, gabt_ref,
        gb_ref, lam_ref, bout_ref,
        dx1_ref, dz_ref, y_ref, dob_ref, ggab_ref, ggb_ref, gcw_ref, gcb_ref, glam_ref, gnw_ref,
        gcarry_s, afirst_s, head_s, aup_s, dh_s, gt_s, dxc_s, xc_s, r_s, ig_s,
    ):
        step = pl.program_id(0)
        tile = nt - 1 - step

        @pl.when(step == 0)
        def _():
            for ref in (ggab_ref, ggb_ref, gcw_ref, gcb_ref, glam_ref, gnw_ref, gcarry_s, afirst_s, head_s):
                ref[...] = jnp.zeros_like(ref)

        first_tile = tile == 0
        xb_halo = jnp.where(first_tile, 0.0, xbt_ref[...])
        h_halo = jnp.where(first_tile, 0.0, ht_ref[...])

        dout = dout_ref[...]
        dob = dout.astype(BF16)
        dob_ref[...] = dob
        dy = _dot_nt(dob, bout_ref[...])
        hs = h_ref[...]
        g = z_ref[:, bw:]
        sg, dsg = _silu_and_grad(g)
        y_ref[...] = (hs * sg).astype(BF16)
        dz_ref[:, bw:] = (dy * hs * dsg).astype(BF16)
        dh_s[...] = dy * sg

        xb = z_ref[:, :bw]
        xc = _conv(xb, xb_halo, cw_ref, cb_ref)
        xc_s[...] = xc
        lam = lam_ref[...]
        sp = _softplus_neg(lam)
        for h in range(B_HEADS):
            cols = slice(h * hd, (h + 1) * hd)
            r, ig, a, _ = _gates(xc[:, cols], gab_ref, gb_ref, sp[:, cols], h, hd)
            r_s[:, cols] = r
            ig_s[:, cols] = ig
            aup_s[:, cols] = _shift_up(a, afirst_s[:, cols], 1)
            afirst_s[:, cols] = jnp.broadcast_to(a[0:1, :], (SUBLANES, hd))
        carry = _scan_blocks(aup_s, dh_s, gt_s, gcarry_s[...], tm, reverse=True)
        gcarry_s[...] = carry

        h_prev = _shift_down(hs, h_halo, 1)
        for h in range(B_HEADS):
            cols = slice(h * hd, (h + 1) * hd)
            xc_h = xc_s[:, cols]
            sp_h = sp[:, cols]
            r, ig = r_s[:, cols], ig_s[:, cols]
            a, mult = _decay(r, sp_h)
            gt = gt_s[:, cols]
            da = gt * h_prev[:, cols]
            dmult = gt * (ig * xc_h)
            dig = gt * (mult * xc_h)
            dxc_direct = gt * (mult * ig)
            dla = da * a - dmult * (a * a) / mult
            glam_ref[:, cols] += jnp.sum(dla * r, axis=0, keepdims=True)
            dr = dla * ((-RG_C) * sp_h)
            dpre = jnp.concatenate([dr * r * (1.0 - r), dig * ig * (1.0 - ig)], axis=1)
            ggb_ref[:, cols] += jnp.sum(dpre[:, :hd], axis=0, keepdims=True)
            ggb_ref[:, bw + h * hd : bw + (h + 1) * hd] += jnp.sum(dpre[:, hd:], axis=0, keepdims=True)
            dpb = dpre.astype(BF16)
            ggab_ref[h] += _dot_tn(xc_h.astype(BF16), dpb)
            dxc_s[:, cols] = dxc_direct + _dot(dpb, gabt_ref[h])
        glam_ref[...] = jnp.where(step == nt - 1, glam_ref[...] * (RG_C * _sigmoid(-lam)), glam_ref[...])

        dxc = dxc_s[...]
        gcb_ref[...] += jnp.sum(dxc, axis=0, keepdims=True)
        dxb = cw_ref[CONV_WIDTH - 1 : CONV_WIDTH, :] * dxc
        gcw_ref[CONV_WIDTH - 1 : CONV_WIDTH, :] += jnp.sum(dxc * xb, axis=0, keepdims=True)
        head = head_s[...]
        for k in range(CONV_WIDTH - 1):
            lag = CONV_WIDTH - 1 - k
            dxb = dxb + cw_ref[k : k + 1, :] * _shift_up(dxc, head, lag)
            gcw_ref[k : k + 1, :] += jnp.sum(dxc * _shift_down(xb, xb_halo, lag), axis=0, keepdims=True)
        head_s[...] = dxc[:SUBLANES, :]
        dz_ref[:, :bw] = dxb.astype(BF16)

        s_cols = 2 * bw // N_CHIPS
        dh1 = jnp.zeros((tm, d), F32)
        for k in range(N_CHIPS):
            dh1 = dh1 + _dot_nt(dz_ref[:, k * s_cols : (k + 1) * s_cols], bin_ref[k])
        x1 = x1_ref[...]
        nw = nw_ref[...]
        _, xh, r1 = _rms_fwd(x1, nw)
        dx, gnw = _rms_bwd(dh1, xh, r1, nw)
        gnw_ref[...] += gnw
        dx1_ref[...] = dout + dx

    rev = lambda i: (nt - 1 - i, 0)
    prev = lambda i: (jnp.maximum(nt - 2 - i, 0), 0, 0)
    return _pcall(
        body,
        name="layer_b_bwd",
        grid=(nt,),
        in_specs=[
            pl.BlockSpec((tm, d), rev),
            pl.BlockSpec((tm, d), rev),
            pl.BlockSpec((tm, 2 * bw), rev),
            pl.BlockSpec((tm, bw), rev),
            pl.BlockSpec((None, SUBLANES, bw), prev),
            pl.BlockSpec((None, SUBLANES, bw), prev),
            _full(nw.shape),
            _full(bin_w.shape),
            _full(cw.shape),
            _full(cb.shape),
            _full(gab.shape),
            _full(gabt.shape),
            _full(gb.shape),
            _full(lam.shape),
            _full(bout.shape),
        ],
        out_specs=[
            pl.BlockSpec((tm, d), rev),
            pl.BlockSpec((tm, 2 * bw), rev),
            pl.BlockSpec((tm, bw), rev),
            pl.BlockSpec((tm, d), rev),
            _full((B_HEADS, hd, 2 * hd)),
            _full((1, 2 * bw)),
            _full((SUBLANES, bw)),
            _full((1, bw)),
            _full((1, bw)),
            _full((1, d)),
        ],
        out_shape=[
            jax.ShapeDtypeStruct((t_rows, d), F32),
            jax.ShapeDtypeStruct((t_rows, 2 * bw), BF16),
            jax.ShapeDtypeStruct((t_rows, bw), BF16),
            jax.ShapeDtypeStruct((t_rows, d), BF16),
            jax.ShapeDtypeStruct((B_HEADS, hd, 2 * hd), F32),
            jax.ShapeDtypeStruct((1, 2 * bw), F32),
            jax.ShapeDtypeStruct((SUBLANES, bw), F32),
            jax.ShapeDtypeStruct((1, bw), F32),
            jax.ShapeDtypeStruct((1, bw), F32),
            jax.ShapeDtypeStruct((1, d), F32),
        ],
        scratch_shapes=[pltpu.VMEM((SUBLANES, bw), F32)] * 3 + [pltpu.VMEM((tm, bw), F32)] * 7,
        compiler_params=_cparams(("arbitrary",)),
    )(dout, x1, z, hseq, xb_tails, h_tails, nw, bin_w, cw, cb, gab, gabt, gb, lam, bout)


def _wgrad(a, b, m_blocks, n_blocks, hook=None):
    k, m = a.shape
    n = b.shape[1]
    bm, bn = m // m_blocks, n // n_blocks

    def body(a_ref, b_ref, o_ref):
        o_ref[...] = _dot_tn(a_ref[...], b_ref[...])

    out = _pcall(
        body,
        hook,
        name=f"wgrad_{m}x{n}",
        grid=(n_blocks, m_blocks),
        in_specs=[pl.BlockSpec((k, bm), lambda j, i: (0, i)), pl.BlockSpec((k, bn), lambda j, i: (0, j))],
        out_specs=[pl.BlockSpec((None, None, bm, bn), lambda j, i: (j, i, 0, 0))],
        out_shape=[jax.ShapeDtypeStruct((n_blocks, m_blocks, bm, bn), F32)],
        compiler_params=_cparams(("arbitrary", "arbitrary")),
    )(a, b)
    return out[0] if hook is None else (out[0][0], out[1])


def _adamw_math(w, g, m, v):
    m = ADAM_B1 * m + (1.0 - ADAM_B1) * g
    v = ADAM_B2 * v + (1.0 - ADAM_B2) * (g * g)
    m_hat = m / (1.0 - ADAM_B1**ADAM_STEP)
    v_hat = v / (1.0 - ADAM_B2**ADAM_STEP)
    delta = -ADAM_LR * (m_hat / (jnp.sqrt(v_hat) + ADAM_EPS) + ADAM_WD * w)
    return delta, m, v


def _adamw(w, g, m, v, hook=None):
    rows, cols = w.shape
    tr = _row_tile(rows, cols, 1024 * 1024)

    def body(w_ref, g_ref, m_ref, v_ref, d_ref, mo_ref, vo_ref):
        d_ref[...], mo_ref[...], vo_ref[...] = _adamw_math(w_ref[...], g_ref[...], m_ref[...], v_ref[...])

    spec = pl.BlockSpec((tr, cols), lambda i: (i, 0))
    return _pcall(
        body,
        hook,
        name=f"adamw_{rows}x{cols}",
        grid=(rows // tr,),
        in_specs=[spec] * 4,
        out_specs=[spec] * 3,
        out_shape=[jax.ShapeDtypeStruct((rows, cols), F32)] * 3,
        compiler_params=_cparams(("arbitrary",)),
    )(w, g, m, v)


def _sum_partials(parts):
    def body(p_ref, o_ref):
        total = p_ref[0, 0:1, :]
        for k in range(1, N_DEV):
            total = total + p_ref[k, 0:1, :]
        o_ref[...] = total

    vmem = pl.BlockSpec(memory_space=pltpu.VMEM)
    return _pcall(
        body,
        name="sum_partials",
        in_specs=[vmem],
        out_specs=vmem,
        out_shape=jax.ShapeDtypeStruct((1, parts.shape[2]), F32),
    )(parts)


def _adamw_many(ws, gs, ms, vs, name, hook=None):
    n = len(ws)

    def body(*refs):
        w_refs, g_refs, m_refs, v_refs = (refs[i * n : (i + 1) * n] for i in range(4))
        d_refs, mo_refs, vo_refs = (refs[(4 + i) * n : (5 + i) * n] for i in range(3))
        for i in range(n):
            d_refs[i][...], mo_refs[i][...], vo_refs[i][...] = _adamw_math(
                w_refs[i][...], g_refs[i][...], m_refs[i][...], v_refs[i][...]
            )

    vmem = pl.BlockSpec(memory_space=pltpu.VMEM)
    outs = _pcall(
        body,
        hook,
        name=name,
        in_specs=[vmem] * (4 * n),
        out_specs=[vmem] * (3 * n),
        out_shape=[jax.ShapeDtypeStruct(w.shape, F32) for w in ws] * 3,
        compiler_params=_cparams(),
    )(*ws, *gs, *ms, *vs)
    extra = None
    if hook is not None:
        outs, extra = outs
    return (outs[:n], outs[n : 2 * n], outs[2 * n :]), extra


def _pack_rows(parts, lanes=LANES):
    flat = jnp.concatenate([p.reshape(-1) for p in parts])
    per = N_DEV * SUBLANES * lanes
    total = -(-flat.shape[0] // per) * per
    flat = jnp.pad(flat, (0, total - flat.shape[0]))
    return flat.reshape(N_DEV, total // (N_DEV * lanes), lanes)


def _unpack(flat, shapes):
    out, at = [], 0
    for s in shapes:
        n = 1
        for dim in s:
            n *= dim
        out.append(flat[at : at + n].reshape(s))
        at += n
    return out


def kernel(x, norm_w, a_w_in, a_ln_w, a_ln_b, a_w_s, a_b_s, a_w_out, b_w_in, b_conv_w, b_conv_b, b_gate_a_w, b_gate_a_b, b_gate_x_w, b_gate_x_b, b_lambda, b_w_out, norm_f_w, loss_target, m_norm_w, m_a_w_in, m_a_ln_w, m_a_ln_b, m_a_w_s, m_a_b_s, m_a_w_out, m_b_w_in, m_b_conv_w, m_b_conv_b, m_b_gate_a_w, m_b_gate_a_b, m_b_gate_x_w, m_b_gate_x_b, m_b_lambda, m_b_w_out, m_norm_f_w, v_norm_w, v_a_w_in, v_a_ln_w, v_a_ln_b, v_a_w_s, v_a_b_s, v_a_w_out, v_b_w_in, v_b_conv_w, v_b_conv_b, v_b_gate_a_w, v_b_gate_a_b, v_b_gate_x_w, v_b_gate_x_b, v_b_lambda, v_b_w_out, v_norm_f_w):
    t_rows, d = x.shape[1], x.shape[2]
    aw = a_ln_w.shape[1]
    bw = b_gate_a_w.shape[1] * b_gate_a_w.shape[2]
    hd = bw // B_HEADS
    mine = 2 * lax.axis_index("x") + lax.axis_index("y")
    core = lax.axis_index("c")
    weights = dict(norm_w=norm_w, a_w_in=a_w_in, a_ln_w=a_ln_w, a_ln_b=a_ln_b, a_w_s=a_w_s, a_b_s=a_b_s, a_w_out=a_w_out, b_w_in=b_w_in, b_conv_w=b_conv_w, b_conv_b=b_conv_b, b_gate_a_w=b_gate_a_w, b_gate_a_b=b_gate_a_b, b_gate_x_w=b_gate_x_w, b_gate_x_b=b_gate_x_b, b_lambda=b_lambda, b_w_out=b_w_out, norm_f_w=norm_f_w)
    m_in = dict(norm_w=m_norm_w, a_w_in=m_a_w_in, a_ln_w=m_a_ln_w, a_ln_b=m_a_ln_b, a_w_s=m_a_w_s, a_b_s=m_a_b_s, a_w_out=m_a_w_out, b_w_in=m_b_w_in, b_conv_w=m_b_conv_w, b_conv_b=m_b_conv_b, b_gate_a_w=m_b_gate_a_w, b_gate_a_b=m_b_gate_a_b, b_gate_x_w=m_b_gate_x_w, b_gate_x_b=m_b_gate_x_b, b_lambda=m_b_lambda, b_w_out=m_b_w_out, norm_f_w=m_norm_f_w)
    v_in = dict(norm_w=v_norm_w, a_w_in=v_a_w_in, a_ln_w=v_a_ln_w, a_ln_b=v_a_ln_b, a_w_s=v_a_w_s, a_b_s=v_a_b_s, a_w_out=v_a_w_out, b_w_in=v_b_w_in, b_conv_w=v_b_conv_w, b_conv_b=v_b_conv_b, b_gate_a_w=v_b_gate_a_w, b_gate_a_b=v_b_gate_a_b, b_gate_x_w=v_b_gate_x_w, b_gate_x_b=v_b_gate_x_b, b_lambda=v_b_lambda, b_w_out=v_b_w_out, norm_f_w=v_norm_f_w)

    win_l = _cast_to_segments(a_w_in[0], mine, 256)
    wout_l = _cast_to_segments(a_w_out[0], mine, 256)
    small_l = jnp.concatenate([b_conv_w[0], b_conv_b, b_gate_a_b, b_gate_x_b, b_lambda], axis=0)
    bin_l, (win_g, wout_g, small_g) = _cast_to_segments(
        b_w_in[0], mine, 128, _gather_hook([win_l, wout_l], small_l)
    )
    bout_l = _cast_to_segments(b_w_out[0], mine, 192)
    win = win_g.reshape(N_CHIPS, d, -1)
    wout = wout_g.reshape(aw, d)

    tril = jnp.tril(jnp.ones((CHUNK, CHUNK), F32))
    wc = (a_w_s[0] * tril[None]).astype(BF16)
    wct = jnp.swapaxes(wc, 1, 2)
    bs_t = a_b_s[0].T
    gab = jnp.concatenate([b_gate_a_w[0], b_gate_x_w[0]], axis=2).astype(BF16)
    gabt = jnp.swapaxes(gab, 1, 2)
    nw0, nw1, nf = norm_w[0:1], norm_w[1:2], norm_f_w.reshape(1, d)

    x0 = x[0]
    (z_a, x1, h0), (bin_g, bout_g) = _layer_a_fwd(
        x0, nw0, win, a_ln_w, a_ln_b, wc, bs_t, wout, TM_FWD, _gather_hook([bin_l, bout_l])
    )
    bin_w = bin_g.reshape(N_CHIPS, d, -1)
    bout = bout_g.reshape(bw, d)
    small_f = jnp.transpose(small_g, (1, 0, 2)).reshape(SUBLANES, bw)
    cw, cb = small_f[0:CONV_WIDTH], small_f[CONV_WIDTH : CONV_WIDTH + 1]
    gb = jnp.concatenate([small_f[5:6], small_f[6:7]], axis=1)
    lam = small_f[7:8]
    z_b, hseq, h1, xb_tails, h_tails, dx2, loss_l, g_nf = _layer_b_fwd(
        x1, nw1, bin_w, cw, cb, gab, gb, lam, bout, nf, loss_target[0], TM_FWD
    )
    dx1, dz_b, y_b, dob_b, g_gab, g_gb, g_cw, g_cb, g_lam, g_nw1 = _layer_b_bwd(
        dx2, x1, z_b, hseq, xb_tails, h_tails, nw1, bin_w, cw, cb, gab, gabt, gb, lam, bout, TM_FWD
    )
    seg = lambda g: g.reshape(N_DEV, -1, g.shape[3])
    x_at, y_at = lax.axis_index("x"), lax.axis_index("y")
    first_no = 2 * (x_at ^ (1 - core)) + (y_at ^ core)
    second_no = 2 * (x_at ^ core) + (y_at ^ (1 - core))
    bf16s = lambda bufs: [BF16] * len(bufs)
    own_half = lambda bufs, got, wires: [
        _add_own_half(b, g, (first_no, N_CHIPS - 1 - mine), core, w) for b, g, w in zip(bufs, got, wires)
    ]
    for_neighbour = lambda bufs, got_a, got1, wires: [
        _add_for_neighbour(b, ga, g1, second_no, core, w) for b, ga, g1, w in zip(bufs, got_a, got1, wires)
    ]
    received = lambda bufs, got_a, got1, got2: [
        _add_received(b, ga, g1, g2, mine, core, core, 2) for b, ga, g1, g2 in zip(bufs, got_a, got1, got2)
    ]

    g_bout = [seg(_wgrad(y_b, dob_b, 2, 1))]
    g_bin, swap_o = _wgrad(h1, dz_b, 1, N_CHIPS, _swap_hook(g_bout))
    g_bin = [seg(g_bin)]
    part_o = own_half(g_bout, swap_o, bf16s(g_bout))
    a_args = (z_a, a_ln_w, a_ln_b, wc, wct, bs_t, wout)
    half = t_rows // TM_A_BWD // 2
    first, rode = _layer_a_bwd(
        dx1, *a_args, (0, half), None, _join_hooks(_swap_hook(g_bin), _send_first_hook(part_o))
    )
    swap_i, got1_o = rode[:1], rode[1:]
    part_i = own_half(g_bin, swap_i, bf16s(g_bin))
    mid_o = for_neighbour(g_bout, swap_o, got1_o, bf16s(g_bout))
    second, rode = _layer_a_bwd(
        dx1, *a_args, (half, 2 * half), first[:3], _join_hooks(_send_first_hook(part_i), _send_second_hook(mid_o))
    )
    got1_i, got2_o = rode[:1], rode[1:]
    dz_a, y_a, dob_a = second[:3]
    g_ws, g_bst, g_lnw, g_lnb = (p + q for p, q in zip(first[3:], second[3:]))
    mid_i = for_neighbour(g_bin, swap_i, got1_i, bf16s(g_bin))
    red_o = received(g_bout, swap_o, got1_o, got2_o)
    g_win, rode = _wgrad(h0, dz_a, 1, N_CHIPS, _join_hooks(_send_second_hook(mid_i), _share_hook(red_o)))
    g_win = [seg(g_win)]
    got2_i, gr_bout = rode[:1], rode[1].reshape(b_w_out.shape[1:])
    red_i = received(g_bin, swap_i, got1_i, got2_i)

    small_shapes = [
        (1, d), (1, aw), (1, aw), (A_GROUPS, CHUNK, CHUNK), (A_GROUPS, CHUNK), (B_HEADS, hd, hd), (B_HEADS, hd, hd),
        (d,), (CONV_WIDTH, bw), (1, bw), (1, bw), (1, bw), (1, bw), (1, 1),
    ]
    small = _pack_rows(
        [
            g_nw1, g_lnw, g_lnb, g_ws, g_bst.T, g_gab[:, :, :hd], g_gab[:, :, hd:],
            g_nf, g_cw[:CONV_WIDTH], g_cb, g_gb[:, :bw], g_gb[:, bw:], g_lam, loss_l[:, :1],
        ]
    )
    g_w, wire_w = g_win + [small], [BF16, F32]
    g_wout, rode = _wgrad(y_a, dob_a, N_CHIPS, 1, _join_hooks(_swap_hook(g_w), _share_hook(red_i)))
    g_wout = seg(g_wout)
    swap_w, gr_bin = rode[:2], rode[2].reshape(b_w_in.shape[1:])

    g_u, wire_u = [g_wout], [BF16]
    part_w = own_half(g_w, swap_w, wire_w)
    (grad_x, g_nw0_mine), rode = _layer_a_bwd_dx(
        dx1, x0, dz_a, nw0, win, TM_A_DX, (0, t_rows // TM_A_DX), None,
        _join_hooks(_send_first_hook(part_w), _swap_hook(g_u)),
    )
    got1_w, swap_u = rode[:2], rode[2:]
    part_u = own_half(g_u, swap_u, wire_u)
    mid_w = for_neighbour(g_w, swap_w, got1_w, wire_w)
    b_names = ("b_w_in", "b_w_out")
    two_d = lambda a: a.reshape(a.shape[-2:])
    bin_out, rode = _adamw(
        two_d(b_w_in), gr_bin, two_d(m_b_w_in), two_d(v_b_w_in),
        _join_hooks(_send_second_hook(mid_w), _send_first_hook(part_u)),
    )
    got2_w, got1_u = rode[:2], rode[2:]
    mid_u = for_neighbour(g_u, swap_u, got1_u, wire_u)
    red_w = received(g_win, swap_w[:1], got1_w[:1], got2_w[:1])
    red_small = _add_received(small, swap_w[1], got1_w[1], got2_w[1], mine, core, 2 * mine + core, N_DEV)
    bout_out, rode = _adamw(
        two_d(b_w_out), gr_bout, two_d(m_b_w_out), two_d(v_b_w_out),
        _join_hooks(_send_second_hook(mid_u), _share_hook(red_w, red_small)),
    )
    b_out = list(zip(bin_out, bout_out))
    got2_u, gr_win, small_r = rode[:1], rode[1].reshape(a_w_in.shape[1:]), rode[2]
    red_wout = received(g_u, swap_u, got1_u, got2_u)
    gr_wout, g_nw0_all = _run_hook(_share_hook(red_wout, None, g_nw0_mine), "share_reduced")
    win_out = _adamw(two_d(a_w_in), gr_win, two_d(m_a_w_in), two_d(v_a_w_in))
    g_nw0 = _sum_partials(g_nw0_all)
    gr_wout = gr_wout.reshape(a_w_out.shape[1:])
    (g_nw1_r, g_a_ln_w, g_a_ln_b, g_a_w_s, g_a_b_s, g_gate_a_w, g_gate_x_w, g_norm_f, gf_cw, gf_cb, gf_gab, gf_gxb,
     gf_lam, loss) = _unpack(small_r.reshape(-1), small_shapes)
    g_norm_w = jnp.concatenate([g_nw0, g_nw1_r], axis=0)
    shard = lambda g: lax.dynamic_slice_in_dim(g, mine * (bw // N_CHIPS), bw // N_CHIPS, axis=1)

    grads = {
        "norm_w": g_norm_w, "a_w_in": gr_win[None], "a_ln_w": g_a_ln_w, "a_ln_b": g_a_ln_b, "a_w_s": g_a_w_s[None],
        "a_b_s": g_a_b_s[None], "a_w_out": gr_wout[None], "b_w_in": gr_bin[None], "b_conv_w": shard(gf_cw)[None],
        "b_conv_b": shard(gf_cb), "b_gate_a_w": g_gate_a_w[None], "b_gate_a_b": shard(gf_gab),
        "b_gate_x_w": g_gate_x_w[None], "b_gate_x_b": shard(gf_gxb), "b_lambda": shard(gf_lam),
        "b_w_out": gr_bout[None], "norm_f_w": g_norm_f,
    }
    names = list(weights)
    delta, new_m, new_v = {}, {}, {}
    delta["a_w_in"], new_m["a_w_in"], new_v["a_w_in"] = win_out
    delta["a_w_out"], new_m["a_w_out"], new_v["a_w_out"] = _adamw(
        two_d(a_w_out), gr_wout, two_d(m_a_w_out), two_d(v_a_w_out)
    )
    small_names = [n for n in names if n not in ("a_w_in", "a_w_out") + b_names]
    at_least_2d = lambda a: a.reshape(1, -1) if a.ndim == 1 else a
    small_out, _ = _adamw_many(
        *[[at_least_2d(src[n]) for n in small_names] for src in (weights, grads, m_in, v_in)], "adamw_small"
    )
    for dst, vals, b_vals in zip((delta, new_m, new_v), small_out, b_out):
        dst.update(zip(small_names, vals))
        dst.update(zip(b_names, b_vals))
    for dst in (delta, new_m, new_v):
        for n in names:
            dst[n] = dst[n].reshape(weights[n].shape)

    return (
        loss.reshape(()),
        grad_x[None],
        *[grads[n] for n in names],
        *[delta[n] for n in names],
        *[new_m[n] for n in names],
        *[new_v[n] for n in names],
    )
```

```python
import jax
import jax.numpy as jnp
from jax import lax
from jax.experimental import pallas as pl
from jax.experimental.pallas import tpu as pltpu

F32 = jnp.float32
BF16 = jnp.bfloat16

RMS_EPS = 1e-6
LN_EPS = 1e-5
RG_C = 8.0
CHUNK = 128
A_GROUPS = 8
B_HEADS = 12
CONV_WIDTH = 4

ADAM_LR = 0.001
ADAM_B1 = 0.9
ADAM_B2 = 0.999
ADAM_EPS = 1e-08
ADAM_WD = 0.01
ADAM_STEP = 10

N_CHIPS = 4
N_DEV = 8
SUBLANES = 8
LANES = 128
V7X_VMEM_BYTES = 64 * 1024 * 1024
VMEM_LIMIT = V7X_VMEM_BYTES * 7 // 8
MESH = pl.DeviceIdType.MESH
ANY = pl.BlockSpec(memory_space=pl.ANY)

TM_FWD = 256
TM_A_BWD = 256
TM_A_DX = 512

GELU_C0 = 0.7978845608028654
GELU_C1 = 0.044715


class _Hook:
    def __init__(self, operands, out_shapes, aliases, n_sems, start, finish, middle=None, late=None):
        self.operands, self.out_shapes, self.aliases, self.n_sems = operands, out_shapes, aliases, n_sems
        self.start, self.finish, self.middle, self.late = start, finish, middle, late


class _SemView:
    def __init__(self, base, off):
        self.base, self.off = base, off

    @property
    def at(self):
        return self

    def __getitem__(self, k):
        return self.base.at[self.off + k]


def _join_hooks(*hooks):
    if len(hooks) == 1:
        return hooks[0]
    operands, out_shapes, aliases, spans = [], [], {}, []
    n_sems = 0
    for h in hooks:
        aliases.update({len(operands) + i: len(out_shapes) + o for i, o in h.aliases.items()})
        spans.append((len(operands), len(h.operands), len(out_shapes), len(h.out_shapes), n_sems))
        operands += list(h.operands)
        out_shapes += list(h.out_shapes)
        n_sems += h.n_sems

    def each(which):
        def run(ins, outs, send, recv):
            for h, (i0, ni, o0, no, s0) in zip(hooks, spans):
                step = getattr(h, which)
                if step is not None:
                    step(ins[i0 : i0 + ni], outs[o0 : o0 + no], _SemView(send, s0), _SemView(recv, s0))

        return run

    middle = each("middle") if any(h.middle is not None for h in hooks) else None
    late = each("late") if any(h.late is not None for h in hooks) else None
    return _Hook(operands, out_shapes, aliases, n_sems, each("start"), each("finish"), middle, late)


def _pcall(body, hook=None, **kw):
    if hook is None:
        return pl.pallas_call(body, **kw)
    n_pre = 0
    if "grid_spec" in kw:
        spec = kw.pop("grid_spec")
        n_pre = spec.num_scalar_prefetch
        kw.update(
            grid=tuple(spec.grid), in_specs=list(spec.in_specs), out_specs=list(spec.out_specs),
            scratch_shapes=list(spec.scratch_shapes),
        )
    n_in, n_out = len(kw["in_specs"]), len(kw["out_shape"])
    hi, ho = len(hook.operands), len(hook.out_shapes)
    grid = kw.get("grid", ())

    def wrapped(*refs):
        pre, refs = refs[:n_pre], refs[n_pre:]
        ins, h_in = refs[:n_in], refs[n_in : n_in + hi]
        outs = refs[n_in + hi : n_in + hi + n_out]
        h_out = refs[n_in + hi + n_out : n_in + hi + n_out + ho]
        scratch = refs[n_in + hi + n_out + ho : -2]
        send_sems, recv_sems = refs[-2:]
        if not grid:
            hook.start(h_in, h_out, send_sems, recv_sems)
            if hook.middle is not None:
                hook.middle(h_in, h_out, send_sems, recv_sems)
            body(*pre, *ins, *outs, *scratch)
            if hook.late is not None:
                hook.late(h_in, h_out, send_sems, recv_sems)
            hook.finish(h_in, h_out, send_sems, recv_sems)
            return
        first = pl.program_id(0) == 0
        last = pl.program_id(0) == grid[0] - 1
        for axis in range(1, len(grid)):
            first = jnp.logical_and(first, pl.program_id(axis) == 0)
            last = jnp.logical_and(last, pl.program_id(axis) == grid[axis] - 1)

        @pl.when(first)
        def _():
            hook.start(h_in, h_out, send_sems, recv_sems)

        for when, step in ((hook.middle, grid[0] // 4), (hook.late, grid[0] - 1)):
            if when is not None:
                assert len(grid) == 1 and grid[0] >= 4

                @pl.when(pl.program_id(0) == step)
                def _(when=when):
                    when(h_in, h_out, send_sems, recv_sems)

        body(*pre, *ins, *outs, *scratch)

        @pl.when(last)
        def _():
            hook.finish(h_in, h_out, send_sems, recv_sems)

    aliases = dict(kw.pop("input_output_aliases", {}))
    aliases.update({n_pre + n_in + i: n_out + o for i, o in hook.aliases.items()})
    kw.update(
        in_specs=list(kw["in_specs"]) + [ANY] * hi,
        out_specs=list(kw["out_specs"]) + [ANY] * ho,
        out_shape=list(kw["out_shape"]) + list(hook.out_shapes),
        scratch_shapes=list(kw.get("scratch_shapes", ()))
        + [pltpu.SemaphoreType.DMA((hook.n_sems,)), pltpu.SemaphoreType.DMA((hook.n_sems,))],
        input_output_aliases=aliases,
    )
    if n_pre:
        kw["grid_spec"] = pltpu.PrefetchScalarGridSpec(
            num_scalar_prefetch=n_pre, grid=kw.pop("grid"), in_specs=kw.pop("in_specs"),
            out_specs=kw.pop("out_specs"), scratch_shapes=kw.pop("scratch_shapes"),
        )
    call = pl.pallas_call(wrapped, **kw)

    def run(*operands):
        outs = call(*operands, *hook.operands)
        return outs[:n_out], outs[n_out:]

    return run


def _run_hook(hook, name):
    def body():
        pass

    return _pcall(body, hook, name=name, in_specs=[], out_specs=[], out_shape=[])()[1]


def _cparams(sem=None):
    return pltpu.CompilerParams(dimension_semantics=sem, vmem_limit_bytes=VMEM_LIMIT)


def _full(shape):
    zeros = (0,) * len(shape)
    return pl.BlockSpec(shape, lambda *_: zeros)


def _scalars(*vals):
    return jnp.stack([jnp.asarray(v, jnp.int32) for v in vals])


def _sigmoid(x):
    return 1.0 / (1.0 + jnp.exp(-x))


def _gelu(x):
    t = jnp.tanh(GELU_C0 * (x + GELU_C1 * (x * x * x)))
    return x * (0.5 * (1.0 + t))


def _gelu_and_grad(x):
    x2 = x * x
    t = jnp.tanh(GELU_C0 * (x + GELU_C1 * (x2 * x)))
    cdf = 0.5 * (1.0 + t)
    return x * cdf, cdf + 0.5 * x * (1.0 - t * t) * (GELU_C0 * (1.0 + 3.0 * GELU_C1 * x2))


def _silu_and_grad(x):
    s = _sigmoid(x)
    return x * s, s * (1.0 + x * (1.0 - s))


def _softplus_neg(lam):
    u = jnp.exp(-jnp.abs(lam))
    w = 1.0 + u
    log1p = jnp.where(w == 1.0, u, jnp.log(w) * (u / jnp.where(w == 1.0, 1.0, w - 1.0)))
    return jnp.maximum(-lam, 0.0) + log1p


def _dot(a, b):
    return jnp.dot(a, b, preferred_element_type=F32)


def _dot_nt(a, b):
    return lax.dot_general(a, b, (((1,), (1,)), ((), ())), preferred_element_type=F32)


def _dot_tn(a, b):
    return lax.dot_general(a, b, (((0,), (0,)), ((), ())), preferred_element_type=F32)


def _shift_down(v, halo, k):
    if k == 0:
        return v
    rolled = pltpu.roll(v, k, 0)
    row = lax.broadcasted_iota(jnp.int32, (SUBLANES, v.shape[1]), 0)
    top = jnp.where(row < k, pltpu.roll(halo, k, 0), rolled[:SUBLANES])
    return jnp.concatenate([top, rolled[SUBLANES:]], axis=0)


def _shift_up(v, head, k):
    if k == 0:
        return v
    n = v.shape[0]
    rolled = pltpu.roll(v, n - k, 0)
    row = lax.broadcasted_iota(jnp.int32, (SUBLANES, v.shape[1]), 0)
    bot = jnp.where(row >= SUBLANES - k, pltpu.roll(head, SUBLANES - k, 0), rolled[n - SUBLANES :])
    return jnp.concatenate([rolled[: n - SUBLANES], bot], axis=0)


def _scan_blocks(a_ref, b_ref, out_ref, carry, n_rows, reverse):
    width = a_ref.shape[1]
    row = lax.broadcasted_iota(jnp.int32, (SUBLANES, width), 0)
    n_blocks = n_rows // SUBLANES

    def block(j, carry):
        i = (n_blocks - 1 - j) if reverse else j
        r0 = pl.multiple_of(i * SUBLANES, SUBLANES)
        a = a_ref[pl.ds(r0, SUBLANES), :]
        b = b_ref[pl.ds(r0, SUBLANES), :]
        for d in (1, 2, 4):
            shift = (SUBLANES - d) if reverse else d
            keep = (row < SUBLANES - d) if reverse else (row >= d)
            a_s = pltpu.roll(a, shift, 0)
            b_s = pltpu.roll(b, shift, 0)
            b = jnp.where(keep, a * b_s + b, b)
            a = jnp.where(keep, a * a_s, a)
        h = a * carry + b
        out_ref[pl.ds(r0, SUBLANES), :] = h
        edge = h[0:1, :] if reverse else h[SUBLANES - 1 : SUBLANES, :]
        return jnp.broadcast_to(edge, (SUBLANES, width))

    return lax.fori_loop(0, n_blocks, block, carry)


def _rms_fwd(x, w):
    r = lax.rsqrt(jnp.mean(x * x, axis=-1, keepdims=True) + RMS_EPS)
    xh = x * r
    return xh * w, xh, r


def _rms_bwd(dh, xh, r, w):
    dxh = dh * w
    dx = r * (dxh - xh * jnp.mean(dxh * xh, axis=-1, keepdims=True))
    return dx, jnp.sum(dh * xh, axis=0, keepdims=True)


def _cast_to_segments(w, mine, rows, hook=None):
    n, c = w.shape
    per = n // 2 // rows

    def body(k_ref, w_ref, o_ref):
        o_ref[...] = w_ref[...].astype(BF16)

    out = _pcall(
        body,
        hook,
        name=f"cast_{n}x{c}",
        grid_spec=pltpu.PrefetchScalarGridSpec(
            num_scalar_prefetch=1,
            grid=(n // rows,),
            in_specs=[pl.BlockSpec((rows, c), lambda i, k_ref: (i, 0))],
            out_specs=[pl.BlockSpec((None, rows, c), lambda i, k_ref: (2 * k_ref[0] + i // per, i % per, 0))],
        ),
        out_shape=[jax.ShapeDtypeStruct((N_DEV, n // 2, c), BF16)],
        compiler_params=_cparams(("arbitrary",)),
    )(_scalars(mine), w)
    return out[0] if hook is None else (out[0][0], out[1])


def _place():
    x, y, c = lax.axis_index("x"), lax.axis_index("y"), lax.axis_index("c")
    chips = [(1 - x, y), (x, 1 - y), (1 - x, 1 - y)]
    return x, y, c, chips


def _chip_no(chip):
    return 2 * chip[0] + chip[1]


def _rcopy(src, dst, send_sem, recv_sem, to):
    return pltpu.make_async_remote_copy(
        src_ref=src, dst_ref=dst, send_sem=send_sem, recv_sem=recv_sem, device_id=to, device_id_type=MESH
    )


def _gather_hook(big, small=None):
    nb = len(big)
    n_sems = 6 * nb + 4

    def places():
        x, y, c, chips = _place()
        first = (x ^ (1 - c), y ^ c)
        second = (x ^ c, y ^ (1 - c))
        return x, y, c, chips, first, second, (1 - x, 1 - y)

    def seg(outs, b, chip, half):
        return outs[b].at[2 * _chip_no(chip) + half]

    def step1(outs, send, recv):
        x, y, c, _, first, _, _ = places()
        return [
            _rcopy(seg(outs, b, (x, y), c), seg(outs, b, (x, y), c), send.at[6 * b], recv.at[6 * b], (*first, c))
            for b in range(nb)
        ]

    def step2(outs, send, recv):
        x, y, c, _, first, second, _ = places()
        copies = []
        for b in range(nb):
            for k, chip in ((1, (x, y)), (2, first)):
                src = seg(outs, b, chip, c)
                copies.append(_rcopy(src, src, send.at[6 * b + k], recv.at[6 * b + k], (*second, c)))
        return copies

    def hand_over(outs, send, recv, k, chip):
        x, y, c, *_ = places()
        return [
            _rcopy(seg(outs, b, chip, c), seg(outs, b, chip, c), send.at[6 * b + k], recv.at[6 * b + k], (x, y, 1 - c))
            for b in range(nb)
        ]

    def wait_landed(outs, send, recv, k, chip, half):
        x, y, c, *_ = places()
        for b in range(nb):
            got = seg(outs, b, chip, half)
            _rcopy(got, got, send.at[6 * b + k], recv.at[6 * b + k], (x, y, c)).wait_recv()

    def small_copies(ins, outs, send, recv):
        x, y, c, chips, *_ = places()
        there = outs[nb].at[_chip_no((x, y))]
        return [
            _rcopy(ins[nb], there, send.at[6 * nb + j], recv.at[6 * nb + j], (*chip, c)) for j, chip in enumerate(chips)
        ]

    def local_copy(ins, outs, send):
        x, y, _, _ = _place()
        return pltpu.make_async_copy(ins[nb], outs[nb].at[_chip_no((x, y))], send.at[6 * nb + 3])

    def start(ins, outs, send, recv):
        for cp in step1(outs, send, recv):
            cp.start()
        if small is not None:
            for cp in small_copies(ins, outs, send, recv):
                cp.start()
            local_copy(ins, outs, send).start()

    def middle(ins, outs, send, recv):
        *_, first, _, _ = places()
        wait_landed(outs, send, recv, 0, first, places()[2])
        for cp in step2(outs, send, recv) + hand_over(outs, send, recv, 3, first):
            cp.start()

    def late(ins, outs, send, recv):
        x, y, c, chips, first, second, diagonal = places()
        for k, chip in ((1, second), (2, diagonal)):
            wait_landed(outs, send, recv, k, chip, c)
            for cp in hand_over(outs, send, recv, 3 + k, chip):
                cp.start()

    def finish(ins, outs, send, recv):
        x, y, c, chips, first, second, diagonal = places()
        wait_landed(outs, send, recv, 3, second, 1 - c)
        wait_landed(outs, send, recv, 4, first, 1 - c)
        wait_landed(outs, send, recv, 5, diagonal, 1 - c)
        sent = step1(outs, send, recv) + step2(outs, send, recv)
        for k, chip in ((3, first), (4, second), (5, diagonal)):
            sent += hand_over(outs, send, recv, k, chip)
        for cp in sent:
            cp.wait_send()
        if small is not None:
            for j, chip in enumerate(chips):
                got = outs[nb].at[_chip_no(chip)]
                _rcopy(got, got, send.at[6 * nb + j], recv.at[6 * nb + j], (x, y, c)).wait_recv()
            for cp in small_copies(ins, outs, send, recv):
                cp.wait_send()
            local_copy(ins, outs, send).wait()

    operands = list(big) + ([small] if small is not None else [])
    out_shapes = [jax.ShapeDtypeStruct(b.shape, b.dtype) for b in big]
    if small is not None:
        out_shapes.append(jax.ShapeDtypeStruct((N_CHIPS, *small.shape), small.dtype))
    return _Hook(operands, out_shapes, {b: b for b in range(nb)}, n_sems, start, finish, middle, late)


def _both_ways_hook(operands, out_shapes, copies_of, n_sems):
    def start(ins, outs, send, recv):
        for cp in copies_of(ins, outs, send, recv):
            cp.start()

    def finish(ins, outs, send, recv):
        for cp in copies_of(ins, outs, send, recv):
            cp.wait()

    return _Hook(operands, out_shapes, {}, n_sems, start, finish)


def _swap_hook(bufs):
    def copies_of(ins, outs, send, recv):
        x, y, c, _ = _place()
        copies = []
        for b in range(len(bufs)):
            for j in range(N_CHIPS):
                k = b * N_CHIPS + j
                copies.append(_rcopy(ins[b].at[2 * j + 1 - c], outs[b].at[j], send.at[k], recv.at[k], (x, y, 1 - c)))
        return copies

    out_shapes = [jax.ShapeDtypeStruct((N_CHIPS, *b.shape[1:]), b.dtype) for b in bufs]
    return _both_ways_hook(list(bufs), out_shapes, copies_of, len(bufs) * N_CHIPS)


def _axis_order():
    x, y, c, _ = _place()
    return (x, y), c, (x ^ (1 - c), y ^ c), (x ^ c, y ^ (1 - c)), (1 - x, 1 - y)


def _send_first_hook(parts):
    def copies_of(ins, outs, send, recv):
        _, c, first, _, _ = _axis_order()
        copies = []
        for b in range(len(parts)):
            for k in range(2):
                sem = 2 * b + k
                copies.append(_rcopy(ins[b].at[k], outs[b].at[k], send.at[sem], recv.at[sem], (*first, c)))
        return copies

    out_shapes = [jax.ShapeDtypeStruct((2, *p.shape[1:]), p.dtype) for p in parts]
    return _both_ways_hook(list(parts), out_shapes, copies_of, len(parts) * 2)


def _send_second_hook(mids):
    def copies_of(ins, outs, send, recv):
        _, c, _, second, _ = _axis_order()
        return [_rcopy(ins[b], outs[b], send.at[b], recv.at[b], (*second, c)) for b in range(len(mids))]

    out_shapes = [jax.ShapeDtypeStruct(m.shape, m.dtype) for m in mids]
    return _both_ways_hook(list(mids), out_shapes, copies_of, len(mids))


def _share_hook(big, small=None, tiny=None):
    nb = len(big)
    n_sems = nb + 7 + N_DEV
    t0 = nb + 7

    def tiny_copies(ins, outs, send, recv):
        x, y, c, _ = _place()
        there = outs[-1].at[2 * _chip_no((x, y)) + c]
        copies = []
        for r in range(1, N_DEV):
            to = (x ^ (r >> 2 & 1), y ^ (r >> 1 & 1), c ^ (r & 1))
            copies.append(_rcopy(ins[-1], there, send.at[t0 + r], recv.at[t0 + r], to))
        return copies

    def tiny_local(ins, outs, send):
        x, y, c, _ = _place()
        return pltpu.make_async_copy(ins[-1], outs[-1].at[2 * _chip_no((x, y)) + c], send.at[t0])

    def first_copies(outs, send, recv):
        x, y, c, chips = _place()
        sibling = (x, y, 1 - c)
        copies = [_rcopy(outs[b].at[c], outs[b].at[c], send.at[b], recv.at[b], sibling) for b in range(nb)]
        if small is not None:
            own = outs[nb].at[2 * _chip_no((x, y)) + c]
            copies.append(_rcopy(own, own, send.at[nb], recv.at[nb], sibling))
            for j, chip in enumerate(chips):
                copies.append(_rcopy(own, own, send.at[nb + 1 + j], recv.at[nb + 1 + j], (*chip, c)))
        return copies

    def start(ins, outs, send, recv):
        for cp in first_copies(outs, send, recv):
            cp.start()
        if tiny is not None:
            for cp in tiny_copies(ins, outs, send, recv):
                cp.start()
            tiny_local(ins, outs, send).start()

    def finish(ins, outs, send, recv):
        x, y, c, chips = _place()
        me, sibling = (x, y, c), (x, y, 1 - c)
        if tiny is not None:
            for cp in tiny_copies(ins, outs, send, recv):
                cp.wait()
            tiny_local(ins, outs, send).wait()
        passed = []
        if small is not None:
            for j, chip in enumerate(chips):
                got = outs[nb].at[2 * _chip_no(chip) + c]
                _rcopy(got, got, send.at[nb + 1 + j], recv.at[nb + 1 + j], me).wait_recv()
                fwd = _rcopy(got, got, send.at[nb + 4 + j], recv.at[nb + 4 + j], sibling)
                fwd.start()
                passed.append(fwd)
        for b in range(nb):
            got = outs[b].at[1 - c]
            _rcopy(got, got, send.at[b], recv.at[b], me).wait_recv()
        if small is not None:
            got = outs[nb].at[2 * _chip_no((x, y)) + 1 - c]
            _rcopy(got, got, send.at[nb], recv.at[nb], me).wait_recv()
            for j, chip in enumerate(chips):
                got = outs[nb].at[2 * _chip_no(chip) + 1 - c]
                _rcopy(got, got, send.at[nb + 4 + j], recv.at[nb + 4 + j], me).wait_recv()
        for cp in first_copies(outs, send, recv) + passed:
            cp.wait_send()

    operands = list(big) + ([small] if small is not None else [])
    out_shapes = [jax.ShapeDtypeStruct(a.shape, a.dtype) for a in operands]
    aliases = {i: i for i in range(len(operands))}
    if tiny is not None:
        operands.append(tiny)
        out_shapes.append(jax.ShapeDtypeStruct((N_DEV, *tiny.shape), tiny.dtype))
    return _Hook(operands, out_shapes, aliases, n_sems, start, finish)


def _row_tile(rows, cols, target_bytes=2 * 1024 * 1024):
    best = SUBLANES
    for t in range(SUBLANES, rows + 1, SUBLANES):
        if rows % t == 0 and t * cols * 4 <= target_bytes:
            best = t
    return best


def _add_own_half(buf, got, owners, c, wire):
    _, rows, cols = buf.shape
    tr = _row_tile(rows, cols)

    def body(s_ref, a_ref, b_ref, o_ref):
        o_ref[...] = (a_ref[...] + b_ref[...].astype(F32)).astype(wire)

    return _pcall(
        body,
        name=f"add_own_half_{rows}x{cols}",
        grid_spec=pltpu.PrefetchScalarGridSpec(
            num_scalar_prefetch=1,
            grid=(2, rows // tr),
            in_specs=[
                pl.BlockSpec((None, None, tr, cols), lambda j, r, s_ref: (s_ref[j], s_ref[2], r, 0)),
                pl.BlockSpec((None, tr, cols), lambda j, r, s_ref: (s_ref[j], r, 0)),
            ],
            out_specs=pl.BlockSpec((None, tr, cols), lambda j, r, s_ref: (j, r, 0)),
        ),
        out_shape=jax.ShapeDtypeStruct((2, rows, cols), wire),
        compiler_params=_cparams(("arbitrary", "arbitrary")),
    )(_scalars(owners[0], owners[1], c), buf.reshape(N_CHIPS, 2, rows, cols), got)


def _add_for_neighbour(buf, got_a, got1, second, c, wire):
    _, rows, cols = buf.shape
    tr = _row_tile(rows, cols)

    def body(s_ref, x_ref, a_ref, g_ref, o_ref):
        o_ref[...] = ((x_ref[...] + a_ref[...].astype(F32)) + g_ref[...].astype(F32)).astype(wire)

    return _pcall(
        body,
        name=f"add_for_neighbour_{rows}x{cols}",
        grid_spec=pltpu.PrefetchScalarGridSpec(
            num_scalar_prefetch=1,
            grid=(rows // tr,),
            in_specs=[
                pl.BlockSpec((None, None, tr, cols), lambda r, s_ref: (s_ref[0], s_ref[1], r, 0)),
                pl.BlockSpec((None, tr, cols), lambda r, s_ref: (s_ref[0], r, 0)),
                pl.BlockSpec((None, tr, cols), lambda r, s_ref: (1, r, 0)),
            ],
            out_specs=pl.BlockSpec((tr, cols), lambda r, s_ref: (r, 0)),
        ),
        out_shape=jax.ShapeDtypeStruct((rows, cols), wire),
        compiler_params=_cparams(("arbitrary",)),
    )(_scalars(second, c), buf.reshape(N_CHIPS, 2, rows, cols), got_a, got1)


def _add_received(buf, got_a, got1, got2, mine, c, slot, n_slots):
    _, rows, cols = buf.shape
    tr = _row_tile(rows, cols)

    def body(s_ref, x_ref, a_ref, g1_ref, g2_ref, o_ref):
        own = x_ref[...] + a_ref[...].astype(F32)
        o_ref[...] = (own + g1_ref[...].astype(F32)) + g2_ref[...].astype(F32)

    return _pcall(
        body,
        name=f"add_received_{rows}x{cols}",
        grid_spec=pltpu.PrefetchScalarGridSpec(
            num_scalar_prefetch=1,
            grid=(rows // tr,),
            in_specs=[
                pl.BlockSpec((None, None, tr, cols), lambda r, s_ref: (s_ref[0], s_ref[1], r, 0)),
                pl.BlockSpec((None, tr, cols), lambda r, s_ref: (s_ref[0], r, 0)),
                pl.BlockSpec((None, tr, cols), lambda r, s_ref: (0, r, 0)),
                pl.BlockSpec((tr, cols), lambda r, s_ref: (r, 0)),
            ],
            out_specs=pl.BlockSpec((None, tr, cols), lambda r, s_ref: (s_ref[2], r, 0)),
        ),
        out_shape=jax.ShapeDtypeStruct((n_slots, rows, cols), F32),
        compiler_params=_cparams(("arbitrary",)),
    )(_scalars(mine, c, slot), buf.reshape(N_CHIPS, 2, rows, cols), got_a, got1, got2)


def _layer_a_fwd(x, nw, win, ln_w, ln_b, wc, bs_t, wout, tm, hook):
    t_rows, d = x.shape
    n_sh, _, s_cols = win.shape
    aw = wout.shape[0]
    gd = aw // A_GROUPS
    tn = 512
    assert s_cols % tn == 0 and aw % tn == 0 and tm % CHUNK == 0

    def body(x_ref, nw_ref, win_ref, lnw_ref, lnb_ref, wc_ref, bst_ref, wout_ref, z_ref, x1_ref, h_ref, u_s, v_s, y_s):
        x = x_ref[...]
        h, _, _ = _rms_fwd(x, nw_ref[...])
        h = h.astype(BF16)
        h_ref[...] = h
        for j in range(3 * aw // tn):
            k, off = divmod(j * tn, s_cols)
            cols = slice((j * tn) % aw, (j * tn) % aw + tn)
            zj = _dot(h, win_ref[k, :, off : off + tn])
            z_ref[:, j * tn : (j + 1) * tn] = zj
            if j * tn < aw:
                u_s[:, cols] = _gelu(zj)
            elif j * tn < 2 * aw:
                v_s[:, cols] = _gelu(zj)
            else:
                u_s[:, cols] = u_s[:, cols] * (zj * _sigmoid(zj))
        v = v_s[...]
        mu = jnp.mean(v, axis=-1, keepdims=True)
        vc = v - mu
        rstd = lax.rsqrt(jnp.mean(vc * vc, axis=-1, keepdims=True) + LN_EPS)
        v_s[...] = (vc * rstd) * lnw_ref[...] + lnb_ref[...]
        for ck in range(tm // CHUNK):
            rows = slice(ck * CHUNK, (ck + 1) * CHUNK)
            for g in range(A_GROUPS):
                cols = slice(g * gd, (g + 1) * gd)
                s = _dot(wc_ref[g], v_s[rows, cols].astype(BF16)) + bst_ref[:, g : g + 1]
                y_s[rows, cols] = (u_s[rows, cols] * s).astype(BF16)
        x1_ref[...] = x + _dot(y_s[...], wout_ref[...])

    row = lambda i: (i, 0)
    return _pcall(
        body,
        hook,
        name="layer_a_fwd",
        grid=(t_rows // tm,),
        in_specs=[
            pl.BlockSpec((tm, d), row),
            _full(nw.shape),
            _full(win.shape),
            _full(ln_w.shape),
            _full(ln_b.shape),
            _full(wc.shape),
            _full(bs_t.shape),
            _full(wout.shape),
        ],
        out_specs=[pl.BlockSpec((tm, 3 * aw), row), pl.BlockSpec((tm, d), row), pl.BlockSpec((tm, d), row)],
        out_shape=[
            jax.ShapeDtypeStruct((t_rows, 3 * aw), F32),
            jax.ShapeDtypeStruct((t_rows, d), F32),
            jax.ShapeDtypeStruct((t_rows, d), BF16),
        ],
        scratch_shapes=[pltpu.VMEM((tm, aw), F32), pltpu.VMEM((tm, aw), F32), pltpu.VMEM((tm, aw), BF16)],
        compiler_params=_cparams(("arbitrary",)),
    )(x, nw, win, ln_w, ln_b, wc, bs_t, wout)


def _layer_a_bwd(dout, z, ln_w, ln_b, wc, wct, bs_t, wout, tiles, earlier, hook):
    t_rows, d = dout.shape
    aw = wout.shape[0]
    gd = aw // A_GROUPS
    tm = TM_A_BWD
    lo, hi = tiles
    n_earlier = 0 if earlier is None else len(earlier)

    def body(dout_ref, z_ref, lnw_ref, lnb_ref, wc_ref, wct_ref, bst_ref, wout_ref, *rest):
        dz_ref, y_ref, dob_ref, gws_ref, gbs_ref, glnw_ref, glnb_ref, u_s, vh_s, ds_s, dvn_s = rest[n_earlier:]

        @pl.when(pl.program_id(0) == 0)
        def _():
            gws_ref[...] = jnp.zeros_like(gws_ref)
            gbs_ref[...] = jnp.zeros_like(gbs_ref)
            glnw_ref[...] = jnp.zeros_like(glnw_ref)
            glnb_ref[...] = jnp.zeros_like(glnb_ref)

        dob = dout_ref[...].astype(BF16)
        dob_ref[...] = dob
        dy = _dot_nt(dob, wout_ref[...])

        zv = z_ref[:, aw : 2 * aw]
        vg, dvg_dz = _gelu_and_grad(zv)
        mu = jnp.mean(vg, axis=-1, keepdims=True)
        vc = vg - mu
        rstd = lax.rsqrt(jnp.mean(vc * vc, axis=-1, keepdims=True) + LN_EPS)
        vh = vc * rstd
        vh_s[...] = vh
        vn = (vh * lnw_ref[...] + lnb_ref[...]).astype(BF16)

        zu = z_ref[:, 0:aw]
        zg = z_ref[:, 2 * aw : 3 * aw]
        u, du_dz = _gelu_and_grad(zu)
        sg, dsg = _silu_and_grad(zg)
        u_s[...] = u * sg
        tril = lax.broadcasted_iota(jnp.int32, (CHUNK, CHUNK), 0) >= lax.broadcasted_iota(jnp.int32, (CHUNK, CHUNK), 1)
        for ck in range(tm // CHUNK):
            rows = slice(ck * CHUNK, (ck + 1) * CHUNK)
            for g in range(A_GROUPS):
                cols = slice(g * gd, (g + 1) * gd)
                vn_g = vn[rows, cols]
                s = _dot(wc_ref[g], vn_g) + bst_ref[:, g : g + 1]
                usg = u_s[rows, cols]
                dy_g = dy[rows, cols]
                y_ref[rows, cols] = (usg * s).astype(BF16)
                ds = dy_g * usg
                ds_s[rows, cols] = dy_g * s
                gbs_ref[:, g : g + 1] += jnp.sum(ds, axis=-1, keepdims=True)
                dsb = ds.astype(BF16)
                gws_ref[g] += jnp.where(tril, _dot_nt(dsb, vn_g), 0.0)
                dvn_s[rows, cols] = _dot(wct_ref[g], dsb)
        dusg = ds_s[...]
        dz_ref[:, 0:aw] = (dusg * sg * du_dz).astype(BF16)
        dz_ref[:, 2 * aw : 3 * aw] = (dusg * u * dsg).astype(BF16)

        dvn = dvn_s[...]
        vh = vh_s[...]
        glnw_ref[...] += jnp.sum(dvn * vh, axis=0, keepdims=True)
        glnb_ref[...] += jnp.sum(dvn, axis=0, keepdims=True)
        dvh = dvn * lnw_ref[...]
        dvg = rstd * (dvh - jnp.mean(dvh, axis=-1, keepdims=True) - vh * jnp.mean(dvh * vh, axis=-1, keepdims=True))
        dz_ref[:, aw : 2 * aw] = (dvg * dvg_dz).astype(BF16)

    row = lambda i: (i + lo, 0)
    call = _pcall(
        body,
        hook,
        name=f"layer_a_bwd_{lo}",
        grid=(hi - lo,),
        in_specs=[
            pl.BlockSpec((tm, d), row),
            pl.BlockSpec((tm, 3 * aw), row),
            _full(ln_w.shape),
            _full(ln_b.shape),
            _full(wc.shape),
            _full(wct.shape),
            _full(bs_t.shape),
            _full(wout.shape),
        ]
        + [ANY] * n_earlier,
        out_specs=[
            pl.BlockSpec((tm, 3 * aw), row),
            pl.BlockSpec((tm, aw), row),
            pl.BlockSpec((tm, d), row),
            _full((A_GROUPS, CHUNK, CHUNK)),
            _full((CHUNK, A_GROUPS)),
            _full((1, aw)),
            _full((1, aw)),
        ],
        out_shape=[
            jax.ShapeDtypeStruct((t_rows, 3 * aw), BF16),
            jax.ShapeDtypeStruct((t_rows, aw), BF16),
            jax.ShapeDtypeStruct((t_rows, d), BF16),
            jax.ShapeDtypeStruct((A_GROUPS, CHUNK, CHUNK), F32),
            jax.ShapeDtypeStruct((CHUNK, A_GROUPS), F32),
            jax.ShapeDtypeStruct((1, aw), F32),
            jax.ShapeDtypeStruct((1, aw), F32),
        ],
        scratch_shapes=[pltpu.VMEM((tm, aw), F32)] * 4,
        input_output_aliases={8 + i: i for i in range(n_earlier)},
        compiler_params=_cparams(("arbitrary",)),
    )
    return call(dout, z, ln_w, ln_b, wc, wct, bs_t, wout, *(earlier or ()))


def _layer_a_bwd_dx(dout, x, dz, nw, win, tm, tiles, earlier, hook):
    t_rows, d = x.shape
    n_sh, _, s_cols = win.shape
    lo, hi = tiles
    n_earlier = 0 if earlier is None else 1

    def body(dout_ref, x_ref, dz_ref, nw_ref, win_ref, *rest):
        gx_ref, gnw_ref = rest[n_earlier:]

        @pl.when(pl.program_id(0) == 0)
        def _():
            gnw_ref[...] = jnp.zeros_like(gnw_ref)

        dh = jnp.zeros((tm, d), F32)
        for k in range(n_sh):
            dh = dh + _dot_nt(dz_ref[:, k * s_cols : (k + 1) * s_cols], win_ref[k])
        nw = nw_ref[...]
        _, xh, r = _rms_fwd(x_ref[...], nw)
        dx, gnw = _rms_bwd(dh, xh, r, nw)
        gnw_ref[0:1, :] += gnw
        gx_ref[...] = dout_ref[...] + dx

    row = lambda i: (i + lo, 0)
    return _pcall(
        body,
        hook,
        name=f"layer_a_bwd_dx_{lo}",
        grid=(hi - lo,),
        in_specs=[
            pl.BlockSpec((tm, d), row),
            pl.BlockSpec((tm, d), row),
            pl.BlockSpec((tm, n_sh * s_cols), row),
            _full(nw.shape),
            _full(win.shape),
        ]
        + [ANY] * n_earlier,
        out_specs=[pl.BlockSpec((tm, d), row), _full((SUBLANES, d))],
        out_shape=[jax.ShapeDtypeStruct((t_rows, d), F32), jax.ShapeDtypeStruct((SUBLANES, d), F32)],
        input_output_aliases={5: 0} if n_earlier else {},
        compiler_params=_cparams(("arbitrary",)),
    )(dout, x, dz, nw, win, *([earlier] if n_earlier else []))


def _decay(r, sp_h):
    log_a = (-RG_C) * r * sp_h
    a = jnp.exp(log_a)
    mult = jnp.sqrt(jnp.tanh(-log_a) * (a * a + 1.0))
    return a, mult


def _gates(xc_h, gab_ref, gb_ref, sp_h, h, hd):
    pre = _dot(xc_h.astype(BF16), gab_ref[h])
    bw = gb_ref.shape[1] // 2
    r = _sigmoid(pre[:, :hd] + gb_ref[:, h * hd : (h + 1) * hd])
    ig = _sigmoid(pre[:, hd:] + gb_ref[:, bw + h * hd : bw + (h + 1) * hd])
    a, mult = _decay(r, sp_h)
    return r, ig, a, mult


def _conv(xb, halo, cw_ref, cb_ref):
    xc = cb_ref[...] + cw_ref[CONV_WIDTH - 1 : CONV_WIDTH, :] * xb
    for k in range(CONV_WIDTH - 1):
        xc = xc + cw_ref[k : k + 1, :] * _shift_down(xb, halo, CONV_WIDTH - 1 - k)
    return xc


def _layer_b_fwd(x1, nw, bin_w, cw, cb, gab, gb, lam, bout, nf, tgt, tm):
    t_rows, d = x1.shape
    bw = bout.shape[0]
    hd = bw // B_HEADS
    nt = t_rows // tm

    def body(
        x1_ref, nw_ref, bin_ref, cw_ref, cb_ref, gab_ref, gb_ref, lam_ref, bout_ref, nf_ref, tgt_ref,
        z_ref, h_ref, h1_ref, xbt_ref, ht_ref, dx2_ref, loss_ref, gnf_ref,
        tail_s, carry_s, a_s, b_s, hs_s, acc_s,
    ):
        @pl.when(pl.program_id(0) == 0)
        def _():
            tail_s[...] = jnp.zeros_like(tail_s)
            carry_s[...] = jnp.zeros_like(carry_s)
            acc_s[...] = jnp.zeros_like(acc_s)
            gnf_ref[...] = jnp.zeros_like(gnf_ref)

        x1 = x1_ref[...]
        h1, _, _ = _rms_fwd(x1, nw_ref[...])
        h1 = h1.astype(BF16)
        h1_ref[...] = h1
        z = jnp.concatenate([_dot(h1, bin_ref[k]) for k in range(N_CHIPS)], axis=1)
        z_ref[...] = z
        xb = z[:, :bw]
        xc = _conv(xb, tail_s[...], cw_ref, cb_ref)
        tail = xb[tm - SUBLANES :, :]
        tail_s[...] = tail
        xbt_ref[...] = tail
        sp = _softplus_neg(lam_ref[...])
        for h in range(B_HEADS):
            cols = slice(h * hd, (h + 1) * hd)
            xc_h = xc[:, cols]
            _, ig, a, mult = _gates(xc_h, gab_ref, gb_ref, sp[:, cols], h, hd)
            a_s[:, cols] = a
            b_s[:, cols] = mult * (ig * xc_h)
        carry = _scan_blocks(a_s, b_s, hs_s, carry_s[...], tm, reverse=False)
        carry_s[...] = carry
        ht_ref[...] = hs_s[tm - SUBLANES :, :]
        hs = hs_s[...]
        h_ref[...] = hs
        g = z[:, bw:]
        y = (hs * (g * _sigmoid(g))).astype(BF16)
        x2 = x1 + _dot(y, bout_ref[...])

        nf = nf_ref[...]
        o, xh, r = _rms_fwd(x2, nf)
        diff = o - tgt_ref[...]
        acc_s[...] += jnp.sum(diff * diff, axis=0, keepdims=True)
        do = diff * (1.0 / d)
        dx2, gnf = _rms_bwd(do, xh, r, nf)
        gnf_ref[...] += gnf
        dx2_ref[...] = dx2

        @pl.when(pl.program_id(0) == nt - 1)
        def _():
            total = jnp.sum(acc_s[...], axis=-1, keepdims=True) * (0.5 / d)
            loss_ref[...] = jnp.broadcast_to(total, loss_ref.shape)

    row = lambda i: (i, 0)
    return _pcall(
        body,
        name="layer_b_fwd",
        grid=(nt,),
        in_specs=[
            pl.BlockSpec((tm, d), row),
            _full(nw.shape),
            _full(bin_w.shape),
            _full(cw.shape),
            _full(cb.shape),
            _full(gab.shape),
            _full(gb.shape),
            _full(lam.shape),
            _full(bout.shape),
            _full(nf.shape),
            pl.BlockSpec((tm, d), row),
        ],
        out_specs=[
            pl.BlockSpec((tm, 2 * bw), row),
            pl.BlockSpec((tm, bw), row),
            pl.BlockSpec((tm, d), row),
            pl.BlockSpec((None, SUBLANES, bw), lambda i: (i, 0, 0)),
            pl.BlockSpec((None, SUBLANES, bw), lambda i: (i, 0, 0)),
            pl.BlockSpec((tm, d), row),
            _full((1, LANES)),
            _full((1, d)),
        ],
        out_shape=[
            jax.ShapeDtypeStruct((t_rows, 2 * bw), F32),
            jax.ShapeDtypeStruct((t_rows, bw), F32),
            jax.ShapeDtypeStruct((t_rows, d), BF16),
            jax.ShapeDtypeStruct((nt, SUBLANES, bw), F32),
            jax.ShapeDtypeStruct((nt, SUBLANES, bw), F32),
            jax.ShapeDtypeStruct((t_rows, d), F32),
            jax.ShapeDtypeStruct((1, LANES), F32),
            jax.ShapeDtypeStruct((1, d), F32),
        ],
        scratch_shapes=[
            pltpu.VMEM((SUBLANES, bw), F32),
            pltpu.VMEM((SUBLANES, bw), F32),
            pltpu.VMEM((tm, bw), F32),
            pltpu.VMEM((tm, bw), F32),
            pltpu.VMEM((tm, bw), F32),
            pltpu.VMEM((1, d), F32),
        ],
        compiler_params=_cparams(("arbitrary",)),
    )(x1, nw, bin_w, cw, cb, gab, gb, lam, bout, nf, tgt)


def _layer_b_bwd(dout, x1, z, hseq, xb_tails, h_tails, nw, bin_w, cw, cb, gab, gabt, gb, lam, bout, tm):
    t_rows, d = x1.shape
    bw = bout.shape[0]
    hd = bw // B_HEADS
    nt = t_rows // tm

    def body(
        dout_ref, x1_ref, z_ref, h_ref, xbt_ref, ht_ref, nw_ref, bin_ref, cw_ref, cb_ref, gab_ref, gabt_ref,
        gb_ref, lam_ref, bout_ref,
        dx1_ref, dz_ref, y_ref, dob_ref, ggab_ref, ggb_ref, gcw_ref, gcb_ref, glam_ref, gnw_ref,
        gcarry_s, afirst_s, head_s, aup_s, dh_s, gt_s, dxc_s, xc_s, r_s, ig_s,
    ):
        step = pl.program_id(0)
        tile = nt - 1 - step

        @pl.when(step == 0)
        def _():
            for ref in (ggab_ref, ggb_ref, gcw_ref, gcb_ref, glam_ref, gnw_ref, gcarry_s, afirst_s, head_s):
                ref[...] = jnp.zeros_like(ref)

        first_tile = tile == 0
        xb_halo = jnp.where(first_tile, 0.0, xbt_ref[...])
        h_halo = jnp.where(first_tile, 0.0, ht_ref[...])

        dout = dout_ref[...]
        dob = dout.astype(BF16)
        dob_ref[...] = dob
        dy = _dot_nt(dob, bout_ref[...])
        hs = h_ref[...]
        g = z_ref[:, bw:]
        sg, dsg = _silu_and_grad(g)
        y_ref[...] = (hs * sg).astype(BF16)
        dz_ref[:, bw:] = (dy * hs * dsg).astype(BF16)
        dh_s[...] = dy * sg

        xb = z_ref[:, :bw]
        xc = _conv(xb, xb_halo, cw_ref, cb_ref)
        xc_s[...] = xc
        lam = lam_ref[...]
        sp = _softplus_neg(lam)
        for h in range(B_HEADS):
            cols = slice(h * hd, (h + 1) * hd)
            r, ig, a, _ = _gates(xc[:, cols], gab_ref, gb_ref, sp[:, cols], h, hd)
            r_s[:, cols] = r
            ig_s[:, cols] = ig
            aup_s[:, cols] = _shift_up(a, afirst_s[:, cols], 1)
            afirst_s[:, cols] = jnp.broadcast_to(a[0:1, :], (SUBLANES, hd))
        carry = _scan_blocks(aup_s, dh_s, gt_s, gcarry_s[...], tm, reverse=True)
        gcarry_s[...] = carry

        h_prev = _shift_down(hs, h_halo, 1)
        for h in range(B_HEADS):
            cols = slice(h * hd, (h + 1) * hd)
            xc_h = xc_s[:, cols]
            sp_h = sp[:, cols]
            r, ig = r_s[:, cols], ig_s[:, cols]
            a, mult = _decay(r, sp_h)
            gt = gt_s[:, cols]
            da = gt * h_prev[:, cols]
            dmult = gt * (ig * xc_h)
            dig = gt * (mult * xc_h)
            dxc_direct = gt * (mult * ig)
            dla = da * a - dmult * (a * a) / mult
            glam_ref[:, cols] += jnp.sum(dla * r, axis=0, keepdims=True)
            dr = dla * ((-RG_C) * sp_h)
            dpre = jnp.concatenate([dr * r * (1.0 - r), dig * ig * (1.0 - ig)], axis=1)
            ggb_ref[:, cols] += jnp.sum(dpre[:, :hd], axis=0, keepdims=True)
            ggb_ref[:, bw + h * hd : bw + (h + 1) * hd] += jnp.sum(dpre[:, hd:], axis=0, keepdims=True)
            dpb = dpre.astype(BF16)
            ggab_ref[h] += _dot_tn(xc_h.astype(BF16), dpb)
            dxc_s[:, cols] = dxc_direct + _dot(dpb, gabt_ref[h])
        glam_ref[...] = jnp.where(step == nt - 1, glam_ref[...] * (RG_C * _sigmoid(-lam)), glam_ref[...])

        dxc = dxc_s[...]
        gcb_ref[...] += jnp.sum(dxc, axis=0, keepdims=True)
        dxb = cw_ref[CONV_WIDTH - 1 : CONV_WIDTH, :] * dxc
        gcw_ref[CONV_WIDTH - 1 : CONV_WIDTH, :] += jnp.sum(dxc * xb, axis=0, keepdims=True)
        head = head_s[...]
        for k in range(CONV_WIDTH - 1):
            lag = CONV_WIDTH - 1 - k
            dxb = dxb + cw_ref[k : k + 1, :] * _shift_up(dxc, head, lag)
            gcw_ref[k : k + 1, :] += jnp.sum(dxc * _shift_down(xb, xb_halo, lag), axis=0, keepdims=True)
        head_s[...] = dxc[:SUBLANES, :]
        dz_ref[:, :bw] = dxb.astype(BF16)

        s_cols = 2 * bw // N_CHIPS
        dh1 = jnp.zeros((tm, d), F32)
        for k in range(N_CHIPS):
            dh1 = dh1 + _dot_nt(dz_ref[:, k * s_cols : (k + 1) * s_cols], bin_ref[k])
        x1 = x1_ref[...]
        nw = nw_ref[...]
        _, xh, r1 = _rms_fwd(x1, nw)
        dx, gnw = _rms_bwd(dh1, xh, r1, nw)
        gnw_ref[...] += gnw
        dx1_ref[...] = dout + dx

    rev = lambda i: (nt - 1 - i, 0)
    prev = lambda i: (jnp.maximum(nt - 2 - i, 0), 0, 0)
    return _pcall(
        body,
        name="layer_b_bwd",
        grid=(nt,),
        in_specs=[
            pl.BlockSpec((tm, d), rev),
            pl.BlockSpec((tm, d), rev),
            pl.BlockSpec((tm, 2 * bw), rev),
            pl.BlockSpec((tm, bw), rev),
            pl.BlockSpec((None, SUBLANES, bw), prev),
            pl.BlockSpec((None, SUBLANES, bw), prev),
            _full(nw.shape),
            _full(bin_w.shape),
            _full(cw.shape),
            _full(cb.shape),
            _full(gab.shape),
            _full(gabt.shape),
            _full(gb.shape),
            _full(lam.shape),
            _full(bout.shape),
        ],
        out_specs=[
            pl.BlockSpec((tm, d), rev),
            pl.BlockSpec((tm, 2 * bw), rev),
            pl.BlockSpec((tm, bw), rev),
            pl.BlockSpec((tm, d), rev),
            _full((B_HEADS, hd, 2 * hd)),
            _full((1, 2 * bw)),
            _full((SUBLANES, bw)),
            _full((1, bw)),
            _full((1, bw)),
            _full((1, d)),
        ],
        out_shape=[
            jax.ShapeDtypeStruct((t_rows, d), F32),
            jax.ShapeDtypeStruct((t_rows, 2 * bw), BF16),
            jax.ShapeDtypeStruct((t_rows, bw), BF16),
            jax.ShapeDtypeStruct((t_rows, d), BF16),
            jax.ShapeDtypeStruct((B_HEADS, hd, 2 * hd), F32),
            jax.ShapeDtypeStruct((1, 2 * bw), F32),
            jax.ShapeDtypeStruct((SUBLANES, bw), F32),
            jax.ShapeDtypeStruct((1, bw), F32),
            jax.ShapeDtypeStruct((1, bw), F32),
            jax.ShapeDtypeStruct((1, d), F32),
        ],
        scratch_shapes=[pltpu.VMEM((SUBLANES, bw), F32)] * 3 + [pltpu.VMEM((tm, bw), F32)] * 7,
        compiler_params=_cparams(("arbitrary",)),
    )(dout, x1, z, hseq, xb_tails, h_tails, nw, bin_w, cw, cb, gab, gabt, gb, lam, bout)


def _wgrad(a, b, m_blocks, n_blocks, hook=None, wire_copy=False):
    k, m = a.shape
    n = b.shape[1]
    bm, bn = m // m_blocks, n // n_blocks

    def body(a_ref, b_ref, o_ref, *wire_ref):
        prod = _dot_tn(a_ref[...], b_ref[...])
        o_ref[...] = prod
        if wire_copy:
            wire_ref[0][...] = prod.astype(BF16)

    out_spec = pl.BlockSpec((None, None, bm, bn), lambda j, i: (j, i, 0, 0))
    shape = (n_blocks, m_blocks, bm, bn)
    out = _pcall(
        body,
        hook,
        name=f"wgrad_{m}x{n}",
        grid=(n_blocks, m_blocks),
        in_specs=[pl.BlockSpec((k, bm), lambda j, i: (0, i)), pl.BlockSpec((k, bn), lambda j, i: (0, j))],
        out_specs=[out_spec] * (1 + wire_copy),
        out_shape=[jax.ShapeDtypeStruct(shape, F32)] + [jax.ShapeDtypeStruct(shape, BF16)] * wire_copy,
        compiler_params=_cparams(("arbitrary", "arbitrary")),
    )(a, b)
    outs, rode = (out, None) if hook is None else out
    outs = outs if wire_copy else outs[0]
    return outs if hook is None else (outs, rode)


def _adamw_math(w, g, m, v):
    m = ADAM_B1 * m + (1.0 - ADAM_B1) * g
    v = ADAM_B2 * v + (1.0 - ADAM_B2) * (g * g)
    m_hat = m / (1.0 - ADAM_B1**ADAM_STEP)
    v_hat = v / (1.0 - ADAM_B2**ADAM_STEP)
    delta = -ADAM_LR * (m_hat / (jnp.sqrt(v_hat) + ADAM_EPS) + ADAM_WD * w)
    return delta, m, v


def _adamw(w, g, m, v, hook=None):
    rows, cols = w.shape
    tr = _row_tile(rows, cols, 1024 * 1024)

    def body(w_ref, g_ref, m_ref, v_ref, d_ref, mo_ref, vo_ref):
        d_ref[...], mo_ref[...], vo_ref[...] = _adamw_math(w_ref[...], g_ref[...], m_ref[...], v_ref[...])

    spec = pl.BlockSpec((tr, cols), lambda i: (i, 0))
    return _pcall(
        body,
        hook,
        name=f"adamw_{rows}x{cols}",
        grid=(rows // tr,),
        in_specs=[spec] * 4,
        out_specs=[spec] * 3,
        out_shape=[jax.ShapeDtypeStruct((rows, cols), F32)] * 3,
        compiler_params=_cparams(("arbitrary",)),
    )(w, g, m, v)


def _sum_partials(parts):
    def body(p_ref, o_ref):
        total = p_ref[0, 0:1, :]
        for k in range(1, N_DEV):
            total = total + p_ref[k, 0:1, :]
        o_ref[...] = total

    vmem = pl.BlockSpec(memory_space=pltpu.VMEM)
    return _pcall(
        body,
        name="sum_partials",
        in_specs=[vmem],
        out_specs=vmem,
        out_shape=jax.ShapeDtypeStruct((1, parts.shape[2]), F32),
    )(parts)


def _adamw_many(ws, gs, ms, vs):
    n = len(ws)

    def body(*refs):
        w_refs, g_refs, m_refs, v_refs = (refs[i * n : (i + 1) * n] for i in range(4))
        d_refs, mo_refs, vo_refs = (refs[(4 + i) * n : (5 + i) * n] for i in range(3))
        for i in range(n):
            d_refs[i][...], mo_refs[i][...], vo_refs[i][...] = _adamw_math(
                w_refs[i][...], g_refs[i][...], m_refs[i][...], v_refs[i][...]
            )

    vmem = pl.BlockSpec(memory_space=pltpu.VMEM)
    outs = _pcall(
        body,
        name="adamw_small",
        in_specs=[vmem] * (4 * n),
        out_specs=[vmem] * (3 * n),
        out_shape=[jax.ShapeDtypeStruct(w.shape, F32) for w in ws] * 3,
        compiler_params=_cparams(),
    )(*ws, *gs, *ms, *vs)
    return outs[:n], outs[n : 2 * n], outs[2 * n :]


def _pack_rows(parts, lanes=LANES):
    flat = jnp.concatenate([p.reshape(-1) for p in parts])
    per = N_DEV * SUBLANES * lanes
    total = -(-flat.shape[0] // per) * per
    flat = jnp.pad(flat, (0, total - flat.shape[0]))
    return flat.reshape(N_DEV, total // (N_DEV * lanes), lanes)


def _unpack(flat, shapes):
    out, at = [], 0
    for s in shapes:
        n = 1
        for dim in s:
            n *= dim
        out.append(flat[at : at + n].reshape(s))
        at += n
    return out


def kernel(x, norm_w, a_w_in, a_ln_w, a_ln_b, a_w_s, a_b_s, a_w_out, b_w_in, b_conv_w, b_conv_b, b_gate_a_w, b_gate_a_b, b_gate_x_w, b_gate_x_b, b_lambda, b_w_out, norm_f_w, loss_target, m_norm_w, m_a_w_in, m_a_ln_w, m_a_ln_b, m_a_w_s, m_a_b_s, m_a_w_out, m_b_w_in, m_b_conv_w, m_b_conv_b, m_b_gate_a_w, m_b_gate_a_b, m_b_gate_x_w, m_b_gate_x_b, m_b_lambda, m_b_w_out, m_norm_f_w, v_norm_w, v_a_w_in, v_a_ln_w, v_a_ln_b, v_a_w_s, v_a_b_s, v_a_w_out, v_b_w_in, v_b_conv_w, v_b_conv_b, v_b_gate_a_w, v_b_gate_a_b, v_b_gate_x_w, v_b_gate_x_b, v_b_lambda, v_b_w_out, v_norm_f_w):
    t_rows, d = x.shape[1], x.shape[2]
    aw = a_ln_w.shape[1]
    bw = b_gate_a_w.shape[1] * b_gate_a_w.shape[2]
    hd = bw // B_HEADS
    mine = 2 * lax.axis_index("x") + lax.axis_index("y")
    core = lax.axis_index("c")
    weights = dict(norm_w=norm_w, a_w_in=a_w_in, a_ln_w=a_ln_w, a_ln_b=a_ln_b, a_w_s=a_w_s, a_b_s=a_b_s, a_w_out=a_w_out, b_w_in=b_w_in, b_conv_w=b_conv_w, b_conv_b=b_conv_b, b_gate_a_w=b_gate_a_w, b_gate_a_b=b_gate_a_b, b_gate_x_w=b_gate_x_w, b_gate_x_b=b_gate_x_b, b_lambda=b_lambda, b_w_out=b_w_out, norm_f_w=norm_f_w)
    m_in = dict(norm_w=m_norm_w, a_w_in=m_a_w_in, a_ln_w=m_a_ln_w, a_ln_b=m_a_ln_b, a_w_s=m_a_w_s, a_b_s=m_a_b_s, a_w_out=m_a_w_out, b_w_in=m_b_w_in, b_conv_w=m_b_conv_w, b_conv_b=m_b_conv_b, b_gate_a_w=m_b_gate_a_w, b_gate_a_b=m_b_gate_a_b, b_gate_x_w=m_b_gate_x_w, b_gate_x_b=m_b_gate_x_b, b_lambda=m_b_lambda, b_w_out=m_b_w_out, norm_f_w=m_norm_f_w)
    v_in = dict(norm_w=v_norm_w, a_w_in=v_a_w_in, a_ln_w=v_a_ln_w, a_ln_b=v_a_ln_b, a_w_s=v_a_w_s, a_b_s=v_a_b_s, a_w_out=v_a_w_out, b_w_in=v_b_w_in, b_conv_w=v_b_conv_w, b_conv_b=v_b_conv_b, b_gate_a_w=v_b_gate_a_w, b_gate_a_b=v_b_gate_a_b, b_gate_x_w=v_b_gate_x_w, b_gate_x_b=v_b_gate_x_b, b_lambda=v_b_lambda, b_w_out=v_b_w_out, norm_f_w=v_norm_f_w)

    win_l = _cast_to_segments(a_w_in[0], mine, 256)
    wout_l = _cast_to_segments(a_w_out[0], mine, 256)
    small_l = jnp.concatenate([b_conv_w[0], b_conv_b, b_gate_a_b, b_gate_x_b, b_lambda], axis=0)
    bin_l, (win_g, wout_g, small_g) = _cast_to_segments(
        b_w_in[0], mine, 128, _gather_hook([win_l, wout_l], small_l)
    )
    bout_l = _cast_to_segments(b_w_out[0], mine, 192)
    win = win_g.reshape(N_CHIPS, d, -1)
    wout = wout_g.reshape(aw, d)

    tril = jnp.tril(jnp.ones((CHUNK, CHUNK), F32))
    wc = (a_w_s[0] * tril[None]).astype(BF16)
    wct = jnp.swapaxes(wc, 1, 2)
    bs_t = a_b_s[0].T
    gab = jnp.concatenate([b_gate_a_w[0], b_gate_x_w[0]], axis=2).astype(BF16)
    gabt = jnp.swapaxes(gab, 1, 2)
    nw0, nw1, nf = norm_w[0:1], norm_w[1:2], norm_f_w.reshape(1, d)

    x0 = x[0]
    (z_a, x1, h0), (bin_g, bout_g) = _layer_a_fwd(
        x0, nw0, win, a_ln_w, a_ln_b, wc, bs_t, wout, TM_FWD, _gather_hook([bin_l, bout_l])
    )
    bin_w = bin_g.reshape(N_CHIPS, d, -1)
    bout = bout_g.reshape(bw, d)
    small_f = jnp.transpose(small_g, (1, 0, 2)).reshape(SUBLANES, bw)
    cw, cb = small_f[0:CONV_WIDTH], small_f[CONV_WIDTH : CONV_WIDTH + 1]
    gb = jnp.concatenate([small_f[5:6], small_f[6:7]], axis=1)
    lam = small_f[7:8]
    z_b, hseq, h1, xb_tails, h_tails, dx2, loss_l, g_nf = _layer_b_fwd(
        x1, nw1, bin_w, cw, cb, gab, gb, lam, bout, nf, loss_target[0], TM_FWD
    )
    dx1, dz_b, y_b, dob_b, g_gab, g_gb, g_cw, g_cb, g_lam, g_nw1 = _layer_b_bwd(
        dx2, x1, z_b, hseq, xb_tails, h_tails, nw1, bin_w, cw, cb, gab, gabt, gb, lam, bout, TM_FWD
    )
    seg = lambda g: g.reshape(N_DEV, -1, g.shape[3])
    x_at, y_at = lax.axis_index("x"), lax.axis_index("y")
    first_no = 2 * (x_at ^ (1 - core)) + (y_at ^ core)
    second_no = 2 * (x_at ^ core) + (y_at ^ (1 - core))
    bf16s = lambda bufs: [BF16] * len(bufs)
    own_half = lambda bufs, got, wires: [
        _add_own_half(b, g, (first_no, N_CHIPS - 1 - mine), core, w) for b, g, w in zip(bufs, got, wires)
    ]
    for_neighbour = lambda bufs, got_a, got1, wires: [
        _add_for_neighbour(b, ga, g1, second_no, core, w) for b, ga, g1, w in zip(bufs, got_a, got1, wires)
    ]
    received = lambda bufs, got_a, got1, got2: [
        _add_received(b, ga, g1, g2, mine, core, core, 2) for b, ga, g1, g2 in zip(bufs, got_a, got1, got2)
    ]

    g_bout = [seg(_wgrad(y_b, dob_b, 2, 1))]
    g_bin, swap_o = _wgrad(h1, dz_b, 1, N_CHIPS, _swap_hook(g_bout))
    g_bin = [seg(g_bin)]
    part_o = own_half(g_bout, swap_o, bf16s(g_bout))
    a_args = (z_a, a_ln_w, a_ln_b, wc, wct, bs_t, wout)
    half = t_rows // TM_A_BWD // 2
    first, rode = _layer_a_bwd(
        dx1, *a_args, (0, half), None, _join_hooks(_swap_hook(g_bin), _send_first_hook(part_o))
    )
    swap_i, got1_o = rode[:1], rode[1:]
    part_i = own_half(g_bin, swap_i, bf16s(g_bin))
    mid_o = for_neighbour(g_bout, swap_o, got1_o, bf16s(g_bout))
    second, rode = _layer_a_bwd(
        dx1, *a_args, (half, 2 * half), first[:3], _join_hooks(_send_first_hook(part_i), _send_second_hook(mid_o))
    )
    got1_i, got2_o = rode[:1], rode[1:]
    dz_a, y_a, dob_a = second[:3]
    g_ws, g_bst, g_lnw, g_lnb = (p + q for p, q in zip(first[3:], second[3:]))
    mid_i = for_neighbour(g_bin, swap_i, got1_i, bf16s(g_bin))
    red_o = received(g_bout, swap_o, got1_o, got2_o)
    (g_win, g_win_wire), rode = _wgrad(
        h0, dz_a, 1, N_CHIPS, _join_hooks(_send_second_hook(mid_i), _share_hook(red_o)), wire_copy=True
    )
    g_win, g_win_wire = [seg(g_win)], seg(g_win_wire)
    got2_i, gr_bout = rode[:1], rode[1].reshape(b_w_out.shape[1:])
    red_i = received(g_bin, swap_i, got1_i, got2_i)

    small_shapes = [
        (1, d), (1, aw), (1, aw), (A_GROUPS, CHUNK, CHUNK), (A_GROUPS, CHUNK), (B_HEADS, hd, hd), (B_HEADS, hd, hd),
        (d,), (CONV_WIDTH, bw), (1, bw), (1, bw), (1, bw), (1, bw), (1, 1),
    ]
    small = _pack_rows(
        [
            g_nw1, g_lnw, g_lnb, g_ws, g_bst.T, g_gab[:, :, :hd], g_gab[:, :, hd:],
            g_nf, g_cw[:CONV_WIDTH], g_cb, g_gb[:, :bw], g_gb[:, bw:], g_lam, loss_l[:, :1],
        ]
    )
    g_w, wire_w = g_win + [small], [BF16, F32]
    g_wout, rode = _wgrad(
        y_a, dob_a, N_CHIPS, 1, _join_hooks(_swap_hook([g_win_wire, small]), _share_hook(red_i))
    )
    g_wout = seg(g_wout)
    swap_w, gr_bin = rode[:2], rode[2].reshape(b_w_in.shape[1:])

    g_u, wire_u = [g_wout], [BF16]
    part_w = own_half(g_w, swap_w, wire_w)
    (grad_x, g_nw0_mine), rode = _layer_a_bwd_dx(
        dx1, x0, dz_a, nw0, win, TM_A_DX, (0, t_rows // TM_A_DX), None,
        _join_hooks(_send_first_hook(part_w), _swap_hook(g_u)),
    )
    got1_w, swap_u = rode[:2], rode[2:]
    part_u = own_half(g_u, swap_u, wire_u)
    mid_w = for_neighbour(g_w, swap_w, got1_w, wire_w)
    b_names = ("b_w_in", "b_w_out")
    two_d = lambda a: a.reshape(a.shape[-2:])
    bin_out, rode = _adamw(
        two_d(b_w_in), gr_bin, two_d(m_b_w_in), two_d(v_b_w_in),
        _join_hooks(_send_second_hook(mid_w), _send_first_hook(part_u)),
    )
    got2_w, got1_u = rode[:2], rode[2:]
    mid_u = for_neighbour(g_u, swap_u, got1_u, wire_u)
    red_w = received(g_win, swap_w[:1], got1_w[:1], got2_w[:1])
    red_small = _add_received(small, swap_w[1], got1_w[1], got2_w[1], mine, core, 2 * mine + core, N_DEV)
    bout_out, rode = _adamw(
        two_d(b_w_out), gr_bout, two_d(m_b_w_out), two_d(v_b_w_out),
        _join_hooks(_send_second_hook(mid_u), _share_hook(red_w, red_small, g_nw0_mine)),
    )
    b_out = list(zip(bin_out, bout_out))
    got2_u, gr_win, small_r, g_nw0_all = rode[:1], rode[1].reshape(a_w_in.shape[1:]), rode[2], rode[3]
    red_wout = received(g_u, swap_u, got1_u, got2_u)
    (gr_wout,) = _run_hook(_share_hook(red_wout), "share_reduced")
    win_out = _adamw(two_d(a_w_in), gr_win, two_d(m_a_w_in), two_d(v_a_w_in))
    g_nw0 = _sum_partials(g_nw0_all)
    gr_wout = gr_wout.reshape(a_w_out.shape[1:])
    (g_nw1_r, g_a_ln_w, g_a_ln_b, g_a_w_s, g_a_b_s, g_gate_a_w, g_gate_x_w, g_norm_f, gf_cw, gf_cb, gf_gab, gf_gxb,
     gf_lam, loss) = _unpack(small_r.reshape(-1), small_shapes)
    g_norm_w = jnp.concatenate([g_nw0, g_nw1_r], axis=0)
    shard = lambda g: lax.dynamic_slice_in_dim(g, mine * (bw // N_CHIPS), bw // N_CHIPS, axis=1)

    grads = {
        "norm_w": g_norm_w, "a_w_in": gr_win[None], "a_ln_w": g_a_ln_w, "a_ln_b": g_a_ln_b, "a_w_s": g_a_w_s[None],
        "a_b_s": g_a_b_s[None], "a_w_out": gr_wout[None], "b_w_in": gr_bin[None], "b_conv_w": shard(gf_cw)[None],
        "b_conv_b": shard(gf_cb), "b_gate_a_w": g_gate_a_w[None], "b_gate_a_b": shard(gf_gab),
        "b_gate_x_w": g_gate_x_w[None], "b_gate_x_b": shard(gf_gxb), "b_lambda": shard(gf_lam),
        "b_w_out": gr_bout[None], "norm_f_w": g_norm_f,
    }
    names = list(weights)
    delta, new_m, new_v = {}, {}, {}
    delta["a_w_in"], new_m["a_w_in"], new_v["a_w_in"] = win_out
    delta["a_w_out"], new_m["a_w_out"], new_v["a_w_out"] = _adamw(
        two_d(a_w_out), gr_wout, two_d(m_a_w_out), two_d(v_a_w_out)
    )
    small_names = [n for n in names if n not in ("a_w_in", "a_w_out") + b_names]
    at_least_2d = lambda a: a.reshape(1, -1) if a.ndim == 1 else a
    small_out = _adamw_many(*[[at_least_2d(src[n]) for n in small_names] for src in (weights, grads, m_in, v_in)])
    for dst, vals, b_vals in zip((delta, new_m, new_v), small_out, b_out):
        dst.update(zip(small_names, vals))
        dst.update(zip(b_names, b_vals))
    for dst in (delta, new_m, new_v):
        for n in names:
            dst[n] = dst[n].reshape(weights[n].shape)

    return (
        loss.reshape(()),
        grad_x[None],
        *[grads[n] for n in names],
        *[delta[n] for n in names],
        *[new_m[n] for n in names],
        *[new_v[n] for n in names],
    )
```

```python
import jax
import jax.numpy as jnp
from jax import lax
from jax.experimental import pallas as pl
from jax.experimental.pallas import tpu as pltpu

F32 = jnp.float32
BF16 = jnp.bfloat16

RMS_EPS = 1e-6
LN_EPS = 1e-5
RG_C = 8.0
CHUNK = 128
A_GROUPS = 8
B_HEADS = 12
CONV_WIDTH = 4

ADAM_LR = 0.001
ADAM_B1 = 0.9
ADAM_B2 = 0.999
ADAM_EPS = 1e-08
ADAM_WD = 0.01
ADAM_STEP = 10

N_CHIPS = 4
N_DEV = 8
SUBLANES = 8
LANES = 128
V7X_VMEM_BYTES = 64 * 1024 * 1024
VMEM_LIMIT = V7X_VMEM_BYTES * 7 // 8
MESH = pl.DeviceIdType.MESH
ANY = pl.BlockSpec(memory_space=pl.ANY)

TM_FWD = 256
TM_A_BWD = 256
TM_A_DX = 512

GELU_C0 = 0.7978845608028654
GELU_C1 = 0.044715


class _Hook:
    def __init__(self, operands, out_shapes, aliases, n_sems, start, finish, middle=None, late=None):
        self.operands, self.out_shapes, self.aliases, self.n_sems = operands, out_shapes, aliases, n_sems
        self.start, self.finish, self.middle, self.late = start, finish, middle, late


class _SemView:
    def __init__(self, base, off):
        self.base, self.off = base, off

    @property
    def at(self):
        return self

    def __getitem__(self, k):
        return self.base.at[self.off + k]


def _join_hooks(*hooks):
    if len(hooks) == 1:
        return hooks[0]
    operands, out_shapes, aliases, spans = [], [], {}, []
    n_sems = 0
    for h in hooks:
        aliases.update({len(operands) + i: len(out_shapes) + o for i, o in h.aliases.items()})
        spans.append((len(operands), len(h.operands), len(out_shapes), len(h.out_shapes), n_sems))
        operands += list(h.operands)
        out_shapes += list(h.out_shapes)
        n_sems += h.n_sems

    def each(which):
        def run(ins, outs, send, recv):
            for h, (i0, ni, o0, no, s0) in zip(hooks, spans):
                step = getattr(h, which)
                if step is not None:
                    step(ins[i0 : i0 + ni], outs[o0 : o0 + no], _SemView(send, s0), _SemView(recv, s0))

        return run

    middle = each("middle") if any(h.middle is not None for h in hooks) else None
    late = each("late") if any(h.late is not None for h in hooks) else None
    return _Hook(operands, out_shapes, aliases, n_sems, each("start"), each("finish"), middle, late)


def _pcall(body, hook=None, **kw):
    if hook is None:
        return pl.pallas_call(body, **kw)
    n_pre = 0
    if "grid_spec" in kw:
        spec = kw.pop("grid_spec")
        n_pre = spec.num_scalar_prefetch
        kw.update(
            grid=tuple(spec.grid), in_specs=list(spec.in_specs), out_specs=list(spec.out_specs),
            scratch_shapes=list(spec.scratch_shapes),
        )
    n_in, n_out = len(kw["in_specs"]), len(kw["out_shape"])
    hi, ho = len(hook.operands), len(hook.out_shapes)
    grid = kw.get("grid", ())

    def wrapped(*refs):
        pre, refs = refs[:n_pre], refs[n_pre:]
        ins, h_in = refs[:n_in], refs[n_in : n_in + hi]
        outs = refs[n_in + hi : n_in + hi + n_out]
        h_out = refs[n_in + hi + n_out : n_in + hi + n_out + ho]
        scratch = refs[n_in + hi + n_out + ho : -2]
        send_sems, recv_sems = refs[-2:]
        if not grid:
            hook.start(h_in, h_out, send_sems, recv_sems)
            if hook.middle is not None:
                hook.middle(h_in, h_out, send_sems, recv_sems)
            body(*pre, *ins, *outs, *scratch)
            if hook.late is not None:
                hook.late(h_in, h_out, send_sems, recv_sems)
            hook.finish(h_in, h_out, send_sems, recv_sems)
            return
        first = pl.program_id(0) == 0
        last = pl.program_id(0) == grid[0] - 1
        for axis in range(1, len(grid)):
            first = jnp.logical_and(first, pl.program_id(axis) == 0)
            last = jnp.logical_and(last, pl.program_id(axis) == grid[axis] - 1)

        @pl.when(first)
        def _():
            hook.start(h_in, h_out, send_sems, recv_sems)

        for when, step in ((hook.middle, grid[0] // 4), (hook.late, grid[0] - 1)):
            if when is not None:
                assert len(grid) == 1 and grid[0] >= 4

                @pl.when(pl.program_id(0) == step)
                def _(when=when):
                    when(h_in, h_out, send_sems, recv_sems)

        body(*pre, *ins, *outs, *scratch)

        @pl.when(last)
        def _():
            hook.finish(h_in, h_out, send_sems, recv_sems)

    aliases = dict(kw.pop("input_output_aliases", {}))
    aliases.update({n_pre + n_in + i: n_out + o for i, o in hook.aliases.items()})
    kw.update(
        in_specs=list(kw["in_specs"]) + [ANY] * hi,
        out_specs=list(kw["out_specs"]) + [ANY] * ho,
        out_shape=list(kw["out_shape"]) + list(hook.out_shapes),
        scratch_shapes=list(kw.get("scratch_shapes", ()))
        + [pltpu.SemaphoreType.DMA((hook.n_sems,)), pltpu.SemaphoreType.DMA((hook.n_sems,))],
        input_output_aliases=aliases,
    )
    if n_pre:
        kw["grid_spec"] = pltpu.PrefetchScalarGridSpec(
            num_scalar_prefetch=n_pre, grid=kw.pop("grid"), in_specs=kw.pop("in_specs"),
            out_specs=kw.pop("out_specs"), scratch_shapes=kw.pop("scratch_shapes"),
        )
    call = pl.pallas_call(wrapped, **kw)

    def run(*operands):
        outs = call(*operands, *hook.operands)
        return outs[:n_out], outs[n_out:]

    return run


def _run_hook(hook, name):
    def body():
        pass

    return _pcall(body, hook, name=name, in_specs=[], out_specs=[], out_shape=[])()[1]


def _cparams(sem=None):
    return pltpu.CompilerParams(dimension_semantics=sem, vmem_limit_bytes=VMEM_LIMIT)


def _full(shape):
    zeros = (0,) * len(shape)
    return pl.BlockSpec(shape, lambda *_: zeros)


def _scalars(*vals):
    return jnp.stack([jnp.asarray(v, jnp.int32) for v in vals])


def _sigmoid(x):
    return 1.0 / (1.0 + jnp.exp(-x))


def _gelu(x):
    t = jnp.tanh(GELU_C0 * (x + GELU_C1 * (x * x * x)))
    return x * (0.5 * (1.0 + t))


def _gelu_and_grad(x):
    x2 = x * x
    t = jnp.tanh(GELU_C0 * (x + GELU_C1 * (x2 * x)))
    cdf = 0.5 * (1.0 + t)
    return x * cdf, cdf + 0.5 * x * (1.0 - t * t) * (GELU_C0 * (1.0 + 3.0 * GELU_C1 * x2))


def _silu_and_grad(x):
    s = _sigmoid(x)
    return x * s, s * (1.0 + x * (1.0 - s))


def _softplus_neg(lam):
    u = jnp.exp(-jnp.abs(lam))
    w = 1.0 + u
    log1p = jnp.where(w == 1.0, u, jnp.log(w) * (u / jnp.where(w == 1.0, 1.0, w - 1.0)))
    return jnp.maximum(-lam, 0.0) + log1p


def _dot(a, b):
    return jnp.dot(a, b, preferred_element_type=F32)


def _dot_nt(a, b):
    return lax.dot_general(a, b, (((1,), (1,)), ((), ())), preferred_element_type=F32)


def _dot_tn(a, b):
    return lax.dot_general(a, b, (((0,), (0,)), ((), ())), preferred_element_type=F32)


def _shift_down(v, halo, k):
    if k == 0:
        return v
    rolled = pltpu.roll(v, k, 0)
    row = lax.broadcasted_iota(jnp.int32, (SUBLANES, v.shape[1]), 0)
    top = jnp.where(row < k, pltpu.roll(halo, k, 0), rolled[:SUBLANES])
    return jnp.concatenate([top, rolled[SUBLANES:]], axis=0)


def _shift_up(v, head, k):
    if k == 0:
        return v
    n = v.shape[0]
    rolled = pltpu.roll(v, n - k, 0)
    row = lax.broadcasted_iota(jnp.int32, (SUBLANES, v.shape[1]), 0)
    bot = jnp.where(row >= SUBLANES - k, pltpu.roll(head, SUBLANES - k, 0), rolled[n - SUBLANES :])
    return jnp.concatenate([rolled[: n - SUBLANES], bot], axis=0)


def _scan_blocks(a_ref, b_ref, out_ref, carry, n_rows, reverse):
    width = a_ref.shape[1]
    row = lax.broadcasted_iota(jnp.int32, (SUBLANES, width), 0)
    n_blocks = n_rows // SUBLANES

    def block(j, carry):
        i = (n_blocks - 1 - j) if reverse else j
        r0 = pl.multiple_of(i * SUBLANES, SUBLANES)
        a = a_ref[pl.ds(r0, SUBLANES), :]
        b = b_ref[pl.ds(r0, SUBLANES), :]
        for d in (1, 2, 4):
            shift = (SUBLANES - d) if reverse else d
            keep = (row < SUBLANES - d) if reverse else (row >= d)
            a_s = pltpu.roll(a, shift, 0)
            b_s = pltpu.roll(b, shift, 0)
            b = jnp.where(keep, a * b_s + b, b)
            a = jnp.where(keep, a * a_s, a)
        h = a * carry + b
        out_ref[pl.ds(r0, SUBLANES), :] = h
        edge = h[0:1, :] if reverse else h[SUBLANES - 1 : SUBLANES, :]
        return jnp.broadcast_to(edge, (SUBLANES, width))

    return lax.fori_loop(0, n_blocks, block, carry)


def _rms_fwd(x, w):
    r = lax.rsqrt(jnp.mean(x * x, axis=-1, keepdims=True) + RMS_EPS)
    xh = x * r
    return xh * w, xh, r


def _rms_bwd(dh, xh, r, w):
    dxh = dh * w
    dx = r * (dxh - xh * jnp.mean(dxh * xh, axis=-1, keepdims=True))
    return dx, jnp.sum(dh * xh, axis=0, keepdims=True)


def _cast_to_segments(w, mine, rows, hook=None):
    n, c = w.shape
    per = n // 2 // rows

    def body(k_ref, w_ref, o_ref):
        o_ref[...] = w_ref[...].astype(BF16)

    out = _pcall(
        body,
        hook,
        name=f"cast_{n}x{c}",
        grid_spec=pltpu.PrefetchScalarGridSpec(
            num_scalar_prefetch=1,
            grid=(n // rows,),
            in_specs=[pl.BlockSpec((rows, c), lambda i, k_ref: (i, 0))],
            out_specs=[pl.BlockSpec((None, rows, c), lambda i, k_ref: (2 * k_ref[0] + i // per, i % per, 0))],
        ),
        out_shape=[jax.ShapeDtypeStruct((N_DEV, n // 2, c), BF16)],
        compiler_params=_cparams(("arbitrary",)),
    )(_scalars(mine), w)
    return out[0] if hook is None else (out[0][0], out[1])


def _place():
    x, y, c = lax.axis_index("x"), lax.axis_index("y"), lax.axis_index("c")
    chips = [(1 - x, y), (x, 1 - y), (1 - x, 1 - y)]
    return x, y, c, chips


def _chip_no(chip):
    return 2 * chip[0] + chip[1]


def _rcopy(src, dst, send_sem, recv_sem, to):
    return pltpu.make_async_remote_copy(
        src_ref=src, dst_ref=dst, send_sem=send_sem, recv_sem=recv_sem, device_id=to, device_id_type=MESH
    )


def _gather_hook(big, small=None):
    nb = len(big)
    n_sems = 6 * nb + 4

    def places():
        x, y, c, chips = _place()
        first = (x ^ (1 - c), y ^ c)
        second = (x ^ c, y ^ (1 - c))
        return x, y, c, chips, first, second, (1 - x, 1 - y)

    def seg(outs, b, chip, half):
        return outs[b].at[2 * _chip_no(chip) + half]

    def step1(outs, send, recv):
        x, y, c, _, first, _, _ = places()
        return [
            _rcopy(seg(outs, b, (x, y), c), seg(outs, b, (x, y), c), send.at[6 * b], recv.at[6 * b], (*first, c))
            for b in range(nb)
        ]

    def step2(outs, send, recv):
        x, y, c, _, first, second, _ = places()
        copies = []
        for b in range(nb):
            for k, chip in ((1, (x, y)), (2, first)):
                src = seg(outs, b, chip, c)
                copies.append(_rcopy(src, src, send.at[6 * b + k], recv.at[6 * b + k], (*second, c)))
        return copies

    def hand_over(outs, send, recv, k, chip):
        x, y, c, *_ = places()
        return [
            _rcopy(seg(outs, b, chip, c), seg(outs, b, chip, c), send.at[6 * b + k], recv.at[6 * b + k], (x, y, 1 - c))
            for b in range(nb)
        ]

    def wait_landed(outs, send, recv, k, chip, half):
        x, y, c, *_ = places()
        for b in range(nb):
            got = seg(outs, b, chip, half)
            _rcopy(got, got, send.at[6 * b + k], recv.at[6 * b + k], (x, y, c)).wait_recv()

    def small_copies(ins, outs, send, recv):
        x, y, c, chips, *_ = places()
        there = outs[nb].at[_chip_no((x, y))]
        return [
            _rcopy(ins[nb], there, send.at[6 * nb + j], recv.at[6 * nb + j], (*chip, c)) for j, chip in enumerate(chips)
        ]

    def local_copy(ins, outs, send):
        x, y, _, _ = _place()
        return pltpu.make_async_copy(ins[nb], outs[nb].at[_chip_no((x, y))], send.at[6 * nb + 3])

    def start(ins, outs, send, recv):
        for cp in step1(outs, send, recv):
            cp.start()
        if small is not None:
            for cp in small_copies(ins, outs, send, recv):
                cp.start()
            local_copy(ins, outs, send).start()

    def middle(ins, outs, send, recv):
        *_, first, _, _ = places()
        wait_landed(outs, send, recv, 0, first, places()[2])
        for cp in step2(outs, send, recv) + hand_over(outs, send, recv, 3, first):
            cp.start()

    def late(ins, outs, send, recv):
        x, y, c, chips, first, second, diagonal = places()
        for k, chip in ((1, second), (2, diagonal)):
            wait_landed(outs, send, recv, k, chip, c)
            for cp in hand_over(outs, send, recv, 3 + k, chip):
                cp.start()

    def finish(ins, outs, send, recv):
        x, y, c, chips, first, second, diagonal = places()
        wait_landed(outs, send, recv, 3, second, 1 - c)
        wait_landed(outs, send, recv, 4, first, 1 - c)
        wait_landed(outs, send, recv, 5, diagonal, 1 - c)
        sent = step1(outs, send, recv) + step2(outs, send, recv)
        for k, chip in ((3, first), (4, second), (5, diagonal)):
            sent += hand_over(outs, send, recv, k, chip)
        for cp in sent:
            cp.wait_send()
        if small is not None:
            for j, chip in enumerate(chips):
                got = outs[nb].at[_chip_no(chip)]
                _rcopy(got, got, send.at[6 * nb + j], recv.at[6 * nb + j], (x, y, c)).wait_recv()
            for cp in small_copies(ins, outs, send, recv):
                cp.wait_send()
            local_copy(ins, outs, send).wait()

    operands = list(big) + ([small] if small is not None else [])
    out_shapes = [jax.ShapeDtypeStruct(b.shape, b.dtype) for b in big]
    if small is not None:
        out_shapes.append(jax.ShapeDtypeStruct((N_CHIPS, *small.shape), small.dtype))
    return _Hook(operands, out_shapes, {b: b for b in range(nb)}, n_sems, start, finish, middle, late)


def _both_ways_hook(operands, out_shapes, copies_of, n_sems):
    def start(ins, outs, send, recv):
        for cp in copies_of(ins, outs, send, recv):
            cp.start()

    def finish(ins, outs, send, recv):
        for cp in copies_of(ins, outs, send, recv):
            cp.wait()

    return _Hook(operands, out_shapes, {}, n_sems, start, finish)


def _swap_hook(bufs):
    def copies_of(ins, outs, send, recv):
        x, y, c, _ = _place()
        copies = []
        for b in range(len(bufs)):
            for j in range(N_CHIPS):
                k = b * N_CHIPS + j
                copies.append(_rcopy(ins[b].at[2 * j + 1 - c], outs[b].at[j], send.at[k], recv.at[k], (x, y, 1 - c)))
        return copies

    out_shapes = [jax.ShapeDtypeStruct((N_CHIPS, *b.shape[1:]), b.dtype) for b in bufs]
    return _both_ways_hook(list(bufs), out_shapes, copies_of, len(bufs) * N_CHIPS)


def _axis_order():
    x, y, c, _ = _place()
    return (x, y), c, (x ^ (1 - c), y ^ c), (x ^ c, y ^ (1 - c)), (1 - x, 1 - y)


def _send_first_hook(parts):
    def copies_of(ins, outs, send, recv):
        _, c, first, _, _ = _axis_order()
        copies = []
        for b in range(len(parts)):
            for k in range(2):
                sem = 2 * b + k
                copies.append(_rcopy(ins[b].at[k], outs[b].at[k], send.at[sem], recv.at[sem], (*first, c)))
        return copies

    out_shapes = [jax.ShapeDtypeStruct((2, *p.shape[1:]), p.dtype) for p in parts]
    return _both_ways_hook(list(parts), out_shapes, copies_of, len(parts) * 2)


def _send_second_hook(mids):
    def copies_of(ins, outs, send, recv):
        _, c, _, second, _ = _axis_order()
        return [_rcopy(ins[b], outs[b], send.at[b], recv.at[b], (*second, c)) for b in range(len(mids))]

    out_shapes = [jax.ShapeDtypeStruct(m.shape, m.dtype) for m in mids]
    return _both_ways_hook(list(mids), out_shapes, copies_of, len(mids))


def _share_hook(big, small=None, tiny=None):
    nb = len(big)
    n_sems = nb + 7 + N_DEV
    t0 = nb + 7

    def tiny_copies(ins, outs, send, recv):
        x, y, c, _ = _place()
        there = outs[-1].at[2 * _chip_no((x, y)) + c]
        copies = []
        for r in range(1, N_DEV):
            to = (x ^ (r >> 2 & 1), y ^ (r >> 1 & 1), c ^ (r & 1))
            copies.append(_rcopy(ins[-1], there, send.at[t0 + r], recv.at[t0 + r], to))
        return copies

    def tiny_local(ins, outs, send):
        x, y, c, _ = _place()
        return pltpu.make_async_copy(ins[-1], outs[-1].at[2 * _chip_no((x, y)) + c], send.at[t0])

    def first_copies(outs, send, recv):
        x, y, c, chips = _place()
        sibling = (x, y, 1 - c)
        copies = [_rcopy(outs[b].at[c], outs[b].at[c], send.at[b], recv.at[b], sibling) for b in range(nb)]
        if small is not None:
            own = outs[nb].at[2 * _chip_no((x, y)) + c]
            copies.append(_rcopy(own, own, send.at[nb], recv.at[nb], sibling))
            for j, chip in enumerate(chips):
                copies.append(_rcopy(own, own, send.at[nb + 1 + j], recv.at[nb + 1 + j], (*chip, c)))
        return copies

    def start(ins, outs, send, recv):
        for cp in first_copies(outs, send, recv):
            cp.start()
        if tiny is not None:
            for cp in tiny_copies(ins, outs, send, recv):
                cp.start()
            tiny_local(ins, outs, send).start()

    def finish(ins, outs, send, recv):
        x, y, c, chips = _place()
        me, sibling = (x, y, c), (x, y, 1 - c)
        if tiny is not None:
            for cp in tiny_copies(ins, outs, send, recv):
                cp.wait()
            tiny_local(ins, outs, send).wait()
        passed = []
        if small is not None:
            for j, chip in enumerate(chips):
                got = outs[nb].at[2 * _chip_no(chip) + c]
                _rcopy(got, got, send.at[nb + 1 + j], recv.at[nb + 1 + j], me).wait_recv()
                fwd = _rcopy(got, got, send.at[nb + 4 + j], recv.at[nb + 4 + j], sibling)
                fwd.start()
                passed.append(fwd)
        for b in range(nb):
            got = outs[b].at[1 - c]
            _rcopy(got, got, send.at[b], recv.at[b], me).wait_recv()
        if small is not None:
            got = outs[nb].at[2 * _chip_no((x, y)) + 1 - c]
            _rcopy(got, got, send.at[nb], recv.at[nb], me).wait_recv()
            for j, chip in enumerate(chips):
                got = outs[nb].at[2 * _chip_no(chip) + 1 - c]
                _rcopy(got, got, send.at[nb + 4 + j], recv.at[nb + 4 + j], me).wait_recv()
        for cp in first_copies(outs, send, recv) + passed:
            cp.wait_send()

    operands = list(big) + ([small] if small is not None else [])
    out_shapes = [jax.ShapeDtypeStruct(a.shape, a.dtype) for a in operands]
    aliases = {i: i for i in range(len(operands))}
    if tiny is not None:
        operands.append(tiny)
        out_shapes.append(jax.ShapeDtypeStruct((N_DEV, *tiny.shape), tiny.dtype))
    return _Hook(operands, out_shapes, aliases, n_sems, start, finish)


def _row_tile(rows, cols, target_bytes=2 * 1024 * 1024):
    best = SUBLANES
    for t in range(SUBLANES, rows + 1, SUBLANES):
        if rows % t == 0 and t * cols * 4 <= target_bytes:
            best = t
    return best


def _halves(buf):
    return buf.reshape(N_CHIPS, 2, *buf.shape[1:])


def _add_own_half(sets, owners, c):
    _, rows, cols = sets[0][0].shape
    tr = _row_tile(rows, cols)
    n = len(sets)

    def body(s_ref, *refs):
        def add(k):
            a_ref, b_ref, o_ref = refs[2 * k], refs[2 * k + 1], refs[2 * n + k]
            o_ref[...] = (a_ref[...] + b_ref[...].astype(F32)).astype(sets[k][2])

        add(0)
        for k in range(1, n):
            pl.when(pl.program_id(1) == 0)(lambda k=k: add(k))

    in_specs, out_specs, out_shape, operands = [], [], [], []
    for k, (buf, got, wire) in enumerate(sets):
        r_k, c_k = (tr, cols) if k == 0 else buf.shape[1:]
        row = (lambda r: r) if k == 0 else (lambda r: 0)
        in_specs += [
            pl.BlockSpec((None, None, r_k, c_k), lambda j, r, s_ref, row=row: (s_ref[j], s_ref[2], row(r), 0)),
            pl.BlockSpec((None, r_k, c_k), lambda j, r, s_ref, row=row: (s_ref[j], row(r), 0)),
        ]
        out_specs.append(pl.BlockSpec((None, r_k, c_k), lambda j, r, s_ref, row=row: (j, row(r), 0)))
        out_shape.append(jax.ShapeDtypeStruct((2, *buf.shape[1:]), wire))
        operands += [_halves(buf), got]
    return _pcall(
        body,
        name=f"add_own_half_{rows}x{cols}",
        grid_spec=pltpu.PrefetchScalarGridSpec(
            num_scalar_prefetch=1, grid=(2, rows // tr), in_specs=in_specs, out_specs=out_specs
        ),
        out_shape=out_shape,
        compiler_params=_cparams(("arbitrary", "arbitrary")),
    )(_scalars(owners[0], owners[1], c), *operands)


def _add_for_neighbour(sets, second, c):
    _, rows, cols = sets[0][0].shape
    tr = _row_tile(rows, cols)
    n = len(sets)

    def body(s_ref, *refs):
        def add(k):
            x_ref, a_ref, g_ref, o_ref = refs[3 * k], refs[3 * k + 1], refs[3 * k + 2], refs[3 * n + k]
            o_ref[...] = ((x_ref[...] + a_ref[...].astype(F32)) + g_ref[...].astype(F32)).astype(sets[k][3])

        add(0)
        for k in range(1, n):
            pl.when(pl.program_id(0) == 0)(lambda k=k: add(k))

    in_specs, out_specs, out_shape, operands = [], [], [], []
    for k, (buf, got_a, got1, wire) in enumerate(sets):
        r_k, c_k = (tr, cols) if k == 0 else buf.shape[1:]
        row = (lambda r: r) if k == 0 else (lambda r: 0)
        in_specs += [
            pl.BlockSpec((None, None, r_k, c_k), lambda r, s_ref, row=row: (s_ref[0], s_ref[1], row(r), 0)),
            pl.BlockSpec((None, r_k, c_k), lambda r, s_ref, row=row: (s_ref[0], row(r), 0)),
            pl.BlockSpec((None, r_k, c_k), lambda r, s_ref, row=row: (1, row(r), 0)),
        ]
        out_specs.append(pl.BlockSpec((r_k, c_k), lambda r, s_ref, row=row: (row(r), 0)))
        out_shape.append(jax.ShapeDtypeStruct(buf.shape[1:], wire))
        operands += [_halves(buf), got_a, got1]
    return _pcall(
        body,
        name=f"add_for_neighbour_{rows}x{cols}",
        grid_spec=pltpu.PrefetchScalarGridSpec(
            num_scalar_prefetch=1, grid=(rows // tr,), in_specs=in_specs, out_specs=out_specs
        ),
        out_shape=out_shape,
        compiler_params=_cparams(("arbitrary",)),
    )(_scalars(second, c), *operands)


def _add_received(sets, mine, c):
    _, rows, cols = sets[0][0].shape
    tr = _row_tile(rows, cols)
    n = len(sets)

    def body(s_ref, *refs):
        def add(k):
            x_ref, a_ref, g1_ref, g2_ref, o_ref = *refs[4 * k : 4 * k + 4], refs[4 * n + k]
            own = x_ref[...] + a_ref[...].astype(F32)
            o_ref[...] = (own + g1_ref[...].astype(F32)) + g2_ref[...].astype(F32)

        add(0)
        for k in range(1, n):
            pl.when(pl.program_id(0) == 0)(lambda k=k: add(k))

    in_specs, out_specs, out_shape, operands = [], [], [], []
    for k, (buf, got_a, got1, got2, slot, n_slots) in enumerate(sets):
        r_k, c_k = (tr, cols) if k == 0 else buf.shape[1:]
        row = (lambda r: r) if k == 0 else (lambda r: 0)
        in_specs += [
            pl.BlockSpec((None, None, r_k, c_k), lambda r, s_ref, row=row: (s_ref[0], s_ref[1], row(r), 0)),
            pl.BlockSpec((None, r_k, c_k), lambda r, s_ref, row=row: (s_ref[0], row(r), 0)),
            pl.BlockSpec((None, r_k, c_k), lambda r, s_ref, row=row: (0, row(r), 0)),
            pl.BlockSpec((r_k, c_k), lambda r, s_ref, row=row: (row(r), 0)),
        ]
        out_specs.append(pl.BlockSpec((None, r_k, c_k), lambda r, s_ref, row=row, k=k: (s_ref[2 + k], row(r), 0)))
        out_shape.append(jax.ShapeDtypeStruct((n_slots, *buf.shape[1:]), F32))
        operands += [_halves(buf), got_a, got1, got2]
    return _pcall(
        body,
        name=f"add_received_{rows}x{cols}",
        grid_spec=pltpu.PrefetchScalarGridSpec(
            num_scalar_prefetch=1, grid=(rows // tr,), in_specs=in_specs, out_specs=out_specs
        ),
        out_shape=out_shape,
        compiler_params=_cparams(("arbitrary",)),
    )(_scalars(mine, c, *[s[4] for s in sets]), *operands)


def _layer_a_fwd(x, nw, win, ln_w, ln_b, wc, bs_t, wout, tm, hook):
    t_rows, d = x.shape
    n_sh, _, s_cols = win.shape
    aw = wout.shape[0]
    gd = aw // A_GROUPS
    tn = 512
    assert s_cols % tn == 0 and aw % tn == 0 and tm % CHUNK == 0

    def body(x_ref, nw_ref, win_ref, lnw_ref, lnb_ref, wc_ref, bst_ref, wout_ref, z_ref, x1_ref, h_ref, u_s, v_s, y_s):
        x = x_ref[...]
        h, _, _ = _rms_fwd(x, nw_ref[...])
        h = h.astype(BF16)
        h_ref[...] = h
        for j in range(3 * aw // tn):
            k, off = divmod(j * tn, s_cols)
            cols = slice((j * tn) % aw, (j * tn) % aw + tn)
            zj = _dot(h, win_ref[k, :, off : off + tn])
            z_ref[:, j * tn : (j + 1) * tn] = zj
            if j * tn < aw:
                u_s[:, cols] = _gelu(zj)
            elif j * tn < 2 * aw:
                v_s[:, cols] = _gelu(zj)
            else:
                u_s[:, cols] = u_s[:, cols] * (zj * _sigmoid(zj))
        v = v_s[...]
        mu = jnp.mean(v, axis=-1, keepdims=True)
        vc = v - mu
        rstd = lax.rsqrt(jnp.mean(vc * vc, axis=-1, keepdims=True) + LN_EPS)
        v_s[...] = (vc * rstd) * lnw_ref[...] + lnb_ref[...]
        for ck in range(tm // CHUNK):
            rows = slice(ck * CHUNK, (ck + 1) * CHUNK)
            for g in range(A_GROUPS):
                cols = slice(g * gd, (g + 1) * gd)
                s = _dot(wc_ref[g], v_s[rows, cols].astype(BF16)) + bst_ref[:, g : g + 1]
                y_s[rows, cols] = (u_s[rows, cols] * s).astype(BF16)
        x1_ref[...] = x + _dot(y_s[...], wout_ref[...])

    row = lambda i: (i, 0)
    return _pcall(
        body,
        hook,
        name="layer_a_fwd",
        grid=(t_rows // tm,),
        in_specs=[
            pl.BlockSpec((tm, d), row),
            _full(nw.shape),
            _full(win.shape),
            _full(ln_w.shape),
            _full(ln_b.shape),
            _full(wc.shape),
            _full(bs_t.shape),
            _full(wout.shape),
        ],
        out_specs=[pl.BlockSpec((tm, 3 * aw), row), pl.BlockSpec((tm, d), row), pl.BlockSpec((tm, d), row)],
        out_shape=[
            jax.ShapeDtypeStruct((t_rows, 3 * aw), F32),
            jax.ShapeDtypeStruct((t_rows, d), F32),
            jax.ShapeDtypeStruct((t_rows, d), BF16),
        ],
        scratch_shapes=[pltpu.VMEM((tm, aw), F32), pltpu.VMEM((tm, aw), F32), pltpu.VMEM((tm, aw), BF16)],
        compiler_params=_cparams(("arbitrary",)),
    )(x, nw, win, ln_w, ln_b, wc, bs_t, wout)


def _layer_a_bwd(dout, z, ln_w, ln_b, wc, wct, bs_t, wout, tiles, earlier, hook):
    t_rows, d = dout.shape
    aw = wout.shape[0]
    gd = aw // A_GROUPS
    tm = TM_A_BWD
    lo, hi = tiles
    n_earlier = 0 if earlier is None else len(earlier)

    def body(dout_ref, z_ref, lnw_ref, lnb_ref, wc_ref, wct_ref, bst_ref, wout_ref, *rest):
        dz_ref, y_ref, dob_ref, gws_ref, gbs_ref, glnw_ref, glnb_ref, u_s, vh_s, ds_s, dvn_s = rest[n_earlier:]

        @pl.when(pl.program_id(0) == 0)
        def _():
            gws_ref[...] = jnp.zeros_like(gws_ref)
            gbs_ref[...] = jnp.zeros_like(gbs_ref)
            glnw_ref[...] = jnp.zeros_like(glnw_ref)
            glnb_ref[...] = jnp.zeros_like(glnb_ref)

        dob = dout_ref[...].astype(BF16)
        dob_ref[...] = dob
        dy = _dot_nt(dob, wout_ref[...])

        zv = z_ref[:, aw : 2 * aw]
        vg, dvg_dz = _gelu_and_grad(zv)
        mu = jnp.mean(vg, axis=-1, keepdims=True)
        vc = vg - mu
        rstd = lax.rsqrt(jnp.mean(vc * vc, axis=-1, keepdims=True) + LN_EPS)
        vh = vc * rstd
        vh_s[...] = vh
        vn = (vh * lnw_ref[...] + lnb_ref[...]).astype(BF16)

        zu = z_ref[:, 0:aw]
        zg = z_ref[:, 2 * aw : 3 * aw]
        u, du_dz = _gelu_and_grad(zu)
        sg, dsg = _silu_and_grad(zg)
        u_s[...] = u * sg
        tril = lax.broadcasted_iota(jnp.int32, (CHUNK, CHUNK), 0) >= lax.broadcasted_iota(jnp.int32, (CHUNK, CHUNK), 1)
        for ck in range(tm // CHUNK):
            rows = slice(ck * CHUNK, (ck + 1) * CHUNK)
            for g in range(A_GROUPS):
                cols = slice(g * gd, (g + 1) * gd)
                vn_g = vn[rows, cols]
                s = _dot(wc_ref[g], vn_g) + bst_ref[:, g : g + 1]
                usg = u_s[rows, cols]
                dy_g = dy[rows, cols]
                y_ref[rows, cols] = (usg * s).astype(BF16)
                ds = dy_g * usg
                ds_s[rows, cols] = dy_g * s
                gbs_ref[:, g : g + 1] += jnp.sum(ds, axis=-1, keepdims=True)
                dsb = ds.astype(BF16)
                gws_ref[g] += jnp.where(tril, _dot_nt(dsb, vn_g), 0.0)
                dvn_s[rows, cols] = _dot(wct_ref[g], dsb)
        dusg = ds_s[...]
        dz_ref[:, 0:aw] = (dusg * sg * du_dz).astype(BF16)
        dz_ref[:, 2 * aw : 3 * aw] = (dusg * u * dsg).astype(BF16)

        dvn = dvn_s[...]
        vh = vh_s[...]
        glnw_ref[...] += jnp.sum(dvn * vh, axis=0, keepdims=True)
        glnb_ref[...] += jnp.sum(dvn, axis=0, keepdims=True)
        dvh = dvn * lnw_ref[...]
        dvg = rstd * (dvh - jnp.mean(dvh, axis=-1, keepdims=True) - vh * jnp.mean(dvh * vh, axis=-1, keepdims=True))
        dz_ref[:, aw : 2 * aw] = (dvg * dvg_dz).astype(BF16)

    row = lambda i: (i + lo, 0)
    call = _pcall(
        body,
        hook,
        name=f"layer_a_bwd_{lo}",
        grid=(hi - lo,),
        in_specs=[
            pl.BlockSpec((tm, d), row),
            pl.BlockSpec((tm, 3 * aw), row),
            _full(ln_w.shape),
            _full(ln_b.shape),
            _full(wc.shape),
            _full(wct.shape),
            _full(bs_t.shape),
            _full(wout.shape),
        ]
        + [ANY] * n_earlier,
        out_specs=[
            pl.BlockSpec((tm, 3 * aw), row),
            pl.BlockSpec((tm, aw), row),
            pl.BlockSpec((tm, d), row),
            _full((A_GROUPS, CHUNK, CHUNK)),
            _full((CHUNK, A_GROUPS)),
            _full((1, aw)),
            _full((1, aw)),
        ],
        out_shape=[
            jax.ShapeDtypeStruct((t_rows, 3 * aw), BF16),
            jax.ShapeDtypeStruct((t_rows, aw), BF16),
            jax.ShapeDtypeStruct((t_rows, d), BF16),
            jax.ShapeDtypeStruct((A_GROUPS, CHUNK, CHUNK), F32),
            jax.ShapeDtypeStruct((CHUNK, A_GROUPS), F32),
            jax.ShapeDtypeStruct((1, aw), F32),
            jax.ShapeDtypeStruct((1, aw), F32),
        ],
        scratch_shapes=[pltpu.VMEM((tm, aw), F32)] * 4,
        input_output_aliases={8 + i: i for i in range(n_earlier)},
        compiler_params=_cparams(("arbitrary",)),
    )
    return call(dout, z, ln_w, ln_b, wc, wct, bs_t, wout, *(earlier or ()))


def _layer_a_bwd_dx(dout, x, dz, nw, win, tm, tiles, earlier, hook):
    t_rows, d = x.shape
    n_sh, _, s_cols = win.shape
    lo, hi = tiles
    n_earlier = 0 if earlier is None else 1

    def body(dout_ref, x_ref, dz_ref, nw_ref, win_ref, *rest):
        gx_ref, gnw_ref = rest[n_earlier:]

        @pl.when(pl.program_id(0) == 0)
        def _():
            gnw_ref[...] = jnp.zeros_like(gnw_ref)

        dh = jnp.zeros((tm, d), F32)
        for k in range(n_sh):
            dh = dh + _dot_nt(dz_ref[:, k * s_cols : (k + 1) * s_cols], win_ref[k])
        nw = nw_ref[...]
        _, xh, r = _rms_fwd(x_ref[...], nw)
        dx, gnw = _rms_bwd(dh, xh, r, nw)
        gnw_ref[0:1, :] += gnw
        gx_ref[...] = dout_ref[...] + dx

    row = lambda i: (i + lo, 0)
    return _pcall(
        body,
        hook,
        name=f"layer_a_bwd_dx_{lo}",
        grid=(hi - lo,),
        in_specs=[
            pl.BlockSpec((tm, d), row),
            pl.BlockSpec((tm, d), row),
            pl.BlockSpec((tm, n_sh * s_cols), row),
            _full(nw.shape),
            _full(win.shape),
        ]
        + [ANY] * n_earlier,
        out_specs=[pl.BlockSpec((tm, d), row), _full((SUBLANES, d))],
        out_shape=[jax.ShapeDtypeStruct((t_rows, d), F32), jax.ShapeDtypeStruct((SUBLANES, d), F32)],
        input_output_aliases={5: 0} if n_earlier else {},
        compiler_params=_cparams(("arbitrary",)),
    )(dout, x, dz, nw, win, *([earlier] if n_earlier else []))


def _decay(r, sp_h):
    log_a = (-RG_C) * r * sp_h
    a = jnp.exp(log_a)
    mult = jnp.sqrt(jnp.tanh(-log_a) * (a * a + 1.0))
    return a, mult


def _gates(xc_h, gab_ref, gb_ref, sp_h, h, hd):
    pre = _dot(xc_h.astype(BF16), gab_ref[h])
    bw = gb_ref.shape[1] // 2
    r = _sigmoid(pre[:, :hd] + gb_ref[:, h * hd : (h + 1) * hd])
    ig = _sigmoid(pre[:, hd:] + gb_ref[:, bw + h * hd : bw + (h + 1) * hd])
    a, mult = _decay(r, sp_h)
    return r, ig, a, mult


def _conv(xb, halo, cw_ref, cb_ref):
    xc = cb_ref[...] + cw_ref[CONV_WIDTH - 1 : CONV_WIDTH, :] * xb
    for k in range(CONV_WIDTH - 1):
        xc = xc + cw_ref[k : k + 1, :] * _shift_down(xb, halo, CONV_WIDTH - 1 - k)
    return xc


def _layer_b_fwd(x1, nw, bin_w, cw, cb, gab, gb, lam, bout, nf, tgt, tm):
    t_rows, d = x1.shape
    bw = bout.shape[0]
    hd = bw // B_HEADS
    nt = t_rows // tm

    def body(
        x1_ref, nw_ref, bin_ref, cw_ref, cb_ref, gab_ref, gb_ref, lam_ref, bout_ref, nf_ref, tgt_ref,
        z_ref, h_ref, h1_ref, dx2_ref, loss_ref, gnf_ref,
        tail_s, carry_s, a_s, b_s, hs_s, acc_s,
    ):
        @pl.when(pl.program_id(0) == 0)
        def _():
            tail_s[...] = jnp.zeros_like(tail_s)
            carry_s[...] = jnp.zeros_like(carry_s)
            acc_s[...] = jnp.zeros_like(acc_s)
            gnf_ref[...] = jnp.zeros_like(gnf_ref)

        x1 = x1_ref[...]
        h1, _, _ = _rms_fwd(x1, nw_ref[...])
        h1 = h1.astype(BF16)
        h1_ref[...] = h1
        z = jnp.concatenate([_dot(h1, bin_ref[k]) for k in range(N_CHIPS)], axis=1)
        z_ref[...] = z
        xb = z[:, :bw]
        xc = _conv(xb, tail_s[...], cw_ref, cb_ref)
        tail = xb[tm - SUBLANES :, :]
        tail_s[...] = tail
        sp = _softplus_neg(lam_ref[...])
        for h in range(B_HEADS):
            cols = slice(h * hd, (h + 1) * hd)
            xc_h = xc[:, cols]
            _, ig, a, mult = _gates(xc_h, gab_ref, gb_ref, sp[:, cols], h, hd)
            a_s[:, cols] = a
            b_s[:, cols] = mult * (ig * xc_h)
        carry = _scan_blocks(a_s, b_s, hs_s, carry_s[...], tm, reverse=False)
        carry_s[...] = carry
        hs = hs_s[...]
        h_ref[...] = hs
        g = z[:, bw:]
        y = (hs * (g * _sigmoid(g))).astype(BF16)
        x2 = x1 + _dot(y, bout_ref[...])

        nf = nf_ref[...]
        o, xh, r = _rms_fwd(x2, nf)
        diff = o - tgt_ref[...]
        acc_s[...] += jnp.sum(diff * diff, axis=0, keepdims=True)
        do = diff * (1.0 / d)
        dx2, gnf = _rms_bwd(do, xh, r, nf)
        gnf_ref[...] += gnf
        dx2_ref[...] = dx2

        @pl.when(pl.program_id(0) == nt - 1)
        def _():
            total = jnp.sum(acc_s[...], axis=-1, keepdims=True) * (0.5 / d)
            loss_ref[...] = jnp.broadcast_to(total, loss_ref.shape)

    row = lambda i: (i, 0)
    return _pcall(
        body,
        name="layer_b_fwd",
        grid=(nt,),
        in_specs=[
            pl.BlockSpec((tm, d), row),
            _full(nw.shape),
            _full(bin_w.shape),
            _full(cw.shape),
            _full(cb.shape),
            _full(gab.shape),
            _full(gb.shape),
            _full(lam.shape),
            _full(bout.shape),
            _full(nf.shape),
            pl.BlockSpec((tm, d), row),
        ],
        out_specs=[
            pl.BlockSpec((tm, 2 * bw), row),
            pl.BlockSpec((tm, bw), row),
            pl.BlockSpec((tm, d), row),
            pl.BlockSpec((tm, d), row),
            _full((1, LANES)),
            _full((1, d)),
        ],
        out_shape=[
            jax.ShapeDtypeStruct((t_rows, 2 * bw), F32),
            jax.ShapeDtypeStruct((t_rows, bw), F32),
            jax.ShapeDtypeStruct((t_rows, d), BF16),
            jax.ShapeDtypeStruct((t_rows, d), F32),
            jax.ShapeDtypeStruct((1, LANES), F32),
            jax.ShapeDtypeStruct((1, d), F32),
        ],
        scratch_shapes=[
            pltpu.VMEM((SUBLANES, bw), F32),
            pltpu.VMEM((SUBLANES, bw), F32),
            pltpu.VMEM((tm, bw), F32),
            pltpu.VMEM((tm, bw), F32),
            pltpu.VMEM((tm, bw), F32),
            pltpu.VMEM((1, d), F32),
        ],
        compiler_params=_cparams(("arbitrary",)),
    )(x1, nw, bin_w, cw, cb, gab, gb, lam, bout, nf, tgt)


def _layer_b_bwd(dout, x1, z, hseq, nw, bin_w, cw, cb, gab, gabt, gb, lam, bout, tm):
    t_rows, d = x1.shape
    bw = bout.shape[0]
    hd = bw // B_HEADS
    nt = t_rows // tm

    def body(
        dout_ref, x1_ref, z_ref, h_ref, xbt_ref, ht_ref, nw_ref, bin_ref, cw_ref, cb_ref, gab_ref, gabt_ref,
        gb_ref, lam_ref, bout_ref,
        dx1_ref, dz_ref, y_ref, dob_ref, ggab_ref, ggb_ref, gcw_ref, gcb_ref, glam_ref, gnw_ref,
        gcarry_s, afirst_s, head_s, aup_s, dh_s, gt_s, dxc_s, xc_s, r_s, ig_s,
    ):
        step = pl.program_id(0)
        tile = nt - 1 - step

        @pl.when(step == 0)
        def _():
            for ref in (ggab_ref, ggb_ref, gcw_ref, gcb_ref, glam_ref, gnw_ref, gcarry_s, afirst_s, head_s):
                ref[...] = jnp.zeros_like(ref)

        first_tile = tile == 0
        xb_halo = jnp.where(first_tile, 0.0, xbt_ref[...])
        h_halo = jnp.where(first_tile, 0.0, ht_ref[...])

        dout = dout_ref[...]
        dob = dout.astype(BF16)
        dob_ref[...] = dob
        dy = _dot_nt(dob, bout_ref[...])
        hs = h_ref[...]
        g = z_ref[:, bw:]
        sg, dsg = _silu_and_grad(g)
        y_ref[...] = (hs * sg).astype(BF16)
        dz_ref[:, bw:] = (dy * hs * dsg).astype(BF16)
        dh_s[...] = dy * sg

        xb = z_ref[:, :bw]
        xc = _conv(xb, xb_halo, cw_ref, cb_ref)
        xc_s[...] = xc
        lam = lam_ref[...]
        sp = _softplus_neg(lam)
        for h in range(B_HEADS):
            cols = slice(h * hd, (h + 1) * hd)
            r, ig, a, _ = _gates(xc[:, cols], gab_ref, gb_ref, sp[:, cols], h, hd)
            r_s[:, cols] = r
            ig_s[:, cols] = ig
            aup_s[:, cols] = _shift_up(a, afirst_s[:, cols], 1)
            afirst_s[:, cols] = jnp.broadcast_to(a[0:1, :], (SUBLANES, hd))
        carry = _scan_blocks(aup_s, dh_s, gt_s, gcarry_s[...], tm, reverse=True)
        gcarry_s[...] = carry

        h_prev = _shift_down(hs, h_halo, 1)
        for h in range(B_HEADS):
            cols = slice(h * hd, (h + 1) * hd)
            xc_h = xc_s[:, cols]
            sp_h = sp[:, cols]
            r, ig = r_s[:, cols], ig_s[:, cols]
            a, mult = _decay(r, sp_h)
            gt = gt_s[:, cols]
            da = gt * h_prev[:, cols]
            dmult = gt * (ig * xc_h)
            dig = gt * (mult * xc_h)
            dxc_direct = gt * (mult * ig)
            dla = da * a - dmult * (a * a) / mult
            glam_ref[:, cols] += jnp.sum(dla * r, axis=0, keepdims=True)
            dr = dla * ((-RG_C) * sp_h)
            dpre = jnp.concatenate([dr * r * (1.0 - r), dig * ig * (1.0 - ig)], axis=1)
            ggb_ref[:, cols] += jnp.sum(dpre[:, :hd], axis=0, keepdims=True)
            ggb_ref[:, bw + h * hd : bw + (h + 1) * hd] += jnp.sum(dpre[:, hd:], axis=0, keepdims=True)
            dpb = dpre.astype(BF16)
            ggab_ref[h] += _dot_tn(xc_h.astype(BF16), dpb)
            dxc_s[:, cols] = dxc_direct + _dot(dpb, gabt_ref[h])
        glam_ref[...] = jnp.where(step == nt - 1, glam_ref[...] * (RG_C * _sigmoid(-lam)), glam_ref[...])

        dxc = dxc_s[...]
        gcb_ref[...] += jnp.sum(dxc, axis=0, keepdims=True)
        dxb = cw_ref[CONV_WIDTH - 1 : CONV_WIDTH, :] * dxc
        gcw_ref[CONV_WIDTH - 1 : CONV_WIDTH, :] += jnp.sum(dxc * xb, axis=0, keepdims=True)
        head = head_s[...]
        for k in range(CONV_WIDTH - 1):
            lag = CONV_WIDTH - 1 - k
            dxb = dxb + cw_ref[k : k + 1, :] * _shift_up(dxc, head, lag)
            gcw_ref[k : k + 1, :] += jnp.sum(dxc * _shift_down(xb, xb_halo, lag), axis=0, keepdims=True)
        head_s[...] = dxc[:SUBLANES, :]
        dz_ref[:, :bw] = dxb.astype(BF16)

        s_cols = 2 * bw // N_CHIPS
        dh1 = jnp.zeros((tm, d), F32)
        for k in range(N_CHIPS):
            dh1 = dh1 + _dot_nt(dz_ref[:, k * s_cols : (k + 1) * s_cols], bin_ref[k])
        x1 = x1_ref[...]
        nw = nw_ref[...]
        _, xh, r1 = _rms_fwd(x1, nw)
        dx, gnw = _rms_bwd(dh1, xh, r1, nw)
        gnw_ref[...] += gnw
        dx1_ref[...] = dout + dx

    rev = lambda i: (nt - 1 - i, 0)
    prev = lambda i: (jnp.maximum((nt - 1 - i) * (tm // SUBLANES) - 1, 0), 0)
    return _pcall(
        body,
        name="layer_b_bwd",
        grid=(nt,),
        in_specs=[
            pl.BlockSpec((tm, d), rev),
            pl.BlockSpec((tm, d), rev),
            pl.BlockSpec((tm, 2 * bw), rev),
            pl.BlockSpec((tm, bw), rev),
            pl.BlockSpec((SUBLANES, bw), prev),
            pl.BlockSpec((SUBLANES, bw), prev),
            _full(nw.shape),
            _full(bin_w.shape),
            _full(cw.shape),
            _full(cb.shape),
            _full(gab.shape),
            _full(gabt.shape),
            _full(gb.shape),
            _full(lam.shape),
            _full(bout.shape),
        ],
        out_specs=[
            pl.BlockSpec((tm, d), rev),
            pl.BlockSpec((tm, 2 * bw), rev),
            pl.BlockSpec((tm, bw), rev),
            pl.BlockSpec((tm, d), rev),
            _full((B_HEADS, hd, 2 * hd)),
            _full((1, 2 * bw)),
            _full((SUBLANES, bw)),
            _full((1, bw)),
            _full((1, bw)),
            _full((1, d)),
        ],
        out_shape=[
            jax.ShapeDtypeStruct((t_rows, d), F32),
            jax.ShapeDtypeStruct((t_rows, 2 * bw), BF16),
            jax.ShapeDtypeStruct((t_rows, bw), BF16),
            jax.ShapeDtypeStruct((t_rows, d), BF16),
            jax.ShapeDtypeStruct((B_HEADS, hd, 2 * hd), F32),
            jax.ShapeDtypeStruct((1, 2 * bw), F32),
            jax.ShapeDtypeStruct((SUBLANES, bw), F32),
            jax.ShapeDtypeStruct((1, bw), F32),
            jax.ShapeDtypeStruct((1, bw), F32),
            jax.ShapeDtypeStruct((1, d), F32),
        ],
        scratch_shapes=[pltpu.VMEM((SUBLANES, bw), F32)] * 3 + [pltpu.VMEM((tm, bw), F32)] * 7,
        compiler_params=_cparams(("arbitrary",)),
    )(dout, x1, z, hseq, z, hseq, nw, bin_w, cw, cb, gab, gabt, gb, lam, bout)


def _wgrad(a, b, m_blocks, n_blocks, hook=None, wire_copy=False):
    k, m = a.shape
    n = b.shape[1]
    bm, bn = m // m_blocks, n // n_blocks

    def body(a_ref, b_ref, o_ref, *wire_ref):
        prod = _dot_tn(a_ref[...], b_ref[...])
        o_ref[...] = prod
        if wire_copy:
            wire_ref[0][...] = prod.astype(BF16)

    out_spec = pl.BlockSpec((None, None, bm, bn), lambda j, i: (j, i, 0, 0))
    shape = (n_blocks, m_blocks, bm, bn)
    out = _pcall(
        body,
        hook,
        name=f"wgrad_{m}x{n}",
        grid=(n_blocks, m_blocks),
        in_specs=[pl.BlockSpec((k, bm), lambda j, i: (0, i)), pl.BlockSpec((k, bn), lambda j, i: (0, j))],
        out_specs=[out_spec] * (1 + wire_copy),
        out_shape=[jax.ShapeDtypeStruct(shape, F32)] + [jax.ShapeDtypeStruct(shape, BF16)] * wire_copy,
        compiler_params=_cparams(("arbitrary", "arbitrary")),
    )(a, b)
    outs, rode = (out, None) if hook is None else out
    outs = outs if wire_copy else outs[0]
    return outs if hook is None else (outs, rode)


def _adamw_math(w, g, m, v):
    m = ADAM_B1 * m + (1.0 - ADAM_B1) * g
    v = ADAM_B2 * v + (1.0 - ADAM_B2) * (g * g)
    m_hat = m / (1.0 - ADAM_B1**ADAM_STEP)
    v_hat = v / (1.0 - ADAM_B2**ADAM_STEP)
    delta = -ADAM_LR * (m_hat / (jnp.sqrt(v_hat) + ADAM_EPS) + ADAM_WD * w)
    return delta, m, v


def _adamw(w, g, m, v, hook=None):
    rows, cols = w.shape
    tr = _row_tile(rows, cols, 1024 * 1024)

    def body(w_ref, g_ref, m_ref, v_ref, d_ref, mo_ref, vo_ref):
        d_ref[...], mo_ref[...], vo_ref[...] = _adamw_math(w_ref[...], g_ref[...], m_ref[...], v_ref[...])

    spec = pl.BlockSpec((tr, cols), lambda i: (i, 0))
    return _pcall(
        body,
        hook,
        name=f"adamw_{rows}x{cols}",
        grid=(rows // tr,),
        in_specs=[spec] * 4,
        out_specs=[spec] * 3,
        out_shape=[jax.ShapeDtypeStruct((rows, cols), F32)] * 3,
        compiler_params=_cparams(("arbitrary",)),
    )(w, g, m, v)


def _sum_partials(parts):
    def body(p_ref, o_ref):
        total = p_ref[0, 0:1, :]
        for k in range(1, N_DEV):
            total = total + p_ref[k, 0:1, :]
        o_ref[...] = total

    vmem = pl.BlockSpec(memory_space=pltpu.VMEM)
    return _pcall(
        body,
        name="sum_partials",
        in_specs=[vmem],
        out_specs=vmem,
        out_shape=jax.ShapeDtypeStruct((1, parts.shape[2]), F32),
    )(parts)


def _adamw_many(ws, gs, ms, vs):
    n = len(ws)

    def body(*refs):
        w_refs, g_refs, m_refs, v_refs = (refs[i * n : (i + 1) * n] for i in range(4))
        d_refs, mo_refs, vo_refs = (refs[(4 + i) * n : (5 + i) * n] for i in range(3))
        for i in range(n):
            d_refs[i][...], mo_refs[i][...], vo_refs[i][...] = _adamw_math(
                w_refs[i][...], g_refs[i][...], m_refs[i][...], v_refs[i][...]
            )

    vmem = pl.BlockSpec(memory_space=pltpu.VMEM)
    outs = _pcall(
        body,
        name="adamw_small",
        in_specs=[vmem] * (4 * n),
        out_specs=[vmem] * (3 * n),
        out_shape=[jax.ShapeDtypeStruct(w.shape, F32) for w in ws] * 3,
        compiler_params=_cparams(),
    )(*ws, *gs, *ms, *vs)
    return outs[:n], outs[n : 2 * n], outs[2 * n :]


def _pack_rows(parts, lanes=LANES):
    flat = jnp.concatenate([p.reshape(-1) for p in parts])
    per = N_DEV * SUBLANES * lanes
    total = -(-flat.shape[0] // per) * per
    flat = jnp.pad(flat, (0, total - flat.shape[0]))
    return flat.reshape(N_DEV, total // (N_DEV * lanes), lanes)


def _unpack(flat, shapes):
    out, at = [], 0
    for s in shapes:
        n = 1
        for dim in s:
            n *= dim
        out.append(flat[at : at + n].reshape(s))
        at += n
    return out


def kernel(x, norm_w, a_w_in, a_ln_w, a_ln_b, a_w_s, a_b_s, a_w_out, b_w_in, b_conv_w, b_conv_b, b_gate_a_w, b_gate_a_b, b_gate_x_w, b_gate_x_b, b_lambda, b_w_out, norm_f_w, loss_target, m_norm_w, m_a_w_in, m_a_ln_w, m_a_ln_b, m_a_w_s, m_a_b_s, m_a_w_out, m_b_w_in, m_b_conv_w, m_b_conv_b, m_b_gate_a_w, m_b_gate_a_b, m_b_gate_x_w, m_b_gate_x_b, m_b_lambda, m_b_w_out, m_norm_f_w, v_norm_w, v_a_w_in, v_a_ln_w, v_a_ln_b, v_a_w_s, v_a_b_s, v_a_w_out, v_b_w_in, v_b_conv_w, v_b_conv_b, v_b_gate_a_w, v_b_gate_a_b, v_b_gate_x_w, v_b_gate_x_b, v_b_lambda, v_b_w_out, v_norm_f_w):
    t_rows, d = x.shape[1], x.shape[2]
    aw = a_ln_w.shape[1]
    bw = b_gate_a_w.shape[1] * b_gate_a_w.shape[2]
    hd = bw // B_HEADS
    mine = 2 * lax.axis_index("x") + lax.axis_index("y")
    core = lax.axis_index("c")
    weights = dict(norm_w=norm_w, a_w_in=a_w_in, a_ln_w=a_ln_w, a_ln_b=a_ln_b, a_w_s=a_w_s, a_b_s=a_b_s, a_w_out=a_w_out, b_w_in=b_w_in, b_conv_w=b_conv_w, b_conv_b=b_conv_b, b_gate_a_w=b_gate_a_w, b_gate_a_b=b_gate_a_b, b_gate_x_w=b_gate_x_w, b_gate_x_b=b_gate_x_b, b_lambda=b_lambda, b_w_out=b_w_out, norm_f_w=norm_f_w)
    m_in = dict(norm_w=m_norm_w, a_w_in=m_a_w_in, a_ln_w=m_a_ln_w, a_ln_b=m_a_ln_b, a_w_s=m_a_w_s, a_b_s=m_a_b_s, a_w_out=m_a_w_out, b_w_in=m_b_w_in, b_conv_w=m_b_conv_w, b_conv_b=m_b_conv_b, b_gate_a_w=m_b_gate_a_w, b_gate_a_b=m_b_gate_a_b, b_gate_x_w=m_b_gate_x_w, b_gate_x_b=m_b_gate_x_b, b_lambda=m_b_lambda, b_w_out=m_b_w_out, norm_f_w=m_norm_f_w)
    v_in = dict(norm_w=v_norm_w, a_w_in=v_a_w_in, a_ln_w=v_a_ln_w, a_ln_b=v_a_ln_b, a_w_s=v_a_w_s, a_b_s=v_a_b_s, a_w_out=v_a_w_out, b_w_in=v_b_w_in, b_conv_w=v_b_conv_w, b_conv_b=v_b_conv_b, b_gate_a_w=v_b_gate_a_w, b_gate_a_b=v_b_gate_a_b, b_gate_x_w=v_b_gate_x_w, b_gate_x_b=v_b_gate_x_b, b_lambda=v_b_lambda, b_w_out=v_b_w_out, norm_f_w=v_norm_f_w)

    win_l = _cast_to_segments(a_w_in[0], mine, 256)
    wout_l = _cast_to_segments(a_w_out[0], mine, 256)
    small_l = jnp.concatenate([b_conv_w[0], b_conv_b, b_gate_a_b, b_gate_x_b, b_lambda], axis=0)
    bin_l, (win_g, wout_g, small_g) = _cast_to_segments(
        b_w_in[0], mine, 128, _gather_hook([win_l, wout_l], small_l)
    )
    bout_l = _cast_to_segments(b_w_out[0], mine, 192)
    win = win_g.reshape(N_CHIPS, d, -1)
    wout = wout_g.reshape(aw, d)

    tril = jnp.tril(jnp.ones((CHUNK, CHUNK), F32))
    wc = (a_w_s[0] * tril[None]).astype(BF16)
    wct = jnp.swapaxes(wc, 1, 2)
    bs_t = a_b_s[0].T
    gab = jnp.concatenate([b_gate_a_w[0], b_gate_x_w[0]], axis=2).astype(BF16)
    gabt = jnp.swapaxes(gab, 1, 2)
    nw0, nw1, nf = norm_w[0:1], norm_w[1:2], norm_f_w.reshape(1, d)

    x0 = x[0]
    (z_a, x1, h0), (bin_g, bout_g) = _layer_a_fwd(
        x0, nw0, win, a_ln_w, a_ln_b, wc, bs_t, wout, TM_FWD, _gather_hook([bin_l, bout_l])
    )
    bin_w = bin_g.reshape(N_CHIPS, d, -1)
    bout = bout_g.reshape(bw, d)
    small_f = jnp.transpose(small_g, (1, 0, 2)).reshape(SUBLANES, bw)
    cw, cb = small_f[0:CONV_WIDTH], small_f[CONV_WIDTH : CONV_WIDTH + 1]
    gb = jnp.concatenate([small_f[5:6], small_f[6:7]], axis=1)
    lam = small_f[7:8]
    z_b, hseq, h1, dx2, loss_l, g_nf = _layer_b_fwd(
        x1, nw1, bin_w, cw, cb, gab, gb, lam, bout, nf, loss_target[0], TM_FWD
    )
    dx1, dz_b, y_b, dob_b, g_gab, g_gb, g_cw, g_cb, g_lam, g_nw1 = _layer_b_bwd(
        dx2, x1, z_b, hseq, nw1, bin_w, cw, cb, gab, gabt, gb, lam, bout, TM_FWD
    )
    seg = lambda g: g.reshape(N_DEV, -1, g.shape[3])
    x_at, y_at = lax.axis_index("x"), lax.axis_index("y")
    first_no = 2 * (x_at ^ (1 - core)) + (y_at ^ core)
    second_no = 2 * (x_at ^ core) + (y_at ^ (1 - core))
    bf16s = lambda bufs: [BF16] * len(bufs)
    own_half = lambda bufs, got, wires: _add_own_half(
        list(zip(bufs, got, wires)), (first_no, N_CHIPS - 1 - mine), core
    )
    for_neighbour = lambda bufs, got_a, got1, wires: _add_for_neighbour(
        list(zip(bufs, got_a, got1, wires)), second_no, core
    )
    received = lambda bufs, got_a, got1, got2: _add_received(
        [(b, ga, g1, g2, core, 2) for b, ga, g1, g2 in zip(bufs, got_a, got1, got2)], mine, core
    )

    g_bout = [seg(_wgrad(y_b, dob_b, 2, 1))]
    g_bin, swap_o = _wgrad(h1, dz_b, 1, N_CHIPS, _swap_hook(g_bout))
    g_bin = [seg(g_bin)]
    part_o = own_half(g_bout, swap_o, bf16s(g_bout))
    a_args = (z_a, a_ln_w, a_ln_b, wc, wct, bs_t, wout)
    half = t_rows // TM_A_BWD // 2
    first, rode = _layer_a_bwd(
        dx1, *a_args, (0, half), None, _join_hooks(_swap_hook(g_bin), _send_first_hook(part_o))
    )
    swap_i, got1_o = rode[:1], rode[1:]
    part_i = own_half(g_bin, swap_i, bf16s(g_bin))
    mid_o = for_neighbour(g_bout, swap_o, got1_o, bf16s(g_bout))
    second, rode = _layer_a_bwd(
        dx1, *a_args, (half, 2 * half), first[:3], _join_hooks(_send_first_hook(part_i), _send_second_hook(mid_o))
    )
    got1_i, got2_o = rode[:1], rode[1:]
    dz_a, y_a, dob_a = second[:3]
    g_ws, g_bst, g_lnw, g_lnb = (p + q for p, q in zip(first[3:], second[3:]))
    mid_i = for_neighbour(g_bin, swap_i, got1_i, bf16s(g_bin))
    red_o = received(g_bout, swap_o, got1_o, got2_o)
    (g_win, g_win_wire), rode = _wgrad(
        h0, dz_a, 1, N_CHIPS, _join_hooks(_send_second_hook(mid_i), _share_hook(red_o)), wire_copy=True
    )
    g_win, g_win_wire = [seg(g_win)], seg(g_win_wire)
    got2_i, gr_bout = rode[:1], rode[1].reshape(b_w_out.shape[1:])
    red_i = received(g_bin, swap_i, got1_i, got2_i)

    small_shapes = [
        (1, d), (1, aw), (1, aw), (A_GROUPS, CHUNK, CHUNK), (A_GROUPS, CHUNK), (B_HEADS, hd, hd), (B_HEADS, hd, hd),
        (d,), (CONV_WIDTH, bw), (1, bw), (1, bw), (1, bw), (1, bw), (1, 1),
    ]
    small = _pack_rows(
        [
            g_nw1, g_lnw, g_lnb, g_ws, g_bst.T, g_gab[:, :, :hd], g_gab[:, :, hd:],
            g_nf, g_cw[:CONV_WIDTH], g_cb, g_gb[:, :bw], g_gb[:, bw:], g_lam, loss_l[:, :1],
        ]
    )
    g_w, wire_w = g_win + [small], [BF16, F32]
    g_wout, rode = _wgrad(
        y_a, dob_a, N_CHIPS, 1, _join_hooks(_swap_hook([g_win_wire, small]), _share_hook(red_i))
    )
    g_wout = seg(g_wout)
    swap_w, gr_bin = rode[:2], rode[2].reshape(b_w_in.shape[1:])

    g_u, wire_u = [g_wout], [BF16]
    part_w = own_half(g_w, swap_w, wire_w)
    (grad_x, g_nw0_mine), rode = _layer_a_bwd_dx(
        dx1, x0, dz_a, nw0, win, TM_A_DX, (0, t_rows // TM_A_DX), None,
        _join_hooks(_send_first_hook(part_w), _swap_hook(g_u)),
    )
    got1_w, swap_u = rode[:2], rode[2:]
    part_u = own_half(g_u, swap_u, wire_u)
    mid_w = for_neighbour(g_w, swap_w, got1_w, wire_w)
    b_names = ("b_w_in", "b_w_out")
    two_d = lambda a: a.reshape(a.shape[-2:])
    bin_out, rode = _adamw(
        two_d(b_w_in), gr_bin, two_d(m_b_w_in), two_d(v_b_w_in),
        _join_hooks(_send_second_hook(mid_w), _send_first_hook(part_u)),
    )
    got2_w, got1_u = rode[:2], rode[2:]
    mid_u = for_neighbour(g_u, swap_u, got1_u, wire_u)
    red_w, red_small = _add_received(
        [
            (g_win[0], swap_w[0], got1_w[0], got2_w[0], core, 2),
            (small, swap_w[1], got1_w[1], got2_w[1], 2 * mine + core, N_DEV),
        ],
        mine, core,
    )
    red_w = [red_w]
    bout_out, rode = _adamw(
        two_d(b_w_out), gr_bout, two_d(m_b_w_out), two_d(v_b_w_out),
        _join_hooks(_send_second_hook(mid_u), _share_hook(red_w, red_small, g_nw0_mine)),
    )
    b_out = list(zip(bin_out, bout_out))
    got2_u, gr_win, small_r, g_nw0_all = rode[:1], rode[1].reshape(a_w_in.shape[1:]), rode[2], rode[3]
    red_wout = received(g_u, swap_u, got1_u, got2_u)
    (gr_wout,) = _run_hook(_share_hook(red_wout), "share_reduced")
    win_out = _adamw(two_d(a_w_in), gr_win, two_d(m_a_w_in), two_d(v_a_w_in))
    g_nw0 = _sum_partials(g_nw0_all)
    gr_wout = gr_wout.reshape(a_w_out.shape[1:])
    (g_nw1_r, g_a_ln_w, g_a_ln_b, g_a_w_s, g_a_b_s, g_gate_a_w, g_gate_x_w, g_norm_f, gf_cw, gf_cb, gf_gab, gf_gxb,
     gf_lam, loss) = _unpack(small_r.reshape(-1), small_shapes)
    g_norm_w = jnp.concatenate([g_nw0, g_nw1_r], axis=0)
    shard = lambda g: lax.dynamic_slice_in_dim(g, mine * (bw // N_CHIPS), bw // N_CHIPS, axis=1)

    grads = {
        "norm_w": g_norm_w, "a_w_in": gr_win[None], "a_ln_w": g_a_ln_w, "a_ln_b": g_a_ln_b, "a_w_s": g_a_w_s[None],
        "a_b_s": g_a_b_s[None], "a_w_out": gr_wout[None], "b_w_in": gr_bin[None], "b_conv_w": shard(gf_cw)[None],
        "b_conv_b": shard(gf_cb), "b_gate_a_w": g_gate_a_w[None], "b_gate_a_b": shard(gf_gab),
        "b_gate_x_w": g_gate_x_w[None], "b_gate_x_b": shard(gf_gxb), "b_lambda": shard(gf_lam),
        "b_w_out": gr_bout[None], "norm_f_w": g_norm_f,
    }
    names = list(weights)
    delta, new_m, new_v = {}, {}, {}
    delta["a_w_in"], new_m["a_w_in"], new_v["a_w_in"] = win_out
    delta["a_w_out"], new_m["a_w_out"], new_v["a_w_out"] = _adamw(
        two_d(a_w_out), gr_wout, two_d(m_a_w_out), two_d(v_a_w_out)
    )
    small_names = [n for n in names if n not in ("a_w_in", "a_w_out") + b_names]
    at_least_2d = lambda a: a.reshape(1, -1) if a.ndim == 1 else a
    small_out = _adamw_many(*[[at_least_2d(src[n]) for n in small_names] for src in (weights, grads, m_in, v_in)])
    for dst, vals, b_vals in zip((delta, new_m, new_v), small_out, b_out):
        dst.update(zip(small_names, vals))
        dst.update(zip(b_names, b_vals))
    for dst in (delta, new_m, new_v):
        for n in names:
            dst[n] = dst[n].reshape(weights[n].shape)

    return (
        loss.reshape(()),
        grad_x[None],
        *[grads[n] for n in names],
        *[delta[n] for n in names],
        *[new_m[n] for n in names],
        *[new_v[n] for n in names],
    )
```

```python
import jax
import jax.numpy as jnp
from jax import lax
from jax.experimental import pallas as pl
from jax.experimental.pallas import tpu as pltpu

F32 = jnp.float32
BF16 = jnp.bfloat16

RMS_EPS = 1e-6
LN_EPS = 1e-5
RG_C = 8.0
CHUNK = 128
A_GROUPS = 8
B_HEADS = 12
CONV_WIDTH = 4

ADAM_LR = 0.001
ADAM_B1 = 0.9
ADAM_B2 = 0.999
ADAM_EPS = 1e-08
ADAM_WD = 0.01
ADAM_STEP = 10

N_CHIPS = 4
N_DEV = 8
SUBLANES = 8
LANES = 128
V7X_VMEM_BYTES = 64 * 1024 * 1024
VMEM_LIMIT = V7X_VMEM_BYTES * 7 // 8
MESH = pl.DeviceIdType.MESH
ANY = pl.BlockSpec(memory_space=pl.ANY)

TM_FWD = 256
TM_A_BWD = 256
TM_A_DX = 512

GELU_C0 = 0.7978845608028654
GELU_C1 = 0.044715


class _Hook:
    def __init__(self, operands, out_shapes, aliases, n_sems, start, finish, middle=None, late=None):
        self.operands, self.out_shapes, self.aliases, self.n_sems = operands, out_shapes, aliases, n_sems
        self.start, self.finish, self.middle, self.late = start, finish, middle, late


class _SemView:
    def __init__(self, base, off):
        self.base, self.off = base, off

    @property
    def at(self):
        return self

    def __getitem__(self, k):
        return self.base.at[self.off + k]


def _join_hooks(*hooks):
    if len(hooks) == 1:
        return hooks[0]
    operands, out_shapes, aliases, spans = [], [], {}, []
    n_sems = 0
    for h in hooks:
        aliases.update({len(operands) + i: len(out_shapes) + o for i, o in h.aliases.items()})
        spans.append((len(operands), len(h.operands), len(out_shapes), len(h.out_shapes), n_sems))
        operands += list(h.operands)
        out_shapes += list(h.out_shapes)
        n_sems += h.n_sems

    def each(which):
        def run(ins, outs, send, recv):
            for h, (i0, ni, o0, no, s0) in zip(hooks, spans):
                step = getattr(h, which)
                if step is not None:
                    step(ins[i0 : i0 + ni], outs[o0 : o0 + no], _SemView(send, s0), _SemView(recv, s0))

        return run

    middle = each("middle") if any(h.middle is not None for h in hooks) else None
    late = each("late") if any(h.late is not None for h in hooks) else None
    return _Hook(operands, out_shapes, aliases, n_sems, each("start"), each("finish"), middle, late)


def _pcall(body, hook=None, **kw):
    if hook is None:
        return pl.pallas_call(body, **kw)
    n_pre = 0
    if "grid_spec" in kw:
        spec = kw.pop("grid_spec")
        n_pre = spec.num_scalar_prefetch
        kw.update(
            grid=tuple(spec.grid), in_specs=list(spec.in_specs), out_specs=list(spec.out_specs),
            scratch_shapes=list(spec.scratch_shapes),
        )
    n_in, n_out = len(kw["in_specs"]), len(kw["out_shape"])
    hi, ho = len(hook.operands), len(hook.out_shapes)
    grid = kw.get("grid", ())

    def wrapped(*refs):
        pre, refs = refs[:n_pre], refs[n_pre:]
        ins, h_in = refs[:n_in], refs[n_in : n_in + hi]
        outs = refs[n_in + hi : n_in + hi + n_out]
        h_out = refs[n_in + hi + n_out : n_in + hi + n_out + ho]
        scratch = refs[n_in + hi + n_out + ho : -2]
        send_sems, recv_sems = refs[-2:]
        if not grid:
            hook.start(h_in, h_out, send_sems, recv_sems)
            if hook.middle is not None:
                hook.middle(h_in, h_out, send_sems, recv_sems)
            body(*pre, *ins, *outs, *scratch)
            if hook.late is not None:
                hook.late(h_in, h_out, send_sems, recv_sems)
            hook.finish(h_in, h_out, send_sems, recv_sems)
            return
        first = pl.program_id(0) == 0
        last = pl.program_id(0) == grid[0] - 1
        for axis in range(1, len(grid)):
            first = jnp.logical_and(first, pl.program_id(axis) == 0)
            last = jnp.logical_and(last, pl.program_id(axis) == grid[axis] - 1)

        @pl.when(first)
        def _():
            hook.start(h_in, h_out, send_sems, recv_sems)

        for when, step in ((hook.middle, grid[0] // 4), (hook.late, grid[0] - 1)):
            if when is not None:
                assert len(grid) == 1 and grid[0] >= 4

                @pl.when(pl.program_id(0) == step)
                def _(when=when):
                    when(h_in, h_out, send_sems, recv_sems)

        body(*pre, *ins, *outs, *scratch)

        @pl.when(last)
        def _():
            hook.finish(h_in, h_out, send_sems, recv_sems)

    aliases = dict(kw.pop("input_output_aliases", {}))
    aliases.update({n_pre + n_in + i: n_out + o for i, o in hook.aliases.items()})
    kw.update(
        in_specs=list(kw["in_specs"]) + [ANY] * hi,
        out_specs=list(kw["out_specs"]) + [ANY] * ho,
        out_shape=list(kw["out_shape"]) + list(hook.out_shapes),
        scratch_shapes=list(kw.get("scratch_shapes", ()))
        + [pltpu.SemaphoreType.DMA((hook.n_sems,)), pltpu.SemaphoreType.DMA((hook.n_sems,))],
        input_output_aliases=aliases,
    )
    if n_pre:
        kw["grid_spec"] = pltpu.PrefetchScalarGridSpec(
            num_scalar_prefetch=n_pre, grid=kw.pop("grid"), in_specs=kw.pop("in_specs"),
            out_specs=kw.pop("out_specs"), scratch_shapes=kw.pop("scratch_shapes"),
        )
    call = pl.pallas_call(wrapped, **kw)

    def run(*operands):
        outs = call(*operands, *hook.operands)
        return outs[:n_out], outs[n_out:]

    return run


def _run_hook(hook, name):
    def body():
        pass

    return _pcall(body, hook, name=name, in_specs=[], out_specs=[], out_shape=[])()[1]


def _cparams(sem=None):
    return pltpu.CompilerParams(dimension_semantics=sem, vmem_limit_bytes=VMEM_LIMIT)


def _full(shape):
    zeros = (0,) * len(shape)
    return pl.BlockSpec(shape, lambda *_: zeros)


def _scalars(*vals):
    return jnp.stack([jnp.asarray(v, jnp.int32) for v in vals])


def _sigmoid(x):
    return 1.0 / (1.0 + jnp.exp(-x))


def _gelu(x):
    t = jnp.tanh(GELU_C0 * (x + GELU_C1 * (x * x * x)))
    return x * (0.5 * (1.0 + t))


def _gelu_and_grad(x):
    x2 = x * x
    t = jnp.tanh(GELU_C0 * (x + GELU_C1 * (x2 * x)))
    cdf = 0.5 * (1.0 + t)
    return x * cdf, cdf + 0.5 * x * (1.0 - t * t) * (GELU_C0 * (1.0 + 3.0 * GELU_C1 * x2))


def _silu_and_grad(x):
    s = _sigmoid(x)
    return x * s, s * (1.0 + x * (1.0 - s))


def _softplus_neg(lam):
    u = jnp.exp(-jnp.abs(lam))
    w = 1.0 + u
    log1p = jnp.where(w == 1.0, u, jnp.log(w) * (u / jnp.where(w == 1.0, 1.0, w - 1.0)))
    return jnp.maximum(-lam, 0.0) + log1p


def _dot(a, b):
    return jnp.dot(a, b, preferred_element_type=F32)


def _dot_nt(a, b):
    return lax.dot_general(a, b, (((1,), (1,)), ((), ())), preferred_element_type=F32)


def _dot_tn(a, b):
    return lax.dot_general(a, b, (((0,), (0,)), ((), ())), preferred_element_type=F32)


def _shift_down(v, halo, k):
    if k == 0:
        return v
    rolled = pltpu.roll(v, k, 0)
    row = lax.broadcasted_iota(jnp.int32, (SUBLANES, v.shape[1]), 0)
    top = jnp.where(row < k, pltpu.roll(halo, k, 0), rolled[:SUBLANES])
    return jnp.concatenate([top, rolled[SUBLANES:]], axis=0)


def _shift_up(v, head, k):
    if k == 0:
        return v
    n = v.shape[0]
    rolled = pltpu.roll(v, n - k, 0)
    row = lax.broadcasted_iota(jnp.int32, (SUBLANES, v.shape[1]), 0)
    bot = jnp.where(row >= SUBLANES - k, pltpu.roll(head, SUBLANES - k, 0), rolled[n - SUBLANES :])
    return jnp.concatenate([rolled[: n - SUBLANES], bot], axis=0)


def _scan_blocks(a_ref, b_ref, out_ref, carry, n_rows, reverse):
    width = a_ref.shape[1]
    row = lax.broadcasted_iota(jnp.int32, (SUBLANES, width), 0)
    n_blocks = n_rows // SUBLANES

    def block(j, carry):
        i = (n_blocks - 1 - j) if reverse else j
        r0 = pl.multiple_of(i * SUBLANES, SUBLANES)
        a = a_ref[pl.ds(r0, SUBLANES), :]
        b = b_ref[pl.ds(r0, SUBLANES), :]
        for d in (1, 2, 4):
            shift = (SUBLANES - d) if reverse else d
            keep = (row < SUBLANES - d) if reverse else (row >= d)
            a_s = pltpu.roll(a, shift, 0)
            b_s = pltpu.roll(b, shift, 0)
            b = jnp.where(keep, a * b_s + b, b)
            a = jnp.where(keep, a * a_s, a)
        h = a * carry + b
        out_ref[pl.ds(r0, SUBLANES), :] = h
        edge = h[0:1, :] if reverse else h[SUBLANES - 1 : SUBLANES, :]
        return jnp.broadcast_to(edge, (SUBLANES, width))

    return lax.fori_loop(0, n_blocks, block, carry)


def _rms_fwd(x, w):
    r = lax.rsqrt(jnp.mean(x * x, axis=-1, keepdims=True) + RMS_EPS)
    xh = x * r
    return xh * w, xh, r


def _rms_bwd(dh, xh, r, w):
    dxh = dh * w
    dx = r * (dxh - xh * jnp.mean(dxh * xh, axis=-1, keepdims=True))
    return dx, jnp.sum(dh * xh, axis=0, keepdims=True)


def _cast_to_segments(ws, mine, steps, hook=None):
    per = steps // 2

    def body(k_ref, *refs):
        for w_ref, o_ref in zip(refs[: len(ws)], refs[len(ws) :]):
            o_ref[...] = w_ref[...].astype(BF16)

    rows = [w.shape[0] // steps for w in ws]
    segment = lambda i, k_ref: (2 * k_ref[0] + i // per, i % per, 0)
    return _pcall(
        body,
        hook,
        name=f"cast_{ws[0].shape[0]}x{ws[0].shape[1]}",
        grid_spec=pltpu.PrefetchScalarGridSpec(
            num_scalar_prefetch=1,
            grid=(steps,),
            in_specs=[pl.BlockSpec((r, w.shape[1]), lambda i, k_ref: (i, 0)) for w, r in zip(ws, rows)],
            out_specs=[pl.BlockSpec((None, r, w.shape[1]), segment) for w, r in zip(ws, rows)],
        ),
        out_shape=[jax.ShapeDtypeStruct((N_DEV, w.shape[0] // 2, w.shape[1]), BF16) for w in ws],
        compiler_params=_cparams(("arbitrary",)),
    )(_scalars(mine), *ws)


def _place():
    x, y, c = lax.axis_index("x"), lax.axis_index("y"), lax.axis_index("c")
    chips = [(1 - x, y), (x, 1 - y), (1 - x, 1 - y)]
    return x, y, c, chips


def _chip_no(chip):
    return 2 * chip[0] + chip[1]


def _rcopy(src, dst, send_sem, recv_sem, to):
    return pltpu.make_async_remote_copy(
        src_ref=src, dst_ref=dst, send_sem=send_sem, recv_sem=recv_sem, device_id=to, device_id_type=MESH
    )


def _gather_hook(big, small=None):
    nb = len(big)
    n_sems = 6 * nb + 4

    def places():
        x, y, c, chips = _place()
        first = (x ^ (1 - c), y ^ c)
        second = (x ^ c, y ^ (1 - c))
        return x, y, c, chips, first, second, (1 - x, 1 - y)

    def seg(outs, b, chip, half):
        return outs[b].at[2 * _chip_no(chip) + half]

    def step1(outs, send, recv):
        x, y, c, _, first, _, _ = places()
        return [
            _rcopy(seg(outs, b, (x, y), c), seg(outs, b, (x, y), c), send.at[6 * b], recv.at[6 * b], (*first, c))
            for b in range(nb)
        ]

    def step2(outs, send, recv):
        x, y, c, _, first, second, _ = places()
        copies = []
        for b in range(nb):
            for k, chip in ((1, (x, y)), (2, first)):
                src = seg(outs, b, chip, c)
                copies.append(_rcopy(src, src, send.at[6 * b + k], recv.at[6 * b + k], (*second, c)))
        return copies

    def hand_over(outs, send, recv, k, chip):
        x, y, c, *_ = places()
        return [
            _rcopy(seg(outs, b, chip, c), seg(outs, b, chip, c), send.at[6 * b + k], recv.at[6 * b + k], (x, y, 1 - c))
            for b in range(nb)
        ]

    def wait_landed(outs, send, recv, k, chip, half):
        x, y, c, *_ = places()
        for b in range(nb):
            got = seg(outs, b, chip, half)
            _rcopy(got, got, send.at[6 * b + k], recv.at[6 * b + k], (x, y, c)).wait_recv()

    def small_copies(ins, outs, send, recv):
        x, y, c, chips, *_ = places()
        there = outs[nb].at[_chip_no((x, y))]
        return [
            _rcopy(ins[nb], there, send.at[6 * nb + j], recv.at[6 * nb + j], (*chip, c)) for j, chip in enumerate(chips)
        ]

    def local_copy(ins, outs, send):
        x, y, _, _ = _place()
        return pltpu.make_async_copy(ins[nb], outs[nb].at[_chip_no((x, y))], send.at[6 * nb + 3])

    def start(ins, outs, send, recv):
        for cp in step1(outs, send, recv):
            cp.start()
        if small is not None:
            for cp in small_copies(ins, outs, send, recv):
                cp.start()
            local_copy(ins, outs, send).start()

    def middle(ins, outs, send, recv):
        *_, first, _, _ = places()
        wait_landed(outs, send, recv, 0, first, places()[2])
        for cp in step2(outs, send, recv) + hand_over(outs, send, recv, 3, first):
            cp.start()

    def late(ins, outs, send, recv):
        x, y, c, chips, first, second, diagonal = places()
        for k, chip in ((1, second), (2, diagonal)):
            wait_landed(outs, send, recv, k, chip, c)
            for cp in hand_over(outs, send, recv, 3 + k, chip):
                cp.start()

    def finish(ins, outs, send, recv):
        x, y, c, chips, first, second, diagonal = places()
        wait_landed(outs, send, recv, 3, second, 1 - c)
        wait_landed(outs, send, recv, 4, first, 1 - c)
        wait_landed(outs, send, recv, 5, diagonal, 1 - c)
        sent = step1(outs, send, recv) + step2(outs, send, recv)
        for k, chip in ((3, first), (4, second), (5, diagonal)):
            sent += hand_over(outs, send, recv, k, chip)
        for cp in sent:
            cp.wait_send()
        if small is not None:
            for j, chip in enumerate(chips):
                got = outs[nb].at[_chip_no(chip)]
                _rcopy(got, got, send.at[6 * nb + j], recv.at[6 * nb + j], (x, y, c)).wait_recv()
            for cp in small_copies(ins, outs, send, recv):
                cp.wait_send()
            local_copy(ins, outs, send).wait()

    operands = list(big) + ([small] if small is not None else [])
    out_shapes = [jax.ShapeDtypeStruct(b.shape, b.dtype) for b in big]
    if small is not None:
        out_shapes.append(jax.ShapeDtypeStruct((N_CHIPS, *small.shape), small.dtype))
    return _Hook(operands, out_shapes, {b: b for b in range(nb)}, n_sems, start, finish, middle, late)


def _both_ways_hook(operands, out_shapes, copies_of, n_sems):
    def start(ins, outs, send, recv):
        for cp in copies_of(ins, outs, send, recv):
            cp.start()

    def finish(ins, outs, send, recv):
        for cp in copies_of(ins, outs, send, recv):
            cp.wait()

    return _Hook(operands, out_shapes, {}, n_sems, start, finish)


def _swap_hook(bufs):
    def copies_of(ins, outs, send, recv):
        x, y, c, _ = _place()
        copies = []
        for b in range(len(bufs)):
            for j in range(N_CHIPS):
                k = b * N_CHIPS + j
                copies.append(_rcopy(ins[b].at[2 * j + 1 - c], outs[b].at[j], send.at[k], recv.at[k], (x, y, 1 - c)))
        return copies

    out_shapes = [jax.ShapeDtypeStruct((N_CHIPS, *b.shape[1:]), b.dtype) for b in bufs]
    return _both_ways_hook(list(bufs), out_shapes, copies_of, len(bufs) * N_CHIPS)


def _axis_order():
    x, y, c, _ = _place()
    return (x, y), c, (x ^ (1 - c), y ^ c), (x ^ c, y ^ (1 - c)), (1 - x, 1 - y)


def _send_first_hook(parts):
    def copies_of(ins, outs, send, recv):
        _, c, first, _, _ = _axis_order()
        copies = []
        for b in range(len(parts)):
            for k in range(2):
                sem = 2 * b + k
                copies.append(_rcopy(ins[b].at[k], outs[b].at[k], send.at[sem], recv.at[sem], (*first, c)))
        return copies

    out_shapes = [jax.ShapeDtypeStruct((2, *p.shape[1:]), p.dtype) for p in parts]
    return _both_ways_hook(list(parts), out_shapes, copies_of, len(parts) * 2)


def _send_second_hook(mids):
    def copies_of(ins, outs, send, recv):
        _, c, _, second, _ = _axis_order()
        return [_rcopy(ins[b], outs[b], send.at[b], recv.at[b], (*second, c)) for b in range(len(mids))]

    out_shapes = [jax.ShapeDtypeStruct(m.shape, m.dtype) for m in mids]
    return _both_ways_hook(list(mids), out_shapes, copies_of, len(mids))


def _share_hook(big, small=None, tiny=None):
    nb = len(big)
    n_sems = nb + 7 + N_DEV
    t0 = nb + 7

    def tiny_copies(ins, outs, send, recv):
        x, y, c, _ = _place()
        there = outs[-1].at[2 * _chip_no((x, y)) + c]
        copies = []
        for r in range(1, N_DEV):
            to = (x ^ (r >> 2 & 1), y ^ (r >> 1 & 1), c ^ (r & 1))
            copies.append(_rcopy(ins[-1], there, send.at[t0 + r], recv.at[t0 + r], to))
        return copies

    def tiny_local(ins, outs, send):
        x, y, c, _ = _place()
        return pltpu.make_async_copy(ins[-1], outs[-1].at[2 * _chip_no((x, y)) + c], send.at[t0])

    def first_copies(outs, send, recv):
        x, y, c, chips = _place()
        sibling = (x, y, 1 - c)
        copies = [_rcopy(outs[b].at[c], outs[b].at[c], send.at[b], recv.at[b], sibling) for b in range(nb)]
        if small is not None:
            own = outs[nb].at[2 * _chip_no((x, y)) + c]
            copies.append(_rcopy(own, own, send.at[nb], recv.at[nb], sibling))
            for j, chip in enumerate(chips):
                copies.append(_rcopy(own, own, send.at[nb + 1 + j], recv.at[nb + 1 + j], (*chip, c)))
        return copies

    def start(ins, outs, send, recv):
        for cp in first_copies(outs, send, recv):
            cp.start()
        if tiny is not None:
            for cp in tiny_copies(ins, outs, send, recv):
                cp.start()
            tiny_local(ins, outs, send).start()

    def finish(ins, outs, send, recv):
        x, y, c, chips = _place()
        me, sibling = (x, y, c), (x, y, 1 - c)
        if tiny is not None:
            for cp in tiny_copies(ins, outs, send, recv):
                cp.wait()
            tiny_local(ins, outs, send).wait()
        passed = []
        if small is not None:
            for j, chip in enumerate(chips):
                got = outs[nb].at[2 * _chip_no(chip) + c]
                _rcopy(got, got, send.at[nb + 1 + j], recv.at[nb + 1 + j], me).wait_recv()
                fwd = _rcopy(got, got, send.at[nb + 4 + j], recv.at[nb + 4 + j], sibling)
                fwd.start()
                passed.append(fwd)
        for b in range(nb):
            got = outs[b].at[1 - c]
            _rcopy(got, got, send.at[b], recv.at[b], me).wait_recv()
        if small is not None:
            got = outs[nb].at[2 * _chip_no((x, y)) + 1 - c]
            _rcopy(got, got, send.at[nb], recv.at[nb], me).wait_recv()
            for j, chip in enumerate(chips):
                got = outs[nb].at[2 * _chip_no(chip) + 1 - c]
                _rcopy(got, got, send.at[nb + 4 + j], recv.at[nb + 4 + j], me).wait_recv()
        for cp in first_copies(outs, send, recv) + passed:
            cp.wait_send()

    operands = list(big) + ([small] if small is not None else [])
    out_shapes = [jax.ShapeDtypeStruct(a.shape, a.dtype) for a in operands]
    aliases = {i: i for i in range(len(operands))}
    if tiny is not None:
        operands.append(tiny)
        out_shapes.append(jax.ShapeDtypeStruct((N_DEV, *tiny.shape), tiny.dtype))
    return _Hook(operands, out_shapes, aliases, n_sems, start, finish)


def _row_tile(rows, cols, target_bytes=2 * 1024 * 1024):
    best = SUBLANES
    for t in range(SUBLANES, rows + 1, SUBLANES):
        if rows % t == 0 and t * cols * 4 <= target_bytes:
            best = t
    return best


def _halves(buf):
    return buf.reshape(N_CHIPS, 2, *buf.shape[1:])


def _add_own_half(sets, owners, c):
    _, rows, cols = sets[0][0].shape
    tr = _row_tile(rows, cols)
    n = len(sets)

    def body(s_ref, *refs):
        def add(k):
            a_ref, b_ref, o_ref = refs[2 * k], refs[2 * k + 1], refs[2 * n + k]
            o_ref[...] = (a_ref[...] + b_ref[...].astype(F32)).astype(sets[k][2])

        add(0)
        for k in range(1, n):
            pl.when(pl.program_id(1) == 0)(lambda k=k: add(k))

    in_specs, out_specs, out_shape, operands = [], [], [], []
    for k, (buf, got, wire) in enumerate(sets):
        r_k, c_k = (tr, cols) if k == 0 else buf.shape[1:]
        row = (lambda r: r) if k == 0 else (lambda r: 0)
        in_specs += [
            pl.BlockSpec((None, None, r_k, c_k), lambda j, r, s_ref, row=row: (s_ref[j], s_ref[2], row(r), 0)),
            pl.BlockSpec((None, r_k, c_k), lambda j, r, s_ref, row=row: (s_ref[j], row(r), 0)),
        ]
        out_specs.append(pl.BlockSpec((None, r_k, c_k), lambda j, r, s_ref, row=row: (j, row(r), 0)))
        out_shape.append(jax.ShapeDtypeStruct((2, *buf.shape[1:]), wire))
        operands += [_halves(buf), got]
    return _pcall(
        body,
        name=f"add_own_half_{rows}x{cols}",
        grid_spec=pltpu.PrefetchScalarGridSpec(
            num_scalar_prefetch=1, grid=(2, rows // tr), in_specs=in_specs, out_specs=out_specs
        ),
        out_shape=out_shape,
        compiler_params=_cparams(("arbitrary", "arbitrary")),
    )(_scalars(owners[0], owners[1], c), *operands)


def _add_for_neighbour(sets, second, c):
    _, rows, cols = sets[0][0].shape
    tr = _row_tile(rows, cols)
    n = len(sets)

    def body(s_ref, *refs):
        def add(k):
            x_ref, a_ref, g_ref, o_ref = refs[3 * k], refs[3 * k + 1], refs[3 * k + 2], refs[3 * n + k]
            o_ref[...] = ((x_ref[...] + a_ref[...].astype(F32)) + g_ref[...].astype(F32)).astype(sets[k][3])

        add(0)
        for k in range(1, n):
            pl.when(pl.program_id(0) == 0)(lambda k=k: add(k))

    in_specs, out_specs, out_shape, operands = [], [], [], []
    for k, (buf, got_a, got1, wire) in enumerate(sets):
        r_k, c_k = (tr, cols) if k == 0 else buf.shape[1:]
        row = (lambda r: r) if k == 0 else (lambda r: 0)
        in_specs += [
            pl.BlockSpec((None, None, r_k, c_k), lambda r, s_ref, row=row: (s_ref[0], s_ref[1], row(r), 0)),
            pl.BlockSpec((None, r_k, c_k), lambda r, s_ref, row=row: (s_ref[0], row(r), 0)),
            pl.BlockSpec((None, r_k, c_k), lambda r, s_ref, row=row: (1, row(r), 0)),
        ]
        out_specs.append(pl.BlockSpec((r_k, c_k), lambda r, s_ref, row=row: (row(r), 0)))
        out_shape.append(jax.ShapeDtypeStruct(buf.shape[1:], wire))
        operands += [_halves(buf), got_a, got1]
    return _pcall(
        body,
        name=f"add_for_neighbour_{rows}x{cols}",
        grid_spec=pltpu.PrefetchScalarGridSpec(
            num_scalar_prefetch=1, grid=(rows // tr,), in_specs=in_specs, out_specs=out_specs
        ),
        out_shape=out_shape,
        compiler_params=_cparams(("arbitrary",)),
    )(_scalars(second, c), *operands)


def _add_received(sets, mine, c):
    _, rows, cols = sets[0][0].shape
    tr = _row_tile(rows, cols)
    n = len(sets)

    def body(s_ref, *refs):
        def add(k):
            x_ref, a_ref, g1_ref, g2_ref, o_ref = *refs[4 * k : 4 * k + 4], refs[4 * n + k]
            own = x_ref[...] + a_ref[...].astype(F32)
            o_ref[...] = (own + g1_ref[...].astype(F32)) + g2_ref[...].astype(F32)

        add(0)
        for k in range(1, n):
            pl.when(pl.program_id(0) == 0)(lambda k=k: add(k))

    in_specs, out_specs, out_shape, operands = [], [], [], []
    for k, (buf, got_a, got1, got2, slot, n_slots) in enumerate(sets):
        r_k, c_k = (tr, cols) if k == 0 else buf.shape[1:]
        row = (lambda r: r) if k == 0 else (lambda r: 0)
        in_specs += [
            pl.BlockSpec((None, None, r_k, c_k), lambda r, s_ref, row=row: (s_ref[0], s_ref[1], row(r), 0)),
            pl.BlockSpec((None, r_k, c_k), lambda r, s_ref, row=row: (s_ref[0], row(r), 0)),
            pl.BlockSpec((None, r_k, c_k), lambda r, s_ref, row=row: (0, row(r), 0)),
            pl.BlockSpec((r_k, c_k), lambda r, s_ref, row=row: (row(r), 0)),
        ]
        out_specs.append(pl.BlockSpec((None, r_k, c_k), lambda r, s_ref, row=row, k=k: (s_ref[2 + k], row(r), 0)))
        out_shape.append(jax.ShapeDtypeStruct((n_slots, *buf.shape[1:]), F32))
        operands += [_halves(buf), got_a, got1, got2]
    return _pcall(
        body,
        name=f"add_received_{rows}x{cols}",
        grid_spec=pltpu.PrefetchScalarGridSpec(
            num_scalar_prefetch=1, grid=(rows // tr,), in_specs=in_specs, out_specs=out_specs
        ),
        out_shape=out_shape,
        compiler_params=_cparams(("arbitrary",)),
    )(_scalars(mine, c, *[s[4] for s in sets]), *operands)


def _layer_a_fwd(x, nw, win, ln_w, ln_b, wc, bs_t, wout, tm, hook):
    t_rows, d = x.shape
    n_sh, _, s_cols = win.shape
    aw = wout.shape[0]
    gd = aw // A_GROUPS
    tn = 512
    assert s_cols % tn == 0 and aw % tn == 0 and tm % CHUNK == 0

    def body(x_ref, nw_ref, win_ref, lnw_ref, lnb_ref, wc_ref, bst_ref, wout_ref, z_ref, x1_ref, h_ref, u_s, v_s, y_s):
        x = x_ref[...]
        h, _, _ = _rms_fwd(x, nw_ref[...])
        h = h.astype(BF16)
        h_ref[...] = h
        for j in range(3 * aw // tn):
            k, off = divmod(j * tn, s_cols)
            cols = slice((j * tn) % aw, (j * tn) % aw + tn)
            zj = _dot(h, win_ref[k, :, off : off + tn])
            z_ref[:, j * tn : (j + 1) * tn] = zj
            if j * tn < aw:
                u_s[:, cols] = _gelu(zj)
            elif j * tn < 2 * aw:
                v_s[:, cols] = _gelu(zj)
            else:
                u_s[:, cols] = u_s[:, cols] * (zj * _sigmoid(zj))
        v = v_s[...]
        mu = jnp.mean(v, axis=-1, keepdims=True)
        vc = v - mu
        rstd = lax.rsqrt(jnp.mean(vc * vc, axis=-1, keepdims=True) + LN_EPS)
        v_s[...] = (vc * rstd) * lnw_ref[...] + lnb_ref[...]
        for ck in range(tm // CHUNK):
            rows = slice(ck * CHUNK, (ck + 1) * CHUNK)
            for g in range(A_GROUPS):
                cols = slice(g * gd, (g + 1) * gd)
                s = _dot(wc_ref[g], v_s[rows, cols].astype(BF16)) + bst_ref[:, g : g + 1]
                y_s[rows, cols] = (u_s[rows, cols] * s).astype(BF16)
        x1_ref[...] = x + _dot(y_s[...], wout_ref[...])

    row = lambda i: (i, 0)
    return _pcall(
        body,
        hook,
        name="layer_a_fwd",
        grid=(t_rows // tm,),
        in_specs=[
            pl.BlockSpec((tm, d), row),
            _full(nw.shape),
            _full(win.shape),
            _full(ln_w.shape),
            _full(ln_b.shape),
            _full(wc.shape),
            _full(bs_t.shape),
            _full(wout.shape),
        ],
        out_specs=[pl.BlockSpec((tm, 3 * aw), row), pl.BlockSpec((tm, d), row), pl.BlockSpec((tm, d), row)],
        out_shape=[
            jax.ShapeDtypeStruct((t_rows, 3 * aw), F32),
            jax.ShapeDtypeStruct((t_rows, d), F32),
            jax.ShapeDtypeStruct((t_rows, d), BF16),
        ],
        scratch_shapes=[pltpu.VMEM((tm, aw), F32), pltpu.VMEM((tm, aw), F32), pltpu.VMEM((tm, aw), BF16)],
        compiler_params=_cparams(("arbitrary",)),
    )(x, nw, win, ln_w, ln_b, wc, bs_t, wout)


def _layer_a_bwd(dout, z, ln_w, ln_b, wc, wct, bs_t, wout, tiles, earlier, hook):
    t_rows, d = dout.shape
    aw = wout.shape[0]
    gd = aw // A_GROUPS
    tm = TM_A_BWD
    lo, hi = tiles
    n_earlier = 0 if earlier is None else len(earlier)

    def body(dout_ref, z_ref, lnw_ref, lnb_ref, wc_ref, wct_ref, bst_ref, wout_ref, *rest):
        dz_ref, y_ref, dob_ref, gws_ref, gbs_ref, glnw_ref, glnb_ref, u_s, vh_s, ds_s, dvn_s = rest[n_earlier:]

        @pl.when(pl.program_id(0) == 0)
        def _():
            gws_ref[...] = jnp.zeros_like(gws_ref)
            gbs_ref[...] = jnp.zeros_like(gbs_ref)
            glnw_ref[...] = jnp.zeros_like(glnw_ref)
            glnb_ref[...] = jnp.zeros_like(glnb_ref)

        dob = dout_ref[...].astype(BF16)
        dob_ref[...] = dob
        dy = _dot_nt(dob, wout_ref[...])

        zv = z_ref[:, aw : 2 * aw]
        vg, dvg_dz = _gelu_and_grad(zv)
        mu = jnp.mean(vg, axis=-1, keepdims=True)
        vc = vg - mu
        rstd = lax.rsqrt(jnp.mean(vc * vc, axis=-1, keepdims=True) + LN_EPS)
        vh = vc * rstd
        vh_s[...] = vh
        vn = (vh * lnw_ref[...] + lnb_ref[...]).astype(BF16)

        zu = z_ref[:, 0:aw]
        zg = z_ref[:, 2 * aw : 3 * aw]
        u, du_dz = _gelu_and_grad(zu)
        sg, dsg = _silu_and_grad(zg)
        u_s[...] = u * sg
        tril = lax.broadcasted_iota(jnp.int32, (CHUNK, CHUNK), 0) >= lax.broadcasted_iota(jnp.int32, (CHUNK, CHUNK), 1)
        for ck in range(tm // CHUNK):
            rows = slice(ck * CHUNK, (ck + 1) * CHUNK)
            for g in range(A_GROUPS):
                cols = slice(g * gd, (g + 1) * gd)
                vn_g = vn[rows, cols]
                s = _dot(wc_ref[g], vn_g) + bst_ref[:, g : g + 1]
                usg = u_s[rows, cols]
                dy_g = dy[rows, cols]
                y_ref[rows, cols] = (usg * s).astype(BF16)
                ds = dy_g * usg
                ds_s[rows, cols] = dy_g * s
                gbs_ref[:, g : g + 1] += jnp.sum(ds, axis=-1, keepdims=True)
                dsb = ds.astype(BF16)
                gws_ref[g] += jnp.where(tril, _dot_nt(dsb, vn_g), 0.0)
                dvn_s[rows, cols] = _dot(wct_ref[g], dsb)
        dusg = ds_s[...]
        dz_ref[:, 0:aw] = (dusg * sg * du_dz).astype(BF16)
        dz_ref[:, 2 * aw : 3 * aw] = (dusg * u * dsg).astype(BF16)

        dvn = dvn_s[...]
        vh = vh_s[...]
        glnw_ref[...] += jnp.sum(dvn * vh, axis=0, keepdims=True)
        glnb_ref[...] += jnp.sum(dvn, axis=0, keepdims=True)
        dvh = dvn * lnw_ref[...]
        dvg = rstd * (dvh - jnp.mean(dvh, axis=-1, keepdims=True) - vh * jnp.mean(dvh * vh, axis=-1, keepdims=True))
        dz_ref[:, aw : 2 * aw] = (dvg * dvg_dz).astype(BF16)

    row = lambda i: (i + lo, 0)
    call = _pcall(
        body,
        hook,
        name=f"layer_a_bwd_{lo}",
        grid=(hi - lo,),
        in_specs=[
            pl.BlockSpec((tm, d), row),
            pl.BlockSpec((tm, 3 * aw), row),
            _full(ln_w.shape),
            _full(ln_b.shape),
            _full(wc.shape),
            _full(wct.shape),
            _full(bs_t.shape),
            _full(wout.shape),
        ]
        + [ANY] * n_earlier,
        out_specs=[
            pl.BlockSpec((tm, 3 * aw), row),
            pl.BlockSpec((tm, aw), row),
            pl.BlockSpec((tm, d), row),
            _full((A_GROUPS, CHUNK, CHUNK)),
            _full((CHUNK, A_GROUPS)),
            _full((1, aw)),
            _full((1, aw)),
        ],
        out_shape=[
            jax.ShapeDtypeStruct((t_rows, 3 * aw), BF16),
            jax.ShapeDtypeStruct((t_rows, aw), BF16),
            jax.ShapeDtypeStruct((t_rows, d), BF16),
            jax.ShapeDtypeStruct((A_GROUPS, CHUNK, CHUNK), F32),
            jax.ShapeDtypeStruct((CHUNK, A_GROUPS), F32),
            jax.ShapeDtypeStruct((1, aw), F32),
            jax.ShapeDtypeStruct((1, aw), F32),
        ],
        scratch_shapes=[pltpu.VMEM((tm, aw), F32)] * 4,
        input_output_aliases={8 + i: i for i in range(n_earlier)},
        compiler_params=_cparams(("arbitrary",)),
    )
    return call(dout, z, ln_w, ln_b, wc, wct, bs_t, wout, *(earlier or ()))


def _layer_a_bwd_dx(dout, x, dz, nw, win, tm, tiles, earlier, hook):
    t_rows, d = x.shape
    n_sh, _, s_cols = win.shape
    lo, hi = tiles
    n_earlier = 0 if earlier is None else 1

    def body(dout_ref, x_ref, dz_ref, nw_ref, win_ref, *rest):
        gx_ref, gnw_ref = rest[n_earlier:]

        @pl.when(pl.program_id(0) == 0)
        def _():
            gnw_ref[...] = jnp.zeros_like(gnw_ref)

        dh = jnp.zeros((tm, d), F32)
        for k in range(n_sh):
            dh = dh + _dot_nt(dz_ref[:, k * s_cols : (k + 1) * s_cols], win_ref[k])
        nw = nw_ref[...]
        _, xh, r = _rms_fwd(x_ref[...], nw)
        dx, gnw = _rms_bwd(dh, xh, r, nw)
        gnw_ref[0:1, :] += gnw
        gx_ref[...] = dout_ref[...] + dx

    row = lambda i: (i + lo, 0)
    return _pcall(
        body,
        hook,
        name=f"layer_a_bwd_dx_{lo}",
        grid=(hi - lo,),
        in_specs=[
            pl.BlockSpec((tm, d), row),
            pl.BlockSpec((tm, d), row),
            pl.BlockSpec((tm, n_sh * s_cols), row),
            _full(nw.shape),
            _full(win.shape),
        ]
        + [ANY] * n_earlier,
        out_specs=[pl.BlockSpec((tm, d), row), _full((SUBLANES, d))],
        out_shape=[jax.ShapeDtypeStruct((t_rows, d), F32), jax.ShapeDtypeStruct((SUBLANES, d), F32)],
        input_output_aliases={5: 0} if n_earlier else {},
        compiler_params=_cparams(("arbitrary",)),
    )(dout, x, dz, nw, win, *([earlier] if n_earlier else []))


def _decay(r, sp_h):
    log_a = (-RG_C) * r * sp_h
    a = jnp.exp(log_a)
    mult = jnp.sqrt(jnp.tanh(-log_a) * (a * a + 1.0))
    return a, mult


ROW_CONV_B, ROW_GATE_A_B, ROW_GATE_X_B, ROW_LAMBDA = range(CONV_WIDTH, CONV_WIDTH + 4)


def _gates(xc_h, gab_ref, vec_ref, sp_h, h, hd):
    pre = _dot(xc_h.astype(BF16), gab_ref[h])
    cols = slice(h * hd, (h + 1) * hd)
    r = _sigmoid(pre[:, :hd] + vec_ref[ROW_GATE_A_B : ROW_GATE_A_B + 1, cols])
    ig = _sigmoid(pre[:, hd:] + vec_ref[ROW_GATE_X_B : ROW_GATE_X_B + 1, cols])
    a, mult = _decay(r, sp_h)
    return r, ig, a, mult


def _conv(xb, halo, vec_ref):
    xc = vec_ref[ROW_CONV_B : ROW_CONV_B + 1, :] + vec_ref[CONV_WIDTH - 1 : CONV_WIDTH, :] * xb
    for k in range(CONV_WIDTH - 1):
        xc = xc + vec_ref[k : k + 1, :] * _shift_down(xb, halo, CONV_WIDTH - 1 - k)
    return xc


def _layer_b_fwd(x1, nw, bin_w, vec, gab, bout, nf, tgt, tm):
    t_rows, d = x1.shape
    bw = bout.shape[0]
    hd = bw // B_HEADS
    nt = t_rows // tm

    def body(
        x1_ref, nw_ref, bin_ref, vec_ref, gab_ref, bout_ref, nf_ref, tgt_ref,
        z_ref, h_ref, h1_ref, dx2_ref, loss_ref, gnf_ref,
        tail_s, carry_s, a_s, b_s, hs_s, acc_s,
    ):
        @pl.when(pl.program_id(0) == 0)
        def _():
            tail_s[...] = jnp.zeros_like(tail_s)
            carry_s[...] = jnp.zeros_like(carry_s)
            acc_s[...] = jnp.zeros_like(acc_s)
            gnf_ref[...] = jnp.zeros_like(gnf_ref)

        x1 = x1_ref[...]
        h1, _, _ = _rms_fwd(x1, nw_ref[...])
        h1 = h1.astype(BF16)
        h1_ref[...] = h1
        z = jnp.concatenate([_dot(h1, bin_ref[k]) for k in range(N_CHIPS)], axis=1)
        z_ref[...] = z
        xb = z[:, :bw]
        xc = _conv(xb, tail_s[...], vec_ref)
        tail = xb[tm - SUBLANES :, :]
        tail_s[...] = tail
        sp = _softplus_neg(vec_ref[ROW_LAMBDA : ROW_LAMBDA + 1, :])
        for h in range(B_HEADS):
            cols = slice(h * hd, (h + 1) * hd)
            xc_h = xc[:, cols]
            _, ig, a, mult = _gates(xc_h, gab_ref, vec_ref, sp[:, cols], h, hd)
            a_s[:, cols] = a
            b_s[:, cols] = mult * (ig * xc_h)
        carry = _scan_blocks(a_s, b_s, hs_s, carry_s[...], tm, reverse=False)
        carry_s[...] = carry
        hs = hs_s[...]
        h_ref[...] = hs
        g = z[:, bw:]
        y = (hs * (g * _sigmoid(g))).astype(BF16)
        x2 = x1 + _dot(y, bout_ref[...])

        nf = nf_ref[...]
        o, xh, r = _rms_fwd(x2, nf)
        diff = o - tgt_ref[...]
        acc_s[...] += jnp.sum(diff * diff, axis=0, keepdims=True)
        do = diff * (1.0 / d)
        dx2, gnf = _rms_bwd(do, xh, r, nf)
        gnf_ref[...] += gnf
        dx2_ref[...] = dx2

        @pl.when(pl.program_id(0) == nt - 1)
        def _():
            total = jnp.sum(acc_s[...], axis=-1, keepdims=True) * (0.5 / d)
            loss_ref[...] = jnp.broadcast_to(total, loss_ref.shape)

    row = lambda i: (i, 0)
    return _pcall(
        body,
        name="layer_b_fwd",
        grid=(nt,),
        in_specs=[
            pl.BlockSpec((tm, d), row),
            _full(nw.shape),
            _full(bin_w.shape),
            _full(vec.shape),
            _full(gab.shape),
            _full(bout.shape),
            _full(nf.shape),
            pl.BlockSpec((tm, d), row),
        ],
        out_specs=[
            pl.BlockSpec((tm, 2 * bw), row),
            pl.BlockSpec((tm, bw), row),
            pl.BlockSpec((tm, d), row),
            pl.BlockSpec((tm, d), row),
            _full((1, LANES)),
            _full((1, d)),
        ],
        out_shape=[
            jax.ShapeDtypeStruct((t_rows, 2 * bw), F32),
            jax.ShapeDtypeStruct((t_rows, bw), F32),
            jax.ShapeDtypeStruct((t_rows, d), BF16),
            jax.ShapeDtypeStruct((t_rows, d), F32),
            jax.ShapeDtypeStruct((1, LANES), F32),
            jax.ShapeDtypeStruct((1, d), F32),
        ],
        scratch_shapes=[
            pltpu.VMEM((SUBLANES, bw), F32),
            pltpu.VMEM((SUBLANES, bw), F32),
            pltpu.VMEM((tm, bw), F32),
            pltpu.VMEM((tm, bw), F32),
            pltpu.VMEM((tm, bw), F32),
            pltpu.VMEM((1, d), F32),
        ],
        compiler_params=_cparams(("arbitrary",)),
    )(x1, nw, bin_w, vec, gab, bout, nf, tgt)


def _layer_b_bwd(dout, x1, z, hseq, nw, bin_w, vec, gab, gabt, bout, tm):
    t_rows, d = x1.shape
    bw = bout.shape[0]
    hd = bw // B_HEADS
    nt = t_rows // tm

    def body(
        dout_ref, x1_ref, z_ref, h_ref, xbt_ref, ht_ref, nw_ref, bin_ref, vec_ref, gab_ref, gabt_ref, bout_ref,
        dx1_ref, dz_ref, y_ref, dob_ref, ggab_ref, ggb_ref, gcw_ref, gcb_ref, glam_ref, gnw_ref,
        gcarry_s, afirst_s, head_s, aup_s, dh_s, gt_s, dxc_s, xc_s, r_s, ig_s,
    ):
        step = pl.program_id(0)
        tile = nt - 1 - step

        @pl.when(step == 0)
        def _():
            for ref in (ggab_ref, ggb_ref, gcw_ref, gcb_ref, glam_ref, gnw_ref, gcarry_s, afirst_s, head_s):
                ref[...] = jnp.zeros_like(ref)

        first_tile = tile == 0
        xb_halo = jnp.where(first_tile, 0.0, xbt_ref[...])
        h_halo = jnp.where(first_tile, 0.0, ht_ref[...])

        dout = dout_ref[...]
        dob = dout.astype(BF16)
        dob_ref[...] = dob
        dy = _dot_nt(dob, bout_ref[...])
        hs = h_ref[...]
        g = z_ref[:, bw:]
        sg, dsg = _silu_and_grad(g)
        y_ref[...] = (hs * sg).astype(BF16)
        dz_ref[:, bw:] = (dy * hs * dsg).astype(BF16)
        dh_s[...] = dy * sg

        xb = z_ref[:, :bw]
        xc = _conv(xb, xb_halo, vec_ref)
        xc_s[...] = xc
        lam = vec_ref[ROW_LAMBDA : ROW_LAMBDA + 1, :]
        sp = _softplus_neg(lam)
        for h in range(B_HEADS):
            cols = slice(h * hd, (h + 1) * hd)
            r, ig, a, _ = _gates(xc[:, cols], gab_ref, vec_ref, sp[:, cols], h, hd)
            r_s[:, cols] = r
            ig_s[:, cols] = ig
            aup_s[:, cols] = _shift_up(a, afirst_s[:, cols], 1)
            afirst_s[:, cols] = jnp.broadcast_to(a[0:1, :], (SUBLANES, hd))
        carry = _scan_blocks(aup_s, dh_s, gt_s, gcarry_s[...], tm, reverse=True)
        gcarry_s[...] = carry

        h_prev = _shift_down(hs, h_halo, 1)
        for h in range(B_HEADS):
            cols = slice(h * hd, (h + 1) * hd)
            xc_h = xc_s[:, cols]
            sp_h = sp[:, cols]
            r, ig = r_s[:, cols], ig_s[:, cols]
            a, mult = _decay(r, sp_h)
            gt = gt_s[:, cols]
            da = gt * h_prev[:, cols]
            dmult = gt * (ig * xc_h)
            dig = gt * (mult * xc_h)
            dxc_direct = gt * (mult * ig)
            dla = da * a - dmult * (a * a) / mult
            glam_ref[:, cols] += jnp.sum(dla * r, axis=0, keepdims=True)
            dr = dla * ((-RG_C) * sp_h)
            dpre = jnp.concatenate([dr * r * (1.0 - r), dig * ig * (1.0 - ig)], axis=1)
            ggb_ref[:, cols] += jnp.sum(dpre[:, :hd], axis=0, keepdims=True)
            ggb_ref[:, bw + h * hd : bw + (h + 1) * hd] += jnp.sum(dpre[:, hd:], axis=0, keepdims=True)
            dpb = dpre.astype(BF16)
            ggab_ref[h] += _dot_tn(xc_h.astype(BF16), dpb)
            dxc_s[:, cols] = dxc_direct + _dot(dpb, gabt_ref[h])
        glam_ref[...] = jnp.where(step == nt - 1, glam_ref[...] * (RG_C * _sigmoid(-lam)), glam_ref[...])

        dxc = dxc_s[...]
        gcb_ref[...] += jnp.sum(dxc, axis=0, keepdims=True)
        dxb = vec_ref[CONV_WIDTH - 1 : CONV_WIDTH, :] * dxc
        gcw_ref[CONV_WIDTH - 1 : CONV_WIDTH, :] += jnp.sum(dxc * xb, axis=0, keepdims=True)
        head = head_s[...]
        for k in range(CONV_WIDTH - 1):
            lag = CONV_WIDTH - 1 - k
            dxb = dxb + vec_ref[k : k + 1, :] * _shift_up(dxc, head, lag)
            gcw_ref[k : k + 1, :] += jnp.sum(dxc * _shift_down(xb, xb_halo, lag), axis=0, keepdims=True)
        head_s[...] = dxc[:SUBLANES, :]
        dz_ref[:, :bw] = dxb.astype(BF16)

        s_cols = 2 * bw // N_CHIPS
        dh1 = jnp.zeros((tm, d), F32)
        for k in range(N_CHIPS):
            dh1 = dh1 + _dot_nt(dz_ref[:, k * s_cols : (k + 1) * s_cols], bin_ref[k])
        x1 = x1_ref[...]
        nw = nw_ref[...]
        _, xh, r1 = _rms_fwd(x1, nw)
        dx, gnw = _rms_bwd(dh1, xh, r1, nw)
        gnw_ref[...] += gnw
        dx1_ref[...] = dout + dx

    rev = lambda i: (nt - 1 - i, 0)
    prev = lambda i: (jnp.maximum((nt - 1 - i) * (tm // SUBLANES) - 1, 0), 0)
    return _pcall(
        body,
        name="layer_b_bwd",
        grid=(nt,),
        in_specs=[
            pl.BlockSpec((tm, d), rev),
            pl.BlockSpec((tm, d), rev),
            pl.BlockSpec((tm, 2 * bw), rev),
            pl.BlockSpec((tm, bw), rev),
            pl.BlockSpec((SUBLANES, bw), prev),
            pl.BlockSpec((SUBLANES, bw), prev),
            _full(nw.shape),
            _full(bin_w.shape),
            _full(vec.shape),
            _full(gab.shape),
            _full(gabt.shape),
            _full(bout.shape),
        ],
        out_specs=[
            pl.BlockSpec((tm, d), rev),
            pl.BlockSpec((tm, 2 * bw), rev),
            pl.BlockSpec((tm, bw), rev),
            pl.BlockSpec((tm, d), rev),
            _full((B_HEADS, hd, 2 * hd)),
            _full((1, 2 * bw)),
            _full((SUBLANES, bw)),
            _full((1, bw)),
            _full((1, bw)),
            _full((1, d)),
        ],
        out_shape=[
            jax.ShapeDtypeStruct((t_rows, d), F32),
            jax.ShapeDtypeStruct((t_rows, 2 * bw), BF16),
            jax.ShapeDtypeStruct((t_rows, bw), BF16),
            jax.ShapeDtypeStruct((t_rows, d), BF16),
            jax.ShapeDtypeStruct((B_HEADS, hd, 2 * hd), F32),
            jax.ShapeDtypeStruct((1, 2 * bw), F32),
            jax.ShapeDtypeStruct((SUBLANES, bw), F32),
            jax.ShapeDtypeStruct((1, bw), F32),
            jax.ShapeDtypeStruct((1, bw), F32),
            jax.ShapeDtypeStruct((1, d), F32),
        ],
        scratch_shapes=[pltpu.VMEM((SUBLANES, bw), F32)] * 3 + [pltpu.VMEM((tm, bw), F32)] * 7,
        compiler_params=_cparams(("arbitrary",)),
    )(dout, x1, z, hseq, z, hseq, nw, bin_w, vec, gab, gabt, bout)


def _wgrad(a, b, m_blocks, n_blocks, hook=None, wire_copy=False):
    k, m = a.shape
    n = b.shape[1]
    bm, bn = m // m_blocks, n // n_blocks

    def body(a_ref, b_ref, o_ref, *wire_ref):
        prod = _dot_tn(a_ref[...], b_ref[...])
        o_ref[...] = prod
        if wire_copy:
            wire_ref[0][...] = prod.astype(BF16)

    out_spec = pl.BlockSpec((None, None, bm, bn), lambda j, i: (j, i, 0, 0))
    shape = (n_blocks, m_blocks, bm, bn)
    out = _pcall(
        body,
        hook,
        name=f"wgrad_{m}x{n}",
        grid=(n_blocks, m_blocks),
        in_specs=[pl.BlockSpec((k, bm), lambda j, i: (0, i)), pl.BlockSpec((k, bn), lambda j, i: (0, j))],
        out_specs=[out_spec] * (1 + wire_copy),
        out_shape=[jax.ShapeDtypeStruct(shape, F32)] + [jax.ShapeDtypeStruct(shape, BF16)] * wire_copy,
        compiler_params=_cparams(("arbitrary", "arbitrary")),
    )(a, b)
    outs, rode = (out, None) if hook is None else out
    outs = outs if wire_copy else outs[0]
    return outs if hook is None else (outs, rode)


def _adamw_math(w, g, m, v):
    m = ADAM_B1 * m + (1.0 - ADAM_B1) * g
    v = ADAM_B2 * v + (1.0 - ADAM_B2) * (g * g)
    m_hat = m / (1.0 - ADAM_B1**ADAM_STEP)
    v_hat = v / (1.0 - ADAM_B2**ADAM_STEP)
    delta = -ADAM_LR * (m_hat / (jnp.sqrt(v_hat) + ADAM_EPS) + ADAM_WD * w)
    return delta, m, v


def _adamw(w, g, m, v, hook=None):
    rows, cols = w.shape
    tr = _row_tile(rows, cols, 1024 * 1024)

    def body(w_ref, g_ref, m_ref, v_ref, d_ref, mo_ref, vo_ref):
        d_ref[...], mo_ref[...], vo_ref[...] = _adamw_math(w_ref[...], g_ref[...], m_ref[...], v_ref[...])

    spec = pl.BlockSpec((tr, cols), lambda i: (i, 0))
    return _pcall(
        body,
        hook,
        name=f"adamw_{rows}x{cols}",
        grid=(rows // tr,),
        in_specs=[spec] * 4,
        out_specs=[spec] * 3,
        out_shape=[jax.ShapeDtypeStruct((rows, cols), F32)] * 3,
        compiler_params=_cparams(("arbitrary",)),
    )(w, g, m, v)


def _sum_partials(parts):
    def body(p_ref, o_ref):
        total = p_ref[0, 0:1, :]
        for k in range(1, N_DEV):
            total = total + p_ref[k, 0:1, :]
        o_ref[...] = total

    vmem = pl.BlockSpec(memory_space=pltpu.VMEM)
    return _pcall(
        body,
        name="sum_partials",
        in_specs=[vmem],
        out_specs=vmem,
        out_shape=jax.ShapeDtypeStruct((1, parts.shape[2]), F32),
    )(parts)


def _adamw_many(ws, gs, ms, vs):
    n = len(ws)

    def body(*refs):
        w_refs, g_refs, m_refs, v_refs = (refs[i * n : (i + 1) * n] for i in range(4))
        d_refs, mo_refs, vo_refs = (refs[(4 + i) * n : (5 + i) * n] for i in range(3))
        for i in range(n):
            d_refs[i][...], mo_refs[i][...], vo_refs[i][...] = _adamw_math(
                w_refs[i][...], g_refs[i][...], m_refs[i][...], v_refs[i][...]
            )

    vmem = pl.BlockSpec(memory_space=pltpu.VMEM)
    outs = _pcall(
        body,
        name="adamw_small",
        in_specs=[vmem] * (4 * n),
        out_specs=[vmem] * (3 * n),
        out_shape=[jax.ShapeDtypeStruct(w.shape, F32) for w in ws] * 3,
        compiler_params=_cparams(),
    )(*ws, *gs, *ms, *vs)
    return outs[:n], outs[n : 2 * n], outs[2 * n :]


def _pack_rows(parts, lanes=LANES):
    flat = jnp.concatenate([p.reshape(-1) for p in parts])
    per = N_DEV * SUBLANES * lanes
    total = -(-flat.shape[0] // per) * per
    flat = jnp.pad(flat, (0, total - flat.shape[0]))
    return flat.reshape(N_DEV, total // (N_DEV * lanes), lanes)


def _unpack(flat, shapes):
    out, at = [], 0
    for s in shapes:
        n = 1
        for dim in s:
            n *= dim
        out.append(flat[at : at + n].reshape(s))
        at += n
    return out


def kernel(x, norm_w, a_w_in, a_ln_w, a_ln_b, a_w_s, a_b_s, a_w_out, b_w_in, b_conv_w, b_conv_b, b_gate_a_w, b_gate_a_b, b_gate_x_w, b_gate_x_b, b_lambda, b_w_out, norm_f_w, loss_target, m_norm_w, m_a_w_in, m_a_ln_w, m_a_ln_b, m_a_w_s, m_a_b_s, m_a_w_out, m_b_w_in, m_b_conv_w, m_b_conv_b, m_b_gate_a_w, m_b_gate_a_b, m_b_gate_x_w, m_b_gate_x_b, m_b_lambda, m_b_w_out, m_norm_f_w, v_norm_w, v_a_w_in, v_a_ln_w, v_a_ln_b, v_a_w_s, v_a_b_s, v_a_w_out, v_b_w_in, v_b_conv_w, v_b_conv_b, v_b_gate_a_w, v_b_gate_a_b, v_b_gate_x_w, v_b_gate_x_b, v_b_lambda, v_b_w_out, v_norm_f_w):
    t_rows, d = x.shape[1], x.shape[2]
    aw = a_ln_w.shape[1]
    bw = b_gate_a_w.shape[1] * b_gate_a_w.shape[2]
    hd = bw // B_HEADS
    mine = 2 * lax.axis_index("x") + lax.axis_index("y")
    core = lax.axis_index("c")
    weights = dict(norm_w=norm_w, a_w_in=a_w_in, a_ln_w=a_ln_w, a_ln_b=a_ln_b, a_w_s=a_w_s, a_b_s=a_b_s, a_w_out=a_w_out, b_w_in=b_w_in, b_conv_w=b_conv_w, b_conv_b=b_conv_b, b_gate_a_w=b_gate_a_w, b_gate_a_b=b_gate_a_b, b_gate_x_w=b_gate_x_w, b_gate_x_b=b_gate_x_b, b_lambda=b_lambda, b_w_out=b_w_out, norm_f_w=norm_f_w)
    m_in = dict(norm_w=m_norm_w, a_w_in=m_a_w_in, a_ln_w=m_a_ln_w, a_ln_b=m_a_ln_b, a_w_s=m_a_w_s, a_b_s=m_a_b_s, a_w_out=m_a_w_out, b_w_in=m_b_w_in, b_conv_w=m_b_conv_w, b_conv_b=m_b_conv_b, b_gate_a_w=m_b_gate_a_w, b_gate_a_b=m_b_gate_a_b, b_gate_x_w=m_b_gate_x_w, b_gate_x_b=m_b_gate_x_b, b_lambda=m_b_lambda, b_w_out=m_b_w_out, norm_f_w=m_norm_f_w)
    v_in = dict(norm_w=v_norm_w, a_w_in=v_a_w_in, a_ln_w=v_a_ln_w, a_ln_b=v_a_ln_b, a_w_s=v_a_w_s, a_b_s=v_a_b_s, a_w_out=v_a_w_out, b_w_in=v_b_w_in, b_conv_w=v_b_conv_w, b_conv_b=v_b_conv_b, b_gate_a_w=v_b_gate_a_w, b_gate_a_b=v_b_gate_a_b, b_gate_x_w=v_b_gate_x_w, b_gate_x_b=v_b_gate_x_b, b_lambda=v_b_lambda, b_w_out=v_b_w_out, norm_f_w=v_norm_f_w)

    win_l, wout_l = _cast_to_segments([a_w_in[0], a_w_out[0]], mine, 4)
    small_l = jnp.concatenate([b_conv_w[0], b_conv_b, b_gate_a_b, b_gate_x_b, b_lambda], axis=0)
    (bin_l, bout_l), (win_g, wout_g, small_g) = _cast_to_segments(
        [b_w_in[0], b_w_out[0]], mine, 8, _gather_hook([win_l, wout_l], small_l)
    )
    win = win_g.reshape(N_CHIPS, d, -1)
    wout = wout_g.reshape(aw, d)

    tril = jnp.tril(jnp.ones((CHUNK, CHUNK), F32))
    wc = (a_w_s[0] * tril[None]).astype(BF16)
    wct = jnp.swapaxes(wc, 1, 2)
    bs_t = a_b_s[0].T
    gab = jnp.concatenate([b_gate_a_w[0], b_gate_x_w[0]], axis=2).astype(BF16)
    gabt = jnp.swapaxes(gab, 1, 2)
    nw0, nw1, nf = norm_w[0:1], norm_w[1:2], norm_f_w.reshape(1, d)

    x0 = x[0]
    (z_a, x1, h0), (bin_g, bout_g) = _layer_a_fwd(
        x0, nw0, win, a_ln_w, a_ln_b, wc, bs_t, wout, TM_FWD, _gather_hook([bin_l, bout_l])
    )
    bin_w = bin_g.reshape(N_CHIPS, d, -1)
    bout = bout_g.reshape(bw, d)
    vec = jnp.transpose(small_g, (1, 0, 2)).reshape(SUBLANES, bw)
    z_b, hseq, h1, dx2, loss_l, g_nf = _layer_b_fwd(x1, nw1, bin_w, vec, gab, bout, nf, loss_target[0], TM_FWD)
    dx1, dz_b, y_b, dob_b, g_gab, g_gb, g_cw, g_cb, g_lam, g_nw1 = _layer_b_bwd(
        dx2, x1, z_b, hseq, nw1, bin_w, vec, gab, gabt, bout, TM_FWD
    )
    seg = lambda g: g.reshape(N_DEV, -1, g.shape[3])
    x_at, y_at = lax.axis_index("x"), lax.axis_index("y")
    first_no = 2 * (x_at ^ (1 - core)) + (y_at ^ core)
    second_no = 2 * (x_at ^ core) + (y_at ^ (1 - core))
    bf16s = lambda bufs: [BF16] * len(bufs)
    own_half = lambda bufs, got, wires: _add_own_half(
        list(zip(bufs, got, wires)), (first_no, N_CHIPS - 1 - mine), core
    )
    for_neighbour = lambda bufs, got_a, got1, wires: _add_for_neighbour(
        list(zip(bufs, got_a, got1, wires)), second_no, core
    )
    received = lambda bufs, got_a, got1, got2: _add_received(
        [(b, ga, g1, g2, core, 2) for b, ga, g1, g2 in zip(bufs, got_a, got1, got2)], mine, core
    )

    g_bout = [seg(_wgrad(y_b, dob_b, 2, 1))]
    g_bin, swap_o = _wgrad(h1, dz_b, 1, N_CHIPS, _swap_hook(g_bout))
    g_bin = [seg(g_bin)]
    part_o = own_half(g_bout, swap_o, bf16s(g_bout))
    a_args = (z_a, a_ln_w, a_ln_b, wc, wct, bs_t, wout)
    half = t_rows // TM_A_BWD // 2
    first, rode = _layer_a_bwd(
        dx1, *a_args, (0, half), None, _join_hooks(_swap_hook(g_bin), _send_first_hook(part_o))
    )
    swap_i, got1_o = rode[:1], rode[1:]
    part_i = own_half(g_bin, swap_i, bf16s(g_bin))
    mid_o = for_neighbour(g_bout, swap_o, got1_o, bf16s(g_bout))
    second, rode = _layer_a_bwd(
        dx1, *a_args, (half, 2 * half), first[:3], _join_hooks(_send_first_hook(part_i), _send_second_hook(mid_o))
    )
    got1_i, got2_o = rode[:1], rode[1:]
    dz_a, y_a, dob_a = second[:3]
    g_ws, g_bst, g_lnw, g_lnb = (p + q for p, q in zip(first[3:], second[3:]))
    mid_i = for_neighbour(g_bin, swap_i, got1_i, bf16s(g_bin))
    red_o = received(g_bout, swap_o, got1_o, got2_o)
    (g_win, g_win_wire), rode = _wgrad(
        h0, dz_a, 1, N_CHIPS, _join_hooks(_send_second_hook(mid_i), _share_hook(red_o)), wire_copy=True
    )
    g_win, g_win_wire = [seg(g_win)], seg(g_win_wire)
    got2_i, gr_bout = rode[:1], rode[1].reshape(b_w_out.shape[1:])
    red_i = received(g_bin, swap_i, got1_i, got2_i)

    small_shapes = [
        (1, d), (1, aw), (1, aw), (A_GROUPS, CHUNK, CHUNK), (A_GROUPS, CHUNK), (B_HEADS, hd, hd), (B_HEADS, hd, hd),
        (d,), (CONV_WIDTH, bw), (1, bw), (1, bw), (1, bw), (1, bw), (1, 1),
    ]
    small = _pack_rows(
        [
            g_nw1, g_lnw, g_lnb, g_ws, g_bst.T, g_gab[:, :, :hd], g_gab[:, :, hd:],
            g_nf, g_cw[:CONV_WIDTH], g_cb, g_gb[:, :bw], g_gb[:, bw:], g_lam, loss_l[:, :1],
        ]
    )
    g_w, wire_w = g_win + [small], [BF16, F32]
    g_wout, rode = _wgrad(
        y_a, dob_a, N_CHIPS, 1, _join_hooks(_swap_hook([g_win_wire, small]), _share_hook(red_i))
    )
    g_wout = seg(g_wout)
    swap_w, gr_bin = rode[:2], rode[2].reshape(b_w_in.shape[1:])

    g_u, wire_u = [g_wout], [BF16]
    part_w = own_half(g_w, swap_w, wire_w)
    (grad_x, g_nw0_mine), rode = _layer_a_bwd_dx(
        dx1, x0, dz_a, nw0, win, TM_A_DX, (0, t_rows // TM_A_DX), None,
        _join_hooks(_send_first_hook(part_w), _swap_hook(g_u)),
    )
    got1_w, swap_u = rode[:2], rode[2:]
    part_u = own_half(g_u, swap_u, wire_u)
    mid_w = for_neighbour(g_w, swap_w, got1_w, wire_w)
    b_names = ("b_w_in", "b_w_out")
    two_d = lambda a: a.reshape(a.shape[-2:])
    bin_out, rode = _adamw(
        two_d(b_w_in), gr_bin, two_d(m_b_w_in), two_d(v_b_w_in),
        _join_hooks(_send_second_hook(mid_w), _send_first_hook(part_u)),
    )
    got2_w, got1_u = rode[:2], rode[2:]
    mid_u = for_neighbour(g_u, swap_u, got1_u, wire_u)
    red_w, red_small = _add_received(
        [
            (g_win[0], swap_w[0], got1_w[0], got2_w[0], core, 2),
            (small, swap_w[1], got1_w[1], got2_w[1], 2 * mine + core, N_DEV),
        ],
        mine, core,
    )
    red_w = [red_w]
    bout_out, rode = _adamw(
        two_d(b_w_out), gr_bout, two_d(m_b_w_out), two_d(v_b_w_out),
        _join_hooks(_send_second_hook(mid_u), _share_hook(red_w, red_small, g_nw0_mine)),
    )
    b_out = list(zip(bin_out, bout_out))
    got2_u, gr_win, small_r, g_nw0_all = rode[:1], rode[1].reshape(a_w_in.shape[1:]), rode[2], rode[3]
    red_wout = received(g_u, swap_u, got1_u, got2_u)
    (gr_wout,) = _run_hook(_share_hook(red_wout), "share_reduced")
    win_out = _adamw(two_d(a_w_in), gr_win, two_d(m_a_w_in), two_d(v_a_w_in))
    g_nw0 = _sum_partials(g_nw0_all)
    gr_wout = gr_wout.reshape(a_w_out.shape[1:])
    (g_nw1_r, g_a_ln_w, g_a_ln_b, g_a_w_s, g_a_b_s, g_gate_a_w, g_gate_x_w, g_norm_f, gf_cw, gf_cb, gf_gab, gf_gxb,
     gf_lam, loss) = _unpack(small_r.reshape(-1), small_shapes)
    g_norm_w = jnp.concatenate([g_nw0, g_nw1_r], axis=0)
    shard = lambda g: lax.dynamic_slice_in_dim(g, mine * (bw // N_CHIPS), bw // N_CHIPS, axis=1)

    grads = {
        "norm_w": g_norm_w, "a_w_in": gr_win[None], "a_ln_w": g_a_ln_w, "a_ln_b": g_a_ln_b, "a_w_s": g_a_w_s[None],
        "a_b_s": g_a_b_s[None], "a_w_out": gr_wout[None], "b_w_in": gr_bin[None], "b_conv_w": shard(gf_cw)[None],
        "b_conv_b": shard(gf_cb), "b_gate_a_w": g_gate_a_w[None], "b_gate_a_b": shard(gf_gab),
        "b_gate_x_w": g_gate_x_w[None], "b_gate_x_b": shard(gf_gxb), "b_lambda": shard(gf_lam),
        "b_w_out": gr_bout[None], "norm_f_w": g_norm_f,
    }
    names = list(weights)
    delta, new_m, new_v = {}, {}, {}
    delta["a_w_in"], new_m["a_w_in"], new_v["a_w_in"] = win_out
    delta["a_w_out"], new_m["a_w_out"], new_v["a_w_out"] = _adamw(
        two_d(a_w_out), gr_wout, two_d(m_a_w_out), two_d(v_a_w_out)
    )
    small_names = [n for n in names if n not in ("a_w_in", "a_w_out") + b_names]
    at_least_2d = lambda a: a.reshape(1, -1) if a.ndim == 1 else a
    small_out = _adamw_many(*[[at_least_2d(src[n]) for n in small_names] for src in (weights, grads, m_in, v_in)])
    for dst, vals, b_vals in zip((delta, new_m, new_v), small_out, b_out):
        dst.update(zip(small_names, vals))
        dst.update(zip(b_names, b_vals))
    for dst in (delta, new_m, new_v):
        for n in names:
            dst[n] = dst[n].reshape(weights[n].shape)

    return (
        loss.reshape(()),
        grad_x[None],
        *[grads[n] for n in names],
        *[delta[n] for n in names],
        *[new_m[n] for n in names],
        *[new_v[n] for n in names],
    )
```

```python
import jax
import jax.numpy as jnp
from jax import lax
from jax.experimental import pallas as pl
from jax.experimental.pallas import tpu as pltpu

F32 = jnp.float32
BF16 = jnp.bfloat16

RMS_EPS = 1e-6
LN_EPS = 1e-5
RG_C = 8.0
CHUNK = 128
A_GROUPS = 8
B_HEADS = 12
CONV_WIDTH = 4

ADAM_LR = 0.001
ADAM_B1 = 0.9
ADAM_B2 = 0.999
ADAM_EPS = 1e-08
ADAM_WD = 0.01
ADAM_STEP = 10

N_CHIPS = 4
N_DEV = 8
SUBLANES = 8
LANES = 128
V7X_VMEM_BYTES = 64 * 1024 * 1024
VMEM_LIMIT = V7X_VMEM_BYTES * 7 // 8
MESH = pl.DeviceIdType.MESH
ANY = pl.BlockSpec(memory_space=pl.ANY)

TM_FWD = 256
TM_A_BWD = 256
TM_A_DX = 512

GELU_C0 = 0.7978845608028654
GELU_C1 = 0.044715


class _Hook:
    def __init__(self, operands, out_shapes, aliases, n_sems, start, finish, middle=None, late=None):
        self.operands, self.out_shapes, self.aliases, self.n_sems = operands, out_shapes, aliases, n_sems
        self.start, self.finish, self.middle, self.late = start, finish, middle, late


class _SemView:
    def __init__(self, base, off):
        self.base, self.off = base, off

    @property
    def at(self):
        return self

    def __getitem__(self, k):
        return self.base.at[self.off + k]


def _join_hooks(*hooks):
    if len(hooks) == 1:
        return hooks[0]
    operands, out_shapes, aliases, spans = [], [], {}, []
    n_sems = 0
    for h in hooks:
        aliases.update({len(operands) + i: len(out_shapes) + o for i, o in h.aliases.items()})
        spans.append((len(operands), len(h.operands), len(out_shapes), len(h.out_shapes), n_sems))
        operands += list(h.operands)
        out_shapes += list(h.out_shapes)
        n_sems += h.n_sems

    def each(which):
        def run(ins, outs, send, recv):
            for h, (i0, ni, o0, no, s0) in zip(hooks, spans):
                step = getattr(h, which)
                if step is not None:
                    step(ins[i0 : i0 + ni], outs[o0 : o0 + no], _SemView(send, s0), _SemView(recv, s0))

        return run

    middle = each("middle") if any(h.middle is not None for h in hooks) else None
    late = each("late") if any(h.late is not None for h in hooks) else None
    return _Hook(operands, out_shapes, aliases, n_sems, each("start"), each("finish"), middle, late)


def _pcall(body, hook=None, **kw):
    if hook is None:
        return pl.pallas_call(body, **kw)
    n_pre = 0
    if "grid_spec" in kw:
        spec = kw.pop("grid_spec")
        n_pre = spec.num_scalar_prefetch
        kw.update(
            grid=tuple(spec.grid), in_specs=list(spec.in_specs), out_specs=list(spec.out_specs),
            scratch_shapes=list(spec.scratch_shapes),
        )
    n_in, n_out = len(kw["in_specs"]), len(kw["out_shape"])
    hi, ho = len(hook.operands), len(hook.out_shapes)
    grid = kw.get("grid", ())

    def wrapped(*refs):
        pre, refs = refs[:n_pre], refs[n_pre:]
        ins, h_in = refs[:n_in], refs[n_in : n_in + hi]
        outs = refs[n_in + hi : n_in + hi + n_out]
        h_out = refs[n_in + hi + n_out : n_in + hi + n_out + ho]
        scratch = refs[n_in + hi + n_out + ho : -2]
        send_sems, recv_sems = refs[-2:]
        if not grid:
            hook.start(h_in, h_out, send_sems, recv_sems)
            if hook.middle is not None:
                hook.middle(h_in, h_out, send_sems, recv_sems)
            body(*pre, *ins, *outs, *scratch)
            if hook.late is not None:
                hook.late(h_in, h_out, send_sems, recv_sems)
            hook.finish(h_in, h_out, send_sems, recv_sems)
            return
        first = pl.program_id(0) == 0
        last = pl.program_id(0) == grid[0] - 1
        for axis in range(1, len(grid)):
            first = jnp.logical_and(first, pl.program_id(axis) == 0)
            last = jnp.logical_and(last, pl.program_id(axis) == grid[axis] - 1)

        @pl.when(first)
        def _():
            hook.start(h_in, h_out, send_sems, recv_sems)

        for when, step in ((hook.middle, grid[0] // 4), (hook.late, grid[0] - 1)):
            if when is not None:
                assert len(grid) == 1 and grid[0] >= 4

                @pl.when(pl.program_id(0) == step)
                def _(when=when):
                    when(h_in, h_out, send_sems, recv_sems)

        body(*pre, *ins, *outs, *scratch)

        @pl.when(last)
        def _():
            hook.finish(h_in, h_out, send_sems, recv_sems)

    aliases = dict(kw.pop("input_output_aliases", {}))
    aliases.update({n_pre + n_in + i: n_out + o for i, o in hook.aliases.items()})
    kw.update(
        in_specs=list(kw["in_specs"]) + [ANY] * hi,
        out_specs=list(kw["out_specs"]) + [ANY] * ho,
        out_shape=list(kw["out_shape"]) + list(hook.out_shapes),
        scratch_shapes=list(kw.get("scratch_shapes", ()))
        + [pltpu.SemaphoreType.DMA((hook.n_sems,)), pltpu.SemaphoreType.DMA((hook.n_sems,))],
        input_output_aliases=aliases,
    )
    if n_pre:
        kw["grid_spec"] = pltpu.PrefetchScalarGridSpec(
            num_scalar_prefetch=n_pre, grid=kw.pop("grid"), in_specs=kw.pop("in_specs"),
            out_specs=kw.pop("out_specs"), scratch_shapes=kw.pop("scratch_shapes"),
        )
    call = pl.pallas_call(wrapped, **kw)

    def run(*operands):
        outs = call(*operands, *hook.operands)
        return outs[:n_out], outs[n_out:]

    return run


def _run_hook(hook, name):
    def body():
        pass

    return _pcall(body, hook, name=name, in_specs=[], out_specs=[], out_shape=[])()[1]


def _cparams(sem=None):
    return pltpu.CompilerParams(dimension_semantics=sem, vmem_limit_bytes=VMEM_LIMIT)


def _full(shape):
    zeros = (0,) * len(shape)
    return pl.BlockSpec(shape, lambda *_: zeros)


def _scalars(*vals):
    return jnp.stack([jnp.asarray(v, jnp.int32) for v in vals])


def _sigmoid(x):
    return 1.0 / (1.0 + jnp.exp(-x))


def _gelu(x):
    t = jnp.tanh(GELU_C0 * (x + GELU_C1 * (x * x * x)))
    return x * (0.5 * (1.0 + t))


def _gelu_and_grad(x):
    x2 = x * x
    t = jnp.tanh(GELU_C0 * (x + GELU_C1 * (x2 * x)))
    cdf = 0.5 * (1.0 + t)
    return x * cdf, cdf + 0.5 * x * (1.0 - t * t) * (GELU_C0 * (1.0 + 3.0 * GELU_C1 * x2))


def _silu_and_grad(x):
    s = _sigmoid(x)
    return x * s, s * (1.0 + x * (1.0 - s))


def _softplus_neg(lam):
    u = jnp.exp(-jnp.abs(lam))
    w = 1.0 + u
    log1p = jnp.where(w == 1.0, u, jnp.log(w) * (u / jnp.where(w == 1.0, 1.0, w - 1.0)))
    return jnp.maximum(-lam, 0.0) + log1p


def _dot(a, b):
    return jnp.dot(a, b, preferred_element_type=F32)


def _dot_nt(a, b):
    return lax.dot_general(a, b, (((1,), (1,)), ((), ())), preferred_element_type=F32)


def _dot_tn(a, b):
    return lax.dot_general(a, b, (((0,), (0,)), ((), ())), preferred_element_type=F32)


def _shift_down(v, halo, k):
    if k == 0:
        return v
    rolled = pltpu.roll(v, k, 0)
    row = lax.broadcasted_iota(jnp.int32, (SUBLANES, v.shape[1]), 0)
    top = jnp.where(row < k, pltpu.roll(halo, k, 0), rolled[:SUBLANES])
    return jnp.concatenate([top, rolled[SUBLANES:]], axis=0)


def _shift_up(v, head, k):
    if k == 0:
        return v
    n = v.shape[0]
    rolled = pltpu.roll(v, n - k, 0)
    row = lax.broadcasted_iota(jnp.int32, (SUBLANES, v.shape[1]), 0)
    bot = jnp.where(row >= SUBLANES - k, pltpu.roll(head, SUBLANES - k, 0), rolled[n - SUBLANES :])
    return jnp.concatenate([rolled[: n - SUBLANES], bot], axis=0)


def _scan_blocks(a_ref, b_ref, out_ref, carry, n_rows, reverse):
    width = a_ref.shape[1]
    row = lax.broadcasted_iota(jnp.int32, (SUBLANES, width), 0)
    n_blocks = n_rows // SUBLANES

    def block(j, carry):
        i = (n_blocks - 1 - j) if reverse else j
        r0 = pl.multiple_of(i * SUBLANES, SUBLANES)
        a = a_ref[pl.ds(r0, SUBLANES), :]
        b = b_ref[pl.ds(r0, SUBLANES), :]
        for d in (1, 2, 4):
            shift = (SUBLANES - d) if reverse else d
            keep = (row < SUBLANES - d) if reverse else (row >= d)
            a_s = pltpu.roll(a, shift, 0)
            b_s = pltpu.roll(b, shift, 0)
            b = jnp.where(keep, a * b_s + b, b)
            a = jnp.where(keep, a * a_s, a)
        h = a * carry + b
        out_ref[pl.ds(r0, SUBLANES), :] = h
        edge = h[0:1, :] if reverse else h[SUBLANES - 1 : SUBLANES, :]
        return jnp.broadcast_to(edge, (SUBLANES, width))

    return lax.fori_loop(0, n_blocks, block, carry)


def _rms_fwd(x, w):
    r = lax.rsqrt(jnp.mean(x * x, axis=-1, keepdims=True) + RMS_EPS)
    xh = x * r
    return xh * w, xh, r


def _rms_bwd(dh, xh, r, w):
    dxh = dh * w
    dx = r * (dxh - xh * jnp.mean(dxh * xh, axis=-1, keepdims=True))
    return dx, jnp.sum(dh * xh, axis=0, keepdims=True)


def _cast_to_segments(ws, mine, steps, hook=None, small_maps=None):
    per = steps // 2
    n = len(ws)
    maps = () if small_maps is None else small_maps

    def body(k_ref, *refs):
        for w_ref, o_ref in zip(refs[:n], refs[n + len(maps) : 2 * n + len(maps)]):
            o_ref[...] = w_ref[...].astype(BF16)
        if small_maps is not None:
            pl.when(pl.program_id(0) == 0)(lambda: _prepare_small_maps(*refs[n : n + 3], *refs[2 * n + 3 :]))

    rows = [w.shape[0] // steps for w in ws]
    segment = lambda i, k_ref: (2 * k_ref[0] + i // per, i % per, 0)
    whole = lambda shape: pl.BlockSpec(shape, lambda i, k_ref: (0,) * len(shape))
    prepared = []
    if small_maps is not None:
        (g, ck, _), (h, hd, _) = maps[0].shape, maps[1].shape
        prepared = [(g, ck, ck), (g, ck, ck), (h, hd, 2 * hd), (h, 2 * hd, hd)]
    out = _pcall(
        body,
        hook,
        name=f"cast_{ws[0].shape[0]}x{ws[0].shape[1]}",
        grid_spec=pltpu.PrefetchScalarGridSpec(
            num_scalar_prefetch=1,
            grid=(steps,),
            in_specs=[pl.BlockSpec((r, w.shape[1]), lambda i, k_ref: (i, 0)) for w, r in zip(ws, rows)]
            + [whole(m.shape) for m in maps],
            out_specs=[pl.BlockSpec((None, r, w.shape[1]), segment) for w, r in zip(ws, rows)]
            + [whole(shape) for shape in prepared],
        ),
        out_shape=[jax.ShapeDtypeStruct((N_DEV, w.shape[0] // 2, w.shape[1]), BF16) for w in ws]
        + [jax.ShapeDtypeStruct(shape, BF16) for shape in prepared],
        compiler_params=_cparams(("arbitrary",)),
    )(_scalars(mine), *ws, *maps)
    return out


def _prepare_small_maps(ws_ref, ga_ref, gx_ref, wc_ref, wct_ref, gab_ref, gabt_ref):
    ck, hd = ws_ref.shape[1], ga_ref.shape[1]
    tril = lax.broadcasted_iota(jnp.int32, (ck, ck), 0) >= lax.broadcasted_iota(jnp.int32, (ck, ck), 1)
    for g in range(ws_ref.shape[0]):
        w = ws_ref[g] * tril.astype(F32)
        wc_ref[g] = w.astype(BF16)
        wct_ref[g] = w.T.astype(BF16)
    for h in range(ga_ref.shape[0]):
        for k, m_ref in enumerate((ga_ref, gx_ref)):
            m = m_ref[h]
            gab_ref[h, :, k * hd : (k + 1) * hd] = m.astype(BF16)
            gabt_ref[h, k * hd : (k + 1) * hd, :] = m.T.astype(BF16)


def _place():
    x, y, c = lax.axis_index("x"), lax.axis_index("y"), lax.axis_index("c")
    chips = [(1 - x, y), (x, 1 - y), (1 - x, 1 - y)]
    return x, y, c, chips


def _chip_no(chip):
    return 2 * chip[0] + chip[1]


def _rcopy(src, dst, send_sem, recv_sem, to):
    return pltpu.make_async_remote_copy(
        src_ref=src, dst_ref=dst, send_sem=send_sem, recv_sem=recv_sem, device_id=to, device_id_type=MESH
    )


def _gather_hook(big, small=None):
    nb = len(big)
    n_sems = 6 * nb + 4

    def places():
        x, y, c, chips = _place()
        first = (x ^ (1 - c), y ^ c)
        second = (x ^ c, y ^ (1 - c))
        return x, y, c, chips, first, second, (1 - x, 1 - y)

    def seg(outs, b, chip, half):
        return outs[b].at[2 * _chip_no(chip) + half]

    def step1(outs, send, recv):
        x, y, c, _, first, _, _ = places()
        return [
            _rcopy(seg(outs, b, (x, y), c), seg(outs, b, (x, y), c), send.at[6 * b], recv.at[6 * b], (*first, c))
            for b in range(nb)
        ]

    def step2(outs, send, recv):
        x, y, c, _, first, second, _ = places()
        copies = []
        for b in range(nb):
            for k, chip in ((1, (x, y)), (2, first)):
                src = seg(outs, b, chip, c)
                copies.append(_rcopy(src, src, send.at[6 * b + k], recv.at[6 * b + k], (*second, c)))
        return copies

    def hand_over(outs, send, recv, k, chip):
        x, y, c, *_ = places()
        return [
            _rcopy(seg(outs, b, chip, c), seg(outs, b, chip, c), send.at[6 * b + k], recv.at[6 * b + k], (x, y, 1 - c))
            for b in range(nb)
        ]

    def wait_landed(outs, send, recv, k, chip, half):
        x, y, c, *_ = places()
        for b in range(nb):
            got = seg(outs, b, chip, half)
            _rcopy(got, got, send.at[6 * b + k], recv.at[6 * b + k], (x, y, c)).wait_recv()

    def small_copies(ins, outs, send, recv):
        x, y, c, chips, *_ = places()
        there = outs[nb].at[_chip_no((x, y))]
        return [
            _rcopy(ins[nb], there, send.at[6 * nb + j], recv.at[6 * nb + j], (*chip, c)) for j, chip in enumerate(chips)
        ]

    def local_copy(ins, outs, send):
        x, y, _, _ = _place()
        return pltpu.make_async_copy(ins[nb], outs[nb].at[_chip_no((x, y))], send.at[6 * nb + 3])

    def start(ins, outs, send, recv):
        for cp in step1(outs, send, recv):
            cp.start()
        if small is not None:
            for cp in small_copies(ins, outs, send, recv):
                cp.start()
            local_copy(ins, outs, send).start()

    def middle(ins, outs, send, recv):
        *_, first, _, _ = places()
        wait_landed(outs, send, recv, 0, first, places()[2])
        for cp in step2(outs, send, recv) + hand_over(outs, send, recv, 3, first):
            cp.start()

    def late(ins, outs, send, recv):
        x, y, c, chips, first, second, diagonal = places()
        for k, chip in ((1, second), (2, diagonal)):
            wait_landed(outs, send, recv, k, chip, c)
            for cp in hand_over(outs, send, recv, 3 + k, chip):
                cp.start()

    def finish(ins, outs, send, recv):
        x, y, c, chips, first, second, diagonal = places()
        wait_landed(outs, send, recv, 3, second, 1 - c)
        wait_landed(outs, send, recv, 4, first, 1 - c)
        wait_landed(outs, send, recv, 5, diagonal, 1 - c)
        sent = step1(outs, send, recv) + step2(outs, send, recv)
        for k, chip in ((3, first), (4, second), (5, diagonal)):
            sent += hand_over(outs, send, recv, k, chip)
        for cp in sent:
            cp.wait_send()
        if small is not None:
            for j, chip in enumerate(chips):
                got = outs[nb].at[_chip_no(chip)]
                _rcopy(got, got, send.at[6 * nb + j], recv.at[6 * nb + j], (x, y, c)).wait_recv()
            for cp in small_copies(ins, outs, send, recv):
                cp.wait_send()
            local_copy(ins, outs, send).wait()

    operands = list(big) + ([small] if small is not None else [])
    out_shapes = [jax.ShapeDtypeStruct(b.shape, b.dtype) for b in big]
    if small is not None:
        out_shapes.append(jax.ShapeDtypeStruct((N_CHIPS, *small.shape), small.dtype))
    return _Hook(operands, out_shapes, {b: b for b in range(nb)}, n_sems, start, finish, middle, late)


def _both_ways_hook(operands, out_shapes, copies_of, n_sems):
    def start(ins, outs, send, recv):
        for cp in copies_of(ins, outs, send, recv):
            cp.start()

    def finish(ins, outs, send, recv):
        for cp in copies_of(ins, outs, send, recv):
            cp.wait()

    return _Hook(operands, out_shapes, {}, n_sems, start, finish)


def _swap_hook(bufs):
    def copies_of(ins, outs, send, recv):
        x, y, c, _ = _place()
        copies = []
        for b in range(len(bufs)):
            for j in range(N_CHIPS):
                k = b * N_CHIPS + j
                copies.append(_rcopy(ins[b].at[2 * j + 1 - c], outs[b].at[j], send.at[k], recv.at[k], (x, y, 1 - c)))
        return copies

    out_shapes = [jax.ShapeDtypeStruct((N_CHIPS, *b.shape[1:]), b.dtype) for b in bufs]
    return _both_ways_hook(list(bufs), out_shapes, copies_of, len(bufs) * N_CHIPS)


def _axis_order():
    x, y, c, _ = _place()
    return (x, y), c, (x ^ (1 - c), y ^ c), (x ^ c, y ^ (1 - c)), (1 - x, 1 - y)


def _send_first_hook(parts):
    def copies_of(ins, outs, send, recv):
        _, c, first, _, _ = _axis_order()
        copies = []
        for b in range(len(parts)):
            for k in range(2):
                sem = 2 * b + k
                copies.append(_rcopy(ins[b].at[k], outs[b].at[k], send.at[sem], recv.at[sem], (*first, c)))
        return copies

    out_shapes = [jax.ShapeDtypeStruct((2, *p.shape[1:]), p.dtype) for p in parts]
    return _both_ways_hook(list(parts), out_shapes, copies_of, len(parts) * 2)


def _send_second_hook(mids):
    def copies_of(ins, outs, send, recv):
        _, c, _, second, _ = _axis_order()
        return [_rcopy(ins[b], outs[b], send.at[b], recv.at[b], (*second, c)) for b in range(len(mids))]

    out_shapes = [jax.ShapeDtypeStruct(m.shape, m.dtype) for m in mids]
    return _both_ways_hook(list(mids), out_shapes, copies_of, len(mids))


def _share_hook(big, small=None, tiny=None):
    nb = len(big)
    n_sems = nb + 7 + N_DEV
    t0 = nb + 7

    def tiny_copies(ins, outs, send, recv):
        x, y, c, _ = _place()
        there = outs[-1].at[2 * _chip_no((x, y)) + c]
        copies = []
        for r in range(1, N_DEV):
            to = (x ^ (r >> 2 & 1), y ^ (r >> 1 & 1), c ^ (r & 1))
            copies.append(_rcopy(ins[-1], there, send.at[t0 + r], recv.at[t0 + r], to))
        return copies

    def tiny_local(ins, outs, send):
        x, y, c, _ = _place()
        return pltpu.make_async_copy(ins[-1], outs[-1].at[2 * _chip_no((x, y)) + c], send.at[t0])

    def first_copies(outs, send, recv):
        x, y, c, chips = _place()
        sibling = (x, y, 1 - c)
        copies = [_rcopy(outs[b].at[c], outs[b].at[c], send.at[b], recv.at[b], sibling) for b in range(nb)]
        if small is not None:
            own = outs[nb].at[2 * _chip_no((x, y)) + c]
            copies.append(_rcopy(own, own, send.at[nb], recv.at[nb], sibling))
            for j, chip in enumerate(chips):
                copies.append(_rcopy(own, own, send.at[nb + 1 + j], recv.at[nb + 1 + j], (*chip, c)))
        return copies

    def start(ins, outs, send, recv):
        for cp in first_copies(outs, send, recv):
            cp.start()
        if tiny is not None:
            for cp in tiny_copies(ins, outs, send, recv):
                cp.start()
            tiny_local(ins, outs, send).start()

    def finish(ins, outs, send, recv):
        x, y, c, chips = _place()
        me, sibling = (x, y, c), (x, y, 1 - c)
        if tiny is not None:
            for cp in tiny_copies(ins, outs, send, recv):
                cp.wait()
            tiny_local(ins, outs, send).wait()
        passed = []
        if small is not None:
            for j, chip in enumerate(chips):
                got = outs[nb].at[2 * _chip_no(chip) + c]
                _rcopy(got, got, send.at[nb + 1 + j], recv.at[nb + 1 + j], me).wait_recv()
                fwd = _rcopy(got, got, send.at[nb + 4 + j], recv.at[nb + 4 + j], sibling)
                fwd.start()
                passed.append(fwd)
        for b in range(nb):
            got = outs[b].at[1 - c]
            _rcopy(got, got, send.at[b], recv.at[b], me).wait_recv()
        if small is not None:
            got = outs[nb].at[2 * _chip_no((x, y)) + 1 - c]
            _rcopy(got, got, send.at[nb], recv.at[nb], me).wait_recv()
            for j, chip in enumerate(chips):
                got = outs[nb].at[2 * _chip_no(chip) + 1 - c]
                _rcopy(got, got, send.at[nb + 4 + j], recv.at[nb + 4 + j], me).wait_recv()
        for cp in first_copies(outs, send, recv) + passed:
            cp.wait_send()

    operands = list(big) + ([small] if small is not None else [])
    out_shapes = [jax.ShapeDtypeStruct(a.shape, a.dtype) for a in operands]
    aliases = {i: i for i in range(len(operands))}
    if tiny is not None:
        operands.append(tiny)
        out_shapes.append(jax.ShapeDtypeStruct((N_DEV, *tiny.shape), tiny.dtype))
    return _Hook(operands, out_shapes, aliases, n_sems, start, finish)


def _row_tile(rows, cols, target_bytes=2 * 1024 * 1024):
    best = SUBLANES
    for t in range(SUBLANES, rows + 1, SUBLANES):
        if rows % t == 0 and t * cols * 4 <= target_bytes:
            best = t
    return best


def _halves(buf):
    return buf.reshape(N_CHIPS, 2, *buf.shape[1:])


def _add_own_half(sets, owners, c):
    _, rows, cols = sets[0][0].shape
    tr = _row_tile(rows, cols)
    n = len(sets)

    def body(s_ref, *refs):
        def add(k):
            a_ref, b_ref, o_ref = refs[2 * k], refs[2 * k + 1], refs[2 * n + k]
            o_ref[...] = (a_ref[...] + b_ref[...].astype(F32)).astype(sets[k][2])

        add(0)
        for k in range(1, n):
            pl.when(pl.program_id(1) == 0)(lambda k=k: add(k))

    in_specs, out_specs, out_shape, operands = [], [], [], []
    for k, (buf, got, wire) in enumerate(sets):
        r_k, c_k = (tr, cols) if k == 0 else buf.shape[1:]
        row = (lambda r: r) if k == 0 else (lambda r: 0)
        in_specs += [
            pl.BlockSpec((None, None, r_k, c_k), lambda j, r, s_ref, row=row: (s_ref[j], s_ref[2], row(r), 0)),
            pl.BlockSpec((None, r_k, c_k), lambda j, r, s_ref, row=row: (s_ref[j], row(r), 0)),
        ]
        out_specs.append(pl.BlockSpec((None, r_k, c_k), lambda j, r, s_ref, row=row: (j, row(r), 0)))
        out_shape.append(jax.ShapeDtypeStruct((2, *buf.shape[1:]), wire))
        operands += [_halves(buf), got]
    return _pcall(
        body,
        name=f"add_own_half_{rows}x{cols}",
        grid_spec=pltpu.PrefetchScalarGridSpec(
            num_scalar_prefetch=1, grid=(2, rows // tr), in_specs=in_specs, out_specs=out_specs
        ),
        out_shape=out_shape,
        compiler_params=_cparams(("arbitrary", "arbitrary")),
    )(_scalars(owners[0], owners[1], c), *operands)


def _add_for_neighbour(sets, second, c):
    _, rows, cols = sets[0][0].shape
    tr = _row_tile(rows, cols)
    n = len(sets)

    def body(s_ref, *refs):
        def add(k):
            x_ref, a_ref, g_ref, o_ref = refs[3 * k], refs[3 * k + 1], refs[3 * k + 2], refs[3 * n + k]
            o_ref[...] = ((x_ref[...] + a_ref[...].astype(F32)) + g_ref[...].astype(F32)).astype(sets[k][3])

        add(0)
        for k in range(1, n):
            pl.when(pl.program_id(0) == 0)(lambda k=k: add(k))

    in_specs, out_specs, out_shape, operands = [], [], [], []
    for k, (buf, got_a, got1, wire) in enumerate(sets):
        r_k, c_k = (tr, cols) if k == 0 else buf.shape[1:]
        row = (lambda r: r) if k == 0 else (lambda r: 0)
        in_specs += [
            pl.BlockSpec((None, None, r_k, c_k), lambda r, s_ref, row=row: (s_ref[0], s_ref[1], row(r), 0)),
            pl.BlockSpec((None, r_k, c_k), lambda r, s_ref, row=row: (s_ref[0], row(r), 0)),
            pl.BlockSpec((None, r_k, c_k), lambda r, s_ref, row=row: (1, row(r), 0)),
        ]
        out_specs.append(pl.BlockSpec((r_k, c_k), lambda r, s_ref, row=row: (row(r), 0)))
        out_shape.append(jax.ShapeDtypeStruct(buf.shape[1:], wire))
        operands += [_halves(buf), got_a, got1]
    return _pcall(
        body,
        name=f"add_for_neighbour_{rows}x{cols}",
        grid_spec=pltpu.PrefetchScalarGridSpec(
            num_scalar_prefetch=1, grid=(rows // tr,), in_specs=in_specs, out_specs=out_specs
        ),
        out_shape=out_shape,
        compiler_params=_cparams(("arbitrary",)),
    )(_scalars(second, c), *operands)


def _add_received(sets, mine, c):
    _, rows, cols = sets[0][0].shape
    tr = _row_tile(rows, cols)
    n = len(sets)

    def body(s_ref, *refs):
        def add(k):
            x_ref, a_ref, g1_ref, g2_ref, o_ref = *refs[4 * k : 4 * k + 4], refs[4 * n + k]
            own = x_ref[...] + a_ref[...].astype(F32)
            o_ref[...] = (own + g1_ref[...].astype(F32)) + g2_ref[...].astype(F32)

        add(0)
        for k in range(1, n):
            pl.when(pl.program_id(0) == 0)(lambda k=k: add(k))

    in_specs, out_specs, out_shape, operands = [], [], [], []
    for k, (buf, got_a, got1, got2, slot, n_slots) in enumerate(sets):
        r_k, c_k = (tr, cols) if k == 0 else buf.shape[1:]
        row = (lambda r: r) if k == 0 else (lambda r: 0)
        in_specs += [
            pl.BlockSpec((None, None, r_k, c_k), lambda r, s_ref, row=row: (s_ref[0], s_ref[1], row(r), 0)),
            pl.BlockSpec((None, r_k, c_k), lambda r, s_ref, row=row: (s_ref[0], row(r), 0)),
            pl.BlockSpec((None, r_k, c_k), lambda r, s_ref, row=row: (0, row(r), 0)),
            pl.BlockSpec((r_k, c_k), lambda r, s_ref, row=row: (row(r), 0)),
        ]
        out_specs.append(pl.BlockSpec((None, r_k, c_k), lambda r, s_ref, row=row, k=k: (s_ref[2 + k], row(r), 0)))
        out_shape.append(jax.ShapeDtypeStruct((n_slots, *buf.shape[1:]), F32))
        operands += [_halves(buf), got_a, got1, got2]
    return _pcall(
        body,
        name=f"add_received_{rows}x{cols}",
        grid_spec=pltpu.PrefetchScalarGridSpec(
            num_scalar_prefetch=1, grid=(rows // tr,), in_specs=in_specs, out_specs=out_specs
        ),
        out_shape=out_shape,
        compiler_params=_cparams(("arbitrary",)),
    )(_scalars(mine, c, *[s[4] for s in sets]), *operands)


def _layer_a_fwd(x, nw, win, ln_w, ln_b, wc, bs_t, wout, tm, hook):
    t_rows, d = x.shape
    n_sh, _, s_cols = win.shape
    aw = wout.shape[0]
    gd = aw // A_GROUPS
    tn = 512
    assert s_cols % tn == 0 and aw % tn == 0 and tm % CHUNK == 0

    def body(x_ref, nw_ref, win_ref, lnw_ref, lnb_ref, wc_ref, bst_ref, wout_ref, z_ref, x1_ref, h_ref, u_s, v_s, y_s):
        x = x_ref[...]
        h, _, _ = _rms_fwd(x, nw_ref[...])
        h = h.astype(BF16)
        h_ref[...] = h
        for j in range(3 * aw // tn):
            k, off = divmod(j * tn, s_cols)
            cols = slice((j * tn) % aw, (j * tn) % aw + tn)
            zj = _dot(h, win_ref[k, :, off : off + tn])
            z_ref[:, j * tn : (j + 1) * tn] = zj
            if j * tn < aw:
                u_s[:, cols] = _gelu(zj)
            elif j * tn < 2 * aw:
                v_s[:, cols] = _gelu(zj)
            else:
                u_s[:, cols] = u_s[:, cols] * (zj * _sigmoid(zj))
        v = v_s[...]
        mu = jnp.mean(v, axis=-1, keepdims=True)
        vc = v - mu
        rstd = lax.rsqrt(jnp.mean(vc * vc, axis=-1, keepdims=True) + LN_EPS)
        v_s[...] = (vc * rstd) * lnw_ref[...] + lnb_ref[...]
        for ck in range(tm // CHUNK):
            rows = slice(ck * CHUNK, (ck + 1) * CHUNK)
            for g in range(A_GROUPS):
                cols = slice(g * gd, (g + 1) * gd)
                s = _dot(wc_ref[g], v_s[rows, cols].astype(BF16)) + bst_ref[:, g : g + 1]
                y_s[rows, cols] = (u_s[rows, cols] * s).astype(BF16)
        x1_ref[...] = x + _dot(y_s[...], wout_ref[...])

    row = lambda i: (i, 0)
    return _pcall(
        body,
        hook,
        name="layer_a_fwd",
        grid=(t_rows // tm,),
        in_specs=[
            pl.BlockSpec((tm, d), row),
            _full(nw.shape),
            _full(win.shape),
            _full(ln_w.shape),
            _full(ln_b.shape),
            _full(wc.shape),
            _full(bs_t.shape),
            _full(wout.shape),
        ],
        out_specs=[pl.BlockSpec((tm, 3 * aw), row), pl.BlockSpec((tm, d), row), pl.BlockSpec((tm, d), row)],
        out_shape=[
            jax.ShapeDtypeStruct((t_rows, 3 * aw), F32),
            jax.ShapeDtypeStruct((t_rows, d), F32),
            jax.ShapeDtypeStruct((t_rows, d), BF16),
        ],
        scratch_shapes=[pltpu.VMEM((tm, aw), F32), pltpu.VMEM((tm, aw), F32), pltpu.VMEM((tm, aw), BF16)],
        compiler_params=_cparams(("arbitrary",)),
    )(x, nw, win, ln_w, ln_b, wc, bs_t, wout)


def _layer_a_bwd(dout, z, ln_w, ln_b, wc, wct, bs_t, wout, tiles, earlier, hook):
    t_rows, d = dout.shape
    aw = wout.shape[0]
    gd = aw // A_GROUPS
    tm = TM_A_BWD
    lo, hi = tiles
    n_earlier = 0 if earlier is None else len(earlier)

    def body(dout_ref, z_ref, lnw_ref, lnb_ref, wc_ref, wct_ref, bst_ref, wout_ref, *rest):
        dz_ref, y_ref, dob_ref, gws_ref, gbs_ref, glnw_ref, glnb_ref, u_s, vh_s, ds_s, dvn_s = rest[n_earlier:]

        @pl.when(pl.program_id(0) == 0)
        def _():
            gws_ref[...] = jnp.zeros_like(gws_ref)
            gbs_ref[...] = jnp.zeros_like(gbs_ref)
            glnw_ref[...] = jnp.zeros_like(glnw_ref)
            glnb_ref[...] = jnp.zeros_like(glnb_ref)

        dob = dout_ref[...].astype(BF16)
        dob_ref[...] = dob
        dy = _dot_nt(dob, wout_ref[...])

        zv = z_ref[:, aw : 2 * aw]
        vg, dvg_dz = _gelu_and_grad(zv)
        mu = jnp.mean(vg, axis=-1, keepdims=True)
        vc = vg - mu
        rstd = lax.rsqrt(jnp.mean(vc * vc, axis=-1, keepdims=True) + LN_EPS)
        vh = vc * rstd
        vh_s[...] = vh
        vn = (vh * lnw_ref[...] + lnb_ref[...]).astype(BF16)

        zu = z_ref[:, 0:aw]
        zg = z_ref[:, 2 * aw : 3 * aw]
        u, du_dz = _gelu_and_grad(zu)
        sg, dsg = _silu_and_grad(zg)
        u_s[...] = u * sg
        tril = lax.broadcasted_iota(jnp.int32, (CHUNK, CHUNK), 0) >= lax.broadcasted_iota(jnp.int32, (CHUNK, CHUNK), 1)
        for ck in range(tm // CHUNK):
            rows = slice(ck * CHUNK, (ck + 1) * CHUNK)
            for g in range(A_GROUPS):
                cols = slice(g * gd, (g + 1) * gd)
                vn_g = vn[rows, cols]
                s = _dot(wc_ref[g], vn_g) + bst_ref[:, g : g + 1]
                usg = u_s[rows, cols]
                dy_g = dy[rows, cols]
                y_ref[rows, cols] = (usg * s).astype(BF16)
                ds = dy_g * usg
                ds_s[rows, cols] = dy_g * s
                gbs_ref[:, g : g + 1] += jnp.sum(ds, axis=-1, keepdims=True)
                dsb = ds.astype(BF16)
                gws_ref[g] += jnp.where(tril, _dot_nt(dsb, vn_g), 0.0)
                dvn_s[rows, cols] = _dot(wct_ref[g], dsb)
        dusg = ds_s[...]
        dz_ref[:, 0:aw] = (dusg * sg * du_dz).astype(BF16)
        dz_ref[:, 2 * aw : 3 * aw] = (dusg * u * dsg).astype(BF16)

        dvn = dvn_s[...]
        vh = vh_s[...]
        glnw_ref[...] += jnp.sum(dvn * vh, axis=0, keepdims=True)
        glnb_ref[...] += jnp.sum(dvn, axis=0, keepdims=True)
        dvh = dvn * lnw_ref[...]
        dvg = rstd * (dvh - jnp.mean(dvh, axis=-1, keepdims=True) - vh * jnp.mean(dvh * vh, axis=-1, keepdims=True))
        dz_ref[:, aw : 2 * aw] = (dvg * dvg_dz).astype(BF16)

    row = lambda i: (i + lo, 0)
    call = _pcall(
        body,
        hook,
        name=f"layer_a_bwd_{lo}",
        grid=(hi - lo,),
        in_specs=[
            pl.BlockSpec((tm, d), row),
            pl.BlockSpec((tm, 3 * aw), row),
            _full(ln_w.shape),
            _full(ln_b.shape),
            _full(wc.shape),
            _full(wct.shape),
            _full(bs_t.shape),
            _full(wout.shape),
        ]
        + [ANY] * n_earlier,
        out_specs=[
            pl.BlockSpec((tm, 3 * aw), row),
            pl.BlockSpec((tm, aw), row),
            pl.BlockSpec((tm, d), row),
            _full((A_GROUPS, CHUNK, CHUNK)),
            _full((CHUNK, A_GROUPS)),
            _full((1, aw)),
            _full((1, aw)),
        ],
        out_shape=[
            jax.ShapeDtypeStruct((t_rows, 3 * aw), BF16),
            jax.ShapeDtypeStruct((t_rows, aw), BF16),
            jax.ShapeDtypeStruct((t_rows, d), BF16),
            jax.ShapeDtypeStruct((A_GROUPS, CHUNK, CHUNK), F32),
            jax.ShapeDtypeStruct((CHUNK, A_GROUPS), F32),
            jax.ShapeDtypeStruct((1, aw), F32),
            jax.ShapeDtypeStruct((1, aw), F32),
        ],
        scratch_shapes=[pltpu.VMEM((tm, aw), F32)] * 4,
        input_output_aliases={8 + i: i for i in range(n_earlier)},
        compiler_params=_cparams(("arbitrary",)),
    )
    return call(dout, z, ln_w, ln_b, wc, wct, bs_t, wout, *(earlier or ()))


def _layer_a_bwd_dx(dout, x, dz, nw, win, tm, tiles, earlier, hook):
    t_rows, d = x.shape
    n_sh, _, s_cols = win.shape
    lo, hi = tiles
    n_earlier = 0 if earlier is None else 1

    def body(dout_ref, x_ref, dz_ref, nw_ref, win_ref, *rest):
        gx_ref, gnw_ref = rest[n_earlier:]

        @pl.when(pl.program_id(0) == 0)
        def _():
            gnw_ref[...] = jnp.zeros_like(gnw_ref)

        dh = jnp.zeros((tm, d), F32)
        for k in range(n_sh):
            dh = dh + _dot_nt(dz_ref[:, k * s_cols : (k + 1) * s_cols], win_ref[k])
        nw = nw_ref[...]
        _, xh, r = _rms_fwd(x_ref[...], nw)
        dx, gnw = _rms_bwd(dh, xh, r, nw)
        gnw_ref[0:1, :] += gnw
        gx_ref[...] = dout_ref[...] + dx

    row = lambda i: (i + lo, 0)
    return _pcall(
        body,
        hook,
        name=f"layer_a_bwd_dx_{lo}",
        grid=(hi - lo,),
        in_specs=[
            pl.BlockSpec((tm, d), row),
            pl.BlockSpec((tm, d), row),
            pl.BlockSpec((tm, n_sh * s_cols), row),
            _full(nw.shape),
            _full(win.shape),
        ]
        + [ANY] * n_earlier,
        out_specs=[pl.BlockSpec((tm, d), row), _full((SUBLANES, d))],
        out_shape=[jax.ShapeDtypeStruct((t_rows, d), F32), jax.ShapeDtypeStruct((SUBLANES, d), F32)],
        input_output_aliases={5: 0} if n_earlier else {},
        compiler_params=_cparams(("arbitrary",)),
    )(dout, x, dz, nw, win, *([earlier] if n_earlier else []))


def _decay(r, sp_h):
    log_a = (-RG_C) * r * sp_h
    a = jnp.exp(log_a)
    mult = jnp.sqrt(jnp.tanh(-log_a) * (a * a + 1.0))
    return a, mult


ROW_CONV_B, ROW_GATE_A_B, ROW_GATE_X_B, ROW_LAMBDA = range(CONV_WIDTH, CONV_WIDTH + 4)


def _gates(xc_h, gab_ref, vec_ref, sp_h, h, hd):
    pre = _dot(xc_h.astype(BF16), gab_ref[h])
    cols = slice(h * hd, (h + 1) * hd)
    r = _sigmoid(pre[:, :hd] + vec_ref[ROW_GATE_A_B : ROW_GATE_A_B + 1, cols])
    ig = _sigmoid(pre[:, hd:] + vec_ref[ROW_GATE_X_B : ROW_GATE_X_B + 1, cols])
    a, mult = _decay(r, sp_h)
    return r, ig, a, mult


def _conv(xb, halo, vec_ref):
    xc = vec_ref[ROW_CONV_B : ROW_CONV_B + 1, :] + vec_ref[CONV_WIDTH - 1 : CONV_WIDTH, :] * xb
    for k in range(CONV_WIDTH - 1):
        xc = xc + vec_ref[k : k + 1, :] * _shift_down(xb, halo, CONV_WIDTH - 1 - k)
    return xc


def _layer_b_fwd(x1, nw, bin_w, vec, gab, bout, nf, tgt, tm):
    t_rows, d = x1.shape
    bw = bout.shape[0]
    hd = bw // B_HEADS
    nt = t_rows // tm

    def body(
        x1_ref, nw_ref, bin_ref, vec_ref, gab_ref, bout_ref, nf_ref, tgt_ref,
        z_ref, h_ref, h1_ref, dx2_ref, loss_ref, gnf_ref,
        tail_s, carry_s, a_s, b_s, hs_s, acc_s,
    ):
        @pl.when(pl.program_id(0) == 0)
        def _():
            tail_s[...] = jnp.zeros_like(tail_s)
            carry_s[...] = jnp.zeros_like(carry_s)
            acc_s[...] = jnp.zeros_like(acc_s)
            gnf_ref[...] = jnp.zeros_like(gnf_ref)

        x1 = x1_ref[...]
        h1, _, _ = _rms_fwd(x1, nw_ref[...])
        h1 = h1.astype(BF16)
        h1_ref[...] = h1
        z = jnp.concatenate([_dot(h1, bin_ref[k]) for k in range(N_CHIPS)], axis=1)
        z_ref[...] = z
        xb = z[:, :bw]
        xc = _conv(xb, tail_s[...], vec_ref)
        tail = xb[tm - SUBLANES :, :]
        tail_s[...] = tail
        sp = _softplus_neg(vec_ref[ROW_LAMBDA : ROW_LAMBDA + 1, :])
        for h in range(B_HEADS):
            cols = slice(h * hd, (h + 1) * hd)
            xc_h = xc[:, cols]
            _, ig, a, mult = _gates(xc_h, gab_ref, vec_ref, sp[:, cols], h, hd)
            a_s[:, cols] = a
            b_s[:, cols] = mult * (ig * xc_h)
        carry = _scan_blocks(a_s, b_s, hs_s, carry_s[...], tm, reverse=False)
        carry_s[...] = carry
        hs = hs_s[...]
        h_ref[...] = hs
        g = z[:, bw:]
        y = (hs * (g * _sigmoid(g))).astype(BF16)
        x2 = x1 + _dot(y, bout_ref[...])

        nf = nf_ref[...]
        o, xh, r = _rms_fwd(x2, nf)
        diff = o - tgt_ref[...]
        acc_s[...] += jnp.sum(diff * diff, axis=0, keepdims=True)
        do = diff * (1.0 / d)
        dx2, gnf = _rms_bwd(do, xh, r, nf)
        gnf_ref[...] += gnf
        dx2_ref[...] = dx2

        @pl.when(pl.program_id(0) == nt - 1)
        def _():
            total = jnp.sum(acc_s[...], axis=-1, keepdims=True) * (0.5 / d)
            loss_ref[...] = jnp.broadcast_to(total, loss_ref.shape)

    row = lambda i: (i, 0)
    return _pcall(
        body,
        name="layer_b_fwd",
        grid=(nt,),
        in_specs=[
            pl.BlockSpec((tm, d), row),
            _full(nw.shape),
            _full(bin_w.shape),
            _full(vec.shape),
            _full(gab.shape),
            _full(bout.shape),
            _full(nf.shape),
            pl.BlockSpec((tm, d), row),
        ],
        out_specs=[
            pl.BlockSpec((tm, 2 * bw), row),
            pl.BlockSpec((tm, bw), row),
            pl.BlockSpec((tm, d), row),
            pl.BlockSpec((tm, d), row),
            _full((1, LANES)),
            _full((1, d)),
        ],
        out_shape=[
            jax.ShapeDtypeStruct((t_rows, 2 * bw), F32),
            jax.ShapeDtypeStruct((t_rows, bw), F32),
            jax.ShapeDtypeStruct((t_rows, d), BF16),
            jax.ShapeDtypeStruct((t_rows, d), F32),
            jax.ShapeDtypeStruct((1, LANES), F32),
            jax.ShapeDtypeStruct((1, d), F32),
        ],
        scratch_shapes=[
            pltpu.VMEM((SUBLANES, bw), F32),
            pltpu.VMEM((SUBLANES, bw), F32),
            pltpu.VMEM((tm, bw), F32),
            pltpu.VMEM((tm, bw), F32),
            pltpu.VMEM((tm, bw), F32),
            pltpu.VMEM((1, d), F32),
        ],
        compiler_params=_cparams(("arbitrary",)),
    )(x1, nw, bin_w, vec, gab, bout, nf, tgt)


def _layer_b_bwd(dout, x1, z, hseq, nw, bin_w, vec, gab, gabt, bout, tm):
    t_rows, d = x1.shape
    bw = bout.shape[0]
    hd = bw // B_HEADS
    nt = t_rows // tm

    def body(
        dout_ref, x1_ref, z_ref, h_ref, xbt_ref, ht_ref, nw_ref, bin_ref, vec_ref, gab_ref, gabt_ref, bout_ref,
        dx1_ref, dz_ref, y_ref, dob_ref, ggab_ref, ggb_ref, gcw_ref, gcb_ref, glam_ref, gnw_ref,
        gcarry_s, afirst_s, head_s, aup_s, dh_s, gt_s, dxc_s, xc_s, r_s, ig_s,
    ):
        step = pl.program_id(0)
        tile = nt - 1 - step

        @pl.when(step == 0)
        def _():
            for ref in (ggab_ref, ggb_ref, gcw_ref, gcb_ref, glam_ref, gnw_ref, gcarry_s, afirst_s, head_s):
                ref[...] = jnp.zeros_like(ref)

        first_tile = tile == 0
        xb_halo = jnp.where(first_tile, 0.0, xbt_ref[...])
        h_halo = jnp.where(first_tile, 0.0, ht_ref[...])

        dout = dout_ref[...]
        dob = dout.astype(BF16)
        dob_ref[...] = dob
        dy = _dot_nt(dob, bout_ref[...])
        hs = h_ref[...]
        g = z_ref[:, bw:]
        sg, dsg = _silu_and_grad(g)
        y_ref[...] = (hs * sg).astype(BF16)
        dz_ref[:, bw:] = (dy * hs * dsg).astype(BF16)
        dh_s[...] = dy * sg

        xb = z_ref[:, :bw]
        xc = _conv(xb, xb_halo, vec_ref)
        xc_s[...] = xc
        lam = vec_ref[ROW_LAMBDA : ROW_LAMBDA + 1, :]
        sp = _softplus_neg(lam)
        for h in range(B_HEADS):
            cols = slice(h * hd, (h + 1) * hd)
            r, ig, a, _ = _gates(xc[:, cols], gab_ref, vec_ref, sp[:, cols], h, hd)
            r_s[:, cols] = r
            ig_s[:, cols] = ig
            aup_s[:, cols] = _shift_up(a, afirst_s[:, cols], 1)
            afirst_s[:, cols] = jnp.broadcast_to(a[0:1, :], (SUBLANES, hd))
        carry = _scan_blocks(aup_s, dh_s, gt_s, gcarry_s[...], tm, reverse=True)
        gcarry_s[...] = carry

        h_prev = _shift_down(hs, h_halo, 1)
        for h in range(B_HEADS):
            cols = slice(h * hd, (h + 1) * hd)
            xc_h = xc_s[:, cols]
            sp_h = sp[:, cols]
            r, ig = r_s[:, cols], ig_s[:, cols]
            a, mult = _decay(r, sp_h)
            gt = gt_s[:, cols]
            da = gt * h_prev[:, cols]
            dmult = gt * (ig * xc_h)
            dig = gt * (mult * xc_h)
            dxc_direct = gt * (mult * ig)
            dla = da * a - dmult * (a * a) / mult
            glam_ref[:, cols] += jnp.sum(dla * r, axis=0, keepdims=True)
            dr = dla * ((-RG_C) * sp_h)
            dpre = jnp.concatenate([dr * r * (1.0 - r), dig * ig * (1.0 - ig)], axis=1)
            ggb_ref[:, cols] += jnp.sum(dpre[:, :hd], axis=0, keepdims=True)
            ggb_ref[:, bw + h * hd : bw + (h + 1) * hd] += jnp.sum(dpre[:, hd:], axis=0, keepdims=True)
            dpb = dpre.astype(BF16)
            ggab_ref[h] += _dot_tn(xc_h.astype(BF16), dpb)
            dxc_s[:, cols] = dxc_direct + _dot(dpb, gabt_ref[h])
        glam_ref[...] = jnp.where(step == nt - 1, glam_ref[...] * (RG_C * _sigmoid(-lam)), glam_ref[...])

        dxc = dxc_s[...]
        gcb_ref[...] += jnp.sum(dxc, axis=0, keepdims=True)
        dxb = vec_ref[CONV_WIDTH - 1 : CONV_WIDTH, :] * dxc
        gcw_ref[CONV_WIDTH - 1 : CONV_WIDTH, :] += jnp.sum(dxc * xb, axis=0, keepdims=True)
        head = head_s[...]
        for k in range(CONV_WIDTH - 1):
            lag = CONV_WIDTH - 1 - k
            dxb = dxb + vec_ref[k : k + 1, :] * _shift_up(dxc, head, lag)
            gcw_ref[k : k + 1, :] += jnp.sum(dxc * _shift_down(xb, xb_halo, lag), axis=0, keepdims=True)
        head_s[...] = dxc[:SUBLANES, :]
        dz_ref[:, :bw] = dxb.astype(BF16)

        s_cols = 2 * bw // N_CHIPS
        dh1 = jnp.zeros((tm, d), F32)
        for k in range(N_CHIPS):
            dh1 = dh1 + _dot_nt(dz_ref[:, k * s_cols : (k + 1) * s_cols], bin_ref[k])
        x1 = x1_ref[...]
        nw = nw_ref[...]
        _, xh, r1 = _rms_fwd(x1, nw)
        dx, gnw = _rms_bwd(dh1, xh, r1, nw)
        gnw_ref[...] += gnw
        dx1_ref[...] = dout + dx

    rev = lambda i: (nt - 1 - i, 0)
    prev = lambda i: (jnp.maximum((nt - 1 - i) * (tm // SUBLANES) - 1, 0), 0)
    return _pcall(
        body,
        name="layer_b_bwd",
        grid=(nt,),
        in_specs=[
            pl.BlockSpec((tm, d), rev),
            pl.BlockSpec((tm, d), rev),
            pl.BlockSpec((tm, 2 * bw), rev),
            pl.BlockSpec((tm, bw), rev),
            pl.BlockSpec((SUBLANES, bw), prev),
            pl.BlockSpec((SUBLANES, bw), prev),
            _full(nw.shape),
            _full(bin_w.shape),
            _full(vec.shape),
            _full(gab.shape),
            _full(gabt.shape),
            _full(bout.shape),
        ],
        out_specs=[
            pl.BlockSpec((tm, d), rev),
            pl.BlockSpec((tm, 2 * bw), rev),
            pl.BlockSpec((tm, bw), rev),
            pl.BlockSpec((tm, d), rev),
            _full((B_HEADS, hd, 2 * hd)),
            _full((1, 2 * bw)),
            _full((SUBLANES, bw)),
            _full((1, bw)),
            _full((1, bw)),
            _full((1, d)),
        ],
        out_shape=[
            jax.ShapeDtypeStruct((t_rows, d), F32),
            jax.ShapeDtypeStruct((t_rows, 2 * bw), BF16),
            jax.ShapeDtypeStruct((t_rows, bw), BF16),
            jax.ShapeDtypeStruct((t_rows, d), BF16),
            jax.ShapeDtypeStruct((B_HEADS, hd, 2 * hd), F32),
            jax.ShapeDtypeStruct((1, 2 * bw), F32),
            jax.ShapeDtypeStruct((SUBLANES, bw), F32),
            jax.ShapeDtypeStruct((1, bw), F32),
            jax.ShapeDtypeStruct((1, bw), F32),
            jax.ShapeDtypeStruct((1, d), F32),
        ],
        scratch_shapes=[pltpu.VMEM((SUBLANES, bw), F32)] * 3 + [pltpu.VMEM((tm, bw), F32)] * 7,
        compiler_params=_cparams(("arbitrary",)),
    )(dout, x1, z, hseq, z, hseq, nw, bin_w, vec, gab, gabt, bout)


def _wgrad(a, b, m_blocks, n_blocks, hook=None, wire_copy=False):
    k, m = a.shape
    n = b.shape[1]
    bm, bn = m // m_blocks, n // n_blocks

    def body(a_ref, b_ref, o_ref, *wire_ref):
        prod = _dot_tn(a_ref[...], b_ref[...])
        o_ref[...] = prod
        if wire_copy:
            wire_ref[0][...] = prod.astype(BF16)

    out_spec = pl.BlockSpec((None, None, bm, bn), lambda j, i: (j, i, 0, 0))
    shape = (n_blocks, m_blocks, bm, bn)
    out = _pcall(
        body,
        hook,
        name=f"wgrad_{m}x{n}",
        grid=(n_blocks, m_blocks),
        in_specs=[pl.BlockSpec((k, bm), lambda j, i: (0, i)), pl.BlockSpec((k, bn), lambda j, i: (0, j))],
        out_specs=[out_spec] * (1 + wire_copy),
        out_shape=[jax.ShapeDtypeStruct(shape, F32)] + [jax.ShapeDtypeStruct(shape, BF16)] * wire_copy,
        compiler_params=_cparams(("arbitrary", "arbitrary")),
    )(a, b)
    outs, rode = (out, None) if hook is None else out
    outs = outs if wire_copy else outs[0]
    return outs if hook is None else (outs, rode)


def _adamw_math(w, g, m, v):
    m = ADAM_B1 * m + (1.0 - ADAM_B1) * g
    v = ADAM_B2 * v + (1.0 - ADAM_B2) * (g * g)
    m_hat = m / (1.0 - ADAM_B1**ADAM_STEP)
    v_hat = v / (1.0 - ADAM_B2**ADAM_STEP)
    delta = -ADAM_LR * (m_hat / (jnp.sqrt(v_hat) + ADAM_EPS) + ADAM_WD * w)
    return delta, m, v


def _adamw(w, g, m, v, hook=None):
    rows, cols = w.shape
    tr = _row_tile(rows, cols, 1024 * 1024)

    def body(w_ref, g_ref, m_ref, v_ref, d_ref, mo_ref, vo_ref):
        d_ref[...], mo_ref[...], vo_ref[...] = _adamw_math(w_ref[...], g_ref[...], m_ref[...], v_ref[...])

    spec = pl.BlockSpec((tr, cols), lambda i: (i, 0))
    return _pcall(
        body,
        hook,
        name=f"adamw_{rows}x{cols}",
        grid=(rows // tr,),
        in_specs=[spec] * 4,
        out_specs=[spec] * 3,
        out_shape=[jax.ShapeDtypeStruct((rows, cols), F32)] * 3,
        compiler_params=_cparams(("arbitrary",)),
    )(w, g, m, v)


def _sum_partials(parts):
    def body(p_ref, o_ref):
        total = p_ref[0, 0:1, :]
        for k in range(1, N_DEV):
            total = total + p_ref[k, 0:1, :]
        o_ref[...] = total

    vmem = pl.BlockSpec(memory_space=pltpu.VMEM)
    return _pcall(
        body,
        name="sum_partials",
        in_specs=[vmem],
        out_specs=vmem,
        out_shape=jax.ShapeDtypeStruct((1, parts.shape[2]), F32),
    )(parts)


def _adamw_many(ws, gs, ms, vs):
    n = len(ws)

    def body(*refs):
        w_refs, g_refs, m_refs, v_refs = (refs[i * n : (i + 1) * n] for i in range(4))
        d_refs, mo_refs, vo_refs = (refs[(4 + i) * n : (5 + i) * n] for i in range(3))
        for i in range(n):
            d_refs[i][...], mo_refs[i][...], vo_refs[i][...] = _adamw_math(
                w_refs[i][...], g_refs[i][...], m_refs[i][...], v_refs[i][...]
            )

    vmem = pl.BlockSpec(memory_space=pltpu.VMEM)
    outs = _pcall(
        body,
        name="adamw_small",
        in_specs=[vmem] * (4 * n),
        out_specs=[vmem] * (3 * n),
        out_shape=[jax.ShapeDtypeStruct(w.shape, F32) for w in ws] * 3,
        compiler_params=_cparams(),
    )(*ws, *gs, *ms, *vs)
    return outs[:n], outs[n : 2 * n], outs[2 * n :]


def _pack_rows(parts, lanes=LANES):
    flat = jnp.concatenate([p.reshape(-1) for p in parts])
    per = N_DEV * SUBLANES * lanes
    total = -(-flat.shape[0] // per) * per
    flat = jnp.pad(flat, (0, total - flat.shape[0]))
    return flat.reshape(N_DEV, total // (N_DEV * lanes), lanes)


def _unpack(flat, shapes):
    out, at = [], 0
    for s in shapes:
        n = 1
        for dim in s:
            n *= dim
        out.append(flat[at : at + n].reshape(s))
        at += n
    return out


def kernel(x, norm_w, a_w_in, a_ln_w, a_ln_b, a_w_s, a_b_s, a_w_out, b_w_in, b_conv_w, b_conv_b, b_gate_a_w, b_gate_a_b, b_gate_x_w, b_gate_x_b, b_lambda, b_w_out, norm_f_w, loss_target, m_norm_w, m_a_w_in, m_a_ln_w, m_a_ln_b, m_a_w_s, m_a_b_s, m_a_w_out, m_b_w_in, m_b_conv_w, m_b_conv_b, m_b_gate_a_w, m_b_gate_a_b, m_b_gate_x_w, m_b_gate_x_b, m_b_lambda, m_b_w_out, m_norm_f_w, v_norm_w, v_a_w_in, v_a_ln_w, v_a_ln_b, v_a_w_s, v_a_b_s, v_a_w_out, v_b_w_in, v_b_conv_w, v_b_conv_b, v_b_gate_a_w, v_b_gate_a_b, v_b_gate_x_w, v_b_gate_x_b, v_b_lambda, v_b_w_out, v_norm_f_w):
    t_rows, d = x.shape[1], x.shape[2]
    aw = a_ln_w.shape[1]
    bw = b_gate_a_w.shape[1] * b_gate_a_w.shape[2]
    hd = bw // B_HEADS
    mine = 2 * lax.axis_index("x") + lax.axis_index("y")
    core = lax.axis_index("c")
    weights = dict(norm_w=norm_w, a_w_in=a_w_in, a_ln_w=a_ln_w, a_ln_b=a_ln_b, a_w_s=a_w_s, a_b_s=a_b_s, a_w_out=a_w_out, b_w_in=b_w_in, b_conv_w=b_conv_w, b_conv_b=b_conv_b, b_gate_a_w=b_gate_a_w, b_gate_a_b=b_gate_a_b, b_gate_x_w=b_gate_x_w, b_gate_x_b=b_gate_x_b, b_lambda=b_lambda, b_w_out=b_w_out, norm_f_w=norm_f_w)
    m_in = dict(norm_w=m_norm_w, a_w_in=m_a_w_in, a_ln_w=m_a_ln_w, a_ln_b=m_a_ln_b, a_w_s=m_a_w_s, a_b_s=m_a_b_s, a_w_out=m_a_w_out, b_w_in=m_b_w_in, b_conv_w=m_b_conv_w, b_conv_b=m_b_conv_b, b_gate_a_w=m_b_gate_a_w, b_gate_a_b=m_b_gate_a_b, b_gate_x_w=m_b_gate_x_w, b_gate_x_b=m_b_gate_x_b, b_lambda=m_b_lambda, b_w_out=m_b_w_out, norm_f_w=m_norm_f_w)
    v_in = dict(norm_w=v_norm_w, a_w_in=v_a_w_in, a_ln_w=v_a_ln_w, a_ln_b=v_a_ln_b, a_w_s=v_a_w_s, a_b_s=v_a_b_s, a_w_out=v_a_w_out, b_w_in=v_b_w_in, b_conv_w=v_b_conv_w, b_conv_b=v_b_conv_b, b_gate_a_w=v_b_gate_a_w, b_gate_a_b=v_b_gate_a_b, b_gate_x_w=v_b_gate_x_w, b_gate_x_b=v_b_gate_x_b, b_lambda=v_b_lambda, b_w_out=v_b_w_out, norm_f_w=v_norm_f_w)

    win_l, wout_l = _cast_to_segments([a_w_in[0], a_w_out[0]], mine, 4)
    small_l = jnp.concatenate([b_conv_w[0], b_conv_b, b_gate_a_b, b_gate_x_b, b_lambda], axis=0)
    (bin_l, bout_l, wc, wct, gab, gabt), (win_g, wout_g, small_g) = _cast_to_segments(
        [b_w_in[0], b_w_out[0]], mine, 8, _gather_hook([win_l, wout_l], small_l),
        (a_w_s[0], b_gate_a_w[0], b_gate_x_w[0]),
    )
    win = win_g.reshape(N_CHIPS, d, -1)
    wout = wout_g.reshape(aw, d)
    bs_t = a_b_s[0].T
    nw0, nw1, nf = norm_w[0:1], norm_w[1:2], norm_f_w.reshape(1, d)

    x0 = x[0]
    (z_a, x1, h0), (bin_g, bout_g) = _layer_a_fwd(
        x0, nw0, win, a_ln_w, a_ln_b, wc, bs_t, wout, TM_FWD, _gather_hook([bin_l, bout_l])
    )
    bin_w = bin_g.reshape(N_CHIPS, d, -1)
    bout = bout_g.reshape(bw, d)
    vec = jnp.transpose(small_g, (1, 0, 2)).reshape(SUBLANES, bw)
    z_b, hseq, h1, dx2, loss_l, g_nf = _layer_b_fwd(x1, nw1, bin_w, vec, gab, bout, nf, loss_target[0], TM_FWD)
    dx1, dz_b, y_b, dob_b, g_gab, g_gb, g_cw, g_cb, g_lam, g_nw1 = _layer_b_bwd(
        dx2, x1, z_b, hseq, nw1, bin_w, vec, gab, gabt, bout, TM_FWD
    )
    seg = lambda g: g.reshape(N_DEV, -1, g.shape[3])
    x_at, y_at = lax.axis_index("x"), lax.axis_index("y")
    first_no = 2 * (x_at ^ (1 - core)) + (y_at ^ core)
    second_no = 2 * (x_at ^ core) + (y_at ^ (1 - core))
    bf16s = lambda bufs: [BF16] * len(bufs)
    own_half = lambda bufs, got, wires: _add_own_half(
        list(zip(bufs, got, wires)), (first_no, N_CHIPS - 1 - mine), core
    )
    for_neighbour = lambda bufs, got_a, got1, wires: _add_for_neighbour(
        list(zip(bufs, got_a, got1, wires)), second_no, core
    )
    received = lambda bufs, got_a, got1, got2: _add_received(
        [(b, ga, g1, g2, core, 2) for b, ga, g1, g2 in zip(bufs, got_a, got1, got2)], mine, core
    )

    g_bout = [seg(_wgrad(y_b, dob_b, 2, 1))]
    g_bin, swap_o = _wgrad(h1, dz_b, 1, N_CHIPS, _swap_hook(g_bout))
    g_bin = [seg(g_bin)]
    part_o = own_half(g_bout, swap_o, bf16s(g_bout))
    a_args = (z_a, a_ln_w, a_ln_b, wc, wct, bs_t, wout)
    half = t_rows // TM_A_BWD // 2
    first, rode = _layer_a_bwd(
        dx1, *a_args, (0, half), None, _join_hooks(_swap_hook(g_bin), _send_first_hook(part_o))
    )
    swap_i, got1_o = rode[:1], rode[1:]
    part_i = own_half(g_bin, swap_i, bf16s(g_bin))
    mid_o = for_neighbour(g_bout, swap_o, got1_o, bf16s(g_bout))
    second, rode = _layer_a_bwd(
        dx1, *a_args, (half, 2 * half), first[:3], _join_hooks(_send_first_hook(part_i), _send_second_hook(mid_o))
    )
    got1_i, got2_o = rode[:1], rode[1:]
    dz_a, y_a, dob_a = second[:3]
    g_ws, g_bst, g_lnw, g_lnb = (p + q for p, q in zip(first[3:], second[3:]))
    mid_i = for_neighbour(g_bin, swap_i, got1_i, bf16s(g_bin))
    red_o = received(g_bout, swap_o, got1_o, got2_o)
    (g_win, g_win_wire), rode = _wgrad(
        h0, dz_a, 1, N_CHIPS, _join_hooks(_send_second_hook(mid_i), _share_hook(red_o)), wire_copy=True
    )
    g_win, g_win_wire = [seg(g_win)], seg(g_win_wire)
    got2_i, gr_bout = rode[:1], rode[1].reshape(b_w_out.shape[1:])
    red_i = received(g_bin, swap_i, got1_i, got2_i)

    small_shapes = [
        (1, d), (1, aw), (1, aw), (A_GROUPS, CHUNK, CHUNK), (A_GROUPS, CHUNK), (B_HEADS, hd, hd), (B_HEADS, hd, hd),
        (d,), (CONV_WIDTH, bw), (1, bw), (1, bw), (1, bw), (1, bw), (1, 1),
    ]
    small = _pack_rows(
        [
            g_nw1, g_lnw, g_lnb, g_ws, g_bst.T, g_gab[:, :, :hd], g_gab[:, :, hd:],
            g_nf, g_cw[:CONV_WIDTH], g_cb, g_gb[:, :bw], g_gb[:, bw:], g_lam, loss_l[:, :1],
        ]
    )
    g_w, wire_w = g_win + [small], [BF16, F32]
    g_wout, rode = _wgrad(
        y_a, dob_a, N_CHIPS, 1, _join_hooks(_swap_hook([g_win_wire, small]), _share_hook(red_i))
    )
    g_wout = seg(g_wout)
    swap_w, gr_bin = rode[:2], rode[2].reshape(b_w_in.shape[1:])

    g_u, wire_u = [g_wout], [BF16]
    part_w = own_half(g_w, swap_w, wire_w)
    (grad_x, g_nw0_mine), rode = _layer_a_bwd_dx(
        dx1, x0, dz_a, nw0, win, TM_A_DX, (0, t_rows // TM_A_DX), None,
        _join_hooks(_send_first_hook(part_w), _swap_hook(g_u)),
    )
    got1_w, swap_u = rode[:2], rode[2:]
    part_u = own_half(g_u, swap_u, wire_u)
    mid_w = for_neighbour(g_w, swap_w, got1_w, wire_w)
    b_names = ("b_w_in", "b_w_out")
    two_d = lambda a: a.reshape(a.shape[-2:])
    bin_out, rode = _adamw(
        two_d(b_w_in), gr_bin, two_d(m_b_w_in), two_d(v_b_w_in),
        _join_hooks(_send_second_hook(mid_w), _send_first_hook(part_u)),
    )
    got2_w, got1_u = rode[:2], rode[2:]
    mid_u = for_neighbour(g_u, swap_u, got1_u, wire_u)
    red_w, red_small = _add_received(
        [
            (g_win[0], swap_w[0], got1_w[0], got2_w[0], core, 2),
            (small, swap_w[1], got1_w[1], got2_w[1], 2 * mine + core, N_DEV),
        ],
        mine, core,
    )
    red_w = [red_w]
    bout_out, rode = _adamw(
        two_d(b_w_out), gr_bout, two_d(m_b_w_out), two_d(v_b_w_out),
        _join_hooks(_send_second_hook(mid_u), _share_hook(red_w, red_small, g_nw0_mine)),
    )
    b_out = list(zip(bin_out, bout_out))
    got2_u, gr_win, small_r, g_nw0_all = rode[:1], rode[1].reshape(a_w_in.shape[1:]), rode[2], rode[3]
    red_wout = received(g_u, swap_u, got1_u, got2_u)
    (gr_wout,) = _run_hook(_share_hook(red_wout), "share_reduced")
    win_out = _adamw(two_d(a_w_in), gr_win, two_d(m_a_w_in), two_d(v_a_w_in))
    g_nw0 = _sum_partials(g_nw0_all)
    gr_wout = gr_wout.reshape(a_w_out.shape[1:])
    (g_nw1_r, g_a_ln_w, g_a_ln_b, g_a_w_s, g_a_b_s, g_gate_a_w, g_gate_x_w, g_norm_f, gf_cw, gf_cb, gf_gab, gf_gxb,
     gf_lam, loss) = _unpack(small_r.reshape(-1), small_shapes)
    g_norm_w = jnp.concatenate([g_nw0, g_nw1_r], axis=0)
    shard = lambda g: lax.dynamic_slice_in_dim(g, mine * (bw // N_CHIPS), bw // N_CHIPS, axis=1)

    grads = {
        "norm_w": g_norm_w, "a_w_in": gr_win[None], "a_ln_w": g_a_ln_w, "a_ln_b": g_a_ln_b, "a_w_s": g_a_w_s[None],
        "a_b_s": g_a_b_s[None], "a_w_out": gr_wout[None], "b_w_in": gr_bin[None], "b_conv_w": shard(gf_cw)[None],
        "b_conv_b": shard(gf_cb), "b_gate_a_w": g_gate_a_w[None], "b_gate_a_b": shard(gf_gab),
        "b_gate_x_w": g_gate_x_w[None], "b_gate_x_b": shard(gf_gxb), "b_lambda": shard(gf_lam),
        "b_w_out": gr_bout[None], "norm_f_w": g_norm_f,
    }
    names = list(weights)
    delta, new_m, new_v = {}, {}, {}
    delta["a_w_in"], new_m["a_w_in"], new_v["a_w_in"] = win_out
    delta["a_w_out"], new_m["a_w_out"], new_v["a_w_out"] = _adamw(
        two_d(a_w_out), gr_wout, two_d(m_a_w_out), two_d(v_a_w_out)
    )
    small_names = [n for n in names if n not in ("a_w_in", "a_w_out") + b_names]
    at_least_2d = lambda a: a.reshape(1, -1) if a.ndim == 1 else a
    small_out = _adamw_many(*[[at_least_2d(src[n]) for n in small_names] for src in (weights, grads, m_in, v_in)])
    for dst, vals, b_vals in zip((delta, new_m, new_v), small_out, b_out):
        dst.update(zip(small_names, vals))
        dst.update(zip(b_names, b_vals))
    for dst in (delta, new_m, new_v):
        for n in names:
            dst[n] = dst[n].reshape(weights[n].shape)

    return (
        loss.reshape(()),
        grad_x[None],
        *[grads[n] for n in names],
        *[delta[n] for n in names],
        *[new_m[n] for n in names],
        *[new_v[n] for n in names],
    )
```

```python
import jax
import jax.numpy as jnp
from jax import lax
from jax.experimental import pallas as pl
from jax.experimental.pallas import tpu as pltpu

F32 = jnp.float32
BF16 = jnp.bfloat16

RMS_EPS = 1e-6
LN_EPS = 1e-5
RG_C = 8.0
CHUNK = 128
A_GROUPS = 8
B_HEADS = 12
CONV_WIDTH = 4

ADAM_LR = 0.001
ADAM_B1 = 0.9
ADAM_B2 = 0.999
ADAM_EPS = 1e-08
ADAM_WD = 0.01
ADAM_STEP = 10

N_CHIPS = 4
N_DEV = 8
SUBLANES = 8
LANES = 128
V7X_VMEM_BYTES = 64 * 1024 * 1024
VMEM_LIMIT = V7X_VMEM_BYTES * 7 // 8
MESH = pl.DeviceIdType.MESH
ANY = pl.BlockSpec(memory_space=pl.ANY)

TM_FWD = 256
TM_A_BWD = 256
TM_A_DX = 512

GELU_C0 = 0.7978845608028654
GELU_C1 = 0.044715


class _Hook:
    def __init__(self, operands, out_shapes, aliases, n_sems, start, finish, middle=None, late=None):
        self.operands, self.out_shapes, self.aliases, self.n_sems = operands, out_shapes, aliases, n_sems
        self.start, self.finish, self.middle, self.late = start, finish, middle, late


class _SemView:
    def __init__(self, base, off):
        self.base, self.off = base, off

    @property
    def at(self):
        return self

    def __getitem__(self, k):
        return self.base.at[self.off + k]


def _join_hooks(*hooks):
    if len(hooks) == 1:
        return hooks[0]
    operands, out_shapes, aliases, spans = [], [], {}, []
    n_sems = 0
    for h in hooks:
        aliases.update({len(operands) + i: len(out_shapes) + o for i, o in h.aliases.items()})
        spans.append((len(operands), len(h.operands), len(out_shapes), len(h.out_shapes), n_sems))
        operands += list(h.operands)
        out_shapes += list(h.out_shapes)
        n_sems += h.n_sems

    def each(which):
        def run(ins, outs, send, recv):
            for h, (i0, ni, o0, no, s0) in zip(hooks, spans):
                step = getattr(h, which)
                if step is not None:
                    step(ins[i0 : i0 + ni], outs[o0 : o0 + no], _SemView(send, s0), _SemView(recv, s0))

        return run

    middle = each("middle") if any(h.middle is not None for h in hooks) else None
    late = each("late") if any(h.late is not None for h in hooks) else None
    return _Hook(operands, out_shapes, aliases, n_sems, each("start"), each("finish"), middle, late)


def _pcall(body, hook=None, **kw):
    if hook is None:
        return pl.pallas_call(body, **kw)
    n_pre = 0
    if "grid_spec" in kw:
        spec = kw.pop("grid_spec")
        n_pre = spec.num_scalar_prefetch
        kw.update(
            grid=tuple(spec.grid), in_specs=list(spec.in_specs), out_specs=list(spec.out_specs),
            scratch_shapes=list(spec.scratch_shapes),
        )
    n_in, n_out = len(kw["in_specs"]), len(kw["out_shape"])
    hi, ho = len(hook.operands), len(hook.out_shapes)
    grid = kw.get("grid", ())

    def wrapped(*refs):
        pre, refs = refs[:n_pre], refs[n_pre:]
        ins, h_in = refs[:n_in], refs[n_in : n_in + hi]
        outs = refs[n_in + hi : n_in + hi + n_out]
        h_out = refs[n_in + hi + n_out : n_in + hi + n_out + ho]
        scratch = refs[n_in + hi + n_out + ho : -2]
        send_sems, recv_sems = refs[-2:]
        if not grid:
            hook.start(h_in, h_out, send_sems, recv_sems)
            if hook.middle is not None:
                hook.middle(h_in, h_out, send_sems, recv_sems)
            body(*pre, *ins, *outs, *scratch)
            if hook.late is not None:
                hook.late(h_in, h_out, send_sems, recv_sems)
            hook.finish(h_in, h_out, send_sems, recv_sems)
            return
        first = pl.program_id(0) == 0
        last = pl.program_id(0) == grid[0] - 1
        for axis in range(1, len(grid)):
            first = jnp.logical_and(first, pl.program_id(axis) == 0)
            last = jnp.logical_and(last, pl.program_id(axis) == grid[axis] - 1)

        @pl.when(first)
        def _():
            hook.start(h_in, h_out, send_sems, recv_sems)

        for when, step in ((hook.middle, grid[0] // 4), (hook.late, grid[0] - 1)):
            if when is not None:
                assert len(grid) == 1 and grid[0] >= 4

                @pl.when(pl.program_id(0) == step)
                def _(when=when):
                    when(h_in, h_out, send_sems, recv_sems)

        body(*pre, *ins, *outs, *scratch)

        @pl.when(last)
        def _():
            hook.finish(h_in, h_out, send_sems, recv_sems)

    aliases = dict(kw.pop("input_output_aliases", {}))
    aliases.update({n_pre + n_in + i: n_out + o for i, o in hook.aliases.items()})
    kw.update(
        in_specs=list(kw["in_specs"]) + [ANY] * hi,
        out_specs=list(kw["out_specs"]) + [ANY] * ho,
        out_shape=list(kw["out_shape"]) + list(hook.out_shapes),
        scratch_shapes=list(kw.get("scratch_shapes", ()))
        + [pltpu.SemaphoreType.DMA((hook.n_sems,)), pltpu.SemaphoreType.DMA((hook.n_sems,))],
        input_output_aliases=aliases,
    )
    if n_pre:
        kw["grid_spec"] = pltpu.PrefetchScalarGridSpec(
            num_scalar_prefetch=n_pre, grid=kw.pop("grid"), in_specs=kw.pop("in_specs"),
            out_specs=kw.pop("out_specs"), scratch_shapes=kw.pop("scratch_shapes"),
        )
    call = pl.pallas_call(wrapped, **kw)

    def run(*operands):
        outs = call(*operands, *hook.operands)
        return outs[:n_out], outs[n_out:]

    return run


def _run_hook(hook, name):
    def body():
        pass

    return _pcall(body, hook, name=name, in_specs=[], out_specs=[], out_shape=[])()[1]


def _cparams(sem=None):
    return pltpu.CompilerParams(dimension_semantics=sem, vmem_limit_bytes=VMEM_LIMIT)


def _full(shape):
    zeros = (0,) * len(shape)
    return pl.BlockSpec(shape, lambda *_: zeros)


def _scalars(*vals):
    return jnp.stack([jnp.asarray(v, jnp.int32) for v in vals])


def _sigmoid(x):
    return 1.0 / (1.0 + jnp.exp(-x))


def _gelu(x):
    t = jnp.tanh(GELU_C0 * (x + GELU_C1 * (x * x * x)))
    return x * (0.5 * (1.0 + t))


def _gelu_and_grad(x):
    x2 = x * x
    t = jnp.tanh(GELU_C0 * (x + GELU_C1 * (x2 * x)))
    cdf = 0.5 * (1.0 + t)
    return x * cdf, cdf + 0.5 * x * (1.0 - t * t) * (GELU_C0 * (1.0 + 3.0 * GELU_C1 * x2))


def _silu_and_grad(x):
    s = _sigmoid(x)
    return x * s, s * (1.0 + x * (1.0 - s))


def _softplus_neg(lam):
    u = jnp.exp(-jnp.abs(lam))
    w = 1.0 + u
    log1p = jnp.where(w == 1.0, u, jnp.log(w) * (u / jnp.where(w == 1.0, 1.0, w - 1.0)))
    return jnp.maximum(-lam, 0.0) + log1p


def _dot(a, b):
    return jnp.dot(a, b, preferred_element_type=F32)


def _dot_nt(a, b):
    return lax.dot_general(a, b, (((1,), (1,)), ((), ())), preferred_element_type=F32)


def _dot_tn(a, b):
    return lax.dot_general(a, b, (((0,), (0,)), ((), ())), preferred_element_type=F32)


def _shift_down(v, halo, k):
    if k == 0:
        return v
    rolled = pltpu.roll(v, k, 0)
    row = lax.broadcasted_iota(jnp.int32, (SUBLANES, v.shape[1]), 0)
    top = jnp.where(row < k, pltpu.roll(halo, k, 0), rolled[:SUBLANES])
    return jnp.concatenate([top, rolled[SUBLANES:]], axis=0)


def _shift_up(v, head, k):
    if k == 0:
        return v
    n = v.shape[0]
    rolled = pltpu.roll(v, n - k, 0)
    row = lax.broadcasted_iota(jnp.int32, (SUBLANES, v.shape[1]), 0)
    bot = jnp.where(row >= SUBLANES - k, pltpu.roll(head, SUBLANES - k, 0), rolled[n - SUBLANES :])
    return jnp.concatenate([rolled[: n - SUBLANES], bot], axis=0)


def _scan_blocks(a_ref, b_ref, out_ref, carry, n_rows, reverse):
    width = a_ref.shape[1]
    row = lax.broadcasted_iota(jnp.int32, (SUBLANES, width), 0)
    n_blocks = n_rows // SUBLANES

    def block(j, carry):
        i = (n_blocks - 1 - j) if reverse else j
        r0 = pl.multiple_of(i * SUBLANES, SUBLANES)
        a = a_ref[pl.ds(r0, SUBLANES), :]
        b = b_ref[pl.ds(r0, SUBLANES), :]
        for d in (1, 2, 4):
            shift = (SUBLANES - d) if reverse else d
            keep = (row < SUBLANES - d) if reverse else (row >= d)
            a_s = pltpu.roll(a, shift, 0)
            b_s = pltpu.roll(b, shift, 0)
            b = jnp.where(keep, a * b_s + b, b)
            a = jnp.where(keep, a * a_s, a)
        h = a * carry + b
        out_ref[pl.ds(r0, SUBLANES), :] = h
        edge = h[0:1, :] if reverse else h[SUBLANES - 1 : SUBLANES, :]
        return jnp.broadcast_to(edge, (SUBLANES, width))

    return lax.fori_loop(0, n_blocks, block, carry)


def _rms_fwd(x, w):
    r = lax.rsqrt(jnp.mean(x * x, axis=-1, keepdims=True) + RMS_EPS)
    xh = x * r
    return xh * w, xh, r


def _rms_bwd(dh, xh, r, w):
    dxh = dh * w
    dx = r * (dxh - xh * jnp.mean(dxh * xh, axis=-1, keepdims=True))
    return dx, jnp.sum(dh * xh, axis=0, keepdims=True)


def _cast_to_segments(ws, mine, steps, hook=None, small_maps=None):
    per = steps // 2
    n = len(ws)
    maps = () if small_maps is None else small_maps

    def body(k_ref, *refs):
        for w_ref, o_ref in zip(refs[:n], refs[n + len(maps) : 2 * n + len(maps)]):
            o_ref[...] = w_ref[...].astype(BF16)
        if small_maps is not None:
            pl.when(pl.program_id(0) == 0)(lambda: _prepare_small_maps(*refs[n : n + 3], *refs[2 * n + 3 :]))

    rows = [w.shape[0] // steps for w in ws]
    segment = lambda i, k_ref: (2 * k_ref[0] + i // per, i % per, 0)
    whole = lambda shape: pl.BlockSpec(shape, lambda i, k_ref: (0,) * len(shape))
    prepared = []
    if small_maps is not None:
        (g, ck, _), (h, hd, _) = maps[0].shape, maps[1].shape
        prepared = [(g, ck, ck), (g, ck, ck), (h, hd, 2 * hd), (h, 2 * hd, hd)]
    out = _pcall(
        body,
        hook,
        name=f"cast_{ws[0].shape[0]}x{ws[0].shape[1]}",
        grid_spec=pltpu.PrefetchScalarGridSpec(
            num_scalar_prefetch=1,
            grid=(steps,),
            in_specs=[pl.BlockSpec((r, w.shape[1]), lambda i, k_ref: (i, 0)) for w, r in zip(ws, rows)]
            + [whole(m.shape) for m in maps],
            out_specs=[pl.BlockSpec((None, r, w.shape[1]), segment) for w, r in zip(ws, rows)]
            + [whole(shape) for shape in prepared],
        ),
        out_shape=[jax.ShapeDtypeStruct((N_DEV, w.shape[0] // 2, w.shape[1]), BF16) for w in ws]
        + [jax.ShapeDtypeStruct(shape, BF16) for shape in prepared],
        compiler_params=_cparams(("arbitrary",)),
    )(_scalars(mine), *ws, *maps)
    return out


def _prepare_small_maps(ws_ref, ga_ref, gx_ref, wc_ref, wct_ref, gab_ref, gabt_ref):
    ck, hd = ws_ref.shape[1], ga_ref.shape[1]
    tril = lax.broadcasted_iota(jnp.int32, (ck, ck), 0) >= lax.broadcasted_iota(jnp.int32, (ck, ck), 1)
    for g in range(ws_ref.shape[0]):
        w = ws_ref[g] * tril.astype(F32)
        wc_ref[g] = w.astype(BF16)
        wct_ref[g] = w.T.astype(BF16)
    for h in range(ga_ref.shape[0]):
        for k, m_ref in enumerate((ga_ref, gx_ref)):
            m = m_ref[h]
            gab_ref[h, :, k * hd : (k + 1) * hd] = m.astype(BF16)
            gabt_ref[h, k * hd : (k + 1) * hd, :] = m.T.astype(BF16)


def _place():
    x, y, c = lax.axis_index("x"), lax.axis_index("y"), lax.axis_index("c")
    chips = [(1 - x, y), (x, 1 - y), (1 - x, 1 - y)]
    return x, y, c, chips


def _chip_no(chip):
    return 2 * chip[0] + chip[1]


def _rcopy(src, dst, send_sem, recv_sem, to):
    return pltpu.make_async_remote_copy(
        src_ref=src, dst_ref=dst, send_sem=send_sem, recv_sem=recv_sem, device_id=to, device_id_type=MESH
    )


def _gather_hook(big, small=None):
    nb = len(big)
    n_sems = 6 * nb + 4

    def places():
        x, y, c, chips = _place()
        first = (x ^ (1 - c), y ^ c)
        second = (x ^ c, y ^ (1 - c))
        return x, y, c, chips, first, second, (1 - x, 1 - y)

    def seg(outs, b, chip, half):
        return outs[b].at[2 * _chip_no(chip) + half]

    def step1(outs, send, recv):
        x, y, c, _, first, _, _ = places()
        return [
            _rcopy(seg(outs, b, (x, y), c), seg(outs, b, (x, y), c), send.at[6 * b], recv.at[6 * b], (*first, c))
            for b in range(nb)
        ]

    def step2(outs, send, recv):
        x, y, c, _, first, second, _ = places()
        copies = []
        for b in range(nb):
            for k, chip in ((1, (x, y)), (2, first)):
                src = seg(outs, b, chip, c)
                copies.append(_rcopy(src, src, send.at[6 * b + k], recv.at[6 * b + k], (*second, c)))
        return copies

    def hand_over(outs, send, recv, k, chip):
        x, y, c, *_ = places()
        return [
            _rcopy(seg(outs, b, chip, c), seg(outs, b, chip, c), send.at[6 * b + k], recv.at[6 * b + k], (x, y, 1 - c))
            for b in range(nb)
        ]

    def wait_landed(outs, send, recv, k, chip, half):
        x, y, c, *_ = places()
        for b in range(nb):
            got = seg(outs, b, chip, half)
            _rcopy(got, got, send.at[6 * b + k], recv.at[6 * b + k], (x, y, c)).wait_recv()

    def small_copies(ins, outs, send, recv):
        x, y, c, chips, *_ = places()
        there = outs[nb].at[_chip_no((x, y))]
        return [
            _rcopy(ins[nb], there, send.at[6 * nb + j], recv.at[6 * nb + j], (*chip, c)) for j, chip in enumerate(chips)
        ]

    def local_copy(ins, outs, send):
        x, y, _, _ = _place()
        return pltpu.make_async_copy(ins[nb], outs[nb].at[_chip_no((x, y))], send.at[6 * nb + 3])

    def start(ins, outs, send, recv):
        for cp in step1(outs, send, recv):
            cp.start()
        if small is not None:
            for cp in small_copies(ins, outs, send, recv):
                cp.start()
            local_copy(ins, outs, send).start()

    def middle(ins, outs, send, recv):
        *_, first, _, _ = places()
        wait_landed(outs, send, recv, 0, first, places()[2])
        for cp in step2(outs, send, recv) + hand_over(outs, send, recv, 3, first):
            cp.start()

    def late(ins, outs, send, recv):
        x, y, c, chips, first, second, diagonal = places()
        for k, chip in ((1, second), (2, diagonal)):
            wait_landed(outs, send, recv, k, chip, c)
            for cp in hand_over(outs, send, recv, 3 + k, chip):
                cp.start()

    def finish(ins, outs, send, recv):
        x, y, c, chips, first, second, diagonal = places()
        wait_landed(outs, send, recv, 3, second, 1 - c)
        wait_landed(outs, send, recv, 4, first, 1 - c)
        wait_landed(outs, send, recv, 5, diagonal, 1 - c)
        sent = step1(outs, send, recv) + step2(outs, send, recv)
        for k, chip in ((3, first), (4, second), (5, diagonal)):
            sent += hand_over(outs, send, recv, k, chip)
        for cp in sent:
            cp.wait_send()
        if small is not None:
            for j, chip in enumerate(chips):
                got = outs[nb].at[_chip_no(chip)]
                _rcopy(got, got, send.at[6 * nb + j], recv.at[6 * nb + j], (x, y, c)).wait_recv()
            for cp in small_copies(ins, outs, send, recv):
                cp.wait_send()
            local_copy(ins, outs, send).wait()

    operands = list(big) + ([small] if small is not None else [])
    out_shapes = [jax.ShapeDtypeStruct(b.shape, b.dtype) for b in big]
    if small is not None:
        out_shapes.append(jax.ShapeDtypeStruct((N_CHIPS, *small.shape), small.dtype))
    return _Hook(operands, out_shapes, {b: b for b in range(nb)}, n_sems, start, finish, middle, late)


def _both_ways_hook(operands, out_shapes, copies_of, n_sems):
    def start(ins, outs, send, recv):
        for cp in copies_of(ins, outs, send, recv):
            cp.start()

    def finish(ins, outs, send, recv):
        for cp in copies_of(ins, outs, send, recv):
            cp.wait()

    return _Hook(operands, out_shapes, {}, n_sems, start, finish)


def _swap_hook(bufs):
    def copies_of(ins, outs, send, recv):
        x, y, c, _ = _place()
        copies = []
        for b in range(len(bufs)):
            for j in range(N_CHIPS):
                k = b * N_CHIPS + j
                copies.append(_rcopy(ins[b].at[2 * j + 1 - c], outs[b].at[j], send.at[k], recv.at[k], (x, y, 1 - c)))
        return copies

    out_shapes = [jax.ShapeDtypeStruct((N_CHIPS, *b.shape[1:]), b.dtype) for b in bufs]
    return _both_ways_hook(list(bufs), out_shapes, copies_of, len(bufs) * N_CHIPS)


def _axis_order():
    x, y, c, _ = _place()
    return (x, y), c, (x ^ (1 - c), y ^ c), (x ^ c, y ^ (1 - c)), (1 - x, 1 - y)


def _send_first_hook(parts):
    def copies_of(ins, outs, send, recv):
        _, c, first, _, _ = _axis_order()
        copies = []
        for b in range(len(parts)):
            for k in range(2):
                sem = 2 * b + k
                copies.append(_rcopy(ins[b].at[k], outs[b].at[k], send.at[sem], recv.at[sem], (*first, c)))
        return copies

    out_shapes = [jax.ShapeDtypeStruct((2, *p.shape[1:]), p.dtype) for p in parts]
    return _both_ways_hook(list(parts), out_shapes, copies_of, len(parts) * 2)


def _send_second_hook(mids):
    def copies_of(ins, outs, send, recv):
        _, c, _, second, _ = _axis_order()
        return [_rcopy(ins[b], outs[b], send.at[b], recv.at[b], (*second, c)) for b in range(len(mids))]

    out_shapes = [jax.ShapeDtypeStruct(m.shape, m.dtype) for m in mids]
    return _both_ways_hook(list(mids), out_shapes, copies_of, len(mids))


def _share_hook(big, small=None, tiny=None):
    nb = len(big)
    n_sems = nb + 7 + N_DEV
    t0 = nb + 7

    def tiny_copies(ins, outs, send, recv):
        x, y, c, _ = _place()
        there = outs[-1].at[2 * _chip_no((x, y)) + c]
        copies = []
        for r in range(1, N_DEV):
            to = (x ^ (r >> 2 & 1), y ^ (r >> 1 & 1), c ^ (r & 1))
            copies.append(_rcopy(ins[-1], there, send.at[t0 + r], recv.at[t0 + r], to))
        return copies

    def tiny_local(ins, outs, send):
        x, y, c, _ = _place()
        return pltpu.make_async_copy(ins[-1], outs[-1].at[2 * _chip_no((x, y)) + c], send.at[t0])

    def first_copies(outs, send, recv):
        x, y, c, chips = _place()
        sibling = (x, y, 1 - c)
        copies = [_rcopy(outs[b].at[c], outs[b].at[c], send.at[b], recv.at[b], sibling) for b in range(nb)]
        if small is not None:
            own = outs[nb].at[2 * _chip_no((x, y)) + c]
            copies.append(_rcopy(own, own, send.at[nb], recv.at[nb], sibling))
            for j, chip in enumerate(chips):
                copies.append(_rcopy(own, own, send.at[nb + 1 + j], recv.at[nb + 1 + j], (*chip, c)))
        return copies

    def start(ins, outs, send, recv):
        for cp in first_copies(outs, send, recv):
            cp.start()
        if tiny is not None:
            for cp in tiny_copies(ins, outs, send, recv):
                cp.start()
            tiny_local(ins, outs, send).start()

    def finish(ins, outs, send, recv):
        x, y, c, chips = _place()
        me, sibling = (x, y, c), (x, y, 1 - c)
        if tiny is not None:
            for cp in tiny_copies(ins, outs, send, recv):
                cp.wait()
            tiny_local(ins, outs, send).wait()
        passed = []
        if small is not None:
            for j, chip in enumerate(chips):
                got = outs[nb].at[2 * _chip_no(chip) + c]
                _rcopy(got, got, send.at[nb + 1 + j], recv.at[nb + 1 + j], me).wait_recv()
                fwd = _rcopy(got, got, send.at[nb + 4 + j], recv.at[nb + 4 + j], sibling)
                fwd.start()
                passed.append(fwd)
        for b in range(nb):
            got = outs[b].at[1 - c]
            _rcopy(got, got, send.at[b], recv.at[b], me).wait_recv()
        if small is not None:
            got = outs[nb].at[2 * _chip_no((x, y)) + 1 - c]
            _rcopy(got, got, send.at[nb], recv.at[nb], me).wait_recv()
            for j, chip in enumerate(chips):
                got = outs[nb].at[2 * _chip_no(chip) + 1 - c]
                _rcopy(got, got, send.at[nb + 4 + j], recv.at[nb + 4 + j], me).wait_recv()
        for cp in first_copies(outs, send, recv) + passed:
            cp.wait_send()

    operands = list(big) + ([small] if small is not None else [])
    out_shapes = [jax.ShapeDtypeStruct(a.shape, a.dtype) for a in operands]
    aliases = {i: i for i in range(len(operands))}
    if tiny is not None:
        operands.append(tiny)
        out_shapes.append(jax.ShapeDtypeStruct((N_DEV, *tiny.shape), tiny.dtype))
    return _Hook(operands, out_shapes, aliases, n_sems, start, finish)


def _row_tile(rows, cols, target_bytes=2 * 1024 * 1024):
    best = SUBLANES
    for t in range(SUBLANES, rows + 1, SUBLANES):
        if rows % t == 0 and t * cols * 4 <= target_bytes:
            best = t
    return best


def _halves(buf):
    return buf.reshape(N_CHIPS, 2, *buf.shape[1:])


def _add_own_half(sets, owners, c):
    _, rows, cols = sets[0][0].shape
    tr = _row_tile(rows, cols)
    n = len(sets)

    def body(s_ref, *refs):
        def add(k):
            a_ref, b_ref, o_ref = refs[2 * k], refs[2 * k + 1], refs[2 * n + k]
            o_ref[...] = (a_ref[...] + b_ref[...].astype(F32)).astype(sets[k][2])

        add(0)
        for k in range(1, n):
            pl.when(pl.program_id(1) == 0)(lambda k=k: add(k))

    in_specs, out_specs, out_shape, operands = [], [], [], []
    for k, (buf, got, wire) in enumerate(sets):
        r_k, c_k = (tr, cols) if k == 0 else buf.shape[1:]
        row = (lambda r: r) if k == 0 else (lambda r: 0)
        in_specs += [
            pl.BlockSpec((None, None, r_k, c_k), lambda j, r, s_ref, row=row: (s_ref[j], s_ref[2], row(r), 0)),
            pl.BlockSpec((None, r_k, c_k), lambda j, r, s_ref, row=row: (s_ref[j], row(r), 0)),
        ]
        out_specs.append(pl.BlockSpec((None, r_k, c_k), lambda j, r, s_ref, row=row: (j, row(r), 0)))
        out_shape.append(jax.ShapeDtypeStruct((2, *buf.shape[1:]), wire))
        operands += [_halves(buf), got]
    return _pcall(
        body,
        name=f"add_own_half_{rows}x{cols}",
        grid_spec=pltpu.PrefetchScalarGridSpec(
            num_scalar_prefetch=1, grid=(2, rows // tr), in_specs=in_specs, out_specs=out_specs
        ),
        out_shape=out_shape,
        compiler_params=_cparams(("arbitrary", "arbitrary")),
    )(_scalars(owners[0], owners[1], c), *operands)


def _add_for_neighbour(sets, second, c):
    _, rows, cols = sets[0][0].shape
    tr = _row_tile(rows, cols)
    n = len(sets)

    def body(s_ref, *refs):
        def add(k):
            x_ref, a_ref, g_ref, o_ref = refs[3 * k], refs[3 * k + 1], refs[3 * k + 2], refs[3 * n + k]
            o_ref[...] = ((x_ref[...] + a_ref[...].astype(F32)) + g_ref[...].astype(F32)).astype(sets[k][3])

        add(0)
        for k in range(1, n):
            pl.when(pl.program_id(0) == 0)(lambda k=k: add(k))

    in_specs, out_specs, out_shape, operands = [], [], [], []
    for k, (buf, got_a, got1, wire) in enumerate(sets):
        r_k, c_k = (tr, cols) if k == 0 else buf.shape[1:]
        row = (lambda r: r) if k == 0 else (lambda r: 0)
        in_specs += [
            pl.BlockSpec((None, None, r_k, c_k), lambda r, s_ref, row=row: (s_ref[0], s_ref[1], row(r), 0)),
            pl.BlockSpec((None, r_k, c_k), lambda r, s_ref, row=row: (s_ref[0], row(r), 0)),
            pl.BlockSpec((None, r_k, c_k), lambda r, s_ref, row=row: (1, row(r), 0)),
        ]
        out_specs.append(pl.BlockSpec((r_k, c_k), lambda r, s_ref, row=row: (row(r), 0)))
        out_shape.append(jax.ShapeDtypeStruct(buf.shape[1:], wire))
        operands += [_halves(buf), got_a, got1]
    return _pcall(
        body,
        name=f"add_for_neighbour_{rows}x{cols}",
        grid_spec=pltpu.PrefetchScalarGridSpec(
            num_scalar_prefetch=1, grid=(rows // tr,), in_specs=in_specs, out_specs=out_specs
        ),
        out_shape=out_shape,
        compiler_params=_cparams(("arbitrary",)),
    )(_scalars(second, c), *operands)


def _add_received(sets, mine, c):
    _, rows, cols = sets[0][0].shape
    tr = _row_tile(rows, cols)
    n = len(sets)

    def body(s_ref, *refs):
        def add(k):
            x_ref, a_ref, g1_ref, g2_ref, o_ref = *refs[4 * k : 4 * k + 4], refs[4 * n + k]
            own = x_ref[...] + a_ref[...].astype(F32)
            o_ref[...] = (own + g1_ref[...].astype(F32)) + g2_ref[...].astype(F32)

        add(0)
        for k in range(1, n):
            pl.when(pl.program_id(0) == 0)(lambda k=k: add(k))

    in_specs, out_specs, out_shape, operands = [], [], [], []
    for k, (buf, got_a, got1, got2, slot, n_slots) in enumerate(sets):
        r_k, c_k = (tr, cols) if k == 0 else buf.shape[1:]
        row = (lambda r: r) if k == 0 else (lambda r: 0)
        in_specs += [
            pl.BlockSpec((None, None, r_k, c_k), lambda r, s_ref, row=row: (s_ref[0], s_ref[1], row(r), 0)),
            pl.BlockSpec((None, r_k, c_k), lambda r, s_ref, row=row: (s_ref[0], row(r), 0)),
            pl.BlockSpec((None, r_k, c_k), lambda r, s_ref, row=row: (0, row(r), 0)),
            pl.BlockSpec((r_k, c_k), lambda r, s_ref, row=row: (row(r), 0)),
        ]
        out_specs.append(pl.BlockSpec((None, r_k, c_k), lambda r, s_ref, row=row, k=k: (s_ref[2 + k], row(r), 0)))
        out_shape.append(jax.ShapeDtypeStruct((n_slots, *buf.shape[1:]), F32))
        operands += [_halves(buf), got_a, got1, got2]
    return _pcall(
        body,
        name=f"add_received_{rows}x{cols}",
        grid_spec=pltpu.PrefetchScalarGridSpec(
            num_scalar_prefetch=1, grid=(rows // tr,), in_specs=in_specs, out_specs=out_specs
        ),
        out_shape=out_shape,
        compiler_params=_cparams(("arbitrary",)),
    )(_scalars(mine, c, *[s[4] for s in sets]), *operands)


def _layer_a_fwd(x, nw, win, ln_w, ln_b, wc, bs_t, wout, tm, hook):
    t_rows, d = x.shape
    n_sh, _, s_cols = win.shape
    aw = wout.shape[0]
    gd = aw // A_GROUPS
    tn = 512
    assert s_cols % tn == 0 and aw % tn == 0 and tm % CHUNK == 0

    def body(x_ref, nw_ref, win_ref, lnw_ref, lnb_ref, wc_ref, bst_ref, wout_ref, z_ref, x1_ref, h_ref, u_s, v_s, y_s):
        x = x_ref[...]
        h, _, _ = _rms_fwd(x, nw_ref[...])
        h = h.astype(BF16)
        h_ref[...] = h
        for j in range(3 * aw // tn):
            k, off = divmod(j * tn, s_cols)
            cols = slice((j * tn) % aw, (j * tn) % aw + tn)
            zj = _dot(h, win_ref[k, :, off : off + tn])
            z_ref[:, j * tn : (j + 1) * tn] = zj
            if j * tn < aw:
                u_s[:, cols] = _gelu(zj)
            elif j * tn < 2 * aw:
                v_s[:, cols] = _gelu(zj)
            else:
                u_s[:, cols] = u_s[:, cols] * (zj * _sigmoid(zj))
        v = v_s[...]
        mu = jnp.mean(v, axis=-1, keepdims=True)
        vc = v - mu
        rstd = lax.rsqrt(jnp.mean(vc * vc, axis=-1, keepdims=True) + LN_EPS)
        v_s[...] = (vc * rstd) * lnw_ref[...] + lnb_ref[...]
        for ck in range(tm // CHUNK):
            rows = slice(ck * CHUNK, (ck + 1) * CHUNK)
            for g in range(A_GROUPS):
                cols = slice(g * gd, (g + 1) * gd)
                s = _dot(wc_ref[g], v_s[rows, cols].astype(BF16)) + bst_ref[:, g : g + 1]
                y_s[rows, cols] = (u_s[rows, cols] * s).astype(BF16)
        x1_ref[...] = x + _dot(y_s[...], wout_ref[...])

    row = lambda i: (i, 0)
    return _pcall(
        body,
        hook,
        name="layer_a_fwd",
        grid=(t_rows // tm,),
        in_specs=[
            pl.BlockSpec((tm, d), row),
            _full(nw.shape),
            _full(win.shape),
            _full(ln_w.shape),
            _full(ln_b.shape),
            _full(wc.shape),
            _full(bs_t.shape),
            _full(wout.shape),
        ],
        out_specs=[pl.BlockSpec((tm, 3 * aw), row), pl.BlockSpec((tm, d), row), pl.BlockSpec((tm, d), row)],
        out_shape=[
            jax.ShapeDtypeStruct((t_rows, 3 * aw), F32),
            jax.ShapeDtypeStruct((t_rows, d), F32),
            jax.ShapeDtypeStruct((t_rows, d), BF16),
        ],
        scratch_shapes=[pltpu.VMEM((tm, aw), F32), pltpu.VMEM((tm, aw), F32), pltpu.VMEM((tm, aw), BF16)],
        compiler_params=_cparams(("arbitrary",)),
    )(x, nw, win, ln_w, ln_b, wc, bs_t, wout)


def _layer_a_bwd(dout, z, ln_w, ln_b, wc, wct, bs_t, wout, tiles, earlier, hook):
    t_rows, d = dout.shape
    aw = wout.shape[0]
    gd = aw // A_GROUPS
    tm = TM_A_BWD
    lo, hi = tiles
    n_earlier = 0 if earlier is None else len(earlier)

    def body(dout_ref, z_ref, lnw_ref, lnb_ref, wc_ref, wct_ref, bst_ref, wout_ref, *rest):
        dz_ref, y_ref, dob_ref, gws_ref, gbs_ref, glnw_ref, glnb_ref, u_s, vh_s, ds_s, dvn_s = rest[n_earlier:]

        @pl.when(pl.program_id(0) == 0)
        def _():
            gws_ref[...] = jnp.zeros_like(gws_ref)
            gbs_ref[...] = jnp.zeros_like(gbs_ref)
            glnw_ref[...] = jnp.zeros_like(glnw_ref)
            glnb_ref[...] = jnp.zeros_like(glnb_ref)

        dob = dout_ref[...].astype(BF16)
        dob_ref[...] = dob
        dy = _dot_nt(dob, wout_ref[...])

        zv = z_ref[:, aw : 2 * aw]
        vg, dvg_dz = _gelu_and_grad(zv)
        mu = jnp.mean(vg, axis=-1, keepdims=True)
        vc = vg - mu
        rstd = lax.rsqrt(jnp.mean(vc * vc, axis=-1, keepdims=True) + LN_EPS)
        vh = vc * rstd
        vh_s[...] = vh
        vn = (vh * lnw_ref[...] + lnb_ref[...]).astype(BF16)

        zu = z_ref[:, 0:aw]
        zg = z_ref[:, 2 * aw : 3 * aw]
        u, du_dz = _gelu_and_grad(zu)
        sg, dsg = _silu_and_grad(zg)
        u_s[...] = u * sg
        tril = lax.broadcasted_iota(jnp.int32, (CHUNK, CHUNK), 0) >= lax.broadcasted_iota(jnp.int32, (CHUNK, CHUNK), 1)
        for ck in range(tm // CHUNK):
            rows = slice(ck * CHUNK, (ck + 1) * CHUNK)
            for g in range(A_GROUPS):
                cols = slice(g * gd, (g + 1) * gd)
                vn_g = vn[rows, cols]
                s = _dot(wc_ref[g], vn_g) + bst_ref[:, g : g + 1]
                usg = u_s[rows, cols]
                dy_g = dy[rows, cols]
                y_ref[rows, cols] = (usg * s).astype(BF16)
                ds = dy_g * usg
                ds_s[rows, cols] = dy_g * s
                gbs_ref[:, g : g + 1] += jnp.sum(ds, axis=-1, keepdims=True)
                dsb = ds.astype(BF16)
                gws_ref[g] += jnp.where(tril, _dot_nt(dsb, vn_g), 0.0)
                dvn_s[rows, cols] = _dot(wct_ref[g], dsb)
        dusg = ds_s[...]
        dz_ref[:, 0:aw] = (dusg * sg * du_dz).astype(BF16)
        dz_ref[:, 2 * aw : 3 * aw] = (dusg * u * dsg).astype(BF16)

        dvn = dvn_s[...]
        vh = vh_s[...]
        glnw_ref[...] += jnp.sum(dvn * vh, axis=0, keepdims=True)
        glnb_ref[...] += jnp.sum(dvn, axis=0, keepdims=True)
        dvh = dvn * lnw_ref[...]
        dvg = rstd * (dvh - jnp.mean(dvh, axis=-1, keepdims=True) - vh * jnp.mean(dvh * vh, axis=-1, keepdims=True))
        dz_ref[:, aw : 2 * aw] = (dvg * dvg_dz).astype(BF16)

    row = lambda i: (i + lo, 0)
    call = _pcall(
        body,
        hook,
        name=f"layer_a_bwd_{lo}",
        grid=(hi - lo,),
        in_specs=[
            pl.BlockSpec((tm, d), row),
            pl.BlockSpec((tm, 3 * aw), row),
            _full(ln_w.shape),
            _full(ln_b.shape),
            _full(wc.shape),
            _full(wct.shape),
            _full(bs_t.shape),
            _full(wout.shape),
        ]
        + [ANY] * n_earlier,
        out_specs=[
            pl.BlockSpec((tm, 3 * aw), row),
            pl.BlockSpec((tm, aw), row),
            pl.BlockSpec((tm, d), row),
            _full((A_GROUPS, CHUNK, CHUNK)),
            _full((CHUNK, A_GROUPS)),
            _full((1, aw)),
            _full((1, aw)),
        ],
        out_shape=[
            jax.ShapeDtypeStruct((t_rows, 3 * aw), BF16),
            jax.ShapeDtypeStruct((t_rows, aw), BF16),
            jax.ShapeDtypeStruct((t_rows, d), BF16),
            jax.ShapeDtypeStruct((A_GROUPS, CHUNK, CHUNK), F32),
            jax.ShapeDtypeStruct((CHUNK, A_GROUPS), F32),
            jax.ShapeDtypeStruct((1, aw), F32),
            jax.ShapeDtypeStruct((1, aw), F32),
        ],
        scratch_shapes=[pltpu.VMEM((tm, aw), F32)] * 4,
        input_output_aliases={8 + i: i for i in range(n_earlier)},
        compiler_params=_cparams(("arbitrary",)),
    )
    return call(dout, z, ln_w, ln_b, wc, wct, bs_t, wout, *(earlier or ()))


def _layer_a_bwd_dx(dout, x, dz, nw, win, tm, tiles, earlier, hook):
    t_rows, d = x.shape
    n_sh, _, s_cols = win.shape
    lo, hi = tiles
    n_earlier = 0 if earlier is None else 1

    def body(dout_ref, x_ref, dz_ref, nw_ref, win_ref, *rest):
        gx_ref, gnw_ref = rest[n_earlier:]

        @pl.when(pl.program_id(0) == 0)
        def _():
            gnw_ref[...] = jnp.zeros_like(gnw_ref)

        dh = jnp.zeros((tm, d), F32)
        for k in range(n_sh):
            dh = dh + _dot_nt(dz_ref[:, k * s_cols : (k + 1) * s_cols], win_ref[k])
        nw = nw_ref[...]
        _, xh, r = _rms_fwd(x_ref[...], nw)
        dx, gnw = _rms_bwd(dh, xh, r, nw)
        gnw_ref[0:1, :] += gnw
        gx_ref[...] = dout_ref[...] + dx

    row = lambda i: (i + lo, 0)
    return _pcall(
        body,
        hook,
        name=f"layer_a_bwd_dx_{lo}",
        grid=(hi - lo,),
        in_specs=[
            pl.BlockSpec((tm, d), row),
            pl.BlockSpec((tm, d), row),
            pl.BlockSpec((tm, n_sh * s_cols), row),
            _full(nw.shape),
            _full(win.shape),
        ]
        + [ANY] * n_earlier,
        out_specs=[pl.BlockSpec((tm, d), row), _full((SUBLANES, d))],
        out_shape=[jax.ShapeDtypeStruct((t_rows, d), F32), jax.ShapeDtypeStruct((SUBLANES, d), F32)],
        input_output_aliases={5: 0} if n_earlier else {},
        compiler_params=_cparams(("arbitrary",)),
    )(dout, x, dz, nw, win, *([earlier] if n_earlier else []))


def _decay(r, sp_h):
    log_a = (-RG_C) * r * sp_h
    a = jnp.exp(log_a)
    mult = jnp.sqrt(jnp.tanh(-log_a) * (a * a + 1.0))
    return a, mult


ROW_CONV_B, ROW_GATE_A_B, ROW_GATE_X_B, ROW_LAMBDA = range(CONV_WIDTH, CONV_WIDTH + 4)


def _gates(xc_h, gab_ref, vec_ref, sp_h, h, hd):
    pre = _dot(xc_h.astype(BF16), gab_ref[h])
    cols = slice(h * hd, (h + 1) * hd)
    r = _sigmoid(pre[:, :hd] + vec_ref[ROW_GATE_A_B : ROW_GATE_A_B + 1, cols])
    ig = _sigmoid(pre[:, hd:] + vec_ref[ROW_GATE_X_B : ROW_GATE_X_B + 1, cols])
    a, mult = _decay(r, sp_h)
    return r, ig, a, mult


def _conv(xb, halo, vec_ref):
    xc = vec_ref[ROW_CONV_B : ROW_CONV_B + 1, :] + vec_ref[CONV_WIDTH - 1 : CONV_WIDTH, :] * xb
    for k in range(CONV_WIDTH - 1):
        xc = xc + vec_ref[k : k + 1, :] * _shift_down(xb, halo, CONV_WIDTH - 1 - k)
    return xc


def _layer_b_fwd(x1, nw, bin_w, vec, gab, bout, nf, tgt, tm):
    t_rows, d = x1.shape
    bw = bout.shape[0]
    hd = bw // B_HEADS
    nt = t_rows // tm

    def body(
        x1_ref, nw_ref, bin_ref, vec_ref, gab_ref, bout_ref, nf_ref, tgt_ref,
        z_ref, h_ref, h1_ref, dx2_ref, loss_ref, gnf_ref,
        tail_s, carry_s, a_s, b_s, hs_s, acc_s,
    ):
        @pl.when(pl.program_id(0) == 0)
        def _():
            tail_s[...] = jnp.zeros_like(tail_s)
            carry_s[...] = jnp.zeros_like(carry_s)
            acc_s[...] = jnp.zeros_like(acc_s)
            gnf_ref[...] = jnp.zeros_like(gnf_ref)

        x1 = x1_ref[...]
        h1, _, _ = _rms_fwd(x1, nw_ref[...])
        h1 = h1.astype(BF16)
        h1_ref[...] = h1
        z = jnp.concatenate([_dot(h1, bin_ref[k]) for k in range(N_CHIPS)], axis=1)
        z_ref[...] = z
        xb = z[:, :bw]
        xc = _conv(xb, tail_s[...], vec_ref)
        tail = xb[tm - SUBLANES :, :]
        tail_s[...] = tail
        sp = _softplus_neg(vec_ref[ROW_LAMBDA : ROW_LAMBDA + 1, :])
        for h in range(B_HEADS):
            cols = slice(h * hd, (h + 1) * hd)
            xc_h = xc[:, cols]
            _, ig, a, mult = _gates(xc_h, gab_ref, vec_ref, sp[:, cols], h, hd)
            a_s[:, cols] = a
            b_s[:, cols] = mult * (ig * xc_h)
        carry = _scan_blocks(a_s, b_s, hs_s, carry_s[...], tm, reverse=False)
        carry_s[...] = carry
        hs = hs_s[...]
        h_ref[...] = hs
        g = z[:, bw:]
        y = (hs * (g * _sigmoid(g))).astype(BF16)
        x2 = x1 + _dot(y, bout_ref[...])

        nf = nf_ref[...]
        o, xh, r = _rms_fwd(x2, nf)
        diff = o - tgt_ref[...]
        acc_s[...] += jnp.sum(diff * diff, axis=0, keepdims=True)
        do = diff * (1.0 / d)
        dx2, gnf = _rms_bwd(do, xh, r, nf)
        gnf_ref[...] += gnf
        dx2_ref[...] = dx2

        @pl.when(pl.program_id(0) == nt - 1)
        def _():
            total = jnp.sum(acc_s[...], axis=-1, keepdims=True) * (0.5 / d)
            loss_ref[...] = jnp.broadcast_to(total, loss_ref.shape)

    row = lambda i: (i, 0)
    return _pcall(
        body,
        name="layer_b_fwd",
        grid=(nt,),
        in_specs=[
            pl.BlockSpec((tm, d), row),
            _full(nw.shape),
            _full(bin_w.shape),
            _full(vec.shape),
            _full(gab.shape),
            _full(bout.shape),
            _full(nf.shape),
            pl.BlockSpec((tm, d), row),
        ],
        out_specs=[
            pl.BlockSpec((tm, 2 * bw), row),
            pl.BlockSpec((tm, bw), row),
            pl.BlockSpec((tm, d), row),
            pl.BlockSpec((tm, d), row),
            _full((1, LANES)),
            _full((1, d)),
        ],
        out_shape=[
            jax.ShapeDtypeStruct((t_rows, 2 * bw), F32),
            jax.ShapeDtypeStruct((t_rows, bw), F32),
            jax.ShapeDtypeStruct((t_rows, d), BF16),
            jax.ShapeDtypeStruct((t_rows, d), F32),
            jax.ShapeDtypeStruct((1, LANES), F32),
            jax.ShapeDtypeStruct((1, d), F32),
        ],
        scratch_shapes=[
            pltpu.VMEM((SUBLANES, bw), F32),
            pltpu.VMEM((SUBLANES, bw), F32),
            pltpu.VMEM((tm, bw), F32),
            pltpu.VMEM((tm, bw), F32),
            pltpu.VMEM((tm, bw), F32),
            pltpu.VMEM((1, d), F32),
        ],
        compiler_params=_cparams(("arbitrary",)),
    )(x1, nw, bin_w, vec, gab, bout, nf, tgt)


def _layer_b_bwd(dout, x1, z, hseq, nw, bin_w, vec, gab, gabt, bout, tm):
    t_rows, d = x1.shape
    bw = bout.shape[0]
    hd = bw // B_HEADS
    nt = t_rows // tm

    def body(
        dout_ref, x1_ref, z_ref, h_ref, xbt_ref, ht_ref, nw_ref, bin_ref, vec_ref, gab_ref, gabt_ref, bout_ref,
        dx1_ref, dz_ref, y_ref, dob_ref, ggab_ref, ggb_ref, gcw_ref, gcb_ref, glam_ref, gnw_ref,
        gcarry_s, afirst_s, head_s, aup_s, dh_s, gt_s, dxc_s, xc_s, r_s, ig_s,
    ):
        step = pl.program_id(0)
        tile = nt - 1 - step

        @pl.when(step == 0)
        def _():
            for ref in (ggab_ref, ggb_ref, gcw_ref, gcb_ref, glam_ref, gnw_ref, gcarry_s, afirst_s, head_s):
                ref[...] = jnp.zeros_like(ref)

        first_tile = tile == 0
        xb_halo = jnp.where(first_tile, 0.0, xbt_ref[...])
        h_halo = jnp.where(first_tile, 0.0, ht_ref[...])

        dout = dout_ref[...]
        dob = dout.astype(BF16)
        dob_ref[...] = dob
        dy = _dot_nt(dob, bout_ref[...])
        hs = h_ref[...]
        g = z_ref[:, bw:]
        sg, dsg = _silu_and_grad(g)
        y_ref[...] = (hs * sg).astype(BF16)
        dz_ref[:, bw:] = (dy * hs * dsg).astype(BF16)
        dh_s[...] = dy * sg

        xb = z_ref[:, :bw]
        xc = _conv(xb, xb_halo, vec_ref)
        xc_s[...] = xc
        lam = vec_ref[ROW_LAMBDA : ROW_LAMBDA + 1, :]
        sp = _softplus_neg(lam)
        for h in range(B_HEADS):
            cols = slice(h * hd, (h + 1) * hd)
            r, ig, a, _ = _gates(xc[:, cols], gab_ref, vec_ref, sp[:, cols], h, hd)
            r_s[:, cols] = r
            ig_s[:, cols] = ig
            aup_s[:, cols] = _shift_up(a, afirst_s[:, cols], 1)
            afirst_s[:, cols] = jnp.broadcast_to(a[0:1, :], (SUBLANES, hd))
        carry = _scan_blocks(aup_s, dh_s, gt_s, gcarry_s[...], tm, reverse=True)
        gcarry_s[...] = carry

        h_prev = _shift_down(hs, h_halo, 1)
        for h in range(B_HEADS):
            cols = slice(h * hd, (h + 1) * hd)
            xc_h = xc_s[:, cols]
            sp_h = sp[:, cols]
            r, ig = r_s[:, cols], ig_s[:, cols]
            a, mult = _decay(r, sp_h)
            gt = gt_s[:, cols]
            da = gt * h_prev[:, cols]
            dmult = gt * (ig * xc_h)
            dig = gt * (mult * xc_h)
            dxc_direct = gt * (mult * ig)
            dla = da * a - dmult * (a * a) / mult
            glam_ref[:, cols] += jnp.sum(dla * r, axis=0, keepdims=True)
            dr = dla * ((-RG_C) * sp_h)
            dpre = jnp.concatenate([dr * r * (1.0 - r), dig * ig * (1.0 - ig)], axis=1)
            ggb_ref[:, cols] += jnp.sum(dpre[:, :hd], axis=0, keepdims=True)
            ggb_ref[:, bw + h * hd : bw + (h + 1) * hd] += jnp.sum(dpre[:, hd:], axis=0, keepdims=True)
            dpb = dpre.astype(BF16)
            ggab_ref[h] += _dot_tn(xc_h.astype(BF16), dpb)
            dxc_s[:, cols] = dxc_direct + _dot(dpb, gabt_ref[h])
        glam_ref[...] = jnp.where(step == nt - 1, glam_ref[...] * (RG_C * _sigmoid(-lam)), glam_ref[...])

        dxc = dxc_s[...]
        gcb_ref[...] += jnp.sum(dxc, axis=0, keepdims=True)
        dxb = vec_ref[CONV_WIDTH - 1 : CONV_WIDTH, :] * dxc
        gcw_ref[CONV_WIDTH - 1 : CONV_WIDTH, :] += jnp.sum(dxc * xb, axis=0, keepdims=True)
        head = head_s[...]
        for k in range(CONV_WIDTH - 1):
            lag = CONV_WIDTH - 1 - k
            dxb = dxb + vec_ref[k : k + 1, :] * _shift_up(dxc, head, lag)
            gcw_ref[k : k + 1, :] += jnp.sum(dxc * _shift_down(xb, xb_halo, lag), axis=0, keepdims=True)
        head_s[...] = dxc[:SUBLANES, :]
        dz_ref[:, :bw] = dxb.astype(BF16)

        s_cols = 2 * bw // N_CHIPS
        dh1 = jnp.zeros((tm, d), F32)
        for k in range(N_CHIPS):
            dh1 = dh1 + _dot_nt(dz_ref[:, k * s_cols : (k + 1) * s_cols], bin_ref[k])
        x1 = x1_ref[...]
        nw = nw_ref[...]
        _, xh, r1 = _rms_fwd(x1, nw)
        dx, gnw = _rms_bwd(dh1, xh, r1, nw)
        gnw_ref[...] += gnw
        dx1_ref[...] = dout + dx

    rev = lambda i: (nt - 1 - i, 0)
    prev = lambda i: (jnp.maximum((nt - 1 - i) * (tm // SUBLANES) - 1, 0), 0)
    return _pcall(
        body,
        name="layer_b_bwd",
        grid=(nt,),
        in_specs=[
            pl.BlockSpec((tm, d), rev),
            pl.BlockSpec((tm, d), rev),
            pl.BlockSpec((tm, 2 * bw), rev),
            pl.BlockSpec((tm, bw), rev),
            pl.BlockSpec((SUBLANES, bw), prev),
            pl.BlockSpec((SUBLANES, bw), prev),
            _full(nw.shape),
            _full(bin_w.shape),
            _full(vec.shape),
            _full(gab.shape),
            _full(gabt.shape),
            _full(bout.shape),
        ],
        out_specs=[
            pl.BlockSpec((tm, d), rev),
            pl.BlockSpec((tm, 2 * bw), rev),
            pl.BlockSpec((tm, bw), rev),
            pl.BlockSpec((tm, d), rev),
            _full((B_HEADS, hd, 2 * hd)),
            _full((1, 2 * bw)),
            _full((SUBLANES, bw)),
            _full((1, bw)),
            _full((1, bw)),
            _full((1, d)),
        ],
        out_shape=[
            jax.ShapeDtypeStruct((t_rows, d), F32),
            jax.ShapeDtypeStruct((t_rows, 2 * bw), BF16),
            jax.ShapeDtypeStruct((t_rows, bw), BF16),
            jax.ShapeDtypeStruct((t_rows, d), BF16),
            jax.ShapeDtypeStruct((B_HEADS, hd, 2 * hd), F32),
            jax.ShapeDtypeStruct((1, 2 * bw), F32),
            jax.ShapeDtypeStruct((SUBLANES, bw), F32),
            jax.ShapeDtypeStruct((1, bw), F32),
            jax.ShapeDtypeStruct((1, bw), F32),
            jax.ShapeDtypeStruct((1, d), F32),
        ],
        scratch_shapes=[pltpu.VMEM((SUBLANES, bw), F32)] * 3 + [pltpu.VMEM((tm, bw), F32)] * 7,
        compiler_params=_cparams(("arbitrary",)),
    )(dout, x1, z, hseq, z, hseq, nw, bin_w, vec, gab, gabt, bout)


def _wgrad(a, b, m_blocks, n_blocks, hook=None, wire_copy=False):
    k, m = a.shape
    n = b.shape[1]
    bm, bn = m // m_blocks, n // n_blocks

    def body(a_ref, b_ref, o_ref, *wire_ref):
        prod = _dot_tn(a_ref[...], b_ref[...])
        o_ref[...] = prod
        if wire_copy:
            wire_ref[0][...] = prod.astype(BF16)

    out_spec = pl.BlockSpec((None, None, bm, bn), lambda j, i: (j, i, 0, 0))
    shape = (n_blocks, m_blocks, bm, bn)
    out = _pcall(
        body,
        hook,
        name=f"wgrad_{m}x{n}",
        grid=(n_blocks, m_blocks),
        in_specs=[pl.BlockSpec((k, bm), lambda j, i: (0, i)), pl.BlockSpec((k, bn), lambda j, i: (0, j))],
        out_specs=[out_spec] * (1 + wire_copy),
        out_shape=[jax.ShapeDtypeStruct(shape, F32)] + [jax.ShapeDtypeStruct(shape, BF16)] * wire_copy,
        compiler_params=_cparams(("arbitrary", "arbitrary")),
    )(a, b)
    outs, rode = (out, None) if hook is None else out
    outs = outs if wire_copy else outs[0]
    return outs if hook is None else (outs, rode)


def _adamw_math(w, g, m, v):
    m = ADAM_B1 * m + (1.0 - ADAM_B1) * g
    v = ADAM_B2 * v + (1.0 - ADAM_B2) * (g * g)
    m_hat = m / (1.0 - ADAM_B1**ADAM_STEP)
    v_hat = v / (1.0 - ADAM_B2**ADAM_STEP)
    delta = -ADAM_LR * (m_hat / (jnp.sqrt(v_hat) + ADAM_EPS) + ADAM_WD * w)
    return delta, m, v


ADAMW_ROW_TILES = 8


def _adamw_rows(ws, gs, ms, vs):
    n = len(ws)

    def body(*refs):
        ins, outs = refs[: 4 * n], refs[4 * n :]
        for i in range(n):
            w_ref, g_ref, m_ref, v_ref = ins[i::n]
            d_ref, mo_ref, vo_ref, go_ref = outs[i::n]
            g = g_ref[...]
            d_ref[...], mo_ref[...], vo_ref[...] = _adamw_math(w_ref[...], g, m_ref[...], v_ref[...])
            go_ref[...] = g

    specs = [pl.BlockSpec((w.shape[0] // ADAMW_ROW_TILES, w.shape[1]), lambda i: (i, 0)) for w in ws]
    outs = _pcall(
        body,
        name="adamw_rows",
        grid=(ADAMW_ROW_TILES,),
        in_specs=specs * 4,
        out_specs=specs * 4,
        out_shape=[jax.ShapeDtypeStruct(w.shape, F32) for w in ws] * 4,
        compiler_params=_cparams(("arbitrary",)),
    )(*ws, *gs, *ms, *vs)
    return [outs[k * n : (k + 1) * n] for k in range(4)]


def _sum_partials(parts):
    def body(p_ref, o_ref):
        total = p_ref[0, 0:1, :]
        for k in range(1, N_DEV):
            total = total + p_ref[k, 0:1, :]
        o_ref[...] = total

    vmem = pl.BlockSpec(memory_space=pltpu.VMEM)
    return _pcall(
        body,
        name="sum_partials",
        in_specs=[vmem],
        out_specs=vmem,
        out_shape=jax.ShapeDtypeStruct((1, parts.shape[2]), F32),
    )(parts)


def _adamw_many(ws, gs, ms, vs):
    n = len(ws)

    def body(*refs):
        w_refs, g_refs, m_refs, v_refs = (refs[i * n : (i + 1) * n] for i in range(4))
        d_refs, mo_refs, vo_refs = (refs[(4 + i) * n : (5 + i) * n] for i in range(3))
        for i in range(n):
            d_refs[i][...], mo_refs[i][...], vo_refs[i][...] = _adamw_math(
                w_refs[i][...], g_refs[i][...], m_refs[i][...], v_refs[i][...]
            )

    vmem = pl.BlockSpec(memory_space=pltpu.VMEM)
    outs = _pcall(
        body,
        name="adamw_small",
        in_specs=[vmem] * (4 * n),
        out_specs=[vmem] * (3 * n),
        out_shape=[jax.ShapeDtypeStruct(w.shape, F32) for w in ws] * 3,
        compiler_params=_cparams(),
    )(*ws, *gs, *ms, *vs)
    return outs[:n], outs[n : 2 * n], outs[2 * n :]


def _pack_rows(parts, lanes=LANES):
    flat = jnp.concatenate([p.reshape(-1) for p in parts])
    per = N_DEV * SUBLANES * lanes
    total = -(-flat.shape[0] // per) * per
    flat = jnp.pad(flat, (0, total - flat.shape[0]))
    return flat.reshape(N_DEV, total // (N_DEV * lanes), lanes)


def _unpack(flat, shapes):
    out, at = [], 0
    for s in shapes:
        n = 1
        for dim in s:
            n *= dim
        out.append(flat[at : at + n].reshape(s))
        at += n
    return out


def kernel(x, norm_w, a_w_in, a_ln_w, a_ln_b, a_w_s, a_b_s, a_w_out, b_w_in, b_conv_w, b_conv_b, b_gate_a_w, b_gate_a_b, b_gate_x_w, b_gate_x_b, b_lambda, b_w_out, norm_f_w, loss_target, m_norm_w, m_a_w_in, m_a_ln_w, m_a_ln_b, m_a_w_s, m_a_b_s, m_a_w_out, m_b_w_in, m_b_conv_w, m_b_conv_b, m_b_gate_a_w, m_b_gate_a_b, m_b_gate_x_w, m_b_gate_x_b, m_b_lambda, m_b_w_out, m_norm_f_w, v_norm_w, v_a_w_in, v_a_ln_w, v_a_ln_b, v_a_w_s, v_a_b_s, v_a_w_out, v_b_w_in, v_b_conv_w, v_b_conv_b, v_b_gate_a_w, v_b_gate_a_b, v_b_gate_x_w, v_b_gate_x_b, v_b_lambda, v_b_w_out, v_norm_f_w):
    t_rows, d = x.shape[1], x.shape[2]
    aw = a_ln_w.shape[1]
    bw = b_gate_a_w.shape[1] * b_gate_a_w.shape[2]
    hd = bw // B_HEADS
    mine = 2 * lax.axis_index("x") + lax.axis_index("y")
    core = lax.axis_index("c")
    weights = dict(norm_w=norm_w, a_w_in=a_w_in, a_ln_w=a_ln_w, a_ln_b=a_ln_b, a_w_s=a_w_s, a_b_s=a_b_s, a_w_out=a_w_out, b_w_in=b_w_in, b_conv_w=b_conv_w, b_conv_b=b_conv_b, b_gate_a_w=b_gate_a_w, b_gate_a_b=b_gate_a_b, b_gate_x_w=b_gate_x_w, b_gate_x_b=b_gate_x_b, b_lambda=b_lambda, b_w_out=b_w_out, norm_f_w=norm_f_w)
    m_in = dict(norm_w=m_norm_w, a_w_in=m_a_w_in, a_ln_w=m_a_ln_w, a_ln_b=m_a_ln_b, a_w_s=m_a_w_s, a_b_s=m_a_b_s, a_w_out=m_a_w_out, b_w_in=m_b_w_in, b_conv_w=m_b_conv_w, b_conv_b=m_b_conv_b, b_gate_a_w=m_b_gate_a_w, b_gate_a_b=m_b_gate_a_b, b_gate_x_w=m_b_gate_x_w, b_gate_x_b=m_b_gate_x_b, b_lambda=m_b_lambda, b_w_out=m_b_w_out, norm_f_w=m_norm_f_w)
    v_in = dict(norm_w=v_norm_w, a_w_in=v_a_w_in, a_ln_w=v_a_ln_w, a_ln_b=v_a_ln_b, a_w_s=v_a_w_s, a_b_s=v_a_b_s, a_w_out=v_a_w_out, b_w_in=v_b_w_in, b_conv_w=v_b_conv_w, b_conv_b=v_b_conv_b, b_gate_a_w=v_b_gate_a_w, b_gate_a_b=v_b_gate_a_b, b_gate_x_w=v_b_gate_x_w, b_gate_x_b=v_b_gate_x_b, b_lambda=v_b_lambda, b_w_out=v_b_w_out, norm_f_w=v_norm_f_w)

    win_l, wout_l = _cast_to_segments([a_w_in[0], a_w_out[0]], mine, 4)
    small_l = jnp.concatenate([b_conv_w[0], b_conv_b, b_gate_a_b, b_gate_x_b, b_lambda], axis=0)
    (bin_l, bout_l, wc, wct, gab, gabt), (win_g, wout_g, small_g) = _cast_to_segments(
        [b_w_in[0], b_w_out[0]], mine, 8, _gather_hook([win_l, wout_l], small_l),
        (a_w_s[0], b_gate_a_w[0], b_gate_x_w[0]),
    )
    win = win_g.reshape(N_CHIPS, d, -1)
    wout = wout_g.reshape(aw, d)
    bs_t = a_b_s[0].T
    nw0, nw1, nf = norm_w[0:1], norm_w[1:2], norm_f_w.reshape(1, d)

    x0 = x[0]
    (z_a, x1, h0), (bin_g, bout_g) = _layer_a_fwd(
        x0, nw0, win, a_ln_w, a_ln_b, wc, bs_t, wout, TM_FWD, _gather_hook([bin_l, bout_l])
    )
    bin_w = bin_g.reshape(N_CHIPS, d, -1)
    bout = bout_g.reshape(bw, d)
    vec = jnp.transpose(small_g, (1, 0, 2)).reshape(SUBLANES, bw)
    z_b, hseq, h1, dx2, loss_l, g_nf = _layer_b_fwd(x1, nw1, bin_w, vec, gab, bout, nf, loss_target[0], TM_FWD)
    dx1, dz_b, y_b, dob_b, g_gab, g_gb, g_cw, g_cb, g_lam, g_nw1 = _layer_b_bwd(
        dx2, x1, z_b, hseq, nw1, bin_w, vec, gab, gabt, bout, TM_FWD
    )
    seg = lambda g: g.reshape(N_DEV, -1, g.shape[3])
    x_at, y_at = lax.axis_index("x"), lax.axis_index("y")
    first_no = 2 * (x_at ^ (1 - core)) + (y_at ^ core)
    second_no = 2 * (x_at ^ core) + (y_at ^ (1 - core))
    bf16s = lambda bufs: [BF16] * len(bufs)
    own_half = lambda bufs, got, wires: _add_own_half(
        list(zip(bufs, got, wires)), (first_no, N_CHIPS - 1 - mine), core
    )
    for_neighbour = lambda bufs, got_a, got1, wires: _add_for_neighbour(
        list(zip(bufs, got_a, got1, wires)), second_no, core
    )
    received = lambda bufs, got_a, got1, got2: _add_received(
        [(b, ga, g1, g2, core, 2) for b, ga, g1, g2 in zip(bufs, got_a, got1, got2)], mine, core
    )

    g_bout = [seg(_wgrad(y_b, dob_b, 2, 1))]
    g_bin, swap_o = _wgrad(h1, dz_b, 1, N_CHIPS, _swap_hook(g_bout))
    g_bin = [seg(g_bin)]
    part_o = own_half(g_bout, swap_o, bf16s(g_bout))
    a_args = (z_a, a_ln_w, a_ln_b, wc, wct, bs_t, wout)
    half = t_rows // TM_A_BWD // 2
    first, rode = _layer_a_bwd(
        dx1, *a_args, (0, half), None, _join_hooks(_swap_hook(g_bin), _send_first_hook(part_o))
    )
    swap_i, got1_o = rode[:1], rode[1:]
    part_i = own_half(g_bin, swap_i, bf16s(g_bin))
    mid_o = for_neighbour(g_bout, swap_o, got1_o, bf16s(g_bout))
    second, rode = _layer_a_bwd(
        dx1, *a_args, (half, 2 * half), first[:3], _join_hooks(_send_first_hook(part_i), _send_second_hook(mid_o))
    )
    got1_i, got2_o = rode[:1], rode[1:]
    dz_a, y_a, dob_a = second[:3]
    g_ws, g_bst, g_lnw, g_lnb = (p + q for p, q in zip(first[3:], second[3:]))
    mid_i = for_neighbour(g_bin, swap_i, got1_i, bf16s(g_bin))
    red_o = received(g_bout, swap_o, got1_o, got2_o)
    (g_win, g_win_wire), rode = _wgrad(
        h0, dz_a, 1, N_CHIPS, _join_hooks(_send_second_hook(mid_i), _share_hook(red_o)), wire_copy=True
    )
    g_win, g_win_wire = [seg(g_win)], seg(g_win_wire)
    got2_i, gr_bout = rode[:1], rode[1].reshape(b_w_out.shape[1:])
    red_i = received(g_bin, swap_i, got1_i, got2_i)

    small_shapes = [
        (1, d), (1, aw), (1, aw), (A_GROUPS, CHUNK, CHUNK), (A_GROUPS, CHUNK), (B_HEADS, hd, hd), (B_HEADS, hd, hd),
        (d,), (CONV_WIDTH, bw), (1, bw), (1, bw), (1, bw), (1, bw), (1, 1),
    ]
    small = _pack_rows(
        [
            g_nw1, g_lnw, g_lnb, g_ws, g_bst.T, g_gab[:, :, :hd], g_gab[:, :, hd:],
            g_nf, g_cw[:CONV_WIDTH], g_cb, g_gb[:, :bw], g_gb[:, bw:], g_lam, loss_l[:, :1],
        ]
    )
    g_w, wire_w = g_win + [small], [BF16, F32]
    g_wout, rode = _wgrad(
        y_a, dob_a, N_CHIPS, 1, _join_hooks(_swap_hook([g_win_wire, small]), _share_hook(red_i))
    )
    g_wout = seg(g_wout)
    swap_w, gr_bin = rode[:2], rode[2].reshape(b_w_in.shape[1:])

    g_u, wire_u = [g_wout], [BF16]
    part_w = own_half(g_w, swap_w, wire_w)
    (grad_x, g_nw0_mine), rode = _layer_a_bwd_dx(
        dx1, x0, dz_a, nw0, win, TM_A_DX, (0, t_rows // TM_A_DX), None,
        _join_hooks(_send_first_hook(part_w), _swap_hook(g_u)),
    )
    got1_w, swap_u = rode[:2], rode[2:]
    part_u = own_half(g_u, swap_u, wire_u)
    mid_w = for_neighbour(g_w, swap_w, got1_w, wire_w)
    rode = _run_hook(_join_hooks(_send_second_hook(mid_w), _send_first_hook(part_u)), "second_axis")
    got2_w, got1_u = rode[:2], rode[2:]
    mid_u = for_neighbour(g_u, swap_u, got1_u, wire_u)
    red_w, red_small = _add_received(
        [
            (g_win[0], swap_w[0], got1_w[0], got2_w[0], core, 2),
            (small, swap_w[1], got1_w[1], got2_w[1], 2 * mine + core, N_DEV),
        ],
        mine, core,
    )
    rode = _run_hook(
        _join_hooks(_send_second_hook(mid_u), _share_hook([red_w], red_small, g_nw0_mine)), "second_axis_and_share"
    )
    got2_u, gr_win, small_r, g_nw0_all = rode[:1], rode[1].reshape(a_w_in.shape[1:]), rode[2], rode[3]
    red_wout = received(g_u, swap_u, got1_u, got2_u)
    (gr_wout,) = _run_hook(_share_hook(red_wout), "share_reduced")
    g_nw0 = _sum_partials(g_nw0_all)
    gr_wout = gr_wout.reshape(a_w_out.shape[1:])
    (g_nw1_r, g_a_ln_w, g_a_ln_b, g_a_w_s, g_a_b_s, g_gate_a_w, g_gate_x_w, g_norm_f, gf_cw, gf_cb, gf_gab, gf_gxb,
     gf_lam, loss) = _unpack(small_r.reshape(-1), small_shapes)
    g_norm_w = jnp.concatenate([g_nw0, g_nw1_r], axis=0)
    shard = lambda g: lax.dynamic_slice_in_dim(g, mine * (bw // N_CHIPS), bw // N_CHIPS, axis=1)

    grads = {
        "norm_w": g_norm_w, "a_w_in": gr_win, "a_ln_w": g_a_ln_w, "a_ln_b": g_a_ln_b, "a_w_s": g_a_w_s[None],
        "a_b_s": g_a_b_s[None], "a_w_out": gr_wout, "b_w_in": gr_bin, "b_conv_w": shard(gf_cw)[None],
        "b_conv_b": shard(gf_cb), "b_gate_a_w": g_gate_a_w[None], "b_gate_a_b": shard(gf_gab),
        "b_gate_x_w": g_gate_x_w[None], "b_gate_x_b": shard(gf_gxb), "b_lambda": shard(gf_lam),
        "b_w_out": gr_bout, "norm_f_w": g_norm_f,
    }
    names = list(weights)
    big_names = ["a_w_in", "a_w_out", "b_w_in", "b_w_out"]
    small_names = [n for n in names if n not in big_names]
    two_d = lambda a: a.reshape(a.shape[-2:])
    at_least_2d = lambda a: a.reshape(1, -1) if a.ndim == 1 else a
    small_out = _adamw_many(*[[at_least_2d(src[n]) for n in small_names] for src in (weights, grads, m_in, v_in)])
    *big_out, big_grads = _adamw_rows(*[[two_d(src[n]) for n in big_names] for src in (weights, grads, m_in, v_in)])
    grads.update(zip(big_names, big_grads))
    delta, new_m, new_v = {}, {}, {}
    for dst, small_vals, big_vals in zip((delta, new_m, new_v), small_out, big_out):
        dst.update(zip(small_names, small_vals))
        dst.update(zip(big_names, big_vals))
    for dst in (grads, delta, new_m, new_v):
        for n in names:
            dst[n] = dst[n].reshape(weights[n].shape)

    return (
        loss.reshape(()),
        grad_x[None],
        *[grads[n] for n in names],
        *[delta[n] for n in names],
        *[new_m[n] for n in names],
        *[new_v[n] for n in names],
    )
```

```python
import jax
import jax.numpy as jnp
from jax import lax
from jax.experimental import pallas as pl
from jax.experimental.pallas import tpu as pltpu

F32 = jnp.float32
BF16 = jnp.bfloat16

RMS_EPS = 1e-6
LN_EPS = 1e-5
RG_C = 8.0
CHUNK = 128
A_GROUPS = 8
B_HEADS = 12
CONV_WIDTH = 4

ADAM_LR = 0.001
ADAM_B1 = 0.9
ADAM_B2 = 0.999
ADAM_EPS = 1e-08
ADAM_WD = 0.01
ADAM_STEP = 10

N_CHIPS = 4
N_DEV = 8
SUBLANES = 8
LANES = 128
V7X_VMEM_BYTES = 64 * 1024 * 1024
VMEM_LIMIT = V7X_VMEM_BYTES * 7 // 8
MESH = pl.DeviceIdType.MESH
ANY = pl.BlockSpec(memory_space=pl.ANY)

TM_FWD = 256
TM_A_BWD = 256
TM_A_DX = 512

GELU_C0 = 0.7978845608028654
GELU_C1 = 0.044715


class _Hook:
    def __init__(self, operands, out_shapes, aliases, n_sems, start, finish, middle=None, late=None):
        self.operands, self.out_shapes, self.aliases, self.n_sems = operands, out_shapes, aliases, n_sems
        self.start, self.finish, self.middle, self.late = start, finish, middle, late


class _SemView:
    def __init__(self, base, off):
        self.base, self.off = base, off

    @property
    def at(self):
        return self

    def __getitem__(self, k):
        return self.base.at[self.off + k]


def _join_hooks(*hooks):
    if len(hooks) == 1:
        return hooks[0]
    operands, out_shapes, aliases, spans = [], [], {}, []
    n_sems = 0
    for h in hooks:
        aliases.update({len(operands) + i: len(out_shapes) + o for i, o in h.aliases.items()})
        spans.append((len(operands), len(h.operands), len(out_shapes), len(h.out_shapes), n_sems))
        operands += list(h.operands)
        out_shapes += list(h.out_shapes)
        n_sems += h.n_sems

    def each(which):
        def run(ins, outs, send, recv):
            for h, (i0, ni, o0, no, s0) in zip(hooks, spans):
                step = getattr(h, which)
                if step is not None:
                    step(ins[i0 : i0 + ni], outs[o0 : o0 + no], _SemView(send, s0), _SemView(recv, s0))

        return run

    middle = each("middle") if any(h.middle is not None for h in hooks) else None
    late = each("late") if any(h.late is not None for h in hooks) else None
    return _Hook(operands, out_shapes, aliases, n_sems, each("start"), each("finish"), middle, late)


def _pcall(body, hook=None, **kw):
    if hook is None:
        return pl.pallas_call(body, **kw)
    n_pre = 0
    if "grid_spec" in kw:
        spec = kw.pop("grid_spec")
        n_pre = spec.num_scalar_prefetch
        kw.update(
            grid=tuple(spec.grid), in_specs=list(spec.in_specs), out_specs=list(spec.out_specs),
            scratch_shapes=list(spec.scratch_shapes),
        )
    n_in, n_out = len(kw["in_specs"]), len(kw["out_shape"])
    hi, ho = len(hook.operands), len(hook.out_shapes)
    grid = kw.get("grid", ())

    def wrapped(*refs):
        pre, refs = refs[:n_pre], refs[n_pre:]
        ins, h_in = refs[:n_in], refs[n_in : n_in + hi]
        outs = refs[n_in + hi : n_in + hi + n_out]
        h_out = refs[n_in + hi + n_out : n_in + hi + n_out + ho]
        scratch = refs[n_in + hi + n_out + ho : -2]
        send_sems, recv_sems = refs[-2:]
        if not grid:
            hook.start(h_in, h_out, send_sems, recv_sems)
            if hook.middle is not None:
                hook.middle(h_in, h_out, send_sems, recv_sems)
            body(*pre, *ins, *outs, *scratch)
            if hook.late is not None:
                hook.late(h_in, h_out, send_sems, recv_sems)
            hook.finish(h_in, h_out, send_sems, recv_sems)
            return
        first = pl.program_id(0) == 0
        last = pl.program_id(0) == grid[0] - 1
        for axis in range(1, len(grid)):
            first = jnp.logical_and(first, pl.program_id(axis) == 0)
            last = jnp.logical_and(last, pl.program_id(axis) == grid[axis] - 1)

        @pl.when(first)
        def _():
            hook.start(h_in, h_out, send_sems, recv_sems)

        for when, step in ((hook.middle, grid[0] // 4), (hook.late, grid[0] - 1)):
            if when is not None:
                assert len(grid) == 1 and grid[0] >= 4

                @pl.when(pl.program_id(0) == step)
                def _(when=when):
                    when(h_in, h_out, send_sems, recv_sems)

        body(*pre, *ins, *outs, *scratch)

        @pl.when(last)
        def _():
            hook.finish(h_in, h_out, send_sems, recv_sems)

    aliases = dict(kw.pop("input_output_aliases", {}))
    aliases.update({n_pre + n_in + i: n_out + o for i, o in hook.aliases.items()})
    kw.update(
        in_specs=list(kw["in_specs"]) + [ANY] * hi,
        out_specs=list(kw["out_specs"]) + [ANY] * ho,
        out_shape=list(kw["out_shape"]) + list(hook.out_shapes),
        scratch_shapes=list(kw.get("scratch_shapes", ()))
        + [pltpu.SemaphoreType.DMA((hook.n_sems,)), pltpu.SemaphoreType.DMA((hook.n_sems,))],
        input_output_aliases=aliases,
    )
    if n_pre:
        kw["grid_spec"] = pltpu.PrefetchScalarGridSpec(
            num_scalar_prefetch=n_pre, grid=kw.pop("grid"), in_specs=kw.pop("in_specs"),
            out_specs=kw.pop("out_specs"), scratch_shapes=kw.pop("scratch_shapes"),
        )
    call = pl.pallas_call(wrapped, **kw)

    def run(*operands):
        outs = call(*operands, *hook.operands)
        return outs[:n_out], outs[n_out:]

    return run


def _run_hook(hook, name):
    def body():
        pass

    return _pcall(body, hook, name=name, in_specs=[], out_specs=[], out_shape=[])()[1]


def _cparams(sem=None):
    return pltpu.CompilerParams(dimension_semantics=sem, vmem_limit_bytes=VMEM_LIMIT)


def _full(shape):
    zeros = (0,) * len(shape)
    return pl.BlockSpec(shape, lambda *_: zeros)


def _scalars(*vals):
    return jnp.stack([jnp.asarray(v, jnp.int32) for v in vals])


def _sigmoid(x):
    return 1.0 / (1.0 + jnp.exp(-x))


def _gelu(x):
    t = jnp.tanh(GELU_C0 * (x + GELU_C1 * (x * x * x)))
    return x * (0.5 * (1.0 + t))


def _gelu_and_grad(x):
    x2 = x * x
    t = jnp.tanh(GELU_C0 * (x + GELU_C1 * (x2 * x)))
    cdf = 0.5 * (1.0 + t)
    return x * cdf, cdf + 0.5 * x * (1.0 - t * t) * (GELU_C0 * (1.0 + 3.0 * GELU_C1 * x2))


def _silu_and_grad(x):
    s = _sigmoid(x)
    return x * s, s * (1.0 + x * (1.0 - s))


def _softplus_neg(lam):
    u = jnp.exp(-jnp.abs(lam))
    w = 1.0 + u
    log1p = jnp.where(w == 1.0, u, jnp.log(w) * (u / jnp.where(w == 1.0, 1.0, w - 1.0)))
    return jnp.maximum(-lam, 0.0) + log1p


def _dot(a, b):
    return jnp.dot(a, b, preferred_element_type=F32)


def _dot_nt(a, b):
    return lax.dot_general(a, b, (((1,), (1,)), ((), ())), preferred_element_type=F32)


def _dot_tn(a, b):
    return lax.dot_general(a, b, (((0,), (0,)), ((), ())), preferred_element_type=F32)


def _shift_down(v, halo, k):
    if k == 0:
        return v
    rolled = pltpu.roll(v, k, 0)
    row = lax.broadcasted_iota(jnp.int32, (SUBLANES, v.shape[1]), 0)
    top = jnp.where(row < k, pltpu.roll(halo, k, 0), rolled[:SUBLANES])
    return jnp.concatenate([top, rolled[SUBLANES:]], axis=0)


def _shift_up(v, head, k):
    if k == 0:
        return v
    n = v.shape[0]
    rolled = pltpu.roll(v, n - k, 0)
    row = lax.broadcasted_iota(jnp.int32, (SUBLANES, v.shape[1]), 0)
    bot = jnp.where(row >= SUBLANES - k, pltpu.roll(head, SUBLANES - k, 0), rolled[n - SUBLANES :])
    return jnp.concatenate([rolled[: n - SUBLANES], bot], axis=0)


def _scan_blocks(a_ref, b_ref, out_ref, carry, n_rows, reverse):
    width = a_ref.shape[1]
    row = lax.broadcasted_iota(jnp.int32, (SUBLANES, width), 0)
    n_blocks = n_rows // SUBLANES

    def block(j, carry):
        i = (n_blocks - 1 - j) if reverse else j
        r0 = pl.multiple_of(i * SUBLANES, SUBLANES)
        a = a_ref[pl.ds(r0, SUBLANES), :]
        b = b_ref[pl.ds(r0, SUBLANES), :]
        for d in (1, 2, 4):
            shift = (SUBLANES - d) if reverse else d
            keep = (row < SUBLANES - d) if reverse else (row >= d)
            a_s = pltpu.roll(a, shift, 0)
            b_s = pltpu.roll(b, shift, 0)
            b = jnp.where(keep, a * b_s + b, b)
            a = jnp.where(keep, a * a_s, a)
        h = a * carry + b
        out_ref[pl.ds(r0, SUBLANES), :] = h
        edge = h[0:1, :] if reverse else h[SUBLANES - 1 : SUBLANES, :]
        return jnp.broadcast_to(edge, (SUBLANES, width))

    return lax.fori_loop(0, n_blocks, block, carry)


def _rms_fwd(x, w):
    r = lax.rsqrt(jnp.mean(x * x, axis=-1, keepdims=True) + RMS_EPS)
    xh = x * r
    return xh * w, xh, r


def _rms_bwd(dh, xh, r, w):
    dxh = dh * w
    dx = r * (dxh - xh * jnp.mean(dxh * xh, axis=-1, keepdims=True))
    return dx, jnp.sum(dh * xh, axis=0, keepdims=True)


def _cast_to_segments(ws, mine, steps, hook=None, small_maps=None):
    per = steps // 2
    n = len(ws)
    maps = () if small_maps is None else small_maps

    def body(k_ref, *refs):
        for w_ref, o_ref in zip(refs[:n], refs[n + len(maps) : 2 * n + len(maps)]):
            o_ref[...] = w_ref[...].astype(BF16)
        if small_maps is not None:
            pl.when(pl.program_id(0) == 0)(lambda: _prepare_small_maps(*refs[n : n + 3], *refs[2 * n + 3 :]))

    rows = [w.shape[0] // steps for w in ws]
    segment = lambda i, k_ref: (2 * k_ref[0] + i // per, i % per, 0)
    whole = lambda shape: pl.BlockSpec(shape, lambda i, k_ref: (0,) * len(shape))
    prepared = []
    if small_maps is not None:
        (g, ck, _), (h, hd, _) = maps[0].shape, maps[1].shape
        prepared = [(g, ck, ck), (g, ck, ck), (h, hd, 2 * hd), (h, 2 * hd, hd)]
    out = _pcall(
        body,
        hook,
        name=f"cast_{ws[0].shape[0]}x{ws[0].shape[1]}",
        grid_spec=pltpu.PrefetchScalarGridSpec(
            num_scalar_prefetch=1,
            grid=(steps,),
            in_specs=[pl.BlockSpec((r, w.shape[1]), lambda i, k_ref: (i, 0)) for w, r in zip(ws, rows)]
            + [whole(m.shape) for m in maps],
            out_specs=[pl.BlockSpec((None, r, w.shape[1]), segment) for w, r in zip(ws, rows)]
            + [whole(shape) for shape in prepared],
        ),
        out_shape=[jax.ShapeDtypeStruct((N_DEV, w.shape[0] // 2, w.shape[1]), BF16) for w in ws]
        + [jax.ShapeDtypeStruct(shape, BF16) for shape in prepared],
        compiler_params=_cparams(("arbitrary",)),
    )(_scalars(mine), *ws, *maps)
    return out


def _prepare_small_maps(ws_ref, ga_ref, gx_ref, wc_ref, wct_ref, gab_ref, gabt_ref):
    ck, hd = ws_ref.shape[1], ga_ref.shape[1]
    tril = lax.broadcasted_iota(jnp.int32, (ck, ck), 0) >= lax.broadcasted_iota(jnp.int32, (ck, ck), 1)
    for g in range(ws_ref.shape[0]):
        w = ws_ref[g] * tril.astype(F32)
        wc_ref[g] = w.astype(BF16)
        wct_ref[g] = w.T.astype(BF16)
    for h in range(ga_ref.shape[0]):
        for k, m_ref in enumerate((ga_ref, gx_ref)):
            m = m_ref[h]
            gab_ref[h, :, k * hd : (k + 1) * hd] = m.astype(BF16)
            gabt_ref[h, k * hd : (k + 1) * hd, :] = m.T.astype(BF16)


def _place():
    x, y, c = lax.axis_index("x"), lax.axis_index("y"), lax.axis_index("c")
    chips = [(1 - x, y), (x, 1 - y), (1 - x, 1 - y)]
    return x, y, c, chips


def _chip_no(chip):
    return 2 * chip[0] + chip[1]


def _rcopy(src, dst, send_sem, recv_sem, to):
    return pltpu.make_async_remote_copy(
        src_ref=src, dst_ref=dst, send_sem=send_sem, recv_sem=recv_sem, device_id=to, device_id_type=MESH
    )


def _gather_hook(big, small=None):
    nb = len(big)
    n_sems = 6 * nb + 4

    def places():
        x, y, c, chips = _place()
        first = (x ^ (1 - c), y ^ c)
        second = (x ^ c, y ^ (1 - c))
        return x, y, c, chips, first, second, (1 - x, 1 - y)

    def seg(outs, b, chip, half):
        return outs[b].at[2 * _chip_no(chip) + half]

    def step1(outs, send, recv):
        x, y, c, _, first, _, _ = places()
        return [
            _rcopy(seg(outs, b, (x, y), c), seg(outs, b, (x, y), c), send.at[6 * b], recv.at[6 * b], (*first, c))
            for b in range(nb)
        ]

    def step2(outs, send, recv):
        x, y, c, _, first, second, _ = places()
        copies = []
        for b in range(nb):
            for k, chip in ((1, (x, y)), (2, first)):
                src = seg(outs, b, chip, c)
                copies.append(_rcopy(src, src, send.at[6 * b + k], recv.at[6 * b + k], (*second, c)))
        return copies

    def hand_over(outs, send, recv, k, chip):
        x, y, c, *_ = places()
        return [
            _rcopy(seg(outs, b, chip, c), seg(outs, b, chip, c), send.at[6 * b + k], recv.at[6 * b + k], (x, y, 1 - c))
            for b in range(nb)
        ]

    def wait_landed(outs, send, recv, k, chip, half):
        x, y, c, *_ = places()
        for b in range(nb):
            got = seg(outs, b, chip, half)
            _rcopy(got, got, send.at[6 * b + k], recv.at[6 * b + k], (x, y, c)).wait_recv()

    def small_copies(ins, outs, send, recv):
        x, y, c, chips, *_ = places()
        there = outs[nb].at[_chip_no((x, y))]
        return [
            _rcopy(ins[nb], there, send.at[6 * nb + j], recv.at[6 * nb + j], (*chip, c)) for j, chip in enumerate(chips)
        ]

    def local_copy(ins, outs, send):
        x, y, _, _ = _place()
        return pltpu.make_async_copy(ins[nb], outs[nb].at[_chip_no((x, y))], send.at[6 * nb + 3])

    def start(ins, outs, send, recv):
        for cp in step1(outs, send, recv):
            cp.start()
        if small is not None:
            for cp in small_copies(ins, outs, send, recv):
                cp.start()
            local_copy(ins, outs, send).start()

    def middle(ins, outs, send, recv):
        *_, first, _, _ = places()
        wait_landed(outs, send, recv, 0, first, places()[2])
        for cp in step2(outs, send, recv) + hand_over(outs, send, recv, 3, first):
            cp.start()

    def late(ins, outs, send, recv):
        x, y, c, chips, first, second, diagonal = places()
        for k, chip in ((1, second), (2, diagonal)):
            wait_landed(outs, send, recv, k, chip, c)
            for cp in hand_over(outs, send, recv, 3 + k, chip):
                cp.start()

    def finish(ins, outs, send, recv):
        x, y, c, chips, first, second, diagonal = places()
        wait_landed(outs, send, recv, 3, second, 1 - c)
        wait_landed(outs, send, recv, 4, first, 1 - c)
        wait_landed(outs, send, recv, 5, diagonal, 1 - c)
        sent = step1(outs, send, recv) + step2(outs, send, recv)
        for k, chip in ((3, first), (4, second), (5, diagonal)):
            sent += hand_over(outs, send, recv, k, chip)
        for cp in sent:
            cp.wait_send()
        if small is not None:
            for j, chip in enumerate(chips):
                got = outs[nb].at[_chip_no(chip)]
                _rcopy(got, got, send.at[6 * nb + j], recv.at[6 * nb + j], (x, y, c)).wait_recv()
            for cp in small_copies(ins, outs, send, recv):
                cp.wait_send()
            local_copy(ins, outs, send).wait()

    operands = list(big) + ([small] if small is not None else [])
    out_shapes = [jax.ShapeDtypeStruct(b.shape, b.dtype) for b in big]
    if small is not None:
        out_shapes.append(jax.ShapeDtypeStruct((N_CHIPS, *small.shape), small.dtype))
    return _Hook(operands, out_shapes, {b: b for b in range(nb)}, n_sems, start, finish, middle, late)


def _both_ways_hook(operands, out_shapes, copies_of, n_sems):
    def start(ins, outs, send, recv):
        for cp in copies_of(ins, outs, send, recv):
            cp.start()

    def finish(ins, outs, send, recv):
        for cp in copies_of(ins, outs, send, recv):
            cp.wait()

    return _Hook(operands, out_shapes, {}, n_sems, start, finish)


def _swap_hook(bufs):
    def copies_of(ins, outs, send, recv):
        x, y, c, _ = _place()
        copies = []
        for b in range(len(bufs)):
            for j in range(N_CHIPS):
                k = b * N_CHIPS + j
                copies.append(_rcopy(ins[b].at[2 * j + 1 - c], outs[b].at[j], send.at[k], recv.at[k], (x, y, 1 - c)))
        return copies

    out_shapes = [jax.ShapeDtypeStruct((N_CHIPS, *b.shape[1:]), b.dtype) for b in bufs]
    return _both_ways_hook(list(bufs), out_shapes, copies_of, len(bufs) * N_CHIPS)


def _axis_order():
    x, y, c, _ = _place()
    return (x, y), c, (x ^ (1 - c), y ^ c), (x ^ c, y ^ (1 - c)), (1 - x, 1 - y)


def _send_first_hook(parts):
    def copies_of(ins, outs, send, recv):
        _, c, first, _, _ = _axis_order()
        copies = []
        for b in range(len(parts)):
            for k in range(2):
                sem = 2 * b + k
                copies.append(_rcopy(ins[b].at[k], outs[b].at[k], send.at[sem], recv.at[sem], (*first, c)))
        return copies

    out_shapes = [jax.ShapeDtypeStruct((2, *p.shape[1:]), p.dtype) for p in parts]
    return _both_ways_hook(list(parts), out_shapes, copies_of, len(parts) * 2)


def _send_second_hook(mids):
    def copies_of(ins, outs, send, recv):
        _, c, _, second, _ = _axis_order()
        return [_rcopy(ins[b], outs[b], send.at[b], recv.at[b], (*second, c)) for b in range(len(mids))]

    out_shapes = [jax.ShapeDtypeStruct(m.shape, m.dtype) for m in mids]
    return _both_ways_hook(list(mids), out_shapes, copies_of, len(mids))


def _share_hook(big, small=None, tiny=None):
    nb = len(big)
    n_sems = nb + 7 + N_DEV
    t0 = nb + 7

    def tiny_copies(ins, outs, send, recv):
        x, y, c, _ = _place()
        there = outs[-1].at[2 * _chip_no((x, y)) + c]
        copies = []
        for r in range(1, N_DEV):
            to = (x ^ (r >> 2 & 1), y ^ (r >> 1 & 1), c ^ (r & 1))
            copies.append(_rcopy(ins[-1], there, send.at[t0 + r], recv.at[t0 + r], to))
        return copies

    def tiny_local(ins, outs, send):
        x, y, c, _ = _place()
        return pltpu.make_async_copy(ins[-1], outs[-1].at[2 * _chip_no((x, y)) + c], send.at[t0])

    def first_copies(outs, send, recv):
        x, y, c, chips = _place()
        sibling = (x, y, 1 - c)
        copies = [_rcopy(outs[b].at[c], outs[b].at[c], send.at[b], recv.at[b], sibling) for b in range(nb)]
        if small is not None:
            own = outs[nb].at[2 * _chip_no((x, y)) + c]
            copies.append(_rcopy(own, own, send.at[nb], recv.at[nb], sibling))
            for j, chip in enumerate(chips):
                copies.append(_rcopy(own, own, send.at[nb + 1 + j], recv.at[nb + 1 + j], (*chip, c)))
        return copies

    def start(ins, outs, send, recv):
        for cp in first_copies(outs, send, recv):
            cp.start()
        if tiny is not None:
            for cp in tiny_copies(ins, outs, send, recv):
                cp.start()
            tiny_local(ins, outs, send).start()

    def finish(ins, outs, send, recv):
        x, y, c, chips = _place()
        me, sibling = (x, y, c), (x, y, 1 - c)
        if tiny is not None:
            for cp in tiny_copies(ins, outs, send, recv):
                cp.wait()
            tiny_local(ins, outs, send).wait()
        passed = []
        if small is not None:
            for j, chip in enumerate(chips):
                got = outs[nb].at[2 * _chip_no(chip) + c]
                _rcopy(got, got, send.at[nb + 1 + j], recv.at[nb + 1 + j], me).wait_recv()
                fwd = _rcopy(got, got, send.at[nb + 4 + j], recv.at[nb + 4 + j], sibling)
                fwd.start()
                passed.append(fwd)
        for b in range(nb):
            got = outs[b].at[1 - c]
            _rcopy(got, got, send.at[b], recv.at[b], me).wait_recv()
        if small is not None:
            got = outs[nb].at[2 * _chip_no((x, y)) + 1 - c]
            _rcopy(got, got, send.at[nb], recv.at[nb], me).wait_recv()
            for j, chip in enumerate(chips):
                got = outs[nb].at[2 * _chip_no(chip) + 1 - c]
                _rcopy(got, got, send.at[nb + 4 + j], recv.at[nb + 4 + j], me).wait_recv()
        for cp in first_copies(outs, send, recv) + passed:
            cp.wait_send()

    operands = list(big) + ([small] if small is not None else [])
    out_shapes = [jax.ShapeDtypeStruct(a.shape, a.dtype) for a in operands]
    aliases = {i: i for i in range(len(operands))}
    if tiny is not None:
        operands.append(tiny)
        out_shapes.append(jax.ShapeDtypeStruct((N_DEV, *tiny.shape), tiny.dtype))
    return _Hook(operands, out_shapes, aliases, n_sems, start, finish)


def _row_tile(rows, cols, target_bytes=2 * 1024 * 1024):
    best = SUBLANES
    for t in range(SUBLANES, rows + 1, SUBLANES):
        if rows % t == 0 and t * cols * 4 <= target_bytes:
            best = t
    return best


def _halves(buf):
    return buf.reshape(N_CHIPS, 2, *buf.shape[1:])


def _sum_jobs(jobs, name):
    main, riders = jobs[0], jobs[1:]
    rows, cols = main[0][0][0][0].shape[-2:]
    tr = _row_tile(rows, cols)
    n_main = len(main[0][0])
    scalars, in_specs, operands, out_specs, out_shape = [], [], [], [], []

    def put(lead):
        scalars.extend(lead)
        return len(scalars) - len(lead)

    starts = [[put(lead) for _, lead in group] for group in main[0]]
    out_starts = [put(lead) for lead in main[2]]
    stride = starts[1][0] - starts[0][0] if len(starts) > 1 else 0
    for (arr, lead), at in zip(main[0][0], starts[0]):
        operands.append(arr)
        in_specs.append(
            pl.BlockSpec(
                (None,) * len(lead) + (tr, cols),
                lambda j, r, s_ref, at=at, n=len(lead): (*[s_ref[at + j * stride + k] for k in range(n)], r, 0),
            )
        )
    nd = len(main[1])
    out_specs.append(
        pl.BlockSpec(
            (None,) * nd + (tr, cols),
            lambda j, r, s_ref, at=out_starts[0]: (*[s_ref[at + j * nd + k] for k in range(nd)], r, 0),
        )
    )
    out_shape.append(jax.ShapeDtypeStruct((*main[1], rows, cols), main[3]))

    for groups, out_dims, out_leads, dtype in riders:
        r_k, c_k = groups[0][0][0].shape[-2:]
        for group in groups:
            for arr, lead in group:
                at = put(lead)
                operands.append(arr)
                in_specs.append(
                    pl.BlockSpec(
                        (None,) * len(lead) + (r_k, c_k),
                        lambda j, r, s_ref, at=at, n=len(lead): (*[s_ref[at + k] for k in range(n)], 0, 0),
                    )
                )
        if len(groups) == 1:
            at, n_lead = put(out_leads[0]), len(out_dims)
            out_specs.append(
                pl.BlockSpec(
                    (None,) * n_lead + (r_k, c_k),
                    lambda j, r, s_ref, at=at, n=n_lead: (*[s_ref[at + k] for k in range(n)], 0, 0),
                )
            )
        else:
            assert out_dims == (len(groups),) and list(out_leads) == [(g,) for g in range(len(groups))]
            out_specs.append(pl.BlockSpec((len(groups), r_k, c_k), lambda j, r, s_ref: (0, 0, 0)))
        out_shape.append(jax.ShapeDtypeStruct((*out_dims, r_k, c_k), dtype))

    def total(term_refs):
        acc = term_refs[0][...].astype(F32)
        for t_ref in term_refs[1:]:
            acc = acc + t_ref[...].astype(F32)
        return acc

    def body(s_ref, *refs):
        ins, outs = refs[: len(operands)], refs[len(operands) :]
        outs[0][...] = total(ins[:n_main]).astype(main[3])

        def ride():
            at = n_main
            for (groups, _, _, dtype), o_ref in zip(riders, outs[1:]):
                for g, group in enumerate(groups):
                    acc = total(ins[at : at + len(group)]).astype(dtype)
                    at += len(group)
                    if len(groups) == 1:
                        o_ref[...] = acc
                    else:
                        o_ref[g] = acc

        if riders:
            pl.when(jnp.logical_and(pl.program_id(0) == 0, pl.program_id(1) == 0))(ride)

    return _pcall(
        body,
        name=name,
        grid_spec=pltpu.PrefetchScalarGridSpec(
            num_scalar_prefetch=1, grid=(len(main[0]), rows // tr), in_specs=in_specs, out_specs=out_specs
        ),
        out_shape=out_shape,
        compiler_params=_cparams(("arbitrary", "arbitrary")),
    )(_scalars(*scalars), *operands)


def _own_half_job(buf, got, wire, owners, c):
    groups = [[(_halves(buf), (j, c)), (got, (j,))] for j in owners]
    return groups, (len(owners),), [(k,) for k in range(len(owners))], wire


def _for_neighbour_job(buf, got, got1, wire, second, c):
    return [[(_halves(buf), (second, c)), (got, (second,)), (got1, (1,))]], (), [()], wire


def _received_job(buf, got, got1, got2, slot, n_slots, mine, c):
    return [[(_halves(buf), (mine, c)), (got, (mine,)), (got1, (0,)), (got2, ())]], (n_slots,), [(slot,)], F32


def _layer_a_fwd(x, nw, win, ln_w, ln_b, wc, bs_t, wout, tm, hook):
    t_rows, d = x.shape
    n_sh, _, s_cols = win.shape
    aw = wout.shape[0]
    gd = aw // A_GROUPS
    tn = 512
    assert s_cols % tn == 0 and aw % tn == 0 and tm % CHUNK == 0

    def body(x_ref, nw_ref, win_ref, lnw_ref, lnb_ref, wc_ref, bst_ref, wout_ref, z_ref, x1_ref, h_ref, u_s, v_s, y_s):
        x = x_ref[...]
        h, _, _ = _rms_fwd(x, nw_ref[...])
        h = h.astype(BF16)
        h_ref[...] = h
        for j in range(3 * aw // tn):
            k, off = divmod(j * tn, s_cols)
            cols = slice((j * tn) % aw, (j * tn) % aw + tn)
            zj = _dot(h, win_ref[k, :, off : off + tn])
            z_ref[:, j * tn : (j + 1) * tn] = zj
            if j * tn < aw:
                u_s[:, cols] = _gelu(zj)
            elif j * tn < 2 * aw:
                v_s[:, cols] = _gelu(zj)
            else:
                u_s[:, cols] = u_s[:, cols] * (zj * _sigmoid(zj))
        v = v_s[...]
        mu = jnp.mean(v, axis=-1, keepdims=True)
        vc = v - mu
        rstd = lax.rsqrt(jnp.mean(vc * vc, axis=-1, keepdims=True) + LN_EPS)
        v_s[...] = (vc * rstd) * lnw_ref[...] + lnb_ref[...]
        for ck in range(tm // CHUNK):
            rows = slice(ck * CHUNK, (ck + 1) * CHUNK)
            for g in range(A_GROUPS):
                cols = slice(g * gd, (g + 1) * gd)
                s = _dot(wc_ref[g], v_s[rows, cols].astype(BF16)) + bst_ref[:, g : g + 1]
                y_s[rows, cols] = (u_s[rows, cols] * s).astype(BF16)
        x1_ref[...] = x + _dot(y_s[...], wout_ref[...])

    row = lambda i: (i, 0)
    return _pcall(
        body,
        hook,
        name="layer_a_fwd",
        grid=(t_rows // tm,),
        in_specs=[
            pl.BlockSpec((tm, d), row),
            _full(nw.shape),
            _full(win.shape),
            _full(ln_w.shape),
            _full(ln_b.shape),
            _full(wc.shape),
            _full(bs_t.shape),
            _full(wout.shape),
        ],
        out_specs=[pl.BlockSpec((tm, 3 * aw), row), pl.BlockSpec((tm, d), row), pl.BlockSpec((tm, d), row)],
        out_shape=[
            jax.ShapeDtypeStruct((t_rows, 3 * aw), F32),
            jax.ShapeDtypeStruct((t_rows, d), F32),
            jax.ShapeDtypeStruct((t_rows, d), BF16),
        ],
        scratch_shapes=[pltpu.VMEM((tm, aw), F32), pltpu.VMEM((tm, aw), F32), pltpu.VMEM((tm, aw), BF16)],
        compiler_params=_cparams(("arbitrary",)),
    )(x, nw, win, ln_w, ln_b, wc, bs_t, wout)


def _layer_a_bwd(dout, z, ln_w, ln_b, wc, wct, bs_t, wout, tiles, earlier, hook):
    t_rows, d = dout.shape
    aw = wout.shape[0]
    gd = aw // A_GROUPS
    tm = TM_A_BWD
    lo, hi = tiles
    n_earlier = 0 if earlier is None else len(earlier)

    def body(dout_ref, z_ref, lnw_ref, lnb_ref, wc_ref, wct_ref, bst_ref, wout_ref, *rest):
        dz_ref, y_ref, dob_ref, gws_ref, gbs_ref, glnw_ref, glnb_ref, u_s, vh_s, ds_s, dvn_s = rest[n_earlier:]

        @pl.when(pl.program_id(0) == 0)
        def _():
            gws_ref[...] = jnp.zeros_like(gws_ref)
            gbs_ref[...] = jnp.zeros_like(gbs_ref)
            glnw_ref[...] = jnp.zeros_like(glnw_ref)
            glnb_ref[...] = jnp.zeros_like(glnb_ref)

        dob = dout_ref[...].astype(BF16)
        dob_ref[...] = dob
        dy = _dot_nt(dob, wout_ref[...])

        zv = z_ref[:, aw : 2 * aw]
        vg, dvg_dz = _gelu_and_grad(zv)
        mu = jnp.mean(vg, axis=-1, keepdims=True)
        vc = vg - mu
        rstd = lax.rsqrt(jnp.mean(vc * vc, axis=-1, keepdims=True) + LN_EPS)
        vh = vc * rstd
        vh_s[...] = vh
        vn = (vh * lnw_ref[...] + lnb_ref[...]).astype(BF16)

        zu = z_ref[:, 0:aw]
        zg = z_ref[:, 2 * aw : 3 * aw]
        u, du_dz = _gelu_and_grad(zu)
        sg, dsg = _silu_and_grad(zg)
        u_s[...] = u * sg
        tril = lax.broadcasted_iota(jnp.int32, (CHUNK, CHUNK), 0) >= lax.broadcasted_iota(jnp.int32, (CHUNK, CHUNK), 1)
        for ck in range(tm // CHUNK):
            rows = slice(ck * CHUNK, (ck + 1) * CHUNK)
            for g in range(A_GROUPS):
                cols = slice(g * gd, (g + 1) * gd)
                vn_g = vn[rows, cols]
                s = _dot(wc_ref[g], vn_g) + bst_ref[:, g : g + 1]
                usg = u_s[rows, cols]
                dy_g = dy[rows, cols]
                y_ref[rows, cols] = (usg * s).astype(BF16)
                ds = dy_g * usg
                ds_s[rows, cols] = dy_g * s
                gbs_ref[:, g : g + 1] += jnp.sum(ds, axis=-1, keepdims=True)
                dsb = ds.astype(BF16)
                gws_ref[g] += jnp.where(tril, _dot_nt(dsb, vn_g), 0.0)
                dvn_s[rows, cols] = _dot(wct_ref[g], dsb)
        dusg = ds_s[...]
        dz_ref[:, 0:aw] = (dusg * sg * du_dz).astype(BF16)
        dz_ref[:, 2 * aw : 3 * aw] = (dusg * u * dsg).astype(BF16)

        dvn = dvn_s[...]
        vh = vh_s[...]
        glnw_ref[...] += jnp.sum(dvn * vh, axis=0, keepdims=True)
        glnb_ref[...] += jnp.sum(dvn, axis=0, keepdims=True)
        dvh = dvn * lnw_ref[...]
        dvg = rstd * (dvh - jnp.mean(dvh, axis=-1, keepdims=True) - vh * jnp.mean(dvh * vh, axis=-1, keepdims=True))
        dz_ref[:, aw : 2 * aw] = (dvg * dvg_dz).astype(BF16)

    row = lambda i: (i + lo, 0)
    call = _pcall(
        body,
        hook,
        name=f"layer_a_bwd_{lo}",
        grid=(hi - lo,),
        in_specs=[
            pl.BlockSpec((tm, d), row),
            pl.BlockSpec((tm, 3 * aw), row),
            _full(ln_w.shape),
            _full(ln_b.shape),
            _full(wc.shape),
            _full(wct.shape),
            _full(bs_t.shape),
            _full(wout.shape),
        ]
        + [ANY] * n_earlier,
        out_specs=[
            pl.BlockSpec((tm, 3 * aw), row),
            pl.BlockSpec((tm, aw), row),
            pl.BlockSpec((tm, d), row),
            _full((A_GROUPS, CHUNK, CHUNK)),
            _full((CHUNK, A_GROUPS)),
            _full((1, aw)),
            _full((1, aw)),
        ],
        out_shape=[
            jax.ShapeDtypeStruct((t_rows, 3 * aw), BF16),
            jax.ShapeDtypeStruct((t_rows, aw), BF16),
            jax.ShapeDtypeStruct((t_rows, d), BF16),
            jax.ShapeDtypeStruct((A_GROUPS, CHUNK, CHUNK), F32),
            jax.ShapeDtypeStruct((CHUNK, A_GROUPS), F32),
            jax.ShapeDtypeStruct((1, aw), F32),
            jax.ShapeDtypeStruct((1, aw), F32),
        ],
        scratch_shapes=[pltpu.VMEM((tm, aw), F32)] * 4,
        input_output_aliases={8 + i: i for i in range(n_earlier)},
        compiler_params=_cparams(("arbitrary",)),
    )
    return call(dout, z, ln_w, ln_b, wc, wct, bs_t, wout, *(earlier or ()))


def _layer_a_bwd_dx(dout, x, dz, nw, win, tm, tiles, earlier, hook):
    t_rows, d = x.shape
    n_sh, _, s_cols = win.shape
    lo, hi = tiles
    n_earlier = 0 if earlier is None else 1

    def body(dout_ref, x_ref, dz_ref, nw_ref, win_ref, *rest):
        gx_ref, gnw_ref = rest[n_earlier:]

        @pl.when(pl.program_id(0) == 0)
        def _():
            gnw_ref[...] = jnp.zeros_like(gnw_ref)

        dh = jnp.zeros((tm, d), F32)
        for k in range(n_sh):
            dh = dh + _dot_nt(dz_ref[:, k * s_cols : (k + 1) * s_cols], win_ref[k])
        nw = nw_ref[...]
        _, xh, r = _rms_fwd(x_ref[...], nw)
        dx, gnw = _rms_bwd(dh, xh, r, nw)
        gnw_ref[0:1, :] += gnw
        gx_ref[...] = dout_ref[...] + dx

    row = lambda i: (i + lo, 0)
    return _pcall(
        body,
        hook,
        name=f"layer_a_bwd_dx_{lo}",
        grid=(hi - lo,),
        in_specs=[
            pl.BlockSpec((tm, d), row),
            pl.BlockSpec((tm, d), row),
            pl.BlockSpec((tm, n_sh * s_cols), row),
            _full(nw.shape),
            _full(win.shape),
        ]
        + [ANY] * n_earlier,
        out_specs=[pl.BlockSpec((tm, d), row), _full((SUBLANES, d))],
        out_shape=[jax.ShapeDtypeStruct((t_rows, d), F32), jax.ShapeDtypeStruct((SUBLANES, d), F32)],
        input_output_aliases={5: 0} if n_earlier else {},
        compiler_params=_cparams(("arbitrary",)),
    )(dout, x, dz, nw, win, *([earlier] if n_earlier else []))


def _decay(r, sp_h):
    log_a = (-RG_C) * r * sp_h
    a = jnp.exp(log_a)
    mult = jnp.sqrt(jnp.tanh(-log_a) * (a * a + 1.0))
    return a, mult


ROW_CONV_B, ROW_GATE_A_B, ROW_GATE_X_B, ROW_LAMBDA = range(CONV_WIDTH, CONV_WIDTH + 4)


def _gates(xc_h, gab_ref, vec_ref, sp_h, h, hd):
    pre = _dot(xc_h.astype(BF16), gab_ref[h])
    cols = slice(h * hd, (h + 1) * hd)
    r = _sigmoid(pre[:, :hd] + vec_ref[ROW_GATE_A_B : ROW_GATE_A_B + 1, cols])
    ig = _sigmoid(pre[:, hd:] + vec_ref[ROW_GATE_X_B : ROW_GATE_X_B + 1, cols])
    a, mult = _decay(r, sp_h)
    return r, ig, a, mult


def _conv(xb, halo, vec_ref):
    xc = vec_ref[ROW_CONV_B : ROW_CONV_B + 1, :] + vec_ref[CONV_WIDTH - 1 : CONV_WIDTH, :] * xb
    for k in range(CONV_WIDTH - 1):
        xc = xc + vec_ref[k : k + 1, :] * _shift_down(xb, halo, CONV_WIDTH - 1 - k)
    return xc


def _layer_b_fwd(x1, nw, bin_w, vec, gab, bout, nf, tgt, tm):
    t_rows, d = x1.shape
    bw = bout.shape[0]
    hd = bw // B_HEADS
    nt = t_rows // tm

    def body(
        x1_ref, nw_ref, bin_ref, vec_ref, gab_ref, bout_ref, nf_ref, tgt_ref,
        z_ref, h_ref, h1_ref, dx2_ref, loss_ref, gnf_ref,
        tail_s, carry_s, a_s, b_s, hs_s, acc_s,
    ):
        @pl.when(pl.program_id(0) == 0)
        def _():
            tail_s[...] = jnp.zeros_like(tail_s)
            carry_s[...] = jnp.zeros_like(carry_s)
            acc_s[...] = jnp.zeros_like(acc_s)
            gnf_ref[...] = jnp.zeros_like(gnf_ref)

        x1 = x1_ref[...]
        h1, _, _ = _rms_fwd(x1, nw_ref[...])
        h1 = h1.astype(BF16)
        h1_ref[...] = h1
        z = jnp.concatenate([_dot(h1, bin_ref[k]) for k in range(N_CHIPS)], axis=1)
        z_ref[...] = z
        xb = z[:, :bw]
        xc = _conv(xb, tail_s[...], vec_ref)
        tail = xb[tm - SUBLANES :, :]
        tail_s[...] = tail
        sp = _softplus_neg(vec_ref[ROW_LAMBDA : ROW_LAMBDA + 1, :])
        for h in range(B_HEADS):
            cols = slice(h * hd, (h + 1) * hd)
            xc_h = xc[:, cols]
            _, ig, a, mult = _gates(xc_h, gab_ref, vec_ref, sp[:, cols], h, hd)
            a_s[:, cols] = a
            b_s[:, cols] = mult * (ig * xc_h)
        carry = _scan_blocks(a_s, b_s, hs_s, carry_s[...], tm, reverse=False)
        carry_s[...] = carry
        hs = hs_s[...]
        h_ref[...] = hs
        g = z[:, bw:]
        y = (hs * (g * _sigmoid(g))).astype(BF16)
        x2 = x1 + _dot(y, bout_ref[...])

        nf = nf_ref[...]
        o, xh, r = _rms_fwd(x2, nf)
        diff = o - tgt_ref[...]
        acc_s[...] += jnp.sum(diff * diff, axis=0, keepdims=True)
        do = diff * (1.0 / d)
        dx2, gnf = _rms_bwd(do, xh, r, nf)
        gnf_ref[...] += gnf
        dx2_ref[...] = dx2

        @pl.when(pl.program_id(0) == nt - 1)
        def _():
            total = jnp.sum(acc_s[...], axis=-1, keepdims=True) * (0.5 / d)
            loss_ref[...] = jnp.broadcast_to(total, loss_ref.shape)

    row = lambda i: (i, 0)
    return _pcall(
        body,
        name="layer_b_fwd",
        grid=(nt,),
        in_specs=[
            pl.BlockSpec((tm, d), row),
            _full(nw.shape),
            _full(bin_w.shape),
            _full(vec.shape),
            _full(gab.shape),
            _full(bout.shape),
            _full(nf.shape),
            pl.BlockSpec((tm, d), row),
        ],
        out_specs=[
            pl.BlockSpec((tm, 2 * bw), row),
            pl.BlockSpec((tm, bw), row),
            pl.BlockSpec((tm, d), row),
            pl.BlockSpec((tm, d), row),
            _full((1, LANES)),
            _full((1, d)),
        ],
        out_shape=[
            jax.ShapeDtypeStruct((t_rows, 2 * bw), F32),
            jax.ShapeDtypeStruct((t_rows, bw), F32),
            jax.ShapeDtypeStruct((t_rows, d), BF16),
            jax.ShapeDtypeStruct((t_rows, d), F32),
            jax.ShapeDtypeStruct((1, LANES), F32),
            jax.ShapeDtypeStruct((1, d), F32),
        ],
        scratch_shapes=[
            pltpu.VMEM((SUBLANES, bw), F32),
            pltpu.VMEM((SUBLANES, bw), F32),
            pltpu.VMEM((tm, bw), F32),
            pltpu.VMEM((tm, bw), F32),
            pltpu.VMEM((tm, bw), F32),
            pltpu.VMEM((1, d), F32),
        ],
        compiler_params=_cparams(("arbitrary",)),
    )(x1, nw, bin_w, vec, gab, bout, nf, tgt)


def _layer_b_bwd(dout, x1, z, hseq, nw, bin_w, vec, gab, gabt, bout, tm):
    t_rows, d = x1.shape
    bw = bout.shape[0]
    hd = bw // B_HEADS
    nt = t_rows // tm

    def body(
        dout_ref, x1_ref, z_ref, h_ref, xbt_ref, ht_ref, nw_ref, bin_ref, vec_ref, gab_ref, gabt_ref, bout_ref,
        dx1_ref, dz_ref, y_ref, dob_ref, ggab_ref, ggb_ref, gcw_ref, gcb_ref, glam_ref, gnw_ref,
        gcarry_s, afirst_s, head_s, aup_s, dh_s, gt_s, dxc_s, xc_s, r_s, ig_s,
    ):
        step = pl.program_id(0)
        tile = nt - 1 - step

        @pl.when(step == 0)
        def _():
            for ref in (ggab_ref, ggb_ref, gcw_ref, gcb_ref, glam_ref, gnw_ref, gcarry_s, afirst_s, head_s):
                ref[...] = jnp.zeros_like(ref)

        first_tile = tile == 0
        xb_halo = jnp.where(first_tile, 0.0, xbt_ref[...])
        h_halo = jnp.where(first_tile, 0.0, ht_ref[...])

        dout = dout_ref[...]
        dob = dout.astype(BF16)
        dob_ref[...] = dob
        dy = _dot_nt(dob, bout_ref[...])
        hs = h_ref[...]
        g = z_ref[:, bw:]
        sg, dsg = _silu_and_grad(g)
        y_ref[...] = (hs * sg).astype(BF16)
        dz_ref[:, bw:] = (dy * hs * dsg).astype(BF16)
        dh_s[...] = dy * sg

        xb = z_ref[:, :bw]
        xc = _conv(xb, xb_halo, vec_ref)
        xc_s[...] = xc
        lam = vec_ref[ROW_LAMBDA : ROW_LAMBDA + 1, :]
        sp = _softplus_neg(lam)
        for h in range(B_HEADS):
            cols = slice(h * hd, (h + 1) * hd)
            r, ig, a, _ = _gates(xc[:, cols], gab_ref, vec_ref, sp[:, cols], h, hd)
            r_s[:, cols] = r
            ig_s[:, cols] = ig
            aup_s[:, cols] = _shift_up(a, afirst_s[:, cols], 1)
            afirst_s[:, cols] = jnp.broadcast_to(a[0:1, :], (SUBLANES, hd))
        carry = _scan_blocks(aup_s, dh_s, gt_s, gcarry_s[...], tm, reverse=True)
        gcarry_s[...] = carry

        h_prev = _shift_down(hs, h_halo, 1)
        for h in range(B_HEADS):
            cols = slice(h * hd, (h + 1) * hd)
            xc_h = xc_s[:, cols]
            sp_h = sp[:, cols]
            r, ig = r_s[:, cols], ig_s[:, cols]
            a, mult = _decay(r, sp_h)
            gt = gt_s[:, cols]
            da = gt * h_prev[:, cols]
            dmult = gt * (ig * xc_h)
            dig = gt * (mult * xc_h)
            dxc_direct = gt * (mult * ig)
            dla = da * a - dmult * (a * a) / mult
            glam_ref[:, cols] += jnp.sum(dla * r, axis=0, keepdims=True)
            dr = dla * ((-RG_C) * sp_h)
            dpre = jnp.concatenate([dr * r * (1.0 - r), dig * ig * (1.0 - ig)], axis=1)
            ggb_ref[:, cols] += jnp.sum(dpre[:, :hd], axis=0, keepdims=True)
            ggb_ref[:, bw + h * hd : bw + (h + 1) * hd] += jnp.sum(dpre[:, hd:], axis=0, keepdims=True)
            dpb = dpre.astype(BF16)
            ggab_ref[h] += _dot_tn(xc_h.astype(BF16), dpb)
            dxc_s[:, cols] = dxc_direct + _dot(dpb, gabt_ref[h])
        glam_ref[...] = jnp.where(step == nt - 1, glam_ref[...] * (RG_C * _sigmoid(-lam)), glam_ref[...])

        dxc = dxc_s[...]
        gcb_ref[...] += jnp.sum(dxc, axis=0, keepdims=True)
        dxb = vec_ref[CONV_WIDTH - 1 : CONV_WIDTH, :] * dxc
        gcw_ref[CONV_WIDTH - 1 : CONV_WIDTH, :] += jnp.sum(dxc * xb, axis=0, keepdims=True)
        head = head_s[...]
        for k in range(CONV_WIDTH - 1):
            lag = CONV_WIDTH - 1 - k
            dxb = dxb + vec_ref[k : k + 1, :] * _shift_up(dxc, head, lag)
            gcw_ref[k : k + 1, :] += jnp.sum(dxc * _shift_down(xb, xb_halo, lag), axis=0, keepdims=True)
        head_s[...] = dxc[:SUBLANES, :]
        dz_ref[:, :bw] = dxb.astype(BF16)

        s_cols = 2 * bw // N_CHIPS
        dh1 = jnp.zeros((tm, d), F32)
        for k in range(N_CHIPS):
            dh1 = dh1 + _dot_nt(dz_ref[:, k * s_cols : (k + 1) * s_cols], bin_ref[k])
        x1 = x1_ref[...]
        nw = nw_ref[...]
        _, xh, r1 = _rms_fwd(x1, nw)
        dx, gnw = _rms_bwd(dh1, xh, r1, nw)
        gnw_ref[...] += gnw
        dx1_ref[...] = dout + dx

    rev = lambda i: (nt - 1 - i, 0)
    prev = lambda i: (jnp.maximum((nt - 1 - i) * (tm // SUBLANES) - 1, 0), 0)
    return _pcall(
        body,
        name="layer_b_bwd",
        grid=(nt,),
        in_specs=[
            pl.BlockSpec((tm, d), rev),
            pl.BlockSpec((tm, d), rev),
            pl.BlockSpec((tm, 2 * bw), rev),
            pl.BlockSpec((tm, bw), rev),
            pl.BlockSpec((SUBLANES, bw), prev),
            pl.BlockSpec((SUBLANES, bw), prev),
            _full(nw.shape),
            _full(bin_w.shape),
            _full(vec.shape),
            _full(gab.shape),
            _full(gabt.shape),
            _full(bout.shape),
        ],
        out_specs=[
            pl.BlockSpec((tm, d), rev),
            pl.BlockSpec((tm, 2 * bw), rev),
            pl.BlockSpec((tm, bw), rev),
            pl.BlockSpec((tm, d), rev),
            _full((B_HEADS, hd, 2 * hd)),
            _full((1, 2 * bw)),
            _full((SUBLANES, bw)),
            _full((1, bw)),
            _full((1, bw)),
            _full((1, d)),
        ],
        out_shape=[
            jax.ShapeDtypeStruct((t_rows, d), F32),
            jax.ShapeDtypeStruct((t_rows, 2 * bw), BF16),
            jax.ShapeDtypeStruct((t_rows, bw), BF16),
            jax.ShapeDtypeStruct((t_rows, d), BF16),
            jax.ShapeDtypeStruct((B_HEADS, hd, 2 * hd), F32),
            jax.ShapeDtypeStruct((1, 2 * bw), F32),
            jax.ShapeDtypeStruct((SUBLANES, bw), F32),
            jax.ShapeDtypeStruct((1, bw), F32),
            jax.ShapeDtypeStruct((1, bw), F32),
            jax.ShapeDtypeStruct((1, d), F32),
        ],
        scratch_shapes=[pltpu.VMEM((SUBLANES, bw), F32)] * 3 + [pltpu.VMEM((tm, bw), F32)] * 7,
        compiler_params=_cparams(("arbitrary",)),
    )(dout, x1, z, hseq, z, hseq, nw, bin_w, vec, gab, gabt, bout)


def _wgrad(a, b, m_blocks, n_blocks, hook=None, wire_copy=False):
    k, m = a.shape
    n = b.shape[1]
    bm, bn = m // m_blocks, n // n_blocks

    def body(a_ref, b_ref, o_ref, *wire_ref):
        prod = _dot_tn(a_ref[...], b_ref[...])
        o_ref[...] = prod
        if wire_copy:
            wire_ref[0][...] = prod.astype(BF16)

    out_spec = pl.BlockSpec((None, None, bm, bn), lambda j, i: (j, i, 0, 0))
    shape = (n_blocks, m_blocks, bm, bn)
    out = _pcall(
        body,
        hook,
        name=f"wgrad_{m}x{n}",
        grid=(n_blocks, m_blocks),
        in_specs=[pl.BlockSpec((k, bm), lambda j, i: (0, i)), pl.BlockSpec((k, bn), lambda j, i: (0, j))],
        out_specs=[out_spec] * (1 + wire_copy),
        out_shape=[jax.ShapeDtypeStruct(shape, F32)] + [jax.ShapeDtypeStruct(shape, BF16)] * wire_copy,
        compiler_params=_cparams(("arbitrary", "arbitrary")),
    )(a, b)
    outs, rode = (out, None) if hook is None else out
    outs = outs if wire_copy else outs[0]
    return outs if hook is None else (outs, rode)


def _adamw_math(w, g, m, v):
    m = ADAM_B1 * m + (1.0 - ADAM_B1) * g
    v = ADAM_B2 * v + (1.0 - ADAM_B2) * (g * g)
    m_hat = m / (1.0 - ADAM_B1**ADAM_STEP)
    v_hat = v / (1.0 - ADAM_B2**ADAM_STEP)
    delta = -ADAM_LR * (m_hat / (jnp.sqrt(v_hat) + ADAM_EPS) + ADAM_WD * w)
    return delta, m, v


ADAMW_ROW_TILES = 8


def _adamw_rows(ws, gs, ms, vs):
    n = len(ws)

    def body(*refs):
        ins, outs = refs[: 4 * n], refs[4 * n :]
        for i in range(n):
            w_ref, g_ref, m_ref, v_ref = ins[i::n]
            d_ref, mo_ref, vo_ref, go_ref = outs[i::n]
            g = g_ref[...]
            d_ref[...], mo_ref[...], vo_ref[...] = _adamw_math(w_ref[...], g, m_ref[...], v_ref[...])
            go_ref[...] = g

    specs = [pl.BlockSpec((w.shape[0] // ADAMW_ROW_TILES, w.shape[1]), lambda i: (i, 0)) for w in ws]
    outs = _pcall(
        body,
        name="adamw_rows",
        grid=(ADAMW_ROW_TILES,),
        in_specs=specs * 4,
        out_specs=specs * 4,
        out_shape=[jax.ShapeDtypeStruct(w.shape, F32) for w in ws] * 4,
        compiler_params=_cparams(("arbitrary",)),
    )(*ws, *gs, *ms, *vs)
    return [outs[k * n : (k + 1) * n] for k in range(4)]


def _sum_partials(parts):
    def body(p_ref, o_ref):
        total = p_ref[0, 0:1, :]
        for k in range(1, N_DEV):
            total = total + p_ref[k, 0:1, :]
        o_ref[...] = total

    vmem = pl.BlockSpec(memory_space=pltpu.VMEM)
    return _pcall(
        body,
        name="sum_partials",
        in_specs=[vmem],
        out_specs=vmem,
        out_shape=jax.ShapeDtypeStruct((1, parts.shape[2]), F32),
    )(parts)


def _adamw_many(ws, gs, ms, vs):
    n = len(ws)

    def body(*refs):
        w_refs, g_refs, m_refs, v_refs = (refs[i * n : (i + 1) * n] for i in range(4))
        d_refs, mo_refs, vo_refs = (refs[(4 + i) * n : (5 + i) * n] for i in range(3))
        for i in range(n):
            d_refs[i][...], mo_refs[i][...], vo_refs[i][...] = _adamw_math(
                w_refs[i][...], g_refs[i][...], m_refs[i][...], v_refs[i][...]
            )

    vmem = pl.BlockSpec(memory_space=pltpu.VMEM)
    outs = _pcall(
        body,
        name="adamw_small",
        in_specs=[vmem] * (4 * n),
        out_specs=[vmem] * (3 * n),
        out_shape=[jax.ShapeDtypeStruct(w.shape, F32) for w in ws] * 3,
        compiler_params=_cparams(),
    )(*ws, *gs, *ms, *vs)
    return outs[:n], outs[n : 2 * n], outs[2 * n :]


def _pack_rows(parts, lanes=LANES):
    flat = jnp.concatenate([p.reshape(-1) for p in parts])
    per = N_DEV * SUBLANES * lanes
    total = -(-flat.shape[0] // per) * per
    flat = jnp.pad(flat, (0, total - flat.shape[0]))
    return flat.reshape(N_DEV, total // (N_DEV * lanes), lanes)


def _unpack(flat, shapes):
    out, at = [], 0
    for s in shapes:
        n = 1
        for dim in s:
            n *= dim
        out.append(flat[at : at + n].reshape(s))
        at += n
    return out


def kernel(x, norm_w, a_w_in, a_ln_w, a_ln_b, a_w_s, a_b_s, a_w_out, b_w_in, b_conv_w, b_conv_b, b_gate_a_w, b_gate_a_b, b_gate_x_w, b_gate_x_b, b_lambda, b_w_out, norm_f_w, loss_target, m_norm_w, m_a_w_in, m_a_ln_w, m_a_ln_b, m_a_w_s, m_a_b_s, m_a_w_out, m_b_w_in, m_b_conv_w, m_b_conv_b, m_b_gate_a_w, m_b_gate_a_b, m_b_gate_x_w, m_b_gate_x_b, m_b_lambda, m_b_w_out, m_norm_f_w, v_norm_w, v_a_w_in, v_a_ln_w, v_a_ln_b, v_a_w_s, v_a_b_s, v_a_w_out, v_b_w_in, v_b_conv_w, v_b_conv_b, v_b_gate_a_w, v_b_gate_a_b, v_b_gate_x_w, v_b_gate_x_b, v_b_lambda, v_b_w_out, v_norm_f_w):
    t_rows, d = x.shape[1], x.shape[2]
    aw = a_ln_w.shape[1]
    bw = b_gate_a_w.shape[1] * b_gate_a_w.shape[2]
    hd = bw // B_HEADS
    mine = 2 * lax.axis_index("x") + lax.axis_index("y")
    core = lax.axis_index("c")
    weights = dict(norm_w=norm_w, a_w_in=a_w_in, a_ln_w=a_ln_w, a_ln_b=a_ln_b, a_w_s=a_w_s, a_b_s=a_b_s, a_w_out=a_w_out, b_w_in=b_w_in, b_conv_w=b_conv_w, b_conv_b=b_conv_b, b_gate_a_w=b_gate_a_w, b_gate_a_b=b_gate_a_b, b_gate_x_w=b_gate_x_w, b_gate_x_b=b_gate_x_b, b_lambda=b_lambda, b_w_out=b_w_out, norm_f_w=norm_f_w)
    m_in = dict(norm_w=m_norm_w, a_w_in=m_a_w_in, a_ln_w=m_a_ln_w, a_ln_b=m_a_ln_b, a_w_s=m_a_w_s, a_b_s=m_a_b_s, a_w_out=m_a_w_out, b_w_in=m_b_w_in, b_conv_w=m_b_conv_w, b_conv_b=m_b_conv_b, b_gate_a_w=m_b_gate_a_w, b_gate_a_b=m_b_gate_a_b, b_gate_x_w=m_b_gate_x_w, b_gate_x_b=m_b_gate_x_b, b_lambda=m_b_lambda, b_w_out=m_b_w_out, norm_f_w=m_norm_f_w)
    v_in = dict(norm_w=v_norm_w, a_w_in=v_a_w_in, a_ln_w=v_a_ln_w, a_ln_b=v_a_ln_b, a_w_s=v_a_w_s, a_b_s=v_a_b_s, a_w_out=v_a_w_out, b_w_in=v_b_w_in, b_conv_w=v_b_conv_w, b_conv_b=v_b_conv_b, b_gate_a_w=v_b_gate_a_w, b_gate_a_b=v_b_gate_a_b, b_gate_x_w=v_b_gate_x_w, b_gate_x_b=v_b_gate_x_b, b_lambda=v_b_lambda, b_w_out=v_b_w_out, norm_f_w=v_norm_f_w)

    win_l, wout_l = _cast_to_segments([a_w_in[0], a_w_out[0]], mine, 4)
    small_l = jnp.concatenate([b_conv_w[0], b_conv_b, b_gate_a_b, b_gate_x_b, b_lambda], axis=0)
    (bin_l, bout_l, wc, wct, gab, gabt), (win_g, wout_g, small_g) = _cast_to_segments(
        [b_w_in[0], b_w_out[0]], mine, 8, _gather_hook([win_l, wout_l], small_l),
        (a_w_s[0], b_gate_a_w[0], b_gate_x_w[0]),
    )
    win = win_g.reshape(N_CHIPS, d, -1)
    wout = wout_g.reshape(aw, d)
    bs_t = a_b_s[0].T
    nw0, nw1, nf = norm_w[0:1], norm_w[1:2], norm_f_w.reshape(1, d)

    x0 = x[0]
    (z_a, x1, h0), (bin_g, bout_g) = _layer_a_fwd(
        x0, nw0, win, a_ln_w, a_ln_b, wc, bs_t, wout, TM_FWD, _gather_hook([bin_l, bout_l])
    )
    bin_w = bin_g.reshape(N_CHIPS, d, -1)
    bout = bout_g.reshape(bw, d)
    vec = jnp.transpose(small_g, (1, 0, 2)).reshape(SUBLANES, bw)
    z_b, hseq, h1, dx2, loss_l, g_nf = _layer_b_fwd(x1, nw1, bin_w, vec, gab, bout, nf, loss_target[0], TM_FWD)
    dx1, dz_b, y_b, dob_b, g_gab, g_gb, g_cw, g_cb, g_lam, g_nw1 = _layer_b_bwd(
        dx2, x1, z_b, hseq, nw1, bin_w, vec, gab, gabt, bout, TM_FWD
    )
    seg = lambda g: g.reshape(N_DEV, -1, g.shape[3])
    x_at, y_at = lax.axis_index("x"), lax.axis_index("y")
    first_no = 2 * (x_at ^ (1 - core)) + (y_at ^ core)
    second_no = 2 * (x_at ^ core) + (y_at ^ (1 - core))
    own_half = lambda buf, got, wire: _own_half_job(buf, got, wire, (first_no, N_CHIPS - 1 - mine), core)
    for_neighbour = lambda buf, got, got1, wire: _for_neighbour_job(buf, got, got1, wire, second_no, core)
    received = lambda buf, got, got1, got2: _received_job(buf, got, got1, got2, core, 2, mine, core)

    g_o = seg(_wgrad(y_b, dob_b, 2, 1))
    g_i, (swap_o,) = _wgrad(h1, dz_b, 1, N_CHIPS, _swap_hook([g_o]))
    g_i = seg(g_i)
    (part_o,) = _sum_jobs([own_half(g_o, swap_o, BF16)], "add_own_half_o")
    a_args = (z_a, a_ln_w, a_ln_b, wc, wct, bs_t, wout)
    half = t_rows // TM_A_BWD // 2
    first, (swap_i, got1_o) = _layer_a_bwd(
        dx1, *a_args, (0, half), None, _join_hooks(_swap_hook([g_i]), _send_first_hook([part_o]))
    )
    part_i, mid_o = _sum_jobs(
        [own_half(g_i, swap_i, BF16), for_neighbour(g_o, swap_o, got1_o, BF16)], "add_own_half_i_for_neighbour_o"
    )
    second, (got1_i, got2_o) = _layer_a_bwd(
        dx1, *a_args, (half, 2 * half), first[:3], _join_hooks(_send_first_hook([part_i]), _send_second_hook([mid_o]))
    )
    dz_a, y_a, dob_a = second[:3]
    g_ws, g_bst, g_lnw, g_lnb = (p + q for p, q in zip(first[3:], second[3:]))
    mid_i, red_o = _sum_jobs(
        [for_neighbour(g_i, swap_i, got1_i, BF16), received(g_o, swap_o, got1_o, got2_o)],
        "add_for_neighbour_i_received_o",
    )
    (g_w, g_w_wire), (got2_i, gr_bout) = _wgrad(
        h0, dz_a, 1, N_CHIPS, _join_hooks(_send_second_hook([mid_i]), _share_hook([red_o])), wire_copy=True
    )
    g_w, g_w_wire = seg(g_w), seg(g_w_wire)
    (red_i,) = _sum_jobs([received(g_i, swap_i, got1_i, got2_i)], "add_received_i")

    small_shapes = [
        (1, d), (1, aw), (1, aw), (A_GROUPS, CHUNK, CHUNK), (A_GROUPS, CHUNK), (B_HEADS, hd, hd), (B_HEADS, hd, hd),
        (d,), (CONV_WIDTH, bw), (1, bw), (1, bw), (1, bw), (1, bw), (1, 1),
    ]
    small = _pack_rows(
        [
            g_nw1, g_lnw, g_lnb, g_ws, g_bst.T, g_gab[:, :, :hd], g_gab[:, :, hd:],
            g_nf, g_cw[:CONV_WIDTH], g_cb, g_gb[:, :bw], g_gb[:, bw:], g_lam, loss_l[:, :1],
        ]
    )
    g_u, (swap_w, swap_s, gr_bin) = _wgrad(
        y_a, dob_a, N_CHIPS, 1, _join_hooks(_swap_hook([g_w_wire, small]), _share_hook([red_i]))
    )
    g_u = seg(g_u)

    part_w, part_s = _sum_jobs([own_half(g_w, swap_w, BF16), own_half(small, swap_s, F32)], "add_own_half_w")
    (grad_x, g_nw0_mine), (got1_w, got1_s, swap_u) = _layer_a_bwd_dx(
        dx1, x0, dz_a, nw0, win, TM_A_DX, (0, t_rows // TM_A_DX), None,
        _join_hooks(_send_first_hook([part_w, part_s]), _swap_hook([g_u])),
    )
    mid_w, mid_s, part_u = _sum_jobs(
        [
            for_neighbour(g_w, swap_w, got1_w, BF16), for_neighbour(small, swap_s, got1_s, F32),
            own_half(g_u, swap_u, BF16),
        ],
        "add_for_neighbour_w_own_half_u",
    )
    got2_w, got2_s, got1_u = _run_hook(
        _join_hooks(_send_second_hook([mid_w, mid_s]), _send_first_hook([part_u])), "second_axis"
    )
    red_w, red_s, mid_u = _sum_jobs(
        [
            received(g_w, swap_w, got1_w, got2_w),
            _received_job(small, swap_s, got1_s, got2_s, 2 * mine + core, N_DEV, mine, core),
            for_neighbour(g_u, swap_u, got1_u, BF16),
        ],
        "add_received_w_for_neighbour_u",
    )
    got2_u, gr_win, small_r, g_nw0_all = _run_hook(
        _join_hooks(_send_second_hook([mid_u]), _share_hook([red_w], red_s, g_nw0_mine)), "second_axis_and_share"
    )
    (red_u,) = _sum_jobs([received(g_u, swap_u, got1_u, got2_u)], "add_received_u")
    (gr_wout,) = _run_hook(_share_hook([red_u]), "share_reduced")
    g_nw0 = _sum_partials(g_nw0_all)
    (g_nw1_r, g_a_ln_w, g_a_ln_b, g_a_w_s, g_a_b_s, g_gate_a_w, g_gate_x_w, g_norm_f, gf_cw, gf_cb, gf_gab, gf_gxb,
     gf_lam, loss) = _unpack(small_r.reshape(-1), small_shapes)
    g_norm_w = jnp.concatenate([g_nw0, g_nw1_r], axis=0)
    shard = lambda g: lax.dynamic_slice_in_dim(g, mine * (bw // N_CHIPS), bw // N_CHIPS, axis=1)

    grads = {
        "norm_w": g_norm_w, "a_w_in": gr_win, "a_ln_w": g_a_ln_w, "a_ln_b": g_a_ln_b, "a_w_s": g_a_w_s[None],
        "a_b_s": g_a_b_s[None], "a_w_out": gr_wout, "b_w_in": gr_bin, "b_conv_w": shard(gf_cw)[None],
        "b_conv_b": shard(gf_cb), "b_gate_a_w": g_gate_a_w[None], "b_gate_a_b": shard(gf_gab),
        "b_gate_x_w": g_gate_x_w[None], "b_gate_x_b": shard(gf_gxb), "b_lambda": shard(gf_lam),
        "b_w_out": gr_bout, "norm_f_w": g_norm_f,
    }
    names = list(weights)
    big_names = ["a_w_in", "a_w_out", "b_w_in", "b_w_out"]
    small_names = [n for n in names if n not in big_names]
    at_least_2d = lambda a: a.reshape(1, -1) if a.ndim == 1 else a
    small_out = _adamw_many(*[[at_least_2d(src[n]) for n in small_names] for src in (weights, grads, m_in, v_in)])
    *big_out, big_grads = _adamw_rows(
        *[[src[n].reshape(weights[n].shape[-2:]) for n in big_names] for src in (weights, grads, m_in, v_in)]
    )
    grads.update(zip(big_names, big_grads))
    delta, new_m, new_v = {}, {}, {}
    for dst, small_vals, big_vals in zip((delta, new_m, new_v), small_out, big_out):
        dst.update(zip(small_names, small_vals))
        dst.update(zip(big_names, big_vals))
    for dst in (grads, delta, new_m, new_v):
        for n in names:
            dst[n] = dst[n].reshape(weights[n].shape)

    return (
        loss.reshape(()),
        grad_x[None],
        *[grads[n] for n in names],
        *[delta[n] for n in names],
        *[new_m[n] for n in names],
        *[new_v[n] for n in names],
    )
```

```python
import jax
import jax.numpy as jnp
from jax import lax
from jax.experimental import pallas as pl
from jax.experimental.pallas import tpu as pltpu

F32 = jnp.float32
BF16 = jnp.bfloat16

RMS_EPS = 1e-6
LN_EPS = 1e-5
RG_C = 8.0
CHUNK = 128
A_GROUPS = 8
B_HEADS = 12
CONV_WIDTH = 4

ADAM_LR = 0.001
ADAM_B1 = 0.9
ADAM_B2 = 0.999
ADAM_EPS = 1e-08
ADAM_WD = 0.01
ADAM_STEP = 10

N_CHIPS = 4
N_DEV = 8
SUBLANES = 8
LANES = 128
V7X_VMEM_BYTES = 64 * 1024 * 1024
VMEM_LIMIT = V7X_VMEM_BYTES * 7 // 8
MESH = pl.DeviceIdType.MESH
ANY = pl.BlockSpec(memory_space=pl.ANY)

TM_FWD = 256
TM_A_BWD = 256
TM_A_DX = 512

GELU_C0 = 0.7978845608028654
GELU_C1 = 0.044715


class _Hook:
    def __init__(self, operands, out_shapes, aliases, n_sems, start, finish, middle=None, late=None):
        self.operands, self.out_shapes, self.aliases, self.n_sems = operands, out_shapes, aliases, n_sems
        self.start, self.finish, self.middle, self.late = start, finish, middle, late


class _SemView:
    def __init__(self, base, off):
        self.base, self.off = base, off

    @property
    def at(self):
        return self

    def __getitem__(self, k):
        return self.base.at[self.off + k]


def _join_hooks(*hooks):
    if len(hooks) == 1:
        return hooks[0]
    operands, out_shapes, aliases, spans = [], [], {}, []
    n_sems = 0
    for h in hooks:
        aliases.update({len(operands) + i: len(out_shapes) + o for i, o in h.aliases.items()})
        spans.append((len(operands), len(h.operands), len(out_shapes), len(h.out_shapes), n_sems))
        operands += list(h.operands)
        out_shapes += list(h.out_shapes)
        n_sems += h.n_sems

    def each(which):
        def run(ins, outs, send, recv):
            for h, (i0, ni, o0, no, s0) in zip(hooks, spans):
                step = getattr(h, which)
                if step is not None:
                    step(ins[i0 : i0 + ni], outs[o0 : o0 + no], _SemView(send, s0), _SemView(recv, s0))

        return run

    middle = each("middle") if any(h.middle is not None for h in hooks) else None
    late = each("late") if any(h.late is not None for h in hooks) else None
    return _Hook(operands, out_shapes, aliases, n_sems, each("start"), each("finish"), middle, late)


def _pcall(body, hook=None, **kw):
    if hook is None:
        return pl.pallas_call(body, **kw)
    n_pre = 0
    if "grid_spec" in kw:
        spec = kw.pop("grid_spec")
        n_pre = spec.num_scalar_prefetch
        kw.update(
            grid=tuple(spec.grid), in_specs=list(spec.in_specs), out_specs=list(spec.out_specs),
            scratch_shapes=list(spec.scratch_shapes),
        )
    n_in, n_out = len(kw["in_specs"]), len(kw["out_shape"])
    hi, ho = len(hook.operands), len(hook.out_shapes)
    grid = kw.get("grid", ())

    def wrapped(*refs):
        pre, refs = refs[:n_pre], refs[n_pre:]
        ins, h_in = refs[:n_in], refs[n_in : n_in + hi]
        outs = refs[n_in + hi : n_in + hi + n_out]
        h_out = refs[n_in + hi + n_out : n_in + hi + n_out + ho]
        scratch = refs[n_in + hi + n_out + ho : -2]
        send_sems, recv_sems = refs[-2:]
        if not grid:
            hook.start(h_in, h_out, send_sems, recv_sems)
            if hook.middle is not None:
                hook.middle(h_in, h_out, send_sems, recv_sems)
            body(*pre, *ins, *outs, *scratch)
            if hook.late is not None:
                hook.late(h_in, h_out, send_sems, recv_sems)
            hook.finish(h_in, h_out, send_sems, recv_sems)
            return
        first = pl.program_id(0) == 0
        last = pl.program_id(0) == grid[0] - 1
        for axis in range(1, len(grid)):
            first = jnp.logical_and(first, pl.program_id(axis) == 0)
            last = jnp.logical_and(last, pl.program_id(axis) == grid[axis] - 1)

        @pl.when(first)
        def _():
            hook.start(h_in, h_out, send_sems, recv_sems)

        for when, step in ((hook.middle, grid[0] // 4), (hook.late, grid[0] - 1)):
            if when is not None:
                assert len(grid) == 1 and grid[0] >= 4

                @pl.when(pl.program_id(0) == step)
                def _(when=when):
                    when(h_in, h_out, send_sems, recv_sems)

        body(*pre, *ins, *outs, *scratch)

        @pl.when(last)
        def _():
            hook.finish(h_in, h_out, send_sems, recv_sems)

    aliases = dict(kw.pop("input_output_aliases", {}))
    aliases.update({n_pre + n_in + i: n_out + o for i, o in hook.aliases.items()})
    kw.update(
        in_specs=list(kw["in_specs"]) + [ANY] * hi,
        out_specs=list(kw["out_specs"]) + [ANY] * ho,
        out_shape=list(kw["out_shape"]) + list(hook.out_shapes),
        scratch_shapes=list(kw.get("scratch_shapes", ()))
        + [pltpu.SemaphoreType.DMA((hook.n_sems,)), pltpu.SemaphoreType.DMA((hook.n_sems,))],
        input_output_aliases=aliases,
    )
    if n_pre:
        kw["grid_spec"] = pltpu.PrefetchScalarGridSpec(
            num_scalar_prefetch=n_pre, grid=kw.pop("grid"), in_specs=kw.pop("in_specs"),
            out_specs=kw.pop("out_specs"), scratch_shapes=kw.pop("scratch_shapes"),
        )
    call = pl.pallas_call(wrapped, **kw)

    def run(*operands):
        outs = call(*operands, *hook.operands)
        return outs[:n_out], outs[n_out:]

    return run


def _run_hook(hook, name):
    def body():
        pass

    return _pcall(body, hook, name=name, in_specs=[], out_specs=[], out_shape=[])()[1]


def _cparams(sem=None):
    return pltpu.CompilerParams(dimension_semantics=sem, vmem_limit_bytes=VMEM_LIMIT)


def _full(shape):
    zeros = (0,) * len(shape)
    return pl.BlockSpec(shape, lambda *_: zeros)


def _scalars(*vals):
    return jnp.stack([jnp.asarray(v, jnp.int32) for v in vals])


def _sigmoid(x):
    return 1.0 / (1.0 + jnp.exp(-x))


def _gelu(x):
    t = jnp.tanh(GELU_C0 * (x + GELU_C1 * (x * x * x)))
    return x * (0.5 * (1.0 + t))


def _gelu_and_grad(x):
    x2 = x * x
    t = jnp.tanh(GELU_C0 * (x + GELU_C1 * (x2 * x)))
    cdf = 0.5 * (1.0 + t)
    return x * cdf, cdf + 0.5 * x * (1.0 - t * t) * (GELU_C0 * (1.0 + 3.0 * GELU_C1 * x2))


def _silu_and_grad(x):
    s = _sigmoid(x)
    return x * s, s * (1.0 + x * (1.0 - s))


def _softplus_neg(lam):
    u = jnp.exp(-jnp.abs(lam))
    w = 1.0 + u
    log1p = jnp.where(w == 1.0, u, jnp.log(w) * (u / jnp.where(w == 1.0, 1.0, w - 1.0)))
    return jnp.maximum(-lam, 0.0) + log1p


def _dot(a, b):
    return jnp.dot(a, b, preferred_element_type=F32)


def _dot_nt(a, b):
    return lax.dot_general(a, b, (((1,), (1,)), ((), ())), preferred_element_type=F32)


def _dot_tn(a, b):
    return lax.dot_general(a, b, (((0,), (0,)), ((), ())), preferred_element_type=F32)


def _shift_down(v, halo, k):
    if k == 0:
        return v
    rolled = pltpu.roll(v, k, 0)
    row = lax.broadcasted_iota(jnp.int32, (SUBLANES, v.shape[1]), 0)
    top = jnp.where(row < k, pltpu.roll(halo, k, 0), rolled[:SUBLANES])
    return jnp.concatenate([top, rolled[SUBLANES:]], axis=0)


def _shift_up(v, head, k):
    if k == 0:
        return v
    n = v.shape[0]
    rolled = pltpu.roll(v, n - k, 0)
    row = lax.broadcasted_iota(jnp.int32, (SUBLANES, v.shape[1]), 0)
    bot = jnp.where(row >= SUBLANES - k, pltpu.roll(head, SUBLANES - k, 0), rolled[n - SUBLANES :])
    return jnp.concatenate([rolled[: n - SUBLANES], bot], axis=0)


def _scan_blocks(a_ref, b_ref, out_ref, carry, n_rows, reverse):
    width = a_ref.shape[1]
    row = lax.broadcasted_iota(jnp.int32, (SUBLANES, width), 0)
    n_blocks = n_rows // SUBLANES

    def block(j, carry):
        i = (n_blocks - 1 - j) if reverse else j
        r0 = pl.multiple_of(i * SUBLANES, SUBLANES)
        a = a_ref[pl.ds(r0, SUBLANES), :]
        b = b_ref[pl.ds(r0, SUBLANES), :]
        for d in (1, 2, 4):
            shift = (SUBLANES - d) if reverse else d
            keep = (row < SUBLANES - d) if reverse else (row >= d)
            a_s = pltpu.roll(a, shift, 0)
            b_s = pltpu.roll(b, shift, 0)
            b = jnp.where(keep, a * b_s + b, b)
            a = jnp.where(keep, a * a_s, a)
        h = a * carry + b
        out_ref[pl.ds(r0, SUBLANES), :] = h
        edge = h[0:1, :] if reverse else h[SUBLANES - 1 : SUBLANES, :]
        return jnp.broadcast_to(edge, (SUBLANES, width))

    return lax.fori_loop(0, n_blocks, block, carry)


def _rms_fwd(x, w):
    r = lax.rsqrt(jnp.mean(x * x, axis=-1, keepdims=True) + RMS_EPS)
    xh = x * r
    return xh * w, xh, r


def _rms_bwd(dh, xh, r, w):
    dxh = dh * w
    dx = r * (dxh - xh * jnp.mean(dxh * xh, axis=-1, keepdims=True))
    return dx, jnp.sum(dh * xh, axis=0, keepdims=True)


def _cast_to_segments(ws, mine, steps, hook=None, small_maps=None):
    per = steps // 2
    n = len(ws)
    maps = () if small_maps is None else small_maps

    def body(k_ref, *refs):
        for w_ref, o_ref in zip(refs[:n], refs[n + len(maps) : 2 * n + len(maps)]):
            o_ref[...] = w_ref[...].astype(BF16)
        if small_maps is not None:
            pl.when(pl.program_id(0) == 0)(lambda: _prepare_small_maps(*refs[n : n + 3], *refs[2 * n + 3 :]))

    rows = [w.shape[0] // steps for w in ws]
    segment = lambda i, k_ref: (2 * k_ref[0] + i // per, i % per, 0)
    whole = lambda shape: pl.BlockSpec(shape, lambda i, k_ref: (0,) * len(shape))
    prepared = []
    if small_maps is not None:
        (g, ck, _), (h, hd, _) = maps[0].shape, maps[1].shape
        prepared = [(g, ck, ck), (g, ck, ck), (h, hd, 2 * hd), (h, 2 * hd, hd)]
    out = _pcall(
        body,
        hook,
        name=f"cast_{ws[0].shape[0]}x{ws[0].shape[1]}",
        grid_spec=pltpu.PrefetchScalarGridSpec(
            num_scalar_prefetch=1,
            grid=(steps,),
            in_specs=[pl.BlockSpec((r, w.shape[1]), lambda i, k_ref: (i, 0)) for w, r in zip(ws, rows)]
            + [whole(m.shape) for m in maps],
            out_specs=[pl.BlockSpec((None, r, w.shape[1]), segment) for w, r in zip(ws, rows)]
            + [whole(shape) for shape in prepared],
        ),
        out_shape=[jax.ShapeDtypeStruct((N_DEV, w.shape[0] // 2, w.shape[1]), BF16) for w in ws]
        + [jax.ShapeDtypeStruct(shape, BF16) for shape in prepared],
        compiler_params=_cparams(("arbitrary",)),
    )(_scalars(mine), *ws, *maps)
    return out


def _prepare_small_maps(ws_ref, ga_ref, gx_ref, wc_ref, wct_ref, gab_ref, gabt_ref):
    ck, hd = ws_ref.shape[1], ga_ref.shape[1]
    tril = lax.broadcasted_iota(jnp.int32, (ck, ck), 0) >= lax.broadcasted_iota(jnp.int32, (ck, ck), 1)
    for g in range(ws_ref.shape[0]):
        w = ws_ref[g] * tril.astype(F32)
        wc_ref[g] = w.astype(BF16)
        wct_ref[g] = w.T.astype(BF16)
    for h in range(ga_ref.shape[0]):
        for k, m_ref in enumerate((ga_ref, gx_ref)):
            m = m_ref[h]
            gab_ref[h, :, k * hd : (k + 1) * hd] = m.astype(BF16)
            gabt_ref[h, k * hd : (k + 1) * hd, :] = m.T.astype(BF16)


def _place():
    x, y, c = lax.axis_index("x"), lax.axis_index("y"), lax.axis_index("c")
    chips = [(1 - x, y), (x, 1 - y), (1 - x, 1 - y)]
    return x, y, c, chips


def _chip_no(chip):
    return 2 * chip[0] + chip[1]


def _rcopy(src, dst, send_sem, recv_sem, to):
    return pltpu.make_async_remote_copy(
        src_ref=src, dst_ref=dst, send_sem=send_sem, recv_sem=recv_sem, device_id=to, device_id_type=MESH
    )


def _gather_hook(big, small=None):
    nb = len(big)
    n_sems = 6 * nb + 4

    def places():
        x, y, c, chips = _place()
        first = (x ^ (1 - c), y ^ c)
        second = (x ^ c, y ^ (1 - c))
        return x, y, c, chips, first, second, (1 - x, 1 - y)

    def seg(outs, b, chip, half):
        return outs[b].at[2 * _chip_no(chip) + half]

    def step1(outs, send, recv):
        x, y, c, _, first, _, _ = places()
        return [
            _rcopy(seg(outs, b, (x, y), c), seg(outs, b, (x, y), c), send.at[6 * b], recv.at[6 * b], (*first, c))
            for b in range(nb)
        ]

    def step2(outs, send, recv):
        x, y, c, _, first, second, _ = places()
        copies = []
        for b in range(nb):
            for k, chip in ((1, (x, y)), (2, first)):
                src = seg(outs, b, chip, c)
                copies.append(_rcopy(src, src, send.at[6 * b + k], recv.at[6 * b + k], (*second, c)))
        return copies

    def hand_over(outs, send, recv, k, chip):
        x, y, c, *_ = places()
        return [
            _rcopy(seg(outs, b, chip, c), seg(outs, b, chip, c), send.at[6 * b + k], recv.at[6 * b + k], (x, y, 1 - c))
            for b in range(nb)
        ]

    def wait_landed(outs, send, recv, k, chip, half):
        x, y, c, *_ = places()
        for b in range(nb):
            got = seg(outs, b, chip, half)
            _rcopy(got, got, send.at[6 * b + k], recv.at[6 * b + k], (x, y, c)).wait_recv()

    def small_copies(ins, outs, send, recv):
        x, y, c, chips, *_ = places()
        there = outs[nb].at[_chip_no((x, y))]
        return [
            _rcopy(ins[nb], there, send.at[6 * nb + j], recv.at[6 * nb + j], (*chip, c)) for j, chip in enumerate(chips)
        ]

    def local_copy(ins, outs, send):
        x, y, _, _ = _place()
        return pltpu.make_async_copy(ins[nb], outs[nb].at[_chip_no((x, y))], send.at[6 * nb + 3])

    def start(ins, outs, send, recv):
        for cp in step1(outs, send, recv):
            cp.start()
        if small is not None:
            for cp in small_copies(ins, outs, send, recv):
                cp.start()
            local_copy(ins, outs, send).start()

    def middle(ins, outs, send, recv):
        *_, first, _, _ = places()
        wait_landed(outs, send, recv, 0, first, places()[2])
        for cp in step2(outs, send, recv) + hand_over(outs, send, recv, 3, first):
            cp.start()

    def late(ins, outs, send, recv):
        x, y, c, chips, first, second, diagonal = places()
        for k, chip in ((1, second), (2, diagonal)):
            wait_landed(outs, send, recv, k, chip, c)
            for cp in hand_over(outs, send, recv, 3 + k, chip):
                cp.start()

    def finish(ins, outs, send, recv):
        x, y, c, chips, first, second, diagonal = places()
        wait_landed(outs, send, recv, 3, second, 1 - c)
        wait_landed(outs, send, recv, 4, first, 1 - c)
        wait_landed(outs, send, recv, 5, diagonal, 1 - c)
        sent = step1(outs, send, recv) + step2(outs, send, recv)
        for k, chip in ((3, first), (4, second), (5, diagonal)):
            sent += hand_over(outs, send, recv, k, chip)
        for cp in sent:
            cp.wait_send()
        if small is not None:
            for j, chip in enumerate(chips):
                got = outs[nb].at[_chip_no(chip)]
                _rcopy(got, got, send.at[6 * nb + j], recv.at[6 * nb + j], (x, y, c)).wait_recv()
            for cp in small_copies(ins, outs, send, recv):
                cp.wait_send()
            local_copy(ins, outs, send).wait()

    operands = list(big) + ([small] if small is not None else [])
    out_shapes = [jax.ShapeDtypeStruct(b.shape, b.dtype) for b in big]
    if small is not None:
        out_shapes.append(jax.ShapeDtypeStruct((N_CHIPS, *small.shape), small.dtype))
    return _Hook(operands, out_shapes, {b: b for b in range(nb)}, n_sems, start, finish, middle, late)


def _both_ways_hook(operands, out_shapes, copies_of, n_sems):
    def start(ins, outs, send, recv):
        for cp in copies_of(ins, outs, send, recv):
            cp.start()

    def finish(ins, outs, send, recv):
        for cp in copies_of(ins, outs, send, recv):
            cp.wait()

    return _Hook(operands, out_shapes, {}, n_sems, start, finish)


def _swap_hook(bufs):
    def copies_of(ins, outs, send, recv):
        x, y, c, _ = _place()
        copies = []
        for b in range(len(bufs)):
            for j in range(N_CHIPS):
                k = b * N_CHIPS + j
                copies.append(_rcopy(ins[b].at[2 * j + 1 - c], outs[b].at[j], send.at[k], recv.at[k], (x, y, 1 - c)))
        return copies

    out_shapes = [jax.ShapeDtypeStruct((N_CHIPS, *b.shape[1:]), b.dtype) for b in bufs]
    return _both_ways_hook(list(bufs), out_shapes, copies_of, len(bufs) * N_CHIPS)


def _axis_order():
    x, y, c, _ = _place()
    return (x, y), c, (x ^ (1 - c), y ^ c), (x ^ c, y ^ (1 - c)), (1 - x, 1 - y)


def _send_first_hook(parts):
    def copies_of(ins, outs, send, recv):
        _, c, first, _, _ = _axis_order()
        copies = []
        for b in range(len(parts)):
            for k in range(2):
                sem = 2 * b + k
                copies.append(_rcopy(ins[b].at[k], outs[b].at[k], send.at[sem], recv.at[sem], (*first, c)))
        return copies

    out_shapes = [jax.ShapeDtypeStruct((2, *p.shape[1:]), p.dtype) for p in parts]
    return _both_ways_hook(list(parts), out_shapes, copies_of, len(parts) * 2)


def _send_second_hook(mids):
    def copies_of(ins, outs, send, recv):
        _, c, _, second, _ = _axis_order()
        return [_rcopy(ins[b], outs[b], send.at[b], recv.at[b], (*second, c)) for b in range(len(mids))]

    out_shapes = [jax.ShapeDtypeStruct(m.shape, m.dtype) for m in mids]
    return _both_ways_hook(list(mids), out_shapes, copies_of, len(mids))


def _share_hook(big, small=None, tiny=None):
    nb = len(big)
    n_sems = nb + 7 + N_DEV
    t0 = nb + 7

    def tiny_copies(ins, outs, send, recv):
        x, y, c, _ = _place()
        there = outs[-1].at[2 * _chip_no((x, y)) + c]
        copies = []
        for r in range(1, N_DEV):
            to = (x ^ (r >> 2 & 1), y ^ (r >> 1 & 1), c ^ (r & 1))
            copies.append(_rcopy(ins[-1], there, send.at[t0 + r], recv.at[t0 + r], to))
        return copies

    def tiny_local(ins, outs, send):
        x, y, c, _ = _place()
        return pltpu.make_async_copy(ins[-1], outs[-1].at[2 * _chip_no((x, y)) + c], send.at[t0])

    def first_copies(outs, send, recv):
        x, y, c, chips = _place()
        sibling = (x, y, 1 - c)
        copies = [_rcopy(outs[b].at[c], outs[b].at[c], send.at[b], recv.at[b], sibling) for b in range(nb)]
        if small is not None:
            own = outs[nb].at[2 * _chip_no((x, y)) + c]
            copies.append(_rcopy(own, own, send.at[nb], recv.at[nb], sibling))
            for j, chip in enumerate(chips):
                copies.append(_rcopy(own, own, send.at[nb + 1 + j], recv.at[nb + 1 + j], (*chip, c)))
        return copies

    def start(ins, outs, send, recv):
        for cp in first_copies(outs, send, recv):
            cp.start()
        if tiny is not None:
            for cp in tiny_copies(ins, outs, send, recv):
                cp.start()
            tiny_local(ins, outs, send).start()

    def finish(ins, outs, send, recv):
        x, y, c, chips = _place()
        me, sibling = (x, y, c), (x, y, 1 - c)
        if tiny is not None:
            for cp in tiny_copies(ins, outs, send, recv):
                cp.wait()
            tiny_local(ins, outs, send).wait()
        passed = []
        if small is not None:
            for j, chip in enumerate(chips):
                got = outs[nb].at[2 * _chip_no(chip) + c]
                _rcopy(got, got, send.at[nb + 1 + j], recv.at[nb + 1 + j], me).wait_recv()
                fwd = _rcopy(got, got, send.at[nb + 4 + j], recv.at[nb + 4 + j], sibling)
                fwd.start()
                passed.append(fwd)
        for b in range(nb):
            got = outs[b].at[1 - c]
            _rcopy(got, got, send.at[b], recv.at[b], me).wait_recv()
        if small is not None:
            got = outs[nb].at[2 * _chip_no((x, y)) + 1 - c]
            _rcopy(got, got, send.at[nb], recv.at[nb], me).wait_recv()
            for j, chip in enumerate(chips):
                got = outs[nb].at[2 * _chip_no(chip) + 1 - c]
                _rcopy(got, got, send.at[nb + 4 + j], recv.at[nb + 4 + j], me).wait_recv()
        for cp in first_copies(outs, send, recv) + passed:
            cp.wait_send()

    operands = list(big) + ([small] if small is not None else [])
    out_shapes = [jax.ShapeDtypeStruct(a.shape, a.dtype) for a in operands]
    aliases = {i: i for i in range(len(operands))}
    if tiny is not None:
        operands.append(tiny)
        out_shapes.append(jax.ShapeDtypeStruct((N_DEV, *tiny.shape), tiny.dtype))
    return _Hook(operands, out_shapes, aliases, n_sems, start, finish)


def _row_tile(rows, cols, target_bytes=2 * 1024 * 1024):
    best = SUBLANES
    for t in range(SUBLANES, rows + 1, SUBLANES):
        if rows % t == 0 and t * cols * 4 <= target_bytes:
            best = t
    return best


def _halves(buf):
    return buf.reshape(N_CHIPS, 2, *buf.shape[1:])


def _sum_jobs(jobs, name):
    main, riders = jobs[0], jobs[1:]
    rows, cols = main[0][0][0][0].shape[-2:]
    tr = _row_tile(rows, cols)
    n_main = len(main[0][0])
    scalars, in_specs, operands, out_specs, out_shape = [], [], [], [], []

    def put(lead):
        scalars.extend(lead)
        return len(scalars) - len(lead)

    starts = [[put(lead) for _, lead in group] for group in main[0]]
    out_starts = [put(lead) for lead in main[2]]
    stride = starts[1][0] - starts[0][0] if len(starts) > 1 else 0
    for (arr, lead), at in zip(main[0][0], starts[0]):
        operands.append(arr)
        in_specs.append(
            pl.BlockSpec(
                (None,) * len(lead) + (tr, cols),
                lambda j, r, s_ref, at=at, n=len(lead): (*[s_ref[at + j * stride + k] for k in range(n)], r, 0),
            )
        )
    nd = len(main[1])
    out_specs.append(
        pl.BlockSpec(
            (None,) * nd + (tr, cols),
            lambda j, r, s_ref, at=out_starts[0]: (*[s_ref[at + j * nd + k] for k in range(nd)], r, 0),
        )
    )
    out_shape.append(jax.ShapeDtypeStruct((*main[1], rows, cols), main[3]))

    for groups, out_dims, out_leads, dtype in riders:
        r_k, c_k = groups[0][0][0].shape[-2:]
        for group in groups:
            for arr, lead in group:
                at = put(lead)
                operands.append(arr)
                in_specs.append(
                    pl.BlockSpec(
                        (None,) * len(lead) + (r_k, c_k),
                        lambda j, r, s_ref, at=at, n=len(lead): (*[s_ref[at + k] for k in range(n)], 0, 0),
                    )
                )
        if len(groups) == 1:
            at, n_lead = put(out_leads[0]), len(out_dims)
            out_specs.append(
                pl.BlockSpec(
                    (None,) * n_lead + (r_k, c_k),
                    lambda j, r, s_ref, at=at, n=n_lead: (*[s_ref[at + k] for k in range(n)], 0, 0),
                )
            )
        else:
            assert out_dims == (len(groups),) and list(out_leads) == [(g,) for g in range(len(groups))]
            out_specs.append(pl.BlockSpec((len(groups), r_k, c_k), lambda j, r, s_ref: (0, 0, 0)))
        out_shape.append(jax.ShapeDtypeStruct((*out_dims, r_k, c_k), dtype))

    def total(term_refs):
        acc = term_refs[0][...].astype(F32)
        for t_ref in term_refs[1:]:
            acc = acc + t_ref[...].astype(F32)
        return acc

    def body(s_ref, *refs):
        ins, outs = refs[: len(operands)], refs[len(operands) :]
        outs[0][...] = total(ins[:n_main]).astype(main[3])

        def ride():
            at = n_main
            for (groups, _, _, dtype), o_ref in zip(riders, outs[1:]):
                for g, group in enumerate(groups):
                    acc = total(ins[at : at + len(group)]).astype(dtype)
                    at += len(group)
                    if len(groups) == 1:
                        o_ref[...] = acc
                    else:
                        o_ref[g] = acc

        if riders:
            pl.when(jnp.logical_and(pl.program_id(0) == 0, pl.program_id(1) == 0))(ride)

    return _pcall(
        body,
        name=name,
        grid_spec=pltpu.PrefetchScalarGridSpec(
            num_scalar_prefetch=1, grid=(len(main[0]), rows // tr), in_specs=in_specs, out_specs=out_specs
        ),
        out_shape=out_shape,
        compiler_params=_cparams(("arbitrary", "arbitrary")),
    )(_scalars(*scalars), *operands)


def _own_half_job(buf, got, wire, owners, c):
    groups = [[(_halves(buf), (j, c)), (got, (j,))] for j in owners]
    return groups, (len(owners),), [(k,) for k in range(len(owners))], wire


def _for_neighbour_job(buf, got, got1, wire, second, c):
    return [[(_halves(buf), (second, c)), (got, (second,)), (got1, (1,))]], (), [()], wire


def _received_job(buf, got, got1, got2, slot, n_slots, mine, c):
    return [[(_halves(buf), (mine, c)), (got, (mine,)), (got1, (0,)), (got2, ())]], (n_slots,), [(slot,)], F32


def _layer_a_fwd(x, nw, win, ln_w, ln_b, wc, bs_t, wout, tm, hook):
    t_rows, d = x.shape
    n_sh, _, s_cols = win.shape
    aw = wout.shape[0]
    gd = aw // A_GROUPS
    tn = 512
    assert s_cols % tn == 0 and aw % tn == 0 and tm % CHUNK == 0

    def body(x_ref, nw_ref, win_ref, lnw_ref, lnb_ref, wc_ref, bst_ref, wout_ref, z_ref, x1_ref, h_ref, u_s, v_s, y_s):
        x = x_ref[...]
        h, _, _ = _rms_fwd(x, nw_ref[...])
        h = h.astype(BF16)
        h_ref[...] = h
        for j in range(3 * aw // tn):
            k, off = divmod(j * tn, s_cols)
            cols = slice((j * tn) % aw, (j * tn) % aw + tn)
            zj = _dot(h, win_ref[k, :, off : off + tn])
            z_ref[:, j * tn : (j + 1) * tn] = zj
            if j * tn < aw:
                u_s[:, cols] = _gelu(zj)
            elif j * tn < 2 * aw:
                v_s[:, cols] = _gelu(zj)
            else:
                u_s[:, cols] = u_s[:, cols] * (zj * _sigmoid(zj))
        v = v_s[...]
        mu = jnp.mean(v, axis=-1, keepdims=True)
        vc = v - mu
        rstd = lax.rsqrt(jnp.mean(vc * vc, axis=-1, keepdims=True) + LN_EPS)
        v_s[...] = (vc * rstd) * lnw_ref[...] + lnb_ref[...]
        for ck in range(tm // CHUNK):
            rows = slice(ck * CHUNK, (ck + 1) * CHUNK)
            for g in range(A_GROUPS):
                cols = slice(g * gd, (g + 1) * gd)
                s = _dot(wc_ref[g], v_s[rows, cols].astype(BF16)) + bst_ref[:, g : g + 1]
                y_s[rows, cols] = (u_s[rows, cols] * s).astype(BF16)
        x1_ref[...] = x + _dot(y_s[...], wout_ref[...])

    row = lambda i: (i, 0)
    return _pcall(
        body,
        hook,
        name="layer_a_fwd",
        grid=(t_rows // tm,),
        in_specs=[
            pl.BlockSpec((tm, d), row),
            _full(nw.shape),
            _full(win.shape),
            _full(ln_w.shape),
            _full(ln_b.shape),
            _full(wc.shape),
            _full(bs_t.shape),
            _full(wout.shape),
        ],
        out_specs=[pl.BlockSpec((tm, 3 * aw), row), pl.BlockSpec((tm, d), row), pl.BlockSpec((tm, d), row)],
        out_shape=[
            jax.ShapeDtypeStruct((t_rows, 3 * aw), F32),
            jax.ShapeDtypeStruct((t_rows, d), F32),
            jax.ShapeDtypeStruct((t_rows, d), BF16),
        ],
        scratch_shapes=[pltpu.VMEM((tm, aw), F32), pltpu.VMEM((tm, aw), F32), pltpu.VMEM((tm, aw), BF16)],
        compiler_params=_cparams(("arbitrary",)),
    )(x, nw, win, ln_w, ln_b, wc, bs_t, wout)


def _layer_a_bwd(dout, z, ln_w, ln_b, wc, wct, bs_t, wout, tiles, earlier, hook):
    t_rows, d = dout.shape
    aw = wout.shape[0]
    gd = aw // A_GROUPS
    tm = TM_A_BWD
    lo, hi = tiles
    n_earlier = 0 if earlier is None else len(earlier)

    def body(dout_ref, z_ref, lnw_ref, lnb_ref, wc_ref, wct_ref, bst_ref, wout_ref, *rest):
        dz_ref, y_ref, dob_ref, gws_ref, gbs_ref, glnw_ref, glnb_ref, u_s, vh_s, ds_s, dvn_s = rest[n_earlier:]

        @pl.when(pl.program_id(0) == 0)
        def _():
            gws_ref[...] = jnp.zeros_like(gws_ref)
            gbs_ref[...] = jnp.zeros_like(gbs_ref)
            glnw_ref[...] = jnp.zeros_like(glnw_ref)
            glnb_ref[...] = jnp.zeros_like(glnb_ref)

        dob = dout_ref[...].astype(BF16)
        dob_ref[...] = dob
        dy = _dot_nt(dob, wout_ref[...])

        zv = z_ref[:, aw : 2 * aw]
        vg, dvg_dz = _gelu_and_grad(zv)
        mu = jnp.mean(vg, axis=-1, keepdims=True)
        vc = vg - mu
        rstd = lax.rsqrt(jnp.mean(vc * vc, axis=-1, keepdims=True) + LN_EPS)
        vh = vc * rstd
        vh_s[...] = vh
        vn = (vh * lnw_ref[...] + lnb_ref[...]).astype(BF16)

        zu = z_ref[:, 0:aw]
        zg = z_ref[:, 2 * aw : 3 * aw]
        u, du_dz = _gelu_and_grad(zu)
        sg, dsg = _silu_and_grad(zg)
        u_s[...] = u * sg
        tril = lax.broadcasted_iota(jnp.int32, (CHUNK, CHUNK), 0) >= lax.broadcasted_iota(jnp.int32, (CHUNK, CHUNK), 1)
        for ck in range(tm // CHUNK):
            rows = slice(ck * CHUNK, (ck + 1) * CHUNK)
            for g in range(A_GROUPS):
                cols = slice(g * gd, (g + 1) * gd)
                vn_g = vn[rows, cols]
                s = _dot(wc_ref[g], vn_g) + bst_ref[:, g : g + 1]
                usg = u_s[rows, cols]
                dy_g = dy[rows, cols]
                y_ref[rows, cols] = (usg * s).astype(BF16)
                ds = dy_g * usg
                ds_s[rows, cols] = dy_g * s
                gbs_ref[:, g : g + 1] += jnp.sum(ds, axis=-1, keepdims=True)
                dsb = ds.astype(BF16)
                gws_ref[g] += jnp.where(tril, _dot_nt(dsb, vn_g), 0.0)
                dvn_s[rows, cols] = _dot(wct_ref[g], dsb)
        dusg = ds_s[...]
        dz_ref[:, 0:aw] = (dusg * sg * du_dz).astype(BF16)
        dz_ref[:, 2 * aw : 3 * aw] = (dusg * u * dsg).astype(BF16)

        dvn = dvn_s[...]
        vh = vh_s[...]
        glnw_ref[...] += jnp.sum(dvn * vh, axis=0, keepdims=True)
        glnb_ref[...] += jnp.sum(dvn, axis=0, keepdims=True)
        dvh = dvn * lnw_ref[...]
        dvg = rstd * (dvh - jnp.mean(dvh, axis=-1, keepdims=True) - vh * jnp.mean(dvh * vh, axis=-1, keepdims=True))
        dz_ref[:, aw : 2 * aw] = (dvg * dvg_dz).astype(BF16)

    row = lambda i: (i + lo, 0)
    call = _pcall(
        body,
        hook,
        name=f"layer_a_bwd_{lo}",
        grid=(hi - lo,),
        in_specs=[
            pl.BlockSpec((tm, d), row),
            pl.BlockSpec((tm, 3 * aw), row),
            _full(ln_w.shape),
            _full(ln_b.shape),
            _full(wc.shape),
            _full(wct.shape),
            _full(bs_t.shape),
            _full(wout.shape),
        ]
        + [ANY] * n_earlier,
        out_specs=[
            pl.BlockSpec((tm, 3 * aw), row),
            pl.BlockSpec((tm, aw), row),
            pl.BlockSpec((tm, d), row),
            _full((A_GROUPS, CHUNK, CHUNK)),
            _full((CHUNK, A_GROUPS)),
            _full((1, aw)),
            _full((1, aw)),
        ],
        out_shape=[
            jax.ShapeDtypeStruct((t_rows, 3 * aw), BF16),
            jax.ShapeDtypeStruct((t_rows, aw), BF16),
            jax.ShapeDtypeStruct((t_rows, d), BF16),
            jax.ShapeDtypeStruct((A_GROUPS, CHUNK, CHUNK), F32),
            jax.ShapeDtypeStruct((CHUNK, A_GROUPS), F32),
            jax.ShapeDtypeStruct((1, aw), F32),
            jax.ShapeDtypeStruct((1, aw), F32),
        ],
        scratch_shapes=[pltpu.VMEM((tm, aw), F32)] * 4,
        input_output_aliases={8 + i: i for i in range(n_earlier)},
        compiler_params=_cparams(("arbitrary",)),
    )
    return call(dout, z, ln_w, ln_b, wc, wct, bs_t, wout, *(earlier or ()))


def _layer_a_bwd_dx(dout, x, dz, nw, win, tm, tiles, earlier, hook):
    t_rows, d = x.shape
    n_sh, _, s_cols = win.shape
    lo, hi = tiles
    n_earlier = 0 if earlier is None else 1

    def body(dout_ref, x_ref, dz_ref, nw_ref, win_ref, *rest):
        gx_ref, gnw_ref = rest[n_earlier:]

        @pl.when(pl.program_id(0) == 0)
        def _():
            gnw_ref[...] = jnp.zeros_like(gnw_ref)

        dh = jnp.zeros((tm, d), F32)
        for k in range(n_sh):
            dh = dh + _dot_nt(dz_ref[:, k * s_cols : (k + 1) * s_cols], win_ref[k])
        nw = nw_ref[...]
        _, xh, r = _rms_fwd(x_ref[...], nw)
        dx, gnw = _rms_bwd(dh, xh, r, nw)
        gnw_ref[0:1, :] += gnw
        gx_ref[...] = dout_ref[...] + dx

    row = lambda i: (i + lo, 0)
    return _pcall(
        body,
        hook,
        name=f"layer_a_bwd_dx_{lo}",
        grid=(hi - lo,),
        in_specs=[
            pl.BlockSpec((tm, d), row),
            pl.BlockSpec((tm, d), row),
            pl.BlockSpec((tm, n_sh * s_cols), row),
            _full(nw.shape),
            _full(win.shape),
        ]
        + [ANY] * n_earlier,
        out_specs=[pl.BlockSpec((tm, d), row), _full((SUBLANES, d))],
        out_shape=[jax.ShapeDtypeStruct((t_rows, d), F32), jax.ShapeDtypeStruct((SUBLANES, d), F32)],
        input_output_aliases={5: 0} if n_earlier else {},
        compiler_params=_cparams(("arbitrary",)),
    )(dout, x, dz, nw, win, *([earlier] if n_earlier else []))


def _decay(r, sp_h):
    log_a = (-RG_C) * r * sp_h
    a = jnp.exp(log_a)
    mult = jnp.sqrt(jnp.tanh(-log_a) * (a * a + 1.0))
    return a, mult


ROW_CONV_B, ROW_GATE_A_B, ROW_GATE_X_B, ROW_LAMBDA = range(CONV_WIDTH, CONV_WIDTH + 4)


def _gates(xc_h, gab_ref, vec_ref, sp_h, h, hd):
    pre = _dot(xc_h.astype(BF16), gab_ref[h])
    cols = slice(h * hd, (h + 1) * hd)
    r = _sigmoid(pre[:, :hd] + vec_ref[ROW_GATE_A_B : ROW_GATE_A_B + 1, cols])
    ig = _sigmoid(pre[:, hd:] + vec_ref[ROW_GATE_X_B : ROW_GATE_X_B + 1, cols])
    a, mult = _decay(r, sp_h)
    return r, ig, a, mult


def _conv(xb, halo, vec_ref):
    xc = vec_ref[ROW_CONV_B : ROW_CONV_B + 1, :] + vec_ref[CONV_WIDTH - 1 : CONV_WIDTH, :] * xb
    for k in range(CONV_WIDTH - 1):
        xc = xc + vec_ref[k : k + 1, :] * _shift_down(xb, halo, CONV_WIDTH - 1 - k)
    return xc


def _layer_b_fwd(x1, nw, bin_w, vec, gab, bout, nf, tgt, tm):
    t_rows, d = x1.shape
    bw = bout.shape[0]
    hd = bw // B_HEADS
    nt = t_rows // tm

    def body(
        x1_ref, nw_ref, bin_ref, vec_ref, gab_ref, bout_ref, nf_ref, tgt_ref,
        z_ref, h_ref, h1_ref, dx2_ref, loss_ref, gnf_ref,
        tail_s, carry_s, a_s, b_s, hs_s, acc_s,
    ):
        @pl.when(pl.program_id(0) == 0)
        def _():
            tail_s[...] = jnp.zeros_like(tail_s)
            carry_s[...] = jnp.zeros_like(carry_s)
            acc_s[...] = jnp.zeros_like(acc_s)
            gnf_ref[...] = jnp.zeros_like(gnf_ref)

        x1 = x1_ref[...]
        h1, _, _ = _rms_fwd(x1, nw_ref[...])
        h1 = h1.astype(BF16)
        h1_ref[...] = h1
        z = jnp.concatenate([_dot(h1, bin_ref[k]) for k in range(N_CHIPS)], axis=1)
        z_ref[...] = z
        xb = z[:, :bw]
        xc = _conv(xb, tail_s[...], vec_ref)
        tail = xb[tm - SUBLANES :, :]
        tail_s[...] = tail
        sp = _softplus_neg(vec_ref[ROW_LAMBDA : ROW_LAMBDA + 1, :])
        for h in range(B_HEADS):
            cols = slice(h * hd, (h + 1) * hd)
            xc_h = xc[:, cols]
            _, ig, a, mult = _gates(xc_h, gab_ref, vec_ref, sp[:, cols], h, hd)
            a_s[:, cols] = a
            b_s[:, cols] = mult * (ig * xc_h)
        carry = _scan_blocks(a_s, b_s, hs_s, carry_s[...], tm, reverse=False)
        carry_s[...] = carry
        hs = hs_s[...]
        h_ref[...] = hs
        g = z[:, bw:]
        y = (hs * (g * _sigmoid(g))).astype(BF16)
        x2 = x1 + _dot(y, bout_ref[...])

        nf = nf_ref[...]
        o, xh, r = _rms_fwd(x2, nf)
        diff = o - tgt_ref[...]
        acc_s[...] += jnp.sum(diff * diff, axis=0, keepdims=True)
        do = diff * (1.0 / d)
        dx2, gnf = _rms_bwd(do, xh, r, nf)
        gnf_ref[...] += gnf
        dx2_ref[...] = dx2

        @pl.when(pl.program_id(0) == nt - 1)
        def _():
            total = jnp.sum(acc_s[...], axis=-1, keepdims=True) * (0.5 / d)
            loss_ref[...] = jnp.broadcast_to(total, loss_ref.shape)

    row = lambda i: (i, 0)
    return _pcall(
        body,
        name="layer_b_fwd",
        grid=(nt,),
        in_specs=[
            pl.BlockSpec((tm, d), row),
            _full(nw.shape),
            _full(bin_w.shape),
            _full(vec.shape),
            _full(gab.shape),
            _full(bout.shape),
            _full(nf.shape),
            pl.BlockSpec((tm, d), row),
        ],
        out_specs=[
            pl.BlockSpec((tm, 2 * bw), row),
            pl.BlockSpec((tm, bw), row),
            pl.BlockSpec((tm, d), row),
            pl.BlockSpec((tm, d), row),
            _full((1, LANES)),
            _full((1, d)),
        ],
        out_shape=[
            jax.ShapeDtypeStruct((t_rows, 2 * bw), F32),
            jax.ShapeDtypeStruct((t_rows, bw), F32),
            jax.ShapeDtypeStruct((t_rows, d), BF16),
            jax.ShapeDtypeStruct((t_rows, d), F32),
            jax.ShapeDtypeStruct((1, LANES), F32),
            jax.ShapeDtypeStruct((1, d), F32),
        ],
        scratch_shapes=[
            pltpu.VMEM((SUBLANES, bw), F32),
            pltpu.VMEM((SUBLANES, bw), F32),
            pltpu.VMEM((tm, bw), F32),
            pltpu.VMEM((tm, bw), F32),
            pltpu.VMEM((tm, bw), F32),
            pltpu.VMEM((1, d), F32),
        ],
        compiler_params=_cparams(("arbitrary",)),
    )(x1, nw, bin_w, vec, gab, bout, nf, tgt)


def _layer_b_bwd(dout, x1, z, hseq, nw, bin_w, vec, gab, gabt, bout, tm):
    t_rows, d = x1.shape
    bw = bout.shape[0]
    hd = bw // B_HEADS
    nt = t_rows // tm

    def body(
        dout_ref, x1_ref, z_ref, h_ref, xbt_ref, ht_ref, nw_ref, bin_ref, vec_ref, gab_ref, gabt_ref, bout_ref,
        dx1_ref, dz_ref, y_ref, dob_ref, ggab_ref, ggb_ref, gcw_ref, gcb_ref, glam_ref, gnw_ref,
        gcarry_s, afirst_s, head_s, aup_s, dh_s, gt_s, dxc_s, xc_s, r_s, ig_s,
    ):
        step = pl.program_id(0)
        tile = nt - 1 - step

        @pl.when(step == 0)
        def _():
            for ref in (ggab_ref, ggb_ref, gcw_ref, gcb_ref, glam_ref, gnw_ref, gcarry_s, afirst_s, head_s):
                ref[...] = jnp.zeros_like(ref)

        first_tile = tile == 0
        xb_halo = jnp.where(first_tile, 0.0, xbt_ref[...])
        h_halo = jnp.where(first_tile, 0.0, ht_ref[...])

        dout = dout_ref[...]
        dob = dout.astype(BF16)
        dob_ref[...] = dob
        dy = _dot_nt(dob, bout_ref[...])
        hs = h_ref[...]
        g = z_ref[:, bw:]
        sg, dsg = _silu_and_grad(g)
        y_ref[...] = (hs * sg).astype(BF16)
        dz_ref[:, bw:] = (dy * hs * dsg).astype(BF16)
        dh_s[...] = dy * sg

        xb = z_ref[:, :bw]
        xc = _conv(xb, xb_halo, vec_ref)
        xc_s[...] = xc
        lam = vec_ref[ROW_LAMBDA : ROW_LAMBDA + 1, :]
        sp = _softplus_neg(lam)
        for h in range(B_HEADS):
            cols = slice(h * hd, (h + 1) * hd)
            r, ig, a, _ = _gates(xc[:, cols], gab_ref, vec_ref, sp[:, cols], h, hd)
            r_s[:, cols] = r
            ig_s[:, cols] = ig
            aup_s[:, cols] = _shift_up(a, afirst_s[:, cols], 1)
            afirst_s[:, cols] = jnp.broadcast_to(a[0:1, :], (SUBLANES, hd))
        carry = _scan_blocks(aup_s, dh_s, gt_s, gcarry_s[...], tm, reverse=True)
        gcarry_s[...] = carry

        h_prev = _shift_down(hs, h_halo, 1)
        for h in range(B_HEADS):
            cols = slice(h * hd, (h + 1) * hd)
            xc_h = xc_s[:, cols]
            sp_h = sp[:, cols]
            r, ig = r_s[:, cols], ig_s[:, cols]
            a, mult = _decay(r, sp_h)
            gt = gt_s[:, cols]
            da = gt * h_prev[:, cols]
            dmult = gt * (ig * xc_h)
            dig = gt * (mult * xc_h)
            dxc_direct = gt * (mult * ig)
            dla = da * a - dmult * (a * a) / mult
            glam_ref[:, cols] += jnp.sum(dla * r, axis=0, keepdims=True)
            dr = dla * ((-RG_C) * sp_h)
            dpre = jnp.concatenate([dr * r * (1.0 - r), dig * ig * (1.0 - ig)], axis=1)
            ggb_ref[:, cols] += jnp.sum(dpre[:, :hd], axis=0, keepdims=True)
            ggb_ref[:, bw + h * hd : bw + (h + 1) * hd] += jnp.sum(dpre[:, hd:], axis=0, keepdims=True)
            dpb = dpre.astype(BF16)
            ggab_ref[h] += _dot_tn(xc_h.astype(BF16), dpb)
            dxc_s[:, cols] = dxc_direct + _dot(dpb, gabt_ref[h])
        glam_ref[...] = jnp.where(step == nt - 1, glam_ref[...] * (RG_C * _sigmoid(-lam)), glam_ref[...])

        dxc = dxc_s[...]
        gcb_ref[...] += jnp.sum(dxc, axis=0, keepdims=True)
        dxb = vec_ref[CONV_WIDTH - 1 : CONV_WIDTH, :] * dxc
        gcw_ref[CONV_WIDTH - 1 : CONV_WIDTH, :] += jnp.sum(dxc * xb, axis=0, keepdims=True)
        head = head_s[...]
        for k in range(CONV_WIDTH - 1):
            lag = CONV_WIDTH - 1 - k
            dxb = dxb + vec_ref[k : k + 1, :] * _shift_up(dxc, head, lag)
            gcw_ref[k : k + 1, :] += jnp.sum(dxc * _shift_down(xb, xb_halo, lag), axis=0, keepdims=True)
        head_s[...] = dxc[:SUBLANES, :]
        dz_ref[:, :bw] = dxb.astype(BF16)

        s_cols = 2 * bw // N_CHIPS
        dh1 = jnp.zeros((tm, d), F32)
        for k in range(N_CHIPS):
            dh1 = dh1 + _dot_nt(dz_ref[:, k * s_cols : (k + 1) * s_cols], bin_ref[k])
        x1 = x1_ref[...]
        nw = nw_ref[...]
        _, xh, r1 = _rms_fwd(x1, nw)
        dx, gnw = _rms_bwd(dh1, xh, r1, nw)
        gnw_ref[...] += gnw
        dx1_ref[...] = dout + dx

    rev = lambda i: (nt - 1 - i, 0)
    prev = lambda i: (jnp.maximum((nt - 1 - i) * (tm // SUBLANES) - 1, 0), 0)
    return _pcall(
        body,
        name="layer_b_bwd",
        grid=(nt,),
        in_specs=[
            pl.BlockSpec((tm, d), rev),
            pl.BlockSpec((tm, d), rev),
            pl.BlockSpec((tm, 2 * bw), rev),
            pl.BlockSpec((tm, bw), rev),
            pl.BlockSpec((SUBLANES, bw), prev),
            pl.BlockSpec((SUBLANES, bw), prev),
            _full(nw.shape),
            _full(bin_w.shape),
            _full(vec.shape),
            _full(gab.shape),
            _full(gabt.shape),
            _full(bout.shape),
        ],
        out_specs=[
            pl.BlockSpec((tm, d), rev),
            pl.BlockSpec((tm, 2 * bw), rev),
            pl.BlockSpec((tm, bw), rev),
            pl.BlockSpec((tm, d), rev),
            _full((B_HEADS, hd, 2 * hd)),
            _full((1, 2 * bw)),
            _full((SUBLANES, bw)),
            _full((1, bw)),
            _full((1, bw)),
            _full((1, d)),
        ],
        out_shape=[
            jax.ShapeDtypeStruct((t_rows, d), F32),
            jax.ShapeDtypeStruct((t_rows, 2 * bw), BF16),
            jax.ShapeDtypeStruct((t_rows, bw), BF16),
            jax.ShapeDtypeStruct((t_rows, d), BF16),
            jax.ShapeDtypeStruct((B_HEADS, hd, 2 * hd), F32),
            jax.ShapeDtypeStruct((1, 2 * bw), F32),
            jax.ShapeDtypeStruct((SUBLANES, bw), F32),
            jax.ShapeDtypeStruct((1, bw), F32),
            jax.ShapeDtypeStruct((1, bw), F32),
            jax.ShapeDtypeStruct((1, d), F32),
        ],
        scratch_shapes=[pltpu.VMEM((SUBLANES, bw), F32)] * 3 + [pltpu.VMEM((tm, bw), F32)] * 7,
        compiler_params=_cparams(("arbitrary",)),
    )(dout, x1, z, hseq, z, hseq, nw, bin_w, vec, gab, gabt, bout)


def _wgrad(a, b, m_blocks, n_blocks, hook=None, wire_copy=False):
    k, m = a.shape
    n = b.shape[1]
    bm, bn = m // m_blocks, n // n_blocks

    def body(a_ref, b_ref, o_ref, *wire_ref):
        prod = _dot_tn(a_ref[...], b_ref[...])
        o_ref[...] = prod
        if wire_copy:
            wire_ref[0][...] = prod.astype(BF16)

    out_spec = pl.BlockSpec((None, None, bm, bn), lambda j, i: (j, i, 0, 0))
    shape = (n_blocks, m_blocks, bm, bn)
    out = _pcall(
        body,
        hook,
        name=f"wgrad_{m}x{n}",
        grid=(n_blocks, m_blocks),
        in_specs=[pl.BlockSpec((k, bm), lambda j, i: (0, i)), pl.BlockSpec((k, bn), lambda j, i: (0, j))],
        out_specs=[out_spec] * (1 + wire_copy),
        out_shape=[jax.ShapeDtypeStruct(shape, F32)] + [jax.ShapeDtypeStruct(shape, BF16)] * wire_copy,
        compiler_params=_cparams(("arbitrary", "arbitrary")),
    )(a, b)
    outs, rode = (out, None) if hook is None else out
    outs = outs if wire_copy else outs[0]
    return outs if hook is None else (outs, rode)


def _adamw_math(w, g, m, v):
    m = ADAM_B1 * m + (1.0 - ADAM_B1) * g
    v = ADAM_B2 * v + (1.0 - ADAM_B2) * (g * g)
    m_hat = m / (1.0 - ADAM_B1**ADAM_STEP)
    v_hat = v / (1.0 - ADAM_B2**ADAM_STEP)
    delta = -ADAM_LR * (m_hat / (jnp.sqrt(v_hat) + ADAM_EPS) + ADAM_WD * w)
    return delta, m, v


ADAMW_ROW_TILES = 8


def _adamw_rows(ws, gs, ms, vs):
    n = len(ws)

    def body(*refs):
        ins, outs = refs[: 4 * n], refs[4 * n :]
        for i in range(n):
            w_ref, g_ref, m_ref, v_ref = ins[i::n]
            d_ref, mo_ref, vo_ref, go_ref = outs[i::n]
            g = g_ref[...]
            d_ref[...], mo_ref[...], vo_ref[...] = _adamw_math(w_ref[...], g, m_ref[...], v_ref[...])
            go_ref[...] = g

    specs = [pl.BlockSpec((w.shape[0] // ADAMW_ROW_TILES, w.shape[1]), lambda i: (i, 0)) for w in ws]
    outs = _pcall(
        body,
        name="adamw_rows",
        grid=(ADAMW_ROW_TILES,),
        in_specs=specs * 4,
        out_specs=specs * 4,
        out_shape=[jax.ShapeDtypeStruct(w.shape, F32) for w in ws] * 4,
        compiler_params=_cparams(("arbitrary",)),
    )(*ws, *gs, *ms, *vs)
    return [outs[k * n : (k + 1) * n] for k in range(4)]


def _sum_partials(parts):
    def body(p_ref, o_ref):
        total = p_ref[0, 0:1, :]
        for k in range(1, N_DEV):
            total = total + p_ref[k, 0:1, :]
        o_ref[...] = total

    vmem = pl.BlockSpec(memory_space=pltpu.VMEM)
    return _pcall(
        body,
        name="sum_partials",
        in_specs=[vmem],
        out_specs=vmem,
        out_shape=jax.ShapeDtypeStruct((1, parts.shape[2]), F32),
    )(parts)


def _adamw_many(ws, gs, ms, vs, pack):
    n = len(ws)
    packed = [i for i in range(n) if isinstance(gs[i], int)]
    g_arrays = [g for g in gs if not isinstance(g, int)]

    def body(*refs):
        w_refs, m_refs, v_refs = (refs[i * n : (i + 1) * n] for i in range(3))
        g_refs = list(refs[3 * n : 3 * n + len(g_arrays)])
        pack_ref = refs[3 * n + len(g_arrays)]
        outs = refs[3 * n + len(g_arrays) + 1 :]
        d_refs, mo_refs, vo_refs, go_refs = outs[:n], outs[n : 2 * n], outs[2 * n : 3 * n], list(outs[3 * n :])
        for i in range(n):
            if i in packed:
                g = pack_ref[gs[i] : gs[i] + ws[i].shape[0], :]
                go_refs.pop(0)[...] = g
            else:
                g = g_refs.pop(0)[...]
            d_refs[i][...], mo_refs[i][...], vo_refs[i][...] = _adamw_math(w_refs[i][...], g, m_refs[i][...], v_refs[i][...])

    vmem = pl.BlockSpec(memory_space=pltpu.VMEM)
    like = [jax.ShapeDtypeStruct(w.shape, F32) for w in ws]
    outs = _pcall(
        body,
        name="adamw_small",
        in_specs=[vmem] * (3 * n + len(g_arrays) + 1),
        out_specs=[vmem] * (3 * n + len(packed)),
        out_shape=like * 3 + [like[i] for i in packed],
        compiler_params=_cparams(),
    )(*ws, *ms, *vs, *g_arrays, pack)
    return outs[:n], outs[n : 2 * n], outs[2 * n : 3 * n], outs[3 * n :]


def _pack_rows(parts, lanes=LANES):
    flat = jnp.concatenate([p.reshape(-1) for p in parts])
    per = N_DEV * SUBLANES * lanes
    total = -(-flat.shape[0] // per) * per
    flat = jnp.pad(flat, (0, total - flat.shape[0]))
    return flat.reshape(N_DEV, total // (N_DEV * lanes), lanes)


def _size(shape):
    n = 1
    for dim in shape:
        n *= dim
    return n


def _unpack(flat, shapes):
    out, at = [], 0
    for s in shapes:
        out.append(flat[at : at + _size(s)].reshape(s))
        at += _size(s)
    return out


def kernel(x, norm_w, a_w_in, a_ln_w, a_ln_b, a_w_s, a_b_s, a_w_out, b_w_in, b_conv_w, b_conv_b, b_gate_a_w, b_gate_a_b, b_gate_x_w, b_gate_x_b, b_lambda, b_w_out, norm_f_w, loss_target, m_norm_w, m_a_w_in, m_a_ln_w, m_a_ln_b, m_a_w_s, m_a_b_s, m_a_w_out, m_b_w_in, m_b_conv_w, m_b_conv_b, m_b_gate_a_w, m_b_gate_a_b, m_b_gate_x_w, m_b_gate_x_b, m_b_lambda, m_b_w_out, m_norm_f_w, v_norm_w, v_a_w_in, v_a_ln_w, v_a_ln_b, v_a_w_s, v_a_b_s, v_a_w_out, v_b_w_in, v_b_conv_w, v_b_conv_b, v_b_gate_a_w, v_b_gate_a_b, v_b_gate_x_w, v_b_gate_x_b, v_b_lambda, v_b_w_out, v_norm_f_w):
    t_rows, d = x.shape[1], x.shape[2]
    aw = a_ln_w.shape[1]
    bw = b_gate_a_w.shape[1] * b_gate_a_w.shape[2]
    hd = bw // B_HEADS
    mine = 2 * lax.axis_index("x") + lax.axis_index("y")
    core = lax.axis_index("c")
    weights = dict(norm_w=norm_w, a_w_in=a_w_in, a_ln_w=a_ln_w, a_ln_b=a_ln_b, a_w_s=a_w_s, a_b_s=a_b_s, a_w_out=a_w_out, b_w_in=b_w_in, b_conv_w=b_conv_w, b_conv_b=b_conv_b, b_gate_a_w=b_gate_a_w, b_gate_a_b=b_gate_a_b, b_gate_x_w=b_gate_x_w, b_gate_x_b=b_gate_x_b, b_lambda=b_lambda, b_w_out=b_w_out, norm_f_w=norm_f_w)
    m_in = dict(norm_w=m_norm_w, a_w_in=m_a_w_in, a_ln_w=m_a_ln_w, a_ln_b=m_a_ln_b, a_w_s=m_a_w_s, a_b_s=m_a_b_s, a_w_out=m_a_w_out, b_w_in=m_b_w_in, b_conv_w=m_b_conv_w, b_conv_b=m_b_conv_b, b_gate_a_w=m_b_gate_a_w, b_gate_a_b=m_b_gate_a_b, b_gate_x_w=m_b_gate_x_w, b_gate_x_b=m_b_gate_x_b, b_lambda=m_b_lambda, b_w_out=m_b_w_out, norm_f_w=m_norm_f_w)
    v_in = dict(norm_w=v_norm_w, a_w_in=v_a_w_in, a_ln_w=v_a_ln_w, a_ln_b=v_a_ln_b, a_w_s=v_a_w_s, a_b_s=v_a_b_s, a_w_out=v_a_w_out, b_w_in=v_b_w_in, b_conv_w=v_b_conv_w, b_conv_b=v_b_conv_b, b_gate_a_w=v_b_gate_a_w, b_gate_a_b=v_b_gate_a_b, b_gate_x_w=v_b_gate_x_w, b_gate_x_b=v_b_gate_x_b, b_lambda=v_b_lambda, b_w_out=v_b_w_out, norm_f_w=v_norm_f_w)

    win_l, wout_l = _cast_to_segments([a_w_in[0], a_w_out[0]], mine, 4)
    small_l = jnp.concatenate([b_conv_w[0], b_conv_b, b_gate_a_b, b_gate_x_b, b_lambda], axis=0)
    (bin_l, bout_l, wc, wct, gab, gabt), (win_g, wout_g, small_g) = _cast_to_segments(
        [b_w_in[0], b_w_out[0]], mine, 8, _gather_hook([win_l, wout_l], small_l),
        (a_w_s[0], b_gate_a_w[0], b_gate_x_w[0]),
    )
    win = win_g.reshape(N_CHIPS, d, -1)
    wout = wout_g.reshape(aw, d)
    bs_t = a_b_s[0].T
    nw0, nw1, nf = norm_w[0:1], norm_w[1:2], norm_f_w.reshape(1, d)

    x0 = x[0]
    (z_a, x1, h0), (bin_g, bout_g) = _layer_a_fwd(
        x0, nw0, win, a_ln_w, a_ln_b, wc, bs_t, wout, TM_FWD, _gather_hook([bin_l, bout_l])
    )
    bin_w = bin_g.reshape(N_CHIPS, d, -1)
    bout = bout_g.reshape(bw, d)
    vec = jnp.transpose(small_g, (1, 0, 2)).reshape(SUBLANES, bw)
    z_b, hseq, h1, dx2, loss_l, g_nf = _layer_b_fwd(x1, nw1, bin_w, vec, gab, bout, nf, loss_target[0], TM_FWD)
    dx1, dz_b, y_b, dob_b, g_gab, g_gb, g_cw, g_cb, g_lam, g_nw1 = _layer_b_bwd(
        dx2, x1, z_b, hseq, nw1, bin_w, vec, gab, gabt, bout, TM_FWD
    )
    seg = lambda g: g.reshape(N_DEV, -1, g.shape[3])
    x_at, y_at = lax.axis_index("x"), lax.axis_index("y")
    first_no = 2 * (x_at ^ (1 - core)) + (y_at ^ core)
    second_no = 2 * (x_at ^ core) + (y_at ^ (1 - core))
    own_half = lambda buf, got, wire: _own_half_job(buf, got, wire, (first_no, N_CHIPS - 1 - mine), core)
    for_neighbour = lambda buf, got, got1, wire: _for_neighbour_job(buf, got, got1, wire, second_no, core)
    received = lambda buf, got, got1, got2: _received_job(buf, got, got1, got2, core, 2, mine, core)

    g_o = seg(_wgrad(y_b, dob_b, 2, 1))
    g_i, (swap_o,) = _wgrad(h1, dz_b, 1, N_CHIPS, _swap_hook([g_o]))
    g_i = seg(g_i)
    (part_o,) = _sum_jobs([own_half(g_o, swap_o, BF16)], "add_own_half_o")
    a_args = (z_a, a_ln_w, a_ln_b, wc, wct, bs_t, wout)
    half = t_rows // TM_A_BWD // 2
    first, (swap_i, got1_o) = _layer_a_bwd(
        dx1, *a_args, (0, half), None, _join_hooks(_swap_hook([g_i]), _send_first_hook([part_o]))
    )
    part_i, mid_o = _sum_jobs(
        [own_half(g_i, swap_i, BF16), for_neighbour(g_o, swap_o, got1_o, BF16)], "add_own_half_i_for_neighbour_o"
    )
    second, (got1_i, got2_o) = _layer_a_bwd(
        dx1, *a_args, (half, 2 * half), first[:3], _join_hooks(_send_first_hook([part_i]), _send_second_hook([mid_o]))
    )
    dz_a, y_a, dob_a = second[:3]
    g_ws, g_bst, g_lnw, g_lnb = (p + q for p, q in zip(first[3:], second[3:]))
    mid_i, red_o = _sum_jobs(
        [for_neighbour(g_i, swap_i, got1_i, BF16), received(g_o, swap_o, got1_o, got2_o)],
        "add_for_neighbour_i_received_o",
    )
    small_shapes = [
        (1, d), (1, aw), (1, aw), (A_GROUPS, CHUNK, CHUNK), (A_GROUPS, CHUNK), (B_HEADS, hd, hd), (B_HEADS, hd, hd),
        (d,), (CONV_WIDTH, bw), (1, bw), (1, bw), (1, bw), (1, bw), (1, 1),
    ]
    small = _pack_rows(
        [
            g_nw1, g_lnw, g_lnb, g_ws, g_bst.T, g_gab[:, :, :hd], g_gab[:, :, hd:],
            g_nf, g_cw[:CONV_WIDTH], g_cb, g_gb[:, :bw], g_gb[:, bw:], g_lam, loss_l[:, :1],
        ]
    )
    (g_w, g_w_wire), (got2_i, gr_bout, swap_s) = _wgrad(
        h0, dz_a, 1, N_CHIPS, _join_hooks(_send_second_hook([mid_i]), _share_hook([red_o]), _swap_hook([small])),
        wire_copy=True,
    )
    g_w, g_w_wire = seg(g_w), seg(g_w_wire)
    (red_i,) = _sum_jobs([received(g_i, swap_i, got1_i, got2_i)], "add_received_i")
    g_u, (swap_w,) = _wgrad(y_a, dob_a, N_CHIPS, 1, _swap_hook([g_w_wire]))
    g_u = seg(g_u)

    part_w, part_s = _sum_jobs([own_half(g_w, swap_w, BF16), own_half(small, swap_s, F32)], "add_own_half_w")
    (grad_x, g_nw0_mine), (got1_w, got1_s, swap_u, gr_bin) = _layer_a_bwd_dx(
        dx1, x0, dz_a, nw0, win, TM_A_DX, (0, t_rows // TM_A_DX), None,
        _join_hooks(_send_first_hook([part_w, part_s]), _swap_hook([g_u]), _share_hook([red_i])),
    )
    mid_w, mid_s, part_u = _sum_jobs(
        [
            for_neighbour(g_w, swap_w, got1_w, BF16), for_neighbour(small, swap_s, got1_s, F32),
            own_half(g_u, swap_u, BF16),
        ],
        "add_for_neighbour_w_own_half_u",
    )
    got2_w, got2_s, got1_u = _run_hook(
        _join_hooks(_send_second_hook([mid_w, mid_s]), _send_first_hook([part_u])), "second_axis"
    )
    red_w, red_s, mid_u = _sum_jobs(
        [
            received(g_w, swap_w, got1_w, got2_w),
            _received_job(small, swap_s, got1_s, got2_s, 2 * mine + core, N_DEV, mine, core),
            for_neighbour(g_u, swap_u, got1_u, BF16),
        ],
        "add_received_w_for_neighbour_u",
    )
    got2_u, gr_win, small_r, g_nw0_all = _run_hook(
        _join_hooks(_send_second_hook([mid_u]), _share_hook([red_w], red_s, g_nw0_mine)), "second_axis_and_share"
    )
    (red_u,) = _sum_jobs([received(g_u, swap_u, got1_u, got2_u)], "add_received_u")
    (gr_wout,) = _run_hook(_share_hook([red_u]), "share_reduced")
    g_nw0 = _sum_partials(g_nw0_all)
    (g_nw1_r, g_a_ln_w, g_a_ln_b, g_a_w_s, g_a_b_s, g_gate_a_w, g_gate_x_w, g_norm_f, gf_cw, gf_cb, gf_gab, gf_gxb,
     gf_lam, loss) = _unpack(small_r.reshape(-1), small_shapes)
    g_norm_w = jnp.concatenate([g_nw0, g_nw1_r], axis=0)
    shard = lambda g: lax.dynamic_slice_in_dim(g, mine * (bw // N_CHIPS), bw // N_CHIPS, axis=1)

    grads = {
        "norm_w": g_norm_w, "a_w_in": gr_win, "a_ln_w": g_a_ln_w, "a_ln_b": g_a_ln_b,
        "a_b_s": g_a_b_s[None], "a_w_out": gr_wout, "b_w_in": gr_bin, "b_conv_w": shard(gf_cw)[None],
        "b_conv_b": shard(gf_cb), "b_gate_a_b": shard(gf_gab),
        "b_gate_x_b": shard(gf_gxb), "b_lambda": shard(gf_lam),
        "b_w_out": gr_bout, "norm_f_w": g_norm_f,
    }
    pack_names = [
        "norm_w", "a_ln_w", "a_ln_b", "a_w_s", "a_b_s", "b_gate_a_w", "b_gate_x_w", "norm_f_w", "b_conv_w", "b_conv_b",
        "b_gate_a_b", "b_gate_x_b", "b_lambda", "loss",
    ]
    row_of, at = {}, 0
    for n, shape in zip(pack_names, small_shapes):
        row_of[n] = at // LANES
        at += _size(shape)
    names = list(weights)
    big_names = ["a_w_in", "a_w_out", "b_w_in", "b_w_out"]
    small_names = [n for n in names if n not in big_names]
    from_pack = [n for n in small_names if n in ("a_w_s", "b_gate_a_w", "b_gate_x_w")]
    as_rows = lambda n, a: a.reshape(-1, LANES) if n in from_pack else a.reshape(1, -1) if a.ndim == 1 else a
    *small_out, pack_grads = _adamw_many(
        [as_rows(n, weights[n]) for n in small_names],
        [row_of[n] if n in from_pack else as_rows(n, grads[n]) for n in small_names],
        [as_rows(n, m_in[n]) for n in small_names],
        [as_rows(n, v_in[n]) for n in small_names],
        small_r.reshape(-1, LANES),
    )
    grads.update(zip(from_pack, pack_grads))
    *big_out, big_grads = _adamw_rows(
        *[[src[n].reshape(weights[n].shape[-2:]) for n in big_names] for src in (weights, grads, m_in, v_in)]
    )
    grads.update(zip(big_names, big_grads))
    delta, new_m, new_v = {}, {}, {}
    for dst, small_vals, big_vals in zip((delta, new_m, new_v), small_out, big_out):
        dst.update(zip(small_names, small_vals))
        dst.update(zip(big_names, big_vals))
    for dst in (grads, delta, new_m, new_v):
        for n in names:
            dst[n] = dst[n].reshape(weights[n].shape)

    return (
        loss.reshape(()),
        grad_x[None],
        *[grads[n] for n in names],
        *[delta[n] for n in names],
        *[new_m[n] for n in names],
        *[new_v[n] for n in names],
    )
```

```python
import jax
import jax.numpy as jnp
from jax import lax
from jax.experimental import pallas as pl
from jax.experimental.pallas import tpu as pltpu

F32 = jnp.float32
BF16 = jnp.bfloat16

RMS_EPS = 1e-6
LN_EPS = 1e-5
RG_C = 8.0
CHUNK = 128
A_GROUPS = 8
B_HEADS = 12
CONV_WIDTH = 4

ADAM_LR = 0.001
ADAM_B1 = 0.9
ADAM_B2 = 0.999
ADAM_EPS = 1e-08
ADAM_WD = 0.01
ADAM_STEP = 10

N_CHIPS = 4
N_DEV = 8
SUBLANES = 8
LANES = 128
V7X_VMEM_BYTES = 64 * 1024 * 1024
VMEM_LIMIT = V7X_VMEM_BYTES * 7 // 8
MESH = pl.DeviceIdType.MESH
ANY = pl.BlockSpec(memory_space=pl.ANY)

TM_FWD = 256
TM_A_BWD = 256
TM_A_DX = 512

GELU_C0 = 0.7978845608028654
GELU_C1 = 0.044715


class _Hook:
    def __init__(self, operands, out_shapes, aliases, n_sems, start, finish, middle=None, late=None):
        self.operands, self.out_shapes, self.aliases, self.n_sems = operands, out_shapes, aliases, n_sems
        self.start, self.finish, self.middle, self.late = start, finish, middle, late


class _SemView:
    def __init__(self, base, off):
        self.base, self.off = base, off

    @property
    def at(self):
        return self

    def __getitem__(self, k):
        return self.base.at[self.off + k]


def _join_hooks(*hooks):
    if len(hooks) == 1:
        return hooks[0]
    operands, out_shapes, aliases, spans = [], [], {}, []
    n_sems = 0
    for h in hooks:
        aliases.update({len(operands) + i: len(out_shapes) + o for i, o in h.aliases.items()})
        spans.append((len(operands), len(h.operands), len(out_shapes), len(h.out_shapes), n_sems))
        operands += list(h.operands)
        out_shapes += list(h.out_shapes)
        n_sems += h.n_sems

    def each(which):
        def run(ins, outs, send, recv):
            for h, (i0, ni, o0, no, s0) in zip(hooks, spans):
                step = getattr(h, which)
                if step is not None:
                    step(ins[i0 : i0 + ni], outs[o0 : o0 + no], _SemView(send, s0), _SemView(recv, s0))

        return run

    middle = each("middle") if any(h.middle is not None for h in hooks) else None
    late = each("late") if any(h.late is not None for h in hooks) else None
    return _Hook(operands, out_shapes, aliases, n_sems, each("start"), each("finish"), middle, late)


def _pcall(body, hook=None, **kw):
    if hook is None:
        return pl.pallas_call(body, **kw)
    n_pre = 0
    if "grid_spec" in kw:
        spec = kw.pop("grid_spec")
        n_pre = spec.num_scalar_prefetch
        kw.update(
            grid=tuple(spec.grid), in_specs=list(spec.in_specs), out_specs=list(spec.out_specs),
            scratch_shapes=list(spec.scratch_shapes),
        )
    n_in, n_out = len(kw["in_specs"]), len(kw["out_shape"])
    hi, ho = len(hook.operands), len(hook.out_shapes)
    grid = kw.get("grid", ())

    def wrapped(*refs):
        pre, refs = refs[:n_pre], refs[n_pre:]
        ins, h_in = refs[:n_in], refs[n_in : n_in + hi]
        outs = refs[n_in + hi : n_in + hi + n_out]
        h_out = refs[n_in + hi + n_out : n_in + hi + n_out + ho]
        scratch = refs[n_in + hi + n_out + ho : -2]
        send_sems, recv_sems = refs[-2:]
        if not grid:
            hook.start(h_in, h_out, send_sems, recv_sems)
            if hook.middle is not None:
                hook.middle(h_in, h_out, send_sems, recv_sems)
            body(*pre, *ins, *outs, *scratch)
            if hook.late is not None:
                hook.late(h_in, h_out, send_sems, recv_sems)
            hook.finish(h_in, h_out, send_sems, recv_sems)
            return
        first = pl.program_id(0) == 0
        last = pl.program_id(0) == grid[0] - 1
        for axis in range(1, len(grid)):
            first = jnp.logical_and(first, pl.program_id(axis) == 0)
            last = jnp.logical_and(last, pl.program_id(axis) == grid[axis] - 1)

        @pl.when(first)
        def _():
            hook.start(h_in, h_out, send_sems, recv_sems)

        for when, step in ((hook.middle, grid[0] // 4), (hook.late, grid[0] - 1)):
            if when is not None:
                assert len(grid) == 1 and grid[0] >= 4

                @pl.when(pl.program_id(0) == step)
                def _(when=when):
                    when(h_in, h_out, send_sems, recv_sems)

        body(*pre, *ins, *outs, *scratch)

        @pl.when(last)
        def _():
            hook.finish(h_in, h_out, send_sems, recv_sems)

    aliases = dict(kw.pop("input_output_aliases", {}))
    aliases.update({n_pre + n_in + i: n_out + o for i, o in hook.aliases.items()})
    kw.update(
        in_specs=list(kw["in_specs"]) + [ANY] * hi,
        out_specs=list(kw["out_specs"]) + [ANY] * ho,
        out_shape=list(kw["out_shape"]) + list(hook.out_shapes),
        scratch_shapes=list(kw.get("scratch_shapes", ()))
        + [pltpu.SemaphoreType.DMA((hook.n_sems,)), pltpu.SemaphoreType.DMA((hook.n_sems,))],
        input_output_aliases=aliases,
    )
    if n_pre:
        kw["grid_spec"] = pltpu.PrefetchScalarGridSpec(
            num_scalar_prefetch=n_pre, grid=kw.pop("grid"), in_specs=kw.pop("in_specs"),
            out_specs=kw.pop("out_specs"), scratch_shapes=kw.pop("scratch_shapes"),
        )
    call = pl.pallas_call(wrapped, **kw)

    def run(*operands):
        outs = call(*operands, *hook.operands)
        return outs[:n_out], outs[n_out:]

    return run


def _run_hook(hook, name):
    def body():
        pass

    return _pcall(body, hook, name=name, in_specs=[], out_specs=[], out_shape=[])()[1]


def _cparams(sem=None):
    return pltpu.CompilerParams(dimension_semantics=sem, vmem_limit_bytes=VMEM_LIMIT)


def _full(shape):
    zeros = (0,) * len(shape)
    return pl.BlockSpec(shape, lambda *_: zeros)


def _scalars(*vals):
    return jnp.stack([jnp.asarray(v, jnp.int32) for v in vals])


def _sigmoid(x):
    return 1.0 / (1.0 + jnp.exp(-x))


def _gelu(x):
    t = jnp.tanh(GELU_C0 * (x + GELU_C1 * (x * x * x)))
    return x * (0.5 * (1.0 + t))


def _gelu_and_grad(x):
    x2 = x * x
    t = jnp.tanh(GELU_C0 * (x + GELU_C1 * (x2 * x)))
    cdf = 0.5 * (1.0 + t)
    return x * cdf, cdf + 0.5 * x * (1.0 - t * t) * (GELU_C0 * (1.0 + 3.0 * GELU_C1 * x2))


def _silu_and_grad(x):
    s = _sigmoid(x)
    return x * s, s * (1.0 + x * (1.0 - s))


def _softplus_neg(lam):
    u = jnp.exp(-jnp.abs(lam))
    w = 1.0 + u
    log1p = jnp.where(w == 1.0, u, jnp.log(w) * (u / jnp.where(w == 1.0, 1.0, w - 1.0)))
    return jnp.maximum(-lam, 0.0) + log1p


def _dot(a, b):
    return jnp.dot(a, b, preferred_element_type=F32)


def _dot_nt(a, b):
    return lax.dot_general(a, b, (((1,), (1,)), ((), ())), preferred_element_type=F32)


def _dot_tn(a, b):
    return lax.dot_general(a, b, (((0,), (0,)), ((), ())), preferred_element_type=F32)


def _shift_down(v, halo, k):
    if k == 0:
        return v
    rolled = pltpu.roll(v, k, 0)
    row = lax.broadcasted_iota(jnp.int32, (SUBLANES, v.shape[1]), 0)
    top = jnp.where(row < k, pltpu.roll(halo, k, 0), rolled[:SUBLANES])
    return jnp.concatenate([top, rolled[SUBLANES:]], axis=0)


def _shift_up(v, head, k):
    if k == 0:
        return v
    n = v.shape[0]
    rolled = pltpu.roll(v, n - k, 0)
    row = lax.broadcasted_iota(jnp.int32, (SUBLANES, v.shape[1]), 0)
    bot = jnp.where(row >= SUBLANES - k, pltpu.roll(head, SUBLANES - k, 0), rolled[n - SUBLANES :])
    return jnp.concatenate([rolled[: n - SUBLANES], bot], axis=0)


def _scan_blocks(a_ref, b_ref, out_ref, carry, n_rows, reverse):
    width = a_ref.shape[1]
    row = lax.broadcasted_iota(jnp.int32, (SUBLANES, width), 0)
    n_blocks = n_rows // SUBLANES

    def block(j, carry):
        i = (n_blocks - 1 - j) if reverse else j
        r0 = pl.multiple_of(i * SUBLANES, SUBLANES)
        a = a_ref[pl.ds(r0, SUBLANES), :]
        b = b_ref[pl.ds(r0, SUBLANES), :]
        for d in (1, 2, 4):
            shift = (SUBLANES - d) if reverse else d
            keep = (row < SUBLANES - d) if reverse else (row >= d)
            a_s = pltpu.roll(a, shift, 0)
            b_s = pltpu.roll(b, shift, 0)
            b = jnp.where(keep, a * b_s + b, b)
            a = jnp.where(keep, a * a_s, a)
        h = a * carry + b
        out_ref[pl.ds(r0, SUBLANES), :] = h
        edge = h[0:1, :] if reverse else h[SUBLANES - 1 : SUBLANES, :]
        return jnp.broadcast_to(edge, (SUBLANES, width))

    return lax.fori_loop(0, n_blocks, block, carry)


def _rms_fwd(x, w):
    r = lax.rsqrt(jnp.mean(x * x, axis=-1, keepdims=True) + RMS_EPS)
    xh = x * r
    return xh * w, xh, r


def _rms_bwd(dh, xh, r, w):
    dxh = dh * w
    dx = r * (dxh - xh * jnp.mean(dxh * xh, axis=-1, keepdims=True))
    return dx, jnp.sum(dh * xh, axis=0, keepdims=True)


def _cast_to_segments(ws, table, steps, hook=None, small_maps=None):
    per = steps // 2
    n = len(ws)
    maps = () if small_maps is None else small_maps

    def body(k_ref, *refs):
        for w_ref, o_ref in zip(refs[:n], refs[n + len(maps) : 2 * n + len(maps)]):
            o_ref[...] = w_ref[...].astype(BF16)
        if small_maps is not None:
            pl.when(pl.program_id(0) == 0)(lambda: _prepare_small_maps(*refs[n : n + 3], *refs[2 * n + 3 :]))

    rows = [w.shape[0] // steps for w in ws]
    segment = lambda i, k_ref: (2 * k_ref[AT_MINE] + i // per, i % per, 0)
    whole = lambda shape: pl.BlockSpec(shape, lambda i, k_ref: (0,) * len(shape))
    prepared = []
    if small_maps is not None:
        (g, ck, _), (h, hd, _) = maps[0].shape, maps[1].shape
        prepared = [(g, ck, ck), (g, ck, ck), (h, hd, 2 * hd), (h, 2 * hd, hd)]
    out = _pcall(
        body,
        hook,
        name=f"cast_{ws[0].shape[0]}x{ws[0].shape[1]}",
        grid_spec=pltpu.PrefetchScalarGridSpec(
            num_scalar_prefetch=1,
            grid=(steps,),
            in_specs=[pl.BlockSpec((r, w.shape[1]), lambda i, k_ref: (i, 0)) for w, r in zip(ws, rows)]
            + [whole(m.shape) for m in maps],
            out_specs=[pl.BlockSpec((None, r, w.shape[1]), segment) for w, r in zip(ws, rows)]
            + [whole(shape) for shape in prepared],
        ),
        out_shape=[jax.ShapeDtypeStruct((N_DEV, w.shape[0] // 2, w.shape[1]), BF16) for w in ws]
        + [jax.ShapeDtypeStruct(shape, BF16) for shape in prepared],
        compiler_params=_cparams(("arbitrary",)),
    )(table, *ws, *maps)
    return out


def _prepare_small_maps(ws_ref, ga_ref, gx_ref, wc_ref, wct_ref, gab_ref, gabt_ref):
    ck, hd = ws_ref.shape[1], ga_ref.shape[1]
    tril = lax.broadcasted_iota(jnp.int32, (ck, ck), 0) >= lax.broadcasted_iota(jnp.int32, (ck, ck), 1)
    for g in range(ws_ref.shape[0]):
        w = ws_ref[g] * tril.astype(F32)
        wc_ref[g] = w.astype(BF16)
        wct_ref[g] = w.T.astype(BF16)
    for h in range(ga_ref.shape[0]):
        for k, m_ref in enumerate((ga_ref, gx_ref)):
            m = m_ref[h]
            gab_ref[h, :, k * hd : (k + 1) * hd] = m.astype(BF16)
            gabt_ref[h, k * hd : (k + 1) * hd, :] = m.T.astype(BF16)


def _place():
    x, y, c = lax.axis_index("x"), lax.axis_index("y"), lax.axis_index("c")
    chips = [(1 - x, y), (x, 1 - y), (1 - x, 1 - y)]
    return x, y, c, chips


def _chip_no(chip):
    return 2 * chip[0] + chip[1]


def _rcopy(src, dst, send_sem, recv_sem, to):
    return pltpu.make_async_remote_copy(
        src_ref=src, dst_ref=dst, send_sem=send_sem, recv_sem=recv_sem, device_id=to, device_id_type=MESH
    )


def _gather_hook(big, small=None):
    nb = len(big)
    n_sems = 6 * nb + 4

    def places():
        x, y, c, chips = _place()
        first = (x ^ (1 - c), y ^ c)
        second = (x ^ c, y ^ (1 - c))
        return x, y, c, chips, first, second, (1 - x, 1 - y)

    def seg(outs, b, chip, half):
        return outs[b].at[2 * _chip_no(chip) + half]

    def step1(outs, send, recv):
        x, y, c, _, first, _, _ = places()
        return [
            _rcopy(seg(outs, b, (x, y), c), seg(outs, b, (x, y), c), send.at[6 * b], recv.at[6 * b], (*first, c))
            for b in range(nb)
        ]

    def step2(outs, send, recv):
        x, y, c, _, first, second, _ = places()
        copies = []
        for b in range(nb):
            for k, chip in ((1, (x, y)), (2, first)):
                src = seg(outs, b, chip, c)
                copies.append(_rcopy(src, src, send.at[6 * b + k], recv.at[6 * b + k], (*second, c)))
        return copies

    def hand_over(outs, send, recv, k, chip):
        x, y, c, *_ = places()
        return [
            _rcopy(seg(outs, b, chip, c), seg(outs, b, chip, c), send.at[6 * b + k], recv.at[6 * b + k], (x, y, 1 - c))
            for b in range(nb)
        ]

    def wait_landed(outs, send, recv, k, chip, half):
        x, y, c, *_ = places()
        for b in range(nb):
            got = seg(outs, b, chip, half)
            _rcopy(got, got, send.at[6 * b + k], recv.at[6 * b + k], (x, y, c)).wait_recv()

    def small_copies(ins, outs, send, recv):
        x, y, c, chips, *_ = places()
        there = outs[nb].at[_chip_no((x, y))]
        return [
            _rcopy(ins[nb], there, send.at[6 * nb + j], recv.at[6 * nb + j], (*chip, c)) for j, chip in enumerate(chips)
        ]

    def local_copy(ins, outs, send):
        x, y, _, _ = _place()
        return pltpu.make_async_copy(ins[nb], outs[nb].at[_chip_no((x, y))], send.at[6 * nb + 3])

    def start(ins, outs, send, recv):
        for cp in step1(outs, send, recv):
            cp.start()
        if small is not None:
            for cp in small_copies(ins, outs, send, recv):
                cp.start()
            local_copy(ins, outs, send).start()

    def middle(ins, outs, send, recv):
        *_, first, _, _ = places()
        wait_landed(outs, send, recv, 0, first, places()[2])
        for cp in step2(outs, send, recv) + hand_over(outs, send, recv, 3, first):
            cp.start()

    def late(ins, outs, send, recv):
        x, y, c, chips, first, second, diagonal = places()
        for k, chip in ((1, second), (2, diagonal)):
            wait_landed(outs, send, recv, k, chip, c)
            for cp in hand_over(outs, send, recv, 3 + k, chip):
                cp.start()

    def finish(ins, outs, send, recv):
        x, y, c, chips, first, second, diagonal = places()
        wait_landed(outs, send, recv, 3, second, 1 - c)
        wait_landed(outs, send, recv, 4, first, 1 - c)
        wait_landed(outs, send, recv, 5, diagonal, 1 - c)
        sent = step1(outs, send, recv) + step2(outs, send, recv)
        for k, chip in ((3, first), (4, second), (5, diagonal)):
            sent += hand_over(outs, send, recv, k, chip)
        for cp in sent:
            cp.wait_send()
        if small is not None:
            for j, chip in enumerate(chips):
                got = outs[nb].at[_chip_no(chip)]
                _rcopy(got, got, send.at[6 * nb + j], recv.at[6 * nb + j], (x, y, c)).wait_recv()
            for cp in small_copies(ins, outs, send, recv):
                cp.wait_send()
            local_copy(ins, outs, send).wait()

    operands = list(big) + ([small] if small is not None else [])
    out_shapes = [jax.ShapeDtypeStruct(b.shape, b.dtype) for b in big]
    if small is not None:
        out_shapes.append(jax.ShapeDtypeStruct((N_CHIPS, *small.shape), small.dtype))
    return _Hook(operands, out_shapes, {b: b for b in range(nb)}, n_sems, start, finish, middle, late)


def _both_ways_hook(operands, out_shapes, copies_of, n_sems):
    def start(ins, outs, send, recv):
        for cp in copies_of(ins, outs, send, recv):
            cp.start()

    def finish(ins, outs, send, recv):
        for cp in copies_of(ins, outs, send, recv):
            cp.wait()

    return _Hook(operands, out_shapes, {}, n_sems, start, finish)


def _swap_hook(bufs):
    def copies_of(ins, outs, send, recv):
        x, y, c, _ = _place()
        copies = []
        for b in range(len(bufs)):
            for j in range(N_CHIPS):
                k = b * N_CHIPS + j
                copies.append(_rcopy(ins[b].at[2 * j + 1 - c], outs[b].at[j], send.at[k], recv.at[k], (x, y, 1 - c)))
        return copies

    out_shapes = [jax.ShapeDtypeStruct((N_CHIPS, *b.shape[1:]), b.dtype) for b in bufs]
    return _both_ways_hook(list(bufs), out_shapes, copies_of, len(bufs) * N_CHIPS)


def _axis_order():
    x, y, c, _ = _place()
    return (x, y), c, (x ^ (1 - c), y ^ c), (x ^ c, y ^ (1 - c)), (1 - x, 1 - y)


def _send_first_hook(parts):
    def copies_of(ins, outs, send, recv):
        _, c, first, _, _ = _axis_order()
        copies = []
        for b in range(len(parts)):
            for k in range(2):
                sem = 2 * b + k
                copies.append(_rcopy(ins[b].at[k], outs[b].at[k], send.at[sem], recv.at[sem], (*first, c)))
        return copies

    out_shapes = [jax.ShapeDtypeStruct((2, *p.shape[1:]), p.dtype) for p in parts]
    return _both_ways_hook(list(parts), out_shapes, copies_of, len(parts) * 2)


def _send_second_hook(mids):
    def copies_of(ins, outs, send, recv):
        _, c, _, second, _ = _axis_order()
        return [_rcopy(ins[b], outs[b], send.at[b], recv.at[b], (*second, c)) for b in range(len(mids))]

    out_shapes = [jax.ShapeDtypeStruct(m.shape, m.dtype) for m in mids]
    return _both_ways_hook(list(mids), out_shapes, copies_of, len(mids))


def _share_hook(big, small=None, tiny=None):
    nb = len(big)
    n_sems = nb + 7 + N_DEV
    t0 = nb + 7

    def tiny_copies(ins, outs, send, recv):
        x, y, c, _ = _place()
        there = outs[-1].at[2 * _chip_no((x, y)) + c]
        copies = []
        for r in range(1, N_DEV):
            to = (x ^ (r >> 2 & 1), y ^ (r >> 1 & 1), c ^ (r & 1))
            copies.append(_rcopy(ins[-1], there, send.at[t0 + r], recv.at[t0 + r], to))
        return copies

    def tiny_local(ins, outs, send):
        x, y, c, _ = _place()
        return pltpu.make_async_copy(ins[-1], outs[-1].at[2 * _chip_no((x, y)) + c], send.at[t0])

    def first_copies(outs, send, recv):
        x, y, c, chips = _place()
        sibling = (x, y, 1 - c)
        copies = [_rcopy(outs[b].at[c], outs[b].at[c], send.at[b], recv.at[b], sibling) for b in range(nb)]
        if small is not None:
            own = outs[nb].at[2 * _chip_no((x, y)) + c]
            copies.append(_rcopy(own, own, send.at[nb], recv.at[nb], sibling))
            for j, chip in enumerate(chips):
                copies.append(_rcopy(own, own, send.at[nb + 1 + j], recv.at[nb + 1 + j], (*chip, c)))
        return copies

    def start(ins, outs, send, recv):
        for cp in first_copies(outs, send, recv):
            cp.start()
        if tiny is not None:
            for cp in tiny_copies(ins, outs, send, recv):
                cp.start()
            tiny_local(ins, outs, send).start()

    def finish(ins, outs, send, recv):
        x, y, c, chips = _place()
        me, sibling = (x, y, c), (x, y, 1 - c)
        if tiny is not None:
            for cp in tiny_copies(ins, outs, send, recv):
                cp.wait()
            tiny_local(ins, outs, send).wait()
        passed = []
        if small is not None:
            for j, chip in enumerate(chips):
                got = outs[nb].at[2 * _chip_no(chip) + c]
                _rcopy(got, got, send.at[nb + 1 + j], recv.at[nb + 1 + j], me).wait_recv()
                fwd = _rcopy(got, got, send.at[nb + 4 + j], recv.at[nb + 4 + j], sibling)
                fwd.start()
                passed.append(fwd)
        for b in range(nb):
            got = outs[b].at[1 - c]
            _rcopy(got, got, send.at[b], recv.at[b], me).wait_recv()
        if small is not None:
            got = outs[nb].at[2 * _chip_no((x, y)) + 1 - c]
            _rcopy(got, got, send.at[nb], recv.at[nb], me).wait_recv()
            for j, chip in enumerate(chips):
                got = outs[nb].at[2 * _chip_no(chip) + 1 - c]
                _rcopy(got, got, send.at[nb + 4 + j], recv.at[nb + 4 + j], me).wait_recv()
        for cp in first_copies(outs, send, recv) + passed:
            cp.wait_send()

    operands = list(big) + ([small] if small is not None else [])
    out_shapes = [jax.ShapeDtypeStruct(a.shape, a.dtype) for a in operands]
    aliases = {i: i for i in range(len(operands))}
    if tiny is not None:
        operands.append(tiny)
        out_shapes.append(jax.ShapeDtypeStruct((N_DEV, *tiny.shape), tiny.dtype))
    return _Hook(operands, out_shapes, aliases, n_sems, start, finish)


SUM_BLOCK_BYTES = 2 * 1024 * 1024


def _row_tile(rows, cols):
    best = SUBLANES
    for t in range(SUBLANES, rows + 1, SUBLANES):
        if rows % t == 0 and t * cols * 4 <= SUM_BLOCK_BYTES:
            best = t
    return best


def _halves(buf):
    return buf.reshape(N_CHIPS, 2, *buf.shape[1:])


AT_ZERO, AT_ONE, AT_FIRST, AT_SECOND, AT_DIAGONAL, AT_MINE, AT_CORE, AT_DEVICE = range(8)


def _index_table():
    x, y, c = lax.axis_index("x"), lax.axis_index("y"), lax.axis_index("c")
    first, second = (x ^ (1 - c), y ^ c), (x ^ c, y ^ (1 - c))
    mine = _chip_no((x, y))
    return _scalars(0, 1, _chip_no(first), _chip_no(second), N_CHIPS - 1 - mine, mine, c, 2 * mine + c)


def _sum_jobs(jobs, name, table):
    main, riders = jobs[0], jobs[1:]
    rows, cols = main[0][0][0][0].shape[-2:]
    tr = _row_tile(rows, cols)
    n_main = len(main[0][0])
    in_specs, operands, out_specs, out_shape = [], [], [], []

    def walked(leads):
        base = leads[0]
        strides = [b - a for a, b in zip(leads[0], leads[1])] if len(leads) > 1 else [0] * len(base)
        assert all(lead == tuple(a + g * s for a, s in zip(base, strides)) for g, lead in enumerate(leads))
        return lambda j, r, s_ref: (*[s_ref[a + j * s] for a, s in zip(base, strides)], r, 0)

    for t, (arr, lead) in enumerate(main[0][0]):
        operands.append(arr)
        in_specs.append(pl.BlockSpec((None,) * len(lead) + (tr, cols), walked([group[t][1] for group in main[0]])))
    out_specs.append(pl.BlockSpec((None,) * len(main[1]) + (tr, cols), walked(main[2])))
    out_shape.append(jax.ShapeDtypeStruct((*main[1], rows, cols), main[3]))

    whole = lambda lead: lambda j, r, s_ref: (*[s_ref[at] for at in lead], 0, 0)
    for groups, out_dims, out_leads, dtype in riders:
        r_k, c_k = groups[0][0][0].shape[-2:]
        for group in groups:
            for arr, lead in group:
                operands.append(arr)
                in_specs.append(pl.BlockSpec((None,) * len(lead) + (r_k, c_k), whole(lead)))
        if len(groups) == 1:
            out_specs.append(pl.BlockSpec((None,) * len(out_dims) + (r_k, c_k), whole(out_leads[0])))
        else:
            assert out_dims == (len(groups),) and list(out_leads) == [(AT_ZERO + g,) for g in range(len(groups))]
            out_specs.append(pl.BlockSpec((len(groups), r_k, c_k), lambda j, r, s_ref: (0, 0, 0)))
        out_shape.append(jax.ShapeDtypeStruct((*out_dims, r_k, c_k), dtype))

    def total(term_refs):
        acc = term_refs[0][...].astype(F32)
        for t_ref in term_refs[1:]:
            acc = acc + t_ref[...].astype(F32)
        return acc

    def body(s_ref, *refs):
        ins, outs = refs[: len(operands)], refs[len(operands) :]
        outs[0][...] = total(ins[:n_main]).astype(main[3])

        def ride():
            at = n_main
            for (groups, _, _, dtype), o_ref in zip(riders, outs[1:]):
                for g, group in enumerate(groups):
                    acc = total(ins[at : at + len(group)]).astype(dtype)
                    at += len(group)
                    if len(groups) == 1:
                        o_ref[...] = acc
                    else:
                        o_ref[g] = acc

        if riders:
            pl.when(jnp.logical_and(pl.program_id(0) == 0, pl.program_id(1) == 0))(ride)

    return _pcall(
        body,
        name=name,
        grid_spec=pltpu.PrefetchScalarGridSpec(
            num_scalar_prefetch=1, grid=(len(main[0]), rows // tr), in_specs=in_specs, out_specs=out_specs
        ),
        out_shape=out_shape,
        compiler_params=_cparams(("arbitrary", "arbitrary")),
    )(table, *operands)


def _own_half_job(buf, got, wire):
    groups = [[(_halves(buf), (j, AT_CORE)), (got, (j,))] for j in (AT_FIRST, AT_DIAGONAL)]
    return groups, (2,), [(AT_ZERO,), (AT_ONE,)], wire


def _for_neighbour_job(buf, got, got1, wire):
    return [[(_halves(buf), (AT_SECOND, AT_CORE)), (got, (AT_SECOND,)), (got1, (AT_ONE,))]], (), [()], wire


def _received_job(buf, got, got1, got2, slot, n_slots):
    terms = [(_halves(buf), (AT_MINE, AT_CORE)), (got, (AT_MINE,)), (got1, (AT_ZERO,)), (got2, ())]
    return [terms], (n_slots,), [(slot,)], F32


def _layer_a_fwd(x, nw, win, ln_w, ln_b, wc, bs_t, wout, tm, hook):
    t_rows, d = x.shape
    n_sh, _, s_cols = win.shape
    aw = wout.shape[0]
    gd = aw // A_GROUPS
    tn = 512
    assert s_cols % tn == 0 and aw % tn == 0 and tm % CHUNK == 0

    def body(x_ref, nw_ref, win_ref, lnw_ref, lnb_ref, wc_ref, bst_ref, wout_ref, z_ref, x1_ref, h_ref, u_s, v_s, y_s):
        x = x_ref[...]
        h, _, _ = _rms_fwd(x, nw_ref[...])
        h = h.astype(BF16)
        h_ref[...] = h
        for j in range(3 * aw // tn):
            k, off = divmod(j * tn, s_cols)
            cols = slice((j * tn) % aw, (j * tn) % aw + tn)
            zj = _dot(h, win_ref[k, :, off : off + tn])
            z_ref[:, j * tn : (j + 1) * tn] = zj
            if j * tn < aw:
                u_s[:, cols] = _gelu(zj)
            elif j * tn < 2 * aw:
                v_s[:, cols] = _gelu(zj)
            else:
                u_s[:, cols] = u_s[:, cols] * (zj * _sigmoid(zj))
        v = v_s[...]
        mu = jnp.mean(v, axis=-1, keepdims=True)
        vc = v - mu
        rstd = lax.rsqrt(jnp.mean(vc * vc, axis=-1, keepdims=True) + LN_EPS)
        v_s[...] = (vc * rstd) * lnw_ref[...] + lnb_ref[...]
        for ck in range(tm // CHUNK):
            rows = slice(ck * CHUNK, (ck + 1) * CHUNK)
            for g in range(A_GROUPS):
                cols = slice(g * gd, (g + 1) * gd)
                s = _dot(wc_ref[g], v_s[rows, cols].astype(BF16)) + bst_ref[:, g : g + 1]
                y_s[rows, cols] = (u_s[rows, cols] * s).astype(BF16)
        x1_ref[...] = x + _dot(y_s[...], wout_ref[...])

    row = lambda i: (i, 0)
    return _pcall(
        body,
        hook,
        name="layer_a_fwd",
        grid=(t_rows // tm,),
        in_specs=[
            pl.BlockSpec((tm, d), row),
            _full(nw.shape),
            _full(win.shape),
            _full(ln_w.shape),
            _full(ln_b.shape),
            _full(wc.shape),
            _full(bs_t.shape),
            _full(wout.shape),
        ],
        out_specs=[pl.BlockSpec((tm, 3 * aw), row), pl.BlockSpec((tm, d), row), pl.BlockSpec((tm, d), row)],
        out_shape=[
            jax.ShapeDtypeStruct((t_rows, 3 * aw), F32),
            jax.ShapeDtypeStruct((t_rows, d), F32),
            jax.ShapeDtypeStruct((t_rows, d), BF16),
        ],
        scratch_shapes=[pltpu.VMEM((tm, aw), F32), pltpu.VMEM((tm, aw), F32), pltpu.VMEM((tm, aw), BF16)],
        compiler_params=_cparams(("arbitrary",)),
    )(x, nw, win, ln_w, ln_b, wc, bs_t, wout)


def _layer_a_bwd(dout, z, ln_w, ln_b, wc, wct, bs_t, wout, tiles, earlier, hook):
    t_rows, d = dout.shape
    aw = wout.shape[0]
    gd = aw // A_GROUPS
    tm = TM_A_BWD
    lo, hi = tiles
    n_earlier = 0 if earlier is None else len(earlier)

    def body(dout_ref, z_ref, lnw_ref, lnb_ref, wc_ref, wct_ref, bst_ref, wout_ref, *rest):
        dz_ref, y_ref, dob_ref, gws_ref, gbs_ref, glnw_ref, glnb_ref, u_s, vh_s, ds_s, dvn_s = rest[n_earlier:]

        @pl.when(pl.program_id(0) == 0)
        def _():
            gws_ref[...] = jnp.zeros_like(gws_ref)
            gbs_ref[...] = jnp.zeros_like(gbs_ref)
            glnw_ref[...] = jnp.zeros_like(glnw_ref)
            glnb_ref[...] = jnp.zeros_like(glnb_ref)

        dob = dout_ref[...].astype(BF16)
        dob_ref[...] = dob
        dy = _dot_nt(dob, wout_ref[...])

        zv = z_ref[:, aw : 2 * aw]
        vg, dvg_dz = _gelu_and_grad(zv)
        mu = jnp.mean(vg, axis=-1, keepdims=True)
        vc = vg - mu
        rstd = lax.rsqrt(jnp.mean(vc * vc, axis=-1, keepdims=True) + LN_EPS)
        vh = vc * rstd
        vh_s[...] = vh
        vn = (vh * lnw_ref[...] + lnb_ref[...]).astype(BF16)

        zu = z_ref[:, 0:aw]
        zg = z_ref[:, 2 * aw : 3 * aw]
        u, du_dz = _gelu_and_grad(zu)
        sg, dsg = _silu_and_grad(zg)
        u_s[...] = u * sg
        tril = lax.broadcasted_iota(jnp.int32, (CHUNK, CHUNK), 0) >= lax.broadcasted_iota(jnp.int32, (CHUNK, CHUNK), 1)
        for ck in range(tm // CHUNK):
            rows = slice(ck * CHUNK, (ck + 1) * CHUNK)
            for g in range(A_GROUPS):
                cols = slice(g * gd, (g + 1) * gd)
                vn_g = vn[rows, cols]
                s = _dot(wc_ref[g], vn_g) + bst_ref[:, g : g + 1]
                usg = u_s[rows, cols]
                dy_g = dy[rows, cols]
                y_ref[rows, cols] = (usg * s).astype(BF16)
                ds = dy_g * usg
                ds_s[rows, cols] = dy_g * s
                gbs_ref[:, g : g + 1] += jnp.sum(ds, axis=-1, keepdims=True)
                dsb = ds.astype(BF16)
                gws_ref[g] += jnp.where(tril, _dot_nt(dsb, vn_g), 0.0)
                dvn_s[rows, cols] = _dot(wct_ref[g], dsb)
        dusg = ds_s[...]
        dz_ref[:, 0:aw] = (dusg * sg * du_dz).astype(BF16)
        dz_ref[:, 2 * aw : 3 * aw] = (dusg * u * dsg).astype(BF16)

        dvn = dvn_s[...]
        vh = vh_s[...]
        glnw_ref[...] += jnp.sum(dvn * vh, axis=0, keepdims=True)
        glnb_ref[...] += jnp.sum(dvn, axis=0, keepdims=True)
        dvh = dvn * lnw_ref[...]
        dvg = rstd * (dvh - jnp.mean(dvh, axis=-1, keepdims=True) - vh * jnp.mean(dvh * vh, axis=-1, keepdims=True))
        dz_ref[:, aw : 2 * aw] = (dvg * dvg_dz).astype(BF16)

    row = lambda i: (i + lo, 0)
    call = _pcall(
        body,
        hook,
        name=f"layer_a_bwd_{lo}",
        grid=(hi - lo,),
        in_specs=[
            pl.BlockSpec((tm, d), row),
            pl.BlockSpec((tm, 3 * aw), row),
            _full(ln_w.shape),
            _full(ln_b.shape),
            _full(wc.shape),
            _full(wct.shape),
            _full(bs_t.shape),
            _full(wout.shape),
        ]
        + [ANY] * n_earlier,
        out_specs=[
            pl.BlockSpec((tm, 3 * aw), row),
            pl.BlockSpec((tm, aw), row),
            pl.BlockSpec((tm, d), row),
            _full((A_GROUPS, CHUNK, CHUNK)),
            _full((CHUNK, A_GROUPS)),
            _full((1, aw)),
            _full((1, aw)),
        ],
        out_shape=[
            jax.ShapeDtypeStruct((t_rows, 3 * aw), BF16),
            jax.ShapeDtypeStruct((t_rows, aw), BF16),
            jax.ShapeDtypeStruct((t_rows, d), BF16),
            jax.ShapeDtypeStruct((A_GROUPS, CHUNK, CHUNK), F32),
            jax.ShapeDtypeStruct((CHUNK, A_GROUPS), F32),
            jax.ShapeDtypeStruct((1, aw), F32),
            jax.ShapeDtypeStruct((1, aw), F32),
        ],
        scratch_shapes=[pltpu.VMEM((tm, aw), F32)] * 4,
        input_output_aliases={8 + i: i for i in range(n_earlier)},
        compiler_params=_cparams(("arbitrary",)),
    )
    return call(dout, z, ln_w, ln_b, wc, wct, bs_t, wout, *(earlier or ()))


def _layer_a_bwd_dx(dout, x, dz, nw, win, tm, tiles, earlier, hook):
    t_rows, d = x.shape
    n_sh, _, s_cols = win.shape
    lo, hi = tiles
    n_earlier = 0 if earlier is None else 1

    def body(dout_ref, x_ref, dz_ref, nw_ref, win_ref, *rest):
        gx_ref, gnw_ref = rest[n_earlier:]

        @pl.when(pl.program_id(0) == 0)
        def _():
            gnw_ref[...] = jnp.zeros_like(gnw_ref)

        dh = jnp.zeros((tm, d), F32)
        for k in range(n_sh):
            dh = dh + _dot_nt(dz_ref[:, k * s_cols : (k + 1) * s_cols], win_ref[k])
        nw = nw_ref[...]
        _, xh, r = _rms_fwd(x_ref[...], nw)
        dx, gnw = _rms_bwd(dh, xh, r, nw)
        gnw_ref[0:1, :] += gnw
        gx_ref[...] = dout_ref[...] + dx

    row = lambda i: (i + lo, 0)
    return _pcall(
        body,
        hook,
        name=f"layer_a_bwd_dx_{lo}",
        grid=(hi - lo,),
        in_specs=[
            pl.BlockSpec((tm, d), row),
            pl.BlockSpec((tm, d), row),
            pl.BlockSpec((tm, n_sh * s_cols), row),
            _full(nw.shape),
            _full(win.shape),
        ]
        + [ANY] * n_earlier,
        out_specs=[pl.BlockSpec((tm, d), row), _full((SUBLANES, d))],
        out_shape=[jax.ShapeDtypeStruct((t_rows, d), F32), jax.ShapeDtypeStruct((SUBLANES, d), F32)],
        input_output_aliases={5: 0} if n_earlier else {},
        compiler_params=_cparams(("arbitrary",)),
    )(dout, x, dz, nw, win, *([earlier] if n_earlier else []))


def _decay(r, sp_h):
    log_a = (-RG_C) * r * sp_h
    a = jnp.exp(log_a)
    mult = jnp.sqrt(jnp.tanh(-log_a) * (a * a + 1.0))
    return a, mult


ROW_CONV_B, ROW_GATE_A_B, ROW_GATE_X_B, ROW_LAMBDA = range(CONV_WIDTH, CONV_WIDTH + 4)


def _gates(xc_h, gab_ref, vec_ref, sp_h, h, hd):
    pre = _dot(xc_h.astype(BF16), gab_ref[h])
    cols = slice(h * hd, (h + 1) * hd)
    r = _sigmoid(pre[:, :hd] + vec_ref[ROW_GATE_A_B : ROW_GATE_A_B + 1, cols])
    ig = _sigmoid(pre[:, hd:] + vec_ref[ROW_GATE_X_B : ROW_GATE_X_B + 1, cols])
    a, mult = _decay(r, sp_h)
    return r, ig, a, mult


def _conv(xb, halo, vec_ref):
    xc = vec_ref[ROW_CONV_B : ROW_CONV_B + 1, :] + vec_ref[CONV_WIDTH - 1 : CONV_WIDTH, :] * xb
    for k in range(CONV_WIDTH - 1):
        xc = xc + vec_ref[k : k + 1, :] * _shift_down(xb, halo, CONV_WIDTH - 1 - k)
    return xc


def _layer_b_fwd(x1, nw, bin_w, vec, gab, bout, nf, tgt, tm):
    t_rows, d = x1.shape
    bw = bout.shape[0]
    hd = bw // B_HEADS
    nt = t_rows // tm

    def body(
        x1_ref, nw_ref, bin_ref, vec_ref, gab_ref, bout_ref, nf_ref, tgt_ref,
        z_ref, h_ref, h1_ref, dx2_ref, loss_ref, gnf_ref,
        tail_s, carry_s, a_s, b_s, hs_s, acc_s,
    ):
        @pl.when(pl.program_id(0) == 0)
        def _():
            tail_s[...] = jnp.zeros_like(tail_s)
            carry_s[...] = jnp.zeros_like(carry_s)
            acc_s[...] = jnp.zeros_like(acc_s)
            gnf_ref[...] = jnp.zeros_like(gnf_ref)

        x1 = x1_ref[...]
        h1, _, _ = _rms_fwd(x1, nw_ref[...])
        h1 = h1.astype(BF16)
        h1_ref[...] = h1
        z = jnp.concatenate([_dot(h1, bin_ref[k]) for k in range(N_CHIPS)], axis=1)
        z_ref[...] = z
        xb = z[:, :bw]
        xc = _conv(xb, tail_s[...], vec_ref)
        tail = xb[tm - SUBLANES :, :]
        tail_s[...] = tail
        sp = _softplus_neg(vec_ref[ROW_LAMBDA : ROW_LAMBDA + 1, :])
        for h in range(B_HEADS):
            cols = slice(h * hd, (h + 1) * hd)
            xc_h = xc[:, cols]
            _, ig, a, mult = _gates(xc_h, gab_ref, vec_ref, sp[:, cols], h, hd)
            a_s[:, cols] = a
            b_s[:, cols] = mult * (ig * xc_h)
        carry = _scan_blocks(a_s, b_s, hs_s, carry_s[...], tm, reverse=False)
        carry_s[...] = carry
        hs = hs_s[...]
        h_ref[...] = hs
        g = z[:, bw:]
        y = (hs * (g * _sigmoid(g))).astype(BF16)
        x2 = x1 + _dot(y, bout_ref[...])

        nf = nf_ref[...]
        o, xh, r = _rms_fwd(x2, nf)
        diff = o - tgt_ref[...]
        acc_s[...] += jnp.sum(diff * diff, axis=0, keepdims=True)
        do = diff * (1.0 / d)
        dx2, gnf = _rms_bwd(do, xh, r, nf)
        gnf_ref[...] += gnf
        dx2_ref[...] = dx2

        @pl.when(pl.program_id(0) == nt - 1)
        def _():
            total = jnp.sum(acc_s[...], axis=-1, keepdims=True) * (0.5 / d)
            loss_ref[...] = jnp.broadcast_to(total, loss_ref.shape)

    row = lambda i: (i, 0)
    return _pcall(
        body,
        name="layer_b_fwd",
        grid=(nt,),
        in_specs=[
            pl.BlockSpec((tm, d), row),
            _full(nw.shape),
            _full(bin_w.shape),
            _full(vec.shape),
            _full(gab.shape),
            _full(bout.shape),
            _full(nf.shape),
            pl.BlockSpec((tm, d), row),
        ],
        out_specs=[
            pl.BlockSpec((tm, 2 * bw), row),
            pl.BlockSpec((tm, bw), row),
            pl.BlockSpec((tm, d), row),
            pl.BlockSpec((tm, d), row),
            _full((1, LANES)),
            _full((1, d)),
        ],
        out_shape=[
            jax.ShapeDtypeStruct((t_rows, 2 * bw), F32),
            jax.ShapeDtypeStruct((t_rows, bw), F32),
            jax.ShapeDtypeStruct((t_rows, d), BF16),
            jax.ShapeDtypeStruct((t_rows, d), F32),
            jax.ShapeDtypeStruct((1, LANES), F32),
            jax.ShapeDtypeStruct((1, d), F32),
        ],
        scratch_shapes=[
            pltpu.VMEM((SUBLANES, bw), F32),
            pltpu.VMEM((SUBLANES, bw), F32),
            pltpu.VMEM((tm, bw), F32),
            pltpu.VMEM((tm, bw), F32),
            pltpu.VMEM((tm, bw), F32),
            pltpu.VMEM((1, d), F32),
        ],
        compiler_params=_cparams(("arbitrary",)),
    )(x1, nw, bin_w, vec, gab, bout, nf, tgt)


def _layer_b_bwd(dout, x1, z, hseq, nw, bin_w, vec, gab, gabt, bout, tm):
    t_rows, d = x1.shape
    bw = bout.shape[0]
    hd = bw // B_HEADS
    nt = t_rows // tm

    def body(
        dout_ref, x1_ref, z_ref, h_ref, xbt_ref, ht_ref, nw_ref, bin_ref, vec_ref, gab_ref, gabt_ref, bout_ref,
        dx1_ref, dz_ref, y_ref, dob_ref, ggab_ref, ggb_ref, gcw_ref, gcb_ref, glam_ref, gnw_ref,
        gcarry_s, afirst_s, head_s, aup_s, dh_s, gt_s, dxc_s, xc_s, r_s, ig_s,
    ):
        step = pl.program_id(0)
        tile = nt - 1 - step

        @pl.when(step == 0)
        def _():
            for ref in (ggab_ref, ggb_ref, gcw_ref, gcb_ref, glam_ref, gnw_ref, gcarry_s, afirst_s, head_s):
                ref[...] = jnp.zeros_like(ref)

        first_tile = tile == 0
        xb_halo = jnp.where(first_tile, 0.0, xbt_ref[...])
        h_halo = jnp.where(first_tile, 0.0, ht_ref[...])

        dout = dout_ref[...]
        dob = dout.astype(BF16)
        dob_ref[...] = dob
        dy = _dot_nt(dob, bout_ref[...])
        hs = h_ref[...]
        g = z_ref[:, bw:]
        sg, dsg = _silu_and_grad(g)
        y_ref[...] = (hs * sg).astype(BF16)
        dz_ref[:, bw:] = (dy * hs * dsg).astype(BF16)
        dh_s[...] = dy * sg

        xb = z_ref[:, :bw]
        xc = _conv(xb, xb_halo, vec_ref)
        xc_s[...] = xc
        lam = vec_ref[ROW_LAMBDA : ROW_LAMBDA + 1, :]
        sp = _softplus_neg(lam)
        for h in range(B_HEADS):
            cols = slice(h * hd, (h + 1) * hd)
            r, ig, a, _ = _gates(xc[:, cols], gab_ref, vec_ref, sp[:, cols], h, hd)
            r_s[:, cols] = r
            ig_s[:, cols] = ig
            aup_s[:, cols] = _shift_up(a, afirst_s[:, cols], 1)
            afirst_s[:, cols] = jnp.broadcast_to(a[0:1, :], (SUBLANES, hd))
        carry = _scan_blocks(aup_s, dh_s, gt_s, gcarry_s[...], tm, reverse=True)
        gcarry_s[...] = carry

        h_prev = _shift_down(hs, h_halo, 1)
        for h in range(B_HEADS):
            cols = slice(h * hd, (h + 1) * hd)
            xc_h = xc_s[:, cols]
            sp_h = sp[:, cols]
            r, ig = r_s[:, cols], ig_s[:, cols]
            a, mult = _decay(r, sp_h)
            gt = gt_s[:, cols]
            da = gt * h_prev[:, cols]
            dmult = gt * (ig * xc_h)
            dig = gt * (mult * xc_h)
            dxc_direct = gt * (mult * ig)
            dla = da * a - dmult * (a * a) / mult
            glam_ref[:, cols] += jnp.sum(dla * r, axis=0, keepdims=True)
            dr = dla * ((-RG_C) * sp_h)
            dpre = jnp.concatenate([dr * r * (1.0 - r), dig * ig * (1.0 - ig)], axis=1)
            ggb_ref[:, cols] += jnp.sum(dpre[:, :hd], axis=0, keepdims=True)
            ggb_ref[:, bw + h * hd : bw + (h + 1) * hd] += jnp.sum(dpre[:, hd:], axis=0, keepdims=True)
            dpb = dpre.astype(BF16)
            ggab_ref[h] += _dot_tn(xc_h.astype(BF16), dpb)
            dxc_s[:, cols] = dxc_direct + _dot(dpb, gabt_ref[h])
        glam_ref[...] = jnp.where(step == nt - 1, glam_ref[...] * (RG_C * _sigmoid(-lam)), glam_ref[...])

        dxc = dxc_s[...]
        gcb_ref[...] += jnp.sum(dxc, axis=0, keepdims=True)
        dxb = vec_ref[CONV_WIDTH - 1 : CONV_WIDTH, :] * dxc
        gcw_ref[CONV_WIDTH - 1 : CONV_WIDTH, :] += jnp.sum(dxc * xb, axis=0, keepdims=True)
        head = head_s[...]
        for k in range(CONV_WIDTH - 1):
            lag = CONV_WIDTH - 1 - k
            dxb = dxb + vec_ref[k : k + 1, :] * _shift_up(dxc, head, lag)
            gcw_ref[k : k + 1, :] += jnp.sum(dxc * _shift_down(xb, xb_halo, lag), axis=0, keepdims=True)
        head_s[...] = dxc[:SUBLANES, :]
        dz_ref[:, :bw] = dxb.astype(BF16)

        s_cols = 2 * bw // N_CHIPS
        dh1 = jnp.zeros((tm, d), F32)
        for k in range(N_CHIPS):
            dh1 = dh1 + _dot_nt(dz_ref[:, k * s_cols : (k + 1) * s_cols], bin_ref[k])
        x1 = x1_ref[...]
        nw = nw_ref[...]
        _, xh, r1 = _rms_fwd(x1, nw)
        dx, gnw = _rms_bwd(dh1, xh, r1, nw)
        gnw_ref[...] += gnw
        dx1_ref[...] = dout + dx

    rev = lambda i: (nt - 1 - i, 0)
    prev = lambda i: (jnp.maximum((nt - 1 - i) * (tm // SUBLANES) - 1, 0), 0)
    return _pcall(
        body,
        name="layer_b_bwd",
        grid=(nt,),
        in_specs=[
            pl.BlockSpec((tm, d), rev),
            pl.BlockSpec((tm, d), rev),
            pl.BlockSpec((tm, 2 * bw), rev),
            pl.BlockSpec((tm, bw), rev),
            pl.BlockSpec((SUBLANES, bw), prev),
            pl.BlockSpec((SUBLANES, bw), prev),
            _full(nw.shape),
            _full(bin_w.shape),
            _full(vec.shape),
            _full(gab.shape),
            _full(gabt.shape),
            _full(bout.shape),
        ],
        out_specs=[
            pl.BlockSpec((tm, d), rev),
            pl.BlockSpec((tm, 2 * bw), rev),
            pl.BlockSpec((tm, bw), rev),
            pl.BlockSpec((tm, d), rev),
            _full((B_HEADS, hd, 2 * hd)),
            _full((1, 2 * bw)),
            _full((SUBLANES, bw)),
            _full((1, bw)),
            _full((1, bw)),
            _full((1, d)),
        ],
        out_shape=[
            jax.ShapeDtypeStruct((t_rows, d), F32),
            jax.ShapeDtypeStruct((t_rows, 2 * bw), BF16),
            jax.ShapeDtypeStruct((t_rows, bw), BF16),
            jax.ShapeDtypeStruct((t_rows, d), BF16),
            jax.ShapeDtypeStruct((B_HEADS, hd, 2 * hd), F32),
            jax.ShapeDtypeStruct((1, 2 * bw), F32),
            jax.ShapeDtypeStruct((SUBLANES, bw), F32),
            jax.ShapeDtypeStruct((1, bw), F32),
            jax.ShapeDtypeStruct((1, bw), F32),
            jax.ShapeDtypeStruct((1, d), F32),
        ],
        scratch_shapes=[pltpu.VMEM((SUBLANES, bw), F32)] * 3 + [pltpu.VMEM((tm, bw), F32)] * 7,
        compiler_params=_cparams(("arbitrary",)),
    )(dout, x1, z, hseq, z, hseq, nw, bin_w, vec, gab, gabt, bout)


def _wgrad(a, b, m_blocks, n_blocks, hook=None, wire_copy=False):
    k, m = a.shape
    n = b.shape[1]
    bm, bn = m // m_blocks, n // n_blocks

    def body(a_ref, b_ref, o_ref, *wire_ref):
        prod = _dot_tn(a_ref[...], b_ref[...])
        o_ref[...] = prod
        if wire_copy:
            wire_ref[0][...] = prod.astype(BF16)

    out_spec = pl.BlockSpec((None, None, bm, bn), lambda j, i: (j, i, 0, 0))
    shape = (n_blocks, m_blocks, bm, bn)
    out = _pcall(
        body,
        hook,
        name=f"wgrad_{m}x{n}",
        grid=(n_blocks, m_blocks),
        in_specs=[pl.BlockSpec((k, bm), lambda j, i: (0, i)), pl.BlockSpec((k, bn), lambda j, i: (0, j))],
        out_specs=[out_spec] * (1 + wire_copy),
        out_shape=[jax.ShapeDtypeStruct(shape, F32)] + [jax.ShapeDtypeStruct(shape, BF16)] * wire_copy,
        compiler_params=_cparams(("arbitrary", "arbitrary")),
    )(a, b)
    outs, rode = (out, None) if hook is None else out
    outs = outs if wire_copy else outs[0]
    return outs if hook is None else (outs, rode)


def _adamw_math(w, g, m, v):
    m = ADAM_B1 * m + (1.0 - ADAM_B1) * g
    v = ADAM_B2 * v + (1.0 - ADAM_B2) * (g * g)
    m_hat = m / (1.0 - ADAM_B1**ADAM_STEP)
    v_hat = v / (1.0 - ADAM_B2**ADAM_STEP)
    delta = -ADAM_LR * (m_hat / (jnp.sqrt(v_hat) + ADAM_EPS) + ADAM_WD * w)
    return delta, m, v


ADAMW_ROW_TILES = 8


def _adamw_rows(ws, gs, ms, vs):
    n = len(ws)

    def body(*refs):
        ins, outs = refs[: 4 * n], refs[4 * n :]
        for i in range(n):
            w_ref, g_ref, m_ref, v_ref = ins[i::n]
            d_ref, mo_ref, vo_ref, go_ref = outs[i::n]
            g = g_ref[...]
            d_ref[...], mo_ref[...], vo_ref[...] = _adamw_math(w_ref[...], g, m_ref[...], v_ref[...])
            go_ref[...] = g

    specs = [pl.BlockSpec((w.shape[0] // ADAMW_ROW_TILES, w.shape[1]), lambda i: (i, 0)) for w in ws]
    outs = _pcall(
        body,
        name="adamw_rows",
        grid=(ADAMW_ROW_TILES,),
        in_specs=specs * 4,
        out_specs=specs * 4,
        out_shape=[jax.ShapeDtypeStruct(w.shape, F32) for w in ws] * 4,
        compiler_params=_cparams(("arbitrary",)),
    )(*ws, *gs, *ms, *vs)
    return [outs[k * n : (k + 1) * n] for k in range(4)]


def _sum_partials(parts):
    def body(p_ref, o_ref):
        total = p_ref[0, 0:1, :]
        for k in range(1, N_DEV):
            total = total + p_ref[k, 0:1, :]
        o_ref[...] = total

    vmem = pl.BlockSpec(memory_space=pltpu.VMEM)
    return _pcall(
        body,
        name="sum_partials",
        in_specs=[vmem],
        out_specs=vmem,
        out_shape=jax.ShapeDtypeStruct((1, parts.shape[2]), F32),
    )(parts)


def _adamw_many(ws, gs, ms, vs, pack):
    n = len(ws)
    packed = [i for i in range(n) if isinstance(gs[i], int)]
    g_arrays = [g for g in gs if not isinstance(g, int)]

    def body(*refs):
        w_refs, m_refs, v_refs = (refs[i * n : (i + 1) * n] for i in range(3))
        g_refs = list(refs[3 * n : 3 * n + len(g_arrays)])
        pack_ref = refs[3 * n + len(g_arrays)]
        outs = refs[3 * n + len(g_arrays) + 1 :]
        d_refs, mo_refs, vo_refs, go_refs = outs[:n], outs[n : 2 * n], outs[2 * n : 3 * n], list(outs[3 * n :])
        for i in range(n):
            if i in packed:
                g = pack_ref[gs[i] : gs[i] + ws[i].shape[0], :]
                go_refs.pop(0)[...] = g
            else:
                g = g_refs.pop(0)[...]
            d_refs[i][...], mo_refs[i][...], vo_refs[i][...] = _adamw_math(w_refs[i][...], g, m_refs[i][...], v_refs[i][...])

    vmem = pl.BlockSpec(memory_space=pltpu.VMEM)
    like = [jax.ShapeDtypeStruct(w.shape, F32) for w in ws]
    outs = _pcall(
        body,
        name="adamw_small",
        in_specs=[vmem] * (3 * n + len(g_arrays) + 1),
        out_specs=[vmem] * (3 * n + len(packed)),
        out_shape=like * 3 + [like[i] for i in packed],
        compiler_params=_cparams(),
    )(*ws, *ms, *vs, *g_arrays, pack)
    return outs[:n], outs[n : 2 * n], outs[2 * n : 3 * n], outs[3 * n :]


def _pack_rows(parts, lanes=LANES):
    flat = jnp.concatenate([p.reshape(-1) for p in parts])
    per = N_DEV * SUBLANES * lanes
    total = -(-flat.shape[0] // per) * per
    flat = jnp.pad(flat, (0, total - flat.shape[0]))
    return flat.reshape(N_DEV, total // (N_DEV * lanes), lanes)


def _size(shape):
    n = 1
    for dim in shape:
        n *= dim
    return n


def _unpack(flat, shapes):
    out, at = [], 0
    for s in shapes:
        out.append(flat[at : at + _size(s)].reshape(s))
        at += _size(s)
    return out


def kernel(x, norm_w, a_w_in, a_ln_w, a_ln_b, a_w_s, a_b_s, a_w_out, b_w_in, b_conv_w, b_conv_b, b_gate_a_w, b_gate_a_b, b_gate_x_w, b_gate_x_b, b_lambda, b_w_out, norm_f_w, loss_target, m_norm_w, m_a_w_in, m_a_ln_w, m_a_ln_b, m_a_w_s, m_a_b_s, m_a_w_out, m_b_w_in, m_b_conv_w, m_b_conv_b, m_b_gate_a_w, m_b_gate_a_b, m_b_gate_x_w, m_b_gate_x_b, m_b_lambda, m_b_w_out, m_norm_f_w, v_norm_w, v_a_w_in, v_a_ln_w, v_a_ln_b, v_a_w_s, v_a_b_s, v_a_w_out, v_b_w_in, v_b_conv_w, v_b_conv_b, v_b_gate_a_w, v_b_gate_a_b, v_b_gate_x_w, v_b_gate_x_b, v_b_lambda, v_b_w_out, v_norm_f_w):
    t_rows, d = x.shape[1], x.shape[2]
    aw = a_ln_w.shape[1]
    bw = b_gate_a_w.shape[1] * b_gate_a_w.shape[2]
    hd = bw // B_HEADS
    mine = 2 * lax.axis_index("x") + lax.axis_index("y")
    core = lax.axis_index("c")
    weights = dict(norm_w=norm_w, a_w_in=a_w_in, a_ln_w=a_ln_w, a_ln_b=a_ln_b, a_w_s=a_w_s, a_b_s=a_b_s, a_w_out=a_w_out, b_w_in=b_w_in, b_conv_w=b_conv_w, b_conv_b=b_conv_b, b_gate_a_w=b_gate_a_w, b_gate_a_b=b_gate_a_b, b_gate_x_w=b_gate_x_w, b_gate_x_b=b_gate_x_b, b_lambda=b_lambda, b_w_out=b_w_out, norm_f_w=norm_f_w)
    m_in = dict(norm_w=m_norm_w, a_w_in=m_a_w_in, a_ln_w=m_a_ln_w, a_ln_b=m_a_ln_b, a_w_s=m_a_w_s, a_b_s=m_a_b_s, a_w_out=m_a_w_out, b_w_in=m_b_w_in, b_conv_w=m_b_conv_w, b_conv_b=m_b_conv_b, b_gate_a_w=m_b_gate_a_w, b_gate_a_b=m_b_gate_a_b, b_gate_x_w=m_b_gate_x_w, b_gate_x_b=m_b_gate_x_b, b_lambda=m_b_lambda, b_w_out=m_b_w_out, norm_f_w=m_norm_f_w)
    v_in = dict(norm_w=v_norm_w, a_w_in=v_a_w_in, a_ln_w=v_a_ln_w, a_ln_b=v_a_ln_b, a_w_s=v_a_w_s, a_b_s=v_a_b_s, a_w_out=v_a_w_out, b_w_in=v_b_w_in, b_conv_w=v_b_conv_w, b_conv_b=v_b_conv_b, b_gate_a_w=v_b_gate_a_w, b_gate_a_b=v_b_gate_a_b, b_gate_x_w=v_b_gate_x_w, b_gate_x_b=v_b_gate_x_b, b_lambda=v_b_lambda, b_w_out=v_b_w_out, norm_f_w=v_norm_f_w)

    table = _index_table()
    win_l, wout_l = _cast_to_segments([a_w_in[0], a_w_out[0]], table, 4)
    small_l = jnp.concatenate([b_conv_w[0], b_conv_b, b_gate_a_b, b_gate_x_b, b_lambda], axis=0)
    (bin_l, bout_l, wc, wct, gab, gabt), (win_g, wout_g, small_g) = _cast_to_segments(
        [b_w_in[0], b_w_out[0]], table, 8, _gather_hook([win_l, wout_l], small_l),
        (a_w_s[0], b_gate_a_w[0], b_gate_x_w[0]),
    )
    win = win_g.reshape(N_CHIPS, d, -1)
    wout = wout_g.reshape(aw, d)
    bs_t = a_b_s[0].T
    nw0, nw1, nf = norm_w[0:1], norm_w[1:2], norm_f_w.reshape(1, d)

    x0 = x[0]
    (z_a, x1, h0), (bin_g, bout_g) = _layer_a_fwd(
        x0, nw0, win, a_ln_w, a_ln_b, wc, bs_t, wout, TM_FWD, _gather_hook([bin_l, bout_l])
    )
    bin_w = bin_g.reshape(N_CHIPS, d, -1)
    bout = bout_g.reshape(bw, d)
    vec = jnp.transpose(small_g, (1, 0, 2)).reshape(SUBLANES, bw)
    z_b, hseq, h1, dx2, loss_l, g_nf = _layer_b_fwd(x1, nw1, bin_w, vec, gab, bout, nf, loss_target[0], TM_FWD)
    dx1, dz_b, y_b, dob_b, g_gab, g_gb, g_cw, g_cb, g_lam, g_nw1 = _layer_b_bwd(
        dx2, x1, z_b, hseq, nw1, bin_w, vec, gab, gabt, bout, TM_FWD
    )
    seg = lambda g: g.reshape(N_DEV, -1, g.shape[3])
    sums = lambda jobs, name: _sum_jobs(jobs, name, table)
    own_half, for_neighbour = _own_half_job, _for_neighbour_job
    received = lambda buf, got, got1, got2: _received_job(buf, got, got1, got2, AT_CORE, 2)

    g_o = seg(_wgrad(y_b, dob_b, 2, 1))
    g_i, (swap_o,) = _wgrad(h1, dz_b, 1, N_CHIPS, _swap_hook([g_o]))
    g_i = seg(g_i)
    (part_o,) = sums([own_half(g_o, swap_o, BF16)], "add_own_half_o")
    a_args = (z_a, a_ln_w, a_ln_b, wc, wct, bs_t, wout)
    half = t_rows // TM_A_BWD // 2
    first, (swap_i, got1_o) = _layer_a_bwd(
        dx1, *a_args, (0, half), None, _join_hooks(_swap_hook([g_i]), _send_first_hook([part_o]))
    )
    part_i, mid_o = sums(
        [own_half(g_i, swap_i, BF16), for_neighbour(g_o, swap_o, got1_o, BF16)], "add_own_half_i_for_neighbour_o"
    )
    second, (got1_i, got2_o) = _layer_a_bwd(
        dx1, *a_args, (half, 2 * half), first[:3], _join_hooks(_send_first_hook([part_i]), _send_second_hook([mid_o]))
    )
    dz_a, y_a, dob_a = second[:3]
    g_ws, g_bst, g_lnw, g_lnb = (p + q for p, q in zip(first[3:], second[3:]))
    mid_i, red_o = sums(
        [for_neighbour(g_i, swap_i, got1_i, BF16), received(g_o, swap_o, got1_o, got2_o)],
        "add_for_neighbour_i_received_o",
    )
    small_shapes = [
        (1, d), (1, aw), (1, aw), (A_GROUPS, CHUNK, CHUNK), (A_GROUPS, CHUNK), (B_HEADS, hd, hd), (B_HEADS, hd, hd),
        (d,), (CONV_WIDTH, bw), (1, bw), (1, bw), (1, bw), (1, bw), (1, 1),
    ]
    small = _pack_rows(
        [
            g_nw1, g_lnw, g_lnb, g_ws, g_bst.T, g_gab[:, :, :hd], g_gab[:, :, hd:],
            g_nf, g_cw[:CONV_WIDTH], g_cb, g_gb[:, :bw], g_gb[:, bw:], g_lam, loss_l[:, :1],
        ]
    )
    (g_w, g_w_wire), (got2_i, gr_bout, swap_s) = _wgrad(
        h0, dz_a, 1, N_CHIPS, _join_hooks(_send_second_hook([mid_i]), _share_hook([red_o]), _swap_hook([small])),
        wire_copy=True,
    )
    g_w, g_w_wire = seg(g_w), seg(g_w_wire)
    (red_i,) = sums([received(g_i, swap_i, got1_i, got2_i)], "add_received_i")
    g_u, (swap_w,) = _wgrad(y_a, dob_a, N_CHIPS, 1, _swap_hook([g_w_wire]))
    g_u = seg(g_u)

    part_w, part_s = sums([own_half(g_w, swap_w, BF16), own_half(small, swap_s, F32)], "add_own_half_w")
    (grad_x, g_nw0_mine), (got1_w, got1_s, swap_u, gr_bin) = _layer_a_bwd_dx(
        dx1, x0, dz_a, nw0, win, TM_A_DX, (0, t_rows // TM_A_DX), None,
        _join_hooks(_send_first_hook([part_w, part_s]), _swap_hook([g_u]), _share_hook([red_i])),
    )
    mid_w, mid_s, part_u = sums(
        [
            for_neighbour(g_w, swap_w, got1_w, BF16), for_neighbour(small, swap_s, got1_s, F32),
            own_half(g_u, swap_u, BF16),
        ],
        "add_for_neighbour_w_own_half_u",
    )
    got2_w, got2_s, got1_u = _run_hook(
        _join_hooks(_send_second_hook([mid_w, mid_s]), _send_first_hook([part_u])), "second_axis"
    )
    red_w, red_s, mid_u = sums(
        [
            received(g_w, swap_w, got1_w, got2_w),
            _received_job(small, swap_s, got1_s, got2_s, AT_DEVICE, N_DEV),
            for_neighbour(g_u, swap_u, got1_u, BF16),
        ],
        "add_received_w_for_neighbour_u",
    )
    got2_u, gr_win, small_r, g_nw0_all = _run_hook(
        _join_hooks(_send_second_hook([mid_u]), _share_hook([red_w], red_s, g_nw0_mine)), "second_axis_and_share"
    )
    (red_u,) = sums([received(g_u, swap_u, got1_u, got2_u)], "add_received_u")
    (gr_wout,) = _run_hook(_share_hook([red_u]), "share_reduced")
    g_nw0 = _sum_partials(g_nw0_all)
    (g_nw1_r, g_a_ln_w, g_a_ln_b, g_a_w_s, g_a_b_s, g_gate_a_w, g_gate_x_w, g_norm_f, gf_cw, gf_cb, gf_gab, gf_gxb,
     gf_lam, loss) = _unpack(small_r.reshape(-1), small_shapes)
    g_norm_w = jnp.concatenate([g_nw0, g_nw1_r], axis=0)
    shard = lambda g: lax.dynamic_slice_in_dim(g, mine * (bw // N_CHIPS), bw // N_CHIPS, axis=1)

    grads = {
        "norm_w": g_norm_w, "a_w_in": gr_win, "a_ln_w": g_a_ln_w, "a_ln_b": g_a_ln_b,
        "a_b_s": g_a_b_s[None], "a_w_out": gr_wout, "b_w_in": gr_bin, "b_conv_w": shard(gf_cw)[None],
        "b_conv_b": shard(gf_cb), "b_gate_a_b": shard(gf_gab),
        "b_gate_x_b": shard(gf_gxb), "b_lambda": shard(gf_lam),
        "b_w_out": gr_bout, "norm_f_w": g_norm_f,
    }
    pack_names = [
        "norm_w", "a_ln_w", "a_ln_b", "a_w_s", "a_b_s", "b_gate_a_w", "b_gate_x_w", "norm_f_w", "b_conv_w", "b_conv_b",
        "b_gate_a_b", "b_gate_x_b", "b_lambda", "loss",
    ]
    row_of, at = {}, 0
    for n, shape in zip(pack_names, small_shapes):
        row_of[n] = at // LANES
        at += _size(shape)
    names = list(weights)
    big_names = ["a_w_in", "a_w_out", "b_w_in", "b_w_out"]
    small_names = [n for n in names if n not in big_names]
    from_pack = [n for n in small_names if n in ("a_w_s", "b_gate_a_w", "b_gate_x_w")]
    as_rows = lambda n, a: a.reshape(-1, LANES) if n in from_pack else a.reshape(1, -1) if a.ndim == 1 else a
    *small_out, pack_grads = _adamw_many(
        [as_rows(n, weights[n]) for n in small_names],
        [row_of[n] if n in from_pack else as_rows(n, grads[n]) for n in small_names],
        [as_rows(n, m_in[n]) for n in small_names],
        [as_rows(n, v_in[n]) for n in small_names],
        small_r.reshape(-1, LANES),
    )
    grads.update(zip(from_pack, pack_grads))
    *big_out, big_grads = _adamw_rows(
        *[[src[n].reshape(weights[n].shape[-2:]) for n in big_names] for src in (weights, grads, m_in, v_in)]
    )
    grads.update(zip(big_names, big_grads))
    delta, new_m, new_v = {}, {}, {}
    for dst, small_vals, big_vals in zip((delta, new_m, new_v), small_out, big_out):
        dst.update(zip(small_names, small_vals))
        dst.update(zip(big_names, big_vals))
    for dst in (grads, delta, new_m, new_v):
        for n in names:
            dst[n] = dst[n].reshape(weights[n].shape)

    return (
        loss.reshape(()),
        grad_x[None],
        *[grads[n] for n in names],
        *[delta[n] for n in names],
        *[new_m[n] for n in names],
        *[new_v[n] for n in names],
    )
```

```python
import jax
import jax.numpy as jnp
from jax import lax
from jax.experimental import pallas as pl
from jax.experimental.pallas import tpu as pltpu

F32 = jnp.float32
BF16 = jnp.bfloat16

RMS_EPS = 1e-6
LN_EPS = 1e-5
RG_C = 8.0
CHUNK = 128
A_GROUPS = 8
B_HEADS = 12
CONV_WIDTH = 4

ADAM_LR = 0.001
ADAM_B1 = 0.9
ADAM_B2 = 0.999
ADAM_EPS = 1e-08
ADAM_WD = 0.01
ADAM_STEP = 10

N_CHIPS = 4
N_DEV = 8
SUBLANES = 8
LANES = 128
V7X_VMEM_BYTES = 64 * 1024 * 1024
VMEM_LIMIT = V7X_VMEM_BYTES * 7 // 8
MESH = pl.DeviceIdType.MESH
ANY = pl.BlockSpec(memory_space=pl.ANY)

TM_FWD = 256
TM_A_BWD = 256
TM_A_DX = 512

GELU_C0 = 0.7978845608028654
GELU_C1 = 0.044715


class _Hook:
    def __init__(self, operands, out_shapes, aliases, n_sems, start, finish, middle=None, late=None):
        self.operands, self.out_shapes, self.aliases, self.n_sems = operands, out_shapes, aliases, n_sems
        self.start, self.finish, self.middle, self.late = start, finish, middle, late


class _SemView:
    def __init__(self, base, off):
        self.base, self.off = base, off

    @property
    def at(self):
        return self

    def __getitem__(self, k):
        return self.base.at[self.off + k]


def _join_hooks(*hooks):
    if len(hooks) == 1:
        return hooks[0]
    operands, out_shapes, aliases, spans = [], [], {}, []
    n_sems = 0
    for h in hooks:
        aliases.update({len(operands) + i: len(out_shapes) + o for i, o in h.aliases.items()})
        spans.append((len(operands), len(h.operands), len(out_shapes), len(h.out_shapes), n_sems))
        operands += list(h.operands)
        out_shapes += list(h.out_shapes)
        n_sems += h.n_sems

    def each(which):
        def run(ins, outs, send, recv):
            for h, (i0, ni, o0, no, s0) in zip(hooks, spans):
                step = getattr(h, which)
                if step is not None:
                    step(ins[i0 : i0 + ni], outs[o0 : o0 + no], _SemView(send, s0), _SemView(recv, s0))

        return run

    middle = each("middle") if any(h.middle is not None for h in hooks) else None
    late = each("late") if any(h.late is not None for h in hooks) else None
    return _Hook(operands, out_shapes, aliases, n_sems, each("start"), each("finish"), middle, late)


def _pcall(body, hook=None, **kw):
    if hook is None:
        return pl.pallas_call(body, **kw)
    n_pre = 0
    if "grid_spec" in kw:
        spec = kw.pop("grid_spec")
        n_pre = spec.num_scalar_prefetch
        kw.update(
            grid=tuple(spec.grid), in_specs=list(spec.in_specs), out_specs=list(spec.out_specs),
            scratch_shapes=list(spec.scratch_shapes),
        )
    n_in, n_out = len(kw["in_specs"]), len(kw["out_shape"])
    hi, ho = len(hook.operands), len(hook.out_shapes)
    grid = kw.get("grid", ())

    def wrapped(*refs):
        pre, refs = refs[:n_pre], refs[n_pre:]
        ins, h_in = refs[:n_in], refs[n_in : n_in + hi]
        outs = refs[n_in + hi : n_in + hi + n_out]
        h_out = refs[n_in + hi + n_out : n_in + hi + n_out + ho]
        scratch = refs[n_in + hi + n_out + ho : -2]
        send_sems, recv_sems = refs[-2:]
        if not grid:
            hook.start(h_in, h_out, send_sems, recv_sems)
            if hook.middle is not None:
                hook.middle(h_in, h_out, send_sems, recv_sems)
            body(*pre, *ins, *outs, *scratch)
            if hook.late is not None:
                hook.late(h_in, h_out, send_sems, recv_sems)
            hook.finish(h_in, h_out, send_sems, recv_sems)
            return
        first = pl.program_id(0) == 0
        last = pl.program_id(0) == grid[0] - 1
        for axis in range(1, len(grid)):
            first = jnp.logical_and(first, pl.program_id(axis) == 0)
            last = jnp.logical_and(last, pl.program_id(axis) == grid[axis] - 1)

        @pl.when(first)
        def _():
            hook.start(h_in, h_out, send_sems, recv_sems)

        for when, step in ((hook.middle, grid[0] // 4), (hook.late, grid[0] - 1)):
            if when is not None:
                assert len(grid) == 1 and grid[0] >= 4

                @pl.when(pl.program_id(0) == step)
                def _(when=when):
                    when(h_in, h_out, send_sems, recv_sems)

        body(*pre, *ins, *outs, *scratch)

        @pl.when(last)
        def _():
            hook.finish(h_in, h_out, send_sems, recv_sems)

    aliases = dict(kw.pop("input_output_aliases", {}))
    aliases.update({n_pre + n_in + i: n_out + o for i, o in hook.aliases.items()})
    kw.update(
        in_specs=list(kw["in_specs"]) + [ANY] * hi,
        out_specs=list(kw["out_specs"]) + [ANY] * ho,
        out_shape=list(kw["out_shape"]) + list(hook.out_shapes),
        scratch_shapes=list(kw.get("scratch_shapes", ()))
        + [pltpu.SemaphoreType.DMA((hook.n_sems,)), pltpu.SemaphoreType.DMA((hook.n_sems,))],
        input_output_aliases=aliases,
    )
    if n_pre:
        kw["grid_spec"] = pltpu.PrefetchScalarGridSpec(
            num_scalar_prefetch=n_pre, grid=kw.pop("grid"), in_specs=kw.pop("in_specs"),
            out_specs=kw.pop("out_specs"), scratch_shapes=kw.pop("scratch_shapes"),
        )
    call = pl.pallas_call(wrapped, **kw)

    def run(*operands):
        outs = call(*operands, *hook.operands)
        return outs[:n_out], outs[n_out:]

    return run


def _run_hook(hook, name):
    def body():
        pass

    return _pcall(body, hook, name=name, in_specs=[], out_specs=[], out_shape=[])()[1]


def _cparams(sem=None):
    return pltpu.CompilerParams(dimension_semantics=sem, vmem_limit_bytes=VMEM_LIMIT)


def _full(shape):
    zeros = (0,) * len(shape)
    return pl.BlockSpec(shape, lambda *_: zeros)


def _scalars(*vals):
    return jnp.stack([jnp.asarray(v, jnp.int32) for v in vals])


def _sigmoid(x):
    return 1.0 / (1.0 + jnp.exp(-x))


def _gelu(x):
    t = jnp.tanh(GELU_C0 * (x + GELU_C1 * (x * x * x)))
    return x * (0.5 * (1.0 + t))


def _gelu_and_grad(x):
    x2 = x * x
    t = jnp.tanh(GELU_C0 * (x + GELU_C1 * (x2 * x)))
    cdf = 0.5 * (1.0 + t)
    return x * cdf, cdf + 0.5 * x * (1.0 - t * t) * (GELU_C0 * (1.0 + 3.0 * GELU_C1 * x2))


def _silu_and_grad(x):
    s = _sigmoid(x)
    return x * s, s * (1.0 + x * (1.0 - s))


def _softplus_neg(lam):
    u = jnp.exp(-jnp.abs(lam))
    w = 1.0 + u
    log1p = jnp.where(w == 1.0, u, jnp.log(w) * (u / jnp.where(w == 1.0, 1.0, w - 1.0)))
    return jnp.maximum(-lam, 0.0) + log1p


def _dot(a, b):
    return jnp.dot(a, b, preferred_element_type=F32)


def _dot_nt(a, b):
    return lax.dot_general(a, b, (((1,), (1,)), ((), ())), preferred_element_type=F32)


def _dot_tn(a, b):
    return lax.dot_general(a, b, (((0,), (0,)), ((), ())), preferred_element_type=F32)


def _shift_down(v, halo, k):
    if k == 0:
        return v
    rolled = pltpu.roll(v, k, 0)
    row = lax.broadcasted_iota(jnp.int32, (SUBLANES, v.shape[1]), 0)
    top = jnp.where(row < k, pltpu.roll(halo, k, 0), rolled[:SUBLANES])
    return jnp.concatenate([top, rolled[SUBLANES:]], axis=0)


def _shift_up(v, head, k):
    if k == 0:
        return v
    n = v.shape[0]
    rolled = pltpu.roll(v, n - k, 0)
    row = lax.broadcasted_iota(jnp.int32, (SUBLANES, v.shape[1]), 0)
    bot = jnp.where(row >= SUBLANES - k, pltpu.roll(head, SUBLANES - k, 0), rolled[n - SUBLANES :])
    return jnp.concatenate([rolled[: n - SUBLANES], bot], axis=0)


def _scan_blocks(a_ref, b_ref, out_ref, carry, n_rows, reverse):
    width = a_ref.shape[1]
    row = lax.broadcasted_iota(jnp.int32, (SUBLANES, width), 0)
    n_blocks = n_rows // SUBLANES

    def block(j, carry):
        i = (n_blocks - 1 - j) if reverse else j
        r0 = pl.multiple_of(i * SUBLANES, SUBLANES)
        a = a_ref[pl.ds(r0, SUBLANES), :]
        b = b_ref[pl.ds(r0, SUBLANES), :]
        for d in (1, 2, 4):
            shift = (SUBLANES - d) if reverse else d
            keep = (row < SUBLANES - d) if reverse else (row >= d)
            a_s = pltpu.roll(a, shift, 0)
            b_s = pltpu.roll(b, shift, 0)
            b = jnp.where(keep, a * b_s + b, b)
            a = jnp.where(keep, a * a_s, a)
        h = a * carry + b
        out_ref[pl.ds(r0, SUBLANES), :] = h
        edge = h[0:1, :] if reverse else h[SUBLANES - 1 : SUBLANES, :]
        return jnp.broadcast_to(edge, (SUBLANES, width))

    return lax.fori_loop(0, n_blocks, block, carry)


def _rms_fwd(x, w):
    r = lax.rsqrt(jnp.mean(x * x, axis=-1, keepdims=True) + RMS_EPS)
    xh = x * r
    return xh * w, xh, r


def _rms_bwd(dh, xh, r, w):
    dxh = dh * w
    dx = r * (dxh - xh * jnp.mean(dxh * xh, axis=-1, keepdims=True))
    return dx, jnp.sum(dh * xh, axis=0, keepdims=True)


def _cast_to_segments(ws, table, steps, hook=None, small_maps=None):
    per = steps // 2
    n = len(ws)
    maps = () if small_maps is None else small_maps

    def body(k_ref, *refs):
        for w_ref, o_ref in zip(refs[:n], refs[n + len(maps) : 2 * n + len(maps)]):
            o_ref[...] = w_ref[...].astype(BF16)
        if small_maps is not None:
            pl.when(pl.program_id(0) == 0)(lambda: _prepare_small_maps(*refs[n : n + 3], *refs[2 * n + 3 :]))

    rows = [w.shape[0] // steps for w in ws]
    segment = lambda i, k_ref: (2 * k_ref[AT_MINE] + i // per, i % per, 0)
    whole = lambda shape: pl.BlockSpec(shape, lambda i, k_ref: (0,) * len(shape))
    prepared = []
    if small_maps is not None:
        (g, ck, _), (h, hd, _) = maps[0].shape, maps[1].shape
        prepared = [(g, ck, ck), (g, ck, ck), (h, hd, 2 * hd), (h, 2 * hd, hd)]
    out = _pcall(
        body,
        hook,
        name=f"cast_{ws[0].shape[0]}x{ws[0].shape[1]}",
        grid_spec=pltpu.PrefetchScalarGridSpec(
            num_scalar_prefetch=1,
            grid=(steps,),
            in_specs=[pl.BlockSpec((r, w.shape[1]), lambda i, k_ref: (i, 0)) for w, r in zip(ws, rows)]
            + [whole(m.shape) for m in maps],
            out_specs=[pl.BlockSpec((None, r, w.shape[1]), segment) for w, r in zip(ws, rows)]
            + [whole(shape) for shape in prepared],
        ),
        out_shape=[jax.ShapeDtypeStruct((N_DEV, w.shape[0] // 2, w.shape[1]), BF16) for w in ws]
        + [jax.ShapeDtypeStruct(shape, BF16) for shape in prepared],
        compiler_params=_cparams(("arbitrary",)),
    )(table, *ws, *maps)
    return out


def _prepare_small_maps(ws_ref, ga_ref, gx_ref, wc_ref, wct_ref, gab_ref, gabt_ref):
    ck, hd = ws_ref.shape[1], ga_ref.shape[1]
    tril = lax.broadcasted_iota(jnp.int32, (ck, ck), 0) >= lax.broadcasted_iota(jnp.int32, (ck, ck), 1)
    for g in range(ws_ref.shape[0]):
        w = ws_ref[g] * tril.astype(F32)
        wc_ref[g] = w.astype(BF16)
        wct_ref[g] = w.T.astype(BF16)
    for h in range(ga_ref.shape[0]):
        for k, m_ref in enumerate((ga_ref, gx_ref)):
            m = m_ref[h]
            gab_ref[h, :, k * hd : (k + 1) * hd] = m.astype(BF16)
            gabt_ref[h, k * hd : (k + 1) * hd, :] = m.T.astype(BF16)


def _place():
    x, y, c = lax.axis_index("x"), lax.axis_index("y"), lax.axis_index("c")
    chips = [(1 - x, y), (x, 1 - y), (1 - x, 1 - y)]
    return x, y, c, chips


def _chip_no(chip):
    return 2 * chip[0] + chip[1]


def _rcopy(src, dst, send_sem, recv_sem, to):
    return pltpu.make_async_remote_copy(
        src_ref=src, dst_ref=dst, send_sem=send_sem, recv_sem=recv_sem, device_id=to, device_id_type=MESH
    )


def _gather_hook(big, small=None):
    nb = len(big)
    n_sems = 6 * nb + 4

    def places():
        x, y, c, chips = _place()
        first = (x ^ (1 - c), y ^ c)
        second = (x ^ c, y ^ (1 - c))
        return x, y, c, chips, first, second, (1 - x, 1 - y)

    def seg(outs, b, chip, half):
        return outs[b].at[2 * _chip_no(chip) + half]

    def step1(outs, send, recv):
        x, y, c, _, first, _, _ = places()
        return [
            _rcopy(seg(outs, b, (x, y), c), seg(outs, b, (x, y), c), send.at[6 * b], recv.at[6 * b], (*first, c))
            for b in range(nb)
        ]

    def step2(outs, send, recv):
        x, y, c, _, first, second, _ = places()
        copies = []
        for b in range(nb):
            for k, chip in ((1, (x, y)), (2, first)):
                src = seg(outs, b, chip, c)
                copies.append(_rcopy(src, src, send.at[6 * b + k], recv.at[6 * b + k], (*second, c)))
        return copies

    def hand_over(outs, send, recv, k, chip):
        x, y, c, *_ = places()
        return [
            _rcopy(seg(outs, b, chip, c), seg(outs, b, chip, c), send.at[6 * b + k], recv.at[6 * b + k], (x, y, 1 - c))
            for b in range(nb)
        ]

    def wait_landed(outs, send, recv, k, chip, half):
        x, y, c, *_ = places()
        for b in range(nb):
            got = seg(outs, b, chip, half)
            _rcopy(got, got, send.at[6 * b + k], recv.at[6 * b + k], (x, y, c)).wait_recv()

    def small_copies(ins, outs, send, recv):
        x, y, c, chips, *_ = places()
        there = outs[nb].at[_chip_no((x, y))]
        return [
            _rcopy(ins[nb], there, send.at[6 * nb + j], recv.at[6 * nb + j], (*chip, c)) for j, chip in enumerate(chips)
        ]

    def local_copy(ins, outs, send):
        x, y, _, _ = _place()
        return pltpu.make_async_copy(ins[nb], outs[nb].at[_chip_no((x, y))], send.at[6 * nb + 3])

    def start(ins, outs, send, recv):
        for cp in step1(outs, send, recv):
            cp.start()
        if small is not None:
            for cp in small_copies(ins, outs, send, recv):
                cp.start()
            local_copy(ins, outs, send).start()

    def middle(ins, outs, send, recv):
        *_, first, _, _ = places()
        wait_landed(outs, send, recv, 0, first, places()[2])
        for cp in step2(outs, send, recv) + hand_over(outs, send, recv, 3, first):
            cp.start()

    def late(ins, outs, send, recv):
        x, y, c, chips, first, second, diagonal = places()
        for k, chip in ((1, second), (2, diagonal)):
            wait_landed(outs, send, recv, k, chip, c)
            for cp in hand_over(outs, send, recv, 3 + k, chip):
                cp.start()

    def finish(ins, outs, send, recv):
        x, y, c, chips, first, second, diagonal = places()
        wait_landed(outs, send, recv, 3, second, 1 - c)
        wait_landed(outs, send, recv, 4, first, 1 - c)
        wait_landed(outs, send, recv, 5, diagonal, 1 - c)
        sent = step1(outs, send, recv) + step2(outs, send, recv)
        for k, chip in ((3, first), (4, second), (5, diagonal)):
            sent += hand_over(outs, send, recv, k, chip)
        for cp in sent:
            cp.wait_send()
        if small is not None:
            for j, chip in enumerate(chips):
                got = outs[nb].at[_chip_no(chip)]
                _rcopy(got, got, send.at[6 * nb + j], recv.at[6 * nb + j], (x, y, c)).wait_recv()
            for cp in small_copies(ins, outs, send, recv):
                cp.wait_send()
            local_copy(ins, outs, send).wait()

    operands = list(big) + ([small] if small is not None else [])
    out_shapes = [jax.ShapeDtypeStruct(b.shape, b.dtype) for b in big]
    if small is not None:
        out_shapes.append(jax.ShapeDtypeStruct((N_CHIPS, *small.shape), small.dtype))
    return _Hook(operands, out_shapes, {b: b for b in range(nb)}, n_sems, start, finish, middle, late)


def _both_ways_hook(operands, out_shapes, copies_of, n_sems):
    def start(ins, outs, send, recv):
        for cp in copies_of(ins, outs, send, recv):
            cp.start()

    def finish(ins, outs, send, recv):
        for cp in copies_of(ins, outs, send, recv):
            cp.wait()

    return _Hook(operands, out_shapes, {}, n_sems, start, finish)


def _swap_hook(bufs):
    def copies_of(ins, outs, send, recv):
        x, y, c, _ = _place()
        copies = []
        for b in range(len(bufs)):
            for j in range(N_CHIPS):
                k = b * N_CHIPS + j
                copies.append(_rcopy(ins[b].at[2 * j + 1 - c], outs[b].at[j], send.at[k], recv.at[k], (x, y, 1 - c)))
        return copies

    out_shapes = [jax.ShapeDtypeStruct((N_CHIPS, *b.shape[1:]), b.dtype) for b in bufs]
    return _both_ways_hook(list(bufs), out_shapes, copies_of, len(bufs) * N_CHIPS)


def _axis_order():
    x, y, c, _ = _place()
    return (x, y), c, (x ^ (1 - c), y ^ c), (x ^ c, y ^ (1 - c)), (1 - x, 1 - y)


def _send_first_hook(parts):
    def copies_of(ins, outs, send, recv):
        _, c, first, _, _ = _axis_order()
        copies = []
        for b in range(len(parts)):
            for k in range(2):
                sem = 2 * b + k
                copies.append(_rcopy(ins[b].at[k], outs[b].at[k], send.at[sem], recv.at[sem], (*first, c)))
        return copies

    out_shapes = [jax.ShapeDtypeStruct((2, *p.shape[1:]), p.dtype) for p in parts]
    return _both_ways_hook(list(parts), out_shapes, copies_of, len(parts) * 2)


def _send_second_hook(mids):
    def copies_of(ins, outs, send, recv):
        _, c, _, second, _ = _axis_order()
        return [_rcopy(ins[b], outs[b], send.at[b], recv.at[b], (*second, c)) for b in range(len(mids))]

    out_shapes = [jax.ShapeDtypeStruct(m.shape, m.dtype) for m in mids]
    return _both_ways_hook(list(mids), out_shapes, copies_of, len(mids))


def _share_hook(big, small=None, tiny=None):
    nb = len(big)
    n_sems = nb + 7 + N_DEV
    t0 = nb + 7

    def tiny_copies(ins, outs, send, recv):
        x, y, c, _ = _place()
        there = outs[-1].at[2 * _chip_no((x, y)) + c]
        copies = []
        for r in range(1, N_DEV):
            to = (x ^ (r >> 2 & 1), y ^ (r >> 1 & 1), c ^ (r & 1))
            copies.append(_rcopy(ins[-1], there, send.at[t0 + r], recv.at[t0 + r], to))
        return copies

    def tiny_local(ins, outs, send):
        x, y, c, _ = _place()
        return pltpu.make_async_copy(ins[-1], outs[-1].at[2 * _chip_no((x, y)) + c], send.at[t0])

    def first_copies(outs, send, recv):
        x, y, c, chips = _place()
        sibling = (x, y, 1 - c)
        copies = [_rcopy(outs[b].at[c], outs[b].at[c], send.at[b], recv.at[b], sibling) for b in range(nb)]
        if small is not None:
            own = outs[nb].at[2 * _chip_no((x, y)) + c]
            copies.append(_rcopy(own, own, send.at[nb], recv.at[nb], sibling))
            for j, chip in enumerate(chips):
                copies.append(_rcopy(own, own, send.at[nb + 1 + j], recv.at[nb + 1 + j], (*chip, c)))
        return copies

    def start(ins, outs, send, recv):
        for cp in first_copies(outs, send, recv):
            cp.start()
        if tiny is not None:
            for cp in tiny_copies(ins, outs, send, recv):
                cp.start()
            tiny_local(ins, outs, send).start()

    def finish(ins, outs, send, recv):
        x, y, c, chips = _place()
        me, sibling = (x, y, c), (x, y, 1 - c)
        if tiny is not None:
            for cp in tiny_copies(ins, outs, send, recv):
                cp.wait()
            tiny_local(ins, outs, send).wait()
        passed = []
        if small is not None:
            for j, chip in enumerate(chips):
                got = outs[nb].at[2 * _chip_no(chip) + c]
                _rcopy(got, got, send.at[nb + 1 + j], recv.at[nb + 1 + j], me).wait_recv()
                fwd = _rcopy(got, got, send.at[nb + 4 + j], recv.at[nb + 4 + j], sibling)
                fwd.start()
                passed.append(fwd)
        for b in range(nb):
            got = outs[b].at[1 - c]
            _rcopy(got, got, send.at[b], recv.at[b], me).wait_recv()
        if small is not None:
            got = outs[nb].at[2 * _chip_no((x, y)) + 1 - c]
            _rcopy(got, got, send.at[nb], recv.at[nb], me).wait_recv()
            for j, chip in enumerate(chips):
                got = outs[nb].at[2 * _chip_no(chip) + 1 - c]
                _rcopy(got, got, send.at[nb + 4 + j], recv.at[nb + 4 + j], me).wait_recv()
        for cp in first_copies(outs, send, recv) + passed:
            cp.wait_send()

    operands = list(big) + ([small] if small is not None else [])
    out_shapes = [jax.ShapeDtypeStruct(a.shape, a.dtype) for a in operands]
    aliases = {i: i for i in range(len(operands))}
    if tiny is not None:
        operands.append(tiny)
        out_shapes.append(jax.ShapeDtypeStruct((N_DEV, *tiny.shape), tiny.dtype))
    return _Hook(operands, out_shapes, aliases, n_sems, start, finish)


SUM_BLOCK_BYTES = 2 * 1024 * 1024


def _row_tile(rows, cols):
    best = SUBLANES
    for t in range(SUBLANES, rows + 1, SUBLANES):
        if rows % t == 0 and t * cols * 4 <= SUM_BLOCK_BYTES:
            best = t
    return best


def _halves(buf):
    return buf.reshape(N_CHIPS, 2, *buf.shape[1:])


AT_ZERO, AT_ONE, AT_FIRST, AT_SECOND, AT_DIAGONAL, AT_MINE, AT_CORE, AT_DEVICE = range(8)


def _index_table():
    x, y, c = lax.axis_index("x"), lax.axis_index("y"), lax.axis_index("c")
    first, second = (x ^ (1 - c), y ^ c), (x ^ c, y ^ (1 - c))
    mine = _chip_no((x, y))
    return _scalars(0, 1, _chip_no(first), _chip_no(second), N_CHIPS - 1 - mine, mine, c, 2 * mine + c)


def _sum_jobs(jobs, name, table):
    main, riders = jobs[0], jobs[1:]
    rows, cols = main[0][0][0][0].shape[-2:]
    tr = _row_tile(rows, cols)
    n_main = len(main[0][0])
    in_specs, operands, out_specs, out_shape = [], [], [], []

    def walked(leads):
        base = leads[0]
        strides = [b - a for a, b in zip(leads[0], leads[1])] if len(leads) > 1 else [0] * len(base)
        assert all(lead == tuple(a + g * s for a, s in zip(base, strides)) for g, lead in enumerate(leads))
        return lambda j, r, s_ref: (*[s_ref[a + j * s] for a, s in zip(base, strides)], r, 0)

    for t, (arr, lead) in enumerate(main[0][0]):
        operands.append(arr)
        in_specs.append(pl.BlockSpec((None,) * len(lead) + (tr, cols), walked([group[t][1] for group in main[0]])))
    out_specs.append(pl.BlockSpec((None,) * len(main[1]) + (tr, cols), walked(main[2])))
    out_shape.append(jax.ShapeDtypeStruct((*main[1], rows, cols), main[3]))

    whole = lambda lead: lambda j, r, s_ref: (*[s_ref[at] for at in lead], 0, 0)
    for groups, out_dims, out_leads, dtype in riders:
        r_k, c_k = groups[0][0][0].shape[-2:]
        for group in groups:
            for arr, lead in group:
                operands.append(arr)
                in_specs.append(pl.BlockSpec((None,) * len(lead) + (r_k, c_k), whole(lead)))
        if len(groups) == 1:
            out_specs.append(pl.BlockSpec((None,) * len(out_dims) + (r_k, c_k), whole(out_leads[0])))
        else:
            assert out_dims == (len(groups),) and list(out_leads) == [(AT_ZERO + g,) for g in range(len(groups))]
            out_specs.append(pl.BlockSpec((len(groups), r_k, c_k), lambda j, r, s_ref: (0, 0, 0)))
        out_shape.append(jax.ShapeDtypeStruct((*out_dims, r_k, c_k), dtype))

    def total(term_refs):
        acc = term_refs[0][...].astype(F32)
        for t_ref in term_refs[1:]:
            acc = acc + t_ref[...].astype(F32)
        return acc

    def body(s_ref, *refs):
        ins, outs = refs[: len(operands)], refs[len(operands) :]
        outs[0][...] = total(ins[:n_main]).astype(main[3])

        def ride():
            at = n_main
            for (groups, _, _, dtype), o_ref in zip(riders, outs[1:]):
                for g, group in enumerate(groups):
                    acc = total(ins[at : at + len(group)]).astype(dtype)
                    at += len(group)
                    if len(groups) == 1:
                        o_ref[...] = acc
                    else:
                        o_ref[g] = acc

        if riders:
            pl.when(jnp.logical_and(pl.program_id(0) == 0, pl.program_id(1) == 0))(ride)

    return _pcall(
        body,
        name=name,
        grid_spec=pltpu.PrefetchScalarGridSpec(
            num_scalar_prefetch=1, grid=(len(main[0]), rows // tr), in_specs=in_specs, out_specs=out_specs
        ),
        out_shape=out_shape,
        compiler_params=_cparams(("arbitrary", "arbitrary")),
    )(table, *operands)


def _own_half_job(buf, got, wire):
    groups = [[(_halves(buf), (j, AT_CORE)), (got, (j,))] for j in (AT_FIRST, AT_DIAGONAL)]
    return groups, (2,), [(AT_ZERO,), (AT_ONE,)], wire


def _for_neighbour_job(buf, got, got1, wire):
    return [[(_halves(buf), (AT_SECOND, AT_CORE)), (got, (AT_SECOND,)), (got1, (AT_ONE,))]], (), [()], wire


def _received_job(buf, got, got1, got2, slot, n_slots):
    terms = [(_halves(buf), (AT_MINE, AT_CORE)), (got, (AT_MINE,)), (got1, (AT_ZERO,)), (got2, ())]
    return [terms], (n_slots,), [(slot,)], F32


def _layer_a_fwd(x, nw, win, ln_w, ln_b, wc, bs_t, wout, tm, hook):
    t_rows, d = x.shape
    n_sh, _, s_cols = win.shape
    aw = wout.shape[0]
    gd = aw // A_GROUPS
    tn = 512
    assert s_cols % tn == 0 and aw % tn == 0 and tm % CHUNK == 0

    def body(x_ref, nw_ref, win_ref, lnw_ref, lnb_ref, wc_ref, bst_ref, wout_ref, z_ref, x1_ref, h_ref, u_s, v_s, y_s):
        x = x_ref[...]
        h, _, _ = _rms_fwd(x, nw_ref[...])
        h = h.astype(BF16)
        h_ref[...] = h
        for j in range(3 * aw // tn):
            k, off = divmod(j * tn, s_cols)
            cols = slice((j * tn) % aw, (j * tn) % aw + tn)
            zj = _dot(h, win_ref[k, :, off : off + tn])
            z_ref[:, j * tn : (j + 1) * tn] = zj
            if j * tn < aw:
                u_s[:, cols] = _gelu(zj)
            elif j * tn < 2 * aw:
                v_s[:, cols] = _gelu(zj)
            else:
                u_s[:, cols] = u_s[:, cols] * (zj * _sigmoid(zj))
        v = v_s[...]
        mu = jnp.mean(v, axis=-1, keepdims=True)
        vc = v - mu
        rstd = lax.rsqrt(jnp.mean(vc * vc, axis=-1, keepdims=True) + LN_EPS)
        v_s[...] = (vc * rstd) * lnw_ref[...] + lnb_ref[...]
        for ck in range(tm // CHUNK):
            rows = slice(ck * CHUNK, (ck + 1) * CHUNK)
            for g in range(A_GROUPS):
                cols = slice(g * gd, (g + 1) * gd)
                s = _dot(wc_ref[g], v_s[rows, cols].astype(BF16)) + bst_ref[:, g : g + 1]
                y_s[rows, cols] = (u_s[rows, cols] * s).astype(BF16)
        x1_ref[...] = x + _dot(y_s[...], wout_ref[...])

    row = lambda i: (i, 0)
    return _pcall(
        body,
        hook,
        name="layer_a_fwd",
        grid=(t_rows // tm,),
        in_specs=[
            pl.BlockSpec((tm, d), row),
            _full(nw.shape),
            _full(win.shape),
            _full(ln_w.shape),
            _full(ln_b.shape),
            _full(wc.shape),
            _full(bs_t.shape),
            _full(wout.shape),
        ],
        out_specs=[pl.BlockSpec((tm, 3 * aw), row), pl.BlockSpec((tm, d), row), pl.BlockSpec((tm, d), row)],
        out_shape=[
            jax.ShapeDtypeStruct((t_rows, 3 * aw), F32),
            jax.ShapeDtypeStruct((t_rows, d), F32),
            jax.ShapeDtypeStruct((t_rows, d), BF16),
        ],
        scratch_shapes=[pltpu.VMEM((tm, aw), F32), pltpu.VMEM((tm, aw), F32), pltpu.VMEM((tm, aw), BF16)],
        compiler_params=_cparams(("arbitrary",)),
    )(x, nw, win, ln_w, ln_b, wc, bs_t, wout)


def _layer_a_bwd(dout, z, ln_w, ln_b, wc, wct, bs_t, wout, tiles, earlier, hook):
    t_rows, d = dout.shape
    aw = wout.shape[0]
    gd = aw // A_GROUPS
    tm = TM_A_BWD
    lo, hi = tiles
    n_earlier = 0 if earlier is None else len(earlier)

    def body(dout_ref, z_ref, lnw_ref, lnb_ref, wc_ref, wct_ref, bst_ref, wout_ref, *rest):
        dz_ref, y_ref, dob_ref, gws_ref, gbs_ref, glnw_ref, glnb_ref, u_s, vh_s, ds_s, dvn_s = rest[n_earlier:]

        @pl.when(pl.program_id(0) == 0)
        def _():
            gws_ref[...] = jnp.zeros_like(gws_ref)
            gbs_ref[...] = jnp.zeros_like(gbs_ref)
            glnw_ref[...] = jnp.zeros_like(glnw_ref)
            glnb_ref[...] = jnp.zeros_like(glnb_ref)

        dob = dout_ref[...].astype(BF16)
        dob_ref[...] = dob
        dy = _dot_nt(dob, wout_ref[...])

        zv = z_ref[:, aw : 2 * aw]
        vg, dvg_dz = _gelu_and_grad(zv)
        mu = jnp.mean(vg, axis=-1, keepdims=True)
        vc = vg - mu
        rstd = lax.rsqrt(jnp.mean(vc * vc, axis=-1, keepdims=True) + LN_EPS)
        vh = vc * rstd
        vh_s[...] = vh
        vn = (vh * lnw_ref[...] + lnb_ref[...]).astype(BF16)

        zu = z_ref[:, 0:aw]
        zg = z_ref[:, 2 * aw : 3 * aw]
        u, du_dz = _gelu_and_grad(zu)
        sg, dsg = _silu_and_grad(zg)
        u_s[...] = u * sg
        tril = lax.broadcasted_iota(jnp.int32, (CHUNK, CHUNK), 0) >= lax.broadcasted_iota(jnp.int32, (CHUNK, CHUNK), 1)
        for ck in range(tm // CHUNK):
            rows = slice(ck * CHUNK, (ck + 1) * CHUNK)
            for g in range(A_GROUPS):
                cols = slice(g * gd, (g + 1) * gd)
                vn_g = vn[rows, cols]
                s = _dot(wc_ref[g], vn_g) + bst_ref[:, g : g + 1]
                usg = u_s[rows, cols]
                dy_g = dy[rows, cols]
                y_ref[rows, cols] = (usg * s).astype(BF16)
                ds = dy_g * usg
                ds_s[rows, cols] = dy_g * s
                gbs_ref[:, g : g + 1] += jnp.sum(ds, axis=-1, keepdims=True)
                dsb = ds.astype(BF16)
                gws_ref[g] += jnp.where(tril, _dot_nt(dsb, vn_g), 0.0)
                dvn_s[rows, cols] = _dot(wct_ref[g], dsb)
        dusg = ds_s[...]
        dz_ref[:, 0:aw] = (dusg * sg * du_dz).astype(BF16)
        dz_ref[:, 2 * aw : 3 * aw] = (dusg * u * dsg).astype(BF16)

        dvn = dvn_s[...]
        vh = vh_s[...]
        glnw_ref[...] += jnp.sum(dvn * vh, axis=0, keepdims=True)
        glnb_ref[...] += jnp.sum(dvn, axis=0, keepdims=True)
        dvh = dvn * lnw_ref[...]
        dvg = rstd * (dvh - jnp.mean(dvh, axis=-1, keepdims=True) - vh * jnp.mean(dvh * vh, axis=-1, keepdims=True))
        dz_ref[:, aw : 2 * aw] = (dvg * dvg_dz).astype(BF16)

    row = lambda i: (i + lo, 0)
    call = _pcall(
        body,
        hook,
        name=f"layer_a_bwd_{lo}",
        grid=(hi - lo,),
        in_specs=[
            pl.BlockSpec((tm, d), row),
            pl.BlockSpec((tm, 3 * aw), row),
            _full(ln_w.shape),
            _full(ln_b.shape),
            _full(wc.shape),
            _full(wct.shape),
            _full(bs_t.shape),
            _full(wout.shape),
        ]
        + [ANY] * n_earlier,
        out_specs=[
            pl.BlockSpec((tm, 3 * aw), row),
            pl.BlockSpec((tm, aw), row),
            pl.BlockSpec((tm, d), row),
            _full((A_GROUPS, CHUNK, CHUNK)),
            _full((CHUNK, A_GROUPS)),
            _full((1, aw)),
            _full((1, aw)),
        ],
        out_shape=[
            jax.ShapeDtypeStruct((t_rows, 3 * aw), BF16),
            jax.ShapeDtypeStruct((t_rows, aw), BF16),
            jax.ShapeDtypeStruct((t_rows, d), BF16),
            jax.ShapeDtypeStruct((A_GROUPS, CHUNK, CHUNK), F32),
            jax.ShapeDtypeStruct((CHUNK, A_GROUPS), F32),
            jax.ShapeDtypeStruct((1, aw), F32),
            jax.ShapeDtypeStruct((1, aw), F32),
        ],
        scratch_shapes=[pltpu.VMEM((tm, aw), F32)] * 4,
        input_output_aliases={8 + i: i for i in range(n_earlier)},
        compiler_params=_cparams(("arbitrary",)),
    )
    return call(dout, z, ln_w, ln_b, wc, wct, bs_t, wout, *(earlier or ()))


def _layer_a_bwd_dx(dout, x, dz, nw, win, tm, tiles, earlier, hook):
    t_rows, d = x.shape
    n_sh, _, s_cols = win.shape
    lo, hi = tiles
    n_earlier = 0 if earlier is None else 1

    def body(dout_ref, x_ref, dz_ref, nw_ref, win_ref, *rest):
        gx_ref, gnw_ref = rest[n_earlier:]

        @pl.when(pl.program_id(0) == 0)
        def _():
            gnw_ref[...] = jnp.zeros_like(gnw_ref)

        dh = jnp.zeros((tm, d), F32)
        for k in range(n_sh):
            dh = dh + _dot_nt(dz_ref[:, k * s_cols : (k + 1) * s_cols], win_ref[k])
        nw = nw_ref[...]
        _, xh, r = _rms_fwd(x_ref[...], nw)
        dx, gnw = _rms_bwd(dh, xh, r, nw)
        gnw_ref[0:1, :] += gnw
        gx_ref[...] = dout_ref[...] + dx

    row = lambda i: (i + lo, 0)
    return _pcall(
        body,
        hook,
        name=f"layer_a_bwd_dx_{lo}",
        grid=(hi - lo,),
        in_specs=[
            pl.BlockSpec((tm, d), row),
            pl.BlockSpec((tm, d), row),
            pl.BlockSpec((tm, n_sh * s_cols), row),
            _full(nw.shape),
            _full(win.shape),
        ]
        + [ANY] * n_earlier,
        out_specs=[pl.BlockSpec((tm, d), row), _full((SUBLANES, d))],
        out_shape=[jax.ShapeDtypeStruct((t_rows, d), F32), jax.ShapeDtypeStruct((SUBLANES, d), F32)],
        input_output_aliases={5: 0} if n_earlier else {},
        compiler_params=_cparams(("arbitrary",)),
    )(dout, x, dz, nw, win, *([earlier] if n_earlier else []))


def _decay(r, sp_h):
    log_a = (-RG_C) * r * sp_h
    a = jnp.exp(log_a)
    mult = jnp.sqrt(jnp.tanh(-log_a) * (a * a + 1.0))
    return a, mult


ROW_CONV_B, ROW_GATE_A_B, ROW_GATE_X_B, ROW_LAMBDA = range(CONV_WIDTH, CONV_WIDTH + 4)


def _gates(xc_h, gab_ref, vec_ref, sp_h, h, hd):
    pre = _dot(xc_h.astype(BF16), gab_ref[h])
    cols = slice(h * hd, (h + 1) * hd)
    r = _sigmoid(pre[:, :hd] + vec_ref[ROW_GATE_A_B : ROW_GATE_A_B + 1, cols])
    ig = _sigmoid(pre[:, hd:] + vec_ref[ROW_GATE_X_B : ROW_GATE_X_B + 1, cols])
    a, mult = _decay(r, sp_h)
    return r, ig, a, mult


def _conv(xb, halo, vec_ref):
    xc = vec_ref[ROW_CONV_B : ROW_CONV_B + 1, :] + vec_ref[CONV_WIDTH - 1 : CONV_WIDTH, :] * xb
    for k in range(CONV_WIDTH - 1):
        xc = xc + vec_ref[k : k + 1, :] * _shift_down(xb, halo, CONV_WIDTH - 1 - k)
    return xc


def _layer_b_fwd(x1, nw, bin_w, vec, gab, bout, nf, tgt, tm):
    t_rows, d = x1.shape
    bw = bout.shape[0]
    hd = bw // B_HEADS
    nt = t_rows // tm

    def body(
        x1_ref, nw_ref, bin_ref, vec_ref, gab_ref, bout_ref, nf_ref, tgt_ref,
        z_ref, h_ref, h1_ref, dx2_ref, loss_ref, gnf_ref,
        tail_s, carry_s, a_s, b_s, hs_s, acc_s,
    ):
        @pl.when(pl.program_id(0) == 0)
        def _():
            tail_s[...] = jnp.zeros_like(tail_s)
            carry_s[...] = jnp.zeros_like(carry_s)
            acc_s[...] = jnp.zeros_like(acc_s)
            gnf_ref[...] = jnp.zeros_like(gnf_ref)

        x1 = x1_ref[...]
        h1, _, _ = _rms_fwd(x1, nw_ref[...])
        h1 = h1.astype(BF16)
        h1_ref[...] = h1
        z = jnp.concatenate([_dot(h1, bin_ref[k]) for k in range(N_CHIPS)], axis=1)
        z_ref[...] = z
        xb = z[:, :bw]
        xc = _conv(xb, tail_s[...], vec_ref)
        tail = xb[tm - SUBLANES :, :]
        tail_s[...] = tail
        sp = _softplus_neg(vec_ref[ROW_LAMBDA : ROW_LAMBDA + 1, :])
        for h in range(B_HEADS):
            cols = slice(h * hd, (h + 1) * hd)
            xc_h = xc[:, cols]
            _, ig, a, mult = _gates(xc_h, gab_ref, vec_ref, sp[:, cols], h, hd)
            a_s[:, cols] = a
            b_s[:, cols] = mult * (ig * xc_h)
        carry = _scan_blocks(a_s, b_s, hs_s, carry_s[...], tm, reverse=False)
        carry_s[...] = carry
        hs = hs_s[...]
        h_ref[...] = hs
        g = z[:, bw:]
        y = (hs * (g * _sigmoid(g))).astype(BF16)
        x2 = x1 + _dot(y, bout_ref[...])

        nf = nf_ref[...]
        o, xh, r = _rms_fwd(x2, nf)
        diff = o - tgt_ref[...]
        acc_s[...] += jnp.sum(diff * diff, axis=0, keepdims=True)
        do = diff * (1.0 / d)
        dx2, gnf = _rms_bwd(do, xh, r, nf)
        gnf_ref[...] += gnf
        dx2_ref[...] = dx2

        @pl.when(pl.program_id(0) == nt - 1)
        def _():
            total = jnp.sum(acc_s[...], axis=-1, keepdims=True) * (0.5 / d)
            loss_ref[...] = jnp.broadcast_to(total, loss_ref.shape)

    row = lambda i: (i, 0)
    return _pcall(
        body,
        name="layer_b_fwd",
        grid=(nt,),
        in_specs=[
            pl.BlockSpec((tm, d), row),
            _full(nw.shape),
            _full(bin_w.shape),
            _full(vec.shape),
            _full(gab.shape),
            _full(bout.shape),
            _full(nf.shape),
            pl.BlockSpec((tm, d), row),
        ],
        out_specs=[
            pl.BlockSpec((tm, 2 * bw), row),
            pl.BlockSpec((tm, bw), row),
            pl.BlockSpec((tm, d), row),
            pl.BlockSpec((tm, d), row),
            _full((1, LANES)),
            _full((1, d)),
        ],
        out_shape=[
            jax.ShapeDtypeStruct((t_rows, 2 * bw), F32),
            jax.ShapeDtypeStruct((t_rows, bw), F32),
            jax.ShapeDtypeStruct((t_rows, d), BF16),
            jax.ShapeDtypeStruct((t_rows, d), F32),
            jax.ShapeDtypeStruct((1, LANES), F32),
            jax.ShapeDtypeStruct((1, d), F32),
        ],
        scratch_shapes=[
            pltpu.VMEM((SUBLANES, bw), F32),
            pltpu.VMEM((SUBLANES, bw), F32),
            pltpu.VMEM((tm, bw), F32),
            pltpu.VMEM((tm, bw), F32),
            pltpu.VMEM((tm, bw), F32),
            pltpu.VMEM((1, d), F32),
        ],
        compiler_params=_cparams(("arbitrary",)),
    )(x1, nw, bin_w, vec, gab, bout, nf, tgt)


def _layer_b_bwd(dout, x1, z, hseq, nw, bin_w, vec, gab, gabt, bout, tm):
    t_rows, d = x1.shape
    bw = bout.shape[0]
    hd = bw // B_HEADS
    nt = t_rows // tm

    def body(
        dout_ref, x1_ref, z_ref, h_ref, xbt_ref, ht_ref, nw_ref, bin_ref, vec_ref, gab_ref, gabt_ref, bout_ref,
        dx1_ref, dz_ref, y_ref, dob_ref, ggab_ref, ggb_ref, gcw_ref, gcb_ref, glam_ref, gnw_ref,
        gcarry_s, afirst_s, head_s, aup_s, dh_s, gt_s, dxc_s, xc_s, r_s, ig_s,
    ):
        step = pl.program_id(0)
        tile = nt - 1 - step

        @pl.when(step == 0)
        def _():
            for ref in (ggab_ref, ggb_ref, gcw_ref, gcb_ref, glam_ref, gnw_ref, gcarry_s, afirst_s, head_s):
                ref[...] = jnp.zeros_like(ref)

        first_tile = tile == 0
        xb_halo = jnp.where(first_tile, 0.0, xbt_ref[...])
        h_halo = jnp.where(first_tile, 0.0, ht_ref[...])

        dout = dout_ref[...]
        dob = dout.astype(BF16)
        dob_ref[...] = dob
        dy = _dot_nt(dob, bout_ref[...])
        hs = h_ref[...]
        g = z_ref[:, bw:]
        sg, dsg = _silu_and_grad(g)
        y_ref[...] = (hs * sg).astype(BF16)
        dz_ref[:, bw:] = (dy * hs * dsg).astype(BF16)
        dh_s[...] = dy * sg

        xb = z_ref[:, :bw]
        xc = _conv(xb, xb_halo, vec_ref)
        xc_s[...] = xc
        lam = vec_ref[ROW_LAMBDA : ROW_LAMBDA + 1, :]
        sp = _softplus_neg(lam)
        for h in range(B_HEADS):
            cols = slice(h * hd, (h + 1) * hd)
            r, ig, a, _ = _gates(xc[:, cols], gab_ref, vec_ref, sp[:, cols], h, hd)
            r_s[:, cols] = r
            ig_s[:, cols] = ig
            aup_s[:, cols] = _shift_up(a, afirst_s[:, cols], 1)
            afirst_s[:, cols] = jnp.broadcast_to(a[0:1, :], (SUBLANES, hd))
        carry = _scan_blocks(aup_s, dh_s, gt_s, gcarry_s[...], tm, reverse=True)
        gcarry_s[...] = carry

        h_prev = _shift_down(hs, h_halo, 1)
        for h in range(B_HEADS):
            cols = slice(h * hd, (h + 1) * hd)
            xc_h = xc_s[:, cols]
            sp_h = sp[:, cols]
            r, ig = r_s[:, cols], ig_s[:, cols]
            a, mult = _decay(r, sp_h)
            gt = gt_s[:, cols]
            da = gt * h_prev[:, cols]
            dmult = gt * (ig * xc_h)
            dig = gt * (mult * xc_h)
            dxc_direct = gt * (mult * ig)
            dla = da * a - dmult * (a * a) / mult
            glam_ref[:, cols] += jnp.sum(dla * r, axis=0, keepdims=True)
            dr = dla * ((-RG_C) * sp_h)
            dpre = jnp.concatenate([dr * r * (1.0 - r), dig * ig * (1.0 - ig)], axis=1)
            ggb_ref[:, cols] += jnp.sum(dpre[:, :hd], axis=0, keepdims=True)
            ggb_ref[:, bw + h * hd : bw + (h + 1) * hd] += jnp.sum(dpre[:, hd:], axis=0, keepdims=True)
            dpb = dpre.astype(BF16)
            ggab_ref[h] += _dot_tn(xc_h.astype(BF16), dpb)
            dxc_s[:, cols] = dxc_direct + _dot(dpb, gabt_ref[h])
        glam_ref[...] = jnp.where(step == nt - 1, glam_ref[...] * (RG_C * _sigmoid(-lam)), glam_ref[...])

        dxc = dxc_s[...]
        gcb_ref[...] += jnp.sum(dxc, axis=0, keepdims=True)
        dxb = vec_ref[CONV_WIDTH - 1 : CONV_WIDTH, :] * dxc
        gcw_ref[CONV_WIDTH - 1 : CONV_WIDTH, :] += jnp.sum(dxc * xb, axis=0, keepdims=True)
        head = head_s[...]
        for k in range(CONV_WIDTH - 1):
            lag = CONV_WIDTH - 1 - k
            dxb = dxb + vec_ref[k : k + 1, :] * _shift_up(dxc, head, lag)
            gcw_ref[k : k + 1, :] += jnp.sum(dxc * _shift_down(xb, xb_halo, lag), axis=0, keepdims=True)
        head_s[...] = dxc[:SUBLANES, :]
        dz_ref[:, :bw] = dxb.astype(BF16)

        s_cols = 2 * bw // N_CHIPS
        dh1 = jnp.zeros((tm, d), F32)
        for k in range(N_CHIPS):
            dh1 = dh1 + _dot_nt(dz_ref[:, k * s_cols : (k + 1) * s_cols], bin_ref[k])
        x1 = x1_ref[...]
        nw = nw_ref[...]
        _, xh, r1 = _rms_fwd(x1, nw)
        dx, gnw = _rms_bwd(dh1, xh, r1, nw)
        gnw_ref[...] += gnw
        dx1_ref[...] = dout + dx

    rev = lambda i: (nt - 1 - i, 0)
    prev = lambda i: (jnp.maximum((nt - 1 - i) * (tm // SUBLANES) - 1, 0), 0)
    return _pcall(
        body,
        name="layer_b_bwd",
        grid=(nt,),
        in_specs=[
            pl.BlockSpec((tm, d), rev),
            pl.BlockSpec((tm, d), rev),
            pl.BlockSpec((tm, 2 * bw), rev),
            pl.BlockSpec((tm, bw), rev),
            pl.BlockSpec((SUBLANES, bw), prev),
            pl.BlockSpec((SUBLANES, bw), prev),
            _full(nw.shape),
            _full(bin_w.shape),
            _full(vec.shape),
            _full(gab.shape),
            _full(gabt.shape),
            _full(bout.shape),
        ],
        out_specs=[
            pl.BlockSpec((tm, d), rev),
            pl.BlockSpec((tm, 2 * bw), rev),
            pl.BlockSpec((tm, bw), rev),
            pl.BlockSpec((tm, d), rev),
            _full((B_HEADS, hd, 2 * hd)),
            _full((1, 2 * bw)),
            _full((SUBLANES, bw)),
            _full((1, bw)),
            _full((1, bw)),
            _full((1, d)),
        ],
        out_shape=[
            jax.ShapeDtypeStruct((t_rows, d), F32),
            jax.ShapeDtypeStruct((t_rows, 2 * bw), BF16),
            jax.ShapeDtypeStruct((t_rows, bw), BF16),
            jax.ShapeDtypeStruct((t_rows, d), BF16),
            jax.ShapeDtypeStruct((B_HEADS, hd, 2 * hd), F32),
            jax.ShapeDtypeStruct((1, 2 * bw), F32),
            jax.ShapeDtypeStruct((SUBLANES, bw), F32),
            jax.ShapeDtypeStruct((1, bw), F32),
            jax.ShapeDtypeStruct((1, bw), F32),
            jax.ShapeDtypeStruct((1, d), F32),
        ],
        scratch_shapes=[pltpu.VMEM((SUBLANES, bw), F32)] * 3 + [pltpu.VMEM((tm, bw), F32)] * 7,
        compiler_params=_cparams(("arbitrary",)),
    )(dout, x1, z, hseq, z, hseq, nw, bin_w, vec, gab, gabt, bout)


def _wgrad(a, b, m_blocks, n_blocks, hook=None, wire_copy=False):
    k, m = a.shape
    n = b.shape[1]
    bm, bn = m // m_blocks, n // n_blocks

    def body(a_ref, b_ref, o_ref, *wire_ref):
        prod = _dot_tn(a_ref[...], b_ref[...])
        o_ref[...] = prod
        if wire_copy:
            wire_ref[0][...] = prod.astype(BF16)

    out_spec = pl.BlockSpec((None, None, bm, bn), lambda j, i: (j, i, 0, 0))
    shape = (n_blocks, m_blocks, bm, bn)
    out = _pcall(
        body,
        hook,
        name=f"wgrad_{m}x{n}",
        grid=(n_blocks, m_blocks),
        in_specs=[pl.BlockSpec((k, bm), lambda j, i: (0, i)), pl.BlockSpec((k, bn), lambda j, i: (0, j))],
        out_specs=[out_spec] * (1 + wire_copy),
        out_shape=[jax.ShapeDtypeStruct(shape, F32)] + [jax.ShapeDtypeStruct(shape, BF16)] * wire_copy,
        compiler_params=_cparams(("arbitrary", "arbitrary")),
    )(a, b)
    outs, rode = (out, None) if hook is None else out
    outs = outs if wire_copy else outs[0]
    return outs if hook is None else (outs, rode)


def _adamw_math(w, g, m, v):
    m = ADAM_B1 * m + (1.0 - ADAM_B1) * g
    v = ADAM_B2 * v + (1.0 - ADAM_B2) * (g * g)
    m_hat = m / (1.0 - ADAM_B1**ADAM_STEP)
    v_hat = v / (1.0 - ADAM_B2**ADAM_STEP)
    delta = -ADAM_LR * (m_hat / (jnp.sqrt(v_hat) + ADAM_EPS) + ADAM_WD * w)
    return delta, m, v


ADAMW_ROW_TILES = 8


def _adamw_rows(ws, gs, ms, vs):
    n = len(ws)

    def body(*refs):
        ins, outs = refs[: 4 * n], refs[4 * n :]
        for i in range(n):
            w_ref, g_ref, m_ref, v_ref = ins[i::n]
            d_ref, mo_ref, vo_ref, go_ref = outs[i::n]
            g = g_ref[...]
            d_ref[...], mo_ref[...], vo_ref[...] = _adamw_math(w_ref[...], g, m_ref[...], v_ref[...])
            go_ref[...] = g

    specs = [pl.BlockSpec((w.shape[0] // ADAMW_ROW_TILES, w.shape[1]), lambda i: (i, 0)) for w in ws]
    outs = _pcall(
        body,
        name="adamw_rows",
        grid=(ADAMW_ROW_TILES,),
        in_specs=specs * 4,
        out_specs=specs * 4,
        out_shape=[jax.ShapeDtypeStruct(w.shape, F32) for w in ws] * 4,
        compiler_params=_cparams(("arbitrary",)),
    )(*ws, *gs, *ms, *vs)
    return [outs[k * n : (k + 1) * n] for k in range(4)]


def _adamw_many(ws, gs, ms, vs, pack):
    n = len(ws)
    packed = [i for i in range(n) if isinstance(gs[i], (int, tuple))]
    g_arrays = [g[0] if isinstance(g, tuple) else g for g in gs if not isinstance(g, int)]

    def body(*refs):
        w_refs, m_refs, v_refs = (refs[i * n : (i + 1) * n] for i in range(3))
        g_refs = list(refs[3 * n : 3 * n + len(g_arrays)])
        pack_ref = refs[3 * n + len(g_arrays)]
        outs = refs[3 * n + len(g_arrays) + 1 :]
        d_refs, mo_refs, vo_refs, go_refs = outs[:n], outs[n : 2 * n], outs[2 * n : 3 * n], list(outs[3 * n :])
        for i in range(n):
            if isinstance(gs[i], int):
                g = pack_ref[gs[i] : gs[i] + ws[i].shape[0], :]
                go_refs.pop(0)[...] = g
            elif isinstance(gs[i], tuple):
                parts_ref, row, go_ref = g_refs.pop(0), gs[i][1], go_refs.pop(0)
                total = parts_ref[0, 0:1, :]
                for k in range(1, N_DEV):
                    total = total + parts_ref[k, 0:1, :]
                go_ref[0:1, :] = total
                lane_rows = [pack_ref[row + r : row + r + 1, :] for r in range(ws[i].shape[1] // LANES)]
                go_ref[1:2, :] = jnp.concatenate(lane_rows, axis=1)
                g = go_ref[...]
            else:
                g = g_refs.pop(0)[...]
            d_refs[i][...], mo_refs[i][...], vo_refs[i][...] = _adamw_math(w_refs[i][...], g, m_refs[i][...], v_refs[i][...])

    vmem = pl.BlockSpec(memory_space=pltpu.VMEM)
    like = [jax.ShapeDtypeStruct(w.shape, F32) for w in ws]
    outs = _pcall(
        body,
        name="adamw_small",
        in_specs=[vmem] * (3 * n + len(g_arrays) + 1),
        out_specs=[vmem] * (3 * n + len(packed)),
        out_shape=like * 3 + [like[i] for i in packed],
        compiler_params=_cparams(),
    )(*ws, *ms, *vs, *g_arrays, pack)
    return outs[:n], outs[n : 2 * n], outs[2 * n : 3 * n], outs[3 * n :]


def _pack_rows(parts, lanes=LANES):
    flat = jnp.concatenate([p.reshape(-1) for p in parts])
    per = N_DEV * SUBLANES * lanes
    total = -(-flat.shape[0] // per) * per
    flat = jnp.pad(flat, (0, total - flat.shape[0]))
    return flat.reshape(N_DEV, total // (N_DEV * lanes), lanes)


def _size(shape):
    n = 1
    for dim in shape:
        n *= dim
    return n


def _unpack(flat, shapes):
    out, at = [], 0
    for s in shapes:
        out.append(flat[at : at + _size(s)].reshape(s))
        at += _size(s)
    return out


def kernel(x, norm_w, a_w_in, a_ln_w, a_ln_b, a_w_s, a_b_s, a_w_out, b_w_in, b_conv_w, b_conv_b, b_gate_a_w, b_gate_a_b, b_gate_x_w, b_gate_x_b, b_lambda, b_w_out, norm_f_w, loss_target, m_norm_w, m_a_w_in, m_a_ln_w, m_a_ln_b, m_a_w_s, m_a_b_s, m_a_w_out, m_b_w_in, m_b_conv_w, m_b_conv_b, m_b_gate_a_w, m_b_gate_a_b, m_b_gate_x_w, m_b_gate_x_b, m_b_lambda, m_b_w_out, m_norm_f_w, v_norm_w, v_a_w_in, v_a_ln_w, v_a_ln_b, v_a_w_s, v_a_b_s, v_a_w_out, v_b_w_in, v_b_conv_w, v_b_conv_b, v_b_gate_a_w, v_b_gate_a_b, v_b_gate_x_w, v_b_gate_x_b, v_b_lambda, v_b_w_out, v_norm_f_w):
    t_rows, d = x.shape[1], x.shape[2]
    aw = a_ln_w.shape[1]
    bw = b_gate_a_w.shape[1] * b_gate_a_w.shape[2]
    hd = bw // B_HEADS
    mine = 2 * lax.axis_index("x") + lax.axis_index("y")
    core = lax.axis_index("c")
    weights = dict(norm_w=norm_w, a_w_in=a_w_in, a_ln_w=a_ln_w, a_ln_b=a_ln_b, a_w_s=a_w_s, a_b_s=a_b_s, a_w_out=a_w_out, b_w_in=b_w_in, b_conv_w=b_conv_w, b_conv_b=b_conv_b, b_gate_a_w=b_gate_a_w, b_gate_a_b=b_gate_a_b, b_gate_x_w=b_gate_x_w, b_gate_x_b=b_gate_x_b, b_lambda=b_lambda, b_w_out=b_w_out, norm_f_w=norm_f_w)
    m_in = dict(norm_w=m_norm_w, a_w_in=m_a_w_in, a_ln_w=m_a_ln_w, a_ln_b=m_a_ln_b, a_w_s=m_a_w_s, a_b_s=m_a_b_s, a_w_out=m_a_w_out, b_w_in=m_b_w_in, b_conv_w=m_b_conv_w, b_conv_b=m_b_conv_b, b_gate_a_w=m_b_gate_a_w, b_gate_a_b=m_b_gate_a_b, b_gate_x_w=m_b_gate_x_w, b_gate_x_b=m_b_gate_x_b, b_lambda=m_b_lambda, b_w_out=m_b_w_out, norm_f_w=m_norm_f_w)
    v_in = dict(norm_w=v_norm_w, a_w_in=v_a_w_in, a_ln_w=v_a_ln_w, a_ln_b=v_a_ln_b, a_w_s=v_a_w_s, a_b_s=v_a_b_s, a_w_out=v_a_w_out, b_w_in=v_b_w_in, b_conv_w=v_b_conv_w, b_conv_b=v_b_conv_b, b_gate_a_w=v_b_gate_a_w, b_gate_a_b=v_b_gate_a_b, b_gate_x_w=v_b_gate_x_w, b_gate_x_b=v_b_gate_x_b, b_lambda=v_b_lambda, b_w_out=v_b_w_out, norm_f_w=v_norm_f_w)

    table = _index_table()
    win_l, wout_l = _cast_to_segments([a_w_in[0], a_w_out[0]], table, 4)
    small_l = jnp.concatenate([b_conv_w[0], b_conv_b, b_gate_a_b, b_gate_x_b, b_lambda], axis=0)
    (bin_l, bout_l, wc, wct, gab, gabt), (win_g, wout_g, small_g) = _cast_to_segments(
        [b_w_in[0], b_w_out[0]], table, 8, _gather_hook([win_l, wout_l], small_l),
        (a_w_s[0], b_gate_a_w[0], b_gate_x_w[0]),
    )
    win = win_g.reshape(N_CHIPS, d, -1)
    wout = wout_g.reshape(aw, d)
    bs_t = a_b_s[0].T
    nw0, nw1, nf = norm_w[0:1], norm_w[1:2], norm_f_w.reshape(1, d)

    x0 = x[0]
    (z_a, x1, h0), (bin_g, bout_g) = _layer_a_fwd(
        x0, nw0, win, a_ln_w, a_ln_b, wc, bs_t, wout, TM_FWD, _gather_hook([bin_l, bout_l])
    )
    bin_w = bin_g.reshape(N_CHIPS, d, -1)
    bout = bout_g.reshape(bw, d)
    vec = jnp.transpose(small_g, (1, 0, 2)).reshape(SUBLANES, bw)
    z_b, hseq, h1, dx2, loss_l, g_nf = _layer_b_fwd(x1, nw1, bin_w, vec, gab, bout, nf, loss_target[0], TM_FWD)
    dx1, dz_b, y_b, dob_b, g_gab, g_gb, g_cw, g_cb, g_lam, g_nw1 = _layer_b_bwd(
        dx2, x1, z_b, hseq, nw1, bin_w, vec, gab, gabt, bout, TM_FWD
    )
    seg = lambda g: g.reshape(N_DEV, -1, g.shape[3])
    sums = lambda jobs, name: _sum_jobs(jobs, name, table)
    own_half, for_neighbour = _own_half_job, _for_neighbour_job
    received = lambda buf, got, got1, got2: _received_job(buf, got, got1, got2, AT_CORE, 2)

    g_o = seg(_wgrad(y_b, dob_b, 2, 1))
    g_i, (swap_o,) = _wgrad(h1, dz_b, 1, N_CHIPS, _swap_hook([g_o]))
    g_i = seg(g_i)
    (part_o,) = sums([own_half(g_o, swap_o, BF16)], "add_own_half_o")
    a_args = (z_a, a_ln_w, a_ln_b, wc, wct, bs_t, wout)
    half = t_rows // TM_A_BWD // 2
    first, (swap_i, got1_o) = _layer_a_bwd(
        dx1, *a_args, (0, half), None, _join_hooks(_swap_hook([g_i]), _send_first_hook([part_o]))
    )
    part_i, mid_o = sums(
        [own_half(g_i, swap_i, BF16), for_neighbour(g_o, swap_o, got1_o, BF16)], "add_own_half_i_for_neighbour_o"
    )
    second, (got1_i, got2_o) = _layer_a_bwd(
        dx1, *a_args, (half, 2 * half), first[:3], _join_hooks(_send_first_hook([part_i]), _send_second_hook([mid_o]))
    )
    dz_a, y_a, dob_a = second[:3]
    g_ws, g_bst, g_lnw, g_lnb = (p + q for p, q in zip(first[3:], second[3:]))
    mid_i, red_o = sums(
        [for_neighbour(g_i, swap_i, got1_i, BF16), received(g_o, swap_o, got1_o, got2_o)],
        "add_for_neighbour_i_received_o",
    )
    small_shapes = [
        (1, d), (1, aw), (1, aw), (A_GROUPS, CHUNK, CHUNK), (A_GROUPS, CHUNK), (B_HEADS, hd, hd), (B_HEADS, hd, hd),
        (d,), (CONV_WIDTH, bw), (1, bw), (1, bw), (1, bw), (1, bw), (1, 1),
    ]
    small = _pack_rows(
        [
            g_nw1, g_lnw, g_lnb, g_ws, g_bst.T, g_gab[:, :, :hd], g_gab[:, :, hd:],
            g_nf, g_cw[:CONV_WIDTH], g_cb, g_gb[:, :bw], g_gb[:, bw:], g_lam, loss_l[:, :1],
        ]
    )
    (g_w, g_w_wire), (got2_i, gr_bout, swap_s) = _wgrad(
        h0, dz_a, 1, N_CHIPS, _join_hooks(_send_second_hook([mid_i]), _share_hook([red_o]), _swap_hook([small])),
        wire_copy=True,
    )
    g_w, g_w_wire = seg(g_w), seg(g_w_wire)
    (red_i,) = sums([received(g_i, swap_i, got1_i, got2_i)], "add_received_i")
    g_u, (swap_w,) = _wgrad(y_a, dob_a, N_CHIPS, 1, _swap_hook([g_w_wire]))
    g_u = seg(g_u)

    part_w, part_s = sums([own_half(g_w, swap_w, BF16), own_half(small, swap_s, F32)], "add_own_half_w")
    (grad_x, g_nw0_mine), (got1_w, got1_s, swap_u, gr_bin) = _layer_a_bwd_dx(
        dx1, x0, dz_a, nw0, win, TM_A_DX, (0, t_rows // TM_A_DX), None,
        _join_hooks(_send_first_hook([part_w, part_s]), _swap_hook([g_u]), _share_hook([red_i])),
    )
    mid_w, mid_s, part_u = sums(
        [
            for_neighbour(g_w, swap_w, got1_w, BF16), for_neighbour(small, swap_s, got1_s, F32),
            own_half(g_u, swap_u, BF16),
        ],
        "add_for_neighbour_w_own_half_u",
    )
    got2_w, got2_s, got1_u = _run_hook(
        _join_hooks(_send_second_hook([mid_w, mid_s]), _send_first_hook([part_u])), "second_axis"
    )
    red_w, red_s, mid_u = sums(
        [
            received(g_w, swap_w, got1_w, got2_w),
            _received_job(small, swap_s, got1_s, got2_s, AT_DEVICE, N_DEV),
            for_neighbour(g_u, swap_u, got1_u, BF16),
        ],
        "add_received_w_for_neighbour_u",
    )
    got2_u, gr_win, small_r, g_nw0_all = _run_hook(
        _join_hooks(_send_second_hook([mid_u]), _share_hook([red_w], red_s, g_nw0_mine)), "second_axis_and_share"
    )
    (red_u,) = sums([received(g_u, swap_u, got1_u, got2_u)], "add_received_u")
    (gr_wout,) = _run_hook(_share_hook([red_u]), "share_reduced")
    (_, g_a_ln_w, g_a_ln_b, _, g_a_b_s, _, _, g_norm_f, gf_cw, gf_cb, gf_gab, gf_gxb, gf_lam, loss) = _unpack(
        small_r.reshape(-1), small_shapes
    )
    shard = lambda g: lax.dynamic_slice_in_dim(g, mine * (bw // N_CHIPS), bw // N_CHIPS, axis=1)

    grads = {
        "a_w_in": gr_win, "a_ln_w": g_a_ln_w, "a_ln_b": g_a_ln_b,
        "a_b_s": g_a_b_s[None], "a_w_out": gr_wout, "b_w_in": gr_bin, "b_conv_w": shard(gf_cw)[None],
        "b_conv_b": shard(gf_cb), "b_gate_a_b": shard(gf_gab),
        "b_gate_x_b": shard(gf_gxb), "b_lambda": shard(gf_lam),
        "b_w_out": gr_bout, "norm_f_w": g_norm_f,
    }
    pack_names = [
        "norm_w", "a_ln_w", "a_ln_b", "a_w_s", "a_b_s", "b_gate_a_w", "b_gate_x_w", "norm_f_w", "b_conv_w", "b_conv_b",
        "b_gate_a_b", "b_gate_x_b", "b_lambda", "loss",
    ]
    row_of, at = {}, 0
    for n, shape in zip(pack_names, small_shapes):
        row_of[n] = at // LANES
        at += _size(shape)
    names = list(weights)
    big_names = ["a_w_in", "a_w_out", "b_w_in", "b_w_out"]
    small_names = [n for n in names if n not in big_names]
    from_pack = [n for n in small_names if n in ("a_w_s", "b_gate_a_w", "b_gate_x_w")]
    as_rows = lambda n, a: a.reshape(-1, LANES) if n in from_pack else a.reshape(1, -1) if a.ndim == 1 else a
    small_grads = {n: row_of[n] for n in from_pack}
    small_grads["norm_w"] = (g_nw0_all, row_of["norm_w"])
    *small_out, pack_grads = _adamw_many(
        [as_rows(n, weights[n]) for n in small_names],
        [small_grads[n] if n in small_grads else as_rows(n, grads[n]) for n in small_names],
        [as_rows(n, m_in[n]) for n in small_names],
        [as_rows(n, v_in[n]) for n in small_names],
        small_r.reshape(-1, LANES),
    )
    grads.update(zip([n for n in small_names if n in small_grads], pack_grads))
    *big_out, big_grads = _adamw_rows(
        *[[src[n].reshape(weights[n].shape[-2:]) for n in big_names] for src in (weights, grads, m_in, v_in)]
    )
    grads.update(zip(big_names, big_grads))
    delta, new_m, new_v = {}, {}, {}
    for dst, small_vals, big_vals in zip((delta, new_m, new_v), small_out, big_out):
        dst.update(zip(small_names, small_vals))
        dst.update(zip(big_names, big_vals))
    for dst in (grads, delta, new_m, new_v):
        for n in names:
            dst[n] = dst[n].reshape(weights[n].shape)

    return (
        loss.reshape(()),
        grad_x[None],
        *[grads[n] for n in names],
        *[delta[n] for n in names],
        *[new_m[n] for n in names],
        *[new_v[n] for n in names],
    )
```

```python
import jax
import jax.numpy as jnp
from jax import lax
from jax.experimental import pallas as pl
from jax.experimental.pallas import tpu as pltpu

F32 = jnp.float32
BF16 = jnp.bfloat16

RMS_EPS = 1e-6
LN_EPS = 1e-5
RG_C = 8.0
CHUNK = 128
A_GROUPS = 8
B_HEADS = 12
CONV_WIDTH = 4

ADAM_LR = 0.001
ADAM_B1 = 0.9
ADAM_B2 = 0.999
ADAM_EPS = 1e-08
ADAM_WD = 0.01
ADAM_STEP = 10

N_CHIPS = 4
N_DEV = 8
SUBLANES = 8
LANES = 128
V7X_VMEM_BYTES = 64 * 1024 * 1024
VMEM_LIMIT = V7X_VMEM_BYTES * 7 // 8
MESH = pl.DeviceIdType.MESH
ANY = pl.BlockSpec(memory_space=pl.ANY)

TM_FWD = 256
TM_A_BWD = 256
TM_A_DX = 512

GELU_C0 = 0.7978845608028654
GELU_C1 = 0.044715


class _Hook:
    def __init__(self, operands, out_shapes, aliases, n_sems, start, finish, middle=None, late=None):
        self.operands, self.out_shapes, self.aliases, self.n_sems = operands, out_shapes, aliases, n_sems
        self.start, self.finish, self.middle, self.late = start, finish, middle, late


class _SemView:
    def __init__(self, base, off):
        self.base, self.off = base, off

    @property
    def at(self):
        return self

    def __getitem__(self, k):
        return self.base.at[self.off + k]


def _join_hooks(*hooks):
    if len(hooks) == 1:
        return hooks[0]
    operands, out_shapes, aliases, spans = [], [], {}, []
    n_sems = 0
    for h in hooks:
        aliases.update({len(operands) + i: len(out_shapes) + o for i, o in h.aliases.items()})
        spans.append((len(operands), len(h.operands), len(out_shapes), len(h.out_shapes), n_sems))
        operands += list(h.operands)
        out_shapes += list(h.out_shapes)
        n_sems += h.n_sems

    def each(which):
        def run(ins, outs, send, recv):
            for h, (i0, ni, o0, no, s0) in zip(hooks, spans):
                step = getattr(h, which)
                if step is not None:
                    step(ins[i0 : i0 + ni], outs[o0 : o0 + no], _SemView(send, s0), _SemView(recv, s0))

        return run

    middle = each("middle") if any(h.middle is not None for h in hooks) else None
    late = each("late") if any(h.late is not None for h in hooks) else None
    return _Hook(operands, out_shapes, aliases, n_sems, each("start"), each("finish"), middle, late)


def _pcall(body, hook=None, **kw):
    if hook is None:
        return pl.pallas_call(body, **kw)
    n_pre = 0
    if "grid_spec" in kw:
        spec = kw.pop("grid_spec")
        n_pre = spec.num_scalar_prefetch
        kw.update(
            grid=tuple(spec.grid), in_specs=list(spec.in_specs), out_specs=list(spec.out_specs),
            scratch_shapes=list(spec.scratch_shapes),
        )
    n_in, n_out = len(kw["in_specs"]), len(kw["out_shape"])
    hi, ho = len(hook.operands), len(hook.out_shapes)
    grid = kw.get("grid", ())

    def wrapped(*refs):
        pre, refs = refs[:n_pre], refs[n_pre:]
        ins, h_in = refs[:n_in], refs[n_in : n_in + hi]
        outs = refs[n_in + hi : n_in + hi + n_out]
        h_out = refs[n_in + hi + n_out : n_in + hi + n_out + ho]
        scratch = refs[n_in + hi + n_out + ho : -2]
        send_sems, recv_sems = refs[-2:]
        if not grid:
            hook.start(h_in, h_out, send_sems, recv_sems)
            if hook.middle is not None:
                hook.middle(h_in, h_out, send_sems, recv_sems)
            body(*pre, *ins, *outs, *scratch)
            if hook.late is not None:
                hook.late(h_in, h_out, send_sems, recv_sems)
            hook.finish(h_in, h_out, send_sems, recv_sems)
            return
        first = pl.program_id(0) == 0
        last = pl.program_id(0) == grid[0] - 1
        for axis in range(1, len(grid)):
            first = jnp.logical_and(first, pl.program_id(axis) == 0)
            last = jnp.logical_and(last, pl.program_id(axis) == grid[axis] - 1)

        @pl.when(first)
        def _():
            hook.start(h_in, h_out, send_sems, recv_sems)

        for when, step in ((hook.middle, grid[0] // 4), (hook.late, grid[0] - 1)):
            if when is not None:
                assert len(grid) == 1 and grid[0] >= 4

                @pl.when(pl.program_id(0) == step)
                def _(when=when):
                    when(h_in, h_out, send_sems, recv_sems)

        body(*pre, *ins, *outs, *scratch)

        @pl.when(last)
        def _():
            hook.finish(h_in, h_out, send_sems, recv_sems)

    aliases = dict(kw.pop("input_output_aliases", {}))
    aliases.update({n_pre + n_in + i: n_out + o for i, o in hook.aliases.items()})
    kw.update(
        in_specs=list(kw["in_specs"]) + [ANY] * hi,
        out_specs=list(kw["out_specs"]) + [ANY] * ho,
        out_shape=list(kw["out_shape"]) + list(hook.out_shapes),
        scratch_shapes=list(kw.get("scratch_shapes", ()))
        + [pltpu.SemaphoreType.DMA((hook.n_sems,)), pltpu.SemaphoreType.DMA((hook.n_sems,))],
        input_output_aliases=aliases,
    )
    if n_pre:
        kw["grid_spec"] = pltpu.PrefetchScalarGridSpec(
            num_scalar_prefetch=n_pre, grid=kw.pop("grid"), in_specs=kw.pop("in_specs"),
            out_specs=kw.pop("out_specs"), scratch_shapes=kw.pop("scratch_shapes"),
        )
    call = pl.pallas_call(wrapped, **kw)

    def run(*operands):
        outs = call(*operands, *hook.operands)
        return outs[:n_out], outs[n_out:]

    return run


def _run_hook(hook, name):
    def body():
        pass

    return _pcall(body, hook, name=name, in_specs=[], out_specs=[], out_shape=[])()[1]


def _cparams(sem=None):
    return pltpu.CompilerParams(dimension_semantics=sem, vmem_limit_bytes=VMEM_LIMIT)


def _full(shape):
    zeros = (0,) * len(shape)
    return pl.BlockSpec(shape, lambda *_: zeros)


def _scalars(*vals):
    return jnp.stack([jnp.asarray(v, jnp.int32) for v in vals])


def _sigmoid(x):
    return 1.0 / (1.0 + jnp.exp(-x))


def _gelu(x):
    t = jnp.tanh(GELU_C0 * (x + GELU_C1 * (x * x * x)))
    return x * (0.5 * (1.0 + t))


def _gelu_and_grad(x):
    x2 = x * x
    t = jnp.tanh(GELU_C0 * (x + GELU_C1 * (x2 * x)))
    cdf = 0.5 * (1.0 + t)
    return x * cdf, cdf + 0.5 * x * (1.0 - t * t) * (GELU_C0 * (1.0 + 3.0 * GELU_C1 * x2))


def _silu_and_grad(x):
    s = _sigmoid(x)
    return x * s, s * (1.0 + x * (1.0 - s))


def _softplus_neg(lam):
    u = jnp.exp(-jnp.abs(lam))
    w = 1.0 + u
    log1p = jnp.where(w == 1.0, u, jnp.log(w) * (u / jnp.where(w == 1.0, 1.0, w - 1.0)))
    return jnp.maximum(-lam, 0.0) + log1p


def _dot(a, b):
    return jnp.dot(a, b, preferred_element_type=F32)


def _dot_nt(a, b):
    return lax.dot_general(a, b, (((1,), (1,)), ((), ())), preferred_element_type=F32)


def _dot_tn(a, b):
    return lax.dot_general(a, b, (((0,), (0,)), ((), ())), preferred_element_type=F32)


def _shift_down(v, halo, k):
    if k == 0:
        return v
    rolled = pltpu.roll(v, k, 0)
    row = lax.broadcasted_iota(jnp.int32, (SUBLANES, v.shape[1]), 0)
    top = jnp.where(row < k, pltpu.roll(halo, k, 0), rolled[:SUBLANES])
    return jnp.concatenate([top, rolled[SUBLANES:]], axis=0)


def _shift_up(v, head, k):
    if k == 0:
        return v
    n = v.shape[0]
    rolled = pltpu.roll(v, n - k, 0)
    row = lax.broadcasted_iota(jnp.int32, (SUBLANES, v.shape[1]), 0)
    bot = jnp.where(row >= SUBLANES - k, pltpu.roll(head, SUBLANES - k, 0), rolled[n - SUBLANES :])
    return jnp.concatenate([rolled[: n - SUBLANES], bot], axis=0)


def _scan_blocks(a_ref, b_ref, out_ref, carry, n_rows, reverse):
    width = a_ref.shape[1]
    row = lax.broadcasted_iota(jnp.int32, (SUBLANES, width), 0)
    n_blocks = n_rows // SUBLANES

    def block(j, carry):
        i = (n_blocks - 1 - j) if reverse else j
        r0 = pl.multiple_of(i * SUBLANES, SUBLANES)
        a = a_ref[pl.ds(r0, SUBLANES), :]
        b = b_ref[pl.ds(r0, SUBLANES), :]
        for d in (1, 2, 4):
            shift = (SUBLANES - d) if reverse else d
            keep = (row < SUBLANES - d) if reverse else (row >= d)
            a_s = pltpu.roll(a, shift, 0)
            b_s = pltpu.roll(b, shift, 0)
            b = jnp.where(keep, a * b_s + b, b)
            a = jnp.where(keep, a * a_s, a)
        h = a * carry + b
        out_ref[pl.ds(r0, SUBLANES), :] = h
        edge = h[0:1, :] if reverse else h[SUBLANES - 1 : SUBLANES, :]
        return jnp.broadcast_to(edge, (SUBLANES, width))

    return lax.fori_loop(0, n_blocks, block, carry)


def _rms_fwd(x, w):
    r = lax.rsqrt(jnp.mean(x * x, axis=-1, keepdims=True) + RMS_EPS)
    xh = x * r
    return xh * w, xh, r


def _rms_bwd(dh, xh, r, w):
    dxh = dh * w
    dx = r * (dxh - xh * jnp.mean(dxh * xh, axis=-1, keepdims=True))
    return dx, jnp.sum(dh * xh, axis=0, keepdims=True)


def _cast_to_segments(ws, table, steps, hook=None, small_maps=None):
    per = steps // 2
    n = len(ws)
    maps = () if small_maps is None else small_maps

    def body(k_ref, *refs):
        for w_ref, o_ref in zip(refs[:n], refs[n + len(maps) : 2 * n + len(maps)]):
            o_ref[...] = w_ref[...].astype(BF16)
        if small_maps is not None:
            pl.when(pl.program_id(0) == 0)(lambda: _prepare_small_maps(*refs[n : n + 3], *refs[2 * n + 3 :]))

    rows = [w.shape[0] // steps for w in ws]
    segment = lambda i, k_ref: (2 * k_ref[AT_MINE] + i // per, i % per, 0)
    whole = lambda shape: pl.BlockSpec(shape, lambda i, k_ref: (0,) * len(shape))
    prepared = []
    if small_maps is not None:
        (g, ck, _), (h, hd, _) = maps[0].shape, maps[1].shape
        prepared = [(g, ck, ck), (g, ck, ck), (h, hd, 2 * hd), (h, 2 * hd, hd)]
    out = _pcall(
        body,
        hook,
        name=f"cast_{ws[0].shape[0]}x{ws[0].shape[1]}",
        grid_spec=pltpu.PrefetchScalarGridSpec(
            num_scalar_prefetch=1,
            grid=(steps,),
            in_specs=[pl.BlockSpec((r, w.shape[1]), lambda i, k_ref: (i, 0)) for w, r in zip(ws, rows)]
            + [whole(m.shape) for m in maps],
            out_specs=[pl.BlockSpec((None, r, w.shape[1]), segment) for w, r in zip(ws, rows)]
            + [whole(shape) for shape in prepared],
        ),
        out_shape=[jax.ShapeDtypeStruct((N_DEV, w.shape[0] // 2, w.shape[1]), BF16) for w in ws]
        + [jax.ShapeDtypeStruct(shape, BF16) for shape in prepared],
        compiler_params=_cparams(("arbitrary",)),
    )(table, *ws, *maps)
    return out


def _prepare_small_maps(ws_ref, ga_ref, gx_ref, wc_ref, wct_ref, gab_ref, gabt_ref):
    ck, hd = ws_ref.shape[1], ga_ref.shape[1]
    tril = lax.broadcasted_iota(jnp.int32, (ck, ck), 0) >= lax.broadcasted_iota(jnp.int32, (ck, ck), 1)
    for g in range(ws_ref.shape[0]):
        w = ws_ref[g] * tril.astype(F32)
        wc_ref[g] = w.astype(BF16)
        wct_ref[g] = w.T.astype(BF16)
    for h in range(ga_ref.shape[0]):
        for k, m_ref in enumerate((ga_ref, gx_ref)):
            m = m_ref[h]
            gab_ref[h, :, k * hd : (k + 1) * hd] = m.astype(BF16)
            gabt_ref[h, k * hd : (k + 1) * hd, :] = m.T.astype(BF16)


def _place():
    x, y, c = lax.axis_index("x"), lax.axis_index("y"), lax.axis_index("c")
    chips = [(1 - x, y), (x, 1 - y), (1 - x, 1 - y)]
    return x, y, c, chips


def _chip_no(chip):
    return 2 * chip[0] + chip[1]


def _rcopy(src, dst, send_sem, recv_sem, to):
    return pltpu.make_async_remote_copy(
        src_ref=src, dst_ref=dst, send_sem=send_sem, recv_sem=recv_sem, device_id=to, device_id_type=MESH
    )


def _gather_hook(big, small=None):
    nb = len(big)
    n_sems = 6 * nb + 4

    def places():
        x, y, c, chips = _place()
        first = (x ^ (1 - c), y ^ c)
        second = (x ^ c, y ^ (1 - c))
        return x, y, c, chips, first, second, (1 - x, 1 - y)

    def seg(outs, b, chip, half):
        return outs[b].at[2 * _chip_no(chip) + half]

    def step1(outs, send, recv):
        x, y, c, _, first, _, _ = places()
        return [
            _rcopy(seg(outs, b, (x, y), c), seg(outs, b, (x, y), c), send.at[6 * b], recv.at[6 * b], (*first, c))
            for b in range(nb)
        ]

    def step2(outs, send, recv):
        x, y, c, _, first, second, _ = places()
        copies = []
        for b in range(nb):
            for k, chip in ((1, (x, y)), (2, first)):
                src = seg(outs, b, chip, c)
                copies.append(_rcopy(src, src, send.at[6 * b + k], recv.at[6 * b + k], (*second, c)))
        return copies

    def hand_over(outs, send, recv, k, chip):
        x, y, c, *_ = places()
        return [
            _rcopy(seg(outs, b, chip, c), seg(outs, b, chip, c), send.at[6 * b + k], recv.at[6 * b + k], (x, y, 1 - c))
            for b in range(nb)
        ]

    def wait_landed(outs, send, recv, k, chip, half):
        x, y, c, *_ = places()
        for b in range(nb):
            got = seg(outs, b, chip, half)
            _rcopy(got, got, send.at[6 * b + k], recv.at[6 * b + k], (x, y, c)).wait_recv()

    def small_copies(ins, outs, send, recv):
        x, y, c, chips, *_ = places()
        there = outs[nb].at[_chip_no((x, y))]
        return [
            _rcopy(ins[nb], there, send.at[6 * nb + j], recv.at[6 * nb + j], (*chip, c)) for j, chip in enumerate(chips)
        ]

    def local_copy(ins, outs, send):
        x, y, _, _ = _place()
        return pltpu.make_async_copy(ins[nb], outs[nb].at[_chip_no((x, y))], send.at[6 * nb + 3])

    def start(ins, outs, send, recv):
        for cp in step1(outs, send, recv):
            cp.start()
        if small is not None:
            for cp in small_copies(ins, outs, send, recv):
                cp.start()
            local_copy(ins, outs, send).start()

    def middle(ins, outs, send, recv):
        *_, first, _, _ = places()
        wait_landed(outs, send, recv, 0, first, places()[2])
        for cp in step2(outs, send, recv) + hand_over(outs, send, recv, 3, first):
            cp.start()

    def late(ins, outs, send, recv):
        x, y, c, chips, first, second, diagonal = places()
        for k, chip in ((1, second), (2, diagonal)):
            wait_landed(outs, send, recv, k, chip, c)
            for cp in hand_over(outs, send, recv, 3 + k, chip):
                cp.start()

    def finish(ins, outs, send, recv):
        x, y, c, chips, first, second, diagonal = places()
        wait_landed(outs, send, recv, 3, second, 1 - c)
        wait_landed(outs, send, recv, 4, first, 1 - c)
        wait_landed(outs, send, recv, 5, diagonal, 1 - c)
        sent = step1(outs, send, recv) + step2(outs, send, recv)
        for k, chip in ((3, first), (4, second), (5, diagonal)):
            sent += hand_over(outs, send, recv, k, chip)
        for cp in sent:
            cp.wait_send()
        if small is not None:
            for j, chip in enumerate(chips):
                got = outs[nb].at[_chip_no(chip)]
                _rcopy(got, got, send.at[6 * nb + j], recv.at[6 * nb + j], (x, y, c)).wait_recv()
            for cp in small_copies(ins, outs, send, recv):
                cp.wait_send()
            local_copy(ins, outs, send).wait()

    operands = list(big) + ([small] if small is not None else [])
    out_shapes = [jax.ShapeDtypeStruct(b.shape, b.dtype) for b in big]
    if small is not None:
        out_shapes.append(jax.ShapeDtypeStruct((N_CHIPS, *small.shape), small.dtype))
    return _Hook(operands, out_shapes, {b: b for b in range(nb)}, n_sems, start, finish, middle, late)


def _both_ways_hook(operands, out_shapes, copies_of, n_sems):
    def start(ins, outs, send, recv):
        for cp in copies_of(ins, outs, send, recv):
            cp.start()

    def finish(ins, outs, send, recv):
        for cp in copies_of(ins, outs, send, recv):
            cp.wait()

    return _Hook(operands, out_shapes, {}, n_sems, start, finish)


def _swap_hook(bufs):
    def copies_of(ins, outs, send, recv):
        x, y, c, _ = _place()
        copies = []
        for b in range(len(bufs)):
            for j in range(N_CHIPS):
                k = b * N_CHIPS + j
                copies.append(_rcopy(ins[b].at[2 * j + 1 - c], outs[b].at[j], send.at[k], recv.at[k], (x, y, 1 - c)))
        return copies

    out_shapes = [jax.ShapeDtypeStruct((N_CHIPS, *b.shape[1:]), b.dtype) for b in bufs]
    return _both_ways_hook(list(bufs), out_shapes, copies_of, len(bufs) * N_CHIPS)


def _axis_order():
    x, y, c, _ = _place()
    return (x, y), c, (x ^ (1 - c), y ^ c), (x ^ c, y ^ (1 - c)), (1 - x, 1 - y)


def _send_first_hook(parts):
    def copies_of(ins, outs, send, recv):
        _, c, first, _, _ = _axis_order()
        copies = []
        for b in range(len(parts)):
            for k in range(2):
                sem = 2 * b + k
                copies.append(_rcopy(ins[b].at[k], outs[b].at[k], send.at[sem], recv.at[sem], (*first, c)))
        return copies

    out_shapes = [jax.ShapeDtypeStruct((2, *p.shape[1:]), p.dtype) for p in parts]
    return _both_ways_hook(list(parts), out_shapes, copies_of, len(parts) * 2)


def _send_second_hook(mids):
    def copies_of(ins, outs, send, recv):
        _, c, _, second, _ = _axis_order()
        return [_rcopy(ins[b], outs[b], send.at[b], recv.at[b], (*second, c)) for b in range(len(mids))]

    out_shapes = [jax.ShapeDtypeStruct(m.shape, m.dtype) for m in mids]
    return _both_ways_hook(list(mids), out_shapes, copies_of, len(mids))


def _share_hook(big, small=None, tiny=None):
    nb = len(big)
    n_sems = nb + 7 + N_DEV
    t0 = nb + 7

    def tiny_copies(ins, outs, send, recv):
        x, y, c, _ = _place()
        there = outs[-1].at[2 * _chip_no((x, y)) + c]
        copies = []
        for r in range(1, N_DEV):
            to = (x ^ (r >> 2 & 1), y ^ (r >> 1 & 1), c ^ (r & 1))
            copies.append(_rcopy(ins[-1], there, send.at[t0 + r], recv.at[t0 + r], to))
        return copies

    def tiny_local(ins, outs, send):
        x, y, c, _ = _place()
        return pltpu.make_async_copy(ins[-1], outs[-1].at[2 * _chip_no((x, y)) + c], send.at[t0])

    def first_copies(outs, send, recv):
        x, y, c, chips = _place()
        sibling = (x, y, 1 - c)
        copies = [_rcopy(outs[b].at[c], outs[b].at[c], send.at[b], recv.at[b], sibling) for b in range(nb)]
        if small is not None:
            own = outs[nb].at[2 * _chip_no((x, y)) + c]
            copies.append(_rcopy(own, own, send.at[nb], recv.at[nb], sibling))
            for j, chip in enumerate(chips):
                copies.append(_rcopy(own, own, send.at[nb + 1 + j], recv.at[nb + 1 + j], (*chip, c)))
        return copies

    def start(ins, outs, send, recv):
        for cp in first_copies(outs, send, recv):
            cp.start()
        if tiny is not None:
            for cp in tiny_copies(ins, outs, send, recv):
                cp.start()
            tiny_local(ins, outs, send).start()

    def finish(ins, outs, send, recv):
        x, y, c, chips = _place()
        me, sibling = (x, y, c), (x, y, 1 - c)
        if tiny is not None:
            for cp in tiny_copies(ins, outs, send, recv):
                cp.wait()
            tiny_local(ins, outs, send).wait()
        passed = []
        if small is not None:
            for j, chip in enumerate(chips):
                got = outs[nb].at[2 * _chip_no(chip) + c]
                _rcopy(got, got, send.at[nb + 1 + j], recv.at[nb + 1 + j], me).wait_recv()
                fwd = _rcopy(got, got, send.at[nb + 4 + j], recv.at[nb + 4 + j], sibling)
                fwd.start()
                passed.append(fwd)
        for b in range(nb):
            got = outs[b].at[1 - c]
            _rcopy(got, got, send.at[b], recv.at[b], me).wait_recv()
        if small is not None:
            got = outs[nb].at[2 * _chip_no((x, y)) + 1 - c]
            _rcopy(got, got, send.at[nb], recv.at[nb], me).wait_recv()
            for j, chip in enumerate(chips):
                got = outs[nb].at[2 * _chip_no(chip) + 1 - c]
                _rcopy(got, got, send.at[nb + 4 + j], recv.at[nb + 4 + j], me).wait_recv()
        for cp in first_copies(outs, send, recv) + passed:
            cp.wait_send()

    operands = list(big) + ([small] if small is not None else [])
    out_shapes = [jax.ShapeDtypeStruct(a.shape, a.dtype) for a in operands]
    aliases = {i: i for i in range(len(operands))}
    if tiny is not None:
        operands.append(tiny)
        out_shapes.append(jax.ShapeDtypeStruct((N_DEV, *tiny.shape), tiny.dtype))
    return _Hook(operands, out_shapes, aliases, n_sems, start, finish)


SUM_BLOCK_BYTES = 2 * 1024 * 1024


def _row_tile(rows, cols):
    best = SUBLANES
    for t in range(SUBLANES, rows + 1, SUBLANES):
        if rows % t == 0 and t * cols * 4 <= SUM_BLOCK_BYTES:
            best = t
    return best


def _halves(buf):
    return buf.reshape(N_CHIPS, 2, *buf.shape[1:])


AT_ZERO, AT_ONE, AT_FIRST, AT_SECOND, AT_DIAGONAL, AT_MINE, AT_CORE, AT_DEVICE = range(8)


def _index_table():
    x, y, c = lax.axis_index("x"), lax.axis_index("y"), lax.axis_index("c")
    first, second = (x ^ (1 - c), y ^ c), (x ^ c, y ^ (1 - c))
    mine = _chip_no((x, y))
    return _scalars(0, 1, _chip_no(first), _chip_no(second), N_CHIPS - 1 - mine, mine, c, 2 * mine + c)


def _sum_jobs(jobs, name, table):
    main, riders = jobs[0], jobs[1:]
    rows, cols = main[0][0][0][0].shape[-2:]
    tr = _row_tile(rows, cols)
    n_main = len(main[0][0])
    in_specs, operands, out_specs, out_shape = [], [], [], []

    def walked(leads):
        base = leads[0]
        strides = [b - a for a, b in zip(leads[0], leads[1])] if len(leads) > 1 else [0] * len(base)
        assert all(lead == tuple(a + g * s for a, s in zip(base, strides)) for g, lead in enumerate(leads))
        return lambda j, r, s_ref: (*[s_ref[a + j * s] for a, s in zip(base, strides)], r, 0)

    for t, (arr, lead) in enumerate(main[0][0]):
        operands.append(arr)
        in_specs.append(pl.BlockSpec((None,) * len(lead) + (tr, cols), walked([group[t][1] for group in main[0]])))
    out_specs.append(pl.BlockSpec((None,) * len(main[1]) + (tr, cols), walked(main[2])))
    out_shape.append(jax.ShapeDtypeStruct((*main[1], rows, cols), main[3]))

    whole = lambda lead: lambda j, r, s_ref: (*[s_ref[at] for at in lead], 0, 0)
    for groups, out_dims, out_leads, dtype in riders:
        r_k, c_k = groups[0][0][0].shape[-2:]
        for group in groups:
            for arr, lead in group:
                operands.append(arr)
                in_specs.append(pl.BlockSpec((None,) * len(lead) + (r_k, c_k), whole(lead)))
        if len(groups) == 1:
            out_specs.append(pl.BlockSpec((None,) * len(out_dims) + (r_k, c_k), whole(out_leads[0])))
        else:
            assert out_dims == (len(groups),) and list(out_leads) == [(AT_ZERO + g,) for g in range(len(groups))]
            out_specs.append(pl.BlockSpec((len(groups), r_k, c_k), lambda j, r, s_ref: (0, 0, 0)))
        out_shape.append(jax.ShapeDtypeStruct((*out_dims, r_k, c_k), dtype))

    def total(term_refs):
        acc = term_refs[0][...].astype(F32)
        for t_ref in term_refs[1:]:
            acc = acc + t_ref[...].astype(F32)
        return acc

    def body(s_ref, *refs):
        ins, outs = refs[: len(operands)], refs[len(operands) :]
        outs[0][...] = total(ins[:n_main]).astype(main[3])

        def ride():
            at = n_main
            for (groups, _, _, dtype), o_ref in zip(riders, outs[1:]):
                for g, group in enumerate(groups):
                    acc = total(ins[at : at + len(group)]).astype(dtype)
                    at += len(group)
                    if len(groups) == 1:
                        o_ref[...] = acc
                    else:
                        o_ref[g] = acc

        if riders:
            pl.when(jnp.logical_and(pl.program_id(0) == 0, pl.program_id(1) == 0))(ride)

    return _pcall(
        body,
        name=name,
        grid_spec=pltpu.PrefetchScalarGridSpec(
            num_scalar_prefetch=1, grid=(len(main[0]), rows // tr), in_specs=in_specs, out_specs=out_specs
        ),
        out_shape=out_shape,
        compiler_params=_cparams(("arbitrary", "arbitrary")),
    )(table, *operands)


def _own_half_job(buf, got, wire):
    groups = [[(_halves(buf), (j, AT_CORE)), (got, (j,))] for j in (AT_FIRST, AT_DIAGONAL)]
    return groups, (2,), [(AT_ZERO,), (AT_ONE,)], wire


def _for_neighbour_job(buf, got, got1, wire):
    return [[(_halves(buf), (AT_SECOND, AT_CORE)), (got, (AT_SECOND,)), (got1, (AT_ONE,))]], (), [()], wire


def _received_job(buf, got, got1, got2, slot, n_slots):
    terms = [(_halves(buf), (AT_MINE, AT_CORE)), (got, (AT_MINE,)), (got1, (AT_ZERO,)), (got2, ())]
    return [terms], (n_slots,), [(slot,)], F32


def _layer_a_fwd(x, nw, win, ln_w, ln_b, wc, bs_t, wout, tm, hook):
    t_rows, d = x.shape
    n_sh, _, s_cols = win.shape
    aw = wout.shape[0]
    gd = aw // A_GROUPS
    tn = 512
    assert s_cols % tn == 0 and aw % tn == 0 and tm % CHUNK == 0

    def body(x_ref, nw_ref, win_ref, lnw_ref, lnb_ref, wc_ref, bst_ref, wout_ref, z_ref, x1_ref, h_ref, u_s, v_s, y_s):
        x = x_ref[...]
        h, _, _ = _rms_fwd(x, nw_ref[...])
        h = h.astype(BF16)
        h_ref[...] = h
        for j in range(3 * aw // tn):
            k, off = divmod(j * tn, s_cols)
            cols = slice((j * tn) % aw, (j * tn) % aw + tn)
            zj = _dot(h, win_ref[k, :, off : off + tn])
            z_ref[:, j * tn : (j + 1) * tn] = zj
            if j * tn < aw:
                u_s[:, cols] = _gelu(zj)
            elif j * tn < 2 * aw:
                v_s[:, cols] = _gelu(zj)
            else:
                u_s[:, cols] = u_s[:, cols] * (zj * _sigmoid(zj))
        v = v_s[...]
        mu = jnp.mean(v, axis=-1, keepdims=True)
        vc = v - mu
        rstd = lax.rsqrt(jnp.mean(vc * vc, axis=-1, keepdims=True) + LN_EPS)
        v_s[...] = (vc * rstd) * lnw_ref[...] + lnb_ref[...]
        for ck in range(tm // CHUNK):
            rows = slice(ck * CHUNK, (ck + 1) * CHUNK)
            for g in range(A_GROUPS):
                cols = slice(g * gd, (g + 1) * gd)
                s = _dot(wc_ref[g], v_s[rows, cols].astype(BF16)) + bst_ref[:, g : g + 1]
                y_s[rows, cols] = (u_s[rows, cols] * s).astype(BF16)
        x1_ref[...] = x + _dot(y_s[...], wout_ref[...])

    row = lambda i: (i, 0)
    return _pcall(
        body,
        hook,
        name="layer_a_fwd",
        grid=(t_rows // tm,),
        in_specs=[
            pl.BlockSpec((tm, d), row),
            _full(nw.shape),
            _full(win.shape),
            _full(ln_w.shape),
            _full(ln_b.shape),
            _full(wc.shape),
            _full(bs_t.shape),
            _full(wout.shape),
        ],
        out_specs=[pl.BlockSpec((tm, 3 * aw), row), pl.BlockSpec((tm, d), row), pl.BlockSpec((tm, d), row)],
        out_shape=[
            jax.ShapeDtypeStruct((t_rows, 3 * aw), F32),
            jax.ShapeDtypeStruct((t_rows, d), F32),
            jax.ShapeDtypeStruct((t_rows, d), BF16),
        ],
        scratch_shapes=[pltpu.VMEM((tm, aw), F32), pltpu.VMEM((tm, aw), F32), pltpu.VMEM((tm, aw), BF16)],
        compiler_params=_cparams(("arbitrary",)),
    )(x, nw, win, ln_w, ln_b, wc, bs_t, wout)


def _layer_a_bwd(dout, z, ln_w, ln_b, wc, wct, bs_t, wout, tiles, earlier, hook):
    t_rows, d = dout.shape
    aw = wout.shape[0]
    gd = aw // A_GROUPS
    tm = TM_A_BWD
    lo, hi = tiles
    n_earlier = 0 if earlier is None else len(earlier)

    def body(dout_ref, z_ref, lnw_ref, lnb_ref, wc_ref, wct_ref, bst_ref, wout_ref, *rest):
        dz_ref, y_ref, dob_ref, gws_ref, gbs_ref, glnw_ref, glnb_ref, u_s, vh_s, ds_s, dvn_s = rest[n_earlier:]

        @pl.when(pl.program_id(0) == 0)
        def _():
            gws_ref[...] = jnp.zeros_like(gws_ref)
            gbs_ref[...] = jnp.zeros_like(gbs_ref)
            glnw_ref[...] = jnp.zeros_like(glnw_ref)
            glnb_ref[...] = jnp.zeros_like(glnb_ref)

        dob = dout_ref[...].astype(BF16)
        dob_ref[...] = dob
        dy = _dot_nt(dob, wout_ref[...])

        zv = z_ref[:, aw : 2 * aw]
        vg, dvg_dz = _gelu_and_grad(zv)
        mu = jnp.mean(vg, axis=-1, keepdims=True)
        vc = vg - mu
        rstd = lax.rsqrt(jnp.mean(vc * vc, axis=-1, keepdims=True) + LN_EPS)
        vh = vc * rstd
        vh_s[...] = vh
        vn = (vh * lnw_ref[...] + lnb_ref[...]).astype(BF16)

        zu = z_ref[:, 0:aw]
        zg = z_ref[:, 2 * aw : 3 * aw]
        u, du_dz = _gelu_and_grad(zu)
        sg, dsg = _silu_and_grad(zg)
        u_s[...] = u * sg
        tril = lax.broadcasted_iota(jnp.int32, (CHUNK, CHUNK), 0) >= lax.broadcasted_iota(jnp.int32, (CHUNK, CHUNK), 1)
        for ck in range(tm // CHUNK):
            rows = slice(ck * CHUNK, (ck + 1) * CHUNK)
            for g in range(A_GROUPS):
                cols = slice(g * gd, (g + 1) * gd)
                vn_g = vn[rows, cols]
                s = _dot(wc_ref[g], vn_g) + bst_ref[:, g : g + 1]
                usg = u_s[rows, cols]
                dy_g = dy[rows, cols]
                y_ref[rows, cols] = (usg * s).astype(BF16)
                ds = dy_g * usg
                ds_s[rows, cols] = dy_g * s
                gbs_ref[:, g : g + 1] += jnp.sum(ds, axis=-1, keepdims=True)
                dsb = ds.astype(BF16)
                gws_ref[g] += jnp.where(tril, _dot_nt(dsb, vn_g), 0.0)
                dvn_s[rows, cols] = _dot(wct_ref[g], dsb)
        dusg = ds_s[...]
        dz_ref[:, 0:aw] = (dusg * sg * du_dz).astype(BF16)
        dz_ref[:, 2 * aw : 3 * aw] = (dusg * u * dsg).astype(BF16)

        dvn = dvn_s[...]
        vh = vh_s[...]
        glnw_ref[...] += jnp.sum(dvn * vh, axis=0, keepdims=True)
        glnb_ref[...] += jnp.sum(dvn, axis=0, keepdims=True)
        dvh = dvn * lnw_ref[...]
        dvg = rstd * (dvh - jnp.mean(dvh, axis=-1, keepdims=True) - vh * jnp.mean(dvh * vh, axis=-1, keepdims=True))
        dz_ref[:, aw : 2 * aw] = (dvg * dvg_dz).astype(BF16)

    row = lambda i: (i + lo, 0)
    call = _pcall(
        body,
        hook,
        name=f"layer_a_bwd_{lo}",
        grid=(hi - lo,),
        in_specs=[
            pl.BlockSpec((tm, d), row),
            pl.BlockSpec((tm, 3 * aw), row),
            _full(ln_w.shape),
            _full(ln_b.shape),
            _full(wc.shape),
            _full(wct.shape),
            _full(bs_t.shape),
            _full(wout.shape),
        ]
        + [ANY] * n_earlier,
        out_specs=[
            pl.BlockSpec((tm, 3 * aw), row),
            pl.BlockSpec((tm, aw), row),
            pl.BlockSpec((tm, d), row),
            _full((A_GROUPS, CHUNK, CHUNK)),
            _full((CHUNK, A_GROUPS)),
            _full((1, aw)),
            _full((1, aw)),
        ],
        out_shape=[
            jax.ShapeDtypeStruct((t_rows, 3 * aw), BF16),
            jax.ShapeDtypeStruct((t_rows, aw), BF16),
            jax.ShapeDtypeStruct((t_rows, d), BF16),
            jax.ShapeDtypeStruct((A_GROUPS, CHUNK, CHUNK), F32),
            jax.ShapeDtypeStruct((CHUNK, A_GROUPS), F32),
            jax.ShapeDtypeStruct((1, aw), F32),
            jax.ShapeDtypeStruct((1, aw), F32),
        ],
        scratch_shapes=[pltpu.VMEM((tm, aw), F32)] * 4,
        input_output_aliases={8 + i: i for i in range(n_earlier)},
        compiler_params=_cparams(("arbitrary",)),
    )
    return call(dout, z, ln_w, ln_b, wc, wct, bs_t, wout, *(earlier or ()))


def _layer_a_bwd_dx(dout, x, dz, nw, win, tm, tiles, earlier, hook):
    t_rows, d = x.shape
    n_sh, _, s_cols = win.shape
    lo, hi = tiles
    n_earlier = 0 if earlier is None else 1

    def body(dout_ref, x_ref, dz_ref, nw_ref, win_ref, *rest):
        gx_ref, gnw_ref = rest[n_earlier:]

        @pl.when(pl.program_id(0) == 0)
        def _():
            gnw_ref[...] = jnp.zeros_like(gnw_ref)

        dh = jnp.zeros((tm, d), F32)
        for k in range(n_sh):
            dh = dh + _dot_nt(dz_ref[:, k * s_cols : (k + 1) * s_cols], win_ref[k])
        nw = nw_ref[...]
        _, xh, r = _rms_fwd(x_ref[...], nw)
        dx, gnw = _rms_bwd(dh, xh, r, nw)
        gnw_ref[0:1, :] += gnw
        gx_ref[...] = dout_ref[...] + dx

    row = lambda i: (i + lo, 0)
    return _pcall(
        body,
        hook,
        name=f"layer_a_bwd_dx_{lo}",
        grid=(hi - lo,),
        in_specs=[
            pl.BlockSpec((tm, d), row),
            pl.BlockSpec((tm, d), row),
            pl.BlockSpec((tm, n_sh * s_cols), row),
            _full(nw.shape),
            _full(win.shape),
        ]
        + [ANY] * n_earlier,
        out_specs=[pl.BlockSpec((tm, d), row), _full((SUBLANES, d))],
        out_shape=[jax.ShapeDtypeStruct((t_rows, d), F32), jax.ShapeDtypeStruct((SUBLANES, d), F32)],
        input_output_aliases={5: 0} if n_earlier else {},
        compiler_params=_cparams(("arbitrary",)),
    )(dout, x, dz, nw, win, *([earlier] if n_earlier else []))


def _decay(r, sp_h):
    log_a = (-RG_C) * r * sp_h
    a = jnp.exp(log_a)
    mult = jnp.sqrt(jnp.tanh(-log_a) * (a * a + 1.0))
    return a, mult


ROW_CONV_B, ROW_GATE_A_B, ROW_GATE_X_B, ROW_LAMBDA = range(CONV_WIDTH, CONV_WIDTH + 4)


def _gates(xc_h, gab_ref, vec_ref, sp_h, h, hd):
    pre = _dot(xc_h.astype(BF16), gab_ref[h])
    cols = slice(h * hd, (h + 1) * hd)
    r = _sigmoid(pre[:, :hd] + vec_ref[ROW_GATE_A_B : ROW_GATE_A_B + 1, cols])
    ig = _sigmoid(pre[:, hd:] + vec_ref[ROW_GATE_X_B : ROW_GATE_X_B + 1, cols])
    a, mult = _decay(r, sp_h)
    return r, ig, a, mult


def _conv(xb, halo, vec_ref):
    xc = vec_ref[ROW_CONV_B : ROW_CONV_B + 1, :] + vec_ref[CONV_WIDTH - 1 : CONV_WIDTH, :] * xb
    for k in range(CONV_WIDTH - 1):
        xc = xc + vec_ref[k : k + 1, :] * _shift_down(xb, halo, CONV_WIDTH - 1 - k)
    return xc


def _layer_b_fwd(x1, nw, bin_w, vec, gab, bout, nf, tgt, tm):
    t_rows, d = x1.shape
    bw = bout.shape[0]
    hd = bw // B_HEADS
    nt = t_rows // tm

    def body(
        x1_ref, nw_ref, bin_ref, vec_ref, gab_ref, bout_ref, nf_ref, tgt_ref,
        z_ref, h_ref, h1_ref, dx2_ref, loss_ref, gnf_ref,
        tail_s, carry_s, a_s, b_s, hs_s, acc_s,
    ):
        @pl.when(pl.program_id(0) == 0)
        def _():
            tail_s[...] = jnp.zeros_like(tail_s)
            carry_s[...] = jnp.zeros_like(carry_s)
            acc_s[...] = jnp.zeros_like(acc_s)
            gnf_ref[...] = jnp.zeros_like(gnf_ref)

        x1 = x1_ref[...]
        h1, _, _ = _rms_fwd(x1, nw_ref[...])
        h1 = h1.astype(BF16)
        h1_ref[...] = h1
        z = jnp.concatenate([_dot(h1, bin_ref[k]) for k in range(N_CHIPS)], axis=1)
        z_ref[...] = z
        xb = z[:, :bw]
        xc = _conv(xb, tail_s[...], vec_ref)
        tail = xb[tm - SUBLANES :, :]
        tail_s[...] = tail
        sp = _softplus_neg(vec_ref[ROW_LAMBDA : ROW_LAMBDA + 1, :])
        for h in range(B_HEADS):
            cols = slice(h * hd, (h + 1) * hd)
            xc_h = xc[:, cols]
            _, ig, a, mult = _gates(xc_h, gab_ref, vec_ref, sp[:, cols], h, hd)
            a_s[:, cols] = a
            b_s[:, cols] = mult * (ig * xc_h)
        carry = _scan_blocks(a_s, b_s, hs_s, carry_s[...], tm, reverse=False)
        carry_s[...] = carry
        hs = hs_s[...]
        h_ref[...] = hs
        g = z[:, bw:]
        y = (hs * (g * _sigmoid(g))).astype(BF16)
        x2 = x1 + _dot(y, bout_ref[...])

        nf = nf_ref[...]
        o, xh, r = _rms_fwd(x2, nf)
        diff = o - tgt_ref[...]
        acc_s[...] += jnp.sum(diff * diff, axis=0, keepdims=True)
        do = diff * (1.0 / d)
        dx2, gnf = _rms_bwd(do, xh, r, nf)
        gnf_ref[...] += gnf
        dx2_ref[...] = dx2

        @pl.when(pl.program_id(0) == nt - 1)
        def _():
            total = jnp.sum(acc_s[...], axis=-1, keepdims=True) * (0.5 / d)
            loss_ref[...] = jnp.broadcast_to(total, loss_ref.shape)

    row = lambda i: (i, 0)
    return _pcall(
        body,
        name="layer_b_fwd",
        grid=(nt,),
        in_specs=[
            pl.BlockSpec((tm, d), row),
            _full(nw.shape),
            _full(bin_w.shape),
            _full(vec.shape),
            _full(gab.shape),
            _full(bout.shape),
            _full(nf.shape),
            pl.BlockSpec((tm, d), row),
        ],
        out_specs=[
            pl.BlockSpec((tm, 2 * bw), row),
            pl.BlockSpec((tm, bw), row),
            pl.BlockSpec((tm, d), row),
            pl.BlockSpec((tm, d), row),
            _full((1, LANES)),
            _full((1, d)),
        ],
        out_shape=[
            jax.ShapeDtypeStruct((t_rows, 2 * bw), F32),
            jax.ShapeDtypeStruct((t_rows, bw), F32),
            jax.ShapeDtypeStruct((t_rows, d), BF16),
            jax.ShapeDtypeStruct((t_rows, d), F32),
            jax.ShapeDtypeStruct((1, LANES), F32),
            jax.ShapeDtypeStruct((1, d), F32),
        ],
        scratch_shapes=[
            pltpu.VMEM((SUBLANES, bw), F32),
            pltpu.VMEM((SUBLANES, bw), F32),
            pltpu.VMEM((tm, bw), F32),
            pltpu.VMEM((tm, bw), F32),
            pltpu.VMEM((tm, bw), F32),
            pltpu.VMEM((1, d), F32),
        ],
        compiler_params=_cparams(("arbitrary",)),
    )(x1, nw, bin_w, vec, gab, bout, nf, tgt)


def _layer_b_bwd(dout, x1, z, hseq, nw, bin_w, vec, gab, gabt, bout, tm):
    t_rows, d = x1.shape
    bw = bout.shape[0]
    hd = bw // B_HEADS
    nt = t_rows // tm

    def body(
        dout_ref, x1_ref, z_ref, h_ref, xbt_ref, ht_ref, nw_ref, bin_ref, vec_ref, gab_ref, gabt_ref, bout_ref,
        dx1_ref, dz_ref, y_ref, dob_ref, ggab_ref, ggb_ref, gcw_ref, gcb_ref, glam_ref, gnw_ref,
        gcarry_s, afirst_s, head_s, aup_s, dh_s, gt_s, dxc_s, xc_s, r_s, ig_s,
    ):
        step = pl.program_id(0)
        tile = nt - 1 - step

        @pl.when(step == 0)
        def _():
            for ref in (ggab_ref, ggb_ref, gcw_ref, gcb_ref, glam_ref, gnw_ref, gcarry_s, afirst_s, head_s):
                ref[...] = jnp.zeros_like(ref)

        first_tile = tile == 0
        xb_halo = jnp.where(first_tile, 0.0, xbt_ref[...])
        h_halo = jnp.where(first_tile, 0.0, ht_ref[...])

        dout = dout_ref[...]
        dob = dout.astype(BF16)
        dob_ref[...] = dob
        dy = _dot_nt(dob, bout_ref[...])
        hs = h_ref[...]
        g = z_ref[:, bw:]
        sg, dsg = _silu_and_grad(g)
        y_ref[...] = (hs * sg).astype(BF16)
        dz_ref[:, bw:] = (dy * hs * dsg).astype(BF16)
        dh_s[...] = dy * sg

        xb = z_ref[:, :bw]
        xc = _conv(xb, xb_halo, vec_ref)
        xc_s[...] = xc
        lam = vec_ref[ROW_LAMBDA : ROW_LAMBDA + 1, :]
        sp = _softplus_neg(lam)
        for h in range(B_HEADS):
            cols = slice(h * hd, (h + 1) * hd)
            r, ig, a, _ = _gates(xc[:, cols], gab_ref, vec_ref, sp[:, cols], h, hd)
            r_s[:, cols] = r
            ig_s[:, cols] = ig
            aup_s[:, cols] = _shift_up(a, afirst_s[:, cols], 1)
            afirst_s[:, cols] = jnp.broadcast_to(a[0:1, :], (SUBLANES, hd))
        carry = _scan_blocks(aup_s, dh_s, gt_s, gcarry_s[...], tm, reverse=True)
        gcarry_s[...] = carry

        h_prev = _shift_down(hs, h_halo, 1)
        for h in range(B_HEADS):
            cols = slice(h * hd, (h + 1) * hd)
            xc_h = xc_s[:, cols]
            sp_h = sp[:, cols]
            r, ig = r_s[:, cols], ig_s[:, cols]
            a, mult = _decay(r, sp_h)
            gt = gt_s[:, cols]
            da = gt * h_prev[:, cols]
            dmult = gt * (ig * xc_h)
            dig = gt * (mult * xc_h)
            dxc_direct = gt * (mult * ig)
            dla = da * a - dmult * (a * a) / mult
            glam_ref[:, cols] += jnp.sum(dla * r, axis=0, keepdims=True)
            dr = dla * ((-RG_C) * sp_h)
            dpre = jnp.concatenate([dr * r * (1.0 - r), dig * ig * (1.0 - ig)], axis=1)
            ggb_ref[:, cols] += jnp.sum(dpre[:, :hd], axis=0, keepdims=True)
            ggb_ref[:, bw + h * hd : bw + (h + 1) * hd] += jnp.sum(dpre[:, hd:], axis=0, keepdims=True)
            dpb = dpre.astype(BF16)
            ggab_ref[h] += _dot_tn(xc_h.astype(BF16), dpb)
            dxc_s[:, cols] = dxc_direct + _dot(dpb, gabt_ref[h])
        glam_ref[...] = jnp.where(step == nt - 1, glam_ref[...] * (RG_C * _sigmoid(-lam)), glam_ref[...])

        dxc = dxc_s[...]
        gcb_ref[...] += jnp.sum(dxc, axis=0, keepdims=True)
        dxb = vec_ref[CONV_WIDTH - 1 : CONV_WIDTH, :] * dxc
        gcw_ref[CONV_WIDTH - 1 : CONV_WIDTH, :] += jnp.sum(dxc * xb, axis=0, keepdims=True)
        head = head_s[...]
        for k in range(CONV_WIDTH - 1):
            lag = CONV_WIDTH - 1 - k
            dxb = dxb + vec_ref[k : k + 1, :] * _shift_up(dxc, head, lag)
            gcw_ref[k : k + 1, :] += jnp.sum(dxc * _shift_down(xb, xb_halo, lag), axis=0, keepdims=True)
        head_s[...] = dxc[:SUBLANES, :]
        dz_ref[:, :bw] = dxb.astype(BF16)

        s_cols = 2 * bw // N_CHIPS
        dh1 = jnp.zeros((tm, d), F32)
        for k in range(N_CHIPS):
            dh1 = dh1 + _dot_nt(dz_ref[:, k * s_cols : (k + 1) * s_cols], bin_ref[k])
        x1 = x1_ref[...]
        nw = nw_ref[...]
        _, xh, r1 = _rms_fwd(x1, nw)
        dx, gnw = _rms_bwd(dh1, xh, r1, nw)
        gnw_ref[...] += gnw
        dx1_ref[...] = dout + dx

    rev = lambda i: (nt - 1 - i, 0)
    prev = lambda i: (jnp.maximum((nt - 1 - i) * (tm // SUBLANES) - 1, 0), 0)
    return _pcall(
        body,
        name="layer_b_bwd",
        grid=(nt,),
        in_specs=[
            pl.BlockSpec((tm, d), rev),
            pl.BlockSpec((tm, d), rev),
            pl.BlockSpec((tm, 2 * bw), rev),
            pl.BlockSpec((tm, bw), rev),
            pl.BlockSpec((SUBLANES, bw), prev),
            pl.BlockSpec((SUBLANES, bw), prev),
            _full(nw.shape),
            _full(bin_w.shape),
            _full(vec.shape),
            _full(gab.shape),
            _full(gabt.shape),
            _full(bout.shape),
        ],
        out_specs=[
            pl.BlockSpec((tm, d), rev),
            pl.BlockSpec((tm, 2 * bw), rev),
            pl.BlockSpec((tm, bw), rev),
            pl.BlockSpec((tm, d), rev),
            _full((B_HEADS, hd, 2 * hd)),
            _full((1, 2 * bw)),
            _full((SUBLANES, bw)),
            _full((1, bw)),
            _full((1, bw)),
            _full((1, d)),
        ],
        out_shape=[
            jax.ShapeDtypeStruct((t_rows, d), F32),
            jax.ShapeDtypeStruct((t_rows, 2 * bw), BF16),
            jax.ShapeDtypeStruct((t_rows, bw), BF16),
            jax.ShapeDtypeStruct((t_rows, d), BF16),
            jax.ShapeDtypeStruct((B_HEADS, hd, 2 * hd), F32),
            jax.ShapeDtypeStruct((1, 2 * bw), F32),
            jax.ShapeDtypeStruct((SUBLANES, bw), F32),
            jax.ShapeDtypeStruct((1, bw), F32),
            jax.ShapeDtypeStruct((1, bw), F32),
            jax.ShapeDtypeStruct((1, d), F32),
        ],
        scratch_shapes=[pltpu.VMEM((SUBLANES, bw), F32)] * 3 + [pltpu.VMEM((tm, bw), F32)] * 7,
        compiler_params=_cparams(("arbitrary",)),
    )(dout, x1, z, hseq, z, hseq, nw, bin_w, vec, gab, gabt, bout)


def _wgrad(a, b, m_blocks, n_blocks, hook=None, wire_copy=False):
    k, m = a.shape
    n = b.shape[1]
    bm, bn = m // m_blocks, n // n_blocks

    def body(a_ref, b_ref, o_ref, *wire_ref):
        prod = _dot_tn(a_ref[...], b_ref[...])
        o_ref[...] = prod
        if wire_copy:
            wire_ref[0][...] = prod.astype(BF16)

    out_spec = pl.BlockSpec((None, None, bm, bn), lambda j, i: (j, i, 0, 0))
    shape = (n_blocks, m_blocks, bm, bn)
    out = _pcall(
        body,
        hook,
        name=f"wgrad_{m}x{n}",
        grid=(n_blocks, m_blocks),
        in_specs=[pl.BlockSpec((k, bm), lambda j, i: (0, i)), pl.BlockSpec((k, bn), lambda j, i: (0, j))],
        out_specs=[out_spec] * (1 + wire_copy),
        out_shape=[jax.ShapeDtypeStruct(shape, F32)] + [jax.ShapeDtypeStruct(shape, BF16)] * wire_copy,
        compiler_params=_cparams(("arbitrary", "arbitrary")),
    )(a, b)
    outs, rode = (out, None) if hook is None else out
    outs = outs if wire_copy else outs[0]
    return outs if hook is None else (outs, rode)


def _adamw_math(w, g, m, v):
    m = ADAM_B1 * m + (1.0 - ADAM_B1) * g
    v = ADAM_B2 * v + (1.0 - ADAM_B2) * (g * g)
    m_hat = m / (1.0 - ADAM_B1**ADAM_STEP)
    v_hat = v / (1.0 - ADAM_B2**ADAM_STEP)
    delta = -ADAM_LR * (m_hat / (jnp.sqrt(v_hat) + ADAM_EPS) + ADAM_WD * w)
    return delta, m, v


ADAMW_ROW_TILES = 8


def _adamw_rows(ws, gs, ms, vs):
    n = len(ws)

    def body(*refs):
        ins, outs = refs[: 4 * n], refs[4 * n :]
        for i in range(n):
            w_ref, g_ref, m_ref, v_ref = ins[i::n]
            d_ref, mo_ref, vo_ref, go_ref = outs[i::n]
            g = g_ref[...]
            d_ref[...], mo_ref[...], vo_ref[...] = _adamw_math(w_ref[...], g, m_ref[...], v_ref[...])
            go_ref[...] = g

    specs = [pl.BlockSpec((w.shape[0] // ADAMW_ROW_TILES, w.shape[1]), lambda i: (i, 0)) for w in ws]
    outs = _pcall(
        body,
        name="adamw_rows",
        grid=(ADAMW_ROW_TILES,),
        in_specs=specs * 4,
        out_specs=specs * 4,
        out_shape=[jax.ShapeDtypeStruct(w.shape, F32) for w in ws] * 4,
        compiler_params=_cparams(("arbitrary",)),
    )(*ws, *gs, *ms, *vs)
    return [outs[k * n : (k + 1) * n] for k in range(4)]


def _adamw_many(ws, gs, ms, vs, pack):
    n = len(ws)
    packed = [i for i in range(n) if isinstance(gs[i], (int, tuple))]
    g_arrays = [g[0] if isinstance(g, tuple) else g for g in gs if not isinstance(g, int)]

    def body(*refs):
        w_refs, m_refs, v_refs = (refs[i * n : (i + 1) * n] for i in range(3))
        g_refs = list(refs[3 * n : 3 * n + len(g_arrays)])
        pack_ref = refs[3 * n + len(g_arrays)]
        outs = refs[3 * n + len(g_arrays) + 1 :]
        d_refs, mo_refs, vo_refs, go_refs = outs[:n], outs[n : 2 * n], outs[2 * n : 3 * n], list(outs[3 * n :])
        for i in range(n):
            if isinstance(gs[i], int):
                g = pack_ref[gs[i] : gs[i] + ws[i].shape[0], :]
                go_refs.pop(0)[...] = g
            elif isinstance(gs[i], tuple):
                parts_ref, row, go_ref = g_refs.pop(0), gs[i][1], go_refs.pop(0)
                total = parts_ref[0, 0:1, :]
                for k in range(1, N_DEV):
                    total = total + parts_ref[k, 0:1, :]
                go_ref[0:1, :] = total
                lane_rows = [pack_ref[row + r : row + r + 1, :] for r in range(ws[i].shape[1] // LANES)]
                go_ref[1:2, :] = jnp.concatenate(lane_rows, axis=1)
                g = go_ref[...]
            else:
                g = g_refs.pop(0)[...]
            d_refs[i][...], mo_refs[i][...], vo_refs[i][...] = _adamw_math(w_refs[i][...], g, m_refs[i][...], v_refs[i][...])

    vmem = pl.BlockSpec(memory_space=pltpu.VMEM)
    like = [jax.ShapeDtypeStruct(w.shape, F32) for w in ws]
    outs = _pcall(
        body,
        name="adamw_small",
        in_specs=[vmem] * (3 * n + len(g_arrays) + 1),
        out_specs=[vmem] * (3 * n + len(packed)),
        out_shape=like * 3 + [like[i] for i in packed],
        compiler_params=_cparams(),
    )(*ws, *ms, *vs, *g_arrays, pack)
    return outs[:n], outs[n : 2 * n], outs[2 * n : 3 * n], outs[3 * n :]


def _pack_rows(parts, lanes=LANES):
    flat = jnp.concatenate([p.reshape(-1) for p in parts])
    per = N_DEV * SUBLANES * lanes
    total = -(-flat.shape[0] // per) * per
    flat = jnp.pad(flat, (0, total - flat.shape[0]))
    return flat.reshape(N_DEV, total // (N_DEV * lanes), lanes)


def _size(shape):
    n = 1
    for dim in shape:
        n *= dim
    return n


def _unpack(flat, shapes):
    out, at = [], 0
    for s in shapes:
        out.append(flat[at : at + _size(s)].reshape(s))
        at += _size(s)
    return out


def kernel(x, norm_w, a_w_in, a_ln_w, a_ln_b, a_w_s, a_b_s, a_w_out, b_w_in, b_conv_w, b_conv_b, b_gate_a_w, b_gate_a_b, b_gate_x_w, b_gate_x_b, b_lambda, b_w_out, norm_f_w, loss_target, m_norm_w, m_a_w_in, m_a_ln_w, m_a_ln_b, m_a_w_s, m_a_b_s, m_a_w_out, m_b_w_in, m_b_conv_w, m_b_conv_b, m_b_gate_a_w, m_b_gate_a_b, m_b_gate_x_w, m_b_gate_x_b, m_b_lambda, m_b_w_out, m_norm_f_w, v_norm_w, v_a_w_in, v_a_ln_w, v_a_ln_b, v_a_w_s, v_a_b_s, v_a_w_out, v_b_w_in, v_b_conv_w, v_b_conv_b, v_b_gate_a_w, v_b_gate_a_b, v_b_gate_x_w, v_b_gate_x_b, v_b_lambda, v_b_w_out, v_norm_f_w):
    t_rows, d = x.shape[1], x.shape[2]
    aw = a_ln_w.shape[1]
    bw = b_gate_a_w.shape[1] * b_gate_a_w.shape[2]
    hd = bw // B_HEADS
    mine = 2 * lax.axis_index("x") + lax.axis_index("y")
    core = lax.axis_index("c")
    weights = dict(norm_w=norm_w, a_w_in=a_w_in, a_ln_w=a_ln_w, a_ln_b=a_ln_b, a_w_s=a_w_s, a_b_s=a_b_s, a_w_out=a_w_out, b_w_in=b_w_in, b_conv_w=b_conv_w, b_conv_b=b_conv_b, b_gate_a_w=b_gate_a_w, b_gate_a_b=b_gate_a_b, b_gate_x_w=b_gate_x_w, b_gate_x_b=b_gate_x_b, b_lambda=b_lambda, b_w_out=b_w_out, norm_f_w=norm_f_w)
    m_in = dict(norm_w=m_norm_w, a_w_in=m_a_w_in, a_ln_w=m_a_ln_w, a_ln_b=m_a_ln_b, a_w_s=m_a_w_s, a_b_s=m_a_b_s, a_w_out=m_a_w_out, b_w_in=m_b_w_in, b_conv_w=m_b_conv_w, b_conv_b=m_b_conv_b, b_gate_a_w=m_b_gate_a_w, b_gate_a_b=m_b_gate_a_b, b_gate_x_w=m_b_gate_x_w, b_gate_x_b=m_b_gate_x_b, b_lambda=m_b_lambda, b_w_out=m_b_w_out, norm_f_w=m_norm_f_w)
    v_in = dict(norm_w=v_norm_w, a_w_in=v_a_w_in, a_ln_w=v_a_ln_w, a_ln_b=v_a_ln_b, a_w_s=v_a_w_s, a_b_s=v_a_b_s, a_w_out=v_a_w_out, b_w_in=v_b_w_in, b_conv_w=v_b_conv_w, b_conv_b=v_b_conv_b, b_gate_a_w=v_b_gate_a_w, b_gate_a_b=v_b_gate_a_b, b_gate_x_w=v_b_gate_x_w, b_gate_x_b=v_b_gate_x_b, b_lambda=v_b_lambda, b_w_out=v_b_w_out, norm_f_w=v_norm_f_w)

    table = _index_table()
    win_l, wout_l = _cast_to_segments([a_w_in[0], a_w_out[0]], table, 4)
    small_l = jnp.concatenate([b_conv_w[0], b_conv_b, b_gate_a_b, b_gate_x_b, b_lambda], axis=0)
    (bin_l, bout_l, wc, wct, gab, gabt), (win_g, wout_g, small_g) = _cast_to_segments(
        [b_w_in[0], b_w_out[0]], table, 8, _gather_hook([win_l, wout_l], small_l),
        (a_w_s[0], b_gate_a_w[0], b_gate_x_w[0]),
    )
    win = win_g.reshape(N_CHIPS, d, -1)
    wout = wout_g.reshape(aw, d)
    bs_t = a_b_s[0].T
    nw0, nw1, nf = norm_w[0:1], norm_w[1:2], norm_f_w.reshape(1, d)

    x0 = x[0]
    (z_a, x1, h0), (bin_g, bout_g) = _layer_a_fwd(
        x0, nw0, win, a_ln_w, a_ln_b, wc, bs_t, wout, TM_FWD, _gather_hook([bin_l, bout_l])
    )
    bin_w = bin_g.reshape(N_CHIPS, d, -1)
    bout = bout_g.reshape(bw, d)
    vec = jnp.transpose(small_g, (1, 0, 2)).reshape(SUBLANES, bw)
    z_b, hseq, h1, dx2, loss_l, g_nf = _layer_b_fwd(x1, nw1, bin_w, vec, gab, bout, nf, loss_target[0], TM_FWD)
    dx1, dz_b, y_b, dob_b, g_gab, g_gb, g_cw, g_cb, g_lam, g_nw1 = _layer_b_bwd(
        dx2, x1, z_b, hseq, nw1, bin_w, vec, gab, gabt, bout, TM_FWD
    )
    seg = lambda g: g.reshape(N_DEV, -1, g.shape[3])
    sums = lambda jobs, name: _sum_jobs(jobs, name, table)
    own_half, for_neighbour = _own_half_job, _for_neighbour_job
    received = lambda buf, got, got1, got2: _received_job(buf, got, got1, got2, AT_CORE, 2)

    g_o = seg(_wgrad(y_b, dob_b, 2, 1))
    g_i, (swap_o,) = _wgrad(h1, dz_b, 1, N_CHIPS, _swap_hook([g_o]))
    g_i = seg(g_i)
    (part_o,) = sums([own_half(g_o, swap_o, BF16)], "add_own_half_o")
    a_args = (z_a, a_ln_w, a_ln_b, wc, wct, bs_t, wout)
    half = t_rows // TM_A_BWD // 2
    first, (swap_i, got1_o) = _layer_a_bwd(
        dx1, *a_args, (0, half), None, _join_hooks(_swap_hook([g_i]), _send_first_hook([part_o]))
    )
    part_i, mid_o = sums(
        [own_half(g_i, swap_i, BF16), for_neighbour(g_o, swap_o, got1_o, BF16)], "add_own_half_i_for_neighbour_o"
    )
    second, (got1_i, got2_o) = _layer_a_bwd(
        dx1, *a_args, (half, 2 * half), first[:3], _join_hooks(_send_first_hook([part_i]), _send_second_hook([mid_o]))
    )
    dz_a, y_a, dob_a = second[:3]
    g_ws, g_bst, g_lnw, g_lnb = (p + q for p, q in zip(first[3:], second[3:]))
    mid_i, red_o = sums(
        [for_neighbour(g_i, swap_i, got1_i, BF16), received(g_o, swap_o, got1_o, got2_o)],
        "add_for_neighbour_i_received_o",
    )
    small_shapes = [
        (1, d), (1, aw), (1, aw), (A_GROUPS, CHUNK, CHUNK), (A_GROUPS, CHUNK), (B_HEADS, hd, hd), (B_HEADS, hd, hd),
        (d,), (CONV_WIDTH, bw), (1, bw), (1, bw), (1, bw), (1, bw), (1, 1),
    ]
    small = _pack_rows(
        [
            g_nw1, g_lnw, g_lnb, g_ws, g_bst.T, g_gab[:, :, :hd], g_gab[:, :, hd:],
            g_nf, g_cw[:CONV_WIDTH], g_cb, g_gb[:, :bw], g_gb[:, bw:], g_lam, loss_l[:, :1],
        ]
    )
    (g_w, g_w_wire), (got2_i, gr_bout, swap_s) = _wgrad(
        h0, dz_a, 1, N_CHIPS, _join_hooks(_send_second_hook([mid_i]), _share_hook([red_o]), _swap_hook([small])),
        wire_copy=True,
    )
    g_w, g_w_wire = seg(g_w), seg(g_w_wire)
    red_i, part_s = sums([received(g_i, swap_i, got1_i, got2_i), own_half(small, swap_s, F32)], "add_received_i")
    g_u, (swap_w, got1_s) = _wgrad(
        y_a, dob_a, N_CHIPS, 1, _join_hooks(_swap_hook([g_w_wire]), _send_first_hook([part_s]))
    )
    g_u = seg(g_u)

    part_w, mid_s = sums(
        [own_half(g_w, swap_w, BF16), for_neighbour(small, swap_s, got1_s, F32)], "add_own_half_w"
    )
    (grad_x, g_nw0_mine), (got1_w, got2_s, swap_u, gr_bin) = _layer_a_bwd_dx(
        dx1, x0, dz_a, nw0, win, TM_A_DX, (0, t_rows // TM_A_DX), None,
        _join_hooks(_send_first_hook([part_w]), _send_second_hook([mid_s]), _swap_hook([g_u]), _share_hook([red_i])),
    )
    mid_w, red_s, part_u = sums(
        [
            for_neighbour(g_w, swap_w, got1_w, BF16),
            _received_job(small, swap_s, got1_s, got2_s, AT_DEVICE, N_DEV),
            own_half(g_u, swap_u, BF16),
        ],
        "add_for_neighbour_w_own_half_u",
    )
    got2_w, got1_u, small_r, g_nw0_all = _run_hook(
        _join_hooks(_send_second_hook([mid_w]), _send_first_hook([part_u]), _share_hook([], red_s, g_nw0_mine)),
        "second_axis_and_gather",
    )
    red_w, mid_u = sums(
        [received(g_w, swap_w, got1_w, got2_w), for_neighbour(g_u, swap_u, got1_u, BF16)],
        "add_received_w_for_neighbour_u",
    )
    got2_u, gr_win = _run_hook(
        _join_hooks(_send_second_hook([mid_u]), _share_hook([red_w])), "second_axis_and_share"
    )
    (red_u,) = sums([received(g_u, swap_u, got1_u, got2_u)], "add_received_u")
    (gr_wout,) = _run_hook(_share_hook([red_u]), "share_reduced")
    (_, g_a_ln_w, g_a_ln_b, _, g_a_b_s, _, _, g_norm_f, gf_cw, gf_cb, gf_gab, gf_gxb, gf_lam, loss) = _unpack(
        small_r.reshape(-1), small_shapes
    )
    shard = lambda g: lax.dynamic_slice_in_dim(g, mine * (bw // N_CHIPS), bw // N_CHIPS, axis=1)

    grads = {
        "a_w_in": gr_win, "a_ln_w": g_a_ln_w, "a_ln_b": g_a_ln_b,
        "a_b_s": g_a_b_s[None], "a_w_out": gr_wout, "b_w_in": gr_bin, "b_conv_w": shard(gf_cw)[None],
        "b_conv_b": shard(gf_cb), "b_gate_a_b": shard(gf_gab),
        "b_gate_x_b": shard(gf_gxb), "b_lambda": shard(gf_lam),
        "b_w_out": gr_bout, "norm_f_w": g_norm_f,
    }
    pack_names = [
        "norm_w", "a_ln_w", "a_ln_b", "a_w_s", "a_b_s", "b_gate_a_w", "b_gate_x_w", "norm_f_w", "b_conv_w", "b_conv_b",
        "b_gate_a_b", "b_gate_x_b", "b_lambda", "loss",
    ]
    row_of, at = {}, 0
    for n, shape in zip(pack_names, small_shapes):
        row_of[n] = at // LANES
        at += _size(shape)
    names = list(weights)
    big_names = ["a_w_in", "a_w_out", "b_w_in", "b_w_out"]
    small_names = [n for n in names if n not in big_names]
    from_pack = [n for n in small_names if n in ("a_w_s", "b_gate_a_w", "b_gate_x_w")]
    as_rows = lambda n, a: a.reshape(-1, LANES) if n in from_pack else a.reshape(1, -1) if a.ndim == 1 else a
    small_grads = {n: row_of[n] for n in from_pack}
    small_grads["norm_w"] = (g_nw0_all, row_of["norm_w"])
    *small_out, pack_grads = _adamw_many(
        [as_rows(n, weights[n]) for n in small_names],
        [small_grads[n] if n in small_grads else as_rows(n, grads[n]) for n in small_names],
        [as_rows(n, m_in[n]) for n in small_names],
        [as_rows(n, v_in[n]) for n in small_names],
        small_r.reshape(-1, LANES),
    )
    grads.update(zip([n for n in small_names if n in small_grads], pack_grads))
    *big_out, big_grads = _adamw_rows(
        *[[src[n].reshape(weights[n].shape[-2:]) for n in big_names] for src in (weights, grads, m_in, v_in)]
    )
    grads.update(zip(big_names, big_grads))
    delta, new_m, new_v = {}, {}, {}
    for dst, small_vals, big_vals in zip((delta, new_m, new_v), small_out, big_out):
        dst.update(zip(small_names, small_vals))
        dst.update(zip(big_names, big_vals))
    for dst in (grads, delta, new_m, new_v):
        for n in names:
            dst[n] = dst[n].reshape(weights[n].shape)

    return (
        loss.reshape(()),
        grad_x[None],
        *[grads[n] for n in names],
        *[delta[n] for n in names],
        *[new_m[n] for n in names],
        *[new_v[n] for n in names],
    )
```

```python
import jax
import jax.numpy as jnp
from jax import lax
from jax.experimental import pallas as pl
from jax.experimental.pallas import tpu as pltpu

F32 = jnp.float32
BF16 = jnp.bfloat16

RMS_EPS = 1e-6
LN_EPS = 1e-5
RG_C = 8.0
CHUNK = 128
A_GROUPS = 8
B_HEADS = 12
CONV_WIDTH = 4

ADAM_LR = 0.001
ADAM_B1 = 0.9
ADAM_B2 = 0.999
ADAM_EPS = 1e-08
ADAM_WD = 0.01
ADAM_STEP = 10

N_CHIPS = 4
N_DEV = 8
SUBLANES = 8
LANES = 128
V7X_VMEM_BYTES = 64 * 1024 * 1024
VMEM_LIMIT = V7X_VMEM_BYTES * 7 // 8
MESH = pl.DeviceIdType.MESH
ANY = pl.BlockSpec(memory_space=pl.ANY)

TM_FWD = 256
TM_A_BWD = 256
TM_A_DX = 512

GELU_C0 = 0.7978845608028654
GELU_C1 = 0.044715


class _Hook:
    def __init__(self, operands, out_shapes, aliases, n_sems, start, finish, middle=None, late=None):
        self.operands, self.out_shapes, self.aliases, self.n_sems = operands, out_shapes, aliases, n_sems
        self.start, self.finish, self.middle, self.late = start, finish, middle, late


class _SemView:
    def __init__(self, base, off):
        self.base, self.off = base, off

    @property
    def at(self):
        return self

    def __getitem__(self, k):
        return self.base.at[self.off + k]


def _join_hooks(*hooks):
    if len(hooks) == 1:
        return hooks[0]
    operands, out_shapes, aliases, spans = [], [], {}, []
    n_sems = 0
    for h in hooks:
        aliases.update({len(operands) + i: len(out_shapes) + o for i, o in h.aliases.items()})
        spans.append((len(operands), len(h.operands), len(out_shapes), len(h.out_shapes), n_sems))
        operands += list(h.operands)
        out_shapes += list(h.out_shapes)
        n_sems += h.n_sems

    def each(which):
        def run(ins, outs, send, recv):
            for h, (i0, ni, o0, no, s0) in zip(hooks, spans):
                step = getattr(h, which)
                if step is not None:
                    step(ins[i0 : i0 + ni], outs[o0 : o0 + no], _SemView(send, s0), _SemView(recv, s0))

        return run

    middle = each("middle") if any(h.middle is not None for h in hooks) else None
    late = each("late") if any(h.late is not None for h in hooks) else None
    return _Hook(operands, out_shapes, aliases, n_sems, each("start"), each("finish"), middle, late)


def _pcall(body, hook=None, **kw):
    if hook is None:
        return pl.pallas_call(body, **kw)
    n_pre = 0
    if "grid_spec" in kw:
        spec = kw.pop("grid_spec")
        n_pre = spec.num_scalar_prefetch
        kw.update(
            grid=tuple(spec.grid), in_specs=list(spec.in_specs), out_specs=list(spec.out_specs),
            scratch_shapes=list(spec.scratch_shapes),
        )
    n_in, n_out = len(kw["in_specs"]), len(kw["out_shape"])
    hi, ho = len(hook.operands), len(hook.out_shapes)
    grid = kw.get("grid", ())

    def wrapped(*refs):
        pre, refs = refs[:n_pre], refs[n_pre:]
        ins, h_in = refs[:n_in], refs[n_in : n_in + hi]
        outs = refs[n_in + hi : n_in + hi + n_out]
        h_out = refs[n_in + hi + n_out : n_in + hi + n_out + ho]
        scratch = refs[n_in + hi + n_out + ho : -2]
        send_sems, recv_sems = refs[-2:]
        if not grid:
            hook.start(h_in, h_out, send_sems, recv_sems)
            if hook.middle is not None:
                hook.middle(h_in, h_out, send_sems, recv_sems)
            body(*pre, *ins, *outs, *scratch)
            if hook.late is not None:
                hook.late(h_in, h_out, send_sems, recv_sems)
            hook.finish(h_in, h_out, send_sems, recv_sems)
            return
        first = pl.program_id(0) == 0
        last = pl.program_id(0) == grid[0] - 1
        for axis in range(1, len(grid)):
            first = jnp.logical_and(first, pl.program_id(axis) == 0)
            last = jnp.logical_and(last, pl.program_id(axis) == grid[axis] - 1)

        @pl.when(first)
        def _():
            hook.start(h_in, h_out, send_sems, recv_sems)

        for when, step in ((hook.middle, grid[0] // 4), (hook.late, grid[0] - 1)):
            if when is not None:
                assert len(grid) == 1 and grid[0] >= 4

                @pl.when(pl.program_id(0) == step)
                def _(when=when):
                    when(h_in, h_out, send_sems, recv_sems)

        body(*pre, *ins, *outs, *scratch)

        @pl.when(last)
        def _():
            hook.finish(h_in, h_out, send_sems, recv_sems)

    aliases = dict(kw.pop("input_output_aliases", {}))
    aliases.update({n_pre + n_in + i: n_out + o for i, o in hook.aliases.items()})
    kw.update(
        in_specs=list(kw["in_specs"]) + [ANY] * hi,
        out_specs=list(kw["out_specs"]) + [ANY] * ho,
        out_shape=list(kw["out_shape"]) + list(hook.out_shapes),
        scratch_shapes=list(kw.get("scratch_shapes", ()))
        + [pltpu.SemaphoreType.DMA((hook.n_sems,)), pltpu.SemaphoreType.DMA((hook.n_sems,))],
        input_output_aliases=aliases,
    )
    if n_pre:
        kw["grid_spec"] = pltpu.PrefetchScalarGridSpec(
            num_scalar_prefetch=n_pre, grid=kw.pop("grid"), in_specs=kw.pop("in_specs"),
            out_specs=kw.pop("out_specs"), scratch_shapes=kw.pop("scratch_shapes"),
        )
    call = pl.pallas_call(wrapped, **kw)

    def run(*operands):
        outs = call(*operands, *hook.operands)
        return outs[:n_out], outs[n_out:]

    return run


def _run_hook(hook, name):
    def body():
        pass

    return _pcall(body, hook, name=name, in_specs=[], out_specs=[], out_shape=[])()[1]


def _cparams(sem=None):
    return pltpu.CompilerParams(dimension_semantics=sem, vmem_limit_bytes=VMEM_LIMIT)


def _full(shape):
    zeros = (0,) * len(shape)
    return pl.BlockSpec(shape, lambda *_: zeros)


def _scalars(*vals):
    return jnp.stack([jnp.asarray(v, jnp.int32) for v in vals])


def _sigmoid(x):
    return 1.0 / (1.0 + jnp.exp(-x))


def _gelu(x):
    t = jnp.tanh(GELU_C0 * (x + GELU_C1 * (x * x * x)))
    return x * (0.5 * (1.0 + t))


def _gelu_and_grad(x):
    x2 = x * x
    t = jnp.tanh(GELU_C0 * (x + GELU_C1 * (x2 * x)))
    cdf = 0.5 * (1.0 + t)
    return x * cdf, cdf + 0.5 * x * (1.0 - t * t) * (GELU_C0 * (1.0 + 3.0 * GELU_C1 * x2))


def _silu_and_grad(x):
    s = _sigmoid(x)
    return x * s, s * (1.0 + x * (1.0 - s))


def _softplus_neg(lam):
    u = jnp.exp(-jnp.abs(lam))
    w = 1.0 + u
    log1p = jnp.where(w == 1.0, u, jnp.log(w) * (u / jnp.where(w == 1.0, 1.0, w - 1.0)))
    return jnp.maximum(-lam, 0.0) + log1p


def _dot(a, b):
    return jnp.dot(a, b, preferred_element_type=F32)


def _dot_nt(a, b):
    return lax.dot_general(a, b, (((1,), (1,)), ((), ())), preferred_element_type=F32)


def _dot_tn(a, b):
    return lax.dot_general(a, b, (((0,), (0,)), ((), ())), preferred_element_type=F32)


def _shift_down(v, halo, k):
    if k == 0:
        return v
    rolled = pltpu.roll(v, k, 0)
    row = lax.broadcasted_iota(jnp.int32, (SUBLANES, v.shape[1]), 0)
    top = jnp.where(row < k, pltpu.roll(halo, k, 0), rolled[:SUBLANES])
    return jnp.concatenate([top, rolled[SUBLANES:]], axis=0)


def _shift_up(v, head, k):
    if k == 0:
        return v
    n = v.shape[0]
    rolled = pltpu.roll(v, n - k, 0)
    row = lax.broadcasted_iota(jnp.int32, (SUBLANES, v.shape[1]), 0)
    bot = jnp.where(row >= SUBLANES - k, pltpu.roll(head, SUBLANES - k, 0), rolled[n - SUBLANES :])
    return jnp.concatenate([rolled[: n - SUBLANES], bot], axis=0)


def _scan_blocks(a_ref, b_ref, out_ref, carry, n_rows, reverse):
    width = a_ref.shape[1]
    row = lax.broadcasted_iota(jnp.int32, (SUBLANES, width), 0)
    n_blocks = n_rows // SUBLANES

    def block(j, carry):
        i = (n_blocks - 1 - j) if reverse else j
        r0 = pl.multiple_of(i * SUBLANES, SUBLANES)
        a = a_ref[pl.ds(r0, SUBLANES), :]
        b = b_ref[pl.ds(r0, SUBLANES), :]
        for d in (1, 2, 4):
            shift = (SUBLANES - d) if reverse else d
            keep = (row < SUBLANES - d) if reverse else (row >= d)
            a_s = pltpu.roll(a, shift, 0)
            b_s = pltpu.roll(b, shift, 0)
            b = jnp.where(keep, a * b_s + b, b)
            a = jnp.where(keep, a * a_s, a)
        h = a * carry + b
        out_ref[pl.ds(r0, SUBLANES), :] = h
        edge = h[0:1, :] if reverse else h[SUBLANES - 1 : SUBLANES, :]
        return jnp.broadcast_to(edge, (SUBLANES, width))

    return lax.fori_loop(0, n_blocks, block, carry)


def _rms_fwd(x, w):
    r = lax.rsqrt(jnp.mean(x * x, axis=-1, keepdims=True) + RMS_EPS)
    xh = x * r
    return xh * w, xh, r


def _rms_bwd(dh, xh, r, w):
    dxh = dh * w
    dx = r * (dxh - xh * jnp.mean(dxh * xh, axis=-1, keepdims=True))
    return dx, jnp.sum(dh * xh, axis=0, keepdims=True)


def _cast_to_segments(ws, table, steps, hook=None, small_maps=None):
    per = steps // 2
    n = len(ws)
    maps = () if small_maps is None else small_maps

    def body(k_ref, *refs):
        for w_ref, o_ref in zip(refs[:n], refs[n + len(maps) : 2 * n + len(maps)]):
            o_ref[...] = w_ref[...].astype(BF16)
        if small_maps is not None:
            pl.when(pl.program_id(0) == 0)(lambda: _prepare_small_maps(*refs[n : n + 3], *refs[2 * n + 3 :]))

    rows = [w.shape[0] // steps for w in ws]
    segment = lambda i, k_ref: (2 * k_ref[AT_MINE] + i // per, i % per, 0)
    whole = lambda shape: pl.BlockSpec(shape, lambda i, k_ref: (0,) * len(shape))
    prepared = []
    if small_maps is not None:
        (g, ck, _), (h, hd, _) = maps[0].shape, maps[1].shape
        prepared = [(g, ck, ck), (g, ck, ck), (h, hd, 2 * hd), (h, 2 * hd, hd)]
    out = _pcall(
        body,
        hook,
        name=f"cast_{ws[0].shape[0]}x{ws[0].shape[1]}",
        grid_spec=pltpu.PrefetchScalarGridSpec(
            num_scalar_prefetch=1,
            grid=(steps,),
            in_specs=[pl.BlockSpec((r, w.shape[1]), lambda i, k_ref: (i, 0)) for w, r in zip(ws, rows)]
            + [whole(m.shape) for m in maps],
            out_specs=[pl.BlockSpec((None, r, w.shape[1]), segment) for w, r in zip(ws, rows)]
            + [whole(shape) for shape in prepared],
        ),
        out_shape=[jax.ShapeDtypeStruct((N_DEV, w.shape[0] // 2, w.shape[1]), BF16) for w in ws]
        + [jax.ShapeDtypeStruct(shape, BF16) for shape in prepared],
        compiler_params=_cparams(("arbitrary",)),
    )(table, *ws, *maps)
    return out


def _prepare_small_maps(ws_ref, ga_ref, gx_ref, wc_ref, wct_ref, gab_ref, gabt_ref):
    ck, hd = ws_ref.shape[1], ga_ref.shape[1]
    tril = lax.broadcasted_iota(jnp.int32, (ck, ck), 0) >= lax.broadcasted_iota(jnp.int32, (ck, ck), 1)
    for g in range(ws_ref.shape[0]):
        w = ws_ref[g] * tril.astype(F32)
        wc_ref[g] = w.astype(BF16)
        wct_ref[g] = w.T.astype(BF16)
    for h in range(ga_ref.shape[0]):
        for k, m_ref in enumerate((ga_ref, gx_ref)):
            m = m_ref[h]
            gab_ref[h, :, k * hd : (k + 1) * hd] = m.astype(BF16)
            gabt_ref[h, k * hd : (k + 1) * hd, :] = m.T.astype(BF16)


def _place():
    x, y, c = lax.axis_index("x"), lax.axis_index("y"), lax.axis_index("c")
    chips = [(1 - x, y), (x, 1 - y), (1 - x, 1 - y)]
    return x, y, c, chips


def _chip_no(chip):
    return 2 * chip[0] + chip[1]


def _rcopy(src, dst, send_sem, recv_sem, to):
    return pltpu.make_async_remote_copy(
        src_ref=src, dst_ref=dst, send_sem=send_sem, recv_sem=recv_sem, device_id=to, device_id_type=MESH
    )


def _gather_hook(big, small=None):
    nb = len(big)
    n_sems = 6 * nb + 4

    def places():
        x, y, c, chips = _place()
        first = (x ^ (1 - c), y ^ c)
        second = (x ^ c, y ^ (1 - c))
        return x, y, c, chips, first, second, (1 - x, 1 - y)

    def seg(outs, b, chip, half):
        return outs[b].at[2 * _chip_no(chip) + half]

    def step1(outs, send, recv):
        x, y, c, _, first, _, _ = places()
        return [
            _rcopy(seg(outs, b, (x, y), c), seg(outs, b, (x, y), c), send.at[6 * b], recv.at[6 * b], (*first, c))
            for b in range(nb)
        ]

    def step2(outs, send, recv):
        x, y, c, _, first, second, _ = places()
        copies = []
        for b in range(nb):
            for k, chip in ((1, (x, y)), (2, first)):
                src = seg(outs, b, chip, c)
                copies.append(_rcopy(src, src, send.at[6 * b + k], recv.at[6 * b + k], (*second, c)))
        return copies

    def hand_over(outs, send, recv, k, chip):
        x, y, c, *_ = places()
        return [
            _rcopy(seg(outs, b, chip, c), seg(outs, b, chip, c), send.at[6 * b + k], recv.at[6 * b + k], (x, y, 1 - c))
            for b in range(nb)
        ]

    def wait_landed(outs, send, recv, k, chip, half):
        x, y, c, *_ = places()
        for b in range(nb):
            got = seg(outs, b, chip, half)
            _rcopy(got, got, send.at[6 * b + k], recv.at[6 * b + k], (x, y, c)).wait_recv()

    def small_copies(ins, outs, send, recv):
        x, y, c, chips, *_ = places()
        there = outs[nb].at[_chip_no((x, y))]
        return [
            _rcopy(ins[nb], there, send.at[6 * nb + j], recv.at[6 * nb + j], (*chip, c)) for j, chip in enumerate(chips)
        ]

    def local_copy(ins, outs, send):
        x, y, _, _ = _place()
        return pltpu.make_async_copy(ins[nb], outs[nb].at[_chip_no((x, y))], send.at[6 * nb + 3])

    def start(ins, outs, send, recv):
        for cp in step1(outs, send, recv):
            cp.start()
        if small is not None:
            for cp in small_copies(ins, outs, send, recv):
                cp.start()
            local_copy(ins, outs, send).start()

    def middle(ins, outs, send, recv):
        *_, first, _, _ = places()
        wait_landed(outs, send, recv, 0, first, places()[2])
        for cp in step2(outs, send, recv) + hand_over(outs, send, recv, 3, first):
            cp.start()

    def late(ins, outs, send, recv):
        x, y, c, chips, first, second, diagonal = places()
        for k, chip in ((1, second), (2, diagonal)):
            wait_landed(outs, send, recv, k, chip, c)
            for cp in hand_over(outs, send, recv, 3 + k, chip):
                cp.start()

    def finish(ins, outs, send, recv):
        x, y, c, chips, first, second, diagonal = places()
        wait_landed(outs, send, recv, 3, second, 1 - c)
        wait_landed(outs, send, recv, 4, first, 1 - c)
        wait_landed(outs, send, recv, 5, diagonal, 1 - c)
        sent = step1(outs, send, recv) + step2(outs, send, recv)
        for k, chip in ((3, first), (4, second), (5, diagonal)):
            sent += hand_over(outs, send, recv, k, chip)
        for cp in sent:
            cp.wait_send()
        if small is not None:
            for j, chip in enumerate(chips):
                got = outs[nb].at[_chip_no(chip)]
                _rcopy(got, got, send.at[6 * nb + j], recv.at[6 * nb + j], (x, y, c)).wait_recv()
            for cp in small_copies(ins, outs, send, recv):
                cp.wait_send()
            local_copy(ins, outs, send).wait()

    operands = list(big) + ([small] if small is not None else [])
    out_shapes = [jax.ShapeDtypeStruct(b.shape, b.dtype) for b in big]
    if small is not None:
        out_shapes.append(jax.ShapeDtypeStruct((N_CHIPS, *small.shape), small.dtype))
    return _Hook(operands, out_shapes, {b: b for b in range(nb)}, n_sems, start, finish, middle, late)


def _both_ways_hook(operands, out_shapes, copies_of, n_sems):
    def start(ins, outs, send, recv):
        for cp in copies_of(ins, outs, send, recv):
            cp.start()

    def finish(ins, outs, send, recv):
        for cp in copies_of(ins, outs, send, recv):
            cp.wait()

    return _Hook(operands, out_shapes, {}, n_sems, start, finish)


def _swap_hook(bufs):
    def copies_of(ins, outs, send, recv):
        x, y, c, _ = _place()
        copies = []
        for b in range(len(bufs)):
            for j in range(N_CHIPS):
                k = b * N_CHIPS + j
                copies.append(_rcopy(ins[b].at[2 * j + 1 - c], outs[b].at[j], send.at[k], recv.at[k], (x, y, 1 - c)))
        return copies

    out_shapes = [jax.ShapeDtypeStruct((N_CHIPS, *b.shape[1:]), b.dtype) for b in bufs]
    return _both_ways_hook(list(bufs), out_shapes, copies_of, len(bufs) * N_CHIPS)


def _axis_order():
    x, y, c, _ = _place()
    return (x, y), c, (x ^ (1 - c), y ^ c), (x ^ c, y ^ (1 - c)), (1 - x, 1 - y)


def _send_first_hook(parts):
    def copies_of(ins, outs, send, recv):
        _, c, first, _, _ = _axis_order()
        copies = []
        for b in range(len(parts)):
            for k in range(2):
                sem = 2 * b + k
                copies.append(_rcopy(ins[b].at[k], outs[b].at[k], send.at[sem], recv.at[sem], (*first, c)))
        return copies

    out_shapes = [jax.ShapeDtypeStruct((2, *p.shape[1:]), p.dtype) for p in parts]
    return _both_ways_hook(list(parts), out_shapes, copies_of, len(parts) * 2)


def _send_second_hook(mids):
    def copies_of(ins, outs, send, recv):
        _, c, _, second, _ = _axis_order()
        return [_rcopy(ins[b], outs[b], send.at[b], recv.at[b], (*second, c)) for b in range(len(mids))]

    out_shapes = [jax.ShapeDtypeStruct(m.shape, m.dtype) for m in mids]
    return _both_ways_hook(list(mids), out_shapes, copies_of, len(mids))


def _share_hook(big, small=None, tiny=None):
    nb = len(big)
    n_sems = nb + 7 + N_DEV
    t0 = nb + 7

    def tiny_copies(ins, outs, send, recv):
        x, y, c, _ = _place()
        there = outs[-1].at[2 * _chip_no((x, y)) + c]
        copies = []
        for r in range(1, N_DEV):
            to = (x ^ (r >> 2 & 1), y ^ (r >> 1 & 1), c ^ (r & 1))
            copies.append(_rcopy(ins[-1], there, send.at[t0 + r], recv.at[t0 + r], to))
        return copies

    def tiny_local(ins, outs, send):
        x, y, c, _ = _place()
        return pltpu.make_async_copy(ins[-1], outs[-1].at[2 * _chip_no((x, y)) + c], send.at[t0])

    def first_copies(outs, send, recv):
        x, y, c, chips = _place()
        sibling = (x, y, 1 - c)
        copies = [_rcopy(outs[b].at[c], outs[b].at[c], send.at[b], recv.at[b], sibling) for b in range(nb)]
        if small is not None:
            own = outs[nb].at[2 * _chip_no((x, y)) + c]
            copies.append(_rcopy(own, own, send.at[nb], recv.at[nb], sibling))
            for j, chip in enumerate(chips):
                copies.append(_rcopy(own, own, send.at[nb + 1 + j], recv.at[nb + 1 + j], (*chip, c)))
        return copies

    def start(ins, outs, send, recv):
        for cp in first_copies(outs, send, recv):
            cp.start()
        if tiny is not None:
            for cp in tiny_copies(ins, outs, send, recv):
                cp.start()
            tiny_local(ins, outs, send).start()

    def finish(ins, outs, send, recv):
        x, y, c, chips = _place()
        me, sibling = (x, y, c), (x, y, 1 - c)
        if tiny is not None:
            for cp in tiny_copies(ins, outs, send, recv):
                cp.wait()
            tiny_local(ins, outs, send).wait()
        passed = []
        if small is not None:
            for j, chip in enumerate(chips):
                got = outs[nb].at[2 * _chip_no(chip) + c]
                _rcopy(got, got, send.at[nb + 1 + j], recv.at[nb + 1 + j], me).wait_recv()
                fwd = _rcopy(got, got, send.at[nb + 4 + j], recv.at[nb + 4 + j], sibling)
                fwd.start()
                passed.append(fwd)
        for b in range(nb):
            got = outs[b].at[1 - c]
            _rcopy(got, got, send.at[b], recv.at[b], me).wait_recv()
        if small is not None:
            got = outs[nb].at[2 * _chip_no((x, y)) + 1 - c]
            _rcopy(got, got, send.at[nb], recv.at[nb], me).wait_recv()
            for j, chip in enumerate(chips):
                got = outs[nb].at[2 * _chip_no(chip) + 1 - c]
                _rcopy(got, got, send.at[nb + 4 + j], recv.at[nb + 4 + j], me).wait_recv()
        for cp in first_copies(outs, send, recv) + passed:
            cp.wait_send()

    operands = list(big) + ([small] if small is not None else [])
    out_shapes = [jax.ShapeDtypeStruct(a.shape, a.dtype) for a in operands]
    aliases = {i: i for i in range(len(operands))}
    if tiny is not None:
        operands.append(tiny)
        out_shapes.append(jax.ShapeDtypeStruct((N_DEV, *tiny.shape), tiny.dtype))
    return _Hook(operands, out_shapes, aliases, n_sems, start, finish)


SUM_BLOCK_BYTES = 2 * 1024 * 1024


def _row_tile(rows, cols):
    best = SUBLANES
    for t in range(SUBLANES, rows + 1, SUBLANES):
        if rows % t == 0 and t * cols * 4 <= SUM_BLOCK_BYTES:
            best = t
    return best


def _halves(buf):
    return buf.reshape(N_CHIPS, 2, *buf.shape[1:])


AT_ZERO, AT_ONE, AT_FIRST, AT_SECOND, AT_DIAGONAL, AT_MINE, AT_CORE, AT_DEVICE = range(8)


def _index_table():
    x, y, c = lax.axis_index("x"), lax.axis_index("y"), lax.axis_index("c")
    first, second = (x ^ (1 - c), y ^ c), (x ^ c, y ^ (1 - c))
    mine = _chip_no((x, y))
    return _scalars(0, 1, _chip_no(first), _chip_no(second), N_CHIPS - 1 - mine, mine, c, 2 * mine + c)


def _sum_jobs(jobs, name, table):
    main, riders = jobs[0], jobs[1:]
    rows, cols = main[0][0][0][0].shape[-2:]
    tr = _row_tile(rows, cols)
    n_main = len(main[0][0])
    in_specs, operands, out_specs, out_shape = [], [], [], []

    def walked(leads):
        base = leads[0]
        strides = [b - a for a, b in zip(leads[0], leads[1])] if len(leads) > 1 else [0] * len(base)
        assert all(lead == tuple(a + g * s for a, s in zip(base, strides)) for g, lead in enumerate(leads))
        return lambda j, r, s_ref: (*[s_ref[a + j * s] for a, s in zip(base, strides)], r, 0)

    for t, (arr, lead) in enumerate(main[0][0]):
        operands.append(arr)
        in_specs.append(pl.BlockSpec((None,) * len(lead) + (tr, cols), walked([group[t][1] for group in main[0]])))
    out_specs.append(pl.BlockSpec((None,) * len(main[1]) + (tr, cols), walked(main[2])))
    out_shape.append(jax.ShapeDtypeStruct((*main[1], rows, cols), main[3]))

    whole = lambda lead: lambda j, r, s_ref: (*[s_ref[at] for at in lead], 0, 0)
    for groups, out_dims, out_leads, dtype in riders:
        r_k, c_k = groups[0][0][0].shape[-2:]
        for group in groups:
            for arr, lead in group:
                operands.append(arr)
                in_specs.append(pl.BlockSpec((None,) * len(lead) + (r_k, c_k), whole(lead)))
        if len(groups) == 1:
            out_specs.append(pl.BlockSpec((None,) * len(out_dims) + (r_k, c_k), whole(out_leads[0])))
        else:
            assert out_dims == (len(groups),) and list(out_leads) == [(AT_ZERO + g,) for g in range(len(groups))]
            out_specs.append(pl.BlockSpec((len(groups), r_k, c_k), lambda j, r, s_ref: (0, 0, 0)))
        out_shape.append(jax.ShapeDtypeStruct((*out_dims, r_k, c_k), dtype))

    def total(term_refs):
        acc = term_refs[0][...].astype(F32)
        for t_ref in term_refs[1:]:
            acc = acc + t_ref[...].astype(F32)
        return acc

    def body(s_ref, *refs):
        ins, outs = refs[: len(operands)], refs[len(operands) :]
        outs[0][...] = total(ins[:n_main]).astype(main[3])

        def ride():
            at = n_main
            for (groups, _, _, dtype), o_ref in zip(riders, outs[1:]):
                for g, group in enumerate(groups):
                    acc = total(ins[at : at + len(group)]).astype(dtype)
                    at += len(group)
                    if len(groups) == 1:
                        o_ref[...] = acc
                    else:
                        o_ref[g] = acc

        if riders:
            pl.when(jnp.logical_and(pl.program_id(0) == 0, pl.program_id(1) == 0))(ride)

    return _pcall(
        body,
        name=name,
        grid_spec=pltpu.PrefetchScalarGridSpec(
            num_scalar_prefetch=1, grid=(len(main[0]), rows // tr), in_specs=in_specs, out_specs=out_specs
        ),
        out_shape=out_shape,
        compiler_params=_cparams(("arbitrary", "arbitrary")),
    )(table, *operands)


def _own_half_job(buf, got, wire):
    groups = [[(_halves(buf), (j, AT_CORE)), (got, (j,))] for j in (AT_FIRST, AT_DIAGONAL)]
    return groups, (2,), [(AT_ZERO,), (AT_ONE,)], wire


def _for_neighbour_job(buf, got, got1, wire):
    return [[(_halves(buf), (AT_SECOND, AT_CORE)), (got, (AT_SECOND,)), (got1, (AT_ONE,))]], (), [()], wire


def _received_job(buf, got, got1, got2, slot, n_slots):
    terms = [(_halves(buf), (AT_MINE, AT_CORE)), (got, (AT_MINE,)), (got1, (AT_ZERO,)), (got2, ())]
    return [terms], (n_slots,), [(slot,)], F32


def _layer_a_fwd(x, nw, win, ln_w, ln_b, wc, bs_t, wout, tm, hook):
    t_rows, d = x.shape
    n_sh, _, s_cols = win.shape
    aw = wout.shape[0]
    gd = aw // A_GROUPS
    tn = 512
    assert s_cols % tn == 0 and aw % tn == 0 and tm % CHUNK == 0

    def body(x_ref, nw_ref, win_ref, lnw_ref, lnb_ref, wc_ref, bst_ref, wout_ref, z_ref, x1_ref, h_ref, u_s, v_s, y_s):
        x = x_ref[...]
        h, _, _ = _rms_fwd(x, nw_ref[...])
        h = h.astype(BF16)
        h_ref[...] = h
        for j in range(3 * aw // tn):
            k, off = divmod(j * tn, s_cols)
            cols = slice((j * tn) % aw, (j * tn) % aw + tn)
            zj = _dot(h, win_ref[k, :, off : off + tn])
            z_ref[:, j * tn : (j + 1) * tn] = zj
            if j * tn < aw:
                u_s[:, cols] = _gelu(zj)
            elif j * tn < 2 * aw:
                v_s[:, cols] = _gelu(zj)
            else:
                u_s[:, cols] = u_s[:, cols] * (zj * _sigmoid(zj))
        v = v_s[...]
        mu = jnp.mean(v, axis=-1, keepdims=True)
        vc = v - mu
        rstd = lax.rsqrt(jnp.mean(vc * vc, axis=-1, keepdims=True) + LN_EPS)
        v_s[...] = (vc * rstd) * lnw_ref[...] + lnb_ref[...]
        for ck in range(tm // CHUNK):
            rows = slice(ck * CHUNK, (ck + 1) * CHUNK)
            for g in range(A_GROUPS):
                cols = slice(g * gd, (g + 1) * gd)
                s = _dot(wc_ref[g], v_s[rows, cols].astype(BF16)) + bst_ref[:, g : g + 1]
                y_s[rows, cols] = (u_s[rows, cols] * s).astype(BF16)
        x1_ref[...] = x + _dot(y_s[...], wout_ref[...])

    row = lambda i: (i, 0)
    return _pcall(
        body,
        hook,
        name="layer_a_fwd",
        grid=(t_rows // tm,),
        in_specs=[
            pl.BlockSpec((tm, d), row),
            _full(nw.shape),
            _full(win.shape),
            _full(ln_w.shape),
            _full(ln_b.shape),
            _full(wc.shape),
            _full(bs_t.shape),
            _full(wout.shape),
        ],
        out_specs=[pl.BlockSpec((tm, 3 * aw), row), pl.BlockSpec((tm, d), row), pl.BlockSpec((tm, d), row)],
        out_shape=[
            jax.ShapeDtypeStruct((t_rows, 3 * aw), F32),
            jax.ShapeDtypeStruct((t_rows, d), F32),
            jax.ShapeDtypeStruct((t_rows, d), BF16),
        ],
        scratch_shapes=[pltpu.VMEM((tm, aw), F32), pltpu.VMEM((tm, aw), F32), pltpu.VMEM((tm, aw), BF16)],
        compiler_params=_cparams(("arbitrary",)),
    )(x, nw, win, ln_w, ln_b, wc, bs_t, wout)


def _layer_a_bwd(dout, z, ln_w, ln_b, wc, wct, bs_t, wout, tiles, earlier, hook):
    t_rows, d = dout.shape
    aw = wout.shape[0]
    gd = aw // A_GROUPS
    tm = TM_A_BWD
    lo, hi = tiles
    n_earlier = 0 if earlier is None else len(earlier)

    def body(dout_ref, z_ref, lnw_ref, lnb_ref, wc_ref, wct_ref, bst_ref, wout_ref, *rest):
        dz_ref, y_ref, dob_ref, gws_ref, gbs_ref, glnw_ref, glnb_ref, u_s, vh_s, ds_s, dvn_s = rest[n_earlier:]

        @pl.when(pl.program_id(0) == 0)
        def _():
            gws_ref[...] = jnp.zeros_like(gws_ref)
            gbs_ref[...] = jnp.zeros_like(gbs_ref)
            glnw_ref[...] = jnp.zeros_like(glnw_ref)
            glnb_ref[...] = jnp.zeros_like(glnb_ref)

        dob = dout_ref[...].astype(BF16)
        dob_ref[...] = dob
        dy = _dot_nt(dob, wout_ref[...])

        zv = z_ref[:, aw : 2 * aw]
        vg, dvg_dz = _gelu_and_grad(zv)
        mu = jnp.mean(vg, axis=-1, keepdims=True)
        vc = vg - mu
        rstd = lax.rsqrt(jnp.mean(vc * vc, axis=-1, keepdims=True) + LN_EPS)
        vh = vc * rstd
        vh_s[...] = vh
        vn = (vh * lnw_ref[...] + lnb_ref[...]).astype(BF16)

        zu = z_ref[:, 0:aw]
        zg = z_ref[:, 2 * aw : 3 * aw]
        u, du_dz = _gelu_and_grad(zu)
        sg, dsg = _silu_and_grad(zg)
        u_s[...] = u * sg
        tril = lax.broadcasted_iota(jnp.int32, (CHUNK, CHUNK), 0) >= lax.broadcasted_iota(jnp.int32, (CHUNK, CHUNK), 1)
        for ck in range(tm // CHUNK):
            rows = slice(ck * CHUNK, (ck + 1) * CHUNK)
            for g in range(A_GROUPS):
                cols = slice(g * gd, (g + 1) * gd)
                vn_g = vn[rows, cols]
                s = _dot(wc_ref[g], vn_g) + bst_ref[:, g : g + 1]
                usg = u_s[rows, cols]
                dy_g = dy[rows, cols]
                y_ref[rows, cols] = (usg * s).astype(BF16)
                ds = dy_g * usg
                ds_s[rows, cols] = dy_g * s
                gbs_ref[:, g : g + 1] += jnp.sum(ds, axis=-1, keepdims=True)
                dsb = ds.astype(BF16)
                gws_ref[g] += jnp.where(tril, _dot_nt(dsb, vn_g), 0.0)
                dvn_s[rows, cols] = _dot(wct_ref[g], dsb)
        dusg = ds_s[...]
        dz_ref[:, 0:aw] = (dusg * sg * du_dz).astype(BF16)
        dz_ref[:, 2 * aw : 3 * aw] = (dusg * u * dsg).astype(BF16)

        dvn = dvn_s[...]
        vh = vh_s[...]
        glnw_ref[...] += jnp.sum(dvn * vh, axis=0, keepdims=True)
        glnb_ref[...] += jnp.sum(dvn, axis=0, keepdims=True)
        dvh = dvn * lnw_ref[...]
        dvg = rstd * (dvh - jnp.mean(dvh, axis=-1, keepdims=True) - vh * jnp.mean(dvh * vh, axis=-1, keepdims=True))
        dz_ref[:, aw : 2 * aw] = (dvg * dvg_dz).astype(BF16)

    row = lambda i: (i + lo, 0)
    call = _pcall(
        body,
        hook,
        name=f"layer_a_bwd_{lo}",
        grid=(hi - lo,),
        in_specs=[
            pl.BlockSpec((tm, d), row),
            pl.BlockSpec((tm, 3 * aw), row),
            _full(ln_w.shape),
            _full(ln_b.shape),
            _full(wc.shape),
            _full(wct.shape),
            _full(bs_t.shape),
            _full(wout.shape),
        ]
        + [ANY] * n_earlier,
        out_specs=[
            pl.BlockSpec((tm, 3 * aw), row),
            pl.BlockSpec((tm, aw), row),
            pl.BlockSpec((tm, d), row),
            _full((A_GROUPS, CHUNK, CHUNK)),
            _full((CHUNK, A_GROUPS)),
            _full((1, aw)),
            _full((1, aw)),
        ],
        out_shape=[
            jax.ShapeDtypeStruct((t_rows, 3 * aw), BF16),
            jax.ShapeDtypeStruct((t_rows, aw), BF16),
            jax.ShapeDtypeStruct((t_rows, d), BF16),
            jax.ShapeDtypeStruct((A_GROUPS, CHUNK, CHUNK), F32),
            jax.ShapeDtypeStruct((CHUNK, A_GROUPS), F32),
            jax.ShapeDtypeStruct((1, aw), F32),
            jax.ShapeDtypeStruct((1, aw), F32),
        ],
        scratch_shapes=[pltpu.VMEM((tm, aw), F32)] * 4,
        input_output_aliases={8 + i: i for i in range(n_earlier)},
        compiler_params=_cparams(("arbitrary",)),
    )
    return call(dout, z, ln_w, ln_b, wc, wct, bs_t, wout, *(earlier or ()))


def _layer_a_bwd_dx(dout, x, dz, nw, win, tm, tiles, earlier, hook):
    t_rows, d = x.shape
    n_sh, _, s_cols = win.shape
    lo, hi = tiles
    n_earlier = 0 if earlier is None else 1

    def body(dout_ref, x_ref, dz_ref, nw_ref, win_ref, *rest):
        gx_ref, gnw_ref = rest[n_earlier:]

        @pl.when(pl.program_id(0) == 0)
        def _():
            gnw_ref[...] = jnp.zeros_like(gnw_ref)

        dh = jnp.zeros((tm, d), F32)
        for k in range(n_sh):
            dh = dh + _dot_nt(dz_ref[:, k * s_cols : (k + 1) * s_cols], win_ref[k])
        nw = nw_ref[...]
        _, xh, r = _rms_fwd(x_ref[...], nw)
        dx, gnw = _rms_bwd(dh, xh, r, nw)
        gnw_ref[0:1, :] += gnw
        gx_ref[...] = dout_ref[...] + dx

    row = lambda i: (i + lo, 0)
    return _pcall(
        body,
        hook,
        name=f"layer_a_bwd_dx_{lo}",
        grid=(hi - lo,),
        in_specs=[
            pl.BlockSpec((tm, d), row),
            pl.BlockSpec((tm, d), row),
            pl.BlockSpec((tm, n_sh * s_cols), row),
            _full(nw.shape),
            _full(win.shape),
        ]
        + [ANY] * n_earlier,
        out_specs=[pl.BlockSpec((tm, d), row), _full((SUBLANES, d))],
        out_shape=[jax.ShapeDtypeStruct((t_rows, d), F32), jax.ShapeDtypeStruct((SUBLANES, d), F32)],
        input_output_aliases={5: 0} if n_earlier else {},
        compiler_params=_cparams(("arbitrary",)),
    )(dout, x, dz, nw, win, *([earlier] if n_earlier else []))


def _decay(r, sp_h):
    log_a = (-RG_C) * r * sp_h
    a = jnp.exp(log_a)
    mult = jnp.sqrt(jnp.tanh(-log_a) * (a * a + 1.0))
    return a, mult


ROW_CONV_B, ROW_GATE_A_B, ROW_GATE_X_B, ROW_LAMBDA = range(CONV_WIDTH, CONV_WIDTH + 4)


def _gates(xc_h, gab_ref, vec_ref, sp_h, h, hd):
    pre = _dot(xc_h.astype(BF16), gab_ref[h])
    cols = slice(h * hd, (h + 1) * hd)
    r = _sigmoid(pre[:, :hd] + vec_ref[ROW_GATE_A_B : ROW_GATE_A_B + 1, cols])
    ig = _sigmoid(pre[:, hd:] + vec_ref[ROW_GATE_X_B : ROW_GATE_X_B + 1, cols])
    a, mult = _decay(r, sp_h)
    return r, ig, a, mult


def _conv(xb, halo, vec_ref):
    xc = vec_ref[ROW_CONV_B : ROW_CONV_B + 1, :] + vec_ref[CONV_WIDTH - 1 : CONV_WIDTH, :] * xb
    for k in range(CONV_WIDTH - 1):
        xc = xc + vec_ref[k : k + 1, :] * _shift_down(xb, halo, CONV_WIDTH - 1 - k)
    return xc


def _layer_b_fwd(x1, nw, bin_w, vec, gab, bout, nf, tgt, tm):
    t_rows, d = x1.shape
    bw = bout.shape[0]
    hd = bw // B_HEADS
    nt = t_rows // tm

    def body(
        x1_ref, nw_ref, bin_ref, vec_ref, gab_ref, bout_ref, nf_ref, tgt_ref,
        z_ref, h_ref, h1_ref, dx2_ref, loss_ref, gnf_ref,
        tail_s, carry_s, a_s, b_s, hs_s, acc_s,
    ):
        @pl.when(pl.program_id(0) == 0)
        def _():
            tail_s[...] = jnp.zeros_like(tail_s)
            carry_s[...] = jnp.zeros_like(carry_s)
            acc_s[...] = jnp.zeros_like(acc_s)
            gnf_ref[...] = jnp.zeros_like(gnf_ref)

        x1 = x1_ref[...]
        h1, _, _ = _rms_fwd(x1, nw_ref[...])
        h1 = h1.astype(BF16)
        h1_ref[...] = h1
        z = jnp.concatenate([_dot(h1, bin_ref[k]) for k in range(N_CHIPS)], axis=1)
        z_ref[...] = z
        xb = z[:, :bw]
        xc = _conv(xb, tail_s[...], vec_ref)
        tail = xb[tm - SUBLANES :, :]
        tail_s[...] = tail
        sp = _softplus_neg(vec_ref[ROW_LAMBDA : ROW_LAMBDA + 1, :])
        for h in range(B_HEADS):
            cols = slice(h * hd, (h + 1) * hd)
            xc_h = xc[:, cols]
            _, ig, a, mult = _gates(xc_h, gab_ref, vec_ref, sp[:, cols], h, hd)
            a_s[:, cols] = a
            b_s[:, cols] = mult * (ig * xc_h)
        carry = _scan_blocks(a_s, b_s, hs_s, carry_s[...], tm, reverse=False)
        carry_s[...] = carry
        hs = hs_s[...]
        h_ref[...] = hs
        g = z[:, bw:]
        y = (hs * (g * _sigmoid(g))).astype(BF16)
        x2 = x1 + _dot(y, bout_ref[...])

        nf = nf_ref[...]
        o, xh, r = _rms_fwd(x2, nf)
        diff = o - tgt_ref[...]
        acc_s[...] += jnp.sum(diff * diff, axis=0, keepdims=True)
        do = diff * (1.0 / d)
        dx2, gnf = _rms_bwd(do, xh, r, nf)
        gnf_ref[...] += gnf
        dx2_ref[...] = dx2

        @pl.when(pl.program_id(0) == nt - 1)
        def _():
            total = jnp.sum(acc_s[...], axis=-1, keepdims=True) * (0.5 / d)
            loss_ref[...] = jnp.broadcast_to(total, loss_ref.shape)

    row = lambda i: (i, 0)
    return _pcall(
        body,
        name="layer_b_fwd",
        grid=(nt,),
        in_specs=[
            pl.BlockSpec((tm, d), row),
            _full(nw.shape),
            _full(bin_w.shape),
            _full(vec.shape),
            _full(gab.shape),
            _full(bout.shape),
            _full(nf.shape),
            pl.BlockSpec((tm, d), row),
        ],
        out_specs=[
            pl.BlockSpec((tm, 2 * bw), row),
            pl.BlockSpec((tm, bw), row),
            pl.BlockSpec((tm, d), row),
            pl.BlockSpec((tm, d), row),
            _full((1, LANES)),
            _full((1, d)),
        ],
        out_shape=[
            jax.ShapeDtypeStruct((t_rows, 2 * bw), F32),
            jax.ShapeDtypeStruct((t_rows, bw), F32),
            jax.ShapeDtypeStruct((t_rows, d), BF16),
            jax.ShapeDtypeStruct((t_rows, d), F32),
            jax.ShapeDtypeStruct((1, LANES), F32),
            jax.ShapeDtypeStruct((1, d), F32),
        ],
        scratch_shapes=[
            pltpu.VMEM((SUBLANES, bw), F32),
            pltpu.VMEM((SUBLANES, bw), F32),
            pltpu.VMEM((tm, bw), F32),
            pltpu.VMEM((tm, bw), F32),
            pltpu.VMEM((tm, bw), F32),
            pltpu.VMEM((1, d), F32),
        ],
        compiler_params=_cparams(("arbitrary",)),
    )(x1, nw, bin_w, vec, gab, bout, nf, tgt)


def _layer_b_bwd(dout, x1, z, hseq, nw, bin_w, vec, gab, gabt, bout, tm):
    t_rows, d = x1.shape
    bw = bout.shape[0]
    hd = bw // B_HEADS
    nt = t_rows // tm

    def body(
        dout_ref, x1_ref, z_ref, h_ref, xbt_ref, ht_ref, nw_ref, bin_ref, vec_ref, gab_ref, gabt_ref, bout_ref,
        dx1_ref, dz_ref, y_ref, dob_ref, ggab_ref, ggb_ref, gcw_ref, gcb_ref, glam_ref, gnw_ref,
        gcarry_s, afirst_s, head_s, aup_s, dh_s, gt_s, dxc_s, xc_s, r_s, ig_s,
    ):
        step = pl.program_id(0)
        tile = nt - 1 - step

        @pl.when(step == 0)
        def _():
            for ref in (ggab_ref, ggb_ref, gcw_ref, gcb_ref, glam_ref, gnw_ref, gcarry_s, afirst_s, head_s):
                ref[...] = jnp.zeros_like(ref)

        first_tile = tile == 0
        xb_halo = jnp.where(first_tile, 0.0, xbt_ref[...])
        h_halo = jnp.where(first_tile, 0.0, ht_ref[...])

        dout = dout_ref[...]
        dob = dout.astype(BF16)
        dob_ref[...] = dob
        dy = _dot_nt(dob, bout_ref[...])
        hs = h_ref[...]
        g = z_ref[:, bw:]
        sg, dsg = _silu_and_grad(g)
        y_ref[...] = (hs * sg).astype(BF16)
        dz_ref[:, bw:] = (dy * hs * dsg).astype(BF16)
        dh_s[...] = dy * sg

        xb = z_ref[:, :bw]
        xc = _conv(xb, xb_halo, vec_ref)
        xc_s[...] = xc
        lam = vec_ref[ROW_LAMBDA : ROW_LAMBDA + 1, :]
        sp = _softplus_neg(lam)
        for h in range(B_HEADS):
            cols = slice(h * hd, (h + 1) * hd)
            r, ig, a, _ = _gates(xc[:, cols], gab_ref, vec_ref, sp[:, cols], h, hd)
            r_s[:, cols] = r
            ig_s[:, cols] = ig
            aup_s[:, cols] = _shift_up(a, afirst_s[:, cols], 1)
            afirst_s[:, cols] = jnp.broadcast_to(a[0:1, :], (SUBLANES, hd))
        carry = _scan_blocks(aup_s, dh_s, gt_s, gcarry_s[...], tm, reverse=True)
        gcarry_s[...] = carry

        h_prev = _shift_down(hs, h_halo, 1)
        for h in range(B_HEADS):
            cols = slice(h * hd, (h + 1) * hd)
            xc_h = xc_s[:, cols]
            sp_h = sp[:, cols]
            r, ig = r_s[:, cols], ig_s[:, cols]
            a, mult = _decay(r, sp_h)
            gt = gt_s[:, cols]
            da = gt * h_prev[:, cols]
            dmult = gt * (ig * xc_h)
            dig = gt * (mult * xc_h)
            dxc_direct = gt * (mult * ig)
            dla = da * a - dmult * (a * a) / mult
            glam_ref[:, cols] += jnp.sum(dla * r, axis=0, keepdims=True)
            dr = dla * ((-RG_C) * sp_h)
            dpre = jnp.concatenate([dr * r * (1.0 - r), dig * ig * (1.0 - ig)], axis=1)
            ggb_ref[:, cols] += jnp.sum(dpre[:, :hd], axis=0, keepdims=True)
            ggb_ref[:, bw + h * hd : bw + (h + 1) * hd] += jnp.sum(dpre[:, hd:], axis=0, keepdims=True)
            dpb = dpre.astype(BF16)
            ggab_ref[h] += _dot_tn(xc_h.astype(BF16), dpb)
            dxc_s[:, cols] = dxc_direct + _dot(dpb, gabt_ref[h])
        glam_ref[...] = jnp.where(step == nt - 1, glam_ref[...] * (RG_C * _sigmoid(-lam)), glam_ref[...])

        dxc = dxc_s[...]
        gcb_ref[...] += jnp.sum(dxc, axis=0, keepdims=True)
        dxb = vec_ref[CONV_WIDTH - 1 : CONV_WIDTH, :] * dxc
        gcw_ref[CONV_WIDTH - 1 : CONV_WIDTH, :] += jnp.sum(dxc * xb, axis=0, keepdims=True)
        head = head_s[...]
        for k in range(CONV_WIDTH - 1):
            lag = CONV_WIDTH - 1 - k
            dxb = dxb + vec_ref[k : k + 1, :] * _shift_up(dxc, head, lag)
            gcw_ref[k : k + 1, :] += jnp.sum(dxc * _shift_down(xb, xb_halo, lag), axis=0, keepdims=True)
        head_s[...] = dxc[:SUBLANES, :]
        dz_ref[:, :bw] = dxb.astype(BF16)

        s_cols = 2 * bw // N_CHIPS
        dh1 = jnp.zeros((tm, d), F32)
        for k in range(N_CHIPS):
            dh1 = dh1 + _dot_nt(dz_ref[:, k * s_cols : (k + 1) * s_cols], bin_ref[k])
        x1 = x1_ref[...]
        nw = nw_ref[...]
        _, xh, r1 = _rms_fwd(x1, nw)
        dx, gnw = _rms_bwd(dh1, xh, r1, nw)
        gnw_ref[...] += gnw
        dx1_ref[...] = dout + dx

    rev = lambda i: (nt - 1 - i, 0)
    prev = lambda i: (jnp.maximum((nt - 1 - i) * (tm // SUBLANES) - 1, 0), 0)
    return _pcall(
        body,
        name="layer_b_bwd",
        grid=(nt,),
        in_specs=[
            pl.BlockSpec((tm, d), rev),
            pl.BlockSpec((tm, d), rev),
            pl.BlockSpec((tm, 2 * bw), rev),
            pl.BlockSpec((tm, bw), rev),
            pl.BlockSpec((SUBLANES, bw), prev),
            pl.BlockSpec((SUBLANES, bw), prev),
            _full(nw.shape),
            _full(bin_w.shape),
            _full(vec.shape),
            _full(gab.shape),
            _full(gabt.shape),
            _full(bout.shape),
        ],
        out_specs=[
            pl.BlockSpec((tm, d), rev),
            pl.BlockSpec((tm, 2 * bw), rev),
            pl.BlockSpec((tm, bw), rev),
            pl.BlockSpec((tm, d), rev),
            _full((B_HEADS, hd, 2 * hd)),
            _full((1, 2 * bw)),
            _full((SUBLANES, bw)),
            _full((1, bw)),
            _full((1, bw)),
            _full((1, d)),
        ],
        out_shape=[
            jax.ShapeDtypeStruct((t_rows, d), F32),
            jax.ShapeDtypeStruct((t_rows, 2 * bw), BF16),
            jax.ShapeDtypeStruct((t_rows, bw), BF16),
            jax.ShapeDtypeStruct((t_rows, d), BF16),
            jax.ShapeDtypeStruct((B_HEADS, hd, 2 * hd), F32),
            jax.ShapeDtypeStruct((1, 2 * bw), F32),
            jax.ShapeDtypeStruct((SUBLANES, bw), F32),
            jax.ShapeDtypeStruct((1, bw), F32),
            jax.ShapeDtypeStruct((1, bw), F32),
            jax.ShapeDtypeStruct((1, d), F32),
        ],
        scratch_shapes=[pltpu.VMEM((SUBLANES, bw), F32)] * 3 + [pltpu.VMEM((tm, bw), F32)] * 7,
        compiler_params=_cparams(("arbitrary",)),
    )(dout, x1, z, hseq, z, hseq, nw, bin_w, vec, gab, gabt, bout)


def _wgrad(a, b, m_blocks, n_blocks, hook=None, wire_copy=False):
    k, m = a.shape
    n = b.shape[1]
    bm, bn = m // m_blocks, n // n_blocks

    def body(a_ref, b_ref, o_ref, *wire_ref):
        prod = _dot_tn(a_ref[...], b_ref[...])
        o_ref[...] = prod
        if wire_copy:
            wire_ref[0][...] = prod.astype(BF16)

    out_spec = pl.BlockSpec((None, None, bm, bn), lambda j, i: (j, i, 0, 0))
    shape = (n_blocks, m_blocks, bm, bn)

    def streamed(*refs):
        pltpu.emit_pipeline(
            body,
            grid=(n_blocks, m_blocks),
            in_specs=[
                pl.BlockSpec((k, bm), lambda j, i: (0, i)),
                pl.BlockSpec((k, bn), lambda j, i: (0, j), pipeline_mode=pl.Buffered(3)),
            ],
            out_specs=[out_spec] * (1 + wire_copy),
        )(*refs)

    out = _pcall(
        streamed,
        hook,
        name=f"wgrad_{m}x{n}",
        in_specs=[ANY, ANY],
        out_specs=[ANY] * (1 + wire_copy),
        out_shape=[jax.ShapeDtypeStruct(shape, F32)] + [jax.ShapeDtypeStruct(shape, BF16)] * wire_copy,
        compiler_params=_cparams(),
    )(a, b)
    outs, rode = (out, None) if hook is None else out
    outs = outs if wire_copy else outs[0]
    return outs if hook is None else (outs, rode)


def _adamw_math(w, g, m, v):
    m = ADAM_B1 * m + (1.0 - ADAM_B1) * g
    v = ADAM_B2 * v + (1.0 - ADAM_B2) * (g * g)
    m_hat = m / (1.0 - ADAM_B1**ADAM_STEP)
    v_hat = v / (1.0 - ADAM_B2**ADAM_STEP)
    delta = -ADAM_LR * (m_hat / (jnp.sqrt(v_hat) + ADAM_EPS) + ADAM_WD * w)
    return delta, m, v


ADAMW_ROW_TILES = 8


def _adamw_rows(ws, gs, ms, vs):
    n = len(ws)

    def body(*refs):
        ins, outs = refs[: 4 * n], refs[4 * n :]
        for i in range(n):
            w_ref, g_ref, m_ref, v_ref = ins[i::n]
            d_ref, mo_ref, vo_ref, go_ref = outs[i::n]
            g = g_ref[...]
            d_ref[...], mo_ref[...], vo_ref[...] = _adamw_math(w_ref[...], g, m_ref[...], v_ref[...])
            go_ref[...] = g

    specs = [pl.BlockSpec((w.shape[0] // ADAMW_ROW_TILES, w.shape[1]), lambda i: (i, 0)) for w in ws]
    outs = _pcall(
        body,
        name="adamw_rows",
        grid=(ADAMW_ROW_TILES,),
        in_specs=specs * 4,
        out_specs=specs * 4,
        out_shape=[jax.ShapeDtypeStruct(w.shape, F32) for w in ws] * 4,
        compiler_params=_cparams(("arbitrary",)),
    )(*ws, *gs, *ms, *vs)
    return [outs[k * n : (k + 1) * n] for k in range(4)]


def _adamw_many(ws, gs, ms, vs, pack):
    n = len(ws)
    packed = [i for i in range(n) if isinstance(gs[i], (int, tuple))]
    g_arrays = [g[0] if isinstance(g, tuple) else g for g in gs if not isinstance(g, int)]

    def body(*refs):
        w_refs, m_refs, v_refs = (refs[i * n : (i + 1) * n] for i in range(3))
        g_refs = list(refs[3 * n : 3 * n + len(g_arrays)])
        pack_ref = refs[3 * n + len(g_arrays)]
        outs = refs[3 * n + len(g_arrays) + 1 :]
        d_refs, mo_refs, vo_refs, go_refs = outs[:n], outs[n : 2 * n], outs[2 * n : 3 * n], list(outs[3 * n :])
        for i in range(n):
            if isinstance(gs[i], int):
                g = pack_ref[gs[i] : gs[i] + ws[i].shape[0], :]
                go_refs.pop(0)[...] = g
            elif isinstance(gs[i], tuple):
                parts_ref, row, go_ref = g_refs.pop(0), gs[i][1], go_refs.pop(0)
                total = parts_ref[0, 0:1, :]
                for k in range(1, N_DEV):
                    total = total + parts_ref[k, 0:1, :]
                go_ref[0:1, :] = total
                lane_rows = [pack_ref[row + r : row + r + 1, :] for r in range(ws[i].shape[1] // LANES)]
                go_ref[1:2, :] = jnp.concatenate(lane_rows, axis=1)
                g = go_ref[...]
            else:
                g = g_refs.pop(0)[...]
            d_refs[i][...], mo_refs[i][...], vo_refs[i][...] = _adamw_math(w_refs[i][...], g, m_refs[i][...], v_refs[i][...])

    vmem = pl.BlockSpec(memory_space=pltpu.VMEM)
    like = [jax.ShapeDtypeStruct(w.shape, F32) for w in ws]
    outs = _pcall(
        body,
        name="adamw_small",
        in_specs=[vmem] * (3 * n + len(g_arrays) + 1),
        out_specs=[vmem] * (3 * n + len(packed)),
        out_shape=like * 3 + [like[i] for i in packed],
        compiler_params=_cparams(),
    )(*ws, *ms, *vs, *g_arrays, pack)
    return outs[:n], outs[n : 2 * n], outs[2 * n : 3 * n], outs[3 * n :]


def _pack_rows(parts, lanes=LANES):
    flat = jnp.concatenate([p.reshape(-1) for p in parts])
    per = N_DEV * SUBLANES * lanes
    total = -(-flat.shape[0] // per) * per
    flat = jnp.pad(flat, (0, total - flat.shape[0]))
    return flat.reshape(N_DEV, total // (N_DEV * lanes), lanes)


def _size(shape):
    n = 1
    for dim in shape:
        n *= dim
    return n


def _unpack(flat, shapes):
    out, at = [], 0
    for s in shapes:
        out.append(flat[at : at + _size(s)].reshape(s))
        at += _size(s)
    return out


def kernel(x, norm_w, a_w_in, a_ln_w, a_ln_b, a_w_s, a_b_s, a_w_out, b_w_in, b_conv_w, b_conv_b, b_gate_a_w, b_gate_a_b, b_gate_x_w, b_gate_x_b, b_lambda, b_w_out, norm_f_w, loss_target, m_norm_w, m_a_w_in, m_a_ln_w, m_a_ln_b, m_a_w_s, m_a_b_s, m_a_w_out, m_b_w_in, m_b_conv_w, m_b_conv_b, m_b_gate_a_w, m_b_gate_a_b, m_b_gate_x_w, m_b_gate_x_b, m_b_lambda, m_b_w_out, m_norm_f_w, v_norm_w, v_a_w_in, v_a_ln_w, v_a_ln_b, v_a_w_s, v_a_b_s, v_a_w_out, v_b_w_in, v_b_conv_w, v_b_conv_b, v_b_gate_a_w, v_b_gate_a_b, v_b_gate_x_w, v_b_gate_x_b, v_b_lambda, v_b_w_out, v_norm_f_w):
    t_rows, d = x.shape[1], x.shape[2]
    aw = a_ln_w.shape[1]
    bw = b_gate_a_w.shape[1] * b_gate_a_w.shape[2]
    hd = bw // B_HEADS
    mine = 2 * lax.axis_index("x") + lax.axis_index("y")
    core = lax.axis_index("c")
    weights = dict(norm_w=norm_w, a_w_in=a_w_in, a_ln_w=a_ln_w, a_ln_b=a_ln_b, a_w_s=a_w_s, a_b_s=a_b_s, a_w_out=a_w_out, b_w_in=b_w_in, b_conv_w=b_conv_w, b_conv_b=b_conv_b, b_gate_a_w=b_gate_a_w, b_gate_a_b=b_gate_a_b, b_gate_x_w=b_gate_x_w, b_gate_x_b=b_gate_x_b, b_lambda=b_lambda, b_w_out=b_w_out, norm_f_w=norm_f_w)
    m_in = dict(norm_w=m_norm_w, a_w_in=m_a_w_in, a_ln_w=m_a_ln_w, a_ln_b=m_a_ln_b, a_w_s=m_a_w_s, a_b_s=m_a_b_s, a_w_out=m_a_w_out, b_w_in=m_b_w_in, b_conv_w=m_b_conv_w, b_conv_b=m_b_conv_b, b_gate_a_w=m_b_gate_a_w, b_gate_a_b=m_b_gate_a_b, b_gate_x_w=m_b_gate_x_w, b_gate_x_b=m_b_gate_x_b, b_lambda=m_b_lambda, b_w_out=m_b_w_out, norm_f_w=m_norm_f_w)
    v_in = dict(norm_w=v_norm_w, a_w_in=v_a_w_in, a_ln_w=v_a_ln_w, a_ln_b=v_a_ln_b, a_w_s=v_a_w_s, a_b_s=v_a_b_s, a_w_out=v_a_w_out, b_w_in=v_b_w_in, b_conv_w=v_b_conv_w, b_conv_b=v_b_conv_b, b_gate_a_w=v_b_gate_a_w, b_gate_a_b=v_b_gate_a_b, b_gate_x_w=v_b_gate_x_w, b_gate_x_b=v_b_gate_x_b, b_lambda=v_b_lambda, b_w_out=v_b_w_out, norm_f_w=v_norm_f_w)

    table = _index_table()
    win_l, wout_l = _cast_to_segments([a_w_in[0], a_w_out[0]], table, 4)
    small_l = jnp.concatenate([b_conv_w[0], b_conv_b, b_gate_a_b, b_gate_x_b, b_lambda], axis=0)
    (bin_l, bout_l, wc, wct, gab, gabt), (win_g, wout_g, small_g) = _cast_to_segments(
        [b_w_in[0], b_w_out[0]], table, 8, _gather_hook([win_l, wout_l], small_l),
        (a_w_s[0], b_gate_a_w[0], b_gate_x_w[0]),
    )
    win = win_g.reshape(N_CHIPS, d, -1)
    wout = wout_g.reshape(aw, d)
    bs_t = a_b_s[0].T
    nw0, nw1, nf = norm_w[0:1], norm_w[1:2], norm_f_w.reshape(1, d)

    x0 = x[0]
    (z_a, x1, h0), (bin_g, bout_g) = _layer_a_fwd(
        x0, nw0, win, a_ln_w, a_ln_b, wc, bs_t, wout, TM_FWD, _gather_hook([bin_l, bout_l])
    )
    bin_w = bin_g.reshape(N_CHIPS, d, -1)
    bout = bout_g.reshape(bw, d)
    vec = jnp.transpose(small_g, (1, 0, 2)).reshape(SUBLANES, bw)
    z_b, hseq, h1, dx2, loss_l, g_nf = _layer_b_fwd(x1, nw1, bin_w, vec, gab, bout, nf, loss_target[0], TM_FWD)
    dx1, dz_b, y_b, dob_b, g_gab, g_gb, g_cw, g_cb, g_lam, g_nw1 = _layer_b_bwd(
        dx2, x1, z_b, hseq, nw1, bin_w, vec, gab, gabt, bout, TM_FWD
    )
    seg = lambda g: g.reshape(N_DEV, -1, g.shape[3])
    sums = lambda jobs, name: _sum_jobs(jobs, name, table)
    own_half, for_neighbour = _own_half_job, _for_neighbour_job
    received = lambda buf, got, got1, got2: _received_job(buf, got, got1, got2, AT_CORE, 2)

    g_o = seg(_wgrad(y_b, dob_b, 2, 1))
    g_i, (swap_o,) = _wgrad(h1, dz_b, 1, N_CHIPS, _swap_hook([g_o]))
    g_i = seg(g_i)
    (part_o,) = sums([own_half(g_o, swap_o, BF16)], "add_own_half_o")
    a_args = (z_a, a_ln_w, a_ln_b, wc, wct, bs_t, wout)
    half = t_rows // TM_A_BWD // 2
    first, (swap_i, got1_o) = _layer_a_bwd(
        dx1, *a_args, (0, half), None, _join_hooks(_swap_hook([g_i]), _send_first_hook([part_o]))
    )
    part_i, mid_o = sums(
        [own_half(g_i, swap_i, BF16), for_neighbour(g_o, swap_o, got1_o, BF16)], "add_own_half_i_for_neighbour_o"
    )
    second, (got1_i, got2_o) = _layer_a_bwd(
        dx1, *a_args, (half, 2 * half), first[:3], _join_hooks(_send_first_hook([part_i]), _send_second_hook([mid_o]))
    )
    dz_a, y_a, dob_a = second[:3]
    g_ws, g_bst, g_lnw, g_lnb = (p + q for p, q in zip(first[3:], second[3:]))
    mid_i, red_o = sums(
        [for_neighbour(g_i, swap_i, got1_i, BF16), received(g_o, swap_o, got1_o, got2_o)],
        "add_for_neighbour_i_received_o",
    )
    small_shapes = [
        (1, d), (1, aw), (1, aw), (A_GROUPS, CHUNK, CHUNK), (A_GROUPS, CHUNK), (B_HEADS, hd, hd), (B_HEADS, hd, hd),
        (d,), (CONV_WIDTH, bw), (1, bw), (1, bw), (1, bw), (1, bw), (1, 1),
    ]
    small = _pack_rows(
        [
            g_nw1, g_lnw, g_lnb, g_ws, g_bst.T, g_gab[:, :, :hd], g_gab[:, :, hd:],
            g_nf, g_cw[:CONV_WIDTH], g_cb, g_gb[:, :bw], g_gb[:, bw:], g_lam, loss_l[:, :1],
        ]
    )
    (g_w, g_w_wire), (got2_i, gr_bout, swap_s) = _wgrad(
        h0, dz_a, 1, N_CHIPS, _join_hooks(_send_second_hook([mid_i]), _share_hook([red_o]), _swap_hook([small])),
        wire_copy=True,
    )
    g_w, g_w_wire = seg(g_w), seg(g_w_wire)
    red_i, part_s = sums([received(g_i, swap_i, got1_i, got2_i), own_half(small, swap_s, F32)], "add_received_i")
    g_u, (swap_w, got1_s) = _wgrad(
        y_a, dob_a, N_CHIPS, 1, _join_hooks(_swap_hook([g_w_wire]), _send_first_hook([part_s]))
    )
    g_u = seg(g_u)

    part_w, mid_s = sums(
        [own_half(g_w, swap_w, BF16), for_neighbour(small, swap_s, got1_s, F32)], "add_own_half_w"
    )
    (grad_x, g_nw0_mine), (got1_w, got2_s, swap_u, gr_bin) = _layer_a_bwd_dx(
        dx1, x0, dz_a, nw0, win, TM_A_DX, (0, t_rows // TM_A_DX), None,
        _join_hooks(_send_first_hook([part_w]), _send_second_hook([mid_s]), _swap_hook([g_u]), _share_hook([red_i])),
    )
    mid_w, red_s, part_u = sums(
        [
            for_neighbour(g_w, swap_w, got1_w, BF16),
            _received_job(small, swap_s, got1_s, got2_s, AT_DEVICE, N_DEV),
            own_half(g_u, swap_u, BF16),
        ],
        "add_for_neighbour_w_own_half_u",
    )
    got2_w, got1_u, small_r, g_nw0_all = _run_hook(
        _join_hooks(_send_second_hook([mid_w]), _send_first_hook([part_u]), _share_hook([], red_s, g_nw0_mine)),
        "second_axis_and_gather",
    )
    red_w, mid_u = sums(
        [received(g_w, swap_w, got1_w, got2_w), for_neighbour(g_u, swap_u, got1_u, BF16)],
        "add_received_w_for_neighbour_u",
    )
    got2_u, gr_win = _run_hook(
        _join_hooks(_send_second_hook([mid_u]), _share_hook([red_w])), "second_axis_and_share"
    )
    (red_u,) = sums([received(g_u, swap_u, got1_u, got2_u)], "add_received_u")
    (gr_wout,) = _run_hook(_share_hook([red_u]), "share_reduced")
    (_, g_a_ln_w, g_a_ln_b, _, g_a_b_s, _, _, g_norm_f, gf_cw, gf_cb, gf_gab, gf_gxb, gf_lam, loss) = _unpack(
        small_r.reshape(-1), small_shapes
    )
    shard = lambda g: lax.dynamic_slice_in_dim(g, mine * (bw // N_CHIPS), bw // N_CHIPS, axis=1)

    grads = {
        "a_w_in": gr_win, "a_ln_w": g_a_ln_w, "a_ln_b": g_a_ln_b,
        "a_b_s": g_a_b_s[None], "a_w_out": gr_wout, "b_w_in": gr_bin, "b_conv_w": shard(gf_cw)[None],
        "b_conv_b": shard(gf_cb), "b_gate_a_b": shard(gf_gab),
        "b_gate_x_b": shard(gf_gxb), "b_lambda": shard(gf_lam),
        "b_w_out": gr_bout, "norm_f_w": g_norm_f,
    }
    pack_names = [
        "norm_w", "a_ln_w", "a_ln_b", "a_w_s", "a_b_s", "b_gate_a_w", "b_gate_x_w", "norm_f_w", "b_conv_w", "b_conv_b",
        "b_gate_a_b", "b_gate_x_b", "b_lambda", "loss",
    ]
    row_of, at = {}, 0
    for n, shape in zip(pack_names, small_shapes):
        row_of[n] = at // LANES
        at += _size(shape)
    names = list(weights)
    big_names = ["a_w_in", "a_w_out", "b_w_in", "b_w_out"]
    small_names = [n for n in names if n not in big_names]
    from_pack = [n for n in small_names if n in ("a_w_s", "b_gate_a_w", "b_gate_x_w")]
    as_rows = lambda n, a: a.reshape(-1, LANES) if n in from_pack else a.reshape(1, -1) if a.ndim == 1 else a
    small_grads = {n: row_of[n] for n in from_pack}
    small_grads["norm_w"] = (g_nw0_all, row_of["norm_w"])
    *small_out, pack_grads = _adamw_many(
        [as_rows(n, weights[n]) for n in small_names],
        [small_grads[n] if n in small_grads else as_rows(n, grads[n]) for n in small_names],
        [as_rows(n, m_in[n]) for n in small_names],
        [as_rows(n, v_in[n]) for n in small_names],
        small_r.reshape(-1, LANES),
    )
    grads.update(zip([n for n in small_names if n in small_grads], pack_grads))
    *big_out, big_grads = _adamw_rows(
        *[[src[n].reshape(weights[n].shape[-2:]) for n in big_names] for src in (weights, grads, m_in, v_in)]
    )
    grads.update(zip(big_names, big_grads))
    delta, new_m, new_v = {}, {}, {}
    for dst, small_vals, big_vals in zip((delta, new_m, new_v), small_out, big_out):
        dst.update(zip(small_names, small_vals))
        dst.update(zip(big_names, big_vals))
    for dst in (grads, delta, new_m, new_v):
        for n in names:
            dst[n] = dst[n].reshape(weights[n].shape)

    return (
        loss.reshape(()),
        grad_x[None],
        *[grads[n] for n in names],
        *[delta[n] for n in names],
        *[new_m[n] for n in names],
        *[new_v[n] for n in names],
    )
```
